```python
import math
import jax, jax.numpy as jnp
from jax import lax
import numpy as np

D_MODEL = 2048
BATCH = 4
SEQ = 4096
DEPTH = 2

GRID_W = 64
HEAD_DIM = 64
ROPE_THETA = 10000.0
NORM_EPS = 1e-6
NEG_INF = -1e30

SSM_HEADS = 8
SSM_HEAD_DIM = 64
SSM_INNER = SSM_HEADS * SSM_HEAD_DIM
SSM_GROUPS = 2
SSM_STATE = 128
SSM_CONV = 5
SSM_CHUNK = 128
SSM_CONV_CH = SSM_INNER + 2 * SSM_GROUPS * SSM_STATE

NA_HEADS = 8
NA_WIN_ROWS = 8
NA_WIN_COLS = 16
NA_COL_BLOCK = 16
NA_KEY_COLS = 32

MLA_HEADS = 8
MLA_Q_RANK = 512
MLA_KV_RANK = 512
MLA_NOPE = 64
MLA_ROPE = 32
MLA_V = 64
MLA_QBLOCK = 128

DIL_HEADS = 8
DIL_PAIRS = ((128, 1), (512, 4), (2048, 16))
DIL_QBLOCK = 128

D_MIX = SSM_INNER + NA_HEADS * HEAD_DIM + MLA_HEADS * MLA_V + DIL_HEADS * HEAD_DIM
IN_SIZES = (SSM_INNER, SSM_CONV_CH, 2 * SSM_HEADS,
            NA_HEADS * HEAD_DIM, NA_HEADS * HEAD_DIM, NA_HEADS * HEAD_DIM,
            MLA_Q_RANK, MLA_KV_RANK, MLA_ROPE,
            DIL_HEADS * HEAD_DIM, DIL_HEADS * HEAD_DIM, DIL_HEADS * HEAD_DIM)
D_IN = sum(IN_SIZES)

D_FF = 5632
N_EXPERTS = 8
TOP_K = 2
D_FF_EXPERT = 5632
N_DENSE = (DEPTH + 1) // 2
N_MOE = DEPTH // 2

kernel_name = 'hybrid_parallel_heads_encoder'


def rms_norm(x, w):
    xf = x.astype(jnp.float32)
    y = xf * lax.rsqrt(jnp.mean(xf * xf, axis=-1, keepdims=True) + NORM_EPS)
    return (y * w.astype(jnp.float32)).astype(x.dtype)


def rope(x, pos):
    d = x.shape[-1]
    inv = ROPE_THETA ** (-jnp.arange(0, d, 2, dtype=jnp.float32) / d)
    ang = pos[:, None] * inv[None, :]
    cos = jnp.cos(ang)[:, None, :]
    sin = jnp.sin(ang)[:, None, :]
    xf = x.astype(jnp.float32)
    x1, x2 = xf[..., : d // 2], xf[..., d // 2:]
    return jnp.concatenate([x1 * cos - x2 * sin, x2 * cos + x1 * sin], axis=-1).astype(x.dtype)


def split_heads(a, n_heads):
    b, t, _ = a.shape
    return a.reshape(b, t, n_heads, -1)


def ssd_chunked(x, dt, a, bmat, cmat):
    b, t, h, p = x.shape
    n = bmat.shape[-1]
    q = SSM_CHUNK
    nc = t // q
    xs = (x.astype(jnp.float32) * dt[..., None]).reshape(b, nc, q, h, p)
    bc = bmat.astype(jnp.float32).reshape(b, nc, q, h, n)
    cc = cmat.astype(jnp.float32).reshape(b, nc, q, h, n)
    da = (dt * a).reshape(b, nc, q, h).transpose(0, 3, 1, 2)
    cs = jnp.cumsum(da, axis=-1)
    lower = np.tril(np.ones((q, q), dtype=bool))
    decay = jnp.exp(jnp.where(lower, cs[..., :, None] - cs[..., None, :], -jnp.inf))
    scores = jnp.einsum('bclhn,bcshn->bhcls', cc, bc) * decay
    y_diag = jnp.einsum('bhcls,bcshp->bclhp', scores, xs)
    decay_to_end = jnp.exp(cs[..., -1:] - cs)
    chunk_states = jnp.einsum('bclhn,bhcl,bclhp->cbhpn', bc, decay_to_end, xs)
    chunk_decay = jnp.exp(cs[..., -1]).transpose(2, 0, 1)

    def carry_state(state, inp):
        s_c, d_c = inp
        return state * d_c[..., None, None] + s_c, state

    _, prev = lax.scan(carry_state, jnp.zeros((b, h, p, n), jnp.float32), (chunk_states, chunk_decay))
    y_off = jnp.einsum('bclhn,cbhpn,bhcl->bclhp', cc, prev, jnp.exp(cs))
    return (y_diag + y_off).reshape(b, t, h, p)


def mamba2_bidirectional(z, xbc, dt_raw, conv_w, conv_b, a_log, dt_bias, d_skip, norm_w):
    b, t, _ = xbc.shape
    pad = SSM_CONV // 2
    xbc = lax.conv_general_dilated(xbc, conv_w.reshape(SSM_CONV, 1, SSM_CONV_CH), (1,), [(pad, pad)],
                                   dimension_numbers=('NWC', 'WIO', 'NWC'),
                                   feature_group_count=SSM_CONV_CH)
    xbc = jax.nn.silu(xbc + conv_b)
    xs, bm, cm = jnp.split(xbc, [SSM_INNER, SSM_INNER + SSM_GROUPS * SSM_STATE], axis=-1)
    xs = xs.reshape(b, t, SSM_HEADS, SSM_HEAD_DIM)
    rep = SSM_HEADS // SSM_GROUPS
    bm = jnp.repeat(bm.reshape(b, t, SSM_GROUPS, SSM_STATE), rep, axis=2)
    cm = jnp.repeat(cm.reshape(b, t, SSM_GROUPS, SSM_STATE), rep, axis=2)
    dt = jax.nn.softplus(dt_raw.astype(jnp.float32) + dt_bias.astype(jnp.float32).reshape(-1))
    dt = dt.reshape(b, t, 2, SSM_HEADS)
    a = -jnp.exp(a_log.astype(jnp.float32))
    flip = lambda u: jnp.flip(u, axis=1)
    y_fwd = ssd_chunked(xs, dt[:, :, 0], a[0], bm, cm)
    y_bwd = flip(ssd_chunked(flip(xs), flip(dt[:, :, 1]), a[1], flip(bm), flip(cm)))
    y = y_fwd + y_bwd + xs.astype(jnp.float32) * d_skip.astype(jnp.float32)[:, None]
    y = y.reshape(b, t, SSM_INNER).astype(z.dtype)
    return rms_norm(y * jax.nn.silu(z), norm_w)


def neighborhood_attention(q, k, v, rpb):
    b, t, h, d = q.shape
    rows = t // GRID_W
    kr = min(NA_WIN_ROWS, rows)
    ncb = GRID_W // NA_COL_BLOCK
    r_idx = np.arange(rows)
    row_start = np.clip(r_idx - kr // 2, 0, rows - kr)
    key_rows = row_start[:, None] + np.arange(kr)[None, :]
    cb = np.arange(ncb)
    blk_col_start = np.clip(cb * NA_COL_BLOCK - NA_WIN_COLS // 2, 0, GRID_W - NA_KEY_COLS)
    key_cols = blk_col_start[:, None] + np.arange(NA_KEY_COLS)[None, :]
    q_cols = cb[:, None] * NA_COL_BLOCK + np.arange(NA_COL_BLOCK)[None, :]
    q_col_start = np.clip(q_cols - NA_WIN_COLS // 2, 0, GRID_W - NA_WIN_COLS)
    kc = key_cols[:, None, :]
    col_in = (kc >= q_col_start[..., None]) & (kc < q_col_start[..., None] + NA_WIN_COLS)
    row_off = key_rows - r_idx[:, None] + NA_WIN_ROWS - 1
    col_off = np.clip(kc - q_cols[..., None] + NA_WIN_COLS - 1, 0, 2 * NA_WIN_COLS - 2)
    bias = rpb[:, row_off[:, None, None, :, None], col_off[None, :, :, None, :]]

    qg = q.reshape(b, rows, ncb, NA_COL_BLOCK, h, d)
    kg = k.reshape(b, rows, GRID_W, h, d)
    vg = v.reshape(b, rows, GRID_W, h, d)
    ridx = key_rows[:, None, :, None]
    cidx = key_cols[None, :, None, :]
    kb = kg[:, ridx, cidx]
    vb = vg[:, ridx, cidx]
    s = jnp.einsum('brcqhd,brckjhd->bhrcqkj', qg, kb).astype(jnp.float32) * (d ** -0.5)
    s = s + bias[None].astype(jnp.float32)
    s = jnp.where(col_in[:, :, None, :], s, NEG_INF)
    s = s.reshape(b, h, rows, ncb, NA_COL_BLOCK, kr * NA_KEY_COLS)
    p = jax.nn.softmax(s, axis=-1).astype(v.dtype)
    vb = vb.reshape(b, rows, ncb, kr * NA_KEY_COLS, h, d)
    o = jnp.einsum('bhrcqn,brcnhd->brcqhd', p, vb)
    return o.reshape(b, t, h, d)


def global_attention(q, k, v):
    b, t, h, dk = q.shape
    nb = t // MLA_QBLOCK
    scale = dk ** -0.5
    qb = q.reshape(b, nb, MLA_QBLOCK, h, dk).transpose(1, 0, 2, 3, 4)

    def block(qi):
        s = jnp.einsum('bqhd,bkhd->bhqk', qi, k).astype(jnp.float32) * scale
        p = jax.nn.softmax(s, axis=-1).astype(v.dtype)
        return jnp.einsum('bhqk,bkhd->bqhd', p, v)

    o = lax.map(block, qb)
    return o.transpose(1, 0, 2, 3, 4).reshape(b, t, h, -1)


def mla_mixer(c_q, c_kv, k_rope, q_norm_w, kv_norm_w, w_uq, w_ukv, pos):
    b, t, _ = c_q.shape
    q = (rms_norm(c_q, q_norm_w) @ w_uq).reshape(b, t, MLA_HEADS, MLA_NOPE + MLA_ROPE)
    kv = (rms_norm(c_kv, kv_norm_w) @ w_ukv).reshape(b, t, MLA_HEADS, MLA_NOPE + MLA_V)
    q_nope, q_pe = q[..., :MLA_NOPE], q[..., MLA_NOPE:]
    k_nope, v = kv[..., :MLA_NOPE], kv[..., MLA_NOPE:]
    q_pe = rope(q_pe, pos)
    k_pe = rope(k_rope[:, :, None, :], pos)
    k_pe = jnp.broadcast_to(k_pe, (b, t, MLA_HEADS, MLA_ROPE))
    qf = jnp.concatenate([q_nope, q_pe], axis=-1)
    kf = jnp.concatenate([k_nope, k_pe], axis=-1)
    return global_attention(qf, kf, v)


def banded_attention(q, k, v, half):
    n, length, h, d = q.shape
    qbs = math.gcd(length, DIL_QBLOCK)
    nb = length // qbs
    span = qbs + 2 * half
    kp = jnp.pad(k, ((0, 0), (half, half), (0, 0), (0, 0)))
    vp = jnp.pad(v, ((0, 0), (half, half), (0, 0), (0, 0)))
    idx = np.arange(nb)[:, None] * qbs + np.arange(span)[None, :]
    kb = kp[:, idx]
    vb = vp[:, idx]
    qb = q.reshape(n, nb, qbs, h, d)
    s = jnp.einsum('nbqhd,nbkhd->nhbqk', qb, kb).astype(jnp.float32) * (d ** -0.5)
    key_pos = (idx - half)[:, None, :]
    q_pos = (np.arange(nb)[:, None] * qbs + np.arange(qbs)[None, :])[:, :, None]
    valid = (np.abs(key_pos - q_pos) <= half) & (key_pos >= 0) & (key_pos < length)
    s = jnp.where(valid, s, NEG_INF)
    m = jnp.max(s, axis=-1, keepdims=True)
    p = jnp.exp(s - m)
    l = jnp.sum(p, axis=-1, keepdims=True)
    o = jnp.einsum('nhbqk,nbkhd->nbqhd', p, vb.astype(jnp.float32))
    o = o / l.transpose(0, 2, 3, 1, 4)
    lse = (m + jnp.log(l))[..., 0].transpose(0, 2, 3, 1)
    return o.reshape(n, length, h, d), lse.reshape(n, length, h)


def dilated_mixer(q, k, v):
    b, t, h, d = q.shape
    outs, lses = [], []
    for window, dil in DIL_PAIRS:
        half = window // (2 * dil)
        sub = t // dil
        fold = lambda a: a.reshape(b, sub, dil, h, d).transpose(0, 2, 1, 3, 4).reshape(b * dil, sub, h, d)
        o, lse = banded_attention(fold(q), fold(k), fold(v), half)
        outs.append(o.reshape(b, dil, sub, h, d).transpose(0, 2, 1, 3, 4).reshape(b, t, h, d))
        lses.append(lse.reshape(b, dil, sub, h).transpose(0, 2, 1, 3).reshape(b, t, h))
    w = jax.nn.softmax(jnp.stack(lses, axis=0), axis=0)
    out = jnp.sum(w[..., None] * jnp.stack(outs, axis=0), axis=0)
    return out.astype(q.dtype)


def swiglu(h, w_gate, w_up, w_down):
    return (jax.nn.silu(h @ w_gate) * (h @ w_up)) @ w_down


def moe_swiglu(h, router_w, w_gate, w_up, w_down):
    logits = (h @ router_w).astype(jnp.float32)
    top_v, top_i = lax.top_k(logits, TOP_K)
    gates = jax.nn.softmax(top_v, axis=-1)
    dense_gate = jnp.sum(jax.nn.one_hot(top_i, N_EXPERTS, dtype=jnp.float32) * gates[..., None], axis=-2)
    dense_gate = dense_gate.astype(h.dtype)
    out = jnp.zeros_like(h)
    for e in range(N_EXPERTS):
        out = out + dense_gate[..., e:e + 1] * swiglu(h, w_gate[e], w_up[e], w_down[e])
    return out


def setup_inputs(seed: int = 0) -> dict:
    key = jax.random.key(seed)
    ks = jax.random.split(key, 32)
    f32 = jnp.float32
    nrm = lambda i, shape, fan_in: jax.random.normal(ks[i], shape, f32) * (fan_in ** -0.5)
    gain = lambda i, shape: 1.0 + 0.05 * jax.random.normal(ks[i], shape, f32)
    x = jax.random.normal(ks[0], (BATCH, SEQ, D_MODEL), f32)
    attn_norm_w = gain(1, (DEPTH, D_MODEL))
    w_in = nrm(2, (DEPTH, D_MODEL, D_IN), D_MODEL)
    conv_w = nrm(3, (DEPTH, SSM_CONV, SSM_CONV_CH), SSM_CONV)
    conv_b = 0.02 * jax.random.normal(ks[4], (DEPTH, SSM_CONV_CH), f32)
    a_log = jnp.log(jax.random.uniform(ks[5], (DEPTH, 2, SSM_HEADS), f32, 1.0, 16.0))
    dt0 = jnp.exp(jax.random.uniform(ks[6], (DEPTH, 2, SSM_HEADS), f32, math.log(1e-3), math.log(1e-1)))
    dt_bias = dt0 + jnp.log(-jnp.expm1(-dt0))
    d_skip = gain(7, (DEPTH, SSM_HEADS))
    ssm_norm_w = gain(8, (DEPTH, SSM_INNER))
    na_rpb = 0.1 * jax.random.normal(ks[9], (DEPTH, NA_HEADS, 2 * NA_WIN_ROWS - 1, 2 * NA_WIN_COLS - 1), f32)
    mla_q_norm_w = gain(10, (DEPTH, MLA_Q_RANK))
    mla_kv_norm_w = gain(11, (DEPTH, MLA_KV_RANK))
    mla_w_uq = nrm(12, (DEPTH, MLA_Q_RANK, MLA_HEADS * (MLA_NOPE + MLA_ROPE)), MLA_Q_RANK)
    mla_w_ukv = nrm(13, (DEPTH, MLA_KV_RANK, MLA_HEADS * (MLA_NOPE + MLA_V)), MLA_KV_RANK)
    w_o = nrm(14, (DEPTH, D_MIX, D_MODEL), D_MIX)
    ffn_norm_w = gain(15, (DEPTH, D_MODEL))
    ffn_w_gate = nrm(16, (N_DENSE, D_MODEL, D_FF), D_MODEL)
    ffn_w_up = nrm(17, (N_DENSE, D_MODEL, D_FF), D_MODEL)
    ffn_w_down = nrm(18, (N_DENSE, D_FF, D_MODEL), D_FF)
    router_w = nrm(19, (N_MOE, D_MODEL, N_EXPERTS), D_MODEL)
    exp_w_gate = nrm(20, (N_MOE, N_EXPERTS, D_MODEL, D_FF_EXPERT), D_MODEL)
    exp_w_up = nrm(21, (N_MOE, N_EXPERTS, D_MODEL, D_FF_EXPERT), D_MODEL)
    exp_w_down = nrm(22, (N_MOE, N_EXPERTS, D_FF_EXPERT, D_MODEL), D_FF_EXPERT)
    final_norm_w = gain(23, (D_MODEL,))
    return {'x': x, 'attn_norm_w': attn_norm_w, 'w_in': w_in, 'conv_w': conv_w, 'conv_b': conv_b,
            'a_log': a_log, 'dt_bias': dt_bias, 'd_skip': d_skip, 'ssm_norm_w': ssm_norm_w,
            'na_rpb': na_rpb, 'mla_q_norm_w': mla_q_norm_w, 'mla_kv_norm_w': mla_kv_norm_w,
            'mla_w_uq': mla_w_uq, 'mla_w_ukv': mla_w_ukv, 'w_o': w_o, 'ffn_norm_w': ffn_norm_w,
            'ffn_w_gate': ffn_w_gate, 'ffn_w_up': ffn_w_up, 'ffn_w_down': ffn_w_down,
            'router_w': router_w, 'exp_w_gate': exp_w_gate, 'exp_w_up': exp_w_up,
            'exp_w_down': exp_w_down, 'final_norm_w': final_norm_w}


def reference(x, attn_norm_w, w_in, conv_w, conv_b, a_log, dt_bias, d_skip, ssm_norm_w, na_rpb,
              mla_q_norm_w, mla_kv_norm_w, mla_w_uq, mla_w_ukv, w_o, ffn_norm_w, ffn_w_gate, ffn_w_up,
              ffn_w_down, router_w, exp_w_gate, exp_w_up, exp_w_down, final_norm_w):
    b, t, _ = x.shape
    pos = jnp.arange(t, dtype=jnp.float32)
    split_points = [int(s) for s in np.cumsum(IN_SIZES)[:-1]]
    for l in range(DEPTH):
        h = rms_norm(x, attn_norm_w[l])
        proj = h @ w_in[l]
        (z, xbc, dt_raw, na_q, na_k, na_v, c_q, c_kv, k_rope,
         dl_q, dl_k, dl_v) = jnp.split(proj, split_points, axis=-1)
        y_ssm = mamba2_bidirectional(z, xbc, dt_raw, conv_w[l], conv_b[l], a_log[l], dt_bias[l],
                                     d_skip[l], ssm_norm_w[l])
        y_na = neighborhood_attention(split_heads(na_q, NA_HEADS), split_heads(na_k, NA_HEADS),
                                      split_heads(na_v, NA_HEADS), na_rpb[l]).reshape(b, t, -1)
        y_mla = mla_mixer(c_q, c_kv, k_rope, mla_q_norm_w[l], mla_kv_norm_w[l], mla_w_uq[l],
                          mla_w_ukv[l], pos).reshape(b, t, -1)
        y_dil = dilated_mixer(rope(split_heads(dl_q, DIL_HEADS), pos), rope(split_heads(dl_k, DIL_HEADS), pos),
                              split_heads(dl_v, DIL_HEADS)).reshape(b, t, -1)
        mixed = jnp.concatenate([y_ssm, y_na, y_mla, y_dil], axis=-1)
        x = x + mixed @ w_o[l]
        h = rms_norm(x, ffn_norm_w[l])
        j = l // 2
        if l % 2 == 0:
            x = x + swiglu(h, ffn_w_gate[j], ffn_w_up[j], ffn_w_down[j])
        else:
            x = x + moe_swiglu(h, router_w[j], exp_w_gate[j], exp_w_up[j], exp_w_down[j])
    return rms_norm(x, final_norm_w)
```

```python
import functools
import math

import numpy as np
import jax
import jax.numpy as jnp
from jax import lax
from jax.experimental import pallas as pl
from jax.experimental.pallas import tpu as pltpu

F32 = jnp.float32
BF16 = jnp.bfloat16

D_MODEL = 2048
GRID_W = 64
HEAD_DIM = 64
ROPE_THETA = 10000.0
NORM_EPS = 1e-6
NEG_INF = -1e30

SSM_HEADS = 8
SSM_HEAD_DIM = 64
SSM_INNER = SSM_HEADS * SSM_HEAD_DIM
SSM_GROUPS = 2
SSM_STATE = 128
SSM_CONV = 5
SSM_CHUNK = 128
SSM_CONV_CH = SSM_INNER + 2 * SSM_GROUPS * SSM_STATE

NA_HEADS = 8
NA_WIN_ROWS = 8
NA_WIN_COLS = 16
NA_COL_BLOCK = 16
NA_KEY_COLS = 32
NA_ROWS_PER_STEP = 4

MLA_HEADS = 8
MLA_Q_RANK = 512
MLA_KV_RANK = 512
MLA_NOPE = 64
MLA_ROPE = 32
MLA_V = 64

DIL_HEADS = 8
DIL_PAIRS = ((128, 1), (512, 4), (2048, 16))
DIL_QBLOCK = 128

N_EXPERTS = 8
TOP_K = 2

LANES = 128
HEAD_PAIR = 2 * HEAD_DIM
N_PAIRS = 4
SLAB = 512

COL_Z, COL_XBC, COL_NAQ, COL_NAK, COL_NAV, COL_CQ, COL_CKV, COL_DLQ, COL_DLK, COL_DLV = (
    0, 1, 3, 4, 5, 6, 7, 8, 9, 10)
PROJ_MAIN = 11 * SLAB
TAIL_DT = 32

VMEM_LIMIT = 56 * 1024 * 1024


def _cparams(sem, vmem=VMEM_LIMIT):
    return pltpu.CompilerParams(dimension_semantics=sem, vmem_limit_bytes=vmem)


def _lane_lt(shape, bound, period=None):
    lane = lax.broadcasted_iota(jnp.int32, shape, len(shape) - 1)
    if period is not None:
        lane = lane % period
    return lane < bound


def _rms(x, w):
    ms = jnp.mean(x * x, axis=-1, keepdims=True)
    return x * lax.rsqrt(ms + NORM_EPS) * w


def _dot(a, b):
    return jnp.dot(a, b, preferred_element_type=F32)


def _dot_nt(a, b):
    return lax.dot_general(a, b, (((1,), (1,)), ((), ())), preferred_element_type=F32)


def _dot_tn(a, b):
    return lax.dot_general(a, b, (((0,), (0,)), ((), ())), preferred_element_type=F32)


def _norm_matmul_kernel(x_ref, nw_ref, w_ref, o_ref, h_ref):
    @pl.when(pl.program_id(1) == 0)
    def _():
        h_ref[...] = _rms(x_ref[...].astype(F32), nw_ref[...]).astype(BF16)

    o_ref[...] = _dot(h_ref[...], w_ref[...]).astype(o_ref.dtype)


def norm_matmul(x, xcol, k, nw, w, out_dtype, tm, tn):
    n = x.shape[0]
    nout = w.shape[1]
    return pl.pallas_call(
        _norm_matmul_kernel,
        grid=(n // tm, nout // tn),
        in_specs=[
            pl.BlockSpec((tm, k), lambda i, j: (i, xcol)),
            pl.BlockSpec((1, k), lambda i, j: (0, 0)),
            pl.BlockSpec((k, tn), lambda i, j: (0, j)),
        ],
        out_specs=pl.BlockSpec((tm, tn), lambda i, j: (i, j)),
        out_shape=jax.ShapeDtypeStruct((n, nout), out_dtype),
        scratch_shapes=[pltpu.VMEM((tm, k), BF16)],
        compiler_params=_cparams(("parallel", "arbitrary")),
        name="norm_matmul",
    )(x, nw.reshape(1, k), w)


def _out_proj_kernel(a0_ref, a1_ref, a2_ref, a3_ref, w_ref, r_ref, o_ref):
    acc = r_ref[...]
    for s, a_ref in enumerate((a0_ref, a1_ref, a2_ref, a3_ref)):
        acc = acc + _dot(a_ref[...], w_ref[s * SLAB:(s + 1) * SLAB, :])
    o_ref[...] = acc


def out_proj(mix, w, res, tm, tn):
    n = res.shape[0]
    return pl.pallas_call(
        _out_proj_kernel,
        grid=(n // tm, D_MODEL // tn),
        in_specs=[pl.BlockSpec((tm, SLAB), lambda i, j: (i, 0))] * 4 + [
            pl.BlockSpec((4 * SLAB, tn), lambda i, j: (0, j)),
            pl.BlockSpec((tm, tn), lambda i, j: (i, j)),
        ],
        out_specs=pl.BlockSpec((tm, tn), lambda i, j: (i, j)),
        out_shape=jax.ShapeDtypeStruct((n, D_MODEL), F32),
        compiler_params=_cparams(("parallel", "arbitrary")),
        name="out_proj",
    )(*mix, w, res)


def _ffn_kernel(x_ref, nw_ref, wg_ref, wu_ref, wd_ref, o_ref, h_ref):
    @pl.when(pl.program_id(1) == 0)
    def _():
        x = x_ref[...]
        h_ref[...] = _rms(x, nw_ref[...]).astype(BF16)
        o_ref[...] = x

    h = h_ref[...]
    g = _dot(h, wg_ref[...])
    u = _dot(h, wu_ref[...])
    a = (g * jax.nn.sigmoid(g) * u).astype(BF16)
    o_ref[...] += _dot(a, wd_ref[...])


def ffn_dense(x, nw, wg, wu, wd, tm, tf):
    n = x.shape[0]
    d_ff = wg.shape[1]
    return pl.pallas_call(
        _ffn_kernel,
        grid=(n // tm, d_ff // tf),
        in_specs=[
            pl.BlockSpec((tm, D_MODEL), lambda i, j: (i, 0)),
            pl.BlockSpec((1, D_MODEL), lambda i, j: (0, 0)),
            pl.BlockSpec((D_MODEL, tf), lambda i, j: (0, j)),
            pl.BlockSpec((D_MODEL, tf), lambda i, j: (0, j)),
            pl.BlockSpec((tf, D_MODEL), lambda i, j: (j, 0)),
        ],
        out_specs=pl.BlockSpec((tm, D_MODEL), lambda i, j: (i, 0)),
        out_shape=jax.ShapeDtypeStruct((n, D_MODEL), F32),
        scratch_shapes=[pltpu.VMEM((tm, D_MODEL), BF16)],
        compiler_params=_cparams(("parallel", "arbitrary")),
        name="ffn_dense",
    )(x, nw.reshape(1, D_MODEL), wg, wu, wd)


def _router_kernel(x_ref, nw_ref, rw_ref, idx_ref, gate_ref):
    h = _rms(x_ref[...], nw_ref[...])
    logits = jnp.dot(h, rw_ref[...], preferred_element_type=F32, precision=lax.Precision.HIGHEST)
    lane = lax.broadcasted_iota(jnp.int32, logits.shape, 1)
    logits = jnp.where(lane < N_EXPERTS, logits, -jnp.inf)
    m1 = jnp.max(logits, axis=-1, keepdims=True)
    i1 = jnp.min(jnp.where(logits == m1, lane, LANES), axis=-1, keepdims=True)
    rest = jnp.where(lane == i1, -jnp.inf, logits)
    m2 = jnp.max(rest, axis=-1, keepdims=True)
    i2 = jnp.min(jnp.where(rest == m2, lane, LANES), axis=-1, keepdims=True)
    e2 = jnp.exp(m2 - m1)
    g1 = 1.0 / (1.0 + e2)
    g2 = e2 / (1.0 + e2)
    idx_ref[...] = jnp.where(lane == 0, i1, i2)[:, :TOP_K]
    gate_ref[...] = jnp.where(lane == 0, g1, g2)[:, :TOP_K]


def moe_router(x, nw, router_w, tm):
    n = x.shape[0]
    rw = jnp.zeros((D_MODEL, LANES), F32).at[:, :N_EXPERTS].set(router_w)
    return pl.pallas_call(
        _router_kernel,
        grid=(n // tm,),
        in_specs=[
            pl.BlockSpec((tm, D_MODEL), lambda i: (i, 0)),
            pl.BlockSpec((1, D_MODEL), lambda i: (0, 0)),
            pl.BlockSpec((D_MODEL, LANES), lambda i: (0, 0)),
        ],
        out_specs=[pl.BlockSpec((tm, TOP_K), lambda i: (i, 0)),
                   pl.BlockSpec((tm, TOP_K), lambda i: (i, 0))],
        out_shape=[jax.ShapeDtypeStruct((n, TOP_K), jnp.int32),
                   jax.ShapeDtypeStruct((n, TOP_K), F32)],
        compiler_params=_cparams(("parallel",)),
        name="moe_router",
    )(x, nw.reshape(1, D_MODEL), rw)


def moe_plan(top_i, tm):
    n = top_i.shape[0]
    flat_e = top_i.reshape(-1)
    onehot = (flat_e[:, None] == jnp.arange(N_EXPERTS, dtype=jnp.int32)[None, :]).astype(jnp.int32)
    csum = jnp.cumsum(onehot, axis=0)
    counts = csum[-1]
    rank = jnp.sum(onehot * csum, axis=1) - 1
    padded = ((counts + tm - 1) // tm) * tm
    pend = jnp.cumsum(padded)
    pstart = pend - padded
    pos = pstart[flat_e] + rank
    n_slots = n * TOP_K + N_EXPERTS * tm
    n_tiles = n_slots // tm
    src = jnp.zeros((n_slots,), jnp.int32).at[pos].set(jnp.arange(n * TOP_K, dtype=jnp.int32) // TOP_K)
    tile_start = jnp.arange(n_tiles, dtype=jnp.int32) * tm
    tile_expert = jnp.sum((tile_start[:, None] >= pend[None, :]).astype(jnp.int32), axis=1)
    tile_valid = (tile_start < pend[-1]).astype(jnp.int32)
    last_valid = jnp.maximum(pend[-1] // tm - 1, 0)
    tile_expert = jnp.where(tile_valid == 1, tile_expert, tile_expert[last_valid]).astype(jnp.int32)
    return src, pos.reshape(n, TOP_K).astype(jnp.int32), tile_expert, tile_valid


def _moe_ffn_kernel(te_ref, tv_ref, src_ref, x_hbm, nw_ref, wg_ref, wu_ref, wd_ref, y_ref,
                    xbuf, h_ref, sem, *, tm):
    i = pl.program_id(0)
    j = pl.program_id(1)
    valid = tv_ref[i] == 1

    def row_copy(r):
        tok = src_ref[0, 0, r]
        return pltpu.make_async_copy(x_hbm.at[pl.ds(tok, 1)], xbuf.at[pl.ds(r, 1)], sem)

    @pl.when(jnp.logical_and(j == 0, jnp.logical_not(valid)))
    def _():
        y_ref[...] = jnp.zeros_like(y_ref)

    @pl.when(jnp.logical_and(j == 0, valid))
    def _():
        def start(r, c):
            row_copy(r).start()
            return c

        lax.fori_loop(0, tm, start, 0)

        def wait(r, c):
            row_copy(r).wait()
            return c

        lax.fori_loop(0, tm, wait, 0)
        h_ref[...] = _rms(xbuf[...], nw_ref[...]).astype(BF16)

    @pl.when(valid)
    def _():
        h = h_ref[...]
        g = _dot(h, wg_ref[0])
        u = _dot(h, wu_ref[0])
        a = (g * jax.nn.sigmoid(g) * u).astype(BF16)
        d = _dot(a, wd_ref[0])

        @pl.when(j == 0)
        def _():
            y_ref[...] = d

        @pl.when(j != 0)
        def _():
            y_ref[...] += d


def moe_ffn(x, nw, wg, wu, wd, src, tile_expert, tile_valid, tm, tf):
    n_slots = src.shape[0]
    d_ff = wg.shape[2]
    nf = d_ff // tf

    def wcol(i, j, te_ref, tv_ref):
        return (te_ref[i], 0, jnp.where(tv_ref[i] == 1, j, nf - 1))

    def wrow(i, j, te_ref, tv_ref):
        return (te_ref[i], jnp.where(tv_ref[i] == 1, j, nf - 1), 0)

    grid_spec = pltpu.PrefetchScalarGridSpec(
        num_scalar_prefetch=2,
        grid=(n_slots // tm, nf),
        in_specs=[
            pl.BlockSpec((1, 1, tm), lambda i, j, *_: (i, 0, 0), memory_space=pltpu.SMEM),
            pl.BlockSpec(memory_space=pl.ANY),
            pl.BlockSpec((1, D_MODEL), lambda i, j, *_: (0, 0)),
            pl.BlockSpec((1, D_MODEL, tf), wcol),
            pl.BlockSpec((1, D_MODEL, tf), wcol),
            pl.BlockSpec((1, tf, D_MODEL), wrow),
        ],
        out_specs=pl.BlockSpec((tm, D_MODEL), lambda i, j, *_: (i, 0)),
        scratch_shapes=[pltpu.VMEM((tm, D_MODEL), F32), pltpu.VMEM((tm, D_MODEL), BF16),
                        pltpu.SemaphoreType.DMA],
    )
    return pl.pallas_call(
        functools.partial(_moe_ffn_kernel, tm=tm),
        grid_spec=grid_spec,
        out_shape=jax.ShapeDtypeStruct((n_slots, D_MODEL), F32),
        compiler_params=_cparams(("arbitrary", "arbitrary")),
        name="moe_ffn",
    )(tile_expert, tile_valid, src.reshape(n_slots // tm, 1, tm), x, nw.reshape(1, D_MODEL), wg, wu, wd)


def _moe_combine_kernel(pos_ref, x_ref, gate_ref, y_hbm, fw_ref, o_ref, ybuf, sem, *, tm, final_norm):
    def row_copy(r, k):
        slot = pos_ref[0, 0, r * TOP_K + k]
        return pltpu.make_async_copy(y_hbm.at[pl.ds(slot, 1)], ybuf.at[k, pl.ds(r, 1)], sem)

    def start(r, c):
        for k in range(TOP_K):
            row_copy(r, k).start()
        return c

    lax.fori_loop(0, tm, start, 0)

    def wait(r, c):
        for k in range(TOP_K):
            row_copy(r, k).wait()
        return c

    lax.fori_loop(0, tm, wait, 0)
    gates = gate_ref[...]
    out = x_ref[...]
    for k in range(TOP_K):
        out = out + gates[:, k:k + 1] * ybuf[k]
    if final_norm:
        out = _rms(out, fw_ref[...])
    o_ref[...] = out


def moe_combine(x, gates, y, pos, final_w, tm):
    n = x.shape[0]
    final_norm = final_w is not None
    fw = final_w if final_norm else jnp.ones((D_MODEL,), F32)
    return pl.pallas_call(
        functools.partial(_moe_combine_kernel, tm=tm, final_norm=final_norm),
        grid=(n // tm,),
        in_specs=[
            pl.BlockSpec((1, 1, tm * TOP_K), lambda i: (i, 0, 0), memory_space=pltpu.SMEM),
            pl.BlockSpec((tm, D_MODEL), lambda i: (i, 0)),
            pl.BlockSpec((tm, TOP_K), lambda i: (i, 0)),
            pl.BlockSpec(memory_space=pl.ANY),
            pl.BlockSpec((1, D_MODEL), lambda i: (0, 0)),
        ],
        out_specs=pl.BlockSpec((tm, D_MODEL), lambda i: (i, 0)),
        out_shape=jax.ShapeDtypeStruct((n, D_MODEL), F32),
        scratch_shapes=[pltpu.VMEM((TOP_K, tm, D_MODEL), F32), pltpu.SemaphoreType.DMA],
        compiler_params=_cparams(("arbitrary",)),
        name="moe_combine",
    )(pos.reshape(n // tm, 1, tm * TOP_K), x, gates, y, fw.reshape(1, D_MODEL))


def _rmsnorm_kernel(x_ref, w_ref, o_ref):
    o_ref[...] = _rms(x_ref[...], w_ref[...])


def rmsnorm_rows(x, w, tm):
    n = x.shape[0]
    return pl.pallas_call(
        _rmsnorm_kernel,
        grid=(n // tm,),
        in_specs=[pl.BlockSpec((tm, D_MODEL), lambda i: (i, 0)),
                  pl.BlockSpec((1, D_MODEL), lambda i: (0, 0))],
        out_specs=pl.BlockSpec((tm, D_MODEL), lambda i: (i, 0)),
        out_shape=jax.ShapeDtypeStruct((n, D_MODEL), F32),
        compiler_params=_cparams(("parallel",)),
        name="final_norm",
    )(x, w.reshape(1, D_MODEL))


CONV_PAD = 8


def _conv_kernel(x_ref, w_ref, b_ref, o_ref, pad_ref, *, t):
    half = SSM_CONV // 2
    zeros = jnp.zeros((CONV_PAD, pad_ref.shape[1]), F32)
    pad_ref[0:CONV_PAD, :] = zeros
    pad_ref[CONV_PAD + t:CONV_PAD + t + CONV_PAD, :] = zeros
    pad_ref[CONV_PAD:CONV_PAD + t, :] = x_ref[...].astype(F32)
    acc = jnp.zeros(o_ref.shape, F32) + b_ref[...]
    for k in range(SSM_CONV):
        acc = acc + pad_ref[pl.ds(CONV_PAD - half + k, t), :] * w_ref[k:k + 1, :]
    o_ref[...] = (acc * jax.nn.sigmoid(acc)).astype(o_ref.dtype)


def conv_silu(proj, conv_w, conv_b, b, t):
    tc = 256
    nblk = SSM_CONV_CH // tc
    col0 = COL_XBC * SLAB // tc
    return pl.pallas_call(
        functools.partial(_conv_kernel, t=t),
        grid=(b, nblk),
        in_specs=[
            pl.BlockSpec((t, tc), lambda i, j: (i, col0 + j)),
            pl.BlockSpec((SSM_CONV, tc), lambda i, j: (0, j)),
            pl.BlockSpec((1, tc), lambda i, j: (0, j)),
        ],
        out_specs=pl.BlockSpec((t, tc), lambda i, j: (i, j)),
        out_shape=jax.ShapeDtypeStruct((b * t, SSM_CONV_CH), BF16),
        scratch_shapes=[pltpu.VMEM((t + 2 * CONV_PAD, tc), F32)],
        compiler_params=_cparams(("parallel", "parallel")),
        name="conv_silu",
    )(proj, conv_w, conv_b.reshape(1, SSM_CONV_CH))


def _ssd_kernel(xbc_ref, tail_ref, bias_ref, alog_ref, tri_ref, y_ref, state_ref, *, direction):
    q = SSM_CHUNK

    @pl.when(pl.program_id(1) == 0)
    def _():
        state_ref[...] = jnp.zeros_like(state_ref)

    dt = jax.nn.softplus(tail_ref[...] + bias_ref[...])
    da = dt * (-jnp.exp(alog_ref[...]))
    cs = jnp.dot(tri_ref[...], da, preferred_element_type=F32, precision=lax.Precision.HIGHEST)
    total = cs[q - 1:q, :]
    if direction == 0:
        e_out = cs
        e_in = total - cs
        e_seg = cs
    else:
        ex = cs - da
        e_out = total - ex
        e_in = ex
        e_seg = -ex
    e_seg_t = jnp.transpose(e_seg)
    dec_out = jnp.exp(e_out)
    dec_in_dt = jnp.exp(e_in) * dt
    dt_t = jnp.transpose(dt)
    dec_tot = jnp.exp(total)

    row = lax.broadcasted_iota(jnp.int32, (q, q), 0)
    col = lax.broadcasted_iota(jnp.int32, (q, q), 1)
    keep = (row >= col) if direction == 0 else (col >= row)
    first_half = _lane_lt((q, HEAD_PAIR), HEAD_DIM)

    xs = xbc_ref[:, 0:SSM_INNER]
    heads_per_group = SSM_HEADS // SSM_GROUPS
    for g in range(SSM_GROUPS):
        bm = xbc_ref[:, SSM_INNER + g * SSM_STATE:SSM_INNER + (g + 1) * SSM_STATE]
        cm = xbc_ref[:, SSM_INNER + (SSM_GROUPS + g) * SSM_STATE:SSM_INNER + (SSM_GROUPS + g + 1) * SSM_STATE]
        cb = _dot_nt(cm, bm)
        bm_f = bm.astype(F32)
        cm_f = cm.astype(F32)
        for pp in range(heads_per_group // 2):
            pair = g * (heads_per_group // 2) + pp
            xs_pair = xs[:, pair * HEAD_PAIR:(pair + 1) * HEAD_PAIR]
            st = state_ref[pair]
            ys, sts, decs = [], [], []
            for hh in range(2):
                lane = TAIL_DT + direction * SSM_HEADS + pair * 2 + hh
                seg = e_seg[:, lane:lane + 1] - e_seg_t[lane:lane + 1, :]
                lmat = jnp.where(keep, jnp.exp(seg), 0.0)
                w = (cb * lmat * dt_t[lane:lane + 1, :]).astype(BF16)
                y = _dot(w, xs_pair)
                y = y + _dot((cm_f * dec_out[:, lane:lane + 1]).astype(BF16), st.astype(BF16))
                ys.append(y)
                sts.append(_dot_tn((bm_f * dec_in_dt[:, lane:lane + 1]).astype(BF16), xs_pair))
                decs.append(dec_tot[:, lane:lane + 1])
            y_ref[:, pair * HEAD_PAIR:(pair + 1) * HEAD_PAIR] = jnp.where(first_half, ys[0], ys[1])
            first_half_s = _lane_lt(st.shape, HEAD_DIM)
            state_ref[pair] = (st * jnp.where(first_half_s, decs[0], decs[1])
                               + jnp.where(first_half_s, sts[0], sts[1]))


def ssd_scan(xbc, tail, dt_bias, a_log, b, t, direction):
    q = SSM_CHUNK
    nc = t // q
    bias_row = jnp.zeros((1, LANES), F32).at[0, TAIL_DT:TAIL_DT + 2 * SSM_HEADS].set(dt_bias.reshape(-1))
    alog_row = jnp.zeros((1, LANES), F32).at[0, TAIL_DT:TAIL_DT + 2 * SSM_HEADS].set(a_log.reshape(-1))
    tri = jnp.asarray(np.tril(np.ones((q, q), np.float32)))

    def chunk(i, c):
        return i * nc + (c if direction == 0 else nc - 1 - c)

    return pl.pallas_call(
        functools.partial(_ssd_kernel, direction=direction),
        grid=(b, nc),
        in_specs=[
            pl.BlockSpec((q, SSM_CONV_CH), lambda i, c: (chunk(i, c), 0)),
            pl.BlockSpec((q, LANES), lambda i, c: (chunk(i, c), 0)),
            pl.BlockSpec((1, LANES), lambda i, c: (0, 0)),
            pl.BlockSpec((1, LANES), lambda i, c: (0, 0)),
            pl.BlockSpec((q, q), lambda i, c: (0, 0)),
        ],
        out_specs=pl.BlockSpec((q, SSM_INNER), lambda i, c: (chunk(i, c), 0)),
        out_shape=jax.ShapeDtypeStruct((b * t, SSM_INNER), F32),
        scratch_shapes=[pltpu.VMEM((N_PAIRS, SSM_STATE, HEAD_PAIR), F32)],
        compiler_params=_cparams(("parallel", "arbitrary")),
        name="ssd_scan_%s" % ("fwd" if direction == 0 else "bwd"),
    )(xbc, tail, bias_row, alog_row, tri)


def _ssd_combine_kernel(yf_ref, yb_ref, xs_ref, z_ref, d_ref, nw_ref, o_ref):
    y = yf_ref[...] + yb_ref[...] + xs_ref[...].astype(F32) * d_ref[...]
    z = z_ref[...].astype(F32)
    o_ref[...] = _rms(y * (z * jax.nn.sigmoid(z)), nw_ref[...]).astype(o_ref.dtype)


def ssd_combine(y_f, y_b, xbc, proj, d_skip, norm_w, tm):
    n = y_f.shape[0]
    d_row = jnp.repeat(d_skip, SSM_HEAD_DIM).reshape(1, SSM_INNER)
    row = lambda i: (i, 0)
    return pl.pallas_call(
        _ssd_combine_kernel,
        grid=(n // tm,),
        in_specs=[
            pl.BlockSpec((tm, SSM_INNER), row),
            pl.BlockSpec((tm, SSM_INNER), row),
            pl.BlockSpec((tm, SSM_INNER), row),
            pl.BlockSpec((tm, SLAB), lambda i: (i, COL_Z)),
            pl.BlockSpec((1, SSM_INNER), lambda i: (0, 0)),
            pl.BlockSpec((1, SSM_INNER), lambda i: (0, 0)),
        ],
        out_specs=pl.BlockSpec((tm, SSM_INNER), row),
        out_shape=jax.ShapeDtypeStruct((n, SSM_INNER), BF16),
        compiler_params=_cparams(("parallel",)),
        name="ssd_combine",
    )(y_f, y_b, xbc, proj, d_row, norm_w.reshape(1, SSM_INNER))


def _pair_softmax_attend(q2, k2, v2, bias_fn):
    first_q = _lane_lt(q2.shape, HEAD_DIM)
    outs, lses = [], []
    for hh in range(2):
        qh = jnp.where(first_q if hh == 0 else jnp.logical_not(first_q), q2, jnp.zeros_like(q2))
        s = bias_fn(hh, _dot_nt(qh, k2))
        m = jnp.max(s, axis=-1, keepdims=True)
        p = jnp.exp(s - m)
        l = jnp.sum(p, axis=-1, keepdims=True)
        outs.append(_dot(p.astype(BF16), v2) / l)
        lses.append(m + jnp.log(l))
    first_o = _lane_lt(outs[0].shape, HEAD_DIM)
    return jnp.where(first_o, outs[0], outs[1]), jnp.where(first_o, lses[0], lses[1])


def na_bias_tables(rpb, rows):
    kr = min(NA_WIN_ROWS, rows)
    qc = np.arange(GRID_W)
    kc = np.arange(GRID_W)
    q_start = np.clip(qc - NA_WIN_COLS // 2, 0, GRID_W - NA_WIN_COLS)
    col_in = (kc[None, :] >= q_start[:, None]) & (kc[None, :] < q_start[:, None] + NA_WIN_COLS)
    col_off = np.clip(kc[None, :] - qc[:, None] + NA_WIN_COLS - 1, 0, 2 * NA_WIN_COLS - 2)

    def table(r):
        row_start = int(np.clip(r - kr // 2, 0, rows - kr))
        row_off = row_start + np.arange(kr) - r + NA_WIN_ROWS - 1
        bias = rpb[:, row_off[None, :, None], col_off[:, None, :]]
        bias = jnp.where(jnp.asarray(col_in)[None, :, None, :], bias, NEG_INF)
        return bias.reshape(NA_HEADS, GRID_W, kr * GRID_W)

    rs = NA_ROWS_PER_STEP
    lo = [table(r) for r in range(rs)]
    mid = [table(min(rs, rows - 1))] * rs
    hi = [table(r) for r in range(rows - rs, rows)]
    return jnp.stack([jnp.stack(lo), jnp.stack(mid), jnp.stack(hi)])


def _na_kernel(q_ref, k_ref, v_ref, bias_ref, o_ref, *, rows, kr):
    step = pl.program_id(1)
    rs = NA_ROWS_PER_STEP
    for rr in range(rs):
        r = step * rs + rr
        row_start = jnp.clip(r - kr // 2, 0, rows - kr)
        k0 = pl.multiple_of(row_start * GRID_W, GRID_W)
        for pair in range(N_PAIRS):
            cols = slice(pair * HEAD_PAIR, (pair + 1) * HEAD_PAIR)
            q2 = q_ref[rr * GRID_W:(rr + 1) * GRID_W, cols] * jnp.asarray(HEAD_DIM ** -0.5, BF16)
            k2 = k_ref[pl.ds(k0, kr * GRID_W), cols]
            v2 = v_ref[pl.ds(k0, kr * GRID_W), cols]
            o, _ = _pair_softmax_attend(
                q2, k2, v2, lambda hh, s: s + bias_ref[0, rr, pair * 2 + hh])
            o_ref[rr * GRID_W:(rr + 1) * GRID_W, cols] = o.astype(o_ref.dtype)


def na_attention(proj, rpb, b, t):
    rows = t // GRID_W
    kr = min(NA_WIN_ROWS, rows)
    rs = NA_ROWS_PER_STEP
    nsteps = rows // rs
    bias = na_bias_tables(rpb, rows)

    def kind(i, s):
        return jnp.where(s == 0, 0, jnp.where(s == nsteps - 1, 2, 1))

    return pl.pallas_call(
        functools.partial(_na_kernel, rows=rows, kr=kr),
        grid=(b, nsteps),
        in_specs=[
            pl.BlockSpec((rs * GRID_W, SLAB), lambda i, s: (i * nsteps + s, COL_NAQ)),
            pl.BlockSpec((t, SLAB), lambda i, s: (i, COL_NAK)),
            pl.BlockSpec((t, SLAB), lambda i, s: (i, COL_NAV)),
            pl.BlockSpec((1, rs, NA_HEADS, GRID_W, kr * GRID_W), lambda i, s: (kind(i, s), 0, 0, 0, 0)),
        ],
        out_specs=pl.BlockSpec((rs * GRID_W, SLAB), lambda i, s: (i * nsteps + s, 0)),
        out_shape=jax.ShapeDtypeStruct((b * t, SLAB), BF16),
        compiler_params=_cparams(("parallel", "arbitrary")),
        name="na_attention",
    )(proj, proj, proj, bias)


def _rope_angles(t, d):
    inv = ROPE_THETA ** (-np.arange(0, d, 2, dtype=np.float32) / d)
    return np.arange(t, dtype=np.float32)[:, None] * inv[None, :]


def rope_tables_pair(t):
    ang = _rope_angles(t, HEAD_DIM)
    cos = np.tile(np.cos(ang), (1, 4))
    sin = np.tile(np.concatenate([-np.sin(ang), np.sin(ang)], axis=1), (1, 2))
    return jnp.asarray(cos, F32), jnp.asarray(sin, F32)


def _rope_qk_kernel(x_ref, cos_ref, sin_ref, o_ref):
    cos = cos_ref[...]
    sin = sin_ref[...]
    half = HEAD_DIM // 2
    for c in range(x_ref.shape[1] // LANES):
        x = x_ref[:, c * LANES:(c + 1) * LANES].astype(F32)
        rot = jnp.where(_lane_lt(x.shape, half, HEAD_DIM),
                        pltpu.roll(x, LANES - half, 1), pltpu.roll(x, half, 1))
        y = x * cos + rot * sin
        if c < N_PAIRS:
            y = y * (HEAD_DIM ** -0.5)
        o_ref[:, c * LANES:(c + 1) * LANES] = y.astype(o_ref.dtype)


def rope_qk(proj, b, t, tm):
    n = b * t
    cos, sin = rope_tables_pair(t)
    nb = t // tm
    return pl.pallas_call(
        _rope_qk_kernel,
        grid=(n // tm,),
        in_specs=[
            pl.BlockSpec((tm, 2 * SLAB), lambda i: (i, COL_DLQ // 2)),
            pl.BlockSpec((tm, LANES), lambda i: (i % nb, 0)),
            pl.BlockSpec((tm, LANES), lambda i: (i % nb, 0)),
        ],
        out_specs=pl.BlockSpec((tm, 2 * SLAB), lambda i: (i, 0)),
        out_shape=jax.ShapeDtypeStruct((n, 2 * SLAB), BF16),
        compiler_params=_cparams(("parallel",)),
        name="rope_qk",
    )(proj, cos, sin)


def _band_kernel(q_ref, k_ref, v_ref, o_ref, lse_ref, *, sub, half, span):
    qb = pl.program_id(2)
    tq = q_ref.shape[0]
    start = jnp.clip(qb * tq - half, 0, sub - span)
    start = pl.multiple_of(start, half)
    q_pos = qb * tq + lax.broadcasted_iota(jnp.int32, (tq, span), 0)
    k_pos = start + lax.broadcasted_iota(jnp.int32, (tq, span), 1)
    valid = jnp.abs(k_pos - q_pos) <= half
    for pair in range(N_PAIRS):
        cols = slice(pair * HEAD_PAIR, (pair + 1) * HEAD_PAIR)
        k2 = k_ref[pl.ds(start, span), cols]
        v2 = v_ref[pl.ds(start, span), cols]
        o, lse = _pair_softmax_attend(q_ref[:, cols], k2, v2,
                                      lambda hh, s: jnp.where(valid, s, NEG_INF))
        o_ref[:, cols] = o
        lse_ref[:, cols] = lse


def band_attention(qk, proj, b, t, window, dil):
    half = window // (2 * dil)
    sub = t // dil
    tq = DIL_QBLOCK
    span = tq + 2 * half
    nqb = sub // tq
    n = b * t
    qk_v = qk.reshape(n // dil, dil * 2 * SLAB)
    proj_v = proj.reshape(n // dil, dil * PROJ_MAIN)
    pm = PROJ_MAIN // SLAB
    o, lse = pl.pallas_call(
        functools.partial(_band_kernel, sub=sub, half=half, span=span),
        grid=(b, dil, nqb),
        in_specs=[
            pl.BlockSpec((tq, SLAB), lambda i, p, s: (i * nqb + s, 2 * p)),
            pl.BlockSpec((sub, SLAB), lambda i, p, s: (i, 2 * p + 1)),
            pl.BlockSpec((sub, SLAB), lambda i, p, s: (i, pm * p + COL_DLV)),
        ],
        out_specs=[pl.BlockSpec((tq, SLAB), lambda i, p, s: (i * nqb + s, p))] * 2,
        out_shape=[jax.ShapeDtypeStruct((n // dil, dil * SLAB), F32)] * 2,
        compiler_params=_cparams(("parallel", "parallel", "arbitrary")),
        name="band_attention_d%d" % dil,
    )(qk_v, qk_v, proj_v)
    return o.reshape(n, SLAB), lse.reshape(n, SLAB)


def _dil_combine_kernel(*refs):
    nbr = len(DIL_PAIRS)
    o_refs, l_refs, out_ref = refs[:nbr], refs[nbr:2 * nbr], refs[2 * nbr]
    lses = [r[...] for r in l_refs]
    m = functools.reduce(jnp.maximum, lses)
    ws = [jnp.exp(l - m) for l in lses]
    den = functools.reduce(jnp.add, ws)
    acc = functools.reduce(jnp.add, [(w / den) * r[...] for w, r in zip(ws, o_refs)])
    out_ref[...] = acc.astype(out_ref.dtype)


def dil_combine(outs, lses, tm):
    n = outs[0].shape[0]
    spec = pl.BlockSpec((tm, SLAB), lambda i: (i, 0))
    return pl.pallas_call(
        _dil_combine_kernel,
        grid=(n // tm,),
        in_specs=[spec] * (2 * len(DIL_PAIRS)),
        out_specs=spec,
        out_shape=jax.ShapeDtypeStruct((n, SLAB), BF16),
        compiler_params=_cparams(("parallel",)),
        name="dil_combine",
    )(*outs, *lses)


MLA_QK = MLA_NOPE + MLA_ROPE


def mla_tables(t):
    ang = _rope_angles(t, MLA_ROPE)
    cos2 = np.concatenate([np.cos(ang), np.cos(ang)], axis=1)
    sin2 = np.concatenate([np.sin(ang), np.sin(ang)], axis=1)
    z = lambda w: np.zeros((t, w), np.float32)
    q_cos = np.concatenate([np.ones((t, MLA_NOPE), np.float32), cos2, z(LANES - MLA_QK)], axis=1)
    q_sin = np.concatenate([z(MLA_NOPE), sin2, z(LANES - MLA_QK)], axis=1)
    k_cos = np.concatenate([cos2, z(LANES - MLA_ROPE)], axis=1)
    k_sin = np.concatenate([-sin2[:, :MLA_ROPE // 2], sin2[:, MLA_ROPE // 2:], z(LANES - MLA_ROPE)], axis=1)
    return tuple(jnp.asarray(a, F32) for a in (q_cos, q_sin, k_cos, k_sin))


def mla_weights(w_uq, w_ukv):
    hq = w_uq.reshape(MLA_Q_RANK, MLA_HEADS, MLA_QK)
    nope, pe = hq[..., :MLA_NOPE], hq[..., MLA_NOPE:]
    pe_rot = jnp.concatenate([-pe[..., MLA_ROPE // 2:], pe[..., :MLA_ROPE // 2]], axis=-1)
    zq = jnp.zeros((MLA_Q_RANK, MLA_HEADS, LANES - MLA_QK), w_uq.dtype)
    w1 = jnp.concatenate([nope, pe, zq], axis=-1).reshape(MLA_Q_RANK, MLA_HEADS * LANES)
    w2 = jnp.concatenate([jnp.zeros_like(nope), pe_rot, zq], axis=-1).reshape(MLA_Q_RANK, MLA_HEADS * LANES)
    hkv = w_ukv.reshape(MLA_KV_RANK, MLA_HEADS, MLA_NOPE + MLA_V)
    k_nope, v = hkv[..., :MLA_NOPE], hkv[..., MLA_NOPE:]
    zk = jnp.zeros((MLA_KV_RANK, MLA_HEADS, LANES - MLA_NOPE), w_ukv.dtype)
    wk = jnp.concatenate([k_nope, zk], axis=-1).reshape(MLA_KV_RANK, MLA_HEADS * LANES)
    wv = v.reshape(MLA_KV_RANK, MLA_HEADS * MLA_V)
    place = np.zeros((LANES, MLA_HEADS * LANES), np.float32)
    for h in range(MLA_HEADS):
        place[np.arange(MLA_ROPE), h * LANES + MLA_NOPE + np.arange(MLA_ROPE)] = 1.0
    return (w1.astype(BF16), w2.astype(BF16), wk.astype(BF16), wv.astype(BF16), jnp.asarray(place, BF16))


def _mla_q_kernel(c_ref, nw_ref, w1_ref, w2_ref, cos_ref, sin_ref, o_ref):
    cn = _rms(c_ref[...].astype(F32), nw_ref[...]).astype(BF16)
    cos = jnp.tile(cos_ref[...], (1, MLA_HEADS))
    sin = jnp.tile(sin_ref[...], (1, MLA_HEADS))
    q = _dot(cn, w1_ref[...]) * cos + _dot(cn, w2_ref[...]) * sin
    o_ref[...] = (q * (MLA_QK ** -0.5)).astype(o_ref.dtype)


def _mla_kv_kernel(c_ref, tail_ref, nw_ref, wk_ref, wv_ref, place_ref, cos_ref, sin_ref, k_ref, v_ref):
    cn = _rms(c_ref[...].astype(F32), nw_ref[...]).astype(BF16)
    kr = tail_ref[...]
    half = MLA_ROPE // 2
    rot = jnp.where(_lane_lt(kr.shape, half), pltpu.roll(kr, LANES - half, 1), pltpu.roll(kr, half, 1))
    k_pe = (kr * cos_ref[...] + rot * sin_ref[...]).astype(BF16)
    k_ref[...] = (_dot(cn, wk_ref[...]) + _dot(k_pe, place_ref[...])).astype(k_ref.dtype)
    v_ref[...] = _dot(cn, wv_ref[...]).astype(v_ref.dtype)


def mla_project(proj, tail, q_norm_w, kv_norm_w, w_uq, w_ukv, b, t, tm):
    n = b * t
    nb = t // tm
    w1, w2, wk, wv, place = mla_weights(w_uq, w_ukv)
    q_cos, q_sin, k_cos, k_sin = mla_tables(t)
    wide = MLA_HEADS * LANES
    full = lambda shape: pl.BlockSpec(shape, lambda i: (0, 0))
    tab = pl.BlockSpec((tm, LANES), lambda i: (i % nb, 0))
    qf = pl.pallas_call(
        _mla_q_kernel,
        grid=(n // tm,),
        in_specs=[pl.BlockSpec((tm, SLAB), lambda i: (i, COL_CQ)), full((1, MLA_Q_RANK)),
                  full((MLA_Q_RANK, wide)), full((MLA_Q_RANK, wide)), tab, tab],
        out_specs=pl.BlockSpec((tm, wide), lambda i: (i, 0)),
        out_shape=jax.ShapeDtypeStruct((n, wide), BF16),
        compiler_params=_cparams(("parallel",)),
        name="mla_q_proj",
    )(proj, q_norm_w.reshape(1, MLA_Q_RANK), w1, w2, q_cos, q_sin)
    kf, vf = pl.pallas_call(
        _mla_kv_kernel,
        grid=(n // tm,),
        in_specs=[pl.BlockSpec((tm, SLAB), lambda i: (i, COL_CKV)),
                  pl.BlockSpec((tm, LANES), lambda i: (i, 0)), full((1, MLA_KV_RANK)),
                  full((MLA_KV_RANK, wide)), full((MLA_KV_RANK, SLAB)), full((LANES, wide)), tab, tab],
        out_specs=[pl.BlockSpec((tm, wide), lambda i: (i, 0)), pl.BlockSpec((tm, SLAB), lambda i: (i, 0))],
        out_shape=[jax.ShapeDtypeStruct((n, wide), BF16), jax.ShapeDtypeStruct((n, SLAB), BF16)],
        compiler_params=_cparams(("parallel",)),
        name="mla_kv_proj",
    )(proj, tail, kv_norm_w.reshape(1, MLA_KV_RANK), wk, wv, place, k_cos, k_sin)
    return qf, kf, vf


def _mla_attn_kernel(q_ref, k_ref, v_ref, o_ref, *, t, tk):
    tq = q_ref.shape[0]
    first = _lane_lt((tq, HEAD_PAIR), HEAD_DIM)
    outs = []
    for hh in range(2):
        q = q_ref[:, hh * LANES:(hh + 1) * LANES]

        def body(c, carry):
            m, l, acc = carry
            k0 = pl.multiple_of(c * tk, tk)
            s = _dot_nt(q, k_ref[pl.ds(k0, tk), hh * LANES:(hh + 1) * LANES])
            m_new = jnp.maximum(m, jnp.max(s, axis=-1, keepdims=True))
            alpha = jnp.exp(m - m_new)
            p = jnp.exp(s - m_new)
            l = alpha * l + jnp.sum(p, axis=-1, keepdims=True)
            acc = alpha * acc + _dot(p.astype(BF16), v_ref[pl.ds(k0, tk), :])
            return m_new, l, acc

        init = (jnp.full((tq, 1), -jnp.inf, F32), jnp.zeros((tq, 1), F32), jnp.zeros((tq, HEAD_PAIR), F32))
        m, l, acc = lax.fori_loop(0, t // tk, body, init)
        outs.append(acc / l)
    o_ref[...] = jnp.where(first, outs[0], outs[1]).astype(o_ref.dtype)


def mla_attention(qf, kf, vf, b, t, tq, tk):
    n = b * t
    nq = t // tq
    return pl.pallas_call(
        functools.partial(_mla_attn_kernel, t=t, tk=tk),
        grid=(b, N_PAIRS, nq),
        in_specs=[
            pl.BlockSpec((tq, 2 * LANES), lambda i, p, s: (i * nq + s, p)),
            pl.BlockSpec((t, 2 * LANES), lambda i, p, s: (i, p)),
            pl.BlockSpec((t, HEAD_PAIR), lambda i, p, s: (i, p)),
        ],
        out_specs=pl.BlockSpec((tq, HEAD_PAIR), lambda i, p, s: (i * nq + s, p)),
        out_shape=jax.ShapeDtypeStruct((n, SLAB), BF16),
        compiler_params=_cparams(("parallel", "parallel", "arbitrary")),
        name="mla_attention",
    )(qf, kf, vf)


def _in_proj_columns():
    sizes = (SSM_INNER, SSM_CONV_CH, 2 * SSM_HEADS, SLAB, SLAB, SLAB, MLA_Q_RANK, MLA_KV_RANK, MLA_ROPE,
             SLAB, SLAB, SLAB)
    off = np.concatenate([[0], np.cumsum(sizes)])
    seg = lambda k: np.arange(off[k], off[k + 1])
    main = np.concatenate([seg(k) for k in (0, 1, 3, 4, 5, 6, 7, 9, 10, 11)])
    tail = np.concatenate([seg(8), seg(2)])
    return main, tail


def mixers(proj, tail, p, l, b, t):
    xbc = conv_silu(proj, p["conv_w"][l], p["conv_b"][l], b, t)
    y_f = ssd_scan(xbc, tail, p["dt_bias"][l], p["a_log"][l], b, t, 0)
    y_b = ssd_scan(xbc, tail, p["dt_bias"][l], p["a_log"][l], b, t, 1)
    y_ssm = ssd_combine(y_f, y_b, xbc, proj, p["d_skip"][l], p["ssm_norm_w"][l], 1024)

    y_na = na_attention(proj, p["na_rpb"][l], b, t)

    qf, kf, vf = mla_project(proj, tail, p["mla_q_norm_w"][l], p["mla_kv_norm_w"][l],
                             p["mla_w_uq"][l], p["mla_w_ukv"][l], b, t, 512)
    y_mla = mla_attention(qf, kf, vf, b, t, 512, 512)

    qk = rope_qk(proj, b, t, 1024)
    outs, lses = zip(*[band_attention(qk, proj, b, t, w, d) for w, d in DIL_PAIRS])
    y_dil = dil_combine(outs, lses, 1024)
    return y_ssm, y_na, y_mla, y_dil


def kernel(x, attn_norm_w, w_in, conv_w, conv_b, a_log, dt_bias, d_skip, ssm_norm_w, na_rpb,
           mla_q_norm_w, mla_kv_norm_w, mla_w_uq, mla_w_ukv, w_o, ffn_norm_w, ffn_w_gate, ffn_w_up,
           ffn_w_down, router_w, exp_w_gate, exp_w_up, exp_w_down, final_norm_w):
    b, t, _ = x.shape
    n = b * t
    depth = w_in.shape[0]
    p = dict(conv_w=conv_w, conv_b=conv_b, a_log=a_log, dt_bias=dt_bias, d_skip=d_skip,
             ssm_norm_w=ssm_norm_w, na_rpb=na_rpb, mla_q_norm_w=mla_q_norm_w,
             mla_kv_norm_w=mla_kv_norm_w, mla_w_uq=mla_w_uq, mla_w_ukv=mla_w_ukv)
    main_cols, tail_cols = _in_proj_columns()
    x = x.reshape(n, D_MODEL)
    moe_tm = 512
    normed = False
    for l in range(depth):
        w_main = w_in[l][:, main_cols].astype(BF16)
        w_tail = jnp.zeros((D_MODEL, LANES), BF16).at[:, :tail_cols.size].set(
            w_in[l][:, tail_cols].astype(BF16))
        proj = norm_matmul(x, 0, D_MODEL, attn_norm_w[l], w_main, BF16, 1024, 512)
        tail = norm_matmul(x, 0, D_MODEL, attn_norm_w[l], w_tail, F32, 1024, LANES)
        mix = mixers(proj, tail, p, l, b, t)
        x = out_proj(mix, w_o[l].astype(BF16), x, 1024, 1024)
        j = l // 2
        if l % 2 == 0:
            x = ffn_dense(x, ffn_norm_w[l], ffn_w_gate[j].astype(BF16), ffn_w_up[j].astype(BF16),
                          ffn_w_down[j].astype(BF16), 512, 512)
        else:
            top_i, gates = moe_router(x, ffn_norm_w[l], router_w[j], 512)
            src, pos, tile_expert, tile_valid = moe_plan(top_i, moe_tm)
            y = moe_ffn(x, ffn_norm_w[l], exp_w_gate[j].astype(BF16), exp_w_up[j].astype(BF16),
                        exp_w_down[j].astype(BF16), src, tile_expert, tile_valid, moe_tm, 512)
            last = l == depth - 1
            x = moe_combine(x, gates, y, pos, final_norm_w if last else None, 256)
            normed = last
    if not normed:
        x = rmsnorm_rows(x, final_norm_w, 1024)
    return x.reshape(b, t, D_MODEL)
```

```python
import functools
import math

import numpy as np
import jax
import jax.numpy as jnp
from jax import lax
from jax.experimental import pallas as pl
from jax.experimental.pallas import tpu as pltpu

F32 = jnp.float32
BF16 = jnp.bfloat16

D_MODEL = 2048
GRID_W = 64
HEAD_DIM = 64
ROPE_THETA = 10000.0
NORM_EPS = 1e-6
NEG_INF = -1e30

SSM_HEADS = 8
SSM_HEAD_DIM = 64
SSM_INNER = SSM_HEADS * SSM_HEAD_DIM
SSM_GROUPS = 2
SSM_STATE = 128
SSM_CONV = 5
SSM_CHUNK = 128
SSM_CONV_CH = SSM_INNER + 2 * SSM_GROUPS * SSM_STATE

NA_HEADS = 8
NA_WIN_ROWS = 8
NA_WIN_COLS = 16
NA_COL_BLOCK = 16
NA_KEY_COLS = 32
NA_ROWS_PER_STEP = 4

MLA_HEADS = 8
MLA_Q_RANK = 512
MLA_KV_RANK = 512
MLA_NOPE = 64
MLA_ROPE = 32
MLA_V = 64

DIL_HEADS = 8
DIL_PAIRS = ((128, 1), (512, 4), (2048, 16))
DIL_QBLOCK = 128

N_EXPERTS = 8
TOP_K = 2

LANES = 128
HEAD_PAIR = 2 * HEAD_DIM
N_PAIRS = 4
SLAB = 512

COL_Z, COL_XBC, COL_NAQ, COL_NAK, COL_NAV, COL_CQ, COL_CKV, COL_DLQ, COL_DLK, COL_DLV = (
    0, 1, 3, 4, 5, 6, 7, 8, 9, 10)
PROJ_MAIN = 11 * SLAB
TAIL_DT = 32

VMEM_LIMIT = 56 * 1024 * 1024


def _cparams(sem, vmem=VMEM_LIMIT):
    return pltpu.CompilerParams(dimension_semantics=sem, vmem_limit_bytes=vmem)


def _lane_lt(shape, bound, period=None):
    lane = lax.broadcasted_iota(jnp.int32, shape, len(shape) - 1)
    if period is not None:
        lane = lane % period
    return lane < bound


def _rms(x, w):
    ms = jnp.mean(x * x, axis=-1, keepdims=True)
    return x * lax.rsqrt(ms + NORM_EPS) * w


def _dot(a, b):
    return jnp.dot(a, b, preferred_element_type=F32)


def _dot_nt(a, b):
    return lax.dot_general(a, b, (((1,), (1,)), ((), ())), preferred_element_type=F32)


def _dot_tn(a, b):
    return lax.dot_general(a, b, (((0,), (0,)), ((), ())), preferred_element_type=F32)


def _cast_kernel(x_ref, o_ref):
    o_ref[...] = x_ref[...].astype(o_ref.dtype)


def cast_bf16(w, tr):
    shape = w.shape
    w2 = w.reshape(-1, shape[-1])
    r, c = w2.shape
    out = pl.pallas_call(
        _cast_kernel,
        grid=(r // tr,),
        in_specs=[pl.BlockSpec((tr, c), lambda i: (i, 0))],
        out_specs=pl.BlockSpec((tr, c), lambda i: (i, 0)),
        out_shape=jax.ShapeDtypeStruct((r, c), BF16),
        compiler_params=_cparams(("parallel",)),
        name="cast_bf16",
    )(w2)
    return out.reshape(shape)


def _norm_matmul_kernel(x_ref, nw_ref, w_ref, o_ref, h_ref):
    @pl.when(pl.program_id(1) == 0)
    def _():
        h_ref[...] = _rms(x_ref[...].astype(F32), nw_ref[...]).astype(BF16)

    o_ref[...] = _dot(h_ref[...], w_ref[...]).astype(o_ref.dtype)


def norm_matmul(x, xcol, k, nw, w, out_dtype, tm, tn):
    n = x.shape[0]
    nout = w.shape[1]
    return pl.pallas_call(
        _norm_matmul_kernel,
        grid=(n // tm, nout // tn),
        in_specs=[
            pl.BlockSpec((tm, k), lambda i, j: (i, xcol)),
            pl.BlockSpec((1, k), lambda i, j: (0, 0)),
            pl.BlockSpec((k, tn), lambda i, j: (0, j)),
        ],
        out_specs=pl.BlockSpec((tm, tn), lambda i, j: (i, j)),
        out_shape=jax.ShapeDtypeStruct((n, nout), out_dtype),
        scratch_shapes=[pltpu.VMEM((tm, k), BF16)],
        compiler_params=_cparams(("parallel", "arbitrary")),
        name="norm_matmul",
    )(x, nw.reshape(1, k), w)


def _out_proj_kernel(a0_ref, a1_ref, a2_ref, a3_ref, w_ref, r_ref, o_ref):
    acc = r_ref[...]
    for s, a_ref in enumerate((a0_ref, a1_ref, a2_ref, a3_ref)):
        acc = acc + _dot(a_ref[...], w_ref[s * SLAB:(s + 1) * SLAB, :])
    o_ref[...] = acc


def out_proj(mix, w, layer, res, tm, tn):
    n = res.shape[0]
    return pl.pallas_call(
        _out_proj_kernel,
        grid=(n // tm, D_MODEL // tn),
        in_specs=[pl.BlockSpec((tm, SLAB), lambda i, j: (i, 0))] * 4 + [
            pl.BlockSpec((None, 4 * SLAB, tn), lambda i, j: (layer, 0, j)),
            pl.BlockSpec((tm, tn), lambda i, j: (i, j)),
        ],
        out_specs=pl.BlockSpec((tm, tn), lambda i, j: (i, j)),
        out_shape=jax.ShapeDtypeStruct((n, D_MODEL), F32),
        compiler_params=_cparams(("parallel", "arbitrary")),
        name="out_proj",
    )(*mix, w, res)


def _ffn_kernel(x_ref, nw_ref, wg_ref, wu_ref, wd_ref, o_ref, h_ref):
    @pl.when(pl.program_id(1) == 0)
    def _():
        x = x_ref[...]
        h_ref[...] = _rms(x, nw_ref[...]).astype(BF16)
        o_ref[...] = x

    h = h_ref[...]
    g = _dot(h, wg_ref[...])
    u = _dot(h, wu_ref[...])
    a = (g * jax.nn.sigmoid(g) * u).astype(BF16)
    o_ref[...] += _dot(a, wd_ref[...])


def ffn_dense(x, nw, wg, wu, wd, layer, tm, tf):
    n = x.shape[0]
    d_ff = wg.shape[-1]
    return pl.pallas_call(
        _ffn_kernel,
        grid=(n // tm, d_ff // tf),
        in_specs=[
            pl.BlockSpec((tm, D_MODEL), lambda i, j: (i, 0)),
            pl.BlockSpec((1, D_MODEL), lambda i, j: (0, 0)),
            pl.BlockSpec((None, D_MODEL, tf), lambda i, j: (layer, 0, j)),
            pl.BlockSpec((None, D_MODEL, tf), lambda i, j: (layer, 0, j)),
            pl.BlockSpec((None, tf, D_MODEL), lambda i, j: (layer, j, 0)),
        ],
        out_specs=pl.BlockSpec((tm, D_MODEL), lambda i, j: (i, 0)),
        out_shape=jax.ShapeDtypeStruct((n, D_MODEL), F32),
        scratch_shapes=[pltpu.VMEM((tm, D_MODEL), BF16)],
        compiler_params=_cparams(("parallel", "arbitrary")),
        name="ffn_dense",
    )(x, nw.reshape(1, D_MODEL), wg, wu, wd)


def _router_kernel(x_ref, nw_ref, rw_ref, idx_ref, gate_ref):
    h = _rms(x_ref[...], nw_ref[...])
    logits = jnp.dot(h, rw_ref[...], preferred_element_type=F32, precision=lax.Precision.HIGHEST)
    lane = lax.broadcasted_iota(jnp.int32, logits.shape, 1)
    logits = jnp.where(lane < N_EXPERTS, logits, -jnp.inf)
    m1 = jnp.max(logits, axis=-1, keepdims=True)
    i1 = jnp.min(jnp.where(logits == m1, lane, LANES), axis=-1, keepdims=True)
    rest = jnp.where(lane == i1, -jnp.inf, logits)
    m2 = jnp.max(rest, axis=-1, keepdims=True)
    i2 = jnp.min(jnp.where(rest == m2, lane, LANES), axis=-1, keepdims=True)
    e2 = jnp.exp(m2 - m1)
    g1 = 1.0 / (1.0 + e2)
    g2 = e2 / (1.0 + e2)
    idx_ref[...] = jnp.where(lane == 0, i1, i2)[:, :TOP_K]
    gate_ref[...] = jnp.where(lane == 0, g1, g2)[:, :TOP_K]


def moe_router(x, nw, router_w, tm):
    n = x.shape[0]
    rw = jnp.zeros((D_MODEL, LANES), F32).at[:, :N_EXPERTS].set(router_w)
    return pl.pallas_call(
        _router_kernel,
        grid=(n // tm,),
        in_specs=[
            pl.BlockSpec((tm, D_MODEL), lambda i: (i, 0)),
            pl.BlockSpec((1, D_MODEL), lambda i: (0, 0)),
            pl.BlockSpec((D_MODEL, LANES), lambda i: (0, 0)),
        ],
        out_specs=[pl.BlockSpec((tm, TOP_K), lambda i: (i, 0)),
                   pl.BlockSpec((tm, TOP_K), lambda i: (i, 0))],
        out_shape=[jax.ShapeDtypeStruct((n, TOP_K), jnp.int32),
                   jax.ShapeDtypeStruct((n, TOP_K), F32)],
        compiler_params=_cparams(("parallel",)),
        name="moe_router",
    )(x, nw.reshape(1, D_MODEL), rw)


def moe_plan(top_i, tm):
    n = top_i.shape[0]
    flat_e = top_i.reshape(-1)
    onehot = (flat_e[:, None] == jnp.arange(N_EXPERTS, dtype=jnp.int32)[None, :]).astype(jnp.int32)
    csum = jnp.cumsum(onehot, axis=0)
    counts = csum[-1]
    rank = jnp.sum(onehot * csum, axis=1) - 1
    padded = ((counts + tm - 1) // tm) * tm
    pend = jnp.cumsum(padded)
    pstart = pend - padded
    pos = pstart[flat_e] + rank
    n_slots = n * TOP_K + N_EXPERTS * tm
    n_tiles = n_slots // tm
    src = jnp.zeros((n_slots,), jnp.int32).at[pos].set(jnp.arange(n * TOP_K, dtype=jnp.int32) // TOP_K)
    tile_start = jnp.arange(n_tiles, dtype=jnp.int32) * tm
    tile_expert = jnp.sum((tile_start[:, None] >= pend[None, :]).astype(jnp.int32), axis=1)
    tile_valid = (tile_start < pend[-1]).astype(jnp.int32)
    last_valid = jnp.maximum(pend[-1] // tm - 1, 0)
    tile_expert = jnp.where(tile_valid == 1, tile_expert, tile_expert[last_valid]).astype(jnp.int32)
    return src, pos.reshape(n, TOP_K).astype(jnp.int32), tile_expert, tile_valid


def _moe_ffn_kernel(te_ref, tv_ref, src_ref, nsrc_ref, x_hbm, nw_ref, wg_ref, wu_ref, wd_ref, y_ref,
                    xbuf, h_ref, sem, *, tm):
    i = pl.program_id(0)
    j = pl.program_id(1)
    n_tiles = pl.num_programs(0)
    valid = tv_ref[i] == 1

    def row_copy(idx_ref, r):
        tok = idx_ref[0, 0, r]
        return pltpu.make_async_copy(x_hbm.at[pl.ds(tok, 1)], xbuf.at[pl.ds(r, 1)], sem)

    def gather(idx_ref):
        def start(r, c):
            row_copy(idx_ref, r).start()
            return c

        lax.fori_loop(0, tm, start, 0)

    @pl.when(jnp.logical_and(j == 0, jnp.logical_not(valid)))
    def _():
        y_ref[...] = jnp.zeros_like(y_ref)

    @pl.when(jnp.logical_and(j == 0, valid))
    def _():
        @pl.when(i == 0)
        def _():
            gather(src_ref)

        def wait(r, c):
            row_copy(src_ref, r).wait()
            return c

        lax.fori_loop(0, tm, wait, 0)
        h_ref[...] = _rms(xbuf[...], nw_ref[...]).astype(BF16)

        nxt = jnp.minimum(i + 1, n_tiles - 1)

        @pl.when(jnp.logical_and(i + 1 < n_tiles, tv_ref[nxt] == 1))
        def _():
            gather(nsrc_ref)

    @pl.when(valid)
    def _():
        h = h_ref[...]
        g = _dot(h, wg_ref[...])
        u = _dot(h, wu_ref[...])
        a = (g * jax.nn.sigmoid(g) * u).astype(BF16)
        d = _dot(a, wd_ref[...])

        @pl.when(j == 0)
        def _():
            y_ref[...] = d

        @pl.when(j != 0)
        def _():
            y_ref[...] += d


def moe_ffn(x, nw, wg, wu, wd, layer, src, tile_expert, tile_valid, tm, tf):
    n_slots = src.shape[0]
    n_tiles = n_slots // tm
    d_ff = wg.shape[-1]
    nf = d_ff // tf

    def wcol(i, j, te_ref, tv_ref):
        return (layer, te_ref[i], 0, jnp.where(tv_ref[i] == 1, j, nf - 1))

    def wrow(i, j, te_ref, tv_ref):
        return (layer, te_ref[i], jnp.where(tv_ref[i] == 1, j, nf - 1), 0)

    grid_spec = pltpu.PrefetchScalarGridSpec(
        num_scalar_prefetch=2,
        grid=(n_tiles, nf),
        in_specs=[
            pl.BlockSpec((1, 1, tm), lambda i, j, *_: (i, 0, 0), memory_space=pltpu.SMEM),
            pl.BlockSpec((1, 1, tm), lambda i, j, *_: (jnp.minimum(i + 1, n_tiles - 1), 0, 0),
                         memory_space=pltpu.SMEM),
            pl.BlockSpec(memory_space=pl.ANY),
            pl.BlockSpec((1, D_MODEL), lambda i, j, *_: (0, 0)),
            pl.BlockSpec((None, None, D_MODEL, tf), wcol),
            pl.BlockSpec((None, None, D_MODEL, tf), wcol),
            pl.BlockSpec((None, None, tf, D_MODEL), wrow),
        ],
        out_specs=pl.BlockSpec((tm, D_MODEL), lambda i, j, *_: (i, 0)),
        scratch_shapes=[pltpu.VMEM((tm, D_MODEL), F32), pltpu.VMEM((tm, D_MODEL), BF16),
                        pltpu.SemaphoreType.DMA],
    )
    src3 = src.reshape(n_tiles, 1, tm)
    return pl.pallas_call(
        functools.partial(_moe_ffn_kernel, tm=tm),
        grid_spec=grid_spec,
        out_shape=jax.ShapeDtypeStruct((n_slots, D_MODEL), F32),
        compiler_params=_cparams(("arbitrary", "arbitrary")),
        name="moe_ffn",
    )(tile_expert, tile_valid, src3, src3, x, nw.reshape(1, D_MODEL), wg, wu, wd)


def _moe_combine_kernel(pos_ref, npos_ref, x_ref, gate_ref, y_hbm, fw_ref, o_ref, ybuf, sem, *, tm, final_norm):
    i = pl.program_id(0)
    n_tiles = pl.num_programs(0)
    cur = i % 2

    def row_copy(idx_ref, buf, r, k):
        slot = idx_ref[0, 0, r * TOP_K + k]
        return pltpu.make_async_copy(y_hbm.at[pl.ds(slot, 1)], ybuf.at[buf, k, pl.ds(r, 1)], sem.at[buf])

    def gather(idx_ref, buf):
        def start(r, c):
            for k in range(TOP_K):
                row_copy(idx_ref, buf, r, k).start()
            return c

        lax.fori_loop(0, tm, start, 0)

    @pl.when(i == 0)
    def _():
        gather(pos_ref, 0)

    @pl.when(i + 1 < n_tiles)
    def _():
        gather(npos_ref, 1 - cur)

    def wait(r, c):
        for k in range(TOP_K):
            row_copy(pos_ref, cur, r, k).wait()
        return c

    lax.fori_loop(0, tm, wait, 0)
    gates = gate_ref[...]
    out = x_ref[...]
    for k in range(TOP_K):
        out = out + gates[:, k:k + 1] * ybuf[cur, k]
    if final_norm:
        out = _rms(out, fw_ref[...])
    o_ref[...] = out


def moe_combine(x, gates, y, pos, final_w, tm):
    n = x.shape[0]
    final_norm = final_w is not None
    fw = final_w if final_norm else jnp.ones((D_MODEL,), F32)
    n_tiles = n // tm
    pos3 = pos.reshape(n_tiles, 1, tm * TOP_K)
    return pl.pallas_call(
        functools.partial(_moe_combine_kernel, tm=tm, final_norm=final_norm),
        grid=(n_tiles,),
        in_specs=[
            pl.BlockSpec((1, 1, tm * TOP_K), lambda i: (i, 0, 0), memory_space=pltpu.SMEM),
            pl.BlockSpec((1, 1, tm * TOP_K), lambda i: (jnp.minimum(i + 1, n_tiles - 1), 0, 0),
                         memory_space=pltpu.SMEM),
            pl.BlockSpec((tm, D_MODEL), lambda i: (i, 0)),
            pl.BlockSpec((tm, TOP_K), lambda i: (i, 0)),
            pl.BlockSpec(memory_space=pl.ANY),
            pl.BlockSpec((1, D_MODEL), lambda i: (0, 0)),
        ],
        out_specs=pl.BlockSpec((tm, D_MODEL), lambda i: (i, 0)),
        out_shape=jax.ShapeDtypeStruct((n, D_MODEL), F32),
        scratch_shapes=[pltpu.VMEM((2, TOP_K, tm, D_MODEL), F32), pltpu.SemaphoreType.DMA((2,))],
        compiler_params=_cparams(("arbitrary",)),
        name="moe_combine",
    )(pos3, pos3, x, gates, y, fw.reshape(1, D_MODEL))


def _rmsnorm_kernel(x_ref, w_ref, o_ref):
    o_ref[...] = _rms(x_ref[...], w_ref[...])


def rmsnorm_rows(x, w, tm):
    n = x.shape[0]
    return pl.pallas_call(
        _rmsnorm_kernel,
        grid=(n // tm,),
        in_specs=[pl.BlockSpec((tm, D_MODEL), lambda i: (i, 0)),
                  pl.BlockSpec((1, D_MODEL), lambda i: (0, 0))],
        out_specs=pl.BlockSpec((tm, D_MODEL), lambda i: (i, 0)),
        out_shape=jax.ShapeDtypeStruct((n, D_MODEL), F32),
        compiler_params=_cparams(("parallel",)),
        name="final_norm",
    )(x, w.reshape(1, D_MODEL))


CONV_PAD = 8


def _conv_kernel(x_ref, w_ref, b_ref, o_ref, pad_ref, *, t):
    half = SSM_CONV // 2
    zeros = jnp.zeros((CONV_PAD, pad_ref.shape[1]), F32)
    pad_ref[0:CONV_PAD, :] = zeros
    pad_ref[CONV_PAD + t:CONV_PAD + t + CONV_PAD, :] = zeros
    pad_ref[CONV_PAD:CONV_PAD + t, :] = x_ref[...].astype(F32)
    acc = jnp.zeros(o_ref.shape, F32) + b_ref[...]
    for k in range(SSM_CONV):
        acc = acc + pad_ref[pl.ds(CONV_PAD - half + k, t), :] * w_ref[k:k + 1, :]
    o_ref[...] = (acc * jax.nn.sigmoid(acc)).astype(o_ref.dtype)


def conv_silu(proj, conv_w, conv_b, b, t):
    tc = 256
    nblk = SSM_CONV_CH // tc
    col0 = COL_XBC * SLAB // tc
    return pl.pallas_call(
        functools.partial(_conv_kernel, t=t),
        grid=(b, nblk),
        in_specs=[
            pl.BlockSpec((t, tc), lambda i, j: (i, col0 + j)),
            pl.BlockSpec((SSM_CONV, tc), lambda i, j: (0, j)),
            pl.BlockSpec((1, tc), lambda i, j: (0, j)),
        ],
        out_specs=pl.BlockSpec((t, tc), lambda i, j: (i, j)),
        out_shape=jax.ShapeDtypeStruct((b * t, SSM_CONV_CH), BF16),
        scratch_shapes=[pltpu.VMEM((t + 2 * CONV_PAD, tc), F32)],
        compiler_params=_cparams(("parallel", "parallel")),
        name="conv_silu",
    )(proj, conv_w, conv_b.reshape(1, SSM_CONV_CH))


def _ssd_kernel(xbc_ref, tail_ref, bias_ref, alog_ref, tri_ref, y_ref, state_ref, *, direction):
    q = SSM_CHUNK

    @pl.when(pl.program_id(1) == 0)
    def _():
        state_ref[...] = jnp.zeros_like(state_ref)

    dt = jax.nn.softplus(tail_ref[...] + bias_ref[...])
    da = dt * (-jnp.exp(alog_ref[...]))
    cs = jnp.dot(tri_ref[...], da, preferred_element_type=F32, precision=lax.Precision.HIGHEST)
    total = cs[q - 1:q, :]
    if direction == 0:
        e_out = cs
        e_in = total - cs
        e_seg = cs
    else:
        ex = cs - da
        e_out = total - ex
        e_in = ex
        e_seg = -ex
    e_seg_t = jnp.transpose(e_seg)
    dec_out = jnp.exp(e_out)
    dec_in_dt = jnp.exp(e_in) * dt
    dt_t = jnp.transpose(dt)
    dec_tot = jnp.exp(total)

    row = lax.broadcasted_iota(jnp.int32, (q, q), 0)
    col = lax.broadcasted_iota(jnp.int32, (q, q), 1)
    keep = (row >= col) if direction == 0 else (col >= row)
    first_half = _lane_lt((q, HEAD_PAIR), HEAD_DIM)

    xs = xbc_ref[:, 0:SSM_INNER]
    heads_per_group = SSM_HEADS // SSM_GROUPS
    for g in range(SSM_GROUPS):
        bm = xbc_ref[:, SSM_INNER + g * SSM_STATE:SSM_INNER + (g + 1) * SSM_STATE]
        cm = xbc_ref[:, SSM_INNER + (SSM_GROUPS + g) * SSM_STATE:SSM_INNER + (SSM_GROUPS + g + 1) * SSM_STATE]
        cb = _dot_nt(cm, bm)
        bm_f = bm.astype(F32)
        cm_f = cm.astype(F32)
        for pp in range(heads_per_group // 2):
            pair = g * (heads_per_group // 2) + pp
            xs_pair = xs[:, pair * HEAD_PAIR:(pair + 1) * HEAD_PAIR]
            st = state_ref[pair]
            ys, sts, decs = [], [], []
            for hh in range(2):
                lane = TAIL_DT + direction * SSM_HEADS + pair * 2 + hh
                seg = e_seg[:, lane:lane + 1] - e_seg_t[lane:lane + 1, :]
                lmat = jnp.where(keep, jnp.exp(seg), 0.0)
                w = (cb * lmat * dt_t[lane:lane + 1, :]).astype(BF16)
                y = _dot(w, xs_pair)
                y = y + _dot((cm_f * dec_out[:, lane:lane + 1]).astype(BF16), st.astype(BF16))
                ys.append(y)
                sts.append(_dot_tn((bm_f * dec_in_dt[:, lane:lane + 1]).astype(BF16), xs_pair))
                decs.append(dec_tot[:, lane:lane + 1])
            y_ref[:, pair * HEAD_PAIR:(pair + 1) * HEAD_PAIR] = jnp.where(first_half, ys[0], ys[1])
            first_half_s = _lane_lt(st.shape, HEAD_DIM)
            state_ref[pair] = (st * jnp.where(first_half_s, decs[0], decs[1])
                               + jnp.where(first_half_s, sts[0], sts[1]))


def ssd_scan(xbc, tail, dt_bias, a_log, b, t, direction):
    q = SSM_CHUNK
    nc = t // q
    bias_row = jnp.zeros((1, LANES), F32).at[0, TAIL_DT:TAIL_DT + 2 * SSM_HEADS].set(dt_bias.reshape(-1))
    alog_row = jnp.zeros((1, LANES), F32).at[0, TAIL_DT:TAIL_DT + 2 * SSM_HEADS].set(a_log.reshape(-1))
    tri = jnp.asarray(np.tril(np.ones((q, q), np.float32)))

    def chunk(i, c):
        return i * nc + (c if direction == 0 else nc - 1 - c)

    return pl.pallas_call(
        functools.partial(_ssd_kernel, direction=direction),
        grid=(b, nc),
        in_specs=[
            pl.BlockSpec((q, SSM_CONV_CH), lambda i, c: (chunk(i, c), 0)),
            pl.BlockSpec((q, LANES), lambda i, c: (chunk(i, c), 0)),
            pl.BlockSpec((1, LANES), lambda i, c: (0, 0)),
            pl.BlockSpec((1, LANES), lambda i, c: (0, 0)),
            pl.BlockSpec((q, q), lambda i, c: (0, 0)),
        ],
        out_specs=pl.BlockSpec((q, SSM_INNER), lambda i, c: (chunk(i, c), 0)),
        out_shape=jax.ShapeDtypeStruct((b * t, SSM_INNER), F32),
        scratch_shapes=[pltpu.VMEM((N_PAIRS, SSM_STATE, HEAD_PAIR), F32)],
        compiler_params=_cparams(("parallel", "arbitrary")),
        name="ssd_scan_%s" % ("fwd" if direction == 0 else "bwd"),
    )(xbc, tail, bias_row, alog_row, tri)


def _ssd_combine_kernel(yf_ref, yb_ref, xs_ref, z_ref, d_ref, nw_ref, o_ref):
    y = yf_ref[...] + yb_ref[...] + xs_ref[...].astype(F32) * d_ref[...]
    z = z_ref[...].astype(F32)
    o_ref[...] = _rms(y * (z * jax.nn.sigmoid(z)), nw_ref[...]).astype(o_ref.dtype)


def ssd_combine(y_f, y_b, xbc, proj, d_skip, norm_w, tm):
    n = y_f.shape[0]
    d_row = jnp.repeat(d_skip, SSM_HEAD_DIM).reshape(1, SSM_INNER)
    row = lambda i: (i, 0)
    return pl.pallas_call(
        _ssd_combine_kernel,
        grid=(n // tm,),
        in_specs=[
            pl.BlockSpec((tm, SSM_INNER), row),
            pl.BlockSpec((tm, SSM_INNER), row),
            pl.BlockSpec((tm, SSM_INNER), row),
            pl.BlockSpec((tm, SLAB), lambda i: (i, COL_Z)),
            pl.BlockSpec((1, SSM_INNER), lambda i: (0, 0)),
            pl.BlockSpec((1, SSM_INNER), lambda i: (0, 0)),
        ],
        out_specs=pl.BlockSpec((tm, SSM_INNER), row),
        out_shape=jax.ShapeDtypeStruct((n, SSM_INNER), BF16),
        compiler_params=_cparams(("parallel",)),
        name="ssd_combine",
    )(y_f, y_b, xbc, proj, d_row, norm_w.reshape(1, SSM_INNER))


def _pair_softmax_attend(q2, k2, v2, bias_fn):
    tq = q2.shape[0]
    first_q = _lane_lt(q2.shape, HEAD_DIM)
    zero = jnp.zeros_like(q2)
    qs = jnp.concatenate([jnp.where(first_q, q2, zero), jnp.where(first_q, zero, q2)], axis=0)
    s = bias_fn(_dot_nt(qs, k2))
    m = jnp.max(s, axis=-1, keepdims=True)
    p = jnp.exp(s - m)
    l = jnp.sum(p, axis=-1, keepdims=True)
    o = _dot(p.astype(BF16), v2) / l
    lse = m + jnp.log(l)
    first_o = _lane_lt((tq, HEAD_PAIR), HEAD_DIM)
    return jnp.where(first_o, o[:tq], o[tq:]), jnp.where(first_o, lse[:tq], lse[tq:])


def na_bias_tables(rpb, rows):
    kr = min(NA_WIN_ROWS, rows)
    qc = np.arange(GRID_W)
    kc = np.arange(GRID_W)
    q_start = np.clip(qc - NA_WIN_COLS // 2, 0, GRID_W - NA_WIN_COLS)
    col_in = (kc[None, :] >= q_start[:, None]) & (kc[None, :] < q_start[:, None] + NA_WIN_COLS)
    col_off = np.clip(kc[None, :] - qc[:, None] + NA_WIN_COLS - 1, 0, 2 * NA_WIN_COLS - 2)

    def table(r):
        row_start = int(np.clip(r - kr // 2, 0, rows - kr))
        row_off = row_start + np.arange(kr) - r + NA_WIN_ROWS - 1
        bias = rpb[:, row_off[None, :, None], col_off[:, None, :]]
        bias = jnp.where(jnp.asarray(col_in)[None, :, None, :], bias, NEG_INF)
        return bias.reshape(N_PAIRS, 2 * GRID_W, kr * GRID_W)

    rs = NA_ROWS_PER_STEP
    lo = [table(r) for r in range(rs)]
    mid = [table(min(rs, rows - 1))] * rs
    hi = [table(r) for r in range(rows - rs, rows)]
    return jnp.stack([jnp.stack(lo), jnp.stack(mid), jnp.stack(hi)])


def _na_kernel(q_ref, k_ref, v_ref, bias_ref, o_ref, *, rows, kr):
    step = pl.program_id(1)
    rs = NA_ROWS_PER_STEP
    for rr in range(rs):
        r = step * rs + rr
        row_start = jnp.clip(r - kr // 2, 0, rows - kr)
        k0 = pl.multiple_of(row_start * GRID_W, GRID_W)
        for pair in range(N_PAIRS):
            cols = slice(pair * HEAD_PAIR, (pair + 1) * HEAD_PAIR)
            q2 = q_ref[rr * GRID_W:(rr + 1) * GRID_W, cols] * jnp.asarray(HEAD_DIM ** -0.5, BF16)
            k2 = k_ref[pl.ds(k0, kr * GRID_W), cols]
            v2 = v_ref[pl.ds(k0, kr * GRID_W), cols]
            o, _ = _pair_softmax_attend(q2, k2, v2, lambda s: s + bias_ref[0, rr, pair])
            o_ref[rr * GRID_W:(rr + 1) * GRID_W, cols] = o.astype(o_ref.dtype)


def na_attention(proj, rpb, b, t):
    rows = t // GRID_W
    kr = min(NA_WIN_ROWS, rows)
    rs = NA_ROWS_PER_STEP
    nsteps = rows // rs
    bias = na_bias_tables(rpb, rows)

    def kind(i, s):
        return jnp.where(s == 0, 0, jnp.where(s == nsteps - 1, 2, 1))

    return pl.pallas_call(
        functools.partial(_na_kernel, rows=rows, kr=kr),
        grid=(b, nsteps),
        in_specs=[
            pl.BlockSpec((rs * GRID_W, SLAB), lambda i, s: (i * nsteps + s, COL_NAQ)),
            pl.BlockSpec((t, SLAB), lambda i, s: (i, COL_NAK)),
            pl.BlockSpec((t, SLAB), lambda i, s: (i, COL_NAV)),
            pl.BlockSpec((1, rs, N_PAIRS, 2 * GRID_W, kr * GRID_W), lambda i, s: (kind(i, s), 0, 0, 0, 0)),
        ],
        out_specs=pl.BlockSpec((rs * GRID_W, SLAB), lambda i, s: (i * nsteps + s, 0)),
        out_shape=jax.ShapeDtypeStruct((b * t, SLAB), BF16),
        compiler_params=_cparams(("parallel", "arbitrary")),
        name="na_attention",
    )(proj, proj, proj, bias)


def _rope_angles(t, d):
    inv = ROPE_THETA ** (-np.arange(0, d, 2, dtype=np.float32) / d)
    return np.arange(t, dtype=np.float32)[:, None] * inv[None, :]


def rope_tables_pair(t):
    ang = _rope_angles(t, HEAD_DIM)
    cos = np.tile(np.cos(ang), (1, 4))
    sin = np.tile(np.concatenate([-np.sin(ang), np.sin(ang)], axis=1), (1, 2))
    return jnp.asarray(cos, F32), jnp.asarray(sin, F32)


def _rope_qkv_kernel(x_ref, v_ref, cos_ref, sin_ref, o_ref):
    cos = cos_ref[...]
    sin = sin_ref[...]
    half = HEAD_DIM // 2
    for c in range(x_ref.shape[1] // LANES):
        x = x_ref[:, c * LANES:(c + 1) * LANES].astype(F32)
        rot = jnp.where(_lane_lt(x.shape, half, HEAD_DIM),
                        pltpu.roll(x, LANES - half, 1), pltpu.roll(x, half, 1))
        y = x * cos + rot * sin
        if c < N_PAIRS:
            y = y * (HEAD_DIM ** -0.5)
        o_ref[:, c * LANES:(c + 1) * LANES] = y.astype(o_ref.dtype)
    o_ref[:, 2 * SLAB:3 * SLAB] = v_ref[...]


def rope_qkv(proj, b, t, tm):
    n = b * t
    cos, sin = rope_tables_pair(t)
    nb = t // tm
    return pl.pallas_call(
        _rope_qkv_kernel,
        grid=(n // tm,),
        in_specs=[
            pl.BlockSpec((tm, 2 * SLAB), lambda i: (i, COL_DLQ // 2)),
            pl.BlockSpec((tm, SLAB), lambda i: (i, COL_DLV)),
            pl.BlockSpec((tm, LANES), lambda i: (i % nb, 0)),
            pl.BlockSpec((tm, LANES), lambda i: (i % nb, 0)),
        ],
        out_specs=pl.BlockSpec((tm, 3 * SLAB), lambda i: (i, 0)),
        out_shape=jax.ShapeDtypeStruct((n, 3 * SLAB), BF16),
        compiler_params=_cparams(("parallel",)),
        name="rope_qkv",
    )(proj, proj, cos, sin)


def _band_kernel(q_ref, k_ref, v_ref, o_ref, lse_ref, *, sub, half, span):
    qb = pl.program_id(2)
    tq = q_ref.shape[0]
    start = jnp.clip(qb * tq - half, 0, sub - span)
    start = pl.multiple_of(start, half)
    q_pos = qb * tq + lax.broadcasted_iota(jnp.int32, (2 * tq, span), 0) % tq
    k_pos = start + lax.broadcasted_iota(jnp.int32, (2 * tq, span), 1)
    valid = jnp.abs(k_pos - q_pos) <= half
    for pair in range(N_PAIRS):
        cols = slice(pair * HEAD_PAIR, (pair + 1) * HEAD_PAIR)
        k2 = k_ref[pl.ds(start, span), cols]
        v2 = v_ref[pl.ds(start, span), cols]
        o, lse = _pair_softmax_attend(q_ref[:, cols], k2, v2, lambda s: jnp.where(valid, s, NEG_INF))
        o_ref[:, cols] = o
        lse_ref[:, cols] = lse


def band_attention(qkv, b, t, window, dil):
    half = window // (2 * dil)
    sub = t // dil
    tq = DIL_QBLOCK
    span = tq + 2 * half
    nqb = sub // tq
    n = b * t
    qkv_v = qkv.reshape(n // dil, dil * 3 * SLAB)
    o, lse = pl.pallas_call(
        functools.partial(_band_kernel, sub=sub, half=half, span=span),
        grid=(b, dil, nqb),
        in_specs=[
            pl.BlockSpec((tq, SLAB), lambda i, p, s: (i * nqb + s, 3 * p)),
            pl.BlockSpec((sub, SLAB), lambda i, p, s: (i, 3 * p + 1)),
            pl.BlockSpec((sub, SLAB), lambda i, p, s: (i, 3 * p + 2)),
        ],
        out_specs=[pl.BlockSpec((tq, SLAB), lambda i, p, s: (i * nqb + s, p))] * 2,
        out_shape=[jax.ShapeDtypeStruct((n // dil, dil * SLAB), F32)] * 2,
        compiler_params=_cparams(("parallel", "parallel", "arbitrary")),
        name="band_attention_d%d" % dil,
    )(qkv_v, qkv_v, qkv_v)
    return o.reshape(n, SLAB), lse.reshape(n, SLAB)


def _dil_combine_kernel(*refs):
    nbr = len(DIL_PAIRS)
    o_refs, l_refs, out_ref = refs[:nbr], refs[nbr:2 * nbr], refs[2 * nbr]
    lses = [r[...] for r in l_refs]
    m = functools.reduce(jnp.maximum, lses)
    ws = [jnp.exp(l - m) for l in lses]
    den = functools.reduce(jnp.add, ws)
    acc = functools.reduce(jnp.add, [(w / den) * r[...] for w, r in zip(ws, o_refs)])
    out_ref[...] = acc.astype(out_ref.dtype)


def dil_combine(outs, lses, tm):
    n = outs[0].shape[0]
    spec = pl.BlockSpec((tm, SLAB), lambda i: (i, 0))
    return pl.pallas_call(
        _dil_combine_kernel,
        grid=(n // tm,),
        in_specs=[spec] * (2 * len(DIL_PAIRS)),
        out_specs=spec,
        out_shape=jax.ShapeDtypeStruct((n, SLAB), BF16),
        compiler_params=_cparams(("parallel",)),
        name="dil_combine",
    )(*outs, *lses)


MLA_QK = MLA_NOPE + MLA_ROPE


def mla_tables(t):
    ang = _rope_angles(t, MLA_ROPE)
    cos2 = np.concatenate([np.cos(ang), np.cos(ang)], axis=1)
    sin2 = np.concatenate([np.sin(ang), np.sin(ang)], axis=1)
    z = lambda w: np.zeros((t, w), np.float32)
    q_cos = np.concatenate([np.ones((t, MLA_NOPE), np.float32), cos2, z(LANES - MLA_QK)], axis=1)
    q_sin = np.concatenate([z(MLA_NOPE), sin2, z(LANES - MLA_QK)], axis=1)
    k_cos = np.concatenate([cos2, z(LANES - MLA_ROPE)], axis=1)
    k_sin = np.concatenate([-sin2[:, :MLA_ROPE // 2], sin2[:, MLA_ROPE // 2:], z(LANES - MLA_ROPE)], axis=1)
    return tuple(jnp.asarray(a, F32) for a in (q_cos, q_sin, k_cos, k_sin))


def mla_weights(w_uq, w_ukv):
    hq = w_uq.reshape(MLA_Q_RANK, MLA_HEADS, MLA_QK)
    nope, pe = hq[..., :MLA_NOPE], hq[..., MLA_NOPE:]
    pe_rot = jnp.concatenate([-pe[..., MLA_ROPE // 2:], pe[..., :MLA_ROPE // 2]], axis=-1)
    zq = jnp.zeros((MLA_Q_RANK, MLA_HEADS, LANES - MLA_QK), w_uq.dtype)
    w1 = jnp.concatenate([nope, pe, zq], axis=-1).reshape(MLA_Q_RANK, MLA_HEADS * LANES)
    w2 = jnp.concatenate([jnp.zeros_like(nope), pe_rot, zq], axis=-1).reshape(MLA_Q_RANK, MLA_HEADS * LANES)
    hkv = w_ukv.reshape(MLA_KV_RANK, MLA_HEADS, MLA_NOPE + MLA_V)
    k_nope, v = hkv[..., :MLA_NOPE], hkv[..., MLA_NOPE:]
    zk = jnp.zeros((MLA_KV_RANK, MLA_HEADS, LANES - MLA_NOPE), w_ukv.dtype)
    wk = jnp.concatenate([k_nope, zk], axis=-1).reshape(MLA_KV_RANK, MLA_HEADS * LANES)
    wv = v.reshape(MLA_KV_RANK, MLA_HEADS * MLA_V)
    place = np.zeros((LANES, MLA_HEADS * LANES), np.float32)
    for h in range(MLA_HEADS):
        place[np.arange(MLA_ROPE), h * LANES + MLA_NOPE + np.arange(MLA_ROPE)] = 1.0
    return (w1.astype(BF16), w2.astype(BF16), wk.astype(BF16), wv.astype(BF16), jnp.asarray(place, BF16))


def _mla_q_kernel(c_ref, nw_ref, w1_ref, w2_ref, cos_ref, sin_ref, o_ref):
    cn = _rms(c_ref[...].astype(F32), nw_ref[...]).astype(BF16)
    cos = jnp.tile(cos_ref[...], (1, MLA_HEADS))
    sin = jnp.tile(sin_ref[...], (1, MLA_HEADS))
    q = _dot(cn, w1_ref[...]) * cos + _dot(cn, w2_ref[...]) * sin
    o_ref[...] = (q * (MLA_QK ** -0.5)).astype(o_ref.dtype)


def _mla_kv_kernel(c_ref, tail_ref, nw_ref, wk_ref, wv_ref, place_ref, cos_ref, sin_ref, k_ref, v_ref):
    cn = _rms(c_ref[...].astype(F32), nw_ref[...]).astype(BF16)
    kr = tail_ref[...]
    half = MLA_ROPE // 2
    rot = jnp.where(_lane_lt(kr.shape, half), pltpu.roll(kr, LANES - half, 1), pltpu.roll(kr, half, 1))
    k_pe = (kr * cos_ref[...] + rot * sin_ref[...]).astype(BF16)
    k_ref[...] = (_dot(cn, wk_ref[...]) + _dot(k_pe, place_ref[...])).astype(k_ref.dtype)
    v_ref[...] = _dot(cn, wv_ref[...]).astype(v_ref.dtype)


def mla_project(proj, tail, q_norm_w, kv_norm_w, w_uq, w_ukv, b, t, tm):
    n = b * t
    nb = t // tm
    w1, w2, wk, wv, place = mla_weights(w_uq, w_ukv)
    q_cos, q_sin, k_cos, k_sin = mla_tables(t)
    wide = MLA_HEADS * LANES
    full = lambda shape: pl.BlockSpec(shape, lambda i: (0, 0))
    tab = pl.BlockSpec((tm, LANES), lambda i: (i % nb, 0))
    qf = pl.pallas_call(
        _mla_q_kernel,
        grid=(n // tm,),
        in_specs=[pl.BlockSpec((tm, SLAB), lambda i: (i, COL_CQ)), full((1, MLA_Q_RANK)),
                  full((MLA_Q_RANK, wide)), full((MLA_Q_RANK, wide)), tab, tab],
        out_specs=pl.BlockSpec((tm, wide), lambda i: (i, 0)),
        out_shape=jax.ShapeDtypeStruct((n, wide), BF16),
        compiler_params=_cparams(("parallel",)),
        name="mla_q_proj",
    )(proj, q_norm_w.reshape(1, MLA_Q_RANK), w1, w2, q_cos, q_sin)
    kf, vf = pl.pallas_call(
        _mla_kv_kernel,
        grid=(n // tm,),
        in_specs=[pl.BlockSpec((tm, SLAB), lambda i: (i, COL_CKV)),
                  pl.BlockSpec((tm, LANES), lambda i: (i, 0)), full((1, MLA_KV_RANK)),
                  full((MLA_KV_RANK, wide)), full((MLA_KV_RANK, SLAB)), full((LANES, wide)), tab, tab],
        out_specs=[pl.BlockSpec((tm, wide), lambda i: (i, 0)), pl.BlockSpec((tm, SLAB), lambda i: (i, 0))],
        out_shape=[jax.ShapeDtypeStruct((n, wide), BF16), jax.ShapeDtypeStruct((n, SLAB), BF16)],
        compiler_params=_cparams(("parallel",)),
        name="mla_kv_proj",
    )(proj, tail, kv_norm_w.reshape(1, MLA_KV_RANK), wk, wv, place, k_cos, k_sin)
    return qf, kf, vf


def _mla_attn_kernel(q_ref, k_ref, v_ref, o_ref, *, t, tk):
    tq = q_ref.shape[0]
    first = _lane_lt((tq, HEAD_PAIR), HEAD_DIM)
    outs = []
    for hh in range(2):
        q = q_ref[:, hh * LANES:(hh + 1) * LANES]

        def body(c, carry):
            m, l, acc = carry
            k0 = pl.multiple_of(c * tk, tk)
            s = _dot_nt(q, k_ref[pl.ds(k0, tk), hh * LANES:(hh + 1) * LANES])
            m_new = jnp.maximum(m, jnp.max(s, axis=-1, keepdims=True))
            alpha = jnp.exp(m - m_new)
            p = jnp.exp(s - m_new)
            l = alpha * l + jnp.sum(p, axis=-1, keepdims=True)
            acc = alpha * acc + _dot(p.astype(BF16), v_ref[pl.ds(k0, tk), :])
            return m_new, l, acc

        init = (jnp.full((tq, 1), -jnp.inf, F32), jnp.zeros((tq, 1), F32), jnp.zeros((tq, HEAD_PAIR), F32))
        m, l, acc = lax.fori_loop(0, t // tk, body, init)
        outs.append(acc / l)
    o_ref[...] = jnp.where(first, outs[0], outs[1]).astype(o_ref.dtype)


def mla_attention(qf, kf, vf, b, t, tq, tk):
    n = b * t
    nq = t // tq
    return pl.pallas_call(
        functools.partial(_mla_attn_kernel, t=t, tk=tk),
        grid=(b, N_PAIRS, nq),
        in_specs=[
            pl.BlockSpec((tq, 2 * LANES), lambda i, p, s: (i * nq + s, p)),
            pl.BlockSpec((t, 2 * LANES), lambda i, p, s: (i, p)),
            pl.BlockSpec((t, HEAD_PAIR), lambda i, p, s: (i, p)),
        ],
        out_specs=pl.BlockSpec((tq, HEAD_PAIR), lambda i, p, s: (i * nq + s, p)),
        out_shape=jax.ShapeDtypeStruct((n, SLAB), BF16),
        compiler_params=_cparams(("parallel", "parallel", "arbitrary")),
        name="mla_attention",
    )(qf, kf, vf)


def _in_proj_segments():
    sizes = (SSM_INNER, SSM_CONV_CH, 2 * SSM_HEADS, SLAB, SLAB, SLAB, MLA_Q_RANK, MLA_KV_RANK, MLA_ROPE,
             SLAB, SLAB, SLAB)
    off = [int(v) for v in np.concatenate([[0], np.cumsum(sizes)])]
    main = ((off[0], off[2]), (off[3], off[8]), (off[9], off[12]))
    tail = ((off[8], off[9]), (off[2], off[3]))
    return main, tail


def _in_proj_columns():
    main, tail = _in_proj_segments()
    cols = lambda segs: np.concatenate([np.arange(a, b) for a, b in segs])
    return cols(main), cols(tail)


def in_proj_weights(w_in_l):
    main, tail = _in_proj_segments()
    w_main = jnp.concatenate([w_in_l[:, a:b] for a, b in main], axis=1).astype(BF16)
    pad = jnp.zeros((D_MODEL, LANES - sum(b - a for a, b in tail)), w_in_l.dtype)
    w_tail = jnp.concatenate([w_in_l[:, a:b] for a, b in tail] + [pad], axis=1).astype(BF16)
    return w_main, w_tail


def mixers(proj, tail, p, l, b, t):
    xbc = conv_silu(proj, p["conv_w"][l], p["conv_b"][l], b, t)
    y_f = ssd_scan(xbc, tail, p["dt_bias"][l], p["a_log"][l], b, t, 0)
    y_b = ssd_scan(xbc, tail, p["dt_bias"][l], p["a_log"][l], b, t, 1)
    y_ssm = ssd_combine(y_f, y_b, xbc, proj, p["d_skip"][l], p["ssm_norm_w"][l], 1024)

    y_na = na_attention(proj, p["na_rpb"][l], b, t)

    qf, kf, vf = mla_project(proj, tail, p["mla_q_norm_w"][l], p["mla_kv_norm_w"][l],
                             p["mla_w_uq"][l], p["mla_w_ukv"][l], b, t, 512)
    y_mla = mla_attention(qf, kf, vf, b, t, 512, 512)

    qkv = rope_qkv(proj, b, t, 1024)
    outs, lses = zip(*[band_attention(qkv, b, t, w, d) for w, d in DIL_PAIRS])
    y_dil = dil_combine(outs, lses, 1024)
    return y_ssm, y_na, y_mla, y_dil


def kernel(x, attn_norm_w, w_in, conv_w, conv_b, a_log, dt_bias, d_skip, ssm_norm_w, na_rpb,
           mla_q_norm_w, mla_kv_norm_w, mla_w_uq, mla_w_ukv, w_o, ffn_norm_w, ffn_w_gate, ffn_w_up,
           ffn_w_down, router_w, exp_w_gate, exp_w_up, exp_w_down, final_norm_w):
    b, t, _ = x.shape
    n = b * t
    depth = w_in.shape[0]
    p = dict(conv_w=conv_w, conv_b=conv_b, a_log=a_log, dt_bias=dt_bias, d_skip=d_skip,
             ssm_norm_w=ssm_norm_w, na_rpb=na_rpb, mla_q_norm_w=mla_q_norm_w,
             mla_kv_norm_w=mla_kv_norm_w, mla_w_uq=mla_w_uq, mla_w_ukv=mla_w_ukv)
    x = x.reshape(n, D_MODEL)
    cast_rows = 256
    w_o_b = cast_bf16(w_o, cast_rows)
    ffn_b = [cast_bf16(w, cast_rows) for w in (ffn_w_gate, ffn_w_up, ffn_w_down)]
    exp_b = [cast_bf16(w, cast_rows) for w in (exp_w_gate, exp_w_up, exp_w_down)]
    moe_tm = 512
    normed = False
    for l in range(depth):
        w_main, w_tail = in_proj_weights(w_in[l])
        proj = norm_matmul(x, 0, D_MODEL, attn_norm_w[l], w_main, BF16, 1024, 512)
        tail = norm_matmul(x, 0, D_MODEL, attn_norm_w[l], w_tail, F32, 1024, LANES)
        mix = mixers(proj, tail, p, l, b, t)
        x = out_proj(mix, w_o_b, l, x, 1024, 1024)
        j = l // 2
        if l % 2 == 0:
            x = ffn_dense(x, ffn_norm_w[l], *ffn_b, j, 512, 512)
        else:
            top_i, gates = moe_router(x, ffn_norm_w[l], router_w[j], 512)
            src, pos, tile_expert, tile_valid = moe_plan(top_i, moe_tm)
            y = moe_ffn(x, ffn_norm_w[l], *exp_b, j, src, tile_expert, tile_valid, moe_tm, 512)
            last = l == depth - 1
            x = moe_combine(x, gates, y, pos, final_norm_w if last else None, 256)
            normed = last
    if not normed:
        x = rmsnorm_rows(x, final_norm_w, 1024)
    return x.reshape(b, t, D_MODEL)
```

```python
import functools
import math

import numpy as np
import jax
import jax.numpy as jnp
from jax import lax
from jax.experimental import pallas as pl
from jax.experimental.pallas import tpu as pltpu

F32 = jnp.float32
BF16 = jnp.bfloat16

D_MODEL = 2048
GRID_W = 64
HEAD_DIM = 64
ROPE_THETA = 10000.0
NORM_EPS = 1e-6
NEG_INF = -1e30

SSM_HEADS = 8
SSM_HEAD_DIM = 64
SSM_INNER = SSM_HEADS * SSM_HEAD_DIM
SSM_GROUPS = 2
SSM_STATE = 128
SSM_CONV = 5
SSM_CHUNK = 128
SSM_CONV_CH = SSM_INNER + 2 * SSM_GROUPS * SSM_STATE

NA_HEADS = 8
NA_WIN_ROWS = 8
NA_WIN_COLS = 16
NA_COL_BLOCK = 16
NA_KEY_COLS = 32
NA_ROWS_PER_STEP = 4

MLA_HEADS = 8
MLA_Q_RANK = 512
MLA_KV_RANK = 512
MLA_NOPE = 64
MLA_ROPE = 32
MLA_V = 64

DIL_HEADS = 8
DIL_PAIRS = ((128, 1), (512, 4), (2048, 16))
DIL_QBLOCK = 128

N_EXPERTS = 8
TOP_K = 2

LANES = 128
HEAD_PAIR = 2 * HEAD_DIM
N_PAIRS = 4
SLAB = 512

COL_Z, COL_XBC, COL_NAQ, COL_NAK, COL_NAV, COL_CQ, COL_CKV, COL_DLQ, COL_DLK, COL_DLV = (
    0, 1, 3, 4, 5, 6, 7, 8, 9, 10)
PROJ_MAIN = 11 * SLAB
TAIL_DT = 32

VMEM_LIMIT = 56 * 1024 * 1024


def _cparams(sem, vmem=VMEM_LIMIT):
    return pltpu.CompilerParams(dimension_semantics=sem, vmem_limit_bytes=vmem)


def _lane_lt(shape, bound, period=None):
    lane = lax.broadcasted_iota(jnp.int32, shape, len(shape) - 1)
    if period is not None:
        lane = lane % period
    return lane < bound


def _rms(x, w):
    ms = jnp.mean(x * x, axis=-1, keepdims=True)
    return x * lax.rsqrt(ms + NORM_EPS) * w


def _dot(a, b):
    return jnp.dot(a, b, preferred_element_type=F32)


def _dot_nt(a, b):
    return lax.dot_general(a, b, (((1,), (1,)), ((), ())), preferred_element_type=F32)


def _dot_tn(a, b):
    return lax.dot_general(a, b, (((0,), (0,)), ((), ())), preferred_element_type=F32)


def _cast_kernel(x_ref, o_ref):
    o_ref[...] = x_ref[...].astype(o_ref.dtype)


def cast_bf16(w, tr):
    shape = w.shape
    w2 = w.reshape(-1, shape[-1])
    r, c = w2.shape
    out = pl.pallas_call(
        _cast_kernel,
        grid=(r // tr,),
        in_specs=[pl.BlockSpec((tr, c), lambda i: (i, 0))],
        out_specs=pl.BlockSpec((tr, c), lambda i: (i, 0)),
        out_shape=jax.ShapeDtypeStruct((r, c), BF16),
        compiler_params=_cparams(("parallel",)),
        name="cast_bf16",
    )(w2)
    return out.reshape(shape)


def _norm_matmul_kernel(x_ref, nw_ref, w_ref, o_ref, h_ref):
    @pl.when(pl.program_id(1) == 0)
    def _():
        h_ref[...] = _rms(x_ref[...].astype(F32), nw_ref[...]).astype(BF16)

    o_ref[...] = _dot(h_ref[...], w_ref[...]).astype(o_ref.dtype)


def norm_matmul(x, xcol, k, nw, w, out_dtype, tm, tn):
    n = x.shape[0]
    nout = w.shape[1]
    return pl.pallas_call(
        _norm_matmul_kernel,
        grid=(n // tm, nout // tn),
        in_specs=[
            pl.BlockSpec((tm, k), lambda i, j: (i, xcol)),
            pl.BlockSpec((1, k), lambda i, j: (0, 0)),
            pl.BlockSpec((k, tn), lambda i, j: (0, j)),
        ],
        out_specs=pl.BlockSpec((tm, tn), lambda i, j: (i, j)),
        out_shape=jax.ShapeDtypeStruct((n, nout), out_dtype),
        scratch_shapes=[pltpu.VMEM((tm, k), BF16)],
        compiler_params=_cparams(("parallel", "arbitrary")),
        name="norm_matmul",
    )(x, nw.reshape(1, k), w)


def _out_proj_kernel(a0_ref, a1_ref, a2_ref, a3_ref, w_ref, r_ref, o_ref):
    acc = r_ref[...]
    for s, a_ref in enumerate((a0_ref, a1_ref, a2_ref, a3_ref)):
        acc = acc + _dot(a_ref[...], w_ref[s * SLAB:(s + 1) * SLAB, :])
    o_ref[...] = acc


def out_proj(mix, w, layer, res, tm, tn):
    n = res.shape[0]
    return pl.pallas_call(
        _out_proj_kernel,
        grid=(n // tm, D_MODEL // tn),
        in_specs=[pl.BlockSpec((tm, SLAB), lambda i, j: (i, 0))] * 4 + [
            pl.BlockSpec((None, 4 * SLAB, tn), lambda i, j: (layer, 0, j)),
            pl.BlockSpec((tm, tn), lambda i, j: (i, j)),
        ],
        out_specs=pl.BlockSpec((tm, tn), lambda i, j: (i, j)),
        out_shape=jax.ShapeDtypeStruct((n, D_MODEL), F32),
        compiler_params=_cparams(("parallel", "arbitrary")),
        name="out_proj",
    )(*mix, w, res)


def _ffn_kernel(x_ref, nw_ref, wg_ref, wu_ref, wd_ref, o_ref, h_ref):
    @pl.when(pl.program_id(1) == 0)
    def _():
        x = x_ref[...]
        h_ref[...] = _rms(x, nw_ref[...]).astype(BF16)
        o_ref[...] = x

    _swiglu_rows(h_ref, wg_ref, wu_ref, wd_ref, o_ref)


FFN_ROW_CHUNK = 512


def _swiglu_rows(h_ref, wg_ref, wu_ref, wd_ref, o_ref):
    tm = h_ref.shape[0]
    for r0 in range(0, tm, FFN_ROW_CHUNK):
        rows = slice(r0, r0 + FFN_ROW_CHUNK)
        h = h_ref[rows, :]
        g = _dot(h, wg_ref[...])
        u = _dot(h, wu_ref[...])
        a = (g * jax.nn.sigmoid(g) * u).astype(BF16)
        o_ref[rows, :] += _dot(a, wd_ref[...])


def ffn_dense(x, nw, wg, wu, wd, layer, tm, tf):
    n = x.shape[0]
    d_ff = wg.shape[-1]
    return pl.pallas_call(
        _ffn_kernel,
        grid=(n // tm, d_ff // tf),
        in_specs=[
            pl.BlockSpec((tm, D_MODEL), lambda i, j: (i, 0), pipeline_mode=pl.Buffered(1)),
            pl.BlockSpec((1, D_MODEL), lambda i, j: (0, 0)),
            pl.BlockSpec((None, D_MODEL, tf), lambda i, j: (layer, 0, j)),
            pl.BlockSpec((None, D_MODEL, tf), lambda i, j: (layer, 0, j)),
            pl.BlockSpec((None, tf, D_MODEL), lambda i, j: (layer, j, 0)),
        ],
        out_specs=pl.BlockSpec((tm, D_MODEL), lambda i, j: (i, 0)),
        out_shape=jax.ShapeDtypeStruct((n, D_MODEL), F32),
        scratch_shapes=[pltpu.VMEM((tm, D_MODEL), BF16)],
        compiler_params=_cparams(("parallel", "arbitrary")),
        name="ffn_dense",
    )(x, nw.reshape(1, D_MODEL), wg, wu, wd)


def _router_kernel(x_ref, nw_ref, rw_ref, idx_ref, gate_ref):
    h = _rms(x_ref[...], nw_ref[...])
    logits = jnp.dot(h, rw_ref[...], preferred_element_type=F32, precision=lax.Precision.HIGHEST)
    lane = lax.broadcasted_iota(jnp.int32, logits.shape, 1)
    logits = jnp.where(lane < N_EXPERTS, logits, -jnp.inf)
    m1 = jnp.max(logits, axis=-1, keepdims=True)
    i1 = jnp.min(jnp.where(logits == m1, lane, LANES), axis=-1, keepdims=True)
    rest = jnp.where(lane == i1, -jnp.inf, logits)
    m2 = jnp.max(rest, axis=-1, keepdims=True)
    i2 = jnp.min(jnp.where(rest == m2, lane, LANES), axis=-1, keepdims=True)
    e2 = jnp.exp(m2 - m1)
    g1 = 1.0 / (1.0 + e2)
    g2 = e2 / (1.0 + e2)
    idx_ref[...] = jnp.where(lane == 0, i1, i2)[:, :TOP_K]
    gate_ref[...] = jnp.where(lane == 0, g1, g2)[:, :TOP_K]


def moe_router(x, nw, router_w, tm):
    n = x.shape[0]
    rw = jnp.zeros((D_MODEL, LANES), F32).at[:, :N_EXPERTS].set(router_w)
    return pl.pallas_call(
        _router_kernel,
        grid=(n // tm,),
        in_specs=[
            pl.BlockSpec((tm, D_MODEL), lambda i: (i, 0)),
            pl.BlockSpec((1, D_MODEL), lambda i: (0, 0)),
            pl.BlockSpec((D_MODEL, LANES), lambda i: (0, 0)),
        ],
        out_specs=[pl.BlockSpec((tm, TOP_K), lambda i: (i, 0)),
                   pl.BlockSpec((tm, TOP_K), lambda i: (i, 0))],
        out_shape=[jax.ShapeDtypeStruct((n, TOP_K), jnp.int32),
                   jax.ShapeDtypeStruct((n, TOP_K), F32)],
        compiler_params=_cparams(("parallel",)),
        name="moe_router",
    )(x, nw.reshape(1, D_MODEL), rw)


def moe_plan(top_i, tm):
    n = top_i.shape[0]
    flat_e = top_i.reshape(-1)
    onehot = (flat_e[:, None] == jnp.arange(N_EXPERTS, dtype=jnp.int32)[None, :]).astype(jnp.int32)
    csum = jnp.cumsum(onehot, axis=0)
    counts = csum[-1]
    rank = jnp.sum(onehot * csum, axis=1) - 1
    padded = ((counts + tm - 1) // tm) * tm
    pend = jnp.cumsum(padded)
    pstart = pend - padded
    pos = pstart[flat_e] + rank
    n_slots = n * TOP_K + N_EXPERTS * tm
    n_tiles = n_slots // tm
    src = jnp.zeros((n_slots,), jnp.int32).at[pos].set(jnp.arange(n * TOP_K, dtype=jnp.int32) // TOP_K)
    tile_start = jnp.arange(n_tiles, dtype=jnp.int32) * tm
    tile_expert = jnp.sum((tile_start[:, None] >= pend[None, :]).astype(jnp.int32), axis=1)
    tile_valid = (tile_start < pend[-1]).astype(jnp.int32)
    last_valid = jnp.maximum(pend[-1] // tm - 1, 0)
    tile_expert = jnp.where(tile_valid == 1, tile_expert, tile_expert[last_valid]).astype(jnp.int32)
    return src, pos.reshape(n, TOP_K).astype(jnp.int32), tile_expert, tile_valid


def _moe_ffn_kernel(te_ref, tv_ref, src_ref, nsrc_ref, x_hbm, nw_ref, wg_ref, wu_ref, wd_ref, y_ref,
                    xbuf, h_ref, sem, *, tm):
    i = pl.program_id(0)
    j = pl.program_id(1)
    n_tiles = pl.num_programs(0)
    valid = tv_ref[i] == 1

    def row_copy(idx_ref, r):
        tok = idx_ref[0, 0, r]
        return pltpu.make_async_copy(x_hbm.at[pl.ds(tok, 1)], xbuf.at[pl.ds(r, 1)], sem)

    def gather(idx_ref):
        def start(r, c):
            row_copy(idx_ref, r).start()
            return c

        lax.fori_loop(0, tm, start, 0)

    @pl.when(j == 0)
    def _():
        y_ref[...] = jnp.zeros_like(y_ref)

    @pl.when(jnp.logical_and(j == 0, valid))
    def _():
        @pl.when(i == 0)
        def _():
            gather(src_ref)

        def wait(r, c):
            row_copy(src_ref, r).wait()
            return c

        lax.fori_loop(0, tm, wait, 0)
        h_ref[...] = _rms(xbuf[...], nw_ref[...]).astype(BF16)

        nxt = jnp.minimum(i + 1, n_tiles - 1)

        @pl.when(jnp.logical_and(i + 1 < n_tiles, tv_ref[nxt] == 1))
        def _():
            gather(nsrc_ref)

    @pl.when(valid)
    def _():
        _swiglu_rows(h_ref, wg_ref, wu_ref, wd_ref, y_ref)


def moe_ffn(x, nw, wg, wu, wd, layer, src, tile_expert, tile_valid, tm, tf):
    n_slots = src.shape[0]
    n_tiles = n_slots // tm
    d_ff = wg.shape[-1]
    nf = d_ff // tf

    def wcol(i, j, te_ref, tv_ref):
        return (layer, te_ref[i], 0, jnp.where(tv_ref[i] == 1, j, nf - 1))

    def wrow(i, j, te_ref, tv_ref):
        return (layer, te_ref[i], jnp.where(tv_ref[i] == 1, j, nf - 1), 0)

    grid_spec = pltpu.PrefetchScalarGridSpec(
        num_scalar_prefetch=2,
        grid=(n_tiles, nf),
        in_specs=[
            pl.BlockSpec((1, 1, tm), lambda i, j, *_: (i, 0, 0), memory_space=pltpu.SMEM),
            pl.BlockSpec((1, 1, tm), lambda i, j, *_: (jnp.minimum(i + 1, n_tiles - 1), 0, 0),
                         memory_space=pltpu.SMEM),
            pl.BlockSpec(memory_space=pl.ANY),
            pl.BlockSpec((1, D_MODEL), lambda i, j, *_: (0, 0)),
            pl.BlockSpec((None, None, D_MODEL, tf), wcol),
            pl.BlockSpec((None, None, D_MODEL, tf), wcol),
            pl.BlockSpec((None, None, tf, D_MODEL), wrow),
        ],
        out_specs=pl.BlockSpec((tm, D_MODEL), lambda i, j, *_: (i, 0)),
        scratch_shapes=[pltpu.VMEM((tm, D_MODEL), F32), pltpu.VMEM((tm, D_MODEL), BF16),
                        pltpu.SemaphoreType.DMA],
    )
    src3 = src.reshape(n_tiles, 1, tm)
    return pl.pallas_call(
        functools.partial(_moe_ffn_kernel, tm=tm),
        grid_spec=grid_spec,
        out_shape=jax.ShapeDtypeStruct((n_slots, D_MODEL), F32),
        compiler_params=_cparams(("arbitrary", "arbitrary")),
        name="moe_ffn",
    )(tile_expert, tile_valid, src3, src3, x, nw.reshape(1, D_MODEL), wg, wu, wd)


def _moe_combine_kernel(pos_ref, npos_ref, x_ref, gate_ref, y_hbm, fw_ref, o_ref, ybuf, sem, *, tm, final_norm):
    i = pl.program_id(0)
    n_tiles = pl.num_programs(0)
    cur = i % 2

    def row_copy(idx_ref, buf, r, k):
        slot = idx_ref[0, 0, r * TOP_K + k]
        return pltpu.make_async_copy(y_hbm.at[pl.ds(slot, 1)], ybuf.at[buf, k, pl.ds(r, 1)], sem.at[buf])

    def gather(idx_ref, buf):
        def start(r, c):
            for k in range(TOP_K):
                row_copy(idx_ref, buf, r, k).start()
            return c

        lax.fori_loop(0, tm, start, 0)

    @pl.when(i == 0)
    def _():
        gather(pos_ref, 0)

    @pl.when(i + 1 < n_tiles)
    def _():
        gather(npos_ref, 1 - cur)

    def wait(r, c):
        for k in range(TOP_K):
            row_copy(pos_ref, cur, r, k).wait()
        return c

    lax.fori_loop(0, tm, wait, 0)
    gates = gate_ref[...]
    out = x_ref[...]
    for k in range(TOP_K):
        out = out + gates[:, k:k + 1] * ybuf[cur, k]
    if final_norm:
        out = _rms(out, fw_ref[...])
    o_ref[...] = out


def moe_combine(x, gates, y, pos, final_w, tm):
    n = x.shape[0]
    final_norm = final_w is not None
    fw = final_w if final_norm else jnp.ones((D_MODEL,), F32)
    n_tiles = n // tm
    pos3 = pos.reshape(n_tiles, 1, tm * TOP_K)
    return pl.pallas_call(
        functools.partial(_moe_combine_kernel, tm=tm, final_norm=final_norm),
        grid=(n_tiles,),
        in_specs=[
            pl.BlockSpec((1, 1, tm * TOP_K), lambda i: (i, 0, 0), memory_space=pltpu.SMEM),
            pl.BlockSpec((1, 1, tm * TOP_K), lambda i: (jnp.minimum(i + 1, n_tiles - 1), 0, 0),
                         memory_space=pltpu.SMEM),
            pl.BlockSpec((tm, D_MODEL), lambda i: (i, 0)),
            pl.BlockSpec((tm, TOP_K), lambda i: (i, 0)),
            pl.BlockSpec(memory_space=pl.ANY),
            pl.BlockSpec((1, D_MODEL), lambda i: (0, 0)),
        ],
        out_specs=pl.BlockSpec((tm, D_MODEL), lambda i: (i, 0)),
        out_shape=jax.ShapeDtypeStruct((n, D_MODEL), F32),
        scratch_shapes=[pltpu.VMEM((2, TOP_K, tm, D_MODEL), F32), pltpu.SemaphoreType.DMA((2,))],
        compiler_params=_cparams(("arbitrary",)),
        name="moe_combine",
    )(pos3, pos3, x, gates, y, fw.reshape(1, D_MODEL))


def _rmsnorm_kernel(x_ref, w_ref, o_ref):
    o_ref[...] = _rms(x_ref[...], w_ref[...])


def rmsnorm_rows(x, w, tm):
    n = x.shape[0]
    return pl.pallas_call(
        _rmsnorm_kernel,
        grid=(n // tm,),
        in_specs=[pl.BlockSpec((tm, D_MODEL), lambda i: (i, 0)),
                  pl.BlockSpec((1, D_MODEL), lambda i: (0, 0))],
        out_specs=pl.BlockSpec((tm, D_MODEL), lambda i: (i, 0)),
        out_shape=jax.ShapeDtypeStruct((n, D_MODEL), F32),
        compiler_params=_cparams(("parallel",)),
        name="final_norm",
    )(x, w.reshape(1, D_MODEL))


CONV_PAD = 8


def _conv_kernel(x_ref, w_ref, b_ref, o_ref, pad_ref, *, t):
    half = SSM_CONV // 2
    zeros = jnp.zeros((CONV_PAD, pad_ref.shape[1]), F32)
    pad_ref[0:CONV_PAD, :] = zeros
    pad_ref[CONV_PAD + t:CONV_PAD + t + CONV_PAD, :] = zeros
    pad_ref[CONV_PAD:CONV_PAD + t, :] = x_ref[...].astype(F32)
    acc = jnp.zeros(o_ref.shape, F32) + b_ref[...]
    for k in range(SSM_CONV):
        acc = acc + pad_ref[pl.ds(CONV_PAD - half + k, t), :] * w_ref[k:k + 1, :]
    o_ref[...] = (acc * jax.nn.sigmoid(acc)).astype(o_ref.dtype)


def conv_silu(proj, conv_w, conv_b, b, t):
    tc = 256
    nblk = SSM_CONV_CH // tc
    col0 = COL_XBC * SLAB // tc
    return pl.pallas_call(
        functools.partial(_conv_kernel, t=t),
        grid=(b, nblk),
        in_specs=[
            pl.BlockSpec((t, tc), lambda i, j: (i, col0 + j)),
            pl.BlockSpec((SSM_CONV, tc), lambda i, j: (0, j)),
            pl.BlockSpec((1, tc), lambda i, j: (0, j)),
        ],
        out_specs=pl.BlockSpec((t, tc), lambda i, j: (i, j)),
        out_shape=jax.ShapeDtypeStruct((b * t, SSM_CONV_CH), BF16),
        scratch_shapes=[pltpu.VMEM((t + 2 * CONV_PAD, tc), F32)],
        compiler_params=_cparams(("parallel", "parallel")),
        name="conv_silu",
    )(proj, conv_w, conv_b.reshape(1, SSM_CONV_CH))


def _ssd_kernel(xbc_ref, tail_ref, bias_ref, alog_ref, tri_ref, y_ref, state_ref, *, direction):
    q = SSM_CHUNK

    @pl.when(pl.program_id(1) == 0)
    def _():
        state_ref[...] = jnp.zeros_like(state_ref)

    dt = jax.nn.softplus(tail_ref[...] + bias_ref[...])
    da = dt * (-jnp.exp(alog_ref[...]))
    cs = jnp.dot(tri_ref[...], da, preferred_element_type=F32, precision=lax.Precision.HIGHEST)
    total = cs[q - 1:q, :]
    if direction == 0:
        e_out = cs
        e_in = total - cs
        e_seg = cs
    else:
        ex = cs - da
        e_out = total - ex
        e_in = ex
        e_seg = -ex
    e_seg_t = jnp.transpose(e_seg)
    dec_out = jnp.exp(e_out)
    dec_in_dt = jnp.exp(e_in) * dt
    dt_t = jnp.transpose(dt)
    dec_tot = jnp.exp(total)

    row = lax.broadcasted_iota(jnp.int32, (q, q), 0)
    col = lax.broadcasted_iota(jnp.int32, (q, q), 1)
    keep = (row >= col) if direction == 0 else (col >= row)
    first_half = _lane_lt((q, HEAD_PAIR), HEAD_DIM)

    xs = xbc_ref[:, 0:SSM_INNER]
    heads_per_group = SSM_HEADS // SSM_GROUPS
    for g in range(SSM_GROUPS):
        bm = xbc_ref[:, SSM_INNER + g * SSM_STATE:SSM_INNER + (g + 1) * SSM_STATE]
        cm = xbc_ref[:, SSM_INNER + (SSM_GROUPS + g) * SSM_STATE:SSM_INNER + (SSM_GROUPS + g + 1) * SSM_STATE]
        cb = _dot_nt(cm, bm)
        bm_f = bm.astype(F32)
        cm_f = cm.astype(F32)
        for pp in range(heads_per_group // 2):
            pair = g * (heads_per_group // 2) + pp
            xs_pair = xs[:, pair * HEAD_PAIR:(pair + 1) * HEAD_PAIR]
            st = state_ref[pair]
            ys, sts, decs = [], [], []
            for hh in range(2):
                lane = TAIL_DT + direction * SSM_HEADS + pair * 2 + hh
                seg = e_seg[:, lane:lane + 1] - e_seg_t[lane:lane + 1, :]
                lmat = jnp.where(keep, jnp.exp(seg), 0.0)
                w = (cb * lmat * dt_t[lane:lane + 1, :]).astype(BF16)
                y = _dot(w, xs_pair)
                y = y + _dot((cm_f * dec_out[:, lane:lane + 1]).astype(BF16), st.astype(BF16))
                ys.append(y)
                sts.append(_dot_tn((bm_f * dec_in_dt[:, lane:lane + 1]).astype(BF16), xs_pair))
                decs.append(dec_tot[:, lane:lane + 1])
            y_ref[:, pair * HEAD_PAIR:(pair + 1) * HEAD_PAIR] = jnp.where(first_half, ys[0], ys[1])
            first_half_s = _lane_lt(st.shape, HEAD_DIM)
            state_ref[pair] = (st * jnp.where(first_half_s, decs[0], decs[1])
                               + jnp.where(first_half_s, sts[0], sts[1]))


def ssd_scan(xbc, tail, dt_bias, a_log, b, t, direction):
    q = SSM_CHUNK
    nc = t // q
    bias_row = jnp.zeros((1, LANES), F32).at[0, TAIL_DT:TAIL_DT + 2 * SSM_HEADS].set(dt_bias.reshape(-1))
    alog_row = jnp.zeros((1, LANES), F32).at[0, TAIL_DT:TAIL_DT + 2 * SSM_HEADS].set(a_log.reshape(-1))
    tri = jnp.asarray(np.tril(np.ones((q, q), np.float32)))

    def chunk(i, c):
        return i * nc + (c if direction == 0 else nc - 1 - c)

    return pl.pallas_call(
        functools.partial(_ssd_kernel, direction=direction),
        grid=(b, nc),
        in_specs=[
            pl.BlockSpec((q, SSM_CONV_CH), lambda i, c: (chunk(i, c), 0)),
            pl.BlockSpec((q, LANES), lambda i, c: (chunk(i, c), 0)),
            pl.BlockSpec((1, LANES), lambda i, c: (0, 0)),
            pl.BlockSpec((1, LANES), lambda i, c: (0, 0)),
            pl.BlockSpec((q, q), lambda i, c: (0, 0)),
        ],
        out_specs=pl.BlockSpec((q, SSM_INNER), lambda i, c: (chunk(i, c), 0)),
        out_shape=jax.ShapeDtypeStruct((b * t, SSM_INNER), F32),
        scratch_shapes=[pltpu.VMEM((N_PAIRS, SSM_STATE, HEAD_PAIR), F32)],
        compiler_params=_cparams(("parallel", "arbitrary")),
        name="ssd_scan_%s" % ("fwd" if direction == 0 else "bwd"),
    )(xbc, tail, bias_row, alog_row, tri)


def _ssd_combine_kernel(yf_ref, yb_ref, xs_ref, z_ref, d_ref, nw_ref, o_ref):
    y = yf_ref[...] + yb_ref[...] + xs_ref[...].astype(F32) * d_ref[...]
    z = z_ref[...].astype(F32)
    o_ref[...] = _rms(y * (z * jax.nn.sigmoid(z)), nw_ref[...]).astype(o_ref.dtype)


def ssd_combine(y_f, y_b, xbc, proj, d_skip, norm_w, tm):
    n = y_f.shape[0]
    d_row = jnp.repeat(d_skip, SSM_HEAD_DIM).reshape(1, SSM_INNER)
    row = lambda i: (i, 0)
    return pl.pallas_call(
        _ssd_combine_kernel,
        grid=(n // tm,),
        in_specs=[
            pl.BlockSpec((tm, SSM_INNER), row),
            pl.BlockSpec((tm, SSM_INNER), row),
            pl.BlockSpec((tm, SSM_INNER), row),
            pl.BlockSpec((tm, SLAB), lambda i: (i, COL_Z)),
            pl.BlockSpec((1, SSM_INNER), lambda i: (0, 0)),
            pl.BlockSpec((1, SSM_INNER), lambda i: (0, 0)),
        ],
        out_specs=pl.BlockSpec((tm, SSM_INNER), row),
        out_shape=jax.ShapeDtypeStruct((n, SSM_INNER), BF16),
        compiler_params=_cparams(("parallel",)),
        name="ssd_combine",
    )(y_f, y_b, xbc, proj, d_row, norm_w.reshape(1, SSM_INNER))


def _pair_softmax_attend(q2, k2, v2, bias_fn):
    tq = q2.shape[0]
    first_q = _lane_lt(q2.shape, HEAD_DIM)
    zero = jnp.zeros_like(q2)
    qs = jnp.concatenate([jnp.where(first_q, q2, zero), jnp.where(first_q, zero, q2)], axis=0)
    s = bias_fn(_dot_nt(qs, k2))
    m = jnp.max(s, axis=-1, keepdims=True)
    p = jnp.exp(s - m)
    l = jnp.sum(p, axis=-1, keepdims=True)
    o = _dot(p.astype(BF16), v2) / l
    lse = m + jnp.log(l)
    first_o = _lane_lt((tq, HEAD_PAIR), HEAD_DIM)
    return jnp.where(first_o, o[:tq], o[tq:]), jnp.where(first_o, lse[:tq], lse[tq:])


def na_bias_tables(rpb, rows):
    kr = min(NA_WIN_ROWS, rows)
    qc = np.arange(GRID_W)
    kc = np.arange(GRID_W)
    q_start = np.clip(qc - NA_WIN_COLS // 2, 0, GRID_W - NA_WIN_COLS)
    col_in = (kc[None, :] >= q_start[:, None]) & (kc[None, :] < q_start[:, None] + NA_WIN_COLS)
    col_off = np.clip(kc[None, :] - qc[:, None] + NA_WIN_COLS - 1, 0, 2 * NA_WIN_COLS - 2)
    onehot = (col_off[None] == np.arange(2 * NA_WIN_COLS - 1)[:, None, None]).astype(np.float32)
    expanded = jnp.einsum("hrc,cqk->hqrk", rpb, jnp.asarray(onehot), precision=lax.Precision.HIGHEST)
    expanded = jnp.where(jnp.asarray(col_in)[None, :, None, :], expanded, NEG_INF)

    def table(r):
        row_start = int(np.clip(r - kr // 2, 0, rows - kr))
        ro0 = row_start - r + NA_WIN_ROWS - 1
        return expanded[:, :, ro0:ro0 + kr, :].reshape(N_PAIRS, 2 * GRID_W, kr * GRID_W)

    rs = NA_ROWS_PER_STEP
    lo = [table(r) for r in range(rs)]
    mid = [table(min(rs, rows - 1))] * rs
    hi = [table(r) for r in range(rows - rs, rows)]
    return jnp.stack([jnp.stack(lo), jnp.stack(mid), jnp.stack(hi)])


def _na_kernel(q_ref, k_ref, v_ref, bias_ref, o_ref, *, rows, kr):
    step = pl.program_id(1)
    rs = NA_ROWS_PER_STEP
    for rr in range(rs):
        r = step * rs + rr
        row_start = jnp.clip(r - kr // 2, 0, rows - kr)
        k0 = pl.multiple_of(row_start * GRID_W, GRID_W)
        for pair in range(N_PAIRS):
            cols = slice(pair * HEAD_PAIR, (pair + 1) * HEAD_PAIR)
            q2 = q_ref[rr * GRID_W:(rr + 1) * GRID_W, cols] * jnp.asarray(HEAD_DIM ** -0.5, BF16)
            k2 = k_ref[pl.ds(k0, kr * GRID_W), cols]
            v2 = v_ref[pl.ds(k0, kr * GRID_W), cols]
            o, _ = _pair_softmax_attend(q2, k2, v2, lambda s: s + bias_ref[0, rr, pair])
            o_ref[rr * GRID_W:(rr + 1) * GRID_W, cols] = o.astype(o_ref.dtype)


def na_attention(proj, rpb, b, t):
    rows = t // GRID_W
    kr = min(NA_WIN_ROWS, rows)
    rs = NA_ROWS_PER_STEP
    nsteps = rows // rs
    bias = na_bias_tables(rpb, rows)

    def kind(i, s):
        return jnp.where(s == 0, 0, jnp.where(s == nsteps - 1, 2, 1))

    return pl.pallas_call(
        functools.partial(_na_kernel, rows=rows, kr=kr),
        grid=(b, nsteps),
        in_specs=[
            pl.BlockSpec((rs * GRID_W, SLAB), lambda i, s: (i * nsteps + s, COL_NAQ)),
            pl.BlockSpec((t, SLAB), lambda i, s: (i, COL_NAK)),
            pl.BlockSpec((t, SLAB), lambda i, s: (i, COL_NAV)),
            pl.BlockSpec((1, rs, N_PAIRS, 2 * GRID_W, kr * GRID_W), lambda i, s: (kind(i, s), 0, 0, 0, 0)),
        ],
        out_specs=pl.BlockSpec((rs * GRID_W, SLAB), lambda i, s: (i * nsteps + s, 0)),
        out_shape=jax.ShapeDtypeStruct((b * t, SLAB), BF16),
        compiler_params=_cparams(("parallel", "arbitrary")),
        name="na_attention",
    )(proj, proj, proj, bias)


def _rope_angles(t, d):
    inv = ROPE_THETA ** (-np.arange(0, d, 2, dtype=np.float32) / d)
    return np.arange(t, dtype=np.float32)[:, None] * inv[None, :]


def rope_tables_pair(t):
    ang = _rope_angles(t, HEAD_DIM)
    cos = np.tile(np.cos(ang), (1, 4))
    sin = np.tile(np.concatenate([-np.sin(ang), np.sin(ang)], axis=1), (1, 2))
    return jnp.asarray(cos, F32), jnp.asarray(sin, F32)


def _rope_qkv_kernel(x_ref, v_ref, cos_ref, sin_ref, o_ref):
    cos = cos_ref[...]
    sin = sin_ref[...]
    half = HEAD_DIM // 2
    for c in range(x_ref.shape[1] // LANES):
        x = x_ref[:, c * LANES:(c + 1) * LANES].astype(F32)
        rot = jnp.where(_lane_lt(x.shape, half, HEAD_DIM),
                        pltpu.roll(x, LANES - half, 1), pltpu.roll(x, half, 1))
        y = x * cos + rot * sin
        if c < N_PAIRS:
            y = y * (HEAD_DIM ** -0.5)
        o_ref[:, c * LANES:(c + 1) * LANES] = y.astype(o_ref.dtype)
    o_ref[:, 2 * SLAB:3 * SLAB] = v_ref[...]


def rope_qkv(proj, b, t, tm):
    n = b * t
    cos, sin = rope_tables_pair(t)
    nb = t // tm
    return pl.pallas_call(
        _rope_qkv_kernel,
        grid=(n // tm,),
        in_specs=[
            pl.BlockSpec((tm, 2 * SLAB), lambda i: (i, COL_DLQ // 2)),
            pl.BlockSpec((tm, SLAB), lambda i: (i, COL_DLV)),
            pl.BlockSpec((tm, LANES), lambda i: (i % nb, 0)),
            pl.BlockSpec((tm, LANES), lambda i: (i % nb, 0)),
        ],
        out_specs=pl.BlockSpec((tm, 3 * SLAB), lambda i: (i, 0)),
        out_shape=jax.ShapeDtypeStruct((n, 3 * SLAB), BF16),
        compiler_params=_cparams(("parallel",)),
        name="rope_qkv",
    )(proj, proj, cos, sin)


def _band_kernel(q_ref, k_ref, v_ref, o_ref, lse_ref, *, sub, half, span):
    qb = pl.program_id(2)
    tq = q_ref.shape[0]
    start = jnp.clip(qb * tq - half, 0, sub - span)
    start = pl.multiple_of(start, half)
    q_pos = qb * tq + lax.broadcasted_iota(jnp.int32, (2 * tq, span), 0) % tq
    k_pos = start + lax.broadcasted_iota(jnp.int32, (2 * tq, span), 1)
    valid = jnp.abs(k_pos - q_pos) <= half
    for pair in range(N_PAIRS):
        cols = slice(pair * HEAD_PAIR, (pair + 1) * HEAD_PAIR)
        k2 = k_ref[pl.ds(start, span), cols]
        v2 = v_ref[pl.ds(start, span), cols]
        o, lse = _pair_softmax_attend(q_ref[:, cols], k2, v2, lambda s: jnp.where(valid, s, NEG_INF))
        o_ref[:, cols] = o
        lse_ref[:, cols] = lse


def band_attention(qkv, b, t, window, dil):
    half = window // (2 * dil)
    sub = t // dil
    tq = DIL_QBLOCK
    span = tq + 2 * half
    nqb = sub // tq
    n = b * t
    qkv_v = qkv.reshape(n // dil, dil * 3 * SLAB)
    o, lse = pl.pallas_call(
        functools.partial(_band_kernel, sub=sub, half=half, span=span),
        grid=(b, dil, nqb),
        in_specs=[
            pl.BlockSpec((tq, SLAB), lambda i, p, s: (i * nqb + s, 3 * p)),
            pl.BlockSpec((sub, SLAB), lambda i, p, s: (i, 3 * p + 1)),
            pl.BlockSpec((sub, SLAB), lambda i, p, s: (i, 3 * p + 2)),
        ],
        out_specs=[pl.BlockSpec((tq, SLAB), lambda i, p, s: (i * nqb + s, p))] * 2,
        out_shape=[jax.ShapeDtypeStruct((n // dil, dil * SLAB), F32)] * 2,
        compiler_params=_cparams(("parallel", "parallel", "arbitrary")),
        name="band_attention_d%d" % dil,
    )(qkv_v, qkv_v, qkv_v)
    return o.reshape(n, SLAB), lse.reshape(n, SLAB)


def _dil_combine_kernel(*refs):
    nbr = len(DIL_PAIRS)
    o_refs, l_refs, out_ref = refs[:nbr], refs[nbr:2 * nbr], refs[2 * nbr]
    lses = [r[...] for r in l_refs]
    m = functools.reduce(jnp.maximum, lses)
    ws = [jnp.exp(l - m) for l in lses]
    den = functools.reduce(jnp.add, ws)
    acc = functools.reduce(jnp.add, [(w / den) * r[...] for w, r in zip(ws, o_refs)])
    out_ref[...] = acc.astype(out_ref.dtype)


def dil_combine(outs, lses, tm):
    n = outs[0].shape[0]
    spec = pl.BlockSpec((tm, SLAB), lambda i: (i, 0))
    return pl.pallas_call(
        _dil_combine_kernel,
        grid=(n // tm,),
        in_specs=[spec] * (2 * len(DIL_PAIRS)),
        out_specs=spec,
        out_shape=jax.ShapeDtypeStruct((n, SLAB), BF16),
        compiler_params=_cparams(("parallel",)),
        name="dil_combine",
    )(*outs, *lses)


MLA_QK = MLA_NOPE + MLA_ROPE


def mla_tables(t):
    ang = _rope_angles(t, MLA_ROPE)
    cos2 = np.concatenate([np.cos(ang), np.cos(ang)], axis=1)
    sin2 = np.concatenate([np.sin(ang), np.sin(ang)], axis=1)
    z = lambda w: np.zeros((t, w), np.float32)
    q_cos = np.concatenate([np.ones((t, MLA_NOPE), np.float32), cos2, z(LANES - MLA_QK)], axis=1)
    q_sin = np.concatenate([z(MLA_NOPE), sin2, z(LANES - MLA_QK)], axis=1)
    k_cos = np.concatenate([cos2, z(LANES - MLA_ROPE)], axis=1)
    k_sin = np.concatenate([-sin2[:, :MLA_ROPE // 2], sin2[:, MLA_ROPE // 2:], z(LANES - MLA_ROPE)], axis=1)
    return tuple(jnp.asarray(a, F32) for a in (q_cos, q_sin, k_cos, k_sin))


def mla_weights(w_uq, w_ukv):
    hq = w_uq.reshape(MLA_Q_RANK, MLA_HEADS, MLA_QK)
    nope, pe = hq[..., :MLA_NOPE], hq[..., MLA_NOPE:]
    pe_rot = jnp.concatenate([-pe[..., MLA_ROPE // 2:], pe[..., :MLA_ROPE // 2]], axis=-1)
    zq = jnp.zeros((MLA_Q_RANK, MLA_HEADS, LANES - MLA_QK), w_uq.dtype)
    w1 = jnp.concatenate([nope, pe, zq], axis=-1).reshape(MLA_Q_RANK, MLA_HEADS * LANES)
    w2 = jnp.concatenate([jnp.zeros_like(nope), pe_rot, zq], axis=-1).reshape(MLA_Q_RANK, MLA_HEADS * LANES)
    hkv = w_ukv.reshape(MLA_KV_RANK, MLA_HEADS, MLA_NOPE + MLA_V)
    k_nope, v = hkv[..., :MLA_NOPE], hkv[..., MLA_NOPE:]
    zk = jnp.zeros((MLA_KV_RANK, MLA_HEADS, LANES - MLA_NOPE), w_ukv.dtype)
    wk = jnp.concatenate([k_nope, zk], axis=-1).reshape(MLA_KV_RANK, MLA_HEADS * LANES)
    zv = jnp.zeros((MLA_KV_RANK, MLA_HEADS, LANES - MLA_V), w_ukv.dtype)
    wv = jnp.concatenate([v, zv], axis=-1).reshape(MLA_KV_RANK, MLA_HEADS * LANES)
    place = np.zeros((LANES, MLA_HEADS * LANES), np.float32)
    ones = np.zeros((1, MLA_HEADS * LANES), np.float32)
    for h in range(MLA_HEADS):
        place[np.arange(MLA_ROPE), h * LANES + MLA_NOPE + np.arange(MLA_ROPE)] = 1.0
        ones[0, h * LANES + MLA_V] = 1.0
    return (w1.astype(BF16), w2.astype(BF16), wk.astype(BF16), wv.astype(BF16), jnp.asarray(place, BF16),
            jnp.asarray(ones, F32))


def _mla_q_kernel(c_ref, nw_ref, w1_ref, w2_ref, cos_ref, sin_ref, o_ref):
    cn = _rms(c_ref[...].astype(F32), nw_ref[...]).astype(BF16)
    cos = jnp.tile(cos_ref[...], (1, MLA_HEADS))
    sin = jnp.tile(sin_ref[...], (1, MLA_HEADS))
    q = _dot(cn, w1_ref[...]) * cos + _dot(cn, w2_ref[...]) * sin
    o_ref[...] = (q * (MLA_QK ** -0.5 * math.log2(math.e))).astype(o_ref.dtype)


def _mla_kv_kernel(c_ref, tail_ref, nw_ref, wk_ref, wv_ref, place_ref, ones_ref, cos_ref, sin_ref,
                   k_ref, v_ref):
    cn = _rms(c_ref[...].astype(F32), nw_ref[...]).astype(BF16)
    kr = tail_ref[...]
    half = MLA_ROPE // 2
    rot = jnp.where(_lane_lt(kr.shape, half), pltpu.roll(kr, LANES - half, 1), pltpu.roll(kr, half, 1))
    k_pe = (kr * cos_ref[...] + rot * sin_ref[...]).astype(BF16)
    k_ref[...] = (_dot(cn, wk_ref[...]) + _dot(k_pe, place_ref[...])).astype(k_ref.dtype)
    v_ref[...] = (_dot(cn, wv_ref[...]) + ones_ref[...]).astype(v_ref.dtype)


def mla_project(proj, tail, q_norm_w, kv_norm_w, w_uq, w_ukv, b, t, tm):
    n = b * t
    nb = t // tm
    w1, w2, wk, wv, place, ones = mla_weights(w_uq, w_ukv)
    q_cos, q_sin, k_cos, k_sin = mla_tables(t)
    wide = MLA_HEADS * LANES
    full = lambda shape: pl.BlockSpec(shape, lambda i: (0, 0))
    tab = pl.BlockSpec((tm, LANES), lambda i: (i % nb, 0))
    qf = pl.pallas_call(
        _mla_q_kernel,
        grid=(n // tm,),
        in_specs=[pl.BlockSpec((tm, SLAB), lambda i: (i, COL_CQ)), full((1, MLA_Q_RANK)),
                  full((MLA_Q_RANK, wide)), full((MLA_Q_RANK, wide)), tab, tab],
        out_specs=pl.BlockSpec((tm, wide), lambda i: (i, 0)),
        out_shape=jax.ShapeDtypeStruct((n, wide), BF16),
        compiler_params=_cparams(("parallel",)),
        name="mla_q_proj",
    )(proj, q_norm_w.reshape(1, MLA_Q_RANK), w1, w2, q_cos, q_sin)
    kf, vf = pl.pallas_call(
        _mla_kv_kernel,
        grid=(n // tm,),
        in_specs=[pl.BlockSpec((tm, SLAB), lambda i: (i, COL_CKV)),
                  pl.BlockSpec((tm, LANES), lambda i: (i, 0)), full((1, MLA_KV_RANK)),
                  full((MLA_KV_RANK, wide)), full((MLA_KV_RANK, wide)), full((LANES, wide)), full((1, wide)),
                  tab, tab],
        out_specs=[pl.BlockSpec((tm, wide), lambda i: (i, 0))] * 2,
        out_shape=[jax.ShapeDtypeStruct((n, wide), BF16)] * 2,
        compiler_params=_cparams(("parallel",)),
        name="mla_kv_proj",
    )(proj, tail, kv_norm_w.reshape(1, MLA_KV_RANK), wk, wv, place, ones, k_cos, k_sin)
    return qf, kf, vf


def _mla_attn_kernel(q_ref, k_ref, v_ref, o_ref, *, t, tk):
    tq = q_ref.shape[0]
    outs = []
    for hh in range(2):
        grp = slice(hh * LANES, (hh + 1) * LANES)
        q = q_ref[:, grp]

        def body(c, carry):
            m, acc = carry
            k0 = pl.multiple_of(c * tk, tk)
            s = _dot_nt(q, k_ref[pl.ds(k0, tk), grp])
            m_new = jnp.maximum(m, jnp.max(s, axis=-1, keepdims=True))
            p = jnp.exp2((s - m_new).astype(BF16))
            acc = jnp.exp2(m - m_new) * acc + _dot(p, v_ref[pl.ds(k0, tk), grp])
            return m_new, acc

        init = (jnp.full((tq, 1), -jnp.inf, F32), jnp.zeros((tq, LANES), F32))
        _, acc = lax.fori_loop(0, t // tk, body, init)
        outs.append(acc / acc[:, MLA_V:MLA_V + 1])
    first = _lane_lt((tq, LANES), MLA_V)
    o_ref[...] = jnp.where(first, outs[0], pltpu.roll(outs[1], MLA_V, 1)).astype(o_ref.dtype)


def mla_attention(qf, kf, vf, b, t, tq, tk):
    n = b * t
    nq = t // tq
    return pl.pallas_call(
        functools.partial(_mla_attn_kernel, t=t, tk=tk),
        grid=(b, N_PAIRS, nq),
        in_specs=[
            pl.BlockSpec((tq, 2 * LANES), lambda i, p, s: (i * nq + s, p)),
            pl.BlockSpec((t, 2 * LANES), lambda i, p, s: (i, p)),
            pl.BlockSpec((t, 2 * LANES), lambda i, p, s: (i, p)),
        ],
        out_specs=pl.BlockSpec((tq, HEAD_PAIR), lambda i, p, s: (i * nq + s, p)),
        out_shape=jax.ShapeDtypeStruct((n, SLAB), BF16),
        compiler_params=_cparams(("parallel", "parallel", "arbitrary")),
        name="mla_attention",
    )(qf, kf, vf)


def _in_proj_segments():
    sizes = (SSM_INNER, SSM_CONV_CH, 2 * SSM_HEADS, SLAB, SLAB, SLAB, MLA_Q_RANK, MLA_KV_RANK, MLA_ROPE,
             SLAB, SLAB, SLAB)
    off = [int(v) for v in np.concatenate([[0], np.cumsum(sizes)])]
    main = ((off[0], off[2]), (off[3], off[8]), (off[9], off[12]))
    tail = ((off[8], off[9]), (off[2], off[3]))
    return main, tail


def _in_proj_columns():
    main, tail = _in_proj_segments()
    cols = lambda segs: np.concatenate([np.arange(a, b) for a, b in segs])
    return cols(main), cols(tail)


def in_proj_weights(w_in_l):
    main, tail = _in_proj_segments()
    w_main = jnp.concatenate([w_in_l[:, a:b] for a, b in main], axis=1).astype(BF16)
    pad = jnp.zeros((D_MODEL, LANES - sum(b - a for a, b in tail)), w_in_l.dtype)
    w_tail = jnp.concatenate([w_in_l[:, a:b] for a, b in tail] + [pad], axis=1).astype(BF16)
    return w_main, w_tail


def mixers(proj, tail, p, l, b, t):
    xbc = conv_silu(proj, p["conv_w"][l], p["conv_b"][l], b, t)
    y_f = ssd_scan(xbc, tail, p["dt_bias"][l], p["a_log"][l], b, t, 0)
    y_b = ssd_scan(xbc, tail, p["dt_bias"][l], p["a_log"][l], b, t, 1)
    y_ssm = ssd_combine(y_f, y_b, xbc, proj, p["d_skip"][l], p["ssm_norm_w"][l], 1024)

    y_na = na_attention(proj, p["na_rpb"][l], b, t)

    qf, kf, vf = mla_project(proj, tail, p["mla_q_norm_w"][l], p["mla_kv_norm_w"][l],
                             p["mla_w_uq"][l], p["mla_w_ukv"][l], b, t, 512)
    y_mla = mla_attention(qf, kf, vf, b, t, 512, 512)

    qkv = rope_qkv(proj, b, t, 1024)
    outs, lses = zip(*[band_attention(qkv, b, t, w, d) for w, d in DIL_PAIRS])
    y_dil = dil_combine(outs, lses, 1024)
    return y_ssm, y_na, y_mla, y_dil


def kernel(x, attn_norm_w, w_in, conv_w, conv_b, a_log, dt_bias, d_skip, ssm_norm_w, na_rpb,
           mla_q_norm_w, mla_kv_norm_w, mla_w_uq, mla_w_ukv, w_o, ffn_norm_w, ffn_w_gate, ffn_w_up,
           ffn_w_down, router_w, exp_w_gate, exp_w_up, exp_w_down, final_norm_w):
    b, t, _ = x.shape
    n = b * t
    depth = w_in.shape[0]
    p = dict(conv_w=conv_w, conv_b=conv_b, a_log=a_log, dt_bias=dt_bias, d_skip=d_skip,
             ssm_norm_w=ssm_norm_w, na_rpb=na_rpb, mla_q_norm_w=mla_q_norm_w,
             mla_kv_norm_w=mla_kv_norm_w, mla_w_uq=mla_w_uq, mla_w_ukv=mla_w_ukv)
    x = x.reshape(n, D_MODEL)
    cast_rows = 256
    w_o_b = cast_bf16(w_o, cast_rows)
    ffn_b = [cast_bf16(w, cast_rows) for w in (ffn_w_gate, ffn_w_up, ffn_w_down)]
    exp_b = [cast_bf16(w, cast_rows) for w in (exp_w_gate, exp_w_up, exp_w_down)]
    moe_tm = 512
    normed = False
    for l in range(depth):
        w_main, w_tail = in_proj_weights(w_in[l])
        proj = norm_matmul(x, 0, D_MODEL, attn_norm_w[l], w_main, BF16, 1024, PROJ_MAIN // 4)
        tail = norm_matmul(x, 0, D_MODEL, attn_norm_w[l], w_tail, F32, 1024, LANES)
        mix = mixers(proj, tail, p, l, b, t)
        x = out_proj(mix, w_o_b, l, x, 1024, 1024)
        j = l // 2
        if l % 2 == 0:
            x = ffn_dense(x, ffn_norm_w[l], *ffn_b, j, 1024, 512)
        else:
            top_i, gates = moe_router(x, ffn_norm_w[l], router_w[j], 512)
            src, pos, tile_expert, tile_valid = moe_plan(top_i, moe_tm)
            y = moe_ffn(x, ffn_norm_w[l], *exp_b, j, src, tile_expert, tile_valid, moe_tm, 512)
            last = l == depth - 1
            x = moe_combine(x, gates, y, pos, final_norm_w if last else None, 256)
            normed = last
    if not normed:
        x = rmsnorm_rows(x, final_norm_w, 1024)
    return x.reshape(b, t, D_MODEL)
```

```python
import functools
import math

import numpy as np
import jax
import jax.numpy as jnp
from jax import lax
from jax.experimental import pallas as pl
from jax.experimental.pallas import tpu as pltpu

F32 = jnp.float32
BF16 = jnp.bfloat16

D_MODEL = 2048
GRID_W = 64
HEAD_DIM = 64
ROPE_THETA = 10000.0
NORM_EPS = 1e-6
NEG_INF = -1e30

SSM_HEADS = 8
SSM_HEAD_DIM = 64
SSM_INNER = SSM_HEADS * SSM_HEAD_DIM
SSM_GROUPS = 2
SSM_STATE = 128
SSM_CONV = 5
SSM_CHUNK = 128
SSM_CONV_CH = SSM_INNER + 2 * SSM_GROUPS * SSM_STATE

NA_HEADS = 8
NA_WIN_ROWS = 8
NA_WIN_COLS = 16
NA_COL_BLOCK = 16
NA_KEY_COLS = 32
NA_ROWS_PER_STEP = 4

MLA_HEADS = 8
MLA_Q_RANK = 512
MLA_KV_RANK = 512
MLA_NOPE = 64
MLA_ROPE = 32
MLA_V = 64

DIL_HEADS = 8
DIL_PAIRS = ((128, 1), (512, 4), (2048, 16))
DIL_QBLOCK = 128

N_EXPERTS = 8
TOP_K = 2

LANES = 128
HEAD_PAIR = 2 * HEAD_DIM
N_PAIRS = 4
SLAB = 512

COL_Z, COL_XBC, COL_NAQ, COL_NAK, COL_NAV, COL_CQ, COL_CKV, COL_DLQ, COL_DLK, COL_DLV = (
    0, 1, 3, 4, 5, 6, 7, 8, 9, 10)
PROJ_MAIN = 11 * SLAB
TAIL_DT = 32

VMEM_LIMIT = 56 * 1024 * 1024


def _cparams(sem, vmem=VMEM_LIMIT):
    return pltpu.CompilerParams(dimension_semantics=sem, vmem_limit_bytes=vmem)


def _lane_lt(shape, bound, period=None):
    lane = lax.broadcasted_iota(jnp.int32, shape, len(shape) - 1)
    if period is not None:
        lane = lane % period
    return lane < bound


def _rms(x, w):
    ms = jnp.mean(x * x, axis=-1, keepdims=True)
    return x * lax.rsqrt(ms + NORM_EPS) * w


def _dot(a, b):
    return jnp.dot(a, b, preferred_element_type=F32)


def _dot_nt(a, b):
    return lax.dot_general(a, b, (((1,), (1,)), ((), ())), preferred_element_type=F32)


def _dot_tn(a, b):
    return lax.dot_general(a, b, (((0,), (0,)), ((), ())), preferred_element_type=F32)


def _cast_kernel(x_ref, o_ref):
    o_ref[...] = x_ref[...].astype(o_ref.dtype)


def cast_bf16(w, tr):
    shape = w.shape
    w2 = w.reshape(-1, shape[-1])
    r, c = w2.shape
    out = pl.pallas_call(
        _cast_kernel,
        grid=(r // tr,),
        in_specs=[pl.BlockSpec((tr, c), lambda i: (i, 0))],
        out_specs=pl.BlockSpec((tr, c), lambda i: (i, 0)),
        out_shape=jax.ShapeDtypeStruct((r, c), BF16),
        compiler_params=_cparams(("parallel",)),
        name="cast_bf16",
    )(w2)
    return out.reshape(shape)


def _norm_matmul_kernel(x_ref, nw_ref, w_ref, o_ref, h_ref):
    @pl.when(pl.program_id(1) == 0)
    def _():
        h_ref[...] = _rms(x_ref[...].astype(F32), nw_ref[...]).astype(BF16)

    o_ref[...] = _dot(h_ref[...], w_ref[...]).astype(o_ref.dtype)


def norm_matmul(x, xcol, k, nw, w, out_dtype, tm, tn):
    n = x.shape[0]
    nout = w.shape[1]
    return pl.pallas_call(
        _norm_matmul_kernel,
        grid=(n // tm, nout // tn),
        in_specs=[
            pl.BlockSpec((tm, k), lambda i, j: (i, xcol)),
            pl.BlockSpec((1, k), lambda i, j: (0, 0)),
            pl.BlockSpec((k, tn), lambda i, j: (0, j)),
        ],
        out_specs=pl.BlockSpec((tm, tn), lambda i, j: (i, j)),
        out_shape=jax.ShapeDtypeStruct((n, nout), out_dtype),
        scratch_shapes=[pltpu.VMEM((tm, k), BF16)],
        compiler_params=_cparams(("parallel", "arbitrary")),
        name="norm_matmul",
    )(x, nw.reshape(1, k), w)


def _out_proj_kernel(a0_ref, a1_ref, a2_ref, a3_ref, w_ref, r_ref, o_ref):
    acc = r_ref[...]
    for s, a_ref in enumerate((a0_ref, a1_ref, a2_ref, a3_ref)):
        acc = acc + _dot(a_ref[...], w_ref[s * SLAB:(s + 1) * SLAB, :])
    o_ref[...] = acc


def out_proj(mix, w, layer, res, tm, tn):
    n = res.shape[0]
    return pl.pallas_call(
        _out_proj_kernel,
        grid=(n // tm, D_MODEL // tn),
        in_specs=[pl.BlockSpec((tm, SLAB), lambda i, j: (i, 0))] * 4 + [
            pl.BlockSpec((None, 4 * SLAB, tn), lambda i, j: (layer, 0, j)),
            pl.BlockSpec((tm, tn), lambda i, j: (i, j)),
        ],
        out_specs=pl.BlockSpec((tm, tn), lambda i, j: (i, j)),
        out_shape=jax.ShapeDtypeStruct((n, D_MODEL), F32),
        compiler_params=_cparams(("parallel", "arbitrary")),
        name="out_proj",
    )(*mix, w, res)


def _ffn_kernel(x_ref, nw_ref, wg_ref, wu_ref, wd_ref, o_ref, h_ref):
    @pl.when(pl.program_id(1) == 0)
    def _():
        x = x_ref[...]
        h_ref[...] = _rms(x, nw_ref[...]).astype(BF16)
        o_ref[...] = x

    _swiglu_rows(h_ref, wg_ref, wu_ref, wd_ref, o_ref)


FFN_ROW_CHUNK = 512


def _swiglu_rows(h_ref, wg_ref, wu_ref, wd_ref, o_ref):
    tm = h_ref.shape[0]
    for r0 in range(0, tm, FFN_ROW_CHUNK):
        rows = slice(r0, r0 + FFN_ROW_CHUNK)
        h = h_ref[rows, :]
        g = _dot(h, wg_ref[...])
        u = _dot(h, wu_ref[...])
        a = (g * jax.nn.sigmoid(g) * u).astype(BF16)
        o_ref[rows, :] += _dot(a, wd_ref[...])


def ffn_dense(x, nw, wg, wu, wd, layer, tm, tf):
    n = x.shape[0]
    d_ff = wg.shape[-1]
    return pl.pallas_call(
        _ffn_kernel,
        grid=(n // tm, d_ff // tf),
        in_specs=[
            pl.BlockSpec((tm, D_MODEL), lambda i, j: (i, 0), pipeline_mode=pl.Buffered(1)),
            pl.BlockSpec((1, D_MODEL), lambda i, j: (0, 0)),
            pl.BlockSpec((None, D_MODEL, tf), lambda i, j: (layer, 0, j)),
            pl.BlockSpec((None, D_MODEL, tf), lambda i, j: (layer, 0, j)),
            pl.BlockSpec((None, tf, D_MODEL), lambda i, j: (layer, j, 0)),
        ],
        out_specs=pl.BlockSpec((tm, D_MODEL), lambda i, j: (i, 0)),
        out_shape=jax.ShapeDtypeStruct((n, D_MODEL), F32),
        scratch_shapes=[pltpu.VMEM((tm, D_MODEL), BF16)],
        compiler_params=_cparams(("parallel", "arbitrary")),
        name="ffn_dense",
    )(x, nw.reshape(1, D_MODEL), wg, wu, wd)


def _router_kernel(x_ref, nw_ref, rw_ref, idx_ref, gate_ref):
    h = _rms(x_ref[...], nw_ref[...])
    logits = jnp.dot(h, rw_ref[...], preferred_element_type=F32, precision=lax.Precision.HIGHEST)
    lane = lax.broadcasted_iota(jnp.int32, logits.shape, 1)
    logits = jnp.where(lane < N_EXPERTS, logits, -jnp.inf)
    m1 = jnp.max(logits, axis=-1, keepdims=True)
    i1 = jnp.min(jnp.where(logits == m1, lane, LANES), axis=-1, keepdims=True)
    rest = jnp.where(lane == i1, -jnp.inf, logits)
    m2 = jnp.max(rest, axis=-1, keepdims=True)
    i2 = jnp.min(jnp.where(rest == m2, lane, LANES), axis=-1, keepdims=True)
    e2 = jnp.exp(m2 - m1)
    g1 = 1.0 / (1.0 + e2)
    g2 = e2 / (1.0 + e2)
    idx_ref[...] = jnp.where(lane == 0, i1, i2)[:, :TOP_K]
    gate_ref[...] = jnp.where(lane == 0, g1, g2)[:, :TOP_K]


def moe_router(x, nw, router_w, tm):
    n = x.shape[0]
    rw = jnp.zeros((D_MODEL, LANES), F32).at[:, :N_EXPERTS].set(router_w)
    return pl.pallas_call(
        _router_kernel,
        grid=(n // tm,),
        in_specs=[
            pl.BlockSpec((tm, D_MODEL), lambda i: (i, 0)),
            pl.BlockSpec((1, D_MODEL), lambda i: (0, 0)),
            pl.BlockSpec((D_MODEL, LANES), lambda i: (0, 0)),
        ],
        out_specs=[pl.BlockSpec((tm, TOP_K), lambda i: (i, 0)),
                   pl.BlockSpec((tm, TOP_K), lambda i: (i, 0))],
        out_shape=[jax.ShapeDtypeStruct((n, TOP_K), jnp.int32),
                   jax.ShapeDtypeStruct((n, TOP_K), F32)],
        compiler_params=_cparams(("parallel",)),
        name="moe_router",
    )(x, nw.reshape(1, D_MODEL), rw)


def moe_plan(top_i, tm):
    n = top_i.shape[0]
    flat_e = top_i.reshape(-1)
    onehot = (flat_e[:, None] == jnp.arange(N_EXPERTS, dtype=jnp.int32)[None, :]).astype(jnp.int32)
    csum = jnp.cumsum(onehot, axis=0)
    counts = csum[-1]
    rank = jnp.sum(onehot * csum, axis=1) - 1
    padded = ((counts + tm - 1) // tm) * tm
    pend = jnp.cumsum(padded)
    pstart = pend - padded
    pos = pstart[flat_e] + rank
    n_slots = n * TOP_K + N_EXPERTS * tm
    n_tiles = n_slots // tm
    src = jnp.zeros((n_slots,), jnp.int32).at[pos].set(jnp.arange(n * TOP_K, dtype=jnp.int32) // TOP_K)
    tile_start = jnp.arange(n_tiles, dtype=jnp.int32) * tm
    tile_expert = jnp.sum((tile_start[:, None] >= pend[None, :]).astype(jnp.int32), axis=1)
    tile_valid = (tile_start < pend[-1]).astype(jnp.int32)
    last_valid = jnp.maximum(pend[-1] // tm - 1, 0)
    tile_expert = jnp.where(tile_valid == 1, tile_expert, tile_expert[last_valid]).astype(jnp.int32)
    return src, pos.reshape(n, TOP_K).astype(jnp.int32), tile_expert, tile_valid


def _moe_ffn_kernel(te_ref, tv_ref, src_ref, nsrc_ref, x_hbm, nw_ref, wg_ref, wu_ref, wd_ref, y_ref,
                    xbuf, h_ref, sem, *, tm):
    i = pl.program_id(0)
    j = pl.program_id(1)
    n_tiles = pl.num_programs(0)
    valid = tv_ref[i] == 1

    def row_copy(idx_ref, r):
        tok = idx_ref[0, 0, r]
        return pltpu.make_async_copy(x_hbm.at[pl.ds(tok, 1)], xbuf.at[pl.ds(r, 1)], sem)

    def gather(idx_ref):
        def start(r, c):
            row_copy(idx_ref, r).start()
            return c

        lax.fori_loop(0, tm, start, 0)

    @pl.when(j == 0)
    def _():
        y_ref[...] = jnp.zeros_like(y_ref)

    @pl.when(jnp.logical_and(j == 0, valid))
    def _():
        @pl.when(i == 0)
        def _():
            gather(src_ref)

        pltpu.make_async_copy(x_hbm.at[pl.ds(0, tm)], xbuf, sem).wait()
        h_ref[...] = _rms(xbuf[...], nw_ref[...]).astype(BF16)

        nxt = jnp.minimum(i + 1, n_tiles - 1)

        @pl.when(jnp.logical_and(i + 1 < n_tiles, tv_ref[nxt] == 1))
        def _():
            gather(nsrc_ref)

    @pl.when(valid)
    def _():
        _swiglu_rows(h_ref, wg_ref, wu_ref, wd_ref, y_ref)


def moe_ffn(x, nw, wg, wu, wd, layer, src, tile_expert, tile_valid, tm, tf):
    n_slots = src.shape[0]
    n_tiles = n_slots // tm
    d_ff = wg.shape[-1]
    nf = d_ff // tf

    def wcol(i, j, te_ref, tv_ref):
        return (layer, te_ref[i], 0, jnp.where(tv_ref[i] == 1, j, nf - 1))

    def wrow(i, j, te_ref, tv_ref):
        return (layer, te_ref[i], jnp.where(tv_ref[i] == 1, j, nf - 1), 0)

    grid_spec = pltpu.PrefetchScalarGridSpec(
        num_scalar_prefetch=2,
        grid=(n_tiles, nf),
        in_specs=[
            pl.BlockSpec((1, 1, tm), lambda i, j, *_: (i, 0, 0), memory_space=pltpu.SMEM),
            pl.BlockSpec((1, 1, tm), lambda i, j, *_: (jnp.minimum(i + 1, n_tiles - 1), 0, 0),
                         memory_space=pltpu.SMEM),
            pl.BlockSpec(memory_space=pl.ANY),
            pl.BlockSpec((1, D_MODEL), lambda i, j, *_: (0, 0)),
            pl.BlockSpec((None, None, D_MODEL, tf), wcol),
            pl.BlockSpec((None, None, D_MODEL, tf), wcol),
            pl.BlockSpec((None, None, tf, D_MODEL), wrow),
        ],
        out_specs=pl.BlockSpec((tm, D_MODEL), lambda i, j, *_: (i, 0)),
        scratch_shapes=[pltpu.VMEM((tm, D_MODEL), F32), pltpu.VMEM((tm, D_MODEL), BF16),
                        pltpu.SemaphoreType.DMA],
    )
    src3 = src.reshape(n_tiles, 1, tm)
    return pl.pallas_call(
        functools.partial(_moe_ffn_kernel, tm=tm),
        grid_spec=grid_spec,
        out_shape=jax.ShapeDtypeStruct((n_slots, D_MODEL), F32),
        compiler_params=_cparams(("arbitrary", "arbitrary")),
        name="moe_ffn",
    )(tile_expert, tile_valid, src3, src3, x, nw.reshape(1, D_MODEL), wg, wu, wd)


def _moe_combine_kernel(pos_ref, npos_ref, x_ref, gate_ref, y_hbm, fw_ref, o_ref, ybuf, sem, *, tm, final_norm):
    i = pl.program_id(0)
    n_tiles = pl.num_programs(0)
    cur = i % 2

    def row_copy(idx_ref, buf, r, k):
        slot = idx_ref[0, 0, r * TOP_K + k]
        return pltpu.make_async_copy(y_hbm.at[pl.ds(slot, 1)], ybuf.at[buf, k, pl.ds(r, 1)], sem.at[buf])

    def gather(idx_ref, buf):
        def start(r, c):
            for k in range(TOP_K):
                row_copy(idx_ref, buf, r, k).start()
            return c

        lax.fori_loop(0, tm, start, 0)

    @pl.when(i == 0)
    def _():
        gather(pos_ref, 0)

    @pl.when(i + 1 < n_tiles)
    def _():
        gather(npos_ref, 1 - cur)

    for k in range(TOP_K):
        pltpu.make_async_copy(y_hbm.at[pl.ds(0, tm)], ybuf.at[cur, k], sem.at[cur]).wait()
    gates = gate_ref[...]
    out = x_ref[...]
    for k in range(TOP_K):
        out = out + gates[:, k:k + 1] * ybuf[cur, k]
    if final_norm:
        out = _rms(out, fw_ref[...])
    o_ref[...] = out


def moe_combine(x, gates, y, pos, final_w, tm):
    n = x.shape[0]
    final_norm = final_w is not None
    fw = final_w if final_norm else jnp.ones((D_MODEL,), F32)
    n_tiles = n // tm
    pos3 = pos.reshape(n_tiles, 1, tm * TOP_K)
    return pl.pallas_call(
        functools.partial(_moe_combine_kernel, tm=tm, final_norm=final_norm),
        grid=(n_tiles,),
        in_specs=[
            pl.BlockSpec((1, 1, tm * TOP_K), lambda i: (i, 0, 0), memory_space=pltpu.SMEM),
            pl.BlockSpec((1, 1, tm * TOP_K), lambda i: (jnp.minimum(i + 1, n_tiles - 1), 0, 0),
                         memory_space=pltpu.SMEM),
            pl.BlockSpec((tm, D_MODEL), lambda i: (i, 0)),
            pl.BlockSpec((tm, TOP_K), lambda i: (i, 0)),
            pl.BlockSpec(memory_space=pl.ANY),
            pl.BlockSpec((1, D_MODEL), lambda i: (0, 0)),
        ],
        out_specs=pl.BlockSpec((tm, D_MODEL), lambda i: (i, 0)),
        out_shape=jax.ShapeDtypeStruct((n, D_MODEL), F32),
        scratch_shapes=[pltpu.VMEM((2, TOP_K, tm, D_MODEL), F32), pltpu.SemaphoreType.DMA((2,))],
        compiler_params=_cparams(("arbitrary",)),
        name="moe_combine",
    )(pos3, pos3, x, gates, y, fw.reshape(1, D_MODEL))


def _rmsnorm_kernel(x_ref, w_ref, o_ref):
    o_ref[...] = _rms(x_ref[...], w_ref[...])


def rmsnorm_rows(x, w, tm):
    n = x.shape[0]
    return pl.pallas_call(
        _rmsnorm_kernel,
        grid=(n // tm,),
        in_specs=[pl.BlockSpec((tm, D_MODEL), lambda i: (i, 0)),
                  pl.BlockSpec((1, D_MODEL), lambda i: (0, 0))],
        out_specs=pl.BlockSpec((tm, D_MODEL), lambda i: (i, 0)),
        out_shape=jax.ShapeDtypeStruct((n, D_MODEL), F32),
        compiler_params=_cparams(("parallel",)),
        name="final_norm",
    )(x, w.reshape(1, D_MODEL))


CONV_PAD = 8


def _conv_kernel(x_ref, w_ref, b_ref, o_ref, pad_ref, *, t):
    half = SSM_CONV // 2
    zeros = jnp.zeros((CONV_PAD, pad_ref.shape[1]), F32)
    pad_ref[0:CONV_PAD, :] = zeros
    pad_ref[CONV_PAD + t:CONV_PAD + t + CONV_PAD, :] = zeros
    pad_ref[CONV_PAD:CONV_PAD + t, :] = x_ref[...].astype(F32)
    acc = jnp.zeros(o_ref.shape, F32) + b_ref[...]
    for k in range(SSM_CONV):
        acc = acc + pad_ref[pl.ds(CONV_PAD - half + k, t), :] * w_ref[k:k + 1, :]
    o_ref[...] = (acc * jax.nn.sigmoid(acc)).astype(o_ref.dtype)


def conv_silu(proj, conv_w, conv_b, b, t):
    tc = 256
    nblk = SSM_CONV_CH // tc
    col0 = COL_XBC * SLAB // tc
    return pl.pallas_call(
        functools.partial(_conv_kernel, t=t),
        grid=(b, nblk),
        in_specs=[
            pl.BlockSpec((t, tc), lambda i, j: (i, col0 + j)),
            pl.BlockSpec((SSM_CONV, tc), lambda i, j: (0, j)),
            pl.BlockSpec((1, tc), lambda i, j: (0, j)),
        ],
        out_specs=pl.BlockSpec((t, tc), lambda i, j: (i, j)),
        out_shape=jax.ShapeDtypeStruct((b * t, SSM_CONV_CH), BF16),
        scratch_shapes=[pltpu.VMEM((t + 2 * CONV_PAD, tc), F32)],
        compiler_params=_cparams(("parallel", "parallel")),
        name="conv_silu",
    )(proj, conv_w, conv_b.reshape(1, SSM_CONV_CH))


def _ssd_kernel(xbc_ref, tail_ref, bias_ref, alog_ref, tri_ref, y_ref, state_ref, *, direction):
    q = SSM_CHUNK

    @pl.when(pl.program_id(1) == 0)
    def _():
        state_ref[...] = jnp.zeros_like(state_ref)

    dt = jax.nn.softplus(tail_ref[...] + bias_ref[...])
    da = dt * (-jnp.exp(alog_ref[...]))
    cs = jnp.dot(tri_ref[...], da, preferred_element_type=F32, precision=lax.Precision.HIGHEST)
    total = cs[q - 1:q, :]
    if direction == 0:
        e_out = cs
        e_in = total - cs
        e_seg = cs
    else:
        ex = cs - da
        e_out = total - ex
        e_in = ex
        e_seg = -ex
    e_seg_t = jnp.transpose(e_seg)
    dec_out = jnp.exp(e_out)
    dec_in_dt = jnp.exp(e_in) * dt
    dt_t = jnp.transpose(dt)
    dec_tot = jnp.exp(total)

    row = lax.broadcasted_iota(jnp.int32, (q, q), 0)
    col = lax.broadcasted_iota(jnp.int32, (q, q), 1)
    keep = (row >= col) if direction == 0 else (col >= row)
    first_half = _lane_lt((q, HEAD_PAIR), HEAD_DIM)

    xs = xbc_ref[:, 0:SSM_INNER]
    heads_per_group = SSM_HEADS // SSM_GROUPS
    for g in range(SSM_GROUPS):
        bm = xbc_ref[:, SSM_INNER + g * SSM_STATE:SSM_INNER + (g + 1) * SSM_STATE]
        cm = xbc_ref[:, SSM_INNER + (SSM_GROUPS + g) * SSM_STATE:SSM_INNER + (SSM_GROUPS + g + 1) * SSM_STATE]
        cb = _dot_nt(cm, bm)
        bm_f = bm.astype(F32)
        cm_f = cm.astype(F32)
        for pp in range(heads_per_group // 2):
            pair = g * (heads_per_group // 2) + pp
            xs_pair = xs[:, pair * HEAD_PAIR:(pair + 1) * HEAD_PAIR]
            st = state_ref[pair]
            ys, sts, decs = [], [], []
            for hh in range(2):
                lane = TAIL_DT + direction * SSM_HEADS + pair * 2 + hh
                seg = e_seg[:, lane:lane + 1] - e_seg_t[lane:lane + 1, :]
                lmat = jnp.where(keep, jnp.exp(seg), 0.0)
                w = (cb * lmat * dt_t[lane:lane + 1, :]).astype(BF16)
                y = _dot(w, xs_pair)
                y = y + _dot((cm_f * dec_out[:, lane:lane + 1]).astype(BF16), st.astype(BF16))
                ys.append(y)
                sts.append(_dot_tn((bm_f * dec_in_dt[:, lane:lane + 1]).astype(BF16), xs_pair))
                decs.append(dec_tot[:, lane:lane + 1])
            y_ref[:, pair * HEAD_PAIR:(pair + 1) * HEAD_PAIR] = jnp.where(first_half, ys[0], ys[1])
            first_half_s = _lane_lt(st.shape, HEAD_DIM)
            state_ref[pair] = (st * jnp.where(first_half_s, decs[0], decs[1])
                               + jnp.where(first_half_s, sts[0], sts[1]))


def ssd_scan(xbc, tail, dt_bias, a_log, b, t, direction):
    q = SSM_CHUNK
    nc = t // q
    bias_row = jnp.zeros((1, LANES), F32).at[0, TAIL_DT:TAIL_DT + 2 * SSM_HEADS].set(dt_bias.reshape(-1))
    alog_row = jnp.zeros((1, LANES), F32).at[0, TAIL_DT:TAIL_DT + 2 * SSM_HEADS].set(a_log.reshape(-1))
    tri = jnp.asarray(np.tril(np.ones((q, q), np.float32)))

    def chunk(i, c):
        return i * nc + (c if direction == 0 else nc - 1 - c)

    return pl.pallas_call(
        functools.partial(_ssd_kernel, direction=direction),
        grid=(b, nc),
        in_specs=[
            pl.BlockSpec((q, SSM_CONV_CH), lambda i, c: (chunk(i, c), 0)),
            pl.BlockSpec((q, LANES), lambda i, c: (chunk(i, c), 0)),
            pl.BlockSpec((1, LANES), lambda i, c: (0, 0)),
            pl.BlockSpec((1, LANES), lambda i, c: (0, 0)),
            pl.BlockSpec((q, q), lambda i, c: (0, 0)),
        ],
        out_specs=pl.BlockSpec((q, SSM_INNER), lambda i, c: (chunk(i, c), 0)),
        out_shape=jax.ShapeDtypeStruct((b * t, SSM_INNER), F32),
        scratch_shapes=[pltpu.VMEM((N_PAIRS, SSM_STATE, HEAD_PAIR), F32)],
        compiler_params=_cparams(("parallel", "arbitrary")),
        name="ssd_scan_%s" % ("fwd" if direction == 0 else "bwd"),
    )(xbc, tail, bias_row, alog_row, tri)


def _ssd_combine_kernel(yf_ref, yb_ref, xs_ref, z_ref, d_ref, nw_ref, o_ref):
    y = yf_ref[...] + yb_ref[...] + xs_ref[...].astype(F32) * d_ref[...]
    z = z_ref[...].astype(F32)
    o_ref[...] = _rms(y * (z * jax.nn.sigmoid(z)), nw_ref[...]).astype(o_ref.dtype)


def ssd_combine(y_f, y_b, xbc, proj, d_skip, norm_w, tm):
    n = y_f.shape[0]
    d_row = jnp.repeat(d_skip, SSM_HEAD_DIM).reshape(1, SSM_INNER)
    row = lambda i: (i, 0)
    return pl.pallas_call(
        _ssd_combine_kernel,
        grid=(n // tm,),
        in_specs=[
            pl.BlockSpec((tm, SSM_INNER), row),
            pl.BlockSpec((tm, SSM_INNER), row),
            pl.BlockSpec((tm, SSM_INNER), row),
            pl.BlockSpec((tm, SLAB), lambda i: (i, COL_Z)),
            pl.BlockSpec((1, SSM_INNER), lambda i: (0, 0)),
            pl.BlockSpec((1, SSM_INNER), lambda i: (0, 0)),
        ],
        out_specs=pl.BlockSpec((tm, SSM_INNER), row),
        out_shape=jax.ShapeDtypeStruct((n, SSM_INNER), BF16),
        compiler_params=_cparams(("parallel",)),
        name="ssd_combine",
    )(y_f, y_b, xbc, proj, d_row, norm_w.reshape(1, SSM_INNER))


def _pair_softmax_attend(q2, k2, v2, bias_fn):
    tq = q2.shape[0]
    first_q = _lane_lt(q2.shape, HEAD_DIM)
    zero = jnp.zeros_like(q2)
    qs = jnp.concatenate([jnp.where(first_q, q2, zero), jnp.where(first_q, zero, q2)], axis=0)
    s = bias_fn(_dot_nt(qs, k2))
    m = jnp.max(s, axis=-1, keepdims=True)
    p = jnp.exp(s - m)
    l = jnp.sum(p, axis=-1, keepdims=True)
    o = _dot(p.astype(BF16), v2) / l
    lse = m + jnp.log(l)
    first_o = _lane_lt((tq, HEAD_PAIR), HEAD_DIM)
    return jnp.where(first_o, o[:tq], o[tq:]), jnp.where(first_o, lse[:tq], lse[tq:])


def na_bias_tables(rpb, rows):
    kr = min(NA_WIN_ROWS, rows)
    qc = np.arange(GRID_W)
    kc = np.arange(GRID_W)
    q_start = np.clip(qc - NA_WIN_COLS // 2, 0, GRID_W - NA_WIN_COLS)
    col_in = (kc[None, :] >= q_start[:, None]) & (kc[None, :] < q_start[:, None] + NA_WIN_COLS)
    col_off = np.clip(kc[None, :] - qc[:, None] + NA_WIN_COLS - 1, 0, 2 * NA_WIN_COLS - 2)
    onehot = (col_off[None] == np.arange(2 * NA_WIN_COLS - 1)[:, None, None]).astype(np.float32)
    expanded = jnp.einsum("hrc,cqk->hqrk", rpb, jnp.asarray(onehot), precision=lax.Precision.HIGHEST)
    expanded = jnp.where(jnp.asarray(col_in)[None, :, None, :], expanded, NEG_INF)

    def table(r):
        row_start = int(np.clip(r - kr // 2, 0, rows - kr))
        ro0 = row_start - r + NA_WIN_ROWS - 1
        return expanded[:, :, ro0:ro0 + kr, :].reshape(N_PAIRS, 2 * GRID_W, kr * GRID_W)

    rs = NA_ROWS_PER_STEP
    lo = [table(r) for r in range(rs)]
    mid = [table(min(rs, rows - 1))] * rs
    hi = [table(r) for r in range(rows - rs, rows)]
    return jnp.stack([jnp.stack(lo), jnp.stack(mid), jnp.stack(hi)])


def _na_kernel(q_ref, k_ref, v_ref, bias_ref, o_ref, *, rows, kr):
    step = pl.program_id(1)
    rs = NA_ROWS_PER_STEP
    for rr in range(rs):
        r = step * rs + rr
        row_start = jnp.clip(r - kr // 2, 0, rows - kr)
        k0 = pl.multiple_of(row_start * GRID_W, GRID_W)
        for pair in range(N_PAIRS):
            cols = slice(pair * HEAD_PAIR, (pair + 1) * HEAD_PAIR)
            q2 = q_ref[rr * GRID_W:(rr + 1) * GRID_W, cols] * jnp.asarray(HEAD_DIM ** -0.5, BF16)
            k2 = k_ref[pl.ds(k0, kr * GRID_W), cols]
            v2 = v_ref[pl.ds(k0, kr * GRID_W), cols]
            o, _ = _pair_softmax_attend(q2, k2, v2, lambda s: s + bias_ref[0, rr, pair])
            o_ref[rr * GRID_W:(rr + 1) * GRID_W, cols] = o.astype(o_ref.dtype)


def na_attention(proj, rpb, b, t):
    rows = t // GRID_W
    kr = min(NA_WIN_ROWS, rows)
    rs = NA_ROWS_PER_STEP
    nsteps = rows // rs
    bias = na_bias_tables(rpb, rows)

    def kind(i, s):
        return jnp.where(s == 0, 0, jnp.where(s == nsteps - 1, 2, 1))

    return pl.pallas_call(
        functools.partial(_na_kernel, rows=rows, kr=kr),
        grid=(b, nsteps),
        in_specs=[
            pl.BlockSpec((rs * GRID_W, SLAB), lambda i, s: (i * nsteps + s, COL_NAQ)),
            pl.BlockSpec((t, SLAB), lambda i, s: (i, COL_NAK)),
            pl.BlockSpec((t, SLAB), lambda i, s: (i, COL_NAV)),
            pl.BlockSpec((1, rs, N_PAIRS, 2 * GRID_W, kr * GRID_W), lambda i, s: (kind(i, s), 0, 0, 0, 0)),
        ],
        out_specs=pl.BlockSpec((rs * GRID_W, SLAB), lambda i, s: (i * nsteps + s, 0)),
        out_shape=jax.ShapeDtypeStruct((b * t, SLAB), BF16),
        compiler_params=_cparams(("parallel", "arbitrary")),
        name="na_attention",
    )(proj, proj, proj, bias)


def _rope_angles(t, d):
    inv = ROPE_THETA ** (-np.arange(0, d, 2, dtype=np.float32) / d)
    return np.arange(t, dtype=np.float32)[:, None] * inv[None, :]


def rope_tables_pair(t):
    ang = _rope_angles(t, HEAD_DIM)
    cos = np.tile(np.cos(ang), (1, 4))
    sin = np.tile(np.concatenate([-np.sin(ang), np.sin(ang)], axis=1), (1, 2))
    return jnp.asarray(cos, F32), jnp.asarray(sin, F32)


FOLD_CHUNK = 256
FOLD_DILS = tuple(d for _, d in DIL_PAIRS if d > 1)


def fold_permutation(dil):
    per = FOLD_CHUNK // dil
    perm = np.zeros((FOLD_CHUNK, FOLD_CHUNK), np.float32)
    dst = np.arange(FOLD_CHUNK)
    perm[dst, (dst % per) * dil + dst // per] = 1.0
    return jnp.asarray(perm, BF16)


def _rope_qkv_kernel(x_ref, v_ref, cos_ref, sin_ref, *rest):
    nd = len(FOLD_DILS)
    perm_refs, o_ref, fold_refs = rest[:nd], rest[nd], rest[nd + 1:]
    cos = cos_ref[...]
    sin = sin_ref[...]
    half = HEAD_DIM // 2
    for c in range(x_ref.shape[1] // LANES):
        x = x_ref[:, c * LANES:(c + 1) * LANES].astype(F32)
        rot = jnp.where(_lane_lt(x.shape, half, HEAD_DIM),
                        pltpu.roll(x, LANES - half, 1), pltpu.roll(x, half, 1))
        y = x * cos + rot * sin
        if c < N_PAIRS:
            y = y * (HEAD_DIM ** -0.5)
        o_ref[0, :, c * LANES:(c + 1) * LANES] = y.astype(o_ref.dtype)
    o_ref[0, :, 2 * SLAB:3 * SLAB] = v_ref[...]
    tm = x_ref.shape[0]
    for dil, perm_ref, f_ref in zip(FOLD_DILS, perm_refs, fold_refs):
        per = FOLD_CHUNK // dil
        for c in range(tm // FOLD_CHUNK):
            folded = _dot(perm_ref[...], o_ref[0, c * FOLD_CHUNK:(c + 1) * FOLD_CHUNK, :]).astype(f_ref.dtype)
            for p in range(dil):
                f_ref[p, c * per:(c + 1) * per, :] = folded[p * per:(p + 1) * per, :]


def rope_qkv(proj, b, t, tm):
    n = b * t
    cos, sin = rope_tables_pair(t)
    nb = t // tm
    fold_spec = lambda d: pl.BlockSpec((None, d, tm // d, 3 * SLAB), lambda i: (i // nb, 0, i % nb, 0))
    outs = pl.pallas_call(
        _rope_qkv_kernel,
        grid=(n // tm,),
        in_specs=[
            pl.BlockSpec((tm, 2 * SLAB), lambda i: (i, COL_DLQ // 2)),
            pl.BlockSpec((tm, SLAB), lambda i: (i, COL_DLV)),
            pl.BlockSpec((tm, LANES), lambda i: (i % nb, 0)),
            pl.BlockSpec((tm, LANES), lambda i: (i % nb, 0)),
        ] + [pl.BlockSpec((FOLD_CHUNK, FOLD_CHUNK), lambda i: (0, 0))] * len(FOLD_DILS),
        out_specs=[fold_spec(1)] + [fold_spec(d) for d in FOLD_DILS],
        out_shape=[jax.ShapeDtypeStruct((b, d, t // d, 3 * SLAB), BF16) for d in (1,) + FOLD_DILS],
        compiler_params=_cparams(("parallel",)),
        name="rope_qkv",
    )(proj, proj, cos, sin, *[fold_permutation(d) for d in FOLD_DILS])
    by_dil = dict(zip((1,) + FOLD_DILS, outs))
    return [by_dil[d] for _, d in DIL_PAIRS]


def _band_kernel(q_ref, k_ref, v_ref, o_ref, lse_ref, *, sub, half, span):
    qb = pl.program_id(2)
    tq = q_ref.shape[0]
    start = jnp.clip(qb * tq - half, 0, sub - span)
    start = pl.multiple_of(start, half)
    q_pos = qb * tq + lax.broadcasted_iota(jnp.int32, (2 * tq, span), 0) % tq
    k_pos = start + lax.broadcasted_iota(jnp.int32, (2 * tq, span), 1)
    valid = jnp.abs(k_pos - q_pos) <= half
    for pair in range(N_PAIRS):
        cols = slice(pair * HEAD_PAIR, (pair + 1) * HEAD_PAIR)
        k2 = k_ref[pl.ds(start, span), cols]
        v2 = v_ref[pl.ds(start, span), cols]
        o, lse = _pair_softmax_attend(q_ref[:, cols], k2, v2, lambda s: jnp.where(valid, s, NEG_INF))
        o_ref[:, cols] = o
        lse_ref[:, cols] = lse


def band_attention(qkv, window, dil):
    b, _, sub, _ = qkv.shape
    half = window // (2 * dil)
    tq = DIL_QBLOCK
    span = tq + 2 * half
    nqb = sub // tq
    return pl.pallas_call(
        functools.partial(_band_kernel, sub=sub, half=half, span=span),
        grid=(b, dil, nqb),
        in_specs=[
            pl.BlockSpec((None, None, tq, SLAB), lambda i, p, s: (i, p, s, 0)),
            pl.BlockSpec((None, None, sub, SLAB), lambda i, p, s: (i, p, 0, 1)),
            pl.BlockSpec((None, None, sub, SLAB), lambda i, p, s: (i, p, 0, 2)),
        ],
        out_specs=[pl.BlockSpec((None, None, tq, SLAB), lambda i, p, s: (i, p, s, 0))] * 2,
        out_shape=[jax.ShapeDtypeStruct((b, dil, sub, SLAB), F32)] * 2,
        compiler_params=_cparams(("parallel", "parallel", "arbitrary")),
        name="band_attention_d%d" % dil,
    )(qkv, qkv, qkv)


def _dil_combine_kernel(*refs):
    nbr = len(DIL_PAIRS)
    o_refs, l_refs, out_ref = refs[:nbr], refs[nbr:2 * nbr], refs[2 * nbr]
    scratch = iter(refs[2 * nbr + 1:])

    def token_order(ref):
        dil = ref.shape[0]
        if dil == 1:
            return ref[0]
        buf = next(scratch)
        per = ref.shape[1]
        for p in range(dil):
            for c in range(SLAB // LANES):
                buf[c, pl.ds(p, per, stride=dil), :] = ref[p, :, c * LANES:(c + 1) * LANES]
        return jnp.concatenate([buf[c] for c in range(SLAB // LANES)], axis=1)

    os = [token_order(r) for r in o_refs]
    lses = [token_order(r) for r in l_refs]
    m = functools.reduce(jnp.maximum, lses)
    ws = [jnp.exp(l - m) for l in lses]
    den = functools.reduce(jnp.add, ws)
    acc = functools.reduce(jnp.add, [(w / den) * o for w, o in zip(ws, os)])
    out_ref[...] = acc.astype(out_ref.dtype)


def dil_combine(outs, lses, tm):
    b, _, t, _ = outs[0].shape
    n = b * t
    nb = t // tm
    spec = lambda a: pl.BlockSpec((None, a.shape[1], tm // a.shape[1], SLAB), lambda i: (i // nb, 0, i % nb, 0))
    n_folded = sum(1 for a in outs + lses if a.shape[1] > 1)
    return pl.pallas_call(
        _dil_combine_kernel,
        grid=(n // tm,),
        in_specs=[spec(a) for a in outs + lses],
        out_specs=pl.BlockSpec((tm, SLAB), lambda i: (i, 0)),
        out_shape=jax.ShapeDtypeStruct((n, SLAB), BF16),
        scratch_shapes=[pltpu.VMEM((SLAB // LANES, tm, LANES), F32)] * n_folded,
        compiler_params=_cparams(("parallel",)),
        name="dil_combine",
    )(*outs, *lses)


MLA_QK = MLA_NOPE + MLA_ROPE


def mla_tables(t):
    ang = _rope_angles(t, MLA_ROPE)
    cos2 = np.concatenate([np.cos(ang), np.cos(ang)], axis=1)
    sin2 = np.concatenate([np.sin(ang), np.sin(ang)], axis=1)
    z = lambda w: np.zeros((t, w), np.float32)
    q_cos = np.concatenate([np.ones((t, MLA_NOPE), np.float32), cos2, z(LANES - MLA_QK)], axis=1)
    q_sin = np.concatenate([z(MLA_NOPE), sin2, z(LANES - MLA_QK)], axis=1)
    k_cos = np.concatenate([cos2, z(LANES - MLA_ROPE)], axis=1)
    k_sin = np.concatenate([-sin2[:, :MLA_ROPE // 2], sin2[:, MLA_ROPE // 2:], z(LANES - MLA_ROPE)], axis=1)
    return tuple(jnp.asarray(a, F32) for a in (q_cos, q_sin, k_cos, k_sin))


def mla_weights(w_uq, w_ukv):
    hq = w_uq.reshape(MLA_Q_RANK, MLA_HEADS, MLA_QK)
    nope, pe = hq[..., :MLA_NOPE], hq[..., MLA_NOPE:]
    pe_rot = jnp.concatenate([-pe[..., MLA_ROPE // 2:], pe[..., :MLA_ROPE // 2]], axis=-1)
    zq = jnp.zeros((MLA_Q_RANK, MLA_HEADS, LANES - MLA_QK), w_uq.dtype)
    w1 = jnp.concatenate([nope, pe, zq], axis=-1).reshape(MLA_Q_RANK, MLA_HEADS * LANES)
    w2 = jnp.concatenate([jnp.zeros_like(nope), pe_rot, zq], axis=-1).reshape(MLA_Q_RANK, MLA_HEADS * LANES)
    hkv = w_ukv.reshape(MLA_KV_RANK, MLA_HEADS, MLA_NOPE + MLA_V)
    k_nope, v = hkv[..., :MLA_NOPE], hkv[..., MLA_NOPE:]
    zk = jnp.zeros((MLA_KV_RANK, MLA_HEADS, LANES - MLA_NOPE), w_ukv.dtype)
    wk = jnp.concatenate([k_nope, zk], axis=-1).reshape(MLA_KV_RANK, MLA_HEADS * LANES)
    zv = jnp.zeros((MLA_KV_RANK, MLA_HEADS, LANES - MLA_V), w_ukv.dtype)
    wv = jnp.concatenate([v, zv], axis=-1).reshape(MLA_KV_RANK, MLA_HEADS * LANES)
    place = np.zeros((LANES, MLA_HEADS * LANES), np.float32)
    ones = np.zeros((1, MLA_HEADS * LANES), np.float32)
    for h in range(MLA_HEADS):
        place[np.arange(MLA_ROPE), h * LANES + MLA_NOPE + np.arange(MLA_ROPE)] = 1.0
        ones[0, h * LANES + MLA_V] = 1.0
    return (w1.astype(BF16), w2.astype(BF16), wk.astype(BF16), wv.astype(BF16), jnp.asarray(place, BF16),
            jnp.asarray(ones, F32))


def _mla_q_kernel(c_ref, nw_ref, w1_ref, w2_ref, cos_ref, sin_ref, o_ref):
    cn = _rms(c_ref[...].astype(F32), nw_ref[...]).astype(BF16)
    cos = jnp.tile(cos_ref[...], (1, MLA_HEADS))
    sin = jnp.tile(sin_ref[...], (1, MLA_HEADS))
    q = _dot(cn, w1_ref[...]) * cos + _dot(cn, w2_ref[...]) * sin
    o_ref[...] = (q * (MLA_QK ** -0.5 * math.log2(math.e))).astype(o_ref.dtype)


def _mla_kv_kernel(c_ref, tail_ref, nw_ref, wk_ref, wv_ref, place_ref, ones_ref, cos_ref, sin_ref,
                   k_ref, v_ref):
    cn = _rms(c_ref[...].astype(F32), nw_ref[...]).astype(BF16)
    kr = tail_ref[...]
    half = MLA_ROPE // 2
    rot = jnp.where(_lane_lt(kr.shape, half), pltpu.roll(kr, LANES - half, 1), pltpu.roll(kr, half, 1))
    k_pe = (kr * cos_ref[...] + rot * sin_ref[...]).astype(BF16)
    k_ref[...] = (_dot(cn, wk_ref[...]) + _dot(k_pe, place_ref[...])).astype(k_ref.dtype)
    v_ref[...] = (_dot(cn, wv_ref[...]) + ones_ref[...]).astype(v_ref.dtype)


def mla_project(proj, tail, q_norm_w, kv_norm_w, w_uq, w_ukv, b, t, tm):
    n = b * t
    nb = t // tm
    w1, w2, wk, wv, place, ones = mla_weights(w_uq, w_ukv)
    q_cos, q_sin, k_cos, k_sin = mla_tables(t)
    wide = MLA_HEADS * LANES
    full = lambda shape: pl.BlockSpec(shape, lambda i: (0, 0))
    tab = pl.BlockSpec((tm, LANES), lambda i: (i % nb, 0))
    qf = pl.pallas_call(
        _mla_q_kernel,
        grid=(n // tm,),
        in_specs=[pl.BlockSpec((tm, SLAB), lambda i: (i, COL_CQ)), full((1, MLA_Q_RANK)),
                  full((MLA_Q_RANK, wide)), full((MLA_Q_RANK, wide)), tab, tab],
        out_specs=pl.BlockSpec((tm, wide), lambda i: (i, 0)),
        out_shape=jax.ShapeDtypeStruct((n, wide), BF16),
        compiler_params=_cparams(("parallel",)),
        name="mla_q_proj",
    )(proj, q_norm_w.reshape(1, MLA_Q_RANK), w1, w2, q_cos, q_sin)
    kf, vf = pl.pallas_call(
        _mla_kv_kernel,
        grid=(n // tm,),
        in_specs=[pl.BlockSpec((tm, SLAB), lambda i: (i, COL_CKV)),
                  pl.BlockSpec((tm, LANES), lambda i: (i, 0)), full((1, MLA_KV_RANK)),
                  full((MLA_KV_RANK, wide)), full((MLA_KV_RANK, wide)), full((LANES, wide)), full((1, wide)),
                  tab, tab],
        out_specs=[pl.BlockSpec((tm, wide), lambda i: (i, 0))] * 2,
        out_shape=[jax.ShapeDtypeStruct((n, wide), BF16)] * 2,
        compiler_params=_cparams(("parallel",)),
        name="mla_kv_proj",
    )(proj, tail, kv_norm_w.reshape(1, MLA_KV_RANK), wk, wv, place, ones, k_cos, k_sin)
    return qf, kf, vf


def _mla_attn_kernel(q_ref, k_ref, v_ref, o_ref, *, t, tk):
    tq = q_ref.shape[0]
    groups = [slice(hh * LANES, (hh + 1) * LANES) for hh in range(2)]
    qs = [q_ref[:, grp] for grp in groups]

    def body(c, carry):
        k0 = pl.multiple_of(c * tk, tk)
        new = []
        for (m, acc), q, grp in zip(carry, qs, groups):
            s = _dot_nt(q, k_ref[pl.ds(k0, tk), grp])
            m_new = jnp.maximum(m, jnp.max(s, axis=-1, keepdims=True))
            p = jnp.exp2((s - m_new).astype(BF16))
            acc = jnp.exp2(m - m_new) * acc + _dot(p, v_ref[pl.ds(k0, tk), grp])
            new.append((m_new, acc))
        return tuple(new)

    init = (jnp.full((tq, 1), -jnp.inf, F32), jnp.zeros((tq, LANES), F32))
    carry = lax.fori_loop(0, t // tk, body, (init, init), unroll=True)
    outs = [acc / acc[:, MLA_V:MLA_V + 1] for _, acc in carry]
    first = _lane_lt((tq, LANES), MLA_V)
    o_ref[...] = jnp.where(first, outs[0], pltpu.roll(outs[1], MLA_V, 1)).astype(o_ref.dtype)


def mla_attention(qf, kf, vf, b, t, tq, tk):
    n = b * t
    nq = t // tq
    return pl.pallas_call(
        functools.partial(_mla_attn_kernel, t=t, tk=tk),
        grid=(b, N_PAIRS, nq),
        in_specs=[
            pl.BlockSpec((tq, 2 * LANES), lambda i, p, s: (i * nq + s, p)),
            pl.BlockSpec((t, 2 * LANES), lambda i, p, s: (i, p)),
            pl.BlockSpec((t, 2 * LANES), lambda i, p, s: (i, p)),
        ],
        out_specs=pl.BlockSpec((tq, HEAD_PAIR), lambda i, p, s: (i * nq + s, p)),
        out_shape=jax.ShapeDtypeStruct((n, SLAB), BF16),
        compiler_params=_cparams(("parallel", "parallel", "arbitrary")),
        name="mla_attention",
    )(qf, kf, vf)


def _in_proj_segments():
    sizes = (SSM_INNER, SSM_CONV_CH, 2 * SSM_HEADS, SLAB, SLAB, SLAB, MLA_Q_RANK, MLA_KV_RANK, MLA_ROPE,
             SLAB, SLAB, SLAB)
    off = [int(v) for v in np.concatenate([[0], np.cumsum(sizes)])]
    main = ((off[0], off[2]), (off[3], off[8]), (off[9], off[12]))
    tail = ((off[8], off[9]), (off[2], off[3]))
    return main, tail


def _in_proj_columns():
    main, tail = _in_proj_segments()
    cols = lambda segs: np.concatenate([np.arange(a, b) for a, b in segs])
    return cols(main), cols(tail)


def in_proj_weights(w_in_l):
    main, tail = _in_proj_segments()
    w_main = jnp.concatenate([w_in_l[:, a:b] for a, b in main], axis=1).astype(BF16)
    pad = jnp.zeros((D_MODEL, LANES - sum(b - a for a, b in tail)), w_in_l.dtype)
    w_tail = jnp.concatenate([w_in_l[:, a:b] for a, b in tail] + [pad], axis=1).astype(BF16)
    return w_main, w_tail


def mixers(proj, tail, p, l, b, t):
    xbc = conv_silu(proj, p["conv_w"][l], p["conv_b"][l], b, t)
    y_f = ssd_scan(xbc, tail, p["dt_bias"][l], p["a_log"][l], b, t, 0)
    y_b = ssd_scan(xbc, tail, p["dt_bias"][l], p["a_log"][l], b, t, 1)
    y_ssm = ssd_combine(y_f, y_b, xbc, proj, p["d_skip"][l], p["ssm_norm_w"][l], 1024)

    y_na = na_attention(proj, p["na_rpb"][l], b, t)

    qf, kf, vf = mla_project(proj, tail, p["mla_q_norm_w"][l], p["mla_kv_norm_w"][l],
                             p["mla_w_uq"][l], p["mla_w_ukv"][l], b, t, 512)
    y_mla = mla_attention(qf, kf, vf, b, t, 512, 512)

    qkvs = rope_qkv(proj, b, t, 1024)
    outs, lses = zip(*[band_attention(qkv, w, d) for qkv, (w, d) in zip(qkvs, DIL_PAIRS)])
    y_dil = dil_combine(outs, lses, 1024)
    return y_ssm, y_na, y_mla, y_dil


def kernel(x, attn_norm_w, w_in, conv_w, conv_b, a_log, dt_bias, d_skip, ssm_norm_w, na_rpb,
           mla_q_norm_w, mla_kv_norm_w, mla_w_uq, mla_w_ukv, w_o, ffn_norm_w, ffn_w_gate, ffn_w_up,
           ffn_w_down, router_w, exp_w_gate, exp_w_up, exp_w_down, final_norm_w):
    b, t, _ = x.shape
    n = b * t
    depth = w_in.shape[0]
    p = dict(conv_w=conv_w, conv_b=conv_b, a_log=a_log, dt_bias=dt_bias, d_skip=d_skip,
             ssm_norm_w=ssm_norm_w, na_rpb=na_rpb, mla_q_norm_w=mla_q_norm_w,
             mla_kv_norm_w=mla_kv_norm_w, mla_w_uq=mla_w_uq, mla_w_ukv=mla_w_ukv)
    x = x.reshape(n, D_MODEL)
    cast_rows = 256
    w_o_b = cast_bf16(w_o, cast_rows)
    ffn_b = [cast_bf16(w, cast_rows) for w in (ffn_w_gate, ffn_w_up, ffn_w_down)]
    exp_b = [cast_bf16(w, cast_rows) for w in (exp_w_gate, exp_w_up, exp_w_down)]
    moe_tm = 512
    normed = False
    for l in range(depth):
        w_main, w_tail = in_proj_weights(w_in[l])
        proj = norm_matmul(x, 0, D_MODEL, attn_norm_w[l], w_main, BF16, 1024, PROJ_MAIN // 4)
        tail = norm_matmul(x, 0, D_MODEL, attn_norm_w[l], w_tail, F32, 1024, LANES)
        mix = mixers(proj, tail, p, l, b, t)
        x = out_proj(mix, w_o_b, l, x, 1024, 1024)
        j = l // 2
        if l % 2 == 0:
            x = ffn_dense(x, ffn_norm_w[l], *ffn_b, j, 512, 512)
        else:
            top_i, gates = moe_router(x, ffn_norm_w[l], router_w[j], 512)
            src, pos, tile_expert, tile_valid = moe_plan(top_i, moe_tm)
            y = moe_ffn(x, ffn_norm_w[l], *exp_b, j, src, tile_expert, tile_valid, moe_tm, 512)
            last = l == depth - 1
            x = moe_combine(x, gates, y, pos, final_norm_w if last else None, 256)
            normed = last
    if not normed:
        x = rmsnorm_rows(x, final_norm_w, 1024)
    return x.reshape(b, t, D_MODEL)
```

```python
import functools
import math

import numpy as np
import jax
import jax.numpy as jnp
from jax import lax
from jax.experimental import pallas as pl
from jax.experimental.pallas import tpu as pltpu

F32 = jnp.float32
BF16 = jnp.bfloat16

D_MODEL = 2048
GRID_W = 64
HEAD_DIM = 64
ROPE_THETA = 10000.0
NORM_EPS = 1e-6
NEG_INF = -1e30

SSM_HEADS = 8
SSM_HEAD_DIM = 64
SSM_INNER = SSM_HEADS * SSM_HEAD_DIM
SSM_GROUPS = 2
SSM_STATE = 128
SSM_CONV = 5
SSM_CHUNK = 128
SSM_CONV_CH = SSM_INNER + 2 * SSM_GROUPS * SSM_STATE

NA_HEADS = 8
NA_WIN_ROWS = 8
NA_WIN_COLS = 16
NA_COL_BLOCK = 16
NA_KEY_COLS = 32
NA_ROWS_PER_STEP = 4

MLA_HEADS = 8
MLA_Q_RANK = 512
MLA_KV_RANK = 512
MLA_NOPE = 64
MLA_ROPE = 32
MLA_V = 64

DIL_HEADS = 8
DIL_PAIRS = ((128, 1), (512, 4), (2048, 16))
DIL_QBLOCK = 128
BAND_BLOCKS_PER_STEP = 4

N_EXPERTS = 8
TOP_K = 2

LANES = 128
HEAD_PAIR = 2 * HEAD_DIM
N_PAIRS = 4
SLAB = 512

COL_Z, COL_XBC, COL_NAQ, COL_NAK, COL_NAV, COL_CQ, COL_CKV, COL_DLQ, COL_DLK, COL_DLV = (
    0, 1, 3, 4, 5, 6, 7, 8, 9, 10)
PROJ_MAIN = 11 * SLAB
TAIL_DT = 32

VMEM_LIMIT = 56 * 1024 * 1024


def _cparams(sem, vmem=VMEM_LIMIT):
    return pltpu.CompilerParams(dimension_semantics=sem, vmem_limit_bytes=vmem)


def _lane_lt(shape, bound, period=None):
    lane = lax.broadcasted_iota(jnp.int32, shape, len(shape) - 1)
    if period is not None:
        lane = lane % period
    return lane < bound


def _rms(x, w):
    ms = jnp.mean(x * x, axis=-1, keepdims=True)
    return x * lax.rsqrt(ms + NORM_EPS) * w


def _dot(a, b):
    return jnp.dot(a, b, preferred_element_type=F32)


def _dot_nt(a, b):
    return lax.dot_general(a, b, (((1,), (1,)), ((), ())), preferred_element_type=F32)


def _dot_tn(a, b):
    return lax.dot_general(a, b, (((0,), (0,)), ((), ())), preferred_element_type=F32)


def _cast_kernel(x_ref, o_ref):
    o_ref[...] = x_ref[...].astype(o_ref.dtype)


def cast_bf16(w, tr):
    shape = w.shape
    w2 = w.reshape(-1, shape[-1])
    r, c = w2.shape
    out = pl.pallas_call(
        _cast_kernel,
        grid=(r // tr,),
        in_specs=[pl.BlockSpec((tr, c), lambda i: (i, 0))],
        out_specs=pl.BlockSpec((tr, c), lambda i: (i, 0)),
        out_shape=jax.ShapeDtypeStruct((r, c), BF16),
        compiler_params=_cparams(("parallel",)),
        name="cast_bf16",
    )(w2)
    return out.reshape(shape)


def _norm_matmul_kernel(x_ref, nw_ref, w_ref, o_ref, h_ref):
    @pl.when(pl.program_id(1) == 0)
    def _():
        h_ref[...] = _rms(x_ref[...].astype(F32), nw_ref[...]).astype(BF16)

    o_ref[...] = _dot(h_ref[...], w_ref[...]).astype(o_ref.dtype)


def norm_matmul(x, xcol, k, nw, w, out_dtype, tm, tn):
    n = x.shape[0]
    nout = w.shape[1]
    return pl.pallas_call(
        _norm_matmul_kernel,
        grid=(n // tm, nout // tn),
        in_specs=[
            pl.BlockSpec((tm, k), lambda i, j: (i, xcol)),
            pl.BlockSpec((1, k), lambda i, j: (0, 0)),
            pl.BlockSpec((k, tn), lambda i, j: (0, j)),
        ],
        out_specs=pl.BlockSpec((tm, tn), lambda i, j: (i, j)),
        out_shape=jax.ShapeDtypeStruct((n, nout), out_dtype),
        scratch_shapes=[pltpu.VMEM((tm, k), BF16)],
        compiler_params=_cparams(("parallel", "arbitrary")),
        name="norm_matmul",
    )(x, nw.reshape(1, k), w)


def _out_proj_kernel(a0_ref, a1_ref, a2_ref, a3_ref, w_ref, r_ref, o_ref):
    acc = r_ref[...]
    for s, a_ref in enumerate((a0_ref, a1_ref, a2_ref, a3_ref)):
        acc = acc + _dot(a_ref[...], w_ref[s * SLAB:(s + 1) * SLAB, :])
    o_ref[...] = acc


def out_proj(mix, w, layer, res, tm, tn):
    n = res.shape[0]
    return pl.pallas_call(
        _out_proj_kernel,
        grid=(n // tm, D_MODEL // tn),
        in_specs=[pl.BlockSpec((tm, SLAB), lambda i, j: (i, 0))] * 4 + [
            pl.BlockSpec((None, 4 * SLAB, tn), lambda i, j: (layer, 0, j)),
            pl.BlockSpec((tm, tn), lambda i, j: (i, j)),
        ],
        out_specs=pl.BlockSpec((tm, tn), lambda i, j: (i, j)),
        out_shape=jax.ShapeDtypeStruct((n, D_MODEL), F32),
        compiler_params=_cparams(("parallel", "arbitrary")),
        name="out_proj",
    )(*mix, w, res)


def _ffn_kernel(x_ref, nw_ref, wg_ref, wu_ref, wd_ref, o_ref, h_ref):
    @pl.when(pl.program_id(1) == 0)
    def _():
        x = x_ref[...]
        h_ref[...] = _rms(x, nw_ref[...]).astype(BF16)
        o_ref[...] = x

    _swiglu_rows(h_ref, wg_ref, wu_ref, wd_ref, o_ref)


FFN_ROW_CHUNK = 512


def _swiglu_rows(h_ref, wg_ref, wu_ref, wd_ref, o_ref):
    tm = h_ref.shape[0]
    for r0 in range(0, tm, FFN_ROW_CHUNK):
        rows = slice(r0, r0 + FFN_ROW_CHUNK)
        h = h_ref[rows, :]
        g = _dot(h, wg_ref[...])
        u = _dot(h, wu_ref[...])
        a = (g * jax.nn.sigmoid(g) * u).astype(BF16)
        o_ref[rows, :] += _dot(a, wd_ref[...])


def ffn_dense(x, nw, wg, wu, wd, layer, tm, tf):
    n = x.shape[0]
    d_ff = wg.shape[-1]
    return pl.pallas_call(
        _ffn_kernel,
        grid=(n // tm, d_ff // tf),
        in_specs=[
            pl.BlockSpec((tm, D_MODEL), lambda i, j: (i, 0)),
            pl.BlockSpec((1, D_MODEL), lambda i, j: (0, 0)),
            pl.BlockSpec((None, D_MODEL, tf), lambda i, j: (layer, 0, j)),
            pl.BlockSpec((None, D_MODEL, tf), lambda i, j: (layer, 0, j)),
            pl.BlockSpec((None, tf, D_MODEL), lambda i, j: (layer, j, 0)),
        ],
        out_specs=pl.BlockSpec((tm, D_MODEL), lambda i, j: (i, 0)),
        out_shape=jax.ShapeDtypeStruct((n, D_MODEL), F32),
        scratch_shapes=[pltpu.VMEM((tm, D_MODEL), BF16)],
        compiler_params=_cparams(("parallel", "arbitrary")),
        name="ffn_dense",
    )(x, nw.reshape(1, D_MODEL), wg, wu, wd)


def _router_kernel(x_ref, nw_ref, rw_ref, idx_ref, gate_ref):
    h = _rms(x_ref[...], nw_ref[...])
    logits = jnp.dot(h, rw_ref[...], preferred_element_type=F32, precision=lax.Precision.HIGHEST)
    lane = lax.broadcasted_iota(jnp.int32, logits.shape, 1)
    logits = jnp.where(lane < N_EXPERTS, logits, -jnp.inf)
    m1 = jnp.max(logits, axis=-1, keepdims=True)
    i1 = jnp.min(jnp.where(logits == m1, lane, LANES), axis=-1, keepdims=True)
    rest = jnp.where(lane == i1, -jnp.inf, logits)
    m2 = jnp.max(rest, axis=-1, keepdims=True)
    i2 = jnp.min(jnp.where(rest == m2, lane, LANES), axis=-1, keepdims=True)
    e2 = jnp.exp(m2 - m1)
    g1 = 1.0 / (1.0 + e2)
    g2 = e2 / (1.0 + e2)
    idx_ref[...] = jnp.where(lane == 0, i1, i2)[:, :TOP_K]
    gate_ref[...] = jnp.where(lane == 0, g1, g2)[:, :TOP_K]


def moe_router(x, nw, router_w, tm):
    n = x.shape[0]
    rw = jnp.zeros((D_MODEL, LANES), F32).at[:, :N_EXPERTS].set(router_w)
    return pl.pallas_call(
        _router_kernel,
        grid=(n // tm,),
        in_specs=[
            pl.BlockSpec((tm, D_MODEL), lambda i: (i, 0)),
            pl.BlockSpec((1, D_MODEL), lambda i: (0, 0)),
            pl.BlockSpec((D_MODEL, LANES), lambda i: (0, 0)),
        ],
        out_specs=[pl.BlockSpec((tm, TOP_K), lambda i: (i, 0)),
                   pl.BlockSpec((tm, TOP_K), lambda i: (i, 0))],
        out_shape=[jax.ShapeDtypeStruct((n, TOP_K), jnp.int32),
                   jax.ShapeDtypeStruct((n, TOP_K), F32)],
        compiler_params=_cparams(("parallel",)),
        name="moe_router",
    )(x, nw.reshape(1, D_MODEL), rw)


def moe_plan(top_i, tm):
    n = top_i.shape[0]
    flat_e = top_i.reshape(-1)
    onehot = (flat_e[:, None] == jnp.arange(N_EXPERTS, dtype=jnp.int32)[None, :]).astype(jnp.int32)
    csum = jnp.cumsum(onehot, axis=0)
    counts = csum[-1]
    rank = jnp.sum(onehot * csum, axis=1) - 1
    padded = ((counts + tm - 1) // tm) * tm
    pend = jnp.cumsum(padded)
    pstart = pend - padded
    pos = pstart[flat_e] + rank
    n_slots = n * TOP_K + N_EXPERTS * tm
    n_tiles = n_slots // tm
    src = jnp.zeros((n_slots,), jnp.int32).at[pos].set(jnp.arange(n * TOP_K, dtype=jnp.int32) // TOP_K)
    tile_start = jnp.arange(n_tiles, dtype=jnp.int32) * tm
    tile_expert = jnp.sum((tile_start[:, None] >= pend[None, :]).astype(jnp.int32), axis=1)
    tile_valid = (tile_start < pend[-1]).astype(jnp.int32)
    last_valid = jnp.maximum(pend[-1] // tm - 1, 0)
    tile_expert = jnp.where(tile_valid == 1, tile_expert, tile_expert[last_valid]).astype(jnp.int32)
    return src, pos.reshape(n, TOP_K).astype(jnp.int32), tile_expert, tile_valid


def _moe_ffn_kernel(te_ref, tv_ref, src_ref, nsrc_ref, x_hbm, nw_ref, wg_ref, wu_ref, wd_ref, y_ref,
                    xbuf, h_ref, sem, *, tm):
    i = pl.program_id(0)
    j = pl.program_id(1)
    n_tiles = pl.num_programs(0)
    valid = tv_ref[i] == 1

    def row_copy(idx_ref, r):
        tok = idx_ref[0, 0, r]
        return pltpu.make_async_copy(x_hbm.at[pl.ds(tok, 1)], xbuf.at[pl.ds(r, 1)], sem)

    def gather(idx_ref):
        def start(r, c):
            row_copy(idx_ref, r).start()
            return c

        lax.fori_loop(0, tm, start, 0, unroll=8)

    @pl.when(j == 0)
    def _():
        y_ref[...] = jnp.zeros_like(y_ref)

    @pl.when(jnp.logical_and(j == 0, valid))
    def _():
        @pl.when(i == 0)
        def _():
            gather(src_ref)

        pltpu.make_async_copy(x_hbm.at[pl.ds(0, tm)], xbuf, sem).wait()
        h_ref[...] = _rms(xbuf[...], nw_ref[...]).astype(BF16)

        nxt = jnp.minimum(i + 1, n_tiles - 1)

        @pl.when(jnp.logical_and(i + 1 < n_tiles, tv_ref[nxt] == 1))
        def _():
            gather(nsrc_ref)

    @pl.when(valid)
    def _():
        _swiglu_rows(h_ref, wg_ref, wu_ref, wd_ref, y_ref)


def moe_ffn(x, nw, wg, wu, wd, layer, src, tile_expert, tile_valid, tm, tf):
    n_slots = src.shape[0]
    n_tiles = n_slots // tm
    d_ff = wg.shape[-1]
    nf = d_ff // tf

    def wcol(i, j, te_ref, tv_ref):
        return (layer, te_ref[i], 0, jnp.where(tv_ref[i] == 1, j, nf - 1))

    def wrow(i, j, te_ref, tv_ref):
        return (layer, te_ref[i], jnp.where(tv_ref[i] == 1, j, nf - 1), 0)

    grid_spec = pltpu.PrefetchScalarGridSpec(
        num_scalar_prefetch=2,
        grid=(n_tiles, nf),
        in_specs=[
            pl.BlockSpec((1, 1, tm), lambda i, j, *_: (i, 0, 0), memory_space=pltpu.SMEM),
            pl.BlockSpec((1, 1, tm), lambda i, j, *_: (jnp.minimum(i + 1, n_tiles - 1), 0, 0),
                         memory_space=pltpu.SMEM),
            pl.BlockSpec(memory_space=pl.ANY),
            pl.BlockSpec((1, D_MODEL), lambda i, j, *_: (0, 0)),
            pl.BlockSpec((None, None, D_MODEL, tf), wcol),
            pl.BlockSpec((None, None, D_MODEL, tf), wcol),
            pl.BlockSpec((None, None, tf, D_MODEL), wrow),
        ],
        out_specs=pl.BlockSpec((tm, D_MODEL), lambda i, j, *_: (i, 0)),
        scratch_shapes=[pltpu.VMEM((tm, D_MODEL), F32), pltpu.VMEM((tm, D_MODEL), BF16),
                        pltpu.SemaphoreType.DMA],
    )
    src3 = src.reshape(n_tiles, 1, tm)
    return pl.pallas_call(
        functools.partial(_moe_ffn_kernel, tm=tm),
        grid_spec=grid_spec,
        out_shape=jax.ShapeDtypeStruct((n_slots, D_MODEL), F32),
        compiler_params=_cparams(("arbitrary", "arbitrary")),
        name="moe_ffn",
    )(tile_expert, tile_valid, src3, src3, x, nw.reshape(1, D_MODEL), wg, wu, wd)


def _moe_combine_kernel(pos_ref, npos_ref, x_ref, gate_ref, y_hbm, fw_ref, o_ref, ybuf, sem, *, tm, final_norm):
    i = pl.program_id(0)
    n_tiles = pl.num_programs(0)
    cur = i % 2

    def row_copy(idx_ref, buf, r, k):
        slot = idx_ref[0, 0, r * TOP_K + k]
        return pltpu.make_async_copy(y_hbm.at[pl.ds(slot, 1)], ybuf.at[buf, k, pl.ds(r, 1)], sem.at[buf])

    def gather(idx_ref, buf):
        def start(r, c):
            for k in range(TOP_K):
                row_copy(idx_ref, buf, r, k).start()
            return c

        lax.fori_loop(0, tm, start, 0, unroll=8)

    @pl.when(i == 0)
    def _():
        gather(pos_ref, 0)

    @pl.when(i + 1 < n_tiles)
    def _():
        gather(npos_ref, 1 - cur)

    for k in range(TOP_K):
        pltpu.make_async_copy(y_hbm.at[pl.ds(0, tm)], ybuf.at[cur, k], sem.at[cur]).wait()
    gates = gate_ref[...]
    out = x_ref[...]
    for k in range(TOP_K):
        out = out + gates[:, k:k + 1] * ybuf[cur, k]
    if final_norm:
        out = _rms(out, fw_ref[...])
    o_ref[...] = out


def moe_combine(x, gates, y, pos, final_w, tm):
    n = x.shape[0]
    final_norm = final_w is not None
    fw = final_w if final_norm else jnp.ones((D_MODEL,), F32)
    n_tiles = n // tm
    pos3 = pos.reshape(n_tiles, 1, tm * TOP_K)
    return pl.pallas_call(
        functools.partial(_moe_combine_kernel, tm=tm, final_norm=final_norm),
        grid=(n_tiles,),
        in_specs=[
            pl.BlockSpec((1, 1, tm * TOP_K), lambda i: (i, 0, 0), memory_space=pltpu.SMEM),
            pl.BlockSpec((1, 1, tm * TOP_K), lambda i: (jnp.minimum(i + 1, n_tiles - 1), 0, 0),
                         memory_space=pltpu.SMEM),
            pl.BlockSpec((tm, D_MODEL), lambda i: (i, 0)),
            pl.BlockSpec((tm, TOP_K), lambda i: (i, 0)),
            pl.BlockSpec(memory_space=pl.ANY),
            pl.BlockSpec((1, D_MODEL), lambda i: (0, 0)),
        ],
        out_specs=pl.BlockSpec((tm, D_MODEL), lambda i: (i, 0)),
        out_shape=jax.ShapeDtypeStruct((n, D_MODEL), F32),
        scratch_shapes=[pltpu.VMEM((2, TOP_K, tm, D_MODEL), F32), pltpu.SemaphoreType.DMA((2,))],
        compiler_params=_cparams(("arbitrary",)),
        name="moe_combine",
    )(pos3, pos3, x, gates, y, fw.reshape(1, D_MODEL))


def _rmsnorm_kernel(x_ref, w_ref, o_ref):
    o_ref[...] = _rms(x_ref[...], w_ref[...])


def rmsnorm_rows(x, w, tm):
    n = x.shape[0]
    return pl.pallas_call(
        _rmsnorm_kernel,
        grid=(n // tm,),
        in_specs=[pl.BlockSpec((tm, D_MODEL), lambda i: (i, 0)),
                  pl.BlockSpec((1, D_MODEL), lambda i: (0, 0))],
        out_specs=pl.BlockSpec((tm, D_MODEL), lambda i: (i, 0)),
        out_shape=jax.ShapeDtypeStruct((n, D_MODEL), F32),
        compiler_params=_cparams(("parallel",)),
        name="final_norm",
    )(x, w.reshape(1, D_MODEL))


CONV_PAD = 8


def _conv_kernel(x_ref, w_ref, b_ref, o_ref, pad_ref, *, t):
    half = SSM_CONV // 2
    zeros = jnp.zeros((CONV_PAD, pad_ref.shape[1]), F32)
    pad_ref[0:CONV_PAD, :] = zeros
    pad_ref[CONV_PAD + t:CONV_PAD + t + CONV_PAD, :] = zeros
    pad_ref[CONV_PAD:CONV_PAD + t, :] = x_ref[...].astype(F32)
    acc = jnp.zeros(o_ref.shape, F32) + b_ref[...]
    for k in range(SSM_CONV):
        acc = acc + pad_ref[pl.ds(CONV_PAD - half + k, t), :] * w_ref[k:k + 1, :]
    o_ref[...] = (acc * jax.nn.sigmoid(acc)).astype(o_ref.dtype)


def conv_silu(proj, conv_w, conv_b, b, t):
    tc = 256
    nblk = SSM_CONV_CH // tc
    col0 = COL_XBC * SLAB // tc
    return pl.pallas_call(
        functools.partial(_conv_kernel, t=t),
        grid=(b, nblk),
        in_specs=[
            pl.BlockSpec((t, tc), lambda i, j: (i, col0 + j)),
            pl.BlockSpec((SSM_CONV, tc), lambda i, j: (0, j)),
            pl.BlockSpec((1, tc), lambda i, j: (0, j)),
        ],
        out_specs=pl.BlockSpec((t, tc), lambda i, j: (i, j)),
        out_shape=jax.ShapeDtypeStruct((b * t, SSM_CONV_CH), BF16),
        scratch_shapes=[pltpu.VMEM((t + 2 * CONV_PAD, tc), F32)],
        compiler_params=_cparams(("parallel", "parallel")),
        name="conv_silu",
    )(proj, conv_w, conv_b.reshape(1, SSM_CONV_CH))


def _ssd_chunk(xbc_ref, tail_ref, bias_ref, alog_ref, tri_ref, y_ref, state_ref, direction):
    q = SSM_CHUNK
    dt = jax.nn.softplus(tail_ref[...] + bias_ref[...])
    da = dt * (-jnp.exp(alog_ref[...]))
    cs = jnp.dot(tri_ref[...], da, preferred_element_type=F32, precision=lax.Precision.HIGHEST)
    total = cs[q - 1:q, :]
    if direction == 0:
        e_out = cs
        e_in = total - cs
        e_seg = cs
    else:
        ex = cs - da
        e_out = total - ex
        e_in = ex
        e_seg = -ex
    e_seg_t = jnp.transpose(e_seg)
    dec_out = jnp.exp(e_out)
    dec_in_dt = jnp.exp(e_in) * dt
    dt_t = jnp.transpose(dt)
    dec_tot = jnp.exp(total)

    row = lax.broadcasted_iota(jnp.int32, (q, q), 0)
    col = lax.broadcasted_iota(jnp.int32, (q, q), 1)
    keep = (row >= col) if direction == 0 else (col >= row)
    first_half = _lane_lt((q, HEAD_PAIR), HEAD_DIM)

    xs = xbc_ref[:, 0:SSM_INNER]
    heads_per_group = SSM_HEADS // SSM_GROUPS
    for g in range(SSM_GROUPS):
        bm = xbc_ref[:, SSM_INNER + g * SSM_STATE:SSM_INNER + (g + 1) * SSM_STATE]
        cm = xbc_ref[:, SSM_INNER + (SSM_GROUPS + g) * SSM_STATE:SSM_INNER + (SSM_GROUPS + g + 1) * SSM_STATE]
        cb = _dot_nt(cm, bm)
        bm_f = bm.astype(F32)
        cm_f = cm.astype(F32)
        for pp in range(heads_per_group // 2):
            pair = g * (heads_per_group // 2) + pp
            xs_pair = xs[:, pair * HEAD_PAIR:(pair + 1) * HEAD_PAIR]
            st = state_ref[pair]
            ys, sts, decs = [], [], []
            for hh in range(2):
                lane = TAIL_DT + direction * SSM_HEADS + pair * 2 + hh
                seg = e_seg[:, lane:lane + 1] - e_seg_t[lane:lane + 1, :]
                lmat = jnp.where(keep, jnp.exp(seg), 0.0)
                w = (cb * lmat * dt_t[lane:lane + 1, :]).astype(BF16)
                y = _dot(w, xs_pair)
                y = y + _dot((cm_f * dec_out[:, lane:lane + 1]).astype(BF16), st.astype(BF16))
                ys.append(y)
                sts.append(_dot_tn((bm_f * dec_in_dt[:, lane:lane + 1]).astype(BF16), xs_pair))
                decs.append(dec_tot[:, lane:lane + 1])
            y_ref[:, pair * HEAD_PAIR:(pair + 1) * HEAD_PAIR] = jnp.where(first_half, ys[0], ys[1])
            first_half_s = _lane_lt(st.shape, HEAD_DIM)
            state_ref[pair] = (st * jnp.where(first_half_s, decs[0], decs[1])
                               + jnp.where(first_half_s, sts[0], sts[1]))


def _ssd_kernel(xbc_f_ref, tail_f_ref, xbc_b_ref, tail_b_ref, bias_ref, alog_ref, tri_ref,
                y_f_ref, y_b_ref, state_ref):
    @pl.when(pl.program_id(1) == 0)
    def _():
        state_ref[...] = jnp.zeros_like(state_ref)

    _ssd_chunk(xbc_f_ref, tail_f_ref, bias_ref, alog_ref, tri_ref, y_f_ref, state_ref.at[0], 0)
    _ssd_chunk(xbc_b_ref, tail_b_ref, bias_ref, alog_ref, tri_ref, y_b_ref, state_ref.at[1], 1)


def ssd_scan(xbc, tail, dt_bias, a_log, b, t):
    q = SSM_CHUNK
    nc = t // q
    bias_row = jnp.zeros((1, LANES), F32).at[0, TAIL_DT:TAIL_DT + 2 * SSM_HEADS].set(dt_bias.reshape(-1))
    alog_row = jnp.zeros((1, LANES), F32).at[0, TAIL_DT:TAIL_DT + 2 * SSM_HEADS].set(a_log.reshape(-1))
    tri = jnp.asarray(np.tril(np.ones((q, q), np.float32)))
    fwd = lambda i, c: (i * nc + c, 0)
    bwd = lambda i, c: (i * nc + nc - 1 - c, 0)
    const = lambda i, c: (0, 0)
    return pl.pallas_call(
        _ssd_kernel,
        grid=(b, nc),
        in_specs=[
            pl.BlockSpec((q, SSM_CONV_CH), fwd), pl.BlockSpec((q, LANES), fwd),
            pl.BlockSpec((q, SSM_CONV_CH), bwd), pl.BlockSpec((q, LANES), bwd),
            pl.BlockSpec((1, LANES), const), pl.BlockSpec((1, LANES), const), pl.BlockSpec((q, q), const),
        ],
        out_specs=[pl.BlockSpec((q, SSM_INNER), fwd), pl.BlockSpec((q, SSM_INNER), bwd)],
        out_shape=[jax.ShapeDtypeStruct((b * t, SSM_INNER), F32)] * 2,
        scratch_shapes=[pltpu.VMEM((2, N_PAIRS, SSM_STATE, HEAD_PAIR), F32)],
        compiler_params=_cparams(("parallel", "arbitrary")),
        name="ssd_scan",
    )(xbc, tail, xbc, tail, bias_row, alog_row, tri)


def _ssd_combine_kernel(yf_ref, yb_ref, xs_ref, z_ref, d_ref, nw_ref, o_ref):
    y = yf_ref[...] + yb_ref[...] + xs_ref[...].astype(F32) * d_ref[...]
    z = z_ref[...].astype(F32)
    o_ref[...] = _rms(y * (z * jax.nn.sigmoid(z)), nw_ref[...]).astype(o_ref.dtype)


def ssd_combine(y_f, y_b, xbc, proj, d_skip, norm_w, tm):
    n = y_f.shape[0]
    d_row = jnp.repeat(d_skip, SSM_HEAD_DIM).reshape(1, SSM_INNER)
    row = lambda i: (i, 0)
    return pl.pallas_call(
        _ssd_combine_kernel,
        grid=(n // tm,),
        in_specs=[
            pl.BlockSpec((tm, SSM_INNER), row),
            pl.BlockSpec((tm, SSM_INNER), row),
            pl.BlockSpec((tm, SSM_INNER), row),
            pl.BlockSpec((tm, SLAB), lambda i: (i, COL_Z)),
            pl.BlockSpec((1, SSM_INNER), lambda i: (0, 0)),
            pl.BlockSpec((1, SSM_INNER), lambda i: (0, 0)),
        ],
        out_specs=pl.BlockSpec((tm, SSM_INNER), row),
        out_shape=jax.ShapeDtypeStruct((n, SSM_INNER), BF16),
        compiler_params=_cparams(("parallel",)),
        name="ssd_combine",
    )(y_f, y_b, xbc, proj, d_row, norm_w.reshape(1, SSM_INNER))


def _pair_softmax_attend(q2, k2, v2, bias_fn):
    tq = q2.shape[0]
    first_q = _lane_lt(q2.shape, HEAD_DIM)
    zero = jnp.zeros_like(q2)
    qs = jnp.concatenate([jnp.where(first_q, q2, zero), jnp.where(first_q, zero, q2)], axis=0)
    s = bias_fn(_dot_nt(qs, k2))
    m = jnp.max(s, axis=-1, keepdims=True)
    p = jnp.exp(s - m)
    l = jnp.sum(p, axis=-1, keepdims=True)
    o = _dot(p.astype(BF16), v2) / l
    lse = m + jnp.log(l)
    first_o = _lane_lt((tq, HEAD_PAIR), HEAD_DIM)
    return jnp.where(first_o, o[:tq], o[tq:]), jnp.where(first_o, lse[:tq], lse[tq:])


def na_bias_tables(rpb, rows):
    kr = min(NA_WIN_ROWS, rows)
    qc = np.arange(GRID_W)
    kc = np.arange(GRID_W)
    q_start = np.clip(qc - NA_WIN_COLS // 2, 0, GRID_W - NA_WIN_COLS)
    col_in = (kc[None, :] >= q_start[:, None]) & (kc[None, :] < q_start[:, None] + NA_WIN_COLS)
    col_off = np.clip(kc[None, :] - qc[:, None] + NA_WIN_COLS - 1, 0, 2 * NA_WIN_COLS - 2)
    onehot = (col_off[None] == np.arange(2 * NA_WIN_COLS - 1)[:, None, None]).astype(np.float32)
    expanded = jnp.einsum("hrc,cqk->hqrk", rpb, jnp.asarray(onehot), precision=lax.Precision.HIGHEST)
    expanded = jnp.where(jnp.asarray(col_in)[None, :, None, :], expanded, NEG_INF)

    def table(r):
        row_start = int(np.clip(r - kr // 2, 0, rows - kr))
        ro0 = row_start - r + NA_WIN_ROWS - 1
        return expanded[:, :, ro0:ro0 + kr, :].reshape(N_PAIRS, 2 * GRID_W, kr * GRID_W)

    rs = NA_ROWS_PER_STEP
    lo = [table(r) for r in range(rs)]
    mid = [table(min(rs, rows - 1))] * rs
    hi = [table(r) for r in range(rows - rs, rows)]
    return jnp.stack([jnp.stack(lo), jnp.stack(mid), jnp.stack(hi)])


def _na_kernel(q_ref, k_ref, v_ref, bias_ref, o_ref, *, rows, kr):
    step = pl.program_id(1)
    rs = NA_ROWS_PER_STEP
    for rr in range(rs):
        r = step * rs + rr
        row_start = jnp.clip(r - kr // 2, 0, rows - kr)
        k0 = pl.multiple_of(row_start * GRID_W, GRID_W)
        for pair in range(N_PAIRS):
            cols = slice(pair * HEAD_PAIR, (pair + 1) * HEAD_PAIR)
            q2 = q_ref[rr * GRID_W:(rr + 1) * GRID_W, cols] * jnp.asarray(HEAD_DIM ** -0.5, BF16)
            k2 = k_ref[pl.ds(k0, kr * GRID_W), cols]
            v2 = v_ref[pl.ds(k0, kr * GRID_W), cols]
            o, _ = _pair_softmax_attend(q2, k2, v2, lambda s: s + bias_ref[0, rr, pair])
            o_ref[rr * GRID_W:(rr + 1) * GRID_W, cols] = o.astype(o_ref.dtype)


def na_attention(proj, rpb, b, t):
    rows = t // GRID_W
    kr = min(NA_WIN_ROWS, rows)
    rs = NA_ROWS_PER_STEP
    nsteps = rows // rs
    bias = na_bias_tables(rpb, rows)

    def kind(i, s):
        return jnp.where(s == 0, 0, jnp.where(s == nsteps - 1, 2, 1))

    return pl.pallas_call(
        functools.partial(_na_kernel, rows=rows, kr=kr),
        grid=(b, nsteps),
        in_specs=[
            pl.BlockSpec((rs * GRID_W, SLAB), lambda i, s: (i * nsteps + s, COL_NAQ)),
            pl.BlockSpec((t, SLAB), lambda i, s: (i, COL_NAK)),
            pl.BlockSpec((t, SLAB), lambda i, s: (i, COL_NAV)),
            pl.BlockSpec((1, rs, N_PAIRS, 2 * GRID_W, kr * GRID_W), lambda i, s: (kind(i, s), 0, 0, 0, 0)),
        ],
        out_specs=pl.BlockSpec((rs * GRID_W, SLAB), lambda i, s: (i * nsteps + s, 0)),
        out_shape=jax.ShapeDtypeStruct((b * t, SLAB), BF16),
        compiler_params=_cparams(("parallel", "arbitrary")),
        name="na_attention",
    )(proj, proj, proj, bias)


def _rope_angles(t, d):
    inv = ROPE_THETA ** (-np.arange(0, d, 2, dtype=np.float32) / d)
    return np.arange(t, dtype=np.float32)[:, None] * inv[None, :]


def rope_tables_pair(t):
    ang = _rope_angles(t, HEAD_DIM)
    cos = np.tile(np.cos(ang), (1, 4))
    sin = np.tile(np.concatenate([-np.sin(ang), np.sin(ang)], axis=1), (1, 2))
    return jnp.asarray(cos, F32), jnp.asarray(sin, F32)


FOLD_CHUNK = 256
FOLD_DILS = tuple(d for _, d in DIL_PAIRS if d > 1)


def fold_permutation(dil):
    per = FOLD_CHUNK // dil
    perm = np.zeros((FOLD_CHUNK, FOLD_CHUNK), np.float32)
    dst = np.arange(FOLD_CHUNK)
    perm[dst, (dst % per) * dil + dst // per] = 1.0
    return jnp.asarray(perm, BF16)


def _rope_qkv_kernel(x_ref, v_ref, cos_ref, sin_ref, *rest):
    nd = len(FOLD_DILS)
    perm_refs, o_ref, fold_refs = rest[:nd], rest[nd], rest[nd + 1:]
    cos = cos_ref[...]
    sin = sin_ref[...]
    half = HEAD_DIM // 2
    for c in range(x_ref.shape[1] // LANES):
        x = x_ref[:, c * LANES:(c + 1) * LANES].astype(F32)
        rot = jnp.where(_lane_lt(x.shape, half, HEAD_DIM),
                        pltpu.roll(x, LANES - half, 1), pltpu.roll(x, half, 1))
        y = x * cos + rot * sin
        if c < N_PAIRS:
            y = y * (HEAD_DIM ** -0.5)
        o_ref[0, :, c * LANES:(c + 1) * LANES] = y.astype(o_ref.dtype)
    o_ref[0, :, 2 * SLAB:3 * SLAB] = v_ref[...]
    tm = x_ref.shape[0]
    for dil, perm_ref, f_ref in zip(FOLD_DILS, perm_refs, fold_refs):
        per = FOLD_CHUNK // dil
        for c in range(tm // FOLD_CHUNK):
            folded = _dot(perm_ref[...], o_ref[0, c * FOLD_CHUNK:(c + 1) * FOLD_CHUNK, :]).astype(f_ref.dtype)
            for p in range(dil):
                f_ref[p, c * per:(c + 1) * per, :] = folded[p * per:(p + 1) * per, :]


def rope_qkv(proj, b, t, tm):
    n = b * t
    cos, sin = rope_tables_pair(t)
    nb = t // tm
    fold_spec = lambda d: pl.BlockSpec((None, d, tm // d, 3 * SLAB), lambda i: (i // nb, 0, i % nb, 0))
    outs = pl.pallas_call(
        _rope_qkv_kernel,
        grid=(n // tm,),
        in_specs=[
            pl.BlockSpec((tm, 2 * SLAB), lambda i: (i, COL_DLQ // 2)),
            pl.BlockSpec((tm, SLAB), lambda i: (i, COL_DLV)),
            pl.BlockSpec((tm, LANES), lambda i: (i % nb, 0)),
            pl.BlockSpec((tm, LANES), lambda i: (i % nb, 0)),
        ] + [pl.BlockSpec((FOLD_CHUNK, FOLD_CHUNK), lambda i: (0, 0))] * len(FOLD_DILS),
        out_specs=[fold_spec(1)] + [fold_spec(d) for d in FOLD_DILS],
        out_shape=[jax.ShapeDtypeStruct((b, d, t // d, 3 * SLAB), BF16) for d in (1,) + FOLD_DILS],
        compiler_params=_cparams(("parallel",)),
        name="rope_qkv",
    )(proj, proj, cos, sin, *[fold_permutation(d) for d in FOLD_DILS])
    by_dil = dict(zip((1,) + FOLD_DILS, outs))
    return [by_dil[d] for _, d in DIL_PAIRS]


def _band_kernel(q_ref, k_ref, v_ref, o_ref, lse_ref, *, sub, half, span):
    tq = DIL_QBLOCK
    blocks = q_ref.shape[0] // tq
    for blk in range(blocks):
        qb = pl.program_id(2) * blocks + blk
        rows = slice(blk * tq, (blk + 1) * tq)
        start = jnp.clip(qb * tq - half, 0, sub - span)
        start = pl.multiple_of(start, half)
        q_pos = qb * tq + lax.broadcasted_iota(jnp.int32, (2 * tq, span), 0) % tq
        k_pos = start + lax.broadcasted_iota(jnp.int32, (2 * tq, span), 1)
        valid = jnp.abs(k_pos - q_pos) <= half
        for pair in range(N_PAIRS):
            cols = slice(pair * HEAD_PAIR, (pair + 1) * HEAD_PAIR)
            k2 = k_ref[pl.ds(start, span), cols]
            v2 = v_ref[pl.ds(start, span), cols]
            o, lse = _pair_softmax_attend(q_ref[rows, cols], k2, v2, lambda s: jnp.where(valid, s, NEG_INF))
            o_ref[rows, cols] = o
            lse_ref[rows, cols] = lse


def band_attention(qkv, window, dil):
    b, _, sub, _ = qkv.shape
    half = window // (2 * dil)
    span = DIL_QBLOCK + 2 * half
    tq = DIL_QBLOCK * min(BAND_BLOCKS_PER_STEP, sub // DIL_QBLOCK)
    nqb = sub // tq
    return pl.pallas_call(
        functools.partial(_band_kernel, sub=sub, half=half, span=span),
        grid=(b, dil, nqb),
        in_specs=[
            pl.BlockSpec((None, None, tq, SLAB), lambda i, p, s: (i, p, s, 0)),
            pl.BlockSpec((None, None, sub, SLAB), lambda i, p, s: (i, p, 0, 1)),
            pl.BlockSpec((None, None, sub, SLAB), lambda i, p, s: (i, p, 0, 2)),
        ],
        out_specs=[pl.BlockSpec((None, None, tq, SLAB), lambda i, p, s: (i, p, s, 0))] * 2,
        out_shape=[jax.ShapeDtypeStruct((b, dil, sub, SLAB), F32)] * 2,
        compiler_params=_cparams(("parallel", "parallel", "arbitrary")),
        name="band_attention_d%d" % dil,
    )(qkv, qkv, qkv)


def _dil_combine_kernel(*refs):
    nbr = len(DIL_PAIRS)
    o_refs, l_refs, out_ref = refs[:nbr], refs[nbr:2 * nbr], refs[2 * nbr]
    scratch = iter(refs[2 * nbr + 1:])

    def token_order(ref):
        dil = ref.shape[0]
        if dil == 1:
            return ref[0]
        buf = next(scratch)
        per = ref.shape[1]
        for p in range(dil):
            for c in range(SLAB // LANES):
                buf[c, pl.ds(p, per, stride=dil), :] = ref[p, :, c * LANES:(c + 1) * LANES]
        return jnp.concatenate([buf[c] for c in range(SLAB // LANES)], axis=1)

    os = [token_order(r) for r in o_refs]
    lses = [token_order(r) for r in l_refs]
    m = functools.reduce(jnp.maximum, lses)
    ws = [jnp.exp(l - m) for l in lses]
    den = functools.reduce(jnp.add, ws)
    acc = functools.reduce(jnp.add, [(w / den) * o for w, o in zip(ws, os)])
    out_ref[...] = acc.astype(out_ref.dtype)


def dil_combine(outs, lses, tm):
    b, _, t, _ = outs[0].shape
    n = b * t
    nb = t // tm
    spec = lambda a: pl.BlockSpec((None, a.shape[1], tm // a.shape[1], SLAB), lambda i: (i // nb, 0, i % nb, 0))
    n_folded = sum(1 for a in outs + lses if a.shape[1] > 1)
    return pl.pallas_call(
        _dil_combine_kernel,
        grid=(n // tm,),
        in_specs=[spec(a) for a in outs + lses],
        out_specs=pl.BlockSpec((tm, SLAB), lambda i: (i, 0)),
        out_shape=jax.ShapeDtypeStruct((n, SLAB), BF16),
        scratch_shapes=[pltpu.VMEM((SLAB // LANES, tm, LANES), F32)] * n_folded,
        compiler_params=_cparams(("parallel",)),
        name="dil_combine",
    )(*outs, *lses)


MLA_QK = MLA_NOPE + MLA_ROPE


def mla_tables(t):
    ang = _rope_angles(t, MLA_ROPE)
    cos2 = np.concatenate([np.cos(ang), np.cos(ang)], axis=1)
    sin2 = np.concatenate([np.sin(ang), np.sin(ang)], axis=1)
    z = lambda w: np.zeros((t, w), np.float32)
    q_cos = np.concatenate([np.ones((t, MLA_NOPE), np.float32), cos2, z(LANES - MLA_QK)], axis=1)
    q_sin = np.concatenate([z(MLA_NOPE), sin2, z(LANES - MLA_QK)], axis=1)
    k_cos = np.concatenate([cos2, z(LANES - MLA_ROPE)], axis=1)
    k_sin = np.concatenate([-sin2[:, :MLA_ROPE // 2], sin2[:, MLA_ROPE // 2:], z(LANES - MLA_ROPE)], axis=1)
    return tuple(jnp.asarray(a, F32) for a in (q_cos, q_sin, k_cos, k_sin))


def mla_weights(w_uq, w_ukv):
    hq = w_uq.reshape(MLA_Q_RANK, MLA_HEADS, MLA_QK)
    nope, pe = hq[..., :MLA_NOPE], hq[..., MLA_NOPE:]
    pe_rot = jnp.concatenate([-pe[..., MLA_ROPE // 2:], pe[..., :MLA_ROPE // 2]], axis=-1)
    zq = jnp.zeros((MLA_Q_RANK, MLA_HEADS, LANES - MLA_QK), w_uq.dtype)
    w1 = jnp.concatenate([nope, pe, zq], axis=-1).reshape(MLA_Q_RANK, MLA_HEADS * LANES)
    w2 = jnp.concatenate([jnp.zeros_like(nope), pe_rot, zq], axis=-1).reshape(MLA_Q_RANK, MLA_HEADS * LANES)
    hkv = w_ukv.reshape(MLA_KV_RANK, MLA_HEADS, MLA_NOPE + MLA_V)
    k_nope, v = hkv[..., :MLA_NOPE], hkv[..., MLA_NOPE:]
    zk = jnp.zeros((MLA_KV_RANK, MLA_HEADS, LANES - MLA_NOPE), w_ukv.dtype)
    wk = jnp.concatenate([k_nope, zk], axis=-1).reshape(MLA_KV_RANK, MLA_HEADS * LANES)
    zv = jnp.zeros((MLA_KV_RANK, MLA_HEADS, LANES - MLA_V), w_ukv.dtype)
    wv = jnp.concatenate([v, zv], axis=-1).reshape(MLA_KV_RANK, MLA_HEADS * LANES)
    place = np.zeros((LANES, MLA_HEADS * LANES), np.float32)
    ones = np.zeros((1, MLA_HEADS * LANES), np.float32)
    for h in range(MLA_HEADS):
        place[np.arange(MLA_ROPE), h * LANES + MLA_NOPE + np.arange(MLA_ROPE)] = 1.0
        ones[0, h * LANES + MLA_V] = 1.0
    return (w1.astype(BF16), w2.astype(BF16), wk.astype(BF16), wv.astype(BF16), jnp.asarray(place, BF16),
            jnp.asarray(ones, F32))


def _mla_q_kernel(c_ref, nw_ref, w1_ref, w2_ref, cos_ref, sin_ref, o_ref):
    cn = _rms(c_ref[...].astype(F32), nw_ref[...]).astype(BF16)
    cos = jnp.tile(cos_ref[...], (1, MLA_HEADS))
    sin = jnp.tile(sin_ref[...], (1, MLA_HEADS))
    q = _dot(cn, w1_ref[...]) * cos + _dot(cn, w2_ref[...]) * sin
    o_ref[...] = (q * (MLA_QK ** -0.5 * math.log2(math.e))).astype(o_ref.dtype)


def _mla_kv_kernel(c_ref, tail_ref, nw_ref, wk_ref, wv_ref, place_ref, ones_ref, cos_ref, sin_ref,
                   k_ref, v_ref):
    cn = _rms(c_ref[...].astype(F32), nw_ref[...]).astype(BF16)
    kr = tail_ref[...]
    half = MLA_ROPE // 2
    rot = jnp.where(_lane_lt(kr.shape, half), pltpu.roll(kr, LANES - half, 1), pltpu.roll(kr, half, 1))
    k_pe = (kr * cos_ref[...] + rot * sin_ref[...]).astype(BF16)
    k_ref[...] = (_dot(cn, wk_ref[...]) + _dot(k_pe, place_ref[...])).astype(k_ref.dtype)
    v_ref[...] = (_dot(cn, wv_ref[...]) + ones_ref[...]).astype(v_ref.dtype)


def mla_project(proj, tail, q_norm_w, kv_norm_w, w_uq, w_ukv, b, t, tm):
    n = b * t
    nb = t // tm
    w1, w2, wk, wv, place, ones = mla_weights(w_uq, w_ukv)
    q_cos, q_sin, k_cos, k_sin = mla_tables(t)
    wide = MLA_HEADS * LANES
    full = lambda shape: pl.BlockSpec(shape, lambda i: (0, 0))
    tab = pl.BlockSpec((tm, LANES), lambda i: (i % nb, 0))
    qf = pl.pallas_call(
        _mla_q_kernel,
        grid=(n // tm,),
        in_specs=[pl.BlockSpec((tm, SLAB), lambda i: (i, COL_CQ)), full((1, MLA_Q_RANK)),
                  full((MLA_Q_RANK, wide)), full((MLA_Q_RANK, wide)), tab, tab],
        out_specs=pl.BlockSpec((tm, wide), lambda i: (i, 0)),
        out_shape=jax.ShapeDtypeStruct((n, wide), BF16),
        compiler_params=_cparams(("parallel",)),
        name="mla_q_proj",
    )(proj, q_norm_w.reshape(1, MLA_Q_RANK), w1, w2, q_cos, q_sin)
    kf, vf = pl.pallas_call(
        _mla_kv_kernel,
        grid=(n // tm,),
        in_specs=[pl.BlockSpec((tm, SLAB), lambda i: (i, COL_CKV)),
                  pl.BlockSpec((tm, LANES), lambda i: (i, 0)), full((1, MLA_KV_RANK)),
                  full((MLA_KV_RANK, wide)), full((MLA_KV_RANK, wide)), full((LANES, wide)), full((1, wide)),
                  tab, tab],
        out_specs=[pl.BlockSpec((tm, wide), lambda i: (i, 0))] * 2,
        out_shape=[jax.ShapeDtypeStruct((n, wide), BF16)] * 2,
        compiler_params=_cparams(("parallel",)),
        name="mla_kv_proj",
    )(proj, tail, kv_norm_w.reshape(1, MLA_KV_RANK), wk, wv, place, ones, k_cos, k_sin)
    return qf, kf, vf


def _mla_attn_kernel(q_ref, k_ref, v_ref, o_ref, *, t, tk):
    tq = q_ref.shape[0]
    groups = [slice(hh * LANES, (hh + 1) * LANES) for hh in range(2)]
    qs = [q_ref[:, grp] for grp in groups]

    def body(c, carry):
        k0 = pl.multiple_of(c * tk, tk)
        new = []
        for (m, acc), q, grp in zip(carry, qs, groups):
            s = _dot_nt(q, k_ref[pl.ds(k0, tk), grp])
            m_new = jnp.maximum(m, jnp.max(s, axis=-1, keepdims=True))
            p = jnp.exp2((s - m_new).astype(BF16))
            acc = jnp.exp2(m - m_new) * acc + _dot(p, v_ref[pl.ds(k0, tk), grp])
            new.append((m_new, acc))
        return tuple(new)

    init = (jnp.full((tq, 1), -jnp.inf, F32), jnp.zeros((tq, LANES), F32))
    carry = lax.fori_loop(0, t // tk, body, (init, init), unroll=True)
    outs = [acc / acc[:, MLA_V:MLA_V + 1] for _, acc in carry]
    first = _lane_lt((tq, LANES), MLA_V)
    o_ref[...] = jnp.where(first, outs[0], pltpu.roll(outs[1], MLA_V, 1)).astype(o_ref.dtype)


def mla_attention(qf, kf, vf, b, t, tq, tk):
    n = b * t
    nq = t // tq
    return pl.pallas_call(
        functools.partial(_mla_attn_kernel, t=t, tk=tk),
        grid=(b, N_PAIRS, nq),
        in_specs=[
            pl.BlockSpec((tq, 2 * LANES), lambda i, p, s: (i * nq + s, p)),
            pl.BlockSpec((t, 2 * LANES), lambda i, p, s: (i, p)),
            pl.BlockSpec((t, 2 * LANES), lambda i, p, s: (i, p)),
        ],
        out_specs=pl.BlockSpec((tq, HEAD_PAIR), lambda i, p, s: (i * nq + s, p)),
        out_shape=jax.ShapeDtypeStruct((n, SLAB), BF16),
        compiler_params=_cparams(("parallel", "parallel", "arbitrary")),
        name="mla_attention",
    )(qf, kf, vf)


def _in_proj_segments():
    sizes = (SSM_INNER, SSM_CONV_CH, 2 * SSM_HEADS, SLAB, SLAB, SLAB, MLA_Q_RANK, MLA_KV_RANK, MLA_ROPE,
             SLAB, SLAB, SLAB)
    off = [int(v) for v in np.concatenate([[0], np.cumsum(sizes)])]
    main = ((off[0], off[2]), (off[3], off[8]), (off[9], off[12]))
    tail = ((off[8], off[9]), (off[2], off[3]))
    return main, tail


def _in_proj_columns():
    main, tail = _in_proj_segments()
    cols = lambda segs: np.concatenate([np.arange(a, b) for a, b in segs])
    return cols(main), cols(tail)


def in_proj_weights(w_in_l):
    main, tail = _in_proj_segments()
    w_main = jnp.concatenate([w_in_l[:, a:b] for a, b in main], axis=1).astype(BF16)
    pad = jnp.zeros((D_MODEL, LANES - sum(b - a for a, b in tail)), w_in_l.dtype)
    w_tail = jnp.concatenate([w_in_l[:, a:b] for a, b in tail] + [pad], axis=1).astype(BF16)
    return w_main, w_tail


def mixers(proj, tail, p, l, b, t):
    xbc = conv_silu(proj, p["conv_w"][l], p["conv_b"][l], b, t)
    y_f, y_b = ssd_scan(xbc, tail, p["dt_bias"][l], p["a_log"][l], b, t)
    y_ssm = ssd_combine(y_f, y_b, xbc, proj, p["d_skip"][l], p["ssm_norm_w"][l], 1024)

    y_na = na_attention(proj, p["na_rpb"][l], b, t)

    qf, kf, vf = mla_project(proj, tail, p["mla_q_norm_w"][l], p["mla_kv_norm_w"][l],
                             p["mla_w_uq"][l], p["mla_w_ukv"][l], b, t, 512)
    y_mla = mla_attention(qf, kf, vf, b, t, 512, 512)

    qkvs = rope_qkv(proj, b, t, 1024)
    outs, lses = zip(*[band_attention(qkv, w, d) for qkv, (w, d) in zip(qkvs, DIL_PAIRS)])
    y_dil = dil_combine(outs, lses, 1024)
    return y_ssm, y_na, y_mla, y_dil


def kernel(x, attn_norm_w, w_in, conv_w, conv_b, a_log, dt_bias, d_skip, ssm_norm_w, na_rpb,
           mla_q_norm_w, mla_kv_norm_w, mla_w_uq, mla_w_ukv, w_o, ffn_norm_w, ffn_w_gate, ffn_w_up,
           ffn_w_down, router_w, exp_w_gate, exp_w_up, exp_w_down, final_norm_w):
    b, t, _ = x.shape
    n = b * t
    depth = w_in.shape[0]
    p = dict(conv_w=conv_w, conv_b=conv_b, a_log=a_log, dt_bias=dt_bias, d_skip=d_skip,
             ssm_norm_w=ssm_norm_w, na_rpb=na_rpb, mla_q_norm_w=mla_q_norm_w,
             mla_kv_norm_w=mla_kv_norm_w, mla_w_uq=mla_w_uq, mla_w_ukv=mla_w_ukv)
    x = x.reshape(n, D_MODEL)
    cast_rows = 256
    w_o_b = cast_bf16(w_o, cast_rows)
    ffn_b = [cast_bf16(w, cast_rows) for w in (ffn_w_gate, ffn_w_up, ffn_w_down)]
    exp_b = [cast_bf16(w, cast_rows) for w in (exp_w_gate, exp_w_up, exp_w_down)]
    moe_tm = 512
    normed = False
    for l in range(depth):
        w_main, w_tail = in_proj_weights(w_in[l])
        proj = norm_matmul(x, 0, D_MODEL, attn_norm_w[l], w_main, BF16, 1024, PROJ_MAIN // 4)
        tail = norm_matmul(x, 0, D_MODEL, attn_norm_w[l], w_tail, F32, 1024, LANES)
        mix = mixers(proj, tail, p, l, b, t)
        x = out_proj(mix, w_o_b, l, x, 1024, 1024)
        j = l // 2
        if l % 2 == 0:
            x = ffn_dense(x, ffn_norm_w[l], *ffn_b, j, 512, 512)
        else:
            top_i, gates = moe_router(x, ffn_norm_w[l], router_w[j], 512)
            src, pos, tile_expert, tile_valid = moe_plan(top_i, moe_tm)
            y = moe_ffn(x, ffn_norm_w[l], *exp_b, j, src, tile_expert, tile_valid, moe_tm, 512)
            last = l == depth - 1
            x = moe_combine(x, gates, y, pos, final_norm_w if last else None, 256)
            normed = last
    if not normed:
        x = rmsnorm_rows(x, final_norm_w, 1024)
    return x.reshape(b, t, D_MODEL)
```

```python
import functools
import math

import numpy as np
import jax
import jax.numpy as jnp
from jax import lax
from jax.experimental import pallas as pl
from jax.experimental.pallas import tpu as pltpu

F32 = jnp.float32
BF16 = jnp.bfloat16

D_MODEL = 2048
GRID_W = 64
HEAD_DIM = 64
ROPE_THETA = 10000.0
NORM_EPS = 1e-6
NEG_INF = -1e30

SSM_HEADS = 8
SSM_HEAD_DIM = 64
SSM_INNER = SSM_HEADS * SSM_HEAD_DIM
SSM_GROUPS = 2
SSM_STATE = 128
SSM_CONV = 5
SSM_CHUNK = 128
SSM_CONV_CH = SSM_INNER + 2 * SSM_GROUPS * SSM_STATE

NA_HEADS = 8
NA_WIN_ROWS = 8
NA_WIN_COLS = 16
NA_COL_BLOCK = 16
NA_KEY_COLS = 32
NA_ROWS_PER_STEP = 4

MLA_HEADS = 8
MLA_Q_RANK = 512
MLA_KV_RANK = 512
MLA_NOPE = 64
MLA_ROPE = 32
MLA_V = 64

DIL_HEADS = 8
DIL_PAIRS = ((128, 1), (512, 4), (2048, 16))
DIL_QBLOCK = 128
BAND_BLOCKS_PER_STEP = 4

N_EXPERTS = 8
TOP_K = 2

LANES = 128
HEAD_PAIR = 2 * HEAD_DIM
N_PAIRS = 4
SLAB = 512

COL_Z, COL_XBC, COL_NAQ, COL_NAK, COL_NAV, COL_CQ, COL_CKV, COL_DLQ, COL_DLK, COL_DLV = (
    0, 1, 3, 4, 5, 6, 7, 8, 9, 10)
PROJ_MAIN = 11 * SLAB
TAIL_DT = 32

VMEM_LIMIT = 56 * 1024 * 1024


def _cparams(sem, vmem=VMEM_LIMIT):
    return pltpu.CompilerParams(dimension_semantics=sem, vmem_limit_bytes=vmem)


def _lane_lt(shape, bound, period=None):
    lane = lax.broadcasted_iota(jnp.int32, shape, len(shape) - 1)
    if period is not None:
        lane = lane % period
    return lane < bound


def _rms(x, w):
    ms = jnp.mean(x * x, axis=-1, keepdims=True)
    return x * lax.rsqrt(ms + NORM_EPS) * w


def _dot(a, b):
    return jnp.dot(a, b, preferred_element_type=F32)


def _dot_nt(a, b):
    return lax.dot_general(a, b, (((1,), (1,)), ((), ())), preferred_element_type=F32)


def _dot_tn(a, b):
    return lax.dot_general(a, b, (((0,), (0,)), ((), ())), preferred_element_type=F32)


def _cast_kernel(x_ref, o_ref):
    o_ref[...] = x_ref[...].astype(o_ref.dtype)


def cast_bf16(w, tr):
    shape = w.shape
    w2 = w.reshape(-1, shape[-1])
    r, c = w2.shape
    out = pl.pallas_call(
        _cast_kernel,
        grid=(r // tr,),
        in_specs=[pl.BlockSpec((tr, c), lambda i: (i, 0))],
        out_specs=pl.BlockSpec((tr, c), lambda i: (i, 0)),
        out_shape=jax.ShapeDtypeStruct((r, c), BF16),
        compiler_params=_cparams(("parallel",)),
        name="cast_bf16",
    )(w2)
    return out.reshape(shape)


def _in_proj_kernel(x_ref, nw_ref, w_ref, wt_ref, o_ref, t_ref, h_ref):
    @pl.when(pl.program_id(1) == 0)
    def _():
        h = _rms(x_ref[...], nw_ref[...]).astype(BF16)
        h_ref[...] = h
        t_ref[...] = _dot(h, wt_ref[...])

    o_ref[...] = _dot(h_ref[...], w_ref[...]).astype(o_ref.dtype)


def in_proj(x, nw, w_main, w_tail, tm, tn):
    n, k = x.shape
    nout = w_main.shape[1]
    return pl.pallas_call(
        _in_proj_kernel,
        grid=(n // tm, nout // tn),
        in_specs=[
            pl.BlockSpec((tm, k), lambda i, j: (i, 0)),
            pl.BlockSpec((1, k), lambda i, j: (0, 0)),
            pl.BlockSpec((k, tn), lambda i, j: (0, j)),
            pl.BlockSpec((k, LANES), lambda i, j: (0, 0)),
        ],
        out_specs=[pl.BlockSpec((tm, tn), lambda i, j: (i, j)), pl.BlockSpec((tm, LANES), lambda i, j: (i, 0))],
        out_shape=[jax.ShapeDtypeStruct((n, nout), BF16), jax.ShapeDtypeStruct((n, LANES), F32)],
        scratch_shapes=[pltpu.VMEM((tm, k), BF16)],
        compiler_params=_cparams(("parallel", "arbitrary")),
        name="in_proj",
    )(x, nw.reshape(1, k), w_main, w_tail)


def _out_proj_kernel(a0_ref, a1_ref, a2_ref, a3_ref, w_ref, r_ref, o_ref):
    acc = r_ref[...]
    for s, a_ref in enumerate((a0_ref, a1_ref, a2_ref, a3_ref)):
        acc = acc + _dot(a_ref[...], w_ref[s * SLAB:(s + 1) * SLAB, :])
    o_ref[...] = acc


def out_proj(mix, w, layer, res, tm, tn):
    n = res.shape[0]
    return pl.pallas_call(
        _out_proj_kernel,
        grid=(n // tm, D_MODEL // tn),
        in_specs=[pl.BlockSpec((tm, SLAB), lambda i, j: (i, 0))] * 4 + [
            pl.BlockSpec((None, 4 * SLAB, tn), lambda i, j: (layer, 0, j)),
            pl.BlockSpec((tm, tn), lambda i, j: (i, j)),
        ],
        out_specs=pl.BlockSpec((tm, tn), lambda i, j: (i, j)),
        out_shape=jax.ShapeDtypeStruct((n, D_MODEL), F32),
        compiler_params=_cparams(("parallel", "arbitrary")),
        name="out_proj",
    )(*mix, w, res)


def _ffn_kernel(x_ref, nw_ref, wg_ref, wu_ref, wd_ref, o_ref, h_ref):
    @pl.when(pl.program_id(1) == 0)
    def _():
        x = x_ref[...]
        h_ref[...] = _rms(x, nw_ref[...]).astype(BF16)
        o_ref[...] = x

    _swiglu_rows(h_ref, wg_ref, wu_ref, wd_ref, o_ref)


FFN_ROW_CHUNK = 512


def _swiglu_rows(h_ref, wg_ref, wu_ref, wd_ref, o_ref):
    tm = h_ref.shape[0]
    for r0 in range(0, tm, FFN_ROW_CHUNK):
        rows = slice(r0, r0 + FFN_ROW_CHUNK)
        h = h_ref[rows, :]
        g = _dot(h, wg_ref[...])
        u = _dot(h, wu_ref[...])
        a = (g * jax.nn.sigmoid(g) * u).astype(BF16)
        o_ref[rows, :] += _dot(a, wd_ref[...])


def ffn_dense(x, nw, wg, wu, wd, layer, tm, tf):
    n = x.shape[0]
    d_ff = wg.shape[-1]
    return pl.pallas_call(
        _ffn_kernel,
        grid=(n // tm, d_ff // tf),
        in_specs=[
            pl.BlockSpec((tm, D_MODEL), lambda i, j: (i, 0)),
            pl.BlockSpec((1, D_MODEL), lambda i, j: (0, 0)),
            pl.BlockSpec((None, D_MODEL, tf), lambda i, j: (layer, 0, j)),
            pl.BlockSpec((None, D_MODEL, tf), lambda i, j: (layer, 0, j)),
            pl.BlockSpec((None, tf, D_MODEL), lambda i, j: (layer, j, 0)),
        ],
        out_specs=pl.BlockSpec((tm, D_MODEL), lambda i, j: (i, 0)),
        out_shape=jax.ShapeDtypeStruct((n, D_MODEL), F32),
        scratch_shapes=[pltpu.VMEM((tm, D_MODEL), BF16)],
        compiler_params=_cparams(("parallel", "arbitrary")),
        name="ffn_dense",
    )(x, nw.reshape(1, D_MODEL), wg, wu, wd)


def _router_kernel(x_ref, nw_ref, rw_ref, idx_ref, gate_ref):
    h = _rms(x_ref[...], nw_ref[...])
    logits = jnp.dot(h, rw_ref[...], preferred_element_type=F32, precision=lax.Precision.HIGHEST)
    lane = lax.broadcasted_iota(jnp.int32, logits.shape, 1)
    logits = jnp.where(lane < N_EXPERTS, logits, -jnp.inf)
    m1 = jnp.max(logits, axis=-1, keepdims=True)
    i1 = jnp.min(jnp.where(logits == m1, lane, LANES), axis=-1, keepdims=True)
    rest = jnp.where(lane == i1, -jnp.inf, logits)
    m2 = jnp.max(rest, axis=-1, keepdims=True)
    i2 = jnp.min(jnp.where(rest == m2, lane, LANES), axis=-1, keepdims=True)
    e2 = jnp.exp(m2 - m1)
    g1 = 1.0 / (1.0 + e2)
    g2 = e2 / (1.0 + e2)
    idx_ref[...] = jnp.where(lane == 0, i1, i2)[:, :TOP_K]
    gate_ref[...] = jnp.where(lane == 0, g1, g2)[:, :TOP_K]


def moe_router(x, nw, router_w, tm):
    n = x.shape[0]
    rw = jnp.zeros((D_MODEL, LANES), F32).at[:, :N_EXPERTS].set(router_w)
    return pl.pallas_call(
        _router_kernel,
        grid=(n // tm,),
        in_specs=[
            pl.BlockSpec((tm, D_MODEL), lambda i: (i, 0)),
            pl.BlockSpec((1, D_MODEL), lambda i: (0, 0)),
            pl.BlockSpec((D_MODEL, LANES), lambda i: (0, 0)),
        ],
        out_specs=[pl.BlockSpec((tm, TOP_K), lambda i: (i, 0)),
                   pl.BlockSpec((tm, TOP_K), lambda i: (i, 0))],
        out_shape=[jax.ShapeDtypeStruct((n, TOP_K), jnp.int32),
                   jax.ShapeDtypeStruct((n, TOP_K), F32)],
        compiler_params=_cparams(("parallel",)),
        name="moe_router",
    )(x, nw.reshape(1, D_MODEL), rw)


def moe_plan(top_i, tm):
    n = top_i.shape[0]
    flat_e = top_i.reshape(-1)
    onehot = (flat_e[:, None] == jnp.arange(N_EXPERTS, dtype=jnp.int32)[None, :]).astype(jnp.int32)
    csum = jnp.cumsum(onehot, axis=0)
    counts = csum[-1]
    rank = jnp.sum(onehot * csum, axis=1) - 1
    padded = ((counts + tm - 1) // tm) * tm
    pend = jnp.cumsum(padded)
    pstart = pend - padded
    pos = pstart[flat_e] + rank
    n_slots = n * TOP_K + N_EXPERTS * tm
    n_tiles = n_slots // tm
    src = jnp.zeros((n_slots,), jnp.int32).at[pos].set(jnp.arange(n * TOP_K, dtype=jnp.int32) // TOP_K)
    tile_start = jnp.arange(n_tiles, dtype=jnp.int32) * tm
    tile_expert = jnp.sum((tile_start[:, None] >= pend[None, :]).astype(jnp.int32), axis=1)
    tile_valid = (tile_start < pend[-1]).astype(jnp.int32)
    last_valid = jnp.maximum(pend[-1] // tm - 1, 0)
    tile_expert = jnp.where(tile_valid == 1, tile_expert, tile_expert[last_valid]).astype(jnp.int32)
    return src, pos.reshape(n, TOP_K).astype(jnp.int32), tile_expert, tile_valid


def _moe_ffn_kernel(te_ref, tv_ref, src_ref, nsrc_ref, x_hbm, nw_ref, wg_ref, wu_ref, wd_ref, y_ref,
                    xbuf, h_ref, sem, *, tm):
    i = pl.program_id(0)
    j = pl.program_id(1)
    n_tiles = pl.num_programs(0)
    valid = tv_ref[i] == 1

    def row_copy(idx_ref, r):
        tok = idx_ref[0, 0, r]
        return pltpu.make_async_copy(x_hbm.at[pl.ds(tok, 1)], xbuf.at[pl.ds(r, 1)], sem)

    def gather(idx_ref):
        def start(r, c):
            row_copy(idx_ref, r).start()
            return c

        lax.fori_loop(0, tm, start, 0, unroll=8)

    @pl.when(j == 0)
    def _():
        y_ref[...] = jnp.zeros_like(y_ref)

    @pl.when(jnp.logical_and(j == 0, valid))
    def _():
        @pl.when(i == 0)
        def _():
            gather(src_ref)

        pltpu.make_async_copy(x_hbm.at[pl.ds(0, tm)], xbuf, sem).wait()
        h_ref[...] = _rms(xbuf[...], nw_ref[...]).astype(BF16)

        nxt = jnp.minimum(i + 1, n_tiles - 1)

        @pl.when(jnp.logical_and(i + 1 < n_tiles, tv_ref[nxt] == 1))
        def _():
            gather(nsrc_ref)

    @pl.when(valid)
    def _():
        _swiglu_rows(h_ref, wg_ref, wu_ref, wd_ref, y_ref)


def moe_ffn(x, nw, wg, wu, wd, layer, src, tile_expert, tile_valid, tm, tf):
    n_slots = src.shape[0]
    n_tiles = n_slots // tm
    d_ff = wg.shape[-1]
    nf = d_ff // tf

    def wcol(i, j, te_ref, tv_ref):
        return (layer, te_ref[i], 0, jnp.where(tv_ref[i] == 1, j, nf - 1))

    def wrow(i, j, te_ref, tv_ref):
        return (layer, te_ref[i], jnp.where(tv_ref[i] == 1, j, nf - 1), 0)

    grid_spec = pltpu.PrefetchScalarGridSpec(
        num_scalar_prefetch=2,
        grid=(n_tiles, nf),
        in_specs=[
            pl.BlockSpec((1, 1, tm), lambda i, j, *_: (i, 0, 0), memory_space=pltpu.SMEM),
            pl.BlockSpec((1, 1, tm), lambda i, j, *_: (jnp.minimum(i + 1, n_tiles - 1), 0, 0),
                         memory_space=pltpu.SMEM),
            pl.BlockSpec(memory_space=pl.ANY),
            pl.BlockSpec((1, D_MODEL), lambda i, j, *_: (0, 0)),
            pl.BlockSpec((None, None, D_MODEL, tf), wcol),
            pl.BlockSpec((None, None, D_MODEL, tf), wcol),
            pl.BlockSpec((None, None, tf, D_MODEL), wrow),
        ],
        out_specs=pl.BlockSpec((tm, D_MODEL), lambda i, j, *_: (i, 0)),
        scratch_shapes=[pltpu.VMEM((tm, D_MODEL), F32), pltpu.VMEM((tm, D_MODEL), BF16),
                        pltpu.SemaphoreType.DMA],
    )
    src3 = src.reshape(n_tiles, 1, tm)
    return pl.pallas_call(
        functools.partial(_moe_ffn_kernel, tm=tm),
        grid_spec=grid_spec,
        out_shape=jax.ShapeDtypeStruct((n_slots, D_MODEL), F32),
        compiler_params=_cparams(("arbitrary", "arbitrary")),
        name="moe_ffn",
    )(tile_expert, tile_valid, src3, src3, x, nw.reshape(1, D_MODEL), wg, wu, wd)


def _moe_combine_kernel(pos_ref, npos_ref, x_ref, gate_ref, y_hbm, fw_ref, o_ref, ybuf, sem, *, tm, final_norm):
    i = pl.program_id(0)
    n_tiles = pl.num_programs(0)
    cur = i % 2

    def row_copy(idx_ref, buf, r, k):
        slot = idx_ref[0, 0, r * TOP_K + k]
        return pltpu.make_async_copy(y_hbm.at[pl.ds(slot, 1)], ybuf.at[buf, k, pl.ds(r, 1)], sem.at[buf])

    def gather(idx_ref, buf):
        def start(r, c):
            for k in range(TOP_K):
                row_copy(idx_ref, buf, r, k).start()
            return c

        lax.fori_loop(0, tm, start, 0, unroll=8)

    @pl.when(i == 0)
    def _():
        gather(pos_ref, 0)

    @pl.when(i + 1 < n_tiles)
    def _():
        gather(npos_ref, 1 - cur)

    for k in range(TOP_K):
        pltpu.make_async_copy(y_hbm.at[pl.ds(0, tm)], ybuf.at[cur, k], sem.at[cur]).wait()
    gates = gate_ref[...]
    out = x_ref[...]
    for k in range(TOP_K):
        out = out + gates[:, k:k + 1] * ybuf[cur, k]
    if final_norm:
        out = _rms(out, fw_ref[...])
    o_ref[...] = out


def moe_combine(x, gates, y, pos, final_w, tm):
    n = x.shape[0]
    final_norm = final_w is not None
    fw = final_w if final_norm else jnp.ones((D_MODEL,), F32)
    n_tiles = n // tm
    pos3 = pos.reshape(n_tiles, 1, tm * TOP_K)
    return pl.pallas_call(
        functools.partial(_moe_combine_kernel, tm=tm, final_norm=final_norm),
        grid=(n_tiles,),
        in_specs=[
            pl.BlockSpec((1, 1, tm * TOP_K), lambda i: (i, 0, 0), memory_space=pltpu.SMEM),
            pl.BlockSpec((1, 1, tm * TOP_K), lambda i: (jnp.minimum(i + 1, n_tiles - 1), 0, 0),
                         memory_space=pltpu.SMEM),
            pl.BlockSpec((tm, D_MODEL), lambda i: (i, 0)),
            pl.BlockSpec((tm, TOP_K), lambda i: (i, 0)),
            pl.BlockSpec(memory_space=pl.ANY),
            pl.BlockSpec((1, D_MODEL), lambda i: (0, 0)),
        ],
        out_specs=pl.BlockSpec((tm, D_MODEL), lambda i: (i, 0)),
        out_shape=jax.ShapeDtypeStruct((n, D_MODEL), F32),
        scratch_shapes=[pltpu.VMEM((2, TOP_K, tm, D_MODEL), F32), pltpu.SemaphoreType.DMA((2,))],
        compiler_params=_cparams(("arbitrary",)),
        name="moe_combine",
    )(pos3, pos3, x, gates, y, fw.reshape(1, D_MODEL))


def _rmsnorm_kernel(x_ref, w_ref, o_ref):
    o_ref[...] = _rms(x_ref[...], w_ref[...])


def rmsnorm_rows(x, w, tm):
    n = x.shape[0]
    return pl.pallas_call(
        _rmsnorm_kernel,
        grid=(n // tm,),
        in_specs=[pl.BlockSpec((tm, D_MODEL), lambda i: (i, 0)),
                  pl.BlockSpec((1, D_MODEL), lambda i: (0, 0))],
        out_specs=pl.BlockSpec((tm, D_MODEL), lambda i: (i, 0)),
        out_shape=jax.ShapeDtypeStruct((n, D_MODEL), F32),
        compiler_params=_cparams(("parallel",)),
        name="final_norm",
    )(x, w.reshape(1, D_MODEL))


CONV_PAD = 8


def _conv_kernel(x_ref, w_ref, b_ref, o_ref, pad_ref, *, t):
    half = SSM_CONV // 2
    zeros = jnp.zeros((CONV_PAD, pad_ref.shape[1]), F32)
    pad_ref[0:CONV_PAD, :] = zeros
    pad_ref[CONV_PAD + t:CONV_PAD + t + CONV_PAD, :] = zeros
    pad_ref[CONV_PAD:CONV_PAD + t, :] = x_ref[...].astype(F32)
    acc = jnp.zeros(o_ref.shape, F32) + b_ref[...]
    for k in range(SSM_CONV):
        acc = acc + pad_ref[pl.ds(CONV_PAD - half + k, t), :] * w_ref[k:k + 1, :]
    o_ref[...] = (acc * jax.nn.sigmoid(acc)).astype(o_ref.dtype)


def conv_silu(proj, conv_w, conv_b, b, t):
    tc = 256
    nblk = SSM_CONV_CH // tc
    col0 = COL_XBC * SLAB // tc
    return pl.pallas_call(
        functools.partial(_conv_kernel, t=t),
        grid=(b, nblk),
        in_specs=[
            pl.BlockSpec((t, tc), lambda i, j: (i, col0 + j)),
            pl.BlockSpec((SSM_CONV, tc), lambda i, j: (0, j)),
            pl.BlockSpec((1, tc), lambda i, j: (0, j)),
        ],
        out_specs=pl.BlockSpec((t, tc), lambda i, j: (i, j)),
        out_shape=jax.ShapeDtypeStruct((b * t, SSM_CONV_CH), BF16),
        scratch_shapes=[pltpu.VMEM((t + 2 * CONV_PAD, tc), F32)],
        compiler_params=_cparams(("parallel", "parallel")),
        name="conv_silu",
    )(proj, conv_w, conv_b.reshape(1, SSM_CONV_CH))


def _ssd_chunk(xbc_ref, tail_ref, bias_ref, alog_ref, tri_ref, y_ref, state_ref, direction):
    q = SSM_CHUNK
    dt = jax.nn.softplus(tail_ref[...] + bias_ref[...])
    da = dt * (-jnp.exp(alog_ref[...]))
    cs = jnp.dot(tri_ref[...], da, preferred_element_type=F32, precision=lax.Precision.HIGHEST)
    total = cs[q - 1:q, :]
    if direction == 0:
        e_out = cs
        e_in = total - cs
        e_seg = cs
    else:
        ex = cs - da
        e_out = total - ex
        e_in = ex
        e_seg = -ex
    e_seg_t = jnp.transpose(e_seg)
    dec_out = jnp.exp(e_out)
    dec_in_dt = jnp.exp(e_in) * dt
    dt_t = jnp.transpose(dt)
    dec_tot = jnp.exp(total)

    row = lax.broadcasted_iota(jnp.int32, (q, q), 0)
    col = lax.broadcasted_iota(jnp.int32, (q, q), 1)
    keep = (row >= col) if direction == 0 else (col >= row)
    first_half = _lane_lt((q, HEAD_PAIR), HEAD_DIM)

    xs = xbc_ref[:, 0:SSM_INNER]
    heads_per_group = SSM_HEADS // SSM_GROUPS
    for g in range(SSM_GROUPS):
        bm = xbc_ref[:, SSM_INNER + g * SSM_STATE:SSM_INNER + (g + 1) * SSM_STATE]
        cm = xbc_ref[:, SSM_INNER + (SSM_GROUPS + g) * SSM_STATE:SSM_INNER + (SSM_GROUPS + g + 1) * SSM_STATE]
        cb = _dot_nt(cm, bm)
        bm_f = bm.astype(F32)
        cm_f = cm.astype(F32)
        for pp in range(heads_per_group // 2):
            pair = g * (heads_per_group // 2) + pp
            xs_pair = xs[:, pair * HEAD_PAIR:(pair + 1) * HEAD_PAIR]
            st = state_ref[pair]
            ys, sts, decs = [], [], []
            for hh in range(2):
                lane = TAIL_DT + direction * SSM_HEADS + pair * 2 + hh
                seg = e_seg[:, lane:lane + 1] - e_seg_t[lane:lane + 1, :]
                lmat = jnp.where(keep, jnp.exp(seg), 0.0)
                w = (cb * lmat * dt_t[lane:lane + 1, :]).astype(BF16)
                y = _dot(w, xs_pair)
                y = y + _dot((cm_f * dec_out[:, lane:lane + 1]).astype(BF16), st.astype(BF16))
                ys.append(y)
                sts.append(_dot_tn((bm_f * dec_in_dt[:, lane:lane + 1]).astype(BF16), xs_pair))
                decs.append(dec_tot[:, lane:lane + 1])
            y_ref[:, pair * HEAD_PAIR:(pair + 1) * HEAD_PAIR] = jnp.where(first_half, ys[0], ys[1])
            first_half_s = _lane_lt(st.shape, HEAD_DIM)
            state_ref[pair] = (st * jnp.where(first_half_s, decs[0], decs[1])
                               + jnp.where(first_half_s, sts[0], sts[1]))


def _ssd_kernel(xbc_f_ref, tail_f_ref, xbc_b_ref, tail_b_ref, bias_ref, alog_ref, tri_ref,
                y_f_ref, y_b_ref, state_ref):
    @pl.when(pl.program_id(1) == 0)
    def _():
        state_ref[...] = jnp.zeros_like(state_ref)

    _ssd_chunk(xbc_f_ref, tail_f_ref, bias_ref, alog_ref, tri_ref, y_f_ref, state_ref.at[0], 0)
    _ssd_chunk(xbc_b_ref, tail_b_ref, bias_ref, alog_ref, tri_ref, y_b_ref, state_ref.at[1], 1)


def ssd_scan(xbc, tail, dt_bias, a_log, b, t):
    q = SSM_CHUNK
    nc = t // q
    bias_row = jnp.zeros((1, LANES), F32).at[0, TAIL_DT:TAIL_DT + 2 * SSM_HEADS].set(dt_bias.reshape(-1))
    alog_row = jnp.zeros((1, LANES), F32).at[0, TAIL_DT:TAIL_DT + 2 * SSM_HEADS].set(a_log.reshape(-1))
    tri = jnp.asarray(np.tril(np.ones((q, q), np.float32)))
    fwd = lambda i, c: (i * nc + c, 0)
    bwd = lambda i, c: (i * nc + nc - 1 - c, 0)
    const = lambda i, c: (0, 0)
    return pl.pallas_call(
        _ssd_kernel,
        grid=(b, nc),
        in_specs=[
            pl.BlockSpec((q, SSM_CONV_CH), fwd), pl.BlockSpec((q, LANES), fwd),
            pl.BlockSpec((q, SSM_CONV_CH), bwd), pl.BlockSpec((q, LANES), bwd),
            pl.BlockSpec((1, LANES), const), pl.BlockSpec((1, LANES), const), pl.BlockSpec((q, q), const),
        ],
        out_specs=[pl.BlockSpec((q, SSM_INNER), fwd), pl.BlockSpec((q, SSM_INNER), bwd)],
        out_shape=[jax.ShapeDtypeStruct((b * t, SSM_INNER), F32)] * 2,
        scratch_shapes=[pltpu.VMEM((2, N_PAIRS, SSM_STATE, HEAD_PAIR), F32)],
        compiler_params=_cparams(("parallel", "arbitrary")),
        name="ssd_scan",
    )(xbc, tail, xbc, tail, bias_row, alog_row, tri)


def _ssd_combine_kernel(yf_ref, yb_ref, xs_ref, z_ref, d_ref, nw_ref, o_ref):
    y = yf_ref[...] + yb_ref[...] + xs_ref[...].astype(F32) * d_ref[...]
    z = z_ref[...].astype(F32)
    o_ref[...] = _rms(y * (z * jax.nn.sigmoid(z)), nw_ref[...]).astype(o_ref.dtype)


def ssd_combine(y_f, y_b, xbc, proj, d_skip, norm_w, tm):
    n = y_f.shape[0]
    d_row = jnp.repeat(d_skip, SSM_HEAD_DIM).reshape(1, SSM_INNER)
    row = lambda i: (i, 0)
    return pl.pallas_call(
        _ssd_combine_kernel,
        grid=(n // tm,),
        in_specs=[
            pl.BlockSpec((tm, SSM_INNER), row),
            pl.BlockSpec((tm, SSM_INNER), row),
            pl.BlockSpec((tm, SSM_INNER), row),
            pl.BlockSpec((tm, SLAB), lambda i: (i, COL_Z)),
            pl.BlockSpec((1, SSM_INNER), lambda i: (0, 0)),
            pl.BlockSpec((1, SSM_INNER), lambda i: (0, 0)),
        ],
        out_specs=pl.BlockSpec((tm, SSM_INNER), row),
        out_shape=jax.ShapeDtypeStruct((n, SSM_INNER), BF16),
        compiler_params=_cparams(("parallel",)),
        name="ssd_combine",
    )(y_f, y_b, xbc, proj, d_row, norm_w.reshape(1, SSM_INNER))


def _pair_scores(q2, k2):
    first_q = _lane_lt(q2.shape, HEAD_DIM)
    zero = jnp.zeros_like(q2)
    qs = jnp.concatenate([jnp.where(first_q, q2, zero), jnp.where(first_q, zero, q2)], axis=0)
    return _dot_nt(qs, k2)


def _pair_attend(s, v2):
    tq = s.shape[0] // 2
    m = jnp.max(s, axis=-1, keepdims=True)
    p = jnp.exp(s - m)
    l = jnp.sum(p, axis=-1, keepdims=True)
    o = _dot(p.astype(BF16), v2) / l
    lse = m + jnp.log(l)
    first_o = _lane_lt((tq, HEAD_PAIR), HEAD_DIM)
    return jnp.where(first_o, o[:tq], o[tq:]), jnp.where(first_o, lse[:tq], lse[tq:])


def na_bias_tables(rpb, rows):
    kr = min(NA_WIN_ROWS, rows)
    qc = np.arange(GRID_W)
    kc = np.arange(GRID_W)
    q_start = np.clip(qc - NA_WIN_COLS // 2, 0, GRID_W - NA_WIN_COLS)
    col_in = (kc[None, :] >= q_start[:, None]) & (kc[None, :] < q_start[:, None] + NA_WIN_COLS)
    col_off = np.clip(kc[None, :] - qc[:, None] + NA_WIN_COLS - 1, 0, 2 * NA_WIN_COLS - 2)
    onehot = (col_off[None] == np.arange(2 * NA_WIN_COLS - 1)[:, None, None]).astype(np.float32)
    expanded = jnp.einsum("hrc,cqk->hqrk", rpb, jnp.asarray(onehot), precision=lax.Precision.HIGHEST)
    expanded = jnp.where(jnp.asarray(col_in)[None, :, None, :], expanded, NEG_INF)

    def table(r):
        row_start = int(np.clip(r - kr // 2, 0, rows - kr))
        ro0 = row_start - r + NA_WIN_ROWS - 1
        return expanded[:, :, ro0:ro0 + kr, :].reshape(N_PAIRS, 2 * GRID_W, kr * GRID_W)

    rs = NA_ROWS_PER_STEP
    lo = [table(r) for r in range(rs)]
    mid = [table(min(rs, rows - 1))] * rs
    hi = [table(r) for r in range(rows - rs, rows)]
    return jnp.stack([jnp.stack(lo), jnp.stack(mid), jnp.stack(hi)])


def _na_kernel(q_ref, k_ref, v_ref, bias_ref, o_ref, *, rows, kr):
    step = pl.program_id(1)
    rs = NA_ROWS_PER_STEP
    for rr in range(rs):
        r = step * rs + rr
        row_start = jnp.clip(r - kr // 2, 0, rows - kr)
        k0 = pl.multiple_of(row_start * GRID_W, GRID_W)
        pair_cols = [slice(pair * HEAD_PAIR, (pair + 1) * HEAD_PAIR) for pair in range(N_PAIRS)]
        scores = []
        for pair, cols in enumerate(pair_cols):
            q2 = q_ref[rr * GRID_W:(rr + 1) * GRID_W, cols] * jnp.asarray(HEAD_DIM ** -0.5, BF16)
            scores.append(_pair_scores(q2, k_ref[pl.ds(k0, kr * GRID_W), cols]) + bias_ref[0, rr, pair])
        for s, cols in zip(scores, pair_cols):
            o, _ = _pair_attend(s, v_ref[pl.ds(k0, kr * GRID_W), cols])
            o_ref[rr * GRID_W:(rr + 1) * GRID_W, cols] = o.astype(o_ref.dtype)


def na_attention(proj, rpb, b, t):
    rows = t // GRID_W
    kr = min(NA_WIN_ROWS, rows)
    rs = NA_ROWS_PER_STEP
    nsteps = rows // rs
    bias = na_bias_tables(rpb, rows)

    def kind(i, s):
        return jnp.where(s == 0, 0, jnp.where(s == nsteps - 1, 2, 1))

    return pl.pallas_call(
        functools.partial(_na_kernel, rows=rows, kr=kr),
        grid=(b, nsteps),
        in_specs=[
            pl.BlockSpec((rs * GRID_W, SLAB), lambda i, s: (i * nsteps + s, COL_NAQ)),
            pl.BlockSpec((t, SLAB), lambda i, s: (i, COL_NAK)),
            pl.BlockSpec((t, SLAB), lambda i, s: (i, COL_NAV)),
            pl.BlockSpec((1, rs, N_PAIRS, 2 * GRID_W, kr * GRID_W), lambda i, s: (kind(i, s), 0, 0, 0, 0)),
        ],
        out_specs=pl.BlockSpec((rs * GRID_W, SLAB), lambda i, s: (i * nsteps + s, 0)),
        out_shape=jax.ShapeDtypeStruct((b * t, SLAB), BF16),
        compiler_params=_cparams(("parallel", "arbitrary")),
        name="na_attention",
    )(proj, proj, proj, bias)


def _rope_angles(t, d):
    inv = ROPE_THETA ** (-np.arange(0, d, 2, dtype=np.float32) / d)
    return np.arange(t, dtype=np.float32)[:, None] * inv[None, :]


def rope_tables_pair(t):
    ang = _rope_angles(t, HEAD_DIM)
    cos = np.tile(np.cos(ang), (1, 4))
    sin = np.tile(np.concatenate([-np.sin(ang), np.sin(ang)], axis=1), (1, 2))
    return jnp.asarray(cos, F32), jnp.asarray(sin, F32)


FOLD_CHUNK = 256
FOLD_DILS = tuple(d for _, d in DIL_PAIRS if d > 1)


def fold_permutation(dil):
    per = FOLD_CHUNK // dil
    perm = np.zeros((FOLD_CHUNK, FOLD_CHUNK), np.float32)
    dst = np.arange(FOLD_CHUNK)
    perm[dst, (dst % per) * dil + dst // per] = 1.0
    return jnp.asarray(perm, BF16)


def _rope_qkv_kernel(x_ref, v_ref, cos_ref, sin_ref, *rest):
    nd = len(FOLD_DILS)
    perm_refs, o_ref, fold_refs = rest[:nd], rest[nd], rest[nd + 1:]
    cos = cos_ref[...]
    sin = sin_ref[...]
    half = HEAD_DIM // 2
    for c in range(x_ref.shape[1] // LANES):
        x = x_ref[:, c * LANES:(c + 1) * LANES].astype(F32)
        rot = jnp.where(_lane_lt(x.shape, half, HEAD_DIM),
                        pltpu.roll(x, LANES - half, 1), pltpu.roll(x, half, 1))
        y = x * cos + rot * sin
        if c < N_PAIRS:
            y = y * (HEAD_DIM ** -0.5)
        o_ref[0, :, c * LANES:(c + 1) * LANES] = y.astype(o_ref.dtype)
    o_ref[0, :, 2 * SLAB:3 * SLAB] = v_ref[...]
    tm = x_ref.shape[0]
    for dil, perm_ref, f_ref in zip(FOLD_DILS, perm_refs, fold_refs):
        per = FOLD_CHUNK // dil
        for c in range(tm // FOLD_CHUNK):
            folded = _dot(perm_ref[...], o_ref[0, c * FOLD_CHUNK:(c + 1) * FOLD_CHUNK, :]).astype(f_ref.dtype)
            for p in range(dil):
                f_ref[p, c * per:(c + 1) * per, :] = folded[p * per:(p + 1) * per, :]


def rope_qkv(proj, b, t, tm):
    n = b * t
    cos, sin = rope_tables_pair(t)
    nb = t // tm
    fold_spec = lambda d: pl.BlockSpec((None, d, tm // d, 3 * SLAB), lambda i: (i // nb, 0, i % nb, 0))
    outs = pl.pallas_call(
        _rope_qkv_kernel,
        grid=(n // tm,),
        in_specs=[
            pl.BlockSpec((tm, 2 * SLAB), lambda i: (i, COL_DLQ // 2)),
            pl.BlockSpec((tm, SLAB), lambda i: (i, COL_DLV)),
            pl.BlockSpec((tm, LANES), lambda i: (i % nb, 0)),
            pl.BlockSpec((tm, LANES), lambda i: (i % nb, 0)),
        ] + [pl.BlockSpec((FOLD_CHUNK, FOLD_CHUNK), lambda i: (0, 0))] * len(FOLD_DILS),
        out_specs=[fold_spec(1)] + [fold_spec(d) for d in FOLD_DILS],
        out_shape=[jax.ShapeDtypeStruct((b, d, t // d, 3 * SLAB), BF16) for d in (1,) + FOLD_DILS],
        compiler_params=_cparams(("parallel",)),
        name="rope_qkv",
    )(proj, proj, cos, sin, *[fold_permutation(d) for d in FOLD_DILS])
    by_dil = dict(zip((1,) + FOLD_DILS, outs))
    return [by_dil[d] for _, d in DIL_PAIRS]


def _band_kernel(q_ref, k_ref, v_ref, o_ref, lse_ref, *, sub, half, span):
    tq = DIL_QBLOCK
    blocks = q_ref.shape[0] // tq
    for blk in range(blocks):
        qb = pl.program_id(2) * blocks + blk
        rows = slice(blk * tq, (blk + 1) * tq)
        start = jnp.clip(qb * tq - half, 0, sub - span)
        start = pl.multiple_of(start, half)
        q_pos = qb * tq + lax.broadcasted_iota(jnp.int32, (2 * tq, span), 0) % tq
        k_pos = start + lax.broadcasted_iota(jnp.int32, (2 * tq, span), 1)
        valid = jnp.abs(k_pos - q_pos) <= half
        pair_cols = [slice(pair * HEAD_PAIR, (pair + 1) * HEAD_PAIR) for pair in range(N_PAIRS)]
        scores = [jnp.where(valid, _pair_scores(q_ref[rows, cols], k_ref[pl.ds(start, span), cols]), NEG_INF)
                  for cols in pair_cols]
        for s, cols in zip(scores, pair_cols):
            o, lse = _pair_attend(s, v_ref[pl.ds(start, span), cols])
            o_ref[rows, cols] = o
            lse_ref[rows, cols] = lse


def band_attention(qkv, window, dil):
    b, _, sub, _ = qkv.shape
    half = window // (2 * dil)
    span = DIL_QBLOCK + 2 * half
    tq = DIL_QBLOCK * min(BAND_BLOCKS_PER_STEP, sub // DIL_QBLOCK)
    nqb = sub // tq
    return pl.pallas_call(
        functools.partial(_band_kernel, sub=sub, half=half, span=span),
        grid=(b, dil, nqb),
        in_specs=[
            pl.BlockSpec((None, None, tq, SLAB), lambda i, p, s: (i, p, s, 0)),
            pl.BlockSpec((None, None, sub, SLAB), lambda i, p, s: (i, p, 0, 1)),
            pl.BlockSpec((None, None, sub, SLAB), lambda i, p, s: (i, p, 0, 2)),
        ],
        out_specs=[pl.BlockSpec((None, None, tq, SLAB), lambda i, p, s: (i, p, s, 0))] * 2,
        out_shape=[jax.ShapeDtypeStruct((b, dil, sub, SLAB), F32)] * 2,
        compiler_params=_cparams(("parallel", "parallel", "arbitrary")),
        name="band_attention_d%d" % dil,
    )(qkv, qkv, qkv)


def _dil_combine_kernel(*refs):
    nbr = len(DIL_PAIRS)
    o_refs, l_refs, out_ref = refs[:nbr], refs[nbr:2 * nbr], refs[2 * nbr]
    scratch = iter(refs[2 * nbr + 1:])

    def token_order(ref):
        dil = ref.shape[0]
        if dil == 1:
            return ref[0]
        buf = next(scratch)
        per = ref.shape[1]
        for p in range(dil):
            for c in range(SLAB // LANES):
                buf[c, pl.ds(p, per, stride=dil), :] = ref[p, :, c * LANES:(c + 1) * LANES]
        return jnp.concatenate([buf[c] for c in range(SLAB // LANES)], axis=1)

    os = [token_order(r) for r in o_refs]
    lses = [token_order(r) for r in l_refs]
    m = functools.reduce(jnp.maximum, lses)
    ws = [jnp.exp(l - m) for l in lses]
    den = functools.reduce(jnp.add, ws)
    acc = functools.reduce(jnp.add, [(w / den) * o for w, o in zip(ws, os)])
    out_ref[...] = acc.astype(out_ref.dtype)


def dil_combine(outs, lses, tm):
    b, _, t, _ = outs[0].shape
    n = b * t
    nb = t // tm
    spec = lambda a: pl.BlockSpec((None, a.shape[1], tm // a.shape[1], SLAB), lambda i: (i // nb, 0, i % nb, 0))
    n_folded = sum(1 for a in outs + lses if a.shape[1] > 1)
    return pl.pallas_call(
        _dil_combine_kernel,
        grid=(n // tm,),
        in_specs=[spec(a) for a in outs + lses],
        out_specs=pl.BlockSpec((tm, SLAB), lambda i: (i, 0)),
        out_shape=jax.ShapeDtypeStruct((n, SLAB), BF16),
        scratch_shapes=[pltpu.VMEM((SLAB // LANES, tm, LANES), F32)] * n_folded,
        compiler_params=_cparams(("parallel",)),
        name="dil_combine",
    )(*outs, *lses)


MLA_QK = MLA_NOPE + MLA_ROPE


def mla_tables(t):
    ang = _rope_angles(t, MLA_ROPE)
    cos2 = np.concatenate([np.cos(ang), np.cos(ang)], axis=1)
    sin2 = np.concatenate([np.sin(ang), np.sin(ang)], axis=1)
    z = lambda w: np.zeros((t, w), np.float32)
    q_cos = np.concatenate([np.ones((t, MLA_NOPE), np.float32), cos2, z(LANES - MLA_QK)], axis=1)
    q_sin = np.concatenate([z(MLA_NOPE), sin2, z(LANES - MLA_QK)], axis=1)
    k_cos = np.concatenate([cos2, z(LANES - MLA_ROPE)], axis=1)
    k_sin = np.concatenate([-sin2[:, :MLA_ROPE // 2], sin2[:, MLA_ROPE // 2:], z(LANES - MLA_ROPE)], axis=1)
    return tuple(jnp.asarray(a, F32) for a in (q_cos, q_sin, k_cos, k_sin))


def mla_weights(w_uq, w_ukv):
    hq = w_uq.reshape(MLA_Q_RANK, MLA_HEADS, MLA_QK)
    nope, pe = hq[..., :MLA_NOPE], hq[..., MLA_NOPE:]
    pe_rot = jnp.concatenate([-pe[..., MLA_ROPE // 2:], pe[..., :MLA_ROPE // 2]], axis=-1)
    zq = jnp.zeros((MLA_Q_RANK, MLA_HEADS, LANES - MLA_QK), w_uq.dtype)
    w1 = jnp.concatenate([nope, pe, zq], axis=-1).reshape(MLA_Q_RANK, MLA_HEADS * LANES)
    w2 = jnp.concatenate([jnp.zeros_like(nope), pe_rot, zq], axis=-1).reshape(MLA_Q_RANK, MLA_HEADS * LANES)
    hkv = w_ukv.reshape(MLA_KV_RANK, MLA_HEADS, MLA_NOPE + MLA_V)
    k_nope, v = hkv[..., :MLA_NOPE], hkv[..., MLA_NOPE:]
    zk = jnp.zeros((MLA_KV_RANK, MLA_HEADS, LANES - MLA_NOPE), w_ukv.dtype)
    wk = jnp.concatenate([k_nope, zk], axis=-1).reshape(MLA_KV_RANK, MLA_HEADS * LANES)
    zv = jnp.zeros((MLA_KV_RANK, MLA_HEADS, LANES - MLA_V), w_ukv.dtype)
    wv = jnp.concatenate([v, zv], axis=-1).reshape(MLA_KV_RANK, MLA_HEADS * LANES)
    place = np.zeros((LANES, MLA_HEADS * LANES), np.float32)
    ones = np.zeros((1, MLA_HEADS * LANES), np.float32)
    for h in range(MLA_HEADS):
        place[np.arange(MLA_ROPE), h * LANES + MLA_NOPE + np.arange(MLA_ROPE)] = 1.0
        ones[0, h * LANES + MLA_V] = 1.0
    return (w1.astype(BF16), w2.astype(BF16), wk.astype(BF16), wv.astype(BF16), jnp.asarray(place, BF16),
            jnp.asarray(ones, F32))


def _mla_q_kernel(c_ref, nw_ref, w1_ref, w2_ref, cos_ref, sin_ref, o_ref):
    cn = _rms(c_ref[...].astype(F32), nw_ref[...]).astype(BF16)
    cos = jnp.tile(cos_ref[...], (1, MLA_HEADS))
    sin = jnp.tile(sin_ref[...], (1, MLA_HEADS))
    q = _dot(cn, w1_ref[...]) * cos + _dot(cn, w2_ref[...]) * sin
    o_ref[...] = (q * (MLA_QK ** -0.5 * math.log2(math.e))).astype(o_ref.dtype)


def _mla_kv_kernel(c_ref, tail_ref, nw_ref, wk_ref, wv_ref, place_ref, ones_ref, cos_ref, sin_ref,
                   k_ref, v_ref):
    cn = _rms(c_ref[...].astype(F32), nw_ref[...]).astype(BF16)
    kr = tail_ref[...]
    half = MLA_ROPE // 2
    rot = jnp.where(_lane_lt(kr.shape, half), pltpu.roll(kr, LANES - half, 1), pltpu.roll(kr, half, 1))
    k_pe = (kr * cos_ref[...] + rot * sin_ref[...]).astype(BF16)
    k_ref[...] = (_dot(cn, wk_ref[...]) + _dot(k_pe, place_ref[...])).astype(k_ref.dtype)
    v_ref[...] = (_dot(cn, wv_ref[...]) + ones_ref[...]).astype(v_ref.dtype)


def mla_project(proj, tail, q_norm_w, kv_norm_w, w_uq, w_ukv, b, t, tm):
    n = b * t
    nb = t // tm
    w1, w2, wk, wv, place, ones = mla_weights(w_uq, w_ukv)
    q_cos, q_sin, k_cos, k_sin = mla_tables(t)
    wide = MLA_HEADS * LANES
    full = lambda shape: pl.BlockSpec(shape, lambda i: (0, 0))
    tab = pl.BlockSpec((tm, LANES), lambda i: (i % nb, 0))
    qf = pl.pallas_call(
        _mla_q_kernel,
        grid=(n // tm,),
        in_specs=[pl.BlockSpec((tm, SLAB), lambda i: (i, COL_CQ)), full((1, MLA_Q_RANK)),
                  full((MLA_Q_RANK, wide)), full((MLA_Q_RANK, wide)), tab, tab],
        out_specs=pl.BlockSpec((tm, wide), lambda i: (i, 0)),
        out_shape=jax.ShapeDtypeStruct((n, wide), BF16),
        compiler_params=_cparams(("parallel",)),
        name="mla_q_proj",
    )(proj, q_norm_w.reshape(1, MLA_Q_RANK), w1, w2, q_cos, q_sin)
    kf, vf = pl.pallas_call(
        _mla_kv_kernel,
        grid=(n // tm,),
        in_specs=[pl.BlockSpec((tm, SLAB), lambda i: (i, COL_CKV)),
                  pl.BlockSpec((tm, LANES), lambda i: (i, 0)), full((1, MLA_KV_RANK)),
                  full((MLA_KV_RANK, wide)), full((MLA_KV_RANK, wide)), full((LANES, wide)), full((1, wide)),
                  tab, tab],
        out_specs=[pl.BlockSpec((tm, wide), lambda i: (i, 0))] * 2,
        out_shape=[jax.ShapeDtypeStruct((n, wide), BF16)] * 2,
        compiler_params=_cparams(("parallel",)),
        name="mla_kv_proj",
    )(proj, tail, kv_norm_w.reshape(1, MLA_KV_RANK), wk, wv, place, ones, k_cos, k_sin)
    return qf, kf, vf


def _mla_attn_kernel(q_ref, k_ref, v_ref, o_ref, *, t, tk):
    tq = q_ref.shape[0]
    groups = [slice(hh * LANES, (hh + 1) * LANES) for hh in range(2)]
    qs = [q_ref[:, grp] for grp in groups]

    def scores(c):
        return [_dot_nt(q, k_ref[c * tk:(c + 1) * tk, grp]) for q, grp in zip(qs, groups)]

    n_chunks = t // tk
    ms = [jnp.full((tq, 1), -jnp.inf, F32)] * 2
    accs = [jnp.zeros((tq, LANES), F32)] * 2
    s_next = scores(0)
    for c in range(n_chunks):
        s_cur = s_next
        if c + 1 < n_chunks:
            s_next = scores(c + 1)
        for hh, grp in enumerate(groups):
            m_new = jnp.maximum(ms[hh], jnp.max(s_cur[hh], axis=-1, keepdims=True))
            p = jnp.exp2((s_cur[hh] - m_new).astype(BF16))
            accs[hh] = jnp.exp2(ms[hh] - m_new) * accs[hh] + _dot(p, v_ref[c * tk:(c + 1) * tk, grp])
            ms[hh] = m_new
    outs = [acc / acc[:, MLA_V:MLA_V + 1] for acc in accs]
    first = _lane_lt((tq, LANES), MLA_V)
    o_ref[...] = jnp.where(first, outs[0], pltpu.roll(outs[1], MLA_V, 1)).astype(o_ref.dtype)


def mla_attention(qf, kf, vf, b, t, tq, tk):
    n = b * t
    nq = t // tq
    return pl.pallas_call(
        functools.partial(_mla_attn_kernel, t=t, tk=tk),
        grid=(b, N_PAIRS, nq),
        in_specs=[
            pl.BlockSpec((tq, 2 * LANES), lambda i, p, s: (i * nq + s, p)),
            pl.BlockSpec((t, 2 * LANES), lambda i, p, s: (i, p)),
            pl.BlockSpec((t, 2 * LANES), lambda i, p, s: (i, p)),
        ],
        out_specs=pl.BlockSpec((tq, HEAD_PAIR), lambda i, p, s: (i * nq + s, p)),
        out_shape=jax.ShapeDtypeStruct((n, SLAB), BF16),
        compiler_params=_cparams(("parallel", "parallel", "arbitrary")),
        name="mla_attention",
    )(qf, kf, vf)


def _in_proj_segments():
    sizes = (SSM_INNER, SSM_CONV_CH, 2 * SSM_HEADS, SLAB, SLAB, SLAB, MLA_Q_RANK, MLA_KV_RANK, MLA_ROPE,
             SLAB, SLAB, SLAB)
    off = [int(v) for v in np.concatenate([[0], np.cumsum(sizes)])]
    main = ((off[0], off[2]), (off[3], off[8]), (off[9], off[12]))
    tail = ((off[8], off[9]), (off[2], off[3]))
    return main, tail


def _in_proj_columns():
    main, tail = _in_proj_segments()
    cols = lambda segs: np.concatenate([np.arange(a, b) for a, b in segs])
    return cols(main), cols(tail)


def in_proj_weights(w_in_l):
    main, tail = _in_proj_segments()
    w_main = jnp.concatenate([w_in_l[:, a:b] for a, b in main], axis=1).astype(BF16)
    pad = jnp.zeros((D_MODEL, LANES - sum(b - a for a, b in tail)), w_in_l.dtype)
    w_tail = jnp.concatenate([w_in_l[:, a:b] for a, b in tail] + [pad], axis=1).astype(BF16)
    return w_main, w_tail


def mixers(proj, tail, p, l, b, t):
    xbc = conv_silu(proj, p["conv_w"][l], p["conv_b"][l], b, t)
    y_f, y_b = ssd_scan(xbc, tail, p["dt_bias"][l], p["a_log"][l], b, t)
    y_ssm = ssd_combine(y_f, y_b, xbc, proj, p["d_skip"][l], p["ssm_norm_w"][l], 1024)

    y_na = na_attention(proj, p["na_rpb"][l], b, t)

    qf, kf, vf = mla_project(proj, tail, p["mla_q_norm_w"][l], p["mla_kv_norm_w"][l],
                             p["mla_w_uq"][l], p["mla_w_ukv"][l], b, t, 512)
    y_mla = mla_attention(qf, kf, vf, b, t, 512, 512)

    qkvs = rope_qkv(proj, b, t, 1024)
    outs, lses = zip(*[band_attention(qkv, w, d) for qkv, (w, d) in zip(qkvs, DIL_PAIRS)])
    y_dil = dil_combine(outs, lses, 1024)
    return y_ssm, y_na, y_mla, y_dil


def kernel(x, attn_norm_w, w_in, conv_w, conv_b, a_log, dt_bias, d_skip, ssm_norm_w, na_rpb,
           mla_q_norm_w, mla_kv_norm_w, mla_w_uq, mla_w_ukv, w_o, ffn_norm_w, ffn_w_gate, ffn_w_up,
           ffn_w_down, router_w, exp_w_gate, exp_w_up, exp_w_down, final_norm_w):
    b, t, _ = x.shape
    n = b * t
    depth = w_in.shape[0]
    p = dict(conv_w=conv_w, conv_b=conv_b, a_log=a_log, dt_bias=dt_bias, d_skip=d_skip,
             ssm_norm_w=ssm_norm_w, na_rpb=na_rpb, mla_q_norm_w=mla_q_norm_w,
             mla_kv_norm_w=mla_kv_norm_w, mla_w_uq=mla_w_uq, mla_w_ukv=mla_w_ukv)
    x = x.reshape(n, D_MODEL)
    cast_rows = 256
    w_o_b = cast_bf16(w_o, cast_rows)
    ffn_b = [cast_bf16(w, cast_rows) for w in (ffn_w_gate, ffn_w_up, ffn_w_down)]
    exp_b = [cast_bf16(w, cast_rows) for w in (exp_w_gate, exp_w_up, exp_w_down)]
    moe_tm = 512
    normed = False
    for l in range(depth):
        w_main, w_tail = in_proj_weights(w_in[l])
        proj, tail = in_proj(x, attn_norm_w[l], w_main, w_tail, 1024, PROJ_MAIN // 4)
        mix = mixers(proj, tail, p, l, b, t)
        x = out_proj(mix, w_o_b, l, x, 1024, 1024)
        j = l // 2
        if l % 2 == 0:
            x = ffn_dense(x, ffn_norm_w[l], *ffn_b, j, 512, 512)
        else:
            top_i, gates = moe_router(x, ffn_norm_w[l], router_w[j], 512)
            src, pos, tile_expert, tile_valid = moe_plan(top_i, moe_tm)
            y = moe_ffn(x, ffn_norm_w[l], *exp_b, j, src, tile_expert, tile_valid, moe_tm, 512)
            last = l == depth - 1
            x = moe_combine(x, gates, y, pos, final_norm_w if last else None, 256)
            normed = last
    if not normed:
        x = rmsnorm_rows(x, final_norm_w, 1024)
    return x.reshape(b, t, D_MODEL)
```

```python
import functools
import math

import numpy as np
import jax
import jax.numpy as jnp
from jax import lax
from jax.experimental import pallas as pl
from jax.experimental.pallas import tpu as pltpu

F32 = jnp.float32
BF16 = jnp.bfloat16

D_MODEL = 2048
GRID_W = 64
HEAD_DIM = 64
ROPE_THETA = 10000.0
NORM_EPS = 1e-6
NEG_INF = -1e30

SSM_HEADS = 8
SSM_HEAD_DIM = 64
SSM_INNER = SSM_HEADS * SSM_HEAD_DIM
SSM_GROUPS = 2
SSM_STATE = 128
SSM_CONV = 5
SSM_CHUNK = 128
SSM_CONV_CH = SSM_INNER + 2 * SSM_GROUPS * SSM_STATE

NA_HEADS = 8
NA_WIN_ROWS = 8
NA_WIN_COLS = 16
NA_COL_BLOCK = 16
NA_KEY_COLS = 32
NA_ROWS_PER_STEP = 4

MLA_HEADS = 8
MLA_Q_RANK = 512
MLA_KV_RANK = 512
MLA_NOPE = 64
MLA_ROPE = 32
MLA_V = 64

DIL_HEADS = 8
DIL_PAIRS = ((128, 1), (512, 4), (2048, 16))
DIL_QBLOCK = 128
BAND_BLOCKS_PER_STEP = 4

N_EXPERTS = 8
TOP_K = 2
SPLIT_PARTS = 3

LANES = 128
HEAD_PAIR = 2 * HEAD_DIM
N_PAIRS = 4
SLAB = 512

COL_Z, COL_XBC, COL_NAQ, COL_NAK, COL_NAV, COL_CQ, COL_CKV, COL_DLQ, COL_DLK, COL_DLV = (
    0, 1, 3, 4, 5, 6, 7, 8, 9, 10)
PROJ_MAIN = 11 * SLAB
TAIL_DT = 32

VMEM_LIMIT = 56 * 1024 * 1024


def _cparams(sem, vmem=VMEM_LIMIT):
    return pltpu.CompilerParams(dimension_semantics=sem, vmem_limit_bytes=vmem)


def _lane_lt(shape, bound, period=None):
    lane = lax.broadcasted_iota(jnp.int32, shape, len(shape) - 1)
    if period is not None:
        lane = lane % period
    return lane < bound


def _rms(x, w):
    ms = jnp.mean(x * x, axis=-1, keepdims=True)
    return x * lax.rsqrt(ms + NORM_EPS) * w


def _dot(a, b):
    return jnp.dot(a, b, preferred_element_type=F32)


def _dot_nt(a, b):
    return lax.dot_general(a, b, (((1,), (1,)), ((), ())), preferred_element_type=F32)


def _dot_tn(a, b):
    return lax.dot_general(a, b, (((0,), (0,)), ((), ())), preferred_element_type=F32)


def _cast_kernel(x_ref, o_ref):
    o_ref[...] = x_ref[...].astype(o_ref.dtype)


def cast_bf16(w, tr):
    shape = w.shape
    w2 = w.reshape(-1, shape[-1])
    r, c = w2.shape
    out = pl.pallas_call(
        _cast_kernel,
        grid=(r // tr,),
        in_specs=[pl.BlockSpec((tr, c), lambda i: (i, 0))],
        out_specs=pl.BlockSpec((tr, c), lambda i: (i, 0)),
        out_shape=jax.ShapeDtypeStruct((r, c), BF16),
        compiler_params=_cparams(("parallel",)),
        name="cast_bf16",
    )(w2)
    return out.reshape(shape)


def _in_proj_kernel(x_ref, nw_ref, w_ref, wt_ref, o_ref, t_ref, h_ref):
    @pl.when(pl.program_id(1) == 0)
    def _():
        h = _rms(x_ref[...], nw_ref[...]).astype(BF16)
        h_ref[...] = h
        t_ref[...] = _dot(h, wt_ref[...])

    o_ref[...] = _dot(h_ref[...], w_ref[...]).astype(o_ref.dtype)


def in_proj(x, nw, w_main, w_tail, tm, tn):
    n, k = x.shape
    nout = w_main.shape[1]
    return pl.pallas_call(
        _in_proj_kernel,
        grid=(n // tm, nout // tn),
        in_specs=[
            pl.BlockSpec((tm, k), lambda i, j: (i, 0)),
            pl.BlockSpec((1, k), lambda i, j: (0, 0)),
            pl.BlockSpec((k, tn), lambda i, j: (0, j)),
            pl.BlockSpec((k, LANES), lambda i, j: (0, 0)),
        ],
        out_specs=[pl.BlockSpec((tm, tn), lambda i, j: (i, j)), pl.BlockSpec((tm, LANES), lambda i, j: (i, 0))],
        out_shape=[jax.ShapeDtypeStruct((n, nout), BF16), jax.ShapeDtypeStruct((n, LANES), F32)],
        scratch_shapes=[pltpu.VMEM((tm, k), BF16)],
        compiler_params=_cparams(("parallel", "arbitrary")),
        name="in_proj",
    )(x, nw.reshape(1, k), w_main, w_tail)


def _out_proj_kernel(a0_ref, a1_ref, a2_ref, a3_ref, w_ref, r_ref, o_ref):
    acc = r_ref[...]
    for s, a_ref in enumerate((a0_ref, a1_ref, a2_ref, a3_ref)):
        acc = acc + _dot(a_ref[...], w_ref[s * SLAB:(s + 1) * SLAB, :])
    o_ref[...] = acc


def out_proj(mix, w, layer, res, tm, tn):
    n = res.shape[0]
    return pl.pallas_call(
        _out_proj_kernel,
        grid=(n // tm, D_MODEL // tn),
        in_specs=[pl.BlockSpec((tm, SLAB), lambda i, j: (i, 0))] * 4 + [
            pl.BlockSpec((None, 4 * SLAB, tn), lambda i, j: (layer, 0, j)),
            pl.BlockSpec((tm, tn), lambda i, j: (i, j)),
        ],
        out_specs=pl.BlockSpec((tm, tn), lambda i, j: (i, j)),
        out_shape=jax.ShapeDtypeStruct((n, D_MODEL), F32),
        compiler_params=_cparams(("parallel", "arbitrary")),
        name="out_proj",
    )(*mix, w, res)


def _ffn_kernel(x_ref, nw_ref, wg_ref, wu_ref, wd_ref, o_ref, h_ref):
    @pl.when(pl.program_id(1) == 0)
    def _():
        x = x_ref[...]
        h_ref[...] = _rms(x, nw_ref[...]).astype(BF16)
        o_ref[...] = x

    _swiglu_rows(h_ref, wg_ref, wu_ref, wd_ref, o_ref)


FFN_ROW_CHUNK = 512


def _swiglu_rows(h_ref, wg_ref, wu_ref, wd_ref, o_ref):
    tm = h_ref.shape[0]
    for r0 in range(0, tm, FFN_ROW_CHUNK):
        rows = slice(r0, r0 + FFN_ROW_CHUNK)
        h = h_ref[rows, :]
        g = _dot(h, wg_ref[...])
        u = _dot(h, wu_ref[...])
        a = (g * jax.nn.sigmoid(g) * u).astype(BF16)
        o_ref[rows, :] += _dot(a, wd_ref[...])


def ffn_dense(x, nw, wg, wu, wd, layer, tm, tf):
    n = x.shape[0]
    d_ff = wg.shape[-1]
    return pl.pallas_call(
        _ffn_kernel,
        grid=(n // tm, d_ff // tf),
        in_specs=[
            pl.BlockSpec((tm, D_MODEL), lambda i, j: (i, 0)),
            pl.BlockSpec((1, D_MODEL), lambda i, j: (0, 0)),
            pl.BlockSpec((None, D_MODEL, tf), lambda i, j: (layer, 0, j)),
            pl.BlockSpec((None, D_MODEL, tf), lambda i, j: (layer, 0, j)),
            pl.BlockSpec((None, tf, D_MODEL), lambda i, j: (layer, j, 0)),
        ],
        out_specs=pl.BlockSpec((tm, D_MODEL), lambda i, j: (i, 0)),
        out_shape=jax.ShapeDtypeStruct((n, D_MODEL), F32),
        scratch_shapes=[pltpu.VMEM((tm, D_MODEL), BF16)],
        compiler_params=_cparams(("parallel", "arbitrary")),
        name="ffn_dense",
    )(x, nw.reshape(1, D_MODEL), wg, wu, wd)


def _router_kernel(x_ref, nw_ref, rw_ref, idx_ref, gate_ref):
    h = _rms(x_ref[...], nw_ref[...])
    acc = jnp.zeros((h.shape[0], LANES), F32)
    rem = h
    for _ in range(SPLIT_PARTS):
        part = rem.astype(BF16)
        acc = acc + _dot(part, rw_ref[...])
        rem = rem - part.astype(F32)
    logits = acc
    for k in range(1, SPLIT_PARTS):
        logits = logits + pltpu.roll(acc, LANES - k * N_EXPERTS, 1)
    lane = lax.broadcasted_iota(jnp.int32, logits.shape, 1)
    logits = jnp.where(lane < N_EXPERTS, logits, -jnp.inf)
    m1 = jnp.max(logits, axis=-1, keepdims=True)
    i1 = jnp.min(jnp.where(logits == m1, lane, LANES), axis=-1, keepdims=True)
    rest = jnp.where(lane == i1, -jnp.inf, logits)
    m2 = jnp.max(rest, axis=-1, keepdims=True)
    i2 = jnp.min(jnp.where(rest == m2, lane, LANES), axis=-1, keepdims=True)
    e2 = jnp.exp(m2 - m1)
    g1 = 1.0 / (1.0 + e2)
    g2 = e2 / (1.0 + e2)
    idx_ref[...] = jnp.where(lane == 0, i1, i2)[:, :TOP_K]
    gate_ref[...] = jnp.where(lane == 0, g1, g2)[:, :TOP_K]


def moe_router(x, nw, router_w, tm):
    n = x.shape[0]
    parts, rem = [], router_w
    for _ in range(SPLIT_PARTS):
        parts.append(rem.astype(BF16))
        rem = rem - parts[-1].astype(F32)
    pad = jnp.zeros((D_MODEL, LANES - SPLIT_PARTS * N_EXPERTS), BF16)
    rw = jnp.concatenate(parts + [pad], axis=1)
    return pl.pallas_call(
        _router_kernel,
        grid=(n // tm,),
        in_specs=[
            pl.BlockSpec((tm, D_MODEL), lambda i: (i, 0)),
            pl.BlockSpec((1, D_MODEL), lambda i: (0, 0)),
            pl.BlockSpec((D_MODEL, LANES), lambda i: (0, 0)),
        ],
        out_specs=[pl.BlockSpec((tm, TOP_K), lambda i: (i, 0)),
                   pl.BlockSpec((tm, TOP_K), lambda i: (i, 0))],
        out_shape=[jax.ShapeDtypeStruct((n, TOP_K), jnp.int32),
                   jax.ShapeDtypeStruct((n, TOP_K), F32)],
        compiler_params=_cparams(("parallel",)),
        name="moe_router",
    )(x, nw.reshape(1, D_MODEL), rw)


def moe_plan(top_i, tm):
    n = top_i.shape[0]
    flat_e = top_i.reshape(-1)
    onehot = (flat_e[:, None] == jnp.arange(N_EXPERTS, dtype=jnp.int32)[None, :]).astype(jnp.int32)
    csum = jnp.cumsum(onehot, axis=0)
    counts = csum[-1]
    rank = jnp.sum(onehot * csum, axis=1) - 1
    padded = ((counts + tm - 1) // tm) * tm
    pend = jnp.cumsum(padded)
    pstart = pend - padded
    pos = pstart[flat_e] + rank
    n_slots = n * TOP_K + N_EXPERTS * tm
    n_tiles = n_slots // tm
    src = jnp.zeros((n_slots,), jnp.int32).at[pos].set(jnp.arange(n * TOP_K, dtype=jnp.int32) // TOP_K)
    tile_start = jnp.arange(n_tiles, dtype=jnp.int32) * tm
    tile_expert = jnp.sum((tile_start[:, None] >= pend[None, :]).astype(jnp.int32), axis=1)
    tile_valid = (tile_start < pend[-1]).astype(jnp.int32)
    last_valid = jnp.maximum(pend[-1] // tm - 1, 0)
    tile_expert = jnp.where(tile_valid == 1, tile_expert, tile_expert[last_valid]).astype(jnp.int32)
    return src, pos.reshape(n, TOP_K).astype(jnp.int32), tile_expert, tile_valid


def _moe_ffn_kernel(te_ref, tv_ref, src_ref, nsrc_ref, x_hbm, nw_ref, wg_ref, wu_ref, wd_ref, y_ref,
                    xbuf, h_ref, sem, *, tm):
    i = pl.program_id(0)
    j = pl.program_id(1)
    n_tiles = pl.num_programs(0)
    valid = tv_ref[i] == 1

    def row_copy(idx_ref, r):
        tok = idx_ref[0, 0, r]
        return pltpu.make_async_copy(x_hbm.at[pl.ds(tok, 1)], xbuf.at[pl.ds(r, 1)], sem)

    def gather(idx_ref):
        def start(r, c):
            row_copy(idx_ref, r).start()
            return c

        lax.fori_loop(0, tm, start, 0, unroll=8)

    @pl.when(j == 0)
    def _():
        y_ref[...] = jnp.zeros_like(y_ref)

    @pl.when(jnp.logical_and(j == 0, valid))
    def _():
        @pl.when(i == 0)
        def _():
            gather(src_ref)

        pltpu.make_async_copy(x_hbm.at[pl.ds(0, tm)], xbuf, sem).wait()
        h_ref[...] = _rms(xbuf[...], nw_ref[...]).astype(BF16)

        nxt = jnp.minimum(i + 1, n_tiles - 1)

        @pl.when(jnp.logical_and(i + 1 < n_tiles, tv_ref[nxt] == 1))
        def _():
            gather(nsrc_ref)

    @pl.when(valid)
    def _():
        _swiglu_rows(h_ref, wg_ref, wu_ref, wd_ref, y_ref)


def moe_ffn(x, nw, wg, wu, wd, layer, src, tile_expert, tile_valid, tm, tf):
    n_slots = src.shape[0]
    n_tiles = n_slots // tm
    d_ff = wg.shape[-1]
    nf = d_ff // tf

    def wcol(i, j, te_ref, tv_ref):
        return (layer, te_ref[i], 0, jnp.where(tv_ref[i] == 1, j, nf - 1))

    def wrow(i, j, te_ref, tv_ref):
        return (layer, te_ref[i], jnp.where(tv_ref[i] == 1, j, nf - 1), 0)

    grid_spec = pltpu.PrefetchScalarGridSpec(
        num_scalar_prefetch=2,
        grid=(n_tiles, nf),
        in_specs=[
            pl.BlockSpec((1, 1, tm), lambda i, j, *_: (i, 0, 0), memory_space=pltpu.SMEM),
            pl.BlockSpec((1, 1, tm), lambda i, j, *_: (jnp.minimum(i + 1, n_tiles - 1), 0, 0),
                         memory_space=pltpu.SMEM),
            pl.BlockSpec(memory_space=pl.ANY),
            pl.BlockSpec((1, D_MODEL), lambda i, j, *_: (0, 0)),
            pl.BlockSpec((None, None, D_MODEL, tf), wcol),
            pl.BlockSpec((None, None, D_MODEL, tf), wcol),
            pl.BlockSpec((None, None, tf, D_MODEL), wrow),
        ],
        out_specs=pl.BlockSpec((tm, D_MODEL), lambda i, j, *_: (i, 0)),
        scratch_shapes=[pltpu.VMEM((tm, D_MODEL), F32), pltpu.VMEM((tm, D_MODEL), BF16),
                        pltpu.SemaphoreType.DMA],
    )
    src3 = src.reshape(n_tiles, 1, tm)
    return pl.pallas_call(
        functools.partial(_moe_ffn_kernel, tm=tm),
        grid_spec=grid_spec,
        out_shape=jax.ShapeDtypeStruct((n_slots, D_MODEL), F32),
        compiler_params=_cparams(("arbitrary", "arbitrary")),
        name="moe_ffn",
    )(tile_expert, tile_valid, src3, src3, x, nw.reshape(1, D_MODEL), wg, wu, wd)


def _moe_combine_kernel(pos_ref, npos_ref, x_ref, gate_ref, y_hbm, fw_ref, o_ref, ybuf, sem, *, tm, final_norm):
    i = pl.program_id(0)
    n_tiles = pl.num_programs(0)
    cur = i % 2

    def row_copy(idx_ref, buf, r, k):
        slot = idx_ref[0, 0, r * TOP_K + k]
        return pltpu.make_async_copy(y_hbm.at[pl.ds(slot, 1)], ybuf.at[buf, k, pl.ds(r, 1)], sem.at[buf])

    def gather(idx_ref, buf):
        def start(r, c):
            for k in range(TOP_K):
                row_copy(idx_ref, buf, r, k).start()
            return c

        lax.fori_loop(0, tm, start, 0, unroll=8)

    @pl.when(i == 0)
    def _():
        gather(pos_ref, 0)

    @pl.when(i + 1 < n_tiles)
    def _():
        gather(npos_ref, 1 - cur)

    for k in range(TOP_K):
        pltpu.make_async_copy(y_hbm.at[pl.ds(0, tm)], ybuf.at[cur, k], sem.at[cur]).wait()
    gates = gate_ref[...]
    out = x_ref[...]
    for k in range(TOP_K):
        out = out + gates[:, k:k + 1] * ybuf[cur, k]
    if final_norm:
        out = _rms(out, fw_ref[...])
    o_ref[...] = out


def moe_combine(x, gates, y, pos, final_w, tm):
    n = x.shape[0]
    final_norm = final_w is not None
    fw = final_w if final_norm else jnp.ones((D_MODEL,), F32)
    n_tiles = n // tm
    pos3 = pos.reshape(n_tiles, 1, tm * TOP_K)
    return pl.pallas_call(
        functools.partial(_moe_combine_kernel, tm=tm, final_norm=final_norm),
        grid=(n_tiles,),
        in_specs=[
            pl.BlockSpec((1, 1, tm * TOP_K), lambda i: (i, 0, 0), memory_space=pltpu.SMEM),
            pl.BlockSpec((1, 1, tm * TOP_K), lambda i: (jnp.minimum(i + 1, n_tiles - 1), 0, 0),
                         memory_space=pltpu.SMEM),
            pl.BlockSpec((tm, D_MODEL), lambda i: (i, 0)),
            pl.BlockSpec((tm, TOP_K), lambda i: (i, 0)),
            pl.BlockSpec(memory_space=pl.ANY),
            pl.BlockSpec((1, D_MODEL), lambda i: (0, 0)),
        ],
        out_specs=pl.BlockSpec((tm, D_MODEL), lambda i: (i, 0)),
        out_shape=jax.ShapeDtypeStruct((n, D_MODEL), F32),
        scratch_shapes=[pltpu.VMEM((2, TOP_K, tm, D_MODEL), F32), pltpu.SemaphoreType.DMA((2,))],
        compiler_params=_cparams(("arbitrary",)),
        name="moe_combine",
    )(pos3, pos3, x, gates, y, fw.reshape(1, D_MODEL))


def _rmsnorm_kernel(x_ref, w_ref, o_ref):
    o_ref[...] = _rms(x_ref[...], w_ref[...])


def rmsnorm_rows(x, w, tm):
    n = x.shape[0]
    return pl.pallas_call(
        _rmsnorm_kernel,
        grid=(n // tm,),
        in_specs=[pl.BlockSpec((tm, D_MODEL), lambda i: (i, 0)),
                  pl.BlockSpec((1, D_MODEL), lambda i: (0, 0))],
        out_specs=pl.BlockSpec((tm, D_MODEL), lambda i: (i, 0)),
        out_shape=jax.ShapeDtypeStruct((n, D_MODEL), F32),
        compiler_params=_cparams(("parallel",)),
        name="final_norm",
    )(x, w.reshape(1, D_MODEL))


CONV_PAD = 8


def _conv_kernel(x_ref, w_ref, b_ref, o_ref, pad_ref, *, t):
    half = SSM_CONV // 2
    zeros = jnp.zeros((CONV_PAD, pad_ref.shape[1]), F32)
    pad_ref[0:CONV_PAD, :] = zeros
    pad_ref[CONV_PAD + t:CONV_PAD + t + CONV_PAD, :] = zeros
    pad_ref[CONV_PAD:CONV_PAD + t, :] = x_ref[...].astype(F32)
    acc = jnp.zeros(o_ref.shape, F32) + b_ref[...]
    for k in range(SSM_CONV):
        acc = acc + pad_ref[pl.ds(CONV_PAD - half + k, t), :] * w_ref[k:k + 1, :]
    o_ref[...] = (acc * jax.nn.sigmoid(acc)).astype(o_ref.dtype)


def conv_silu(proj, conv_w, conv_b, b, t):
    tc = 256
    nblk = SSM_CONV_CH // tc
    col0 = COL_XBC * SLAB // tc
    return pl.pallas_call(
        functools.partial(_conv_kernel, t=t),
        grid=(b, nblk),
        in_specs=[
            pl.BlockSpec((t, tc), lambda i, j: (i, col0 + j)),
            pl.BlockSpec((SSM_CONV, tc), lambda i, j: (0, j)),
            pl.BlockSpec((1, tc), lambda i, j: (0, j)),
        ],
        out_specs=pl.BlockSpec((t, tc), lambda i, j: (i, j)),
        out_shape=jax.ShapeDtypeStruct((b * t, SSM_CONV_CH), BF16),
        scratch_shapes=[pltpu.VMEM((t + 2 * CONV_PAD, tc), F32)],
        compiler_params=_cparams(("parallel", "parallel")),
        name="conv_silu",
    )(proj, conv_w, conv_b.reshape(1, SSM_CONV_CH))


def _ssd_stage1(xbc_ref, tail_ref, bias_ref, alog_ref, tri_ref, sel_ref, state_ref, direction):
    q = SSM_CHUNK
    dt = jax.nn.softplus(tail_ref[...] + bias_ref[...])
    da = dt * (-jnp.exp(alog_ref[...]))
    cs = jnp.dot(tri_ref[...], da, preferred_element_type=F32, precision=lax.Precision.HIGHEST)
    total = cs[q - 1:q, :]
    if direction == 0:
        e_out = cs
        e_in = total - cs
        e_seg = cs
    else:
        ex = cs - da
        e_out = total - ex
        e_in = ex
        e_seg = -ex
    dec_out_b = jnp.exp(e_out).astype(BF16)
    dec_in_dt_b = (jnp.exp(e_in) * dt).astype(BF16)
    pairs_per_group = N_PAIRS // SSM_GROUPS
    cbs, y_offs, in_scales = [], [], []
    for g in range(SSM_GROUPS):
        bm = xbc_ref[:, SSM_INNER + g * SSM_STATE:SSM_INNER + (g + 1) * SSM_STATE]
        cm = xbc_ref[:, SSM_INNER + (SSM_GROUPS + g) * SSM_STATE:SSM_INNER + (SSM_GROUPS + g + 1) * SSM_STATE]
        cbs.append(_dot_nt(cm, bm))
        for pair in range(g * pairs_per_group, (g + 1) * pairs_per_group):
            sel = sel_ref[direction, pair]
            y_offs.append(_dot(dec_out_b, sel) * _dot(cm, state_ref[pair].astype(BF16)))
            in_scales.append(_dot(dec_in_dt_b, sel))
    return dict(e_seg=e_seg, e_seg_t=jnp.transpose(e_seg), dt_t=jnp.transpose(dt), dec_tot=jnp.exp(total),
                cbs=cbs, y_offs=y_offs, in_scales=in_scales)


def _ssd_stage2(ctx, xbc_ref, y_ref, state_ref, direction):
    q = SSM_CHUNK
    row = lax.broadcasted_iota(jnp.int32, (q, q), 0)
    col = lax.broadcasted_iota(jnp.int32, (q, q), 1)
    keep = (row >= col) if direction == 0 else (col >= row)
    first_half = _lane_lt((q, HEAD_PAIR), HEAD_DIM)
    pairs_per_group = N_PAIRS // SSM_GROUPS
    for pair in range(N_PAIRS):
        g = pair // pairs_per_group
        bm = xbc_ref[:, SSM_INNER + g * SSM_STATE:SSM_INNER + (g + 1) * SSM_STATE]
        xs_pair = xbc_ref[:, pair * HEAD_PAIR:(pair + 1) * HEAD_PAIR]
        ys, decs = [], []
        for hh in range(2):
            lane = TAIL_DT + direction * SSM_HEADS + pair * 2 + hh
            seg = ctx["e_seg"][:, lane:lane + 1] - ctx["e_seg_t"][lane:lane + 1, :]
            lmat = jnp.where(keep, jnp.exp(seg), 0.0)
            w = (ctx["cbs"][g] * lmat * ctx["dt_t"][lane:lane + 1, :]).astype(BF16)
            ys.append(_dot(w, xs_pair))
            decs.append(ctx["dec_tot"][:, lane:lane + 1])
        y_ref[:, pair * HEAD_PAIR:(pair + 1) * HEAD_PAIR] = jnp.where(first_half, ys[0], ys[1]) + ctx["y_offs"][pair]
        st = state_ref[pair]
        first_half_s = _lane_lt(st.shape, HEAD_DIM)
        state_ref[pair] = (st * jnp.where(first_half_s, decs[0], decs[1])
                           + _dot_tn(bm, (xs_pair * ctx["in_scales"][pair]).astype(BF16)))


def _ssd_kernel(xbc_f_ref, tail_f_ref, xbc_b_ref, tail_b_ref, bias_ref, alog_ref, tri_ref, sel_ref,
                y_f_ref, y_b_ref, state_ref):
    @pl.when(pl.program_id(1) == 0)
    def _():
        state_ref[...] = jnp.zeros_like(state_ref)

    ins = ((xbc_f_ref, tail_f_ref, y_f_ref), (xbc_b_ref, tail_b_ref, y_b_ref))
    ctxs = [_ssd_stage1(xbc_ref, tail_ref, bias_ref, alog_ref, tri_ref, sel_ref, state_ref.at[d], d)
            for d, (xbc_ref, tail_ref, _) in enumerate(ins)]
    for d, (xbc_ref, _, y_ref) in enumerate(ins):
        _ssd_stage2(ctxs[d], xbc_ref, y_ref, state_ref.at[d], d)


def _ssd_lane_selectors():
    sel = np.zeros((2, N_PAIRS, LANES, HEAD_PAIR), np.float32)
    for d in range(2):
        for pair in range(N_PAIRS):
            for hh in range(2):
                sel[d, pair, TAIL_DT + d * SSM_HEADS + pair * 2 + hh, hh * HEAD_DIM:(hh + 1) * HEAD_DIM] = 1.0
    return jnp.asarray(sel, BF16)


def ssd_scan(xbc, tail, dt_bias, a_log, b, t):
    q = SSM_CHUNK
    nc = t // q
    bias_row = jnp.zeros((1, LANES), F32).at[0, TAIL_DT:TAIL_DT + 2 * SSM_HEADS].set(dt_bias.reshape(-1))
    alog_row = jnp.zeros((1, LANES), F32).at[0, TAIL_DT:TAIL_DT + 2 * SSM_HEADS].set(a_log.reshape(-1))
    tri = jnp.asarray(np.tril(np.ones((q, q), np.float32)))
    fwd = lambda i, c: (i * nc + c, 0)
    bwd = lambda i, c: (i * nc + nc - 1 - c, 0)
    const = lambda i, c: (0, 0)
    return pl.pallas_call(
        _ssd_kernel,
        grid=(b, nc),
        in_specs=[
            pl.BlockSpec((q, SSM_CONV_CH), fwd), pl.BlockSpec((q, LANES), fwd),
            pl.BlockSpec((q, SSM_CONV_CH), bwd), pl.BlockSpec((q, LANES), bwd),
            pl.BlockSpec((1, LANES), const), pl.BlockSpec((1, LANES), const), pl.BlockSpec((q, q), const),
            pl.BlockSpec((2, N_PAIRS, LANES, HEAD_PAIR), lambda i, c: (0, 0, 0, 0)),
        ],
        out_specs=[pl.BlockSpec((q, SSM_INNER), fwd), pl.BlockSpec((q, SSM_INNER), bwd)],
        out_shape=[jax.ShapeDtypeStruct((b * t, SSM_INNER), F32)] * 2,
        scratch_shapes=[pltpu.VMEM((2, N_PAIRS, SSM_STATE, HEAD_PAIR), F32)],
        compiler_params=_cparams(("parallel", "arbitrary")),
        name="ssd_scan",
    )(xbc, tail, xbc, tail, bias_row, alog_row, tri, _ssd_lane_selectors())


def _ssd_combine_kernel(yf_ref, yb_ref, xs_ref, z_ref, d_ref, nw_ref, o_ref):
    y = yf_ref[...] + yb_ref[...] + xs_ref[...].astype(F32) * d_ref[...]
    z = z_ref[...].astype(F32)
    o_ref[...] = _rms(y * (z * jax.nn.sigmoid(z)), nw_ref[...]).astype(o_ref.dtype)


def ssd_combine(y_f, y_b, xbc, proj, d_skip, norm_w, tm):
    n = y_f.shape[0]
    d_row = jnp.repeat(d_skip, SSM_HEAD_DIM).reshape(1, SSM_INNER)
    row = lambda i: (i, 0)
    return pl.pallas_call(
        _ssd_combine_kernel,
        grid=(n // tm,),
        in_specs=[
            pl.BlockSpec((tm, SSM_INNER), row),
            pl.BlockSpec((tm, SSM_INNER), row),
            pl.BlockSpec((tm, SSM_INNER), row),
            pl.BlockSpec((tm, SLAB), lambda i: (i, COL_Z)),
            pl.BlockSpec((1, SSM_INNER), lambda i: (0, 0)),
            pl.BlockSpec((1, SSM_INNER), lambda i: (0, 0)),
        ],
        out_specs=pl.BlockSpec((tm, SSM_INNER), row),
        out_shape=jax.ShapeDtypeStruct((n, SSM_INNER), BF16),
        compiler_params=_cparams(("parallel",)),
        name="ssd_combine",
    )(y_f, y_b, xbc, proj, d_row, norm_w.reshape(1, SSM_INNER))


def _pair_scores(q2, k2):
    first_q = _lane_lt(q2.shape, HEAD_DIM)
    zero = jnp.zeros_like(q2)
    qs = jnp.concatenate([jnp.where(first_q, q2, zero), jnp.where(first_q, zero, q2)], axis=0)
    return _dot_nt(qs, k2)


def _pair_attend(s, v2):
    tq = s.shape[0] // 2
    m = jnp.max(s, axis=-1, keepdims=True)
    p = jnp.exp(s - m)
    l = jnp.sum(p, axis=-1, keepdims=True)
    o = _dot(p.astype(BF16), v2) / l
    lse = m + jnp.log(l)
    first_o = _lane_lt((tq, HEAD_PAIR), HEAD_DIM)
    return jnp.where(first_o, o[:tq], o[tq:]), jnp.where(first_o, lse[:tq], lse[tq:])


def na_bias_tables(rpb, rows):
    kr = min(NA_WIN_ROWS, rows)
    qc = np.arange(GRID_W)
    kc = np.arange(GRID_W)
    q_start = np.clip(qc - NA_WIN_COLS // 2, 0, GRID_W - NA_WIN_COLS)
    col_in = (kc[None, :] >= q_start[:, None]) & (kc[None, :] < q_start[:, None] + NA_WIN_COLS)
    col_off = np.clip(kc[None, :] - qc[:, None] + NA_WIN_COLS - 1, 0, 2 * NA_WIN_COLS - 2)
    onehot = (col_off[None] == np.arange(2 * NA_WIN_COLS - 1)[:, None, None]).astype(np.float32)
    expanded = jnp.einsum("hrc,cqk->hqrk", rpb, jnp.asarray(onehot), precision=lax.Precision.HIGHEST)
    expanded = jnp.where(jnp.asarray(col_in)[None, :, None, :], expanded, NEG_INF)

    def table(r):
        row_start = int(np.clip(r - kr // 2, 0, rows - kr))
        ro0 = row_start - r + NA_WIN_ROWS - 1
        return expanded[:, :, ro0:ro0 + kr, :].reshape(N_PAIRS, 2 * GRID_W, kr * GRID_W)

    rs = NA_ROWS_PER_STEP
    lo = [table(r) for r in range(rs)]
    mid = [table(min(rs, rows - 1))] * rs
    hi = [table(r) for r in range(rows - rs, rows)]
    return jnp.stack([jnp.stack(lo), jnp.stack(mid), jnp.stack(hi)])


def _na_kernel(q_ref, k_ref, v_ref, bias_ref, o_ref, *, rows, kr):
    step = pl.program_id(1)
    rs = NA_ROWS_PER_STEP
    for rr in range(rs):
        r = step * rs + rr
        row_start = jnp.clip(r - kr // 2, 0, rows - kr)
        k0 = pl.multiple_of(row_start * GRID_W, GRID_W)
        pair_cols = [slice(pair * HEAD_PAIR, (pair + 1) * HEAD_PAIR) for pair in range(N_PAIRS)]
        scores = []
        for pair, cols in enumerate(pair_cols):
            q2 = q_ref[rr * GRID_W:(rr + 1) * GRID_W, cols] * jnp.asarray(HEAD_DIM ** -0.5, BF16)
            scores.append(_pair_scores(q2, k_ref[pl.ds(k0, kr * GRID_W), cols]) + bias_ref[0, rr, pair])
        for s, cols in zip(scores, pair_cols):
            o, _ = _pair_attend(s, v_ref[pl.ds(k0, kr * GRID_W), cols])
            o_ref[rr * GRID_W:(rr + 1) * GRID_W, cols] = o.astype(o_ref.dtype)


def na_attention(proj, rpb, b, t):
    rows = t // GRID_W
    kr = min(NA_WIN_ROWS, rows)
    rs = NA_ROWS_PER_STEP
    nsteps = rows // rs
    bias = na_bias_tables(rpb, rows)

    def kind(i, s):
        return jnp.where(s == 0, 0, jnp.where(s == nsteps - 1, 2, 1))

    return pl.pallas_call(
        functools.partial(_na_kernel, rows=rows, kr=kr),
        grid=(b, nsteps),
        in_specs=[
            pl.BlockSpec((rs * GRID_W, SLAB), lambda i, s: (i * nsteps + s, COL_NAQ)),
            pl.BlockSpec((t, SLAB), lambda i, s: (i, COL_NAK)),
            pl.BlockSpec((t, SLAB), lambda i, s: (i, COL_NAV)),
            pl.BlockSpec((1, rs, N_PAIRS, 2 * GRID_W, kr * GRID_W), lambda i, s: (kind(i, s), 0, 0, 0, 0)),
        ],
        out_specs=pl.BlockSpec((rs * GRID_W, SLAB), lambda i, s: (i * nsteps + s, 0)),
        out_shape=jax.ShapeDtypeStruct((b * t, SLAB), BF16),
        compiler_params=_cparams(("parallel", "arbitrary")),
        name="na_attention",
    )(proj, proj, proj, bias)


def _rope_angles(t, d):
    inv = ROPE_THETA ** (-np.arange(0, d, 2, dtype=np.float32) / d)
    return np.arange(t, dtype=np.float32)[:, None] * inv[None, :]


def rope_tables_pair(t):
    ang = _rope_angles(t, HEAD_DIM)
    cos = np.tile(np.cos(ang), (1, 4))
    sin = np.tile(np.concatenate([-np.sin(ang), np.sin(ang)], axis=1), (1, 2))
    return jnp.asarray(cos, F32), jnp.asarray(sin, F32)


FOLD_CHUNK = 256
FOLD_DILS = tuple(d for _, d in DIL_PAIRS if d > 1)


def fold_permutation(dil):
    per = FOLD_CHUNK // dil
    perm = np.zeros((FOLD_CHUNK, FOLD_CHUNK), np.float32)
    dst = np.arange(FOLD_CHUNK)
    perm[dst, (dst % per) * dil + dst // per] = 1.0
    return jnp.asarray(perm, BF16)


def _rope_qkv_kernel(x_ref, v_ref, cos_ref, sin_ref, *rest):
    nd = len(FOLD_DILS)
    perm_refs, o_ref, fold_refs = rest[:nd], rest[nd], rest[nd + 1:]
    cos = cos_ref[...]
    sin = sin_ref[...]
    half = HEAD_DIM // 2
    for c in range(x_ref.shape[1] // LANES):
        x = x_ref[:, c * LANES:(c + 1) * LANES].astype(F32)
        rot = jnp.where(_lane_lt(x.shape, half, HEAD_DIM),
                        pltpu.roll(x, LANES - half, 1), pltpu.roll(x, half, 1))
        y = x * cos + rot * sin
        if c < N_PAIRS:
            y = y * (HEAD_DIM ** -0.5)
        o_ref[0, :, c * LANES:(c + 1) * LANES] = y.astype(o_ref.dtype)
    o_ref[0, :, 2 * SLAB:3 * SLAB] = v_ref[...]
    tm = x_ref.shape[0]
    for dil, perm_ref, f_ref in zip(FOLD_DILS, perm_refs, fold_refs):
        per = FOLD_CHUNK // dil
        for c in range(tm // FOLD_CHUNK):
            folded = _dot(perm_ref[...], o_ref[0, c * FOLD_CHUNK:(c + 1) * FOLD_CHUNK, :]).astype(f_ref.dtype)
            for p in range(dil):
                f_ref[p, c * per:(c + 1) * per, :] = folded[p * per:(p + 1) * per, :]


def rope_qkv(proj, b, t, tm):
    n = b * t
    cos, sin = rope_tables_pair(t)
    nb = t // tm
    fold_spec = lambda d: pl.BlockSpec((None, d, tm // d, 3 * SLAB), lambda i: (i // nb, 0, i % nb, 0))
    outs = pl.pallas_call(
        _rope_qkv_kernel,
        grid=(n // tm,),
        in_specs=[
            pl.BlockSpec((tm, 2 * SLAB), lambda i: (i, COL_DLQ // 2)),
            pl.BlockSpec((tm, SLAB), lambda i: (i, COL_DLV)),
            pl.BlockSpec((tm, LANES), lambda i: (i % nb, 0)),
            pl.BlockSpec((tm, LANES), lambda i: (i % nb, 0)),
        ] + [pl.BlockSpec((FOLD_CHUNK, FOLD_CHUNK), lambda i: (0, 0))] * len(FOLD_DILS),
        out_specs=[fold_spec(1)] + [fold_spec(d) for d in FOLD_DILS],
        out_shape=[jax.ShapeDtypeStruct((b, d, t // d, 3 * SLAB), BF16) for d in (1,) + FOLD_DILS],
        compiler_params=_cparams(("parallel",)),
        name="rope_qkv",
    )(proj, proj, cos, sin, *[fold_permutation(d) for d in FOLD_DILS])
    by_dil = dict(zip((1,) + FOLD_DILS, outs))
    return [by_dil[d] for _, d in DIL_PAIRS]


def _band_kernel(q_ref, k_ref, v_ref, o_ref, lse_ref, *, sub, half, span):
    tq = DIL_QBLOCK
    blocks = q_ref.shape[0] // tq
    for blk in range(blocks):
        qb = pl.program_id(2) * blocks + blk
        rows = slice(blk * tq, (blk + 1) * tq)
        start = jnp.clip(qb * tq - half, 0, sub - span)
        start = pl.multiple_of(start, half)
        q_pos = qb * tq + lax.broadcasted_iota(jnp.int32, (2 * tq, span), 0) % tq
        k_pos = start + lax.broadcasted_iota(jnp.int32, (2 * tq, span), 1)
        valid = jnp.abs(k_pos - q_pos) <= half
        pair_cols = [slice(pair * HEAD_PAIR, (pair + 1) * HEAD_PAIR) for pair in range(N_PAIRS)]
        scores = [jnp.where(valid, _pair_scores(q_ref[rows, cols], k_ref[pl.ds(start, span), cols]), NEG_INF)
                  for cols in pair_cols]
        for s, cols in zip(scores, pair_cols):
            o, lse = _pair_attend(s, v_ref[pl.ds(start, span), cols])
            o_ref[rows, cols] = o
            lse_ref[rows, cols] = lse


def band_attention(qkv, window, dil):
    b, _, sub, _ = qkv.shape
    half = window // (2 * dil)
    span = DIL_QBLOCK + 2 * half
    tq = DIL_QBLOCK * min(BAND_BLOCKS_PER_STEP, sub // DIL_QBLOCK)
    nqb = sub // tq
    return pl.pallas_call(
        functools.partial(_band_kernel, sub=sub, half=half, span=span),
        grid=(b, dil, nqb),
        in_specs=[
            pl.BlockSpec((None, None, tq, SLAB), lambda i, p, s: (i, p, s, 0)),
            pl.BlockSpec((None, None, sub, SLAB), lambda i, p, s: (i, p, 0, 1)),
            pl.BlockSpec((None, None, sub, SLAB), lambda i, p, s: (i, p, 0, 2)),
        ],
        out_specs=[pl.BlockSpec((None, None, tq, SLAB), lambda i, p, s: (i, p, s, 0))] * 2,
        out_shape=[jax.ShapeDtypeStruct((b, dil, sub, SLAB), F32)] * 2,
        compiler_params=_cparams(("parallel", "parallel", "arbitrary")),
        name="band_attention_d%d" % dil,
    )(qkv, qkv, qkv)


def _dil_combine_kernel(*refs):
    nbr = len(DIL_PAIRS)
    o_refs, l_refs, out_ref = refs[:nbr], refs[nbr:2 * nbr], refs[2 * nbr]
    scratch = iter(refs[2 * nbr + 1:])

    def token_order(ref):
        dil = ref.shape[0]
        if dil == 1:
            return ref[0]
        buf = next(scratch)
        per = ref.shape[1]
        for p in range(dil):
            for c in range(SLAB // LANES):
                buf[c, pl.ds(p, per, stride=dil), :] = ref[p, :, c * LANES:(c + 1) * LANES]
        return jnp.concatenate([buf[c] for c in range(SLAB // LANES)], axis=1)

    os = [token_order(r) for r in o_refs]
    lses = [token_order(r) for r in l_refs]
    m = functools.reduce(jnp.maximum, lses)
    ws = [jnp.exp(l - m) for l in lses]
    den = functools.reduce(jnp.add, ws)
    acc = functools.reduce(jnp.add, [(w / den) * o for w, o in zip(ws, os)])
    out_ref[...] = acc.astype(out_ref.dtype)


def dil_combine(outs, lses, tm):
    b, _, t, _ = outs[0].shape
    n = b * t
    nb = t // tm
    spec = lambda a: pl.BlockSpec((None, a.shape[1], tm // a.shape[1], SLAB), lambda i: (i // nb, 0, i % nb, 0))
    n_folded = sum(1 for a in outs + lses if a.shape[1] > 1)
    return pl.pallas_call(
        _dil_combine_kernel,
        grid=(n // tm,),
        in_specs=[spec(a) for a in outs + lses],
        out_specs=pl.BlockSpec((tm, SLAB), lambda i: (i, 0)),
        out_shape=jax.ShapeDtypeStruct((n, SLAB), BF16),
        scratch_shapes=[pltpu.VMEM((SLAB // LANES, tm, LANES), F32)] * n_folded,
        compiler_params=_cparams(("parallel",)),
        name="dil_combine",
    )(*outs, *lses)


MLA_QK = MLA_NOPE + MLA_ROPE


def mla_tables(t):
    ang = _rope_angles(t, MLA_ROPE)
    cos2 = np.concatenate([np.cos(ang), np.cos(ang)], axis=1)
    sin2 = np.concatenate([np.sin(ang), np.sin(ang)], axis=1)
    z = lambda w: np.zeros((t, w), np.float32)
    q_cos = np.concatenate([np.ones((t, MLA_NOPE), np.float32), cos2, z(LANES - MLA_QK)], axis=1)
    q_sin = np.concatenate([z(MLA_NOPE), sin2, z(LANES - MLA_QK)], axis=1)
    k_cos = np.concatenate([cos2, z(LANES - MLA_ROPE)], axis=1)
    k_sin = np.concatenate([-sin2[:, :MLA_ROPE // 2], sin2[:, MLA_ROPE // 2:], z(LANES - MLA_ROPE)], axis=1)
    return tuple(jnp.asarray(a, F32) for a in (q_cos, q_sin, k_cos, k_sin))


def mla_weights(w_uq, w_ukv):
    hq = w_uq.reshape(MLA_Q_RANK, MLA_HEADS, MLA_QK)
    nope, pe = hq[..., :MLA_NOPE], hq[..., MLA_NOPE:]
    pe_rot = jnp.concatenate([-pe[..., MLA_ROPE // 2:], pe[..., :MLA_ROPE // 2]], axis=-1)
    zq = jnp.zeros((MLA_Q_RANK, MLA_HEADS, LANES - MLA_QK), w_uq.dtype)
    w1 = jnp.concatenate([nope, pe, zq], axis=-1).reshape(MLA_Q_RANK, MLA_HEADS * LANES)
    w2 = jnp.concatenate([jnp.zeros_like(nope), pe_rot, zq], axis=-1).reshape(MLA_Q_RANK, MLA_HEADS * LANES)
    hkv = w_ukv.reshape(MLA_KV_RANK, MLA_HEADS, MLA_NOPE + MLA_V)
    k_nope, v = hkv[..., :MLA_NOPE], hkv[..., MLA_NOPE:]
    zk = jnp.zeros((MLA_KV_RANK, MLA_HEADS, LANES - MLA_NOPE), w_ukv.dtype)
    wk = jnp.concatenate([k_nope, zk], axis=-1).reshape(MLA_KV_RANK, MLA_HEADS * LANES)
    zv = jnp.zeros((MLA_KV_RANK, MLA_HEADS, LANES - MLA_V), w_ukv.dtype)
    wv = jnp.concatenate([v, zv], axis=-1).reshape(MLA_KV_RANK, MLA_HEADS * LANES)
    place = np.zeros((LANES, MLA_HEADS * LANES), np.float32)
    ones = np.zeros((1, MLA_HEADS * LANES), np.float32)
    for h in range(MLA_HEADS):
        place[np.arange(MLA_ROPE), h * LANES + MLA_NOPE + np.arange(MLA_ROPE)] = 1.0
        ones[0, h * LANES + MLA_V] = 1.0
    return (w1.astype(BF16), w2.astype(BF16), wk.astype(BF16), wv.astype(BF16), jnp.asarray(place, BF16),
            jnp.asarray(ones, F32))


def _mla_q_kernel(c_ref, nw_ref, w1_ref, w2_ref, cos_ref, sin_ref, o_ref):
    cn = _rms(c_ref[...].astype(F32), nw_ref[...]).astype(BF16)
    cos = jnp.tile(cos_ref[...], (1, MLA_HEADS))
    sin = jnp.tile(sin_ref[...], (1, MLA_HEADS))
    q = _dot(cn, w1_ref[...]) * cos + _dot(cn, w2_ref[...]) * sin
    o_ref[...] = (q * (MLA_QK ** -0.5 * math.log2(math.e))).astype(o_ref.dtype)


def _mla_kv_kernel(c_ref, tail_ref, nw_ref, wk_ref, wv_ref, place_ref, ones_ref, cos_ref, sin_ref,
                   k_ref, v_ref):
    cn = _rms(c_ref[...].astype(F32), nw_ref[...]).astype(BF16)
    kr = tail_ref[...]
    half = MLA_ROPE // 2
    rot = jnp.where(_lane_lt(kr.shape, half), pltpu.roll(kr, LANES - half, 1), pltpu.roll(kr, half, 1))
    k_pe = (kr * cos_ref[...] + rot * sin_ref[...]).astype(BF16)
    k_ref[...] = (_dot(cn, wk_ref[...]) + _dot(k_pe, place_ref[...])).astype(k_ref.dtype)
    v_ref[...] = (_dot(cn, wv_ref[...]) + ones_ref[...]).astype(v_ref.dtype)


def mla_project(proj, tail, q_norm_w, kv_norm_w, w_uq, w_ukv, b, t, tm):
    n = b * t
    nb = t // tm
    w1, w2, wk, wv, place, ones = mla_weights(w_uq, w_ukv)
    q_cos, q_sin, k_cos, k_sin = mla_tables(t)
    wide = MLA_HEADS * LANES
    full = lambda shape: pl.BlockSpec(shape, lambda i: (0, 0))
    tab = pl.BlockSpec((tm, LANES), lambda i: (i % nb, 0))
    qf = pl.pallas_call(
        _mla_q_kernel,
        grid=(n // tm,),
        in_specs=[pl.BlockSpec((tm, SLAB), lambda i: (i, COL_CQ)), full((1, MLA_Q_RANK)),
                  full((MLA_Q_RANK, wide)), full((MLA_Q_RANK, wide)), tab, tab],
        out_specs=pl.BlockSpec((tm, wide), lambda i: (i, 0)),
        out_shape=jax.ShapeDtypeStruct((n, wide), BF16),
        compiler_params=_cparams(("parallel",)),
        name="mla_q_proj",
    )(proj, q_norm_w.reshape(1, MLA_Q_RANK), w1, w2, q_cos, q_sin)
    kf, vf = pl.pallas_call(
        _mla_kv_kernel,
        grid=(n // tm,),
        in_specs=[pl.BlockSpec((tm, SLAB), lambda i: (i, COL_CKV)),
                  pl.BlockSpec((tm, LANES), lambda i: (i, 0)), full((1, MLA_KV_RANK)),
                  full((MLA_KV_RANK, wide)), full((MLA_KV_RANK, wide)), full((LANES, wide)), full((1, wide)),
                  tab, tab],
        out_specs=[pl.BlockSpec((tm, wide), lambda i: (i, 0))] * 2,
        out_shape=[jax.ShapeDtypeStruct((n, wide), BF16)] * 2,
        compiler_params=_cparams(("parallel",)),
        name="mla_kv_proj",
    )(proj, tail, kv_norm_w.reshape(1, MLA_KV_RANK), wk, wv, place, ones, k_cos, k_sin)
    return qf, kf, vf


def _mla_attn_kernel(q_ref, k_ref, v_ref, o_ref, *, t, tk):
    tq = q_ref.shape[0]
    groups = [slice(hh * LANES, (hh + 1) * LANES) for hh in range(2)]
    qs = [q_ref[:, grp] for grp in groups]

    def scores(c):
        return [_dot_nt(q, k_ref[c * tk:(c + 1) * tk, grp]) for q, grp in zip(qs, groups)]

    n_chunks = t // tk
    ms = [jnp.full((tq, 1), -jnp.inf, F32)] * 2
    accs = [jnp.zeros((tq, LANES), F32)] * 2
    s_next = scores(0)
    for c in range(n_chunks):
        s_cur = s_next
        if c + 1 < n_chunks:
            s_next = scores(c + 1)
        for hh, grp in enumerate(groups):
            m_new = jnp.maximum(ms[hh], jnp.max(s_cur[hh], axis=-1, keepdims=True))
            p = jnp.exp2((s_cur[hh] - m_new).astype(BF16))
            accs[hh] = jnp.exp2(ms[hh] - m_new) * accs[hh] + _dot(p, v_ref[c * tk:(c + 1) * tk, grp])
            ms[hh] = m_new
    outs = [acc / acc[:, MLA_V:MLA_V + 1] for acc in accs]
    first = _lane_lt((tq, LANES), MLA_V)
    o_ref[...] = jnp.where(first, outs[0], pltpu.roll(outs[1], MLA_V, 1)).astype(o_ref.dtype)


def mla_attention(qf, kf, vf, b, t, tq, tk):
    n = b * t
    nq = t // tq
    return pl.pallas_call(
        functools.partial(_mla_attn_kernel, t=t, tk=tk),
        grid=(b, N_PAIRS, nq),
        in_specs=[
            pl.BlockSpec((tq, 2 * LANES), lambda i, p, s: (i * nq + s, p)),
            pl.BlockSpec((t, 2 * LANES), lambda i, p, s: (i, p)),
            pl.BlockSpec((t, 2 * LANES), lambda i, p, s: (i, p)),
        ],
        out_specs=pl.BlockSpec((tq, HEAD_PAIR), lambda i, p, s: (i * nq + s, p)),
        out_shape=jax.ShapeDtypeStruct((n, SLAB), BF16),
        compiler_params=_cparams(("parallel", "parallel", "arbitrary")),
        name="mla_attention",
    )(qf, kf, vf)


def _in_proj_segments():
    sizes = (SSM_INNER, SSM_CONV_CH, 2 * SSM_HEADS, SLAB, SLAB, SLAB, MLA_Q_RANK, MLA_KV_RANK, MLA_ROPE,
             SLAB, SLAB, SLAB)
    off = [int(v) for v in np.concatenate([[0], np.cumsum(sizes)])]
    main = ((off[0], off[2]), (off[3], off[8]), (off[9], off[12]))
    tail = ((off[8], off[9]), (off[2], off[3]))
    return main, tail


def _in_proj_columns():
    main, tail = _in_proj_segments()
    cols = lambda segs: np.concatenate([np.arange(a, b) for a, b in segs])
    return cols(main), cols(tail)


def in_proj_weights(w_in_l):
    main, tail = _in_proj_segments()
    w_main = jnp.concatenate([w_in_l[:, a:b] for a, b in main], axis=1).astype(BF16)
    pad = jnp.zeros((D_MODEL, LANES - sum(b - a for a, b in tail)), w_in_l.dtype)
    w_tail = jnp.concatenate([w_in_l[:, a:b] for a, b in tail] + [pad], axis=1).astype(BF16)
    return w_main, w_tail


def mixers(proj, tail, p, l, b, t):
    xbc = conv_silu(proj, p["conv_w"][l], p["conv_b"][l], b, t)
    y_f, y_b = ssd_scan(xbc, tail, p["dt_bias"][l], p["a_log"][l], b, t)
    y_ssm = ssd_combine(y_f, y_b, xbc, proj, p["d_skip"][l], p["ssm_norm_w"][l], 1024)

    y_na = na_attention(proj, p["na_rpb"][l], b, t)

    qf, kf, vf = mla_project(proj, tail, p["mla_q_norm_w"][l], p["mla_kv_norm_w"][l],
                             p["mla_w_uq"][l], p["mla_w_ukv"][l], b, t, 512)
    y_mla = mla_attention(qf, kf, vf, b, t, 512, 512)

    qkvs = rope_qkv(proj, b, t, 1024)
    outs, lses = zip(*[band_attention(qkv, w, d) for qkv, (w, d) in zip(qkvs, DIL_PAIRS)])
    y_dil = dil_combine(outs, lses, 1024)
    return y_ssm, y_na, y_mla, y_dil


def kernel(x, attn_norm_w, w_in, conv_w, conv_b, a_log, dt_bias, d_skip, ssm_norm_w, na_rpb,
           mla_q_norm_w, mla_kv_norm_w, mla_w_uq, mla_w_ukv, w_o, ffn_norm_w, ffn_w_gate, ffn_w_up,
           ffn_w_down, router_w, exp_w_gate, exp_w_up, exp_w_down, final_norm_w):
    b, t, _ = x.shape
    n = b * t
    depth = w_in.shape[0]
    p = dict(conv_w=conv_w, conv_b=conv_b, a_log=a_log, dt_bias=dt_bias, d_skip=d_skip,
             ssm_norm_w=ssm_norm_w, na_rpb=na_rpb, mla_q_norm_w=mla_q_norm_w,
             mla_kv_norm_w=mla_kv_norm_w, mla_w_uq=mla_w_uq, mla_w_ukv=mla_w_ukv)
    x = x.reshape(n, D_MODEL)
    cast_rows = 256
    w_o_b = cast_bf16(w_o, cast_rows)
    ffn_b = [cast_bf16(w, cast_rows) for w in (ffn_w_gate, ffn_w_up, ffn_w_down)]
    exp_b = [cast_bf16(w, cast_rows) for w in (exp_w_gate, exp_w_up, exp_w_down)]
    moe_tm = 512
    normed = False
    for l in range(depth):
        w_main, w_tail = in_proj_weights(w_in[l])
        proj, tail = in_proj(x, attn_norm_w[l], w_main, w_tail, 1024, PROJ_MAIN // 4)
        mix = mixers(proj, tail, p, l, b, t)
        x = out_proj(mix, w_o_b, l, x, 1024, 1024)
        j = l // 2
        if l % 2 == 0:
            x = ffn_dense(x, ffn_norm_w[l], *ffn_b, j, 512, 512)
        else:
            top_i, gates = moe_router(x, ffn_norm_w[l], router_w[j], 512)
            src, pos, tile_expert, tile_valid = moe_plan(top_i, moe_tm)
            y = moe_ffn(x, ffn_norm_w[l], *exp_b, j, src, tile_expert, tile_valid, moe_tm, 512)
            last = l == depth - 1
            x = moe_combine(x, gates, y, pos, final_norm_w if last else None, 256)
            normed = last
    if not normed:
        x = rmsnorm_rows(x, final_norm_w, 1024)
    return x.reshape(b, t, D_MODEL)
```

```python
import functools
import math

import numpy as np
import jax
import jax.numpy as jnp
from jax import lax
from jax.experimental import pallas as pl
from jax.experimental.pallas import tpu as pltpu

F32 = jnp.float32
BF16 = jnp.bfloat16

D_MODEL = 2048
GRID_W = 64
HEAD_DIM = 64
ROPE_THETA = 10000.0
NORM_EPS = 1e-6
NEG_INF = -1e30

SSM_HEADS = 8
SSM_HEAD_DIM = 64
SSM_INNER = SSM_HEADS * SSM_HEAD_DIM
SSM_GROUPS = 2
SSM_STATE = 128
SSM_CONV = 5
SSM_CHUNK = 128
SSM_CONV_CH = SSM_INNER + 2 * SSM_GROUPS * SSM_STATE

NA_HEADS = 8
NA_WIN_ROWS = 8
NA_WIN_COLS = 16
NA_COL_BLOCK = 16
NA_KEY_COLS = 32
NA_ROWS_PER_STEP = 4

MLA_HEADS = 8
MLA_Q_RANK = 512
MLA_KV_RANK = 512
MLA_NOPE = 64
MLA_ROPE = 32
MLA_V = 64

DIL_HEADS = 8
DIL_PAIRS = ((128, 1), (512, 4), (2048, 16))
DIL_QBLOCK = 128
BAND_BLOCKS_PER_STEP = 4

N_EXPERTS = 8
TOP_K = 2
SPLIT_PARTS = 3

LANES = 128
HEAD_PAIR = 2 * HEAD_DIM
N_PAIRS = 4
SLAB = 512

COL_Z, COL_XBC, COL_NAQ, COL_NAK, COL_NAV, COL_CQ, COL_CKV, COL_DLQ, COL_DLK, COL_DLV = (
    0, 1, 3, 4, 5, 6, 7, 8, 9, 10)
PROJ_MAIN = 11 * SLAB
TAIL_DT = 32

VMEM_LIMIT = 56 * 1024 * 1024


def _cparams(sem, vmem=VMEM_LIMIT):
    return pltpu.CompilerParams(dimension_semantics=sem, vmem_limit_bytes=vmem)


def _lane_lt(shape, bound, period=None):
    lane = lax.broadcasted_iota(jnp.int32, shape, len(shape) - 1)
    if period is not None:
        lane = lane % period
    return lane < bound


def _rms(x, w):
    ms = jnp.mean(x * x, axis=-1, keepdims=True)
    return x * lax.rsqrt(ms + NORM_EPS) * w


def _dot(a, b):
    return jnp.dot(a, b, preferred_element_type=F32)


def _dot_nt(a, b):
    return lax.dot_general(a, b, (((1,), (1,)), ((), ())), preferred_element_type=F32)


def _dot_tn(a, b):
    return lax.dot_general(a, b, (((0,), (0,)), ((), ())), preferred_element_type=F32)


def _cast_kernel(x_ref, o_ref):
    o_ref[...] = x_ref[...].astype(o_ref.dtype)


def cast_bf16(w, tr):
    shape = w.shape
    w2 = w.reshape(-1, shape[-1])
    r, c = w2.shape
    out = pl.pallas_call(
        _cast_kernel,
        grid=(r // tr,),
        in_specs=[pl.BlockSpec((tr, c), lambda i: (i, 0))],
        out_specs=pl.BlockSpec((tr, c), lambda i: (i, 0)),
        out_shape=jax.ShapeDtypeStruct((r, c), BF16),
        compiler_params=_cparams(("parallel",)),
        name="cast_bf16",
    )(w2)
    return out.reshape(shape)


def _in_proj_kernel(x_ref, nw_ref, w_ref, wt_ref, o_ref, t_ref, h_ref):
    @pl.when(pl.program_id(1) == 0)
    def _():
        h = _rms(x_ref[...], nw_ref[...]).astype(BF16)
        h_ref[...] = h
        t_ref[...] = _dot(h, wt_ref[...])

    o_ref[...] = _dot(h_ref[...], w_ref[...]).astype(o_ref.dtype)


def in_proj(x, nw, w_main, w_tail, tm, tn):
    n, k = x.shape
    nout = w_main.shape[1]
    return pl.pallas_call(
        _in_proj_kernel,
        grid=(n // tm, nout // tn),
        in_specs=[
            pl.BlockSpec((tm, k), lambda i, j: (i, 0)),
            pl.BlockSpec((1, k), lambda i, j: (0, 0)),
            pl.BlockSpec((k, tn), lambda i, j: (0, j)),
            pl.BlockSpec((k, LANES), lambda i, j: (0, 0)),
        ],
        out_specs=[pl.BlockSpec((tm, tn), lambda i, j: (i, j)), pl.BlockSpec((tm, LANES), lambda i, j: (i, 0))],
        out_shape=[jax.ShapeDtypeStruct((n, nout), BF16), jax.ShapeDtypeStruct((n, LANES), F32)],
        scratch_shapes=[pltpu.VMEM((tm, k), BF16)],
        compiler_params=_cparams(("parallel", "arbitrary")),
        name="in_proj",
    )(x, nw.reshape(1, k), w_main, w_tail)


def _out_proj_kernel(a0_ref, a1_ref, a2_ref, a3_ref, w_ref, r_ref, o_ref):
    acc = r_ref[...]
    for s, a_ref in enumerate((a0_ref, a1_ref, a2_ref, a3_ref)):
        acc = acc + _dot(a_ref[...], w_ref[s * SLAB:(s + 1) * SLAB, :])
    o_ref[...] = acc


def out_proj(mix, w, layer, res, tm, tn):
    n = res.shape[0]
    return pl.pallas_call(
        _out_proj_kernel,
        grid=(n // tm, D_MODEL // tn),
        in_specs=[pl.BlockSpec((tm, SLAB), lambda i, j: (i, 0))] * 4 + [
            pl.BlockSpec((None, 4 * SLAB, tn), lambda i, j: (layer, 0, j)),
            pl.BlockSpec((tm, tn), lambda i, j: (i, j)),
        ],
        out_specs=pl.BlockSpec((tm, tn), lambda i, j: (i, j)),
        out_shape=jax.ShapeDtypeStruct((n, D_MODEL), F32),
        compiler_params=_cparams(("parallel", "arbitrary")),
        name="out_proj",
    )(*mix, w, res)


def _ffn_kernel(x_ref, nw_ref, wg_ref, wu_ref, wd_ref, o_ref, h_ref):
    @pl.when(pl.program_id(1) == 0)
    def _():
        x = x_ref[...]
        h_ref[...] = _rms(x, nw_ref[...]).astype(BF16)
        o_ref[...] = x

    _swiglu_rows(h_ref, wg_ref, wu_ref, wd_ref, o_ref)


FFN_ROW_CHUNK = 512


def _swiglu_rows(h_ref, wg_ref, wu_ref, wd_ref, o_ref):
    tm = h_ref.shape[0]
    for r0 in range(0, tm, FFN_ROW_CHUNK):
        rows = slice(r0, r0 + FFN_ROW_CHUNK)
        h = h_ref[rows, :]
        g = _dot(h, wg_ref[...])
        u = _dot(h, wu_ref[...])
        a = (g * jax.nn.sigmoid(g) * u).astype(BF16)
        o_ref[rows, :] += _dot(a, wd_ref[...])


def ffn_dense(x, nw, wg, wu, wd, layer, tm, tf):
    n = x.shape[0]
    d_ff = wg.shape[-1]
    return pl.pallas_call(
        _ffn_kernel,
        grid=(n // tm, d_ff // tf),
        in_specs=[
            pl.BlockSpec((tm, D_MODEL), lambda i, j: (i, 0)),
            pl.BlockSpec((1, D_MODEL), lambda i, j: (0, 0)),
            pl.BlockSpec((None, D_MODEL, tf), lambda i, j: (layer, 0, j)),
            pl.BlockSpec((None, D_MODEL, tf), lambda i, j: (layer, 0, j)),
            pl.BlockSpec((None, tf, D_MODEL), lambda i, j: (layer, j, 0)),
        ],
        out_specs=pl.BlockSpec((tm, D_MODEL), lambda i, j: (i, 0)),
        out_shape=jax.ShapeDtypeStruct((n, D_MODEL), F32),
        scratch_shapes=[pltpu.VMEM((tm, D_MODEL), BF16)],
        compiler_params=_cparams(("parallel", "arbitrary")),
        name="ffn_dense",
    )(x, nw.reshape(1, D_MODEL), wg, wu, wd)


def _router_kernel(x_ref, nw_ref, rw_ref, idx_ref, gate_ref):
    h = _rms(x_ref[...], nw_ref[...])
    acc = jnp.zeros((h.shape[0], LANES), F32)
    rem = h
    for _ in range(SPLIT_PARTS):
        part = rem.astype(BF16)
        acc = acc + _dot(part, rw_ref[...])
        rem = rem - part.astype(F32)
    logits = acc
    for k in range(1, SPLIT_PARTS):
        logits = logits + pltpu.roll(acc, LANES - k * N_EXPERTS, 1)
    lane = lax.broadcasted_iota(jnp.int32, logits.shape, 1)
    logits = jnp.where(lane < N_EXPERTS, logits, -jnp.inf)
    m1 = jnp.max(logits, axis=-1, keepdims=True)
    i1 = jnp.min(jnp.where(logits == m1, lane, LANES), axis=-1, keepdims=True)
    rest = jnp.where(lane == i1, -jnp.inf, logits)
    m2 = jnp.max(rest, axis=-1, keepdims=True)
    i2 = jnp.min(jnp.where(rest == m2, lane, LANES), axis=-1, keepdims=True)
    e2 = jnp.exp(m2 - m1)
    g1 = 1.0 / (1.0 + e2)
    g2 = e2 / (1.0 + e2)
    idx_ref[...] = jnp.where(lane == 0, i1, i2)[:, :TOP_K]
    gate_ref[...] = jnp.where(lane == 0, g1, g2)[:, :TOP_K]


def moe_router(x, nw, router_w, tm):
    n = x.shape[0]
    parts, rem = [], router_w
    for _ in range(SPLIT_PARTS):
        parts.append(rem.astype(BF16))
        rem = rem - parts[-1].astype(F32)
    pad = jnp.zeros((D_MODEL, LANES - SPLIT_PARTS * N_EXPERTS), BF16)
    rw = jnp.concatenate(parts + [pad], axis=1)
    return pl.pallas_call(
        _router_kernel,
        grid=(n // tm,),
        in_specs=[
            pl.BlockSpec((tm, D_MODEL), lambda i: (i, 0)),
            pl.BlockSpec((1, D_MODEL), lambda i: (0, 0)),
            pl.BlockSpec((D_MODEL, LANES), lambda i: (0, 0)),
        ],
        out_specs=[pl.BlockSpec((tm, TOP_K), lambda i: (i, 0)),
                   pl.BlockSpec((tm, TOP_K), lambda i: (i, 0))],
        out_shape=[jax.ShapeDtypeStruct((n, TOP_K), jnp.int32),
                   jax.ShapeDtypeStruct((n, TOP_K), F32)],
        compiler_params=_cparams(("parallel",)),
        name="moe_router",
    )(x, nw.reshape(1, D_MODEL), rw)


def moe_plan(top_i, tm):
    n = top_i.shape[0]
    flat_e = top_i.reshape(-1)
    onehot = (flat_e[:, None] == jnp.arange(N_EXPERTS, dtype=jnp.int32)[None, :]).astype(jnp.int32)
    csum = jnp.cumsum(onehot, axis=0)
    counts = csum[-1]
    rank = jnp.sum(onehot * csum, axis=1) - 1
    padded = ((counts + tm - 1) // tm) * tm
    pend = jnp.cumsum(padded)
    pstart = pend - padded
    pos = pstart[flat_e] + rank
    n_slots = n * TOP_K + N_EXPERTS * tm
    n_tiles = n_slots // tm
    src = jnp.zeros((n_slots,), jnp.int32).at[pos].set(jnp.arange(n * TOP_K, dtype=jnp.int32) // TOP_K)
    tile_start = jnp.arange(n_tiles, dtype=jnp.int32) * tm
    tile_expert = jnp.sum((tile_start[:, None] >= pend[None, :]).astype(jnp.int32), axis=1)
    tile_valid = (tile_start < pend[-1]).astype(jnp.int32)
    last_valid = jnp.maximum(pend[-1] // tm - 1, 0)
    tile_expert = jnp.where(tile_valid == 1, tile_expert, tile_expert[last_valid]).astype(jnp.int32)
    return src, pos.reshape(n, TOP_K).astype(jnp.int32), tile_expert, tile_valid


def _moe_ffn_kernel(te_ref, tv_ref, src_ref, nsrc_ref, x_hbm, nw_ref, wg_ref, wu_ref, wd_ref, y_ref,
                    xbuf, h_ref, sem, *, tm):
    i = pl.program_id(0)
    j = pl.program_id(1)
    n_tiles = pl.num_programs(0)
    valid = tv_ref[i] == 1

    def row_copy(idx_ref, r):
        tok = idx_ref[0, 0, r]
        return pltpu.make_async_copy(x_hbm.at[pl.ds(tok, 1)], xbuf.at[pl.ds(r, 1)], sem)

    def gather(idx_ref):
        def start(r, c):
            row_copy(idx_ref, r).start()
            return c

        lax.fori_loop(0, tm, start, 0, unroll=8)

    @pl.when(j == 0)
    def _():
        y_ref[...] = jnp.zeros_like(y_ref)

    @pl.when(jnp.logical_and(j == 0, valid))
    def _():
        @pl.when(i == 0)
        def _():
            gather(src_ref)

        pltpu.make_async_copy(x_hbm.at[pl.ds(0, tm)], xbuf, sem).wait()
        h_ref[...] = _rms(xbuf[...], nw_ref[...]).astype(BF16)

        nxt = jnp.minimum(i + 1, n_tiles - 1)

        @pl.when(jnp.logical_and(i + 1 < n_tiles, tv_ref[nxt] == 1))
        def _():
            gather(nsrc_ref)

    @pl.when(valid)
    def _():
        _swiglu_rows(h_ref, wg_ref, wu_ref, wd_ref, y_ref)


def moe_ffn(x, nw, wg, wu, wd, layer, src, tile_expert, tile_valid, tm, tf):
    n_slots = src.shape[0]
    n_tiles = n_slots // tm
    d_ff = wg.shape[-1]
    nf = d_ff // tf

    def wcol(i, j, te_ref, tv_ref):
        return (layer, te_ref[i], 0, jnp.where(tv_ref[i] == 1, j, nf - 1))

    def wrow(i, j, te_ref, tv_ref):
        return (layer, te_ref[i], jnp.where(tv_ref[i] == 1, j, nf - 1), 0)

    grid_spec = pltpu.PrefetchScalarGridSpec(
        num_scalar_prefetch=2,
        grid=(n_tiles, nf),
        in_specs=[
            pl.BlockSpec((1, 1, tm), lambda i, j, *_: (i, 0, 0), memory_space=pltpu.SMEM),
            pl.BlockSpec((1, 1, tm), lambda i, j, *_: (jnp.minimum(i + 1, n_tiles - 1), 0, 0),
                         memory_space=pltpu.SMEM),
            pl.BlockSpec(memory_space=pl.ANY),
            pl.BlockSpec((1, D_MODEL), lambda i, j, *_: (0, 0)),
            pl.BlockSpec((None, None, D_MODEL, tf), wcol),
            pl.BlockSpec((None, None, D_MODEL, tf), wcol),
            pl.BlockSpec((None, None, tf, D_MODEL), wrow),
        ],
        out_specs=pl.BlockSpec((tm, D_MODEL), lambda i, j, *_: (i, 0)),
        scratch_shapes=[pltpu.VMEM((tm, D_MODEL), F32), pltpu.VMEM((tm, D_MODEL), BF16),
                        pltpu.SemaphoreType.DMA],
    )
    src3 = src.reshape(n_tiles, 1, tm)
    return pl.pallas_call(
        functools.partial(_moe_ffn_kernel, tm=tm),
        grid_spec=grid_spec,
        out_shape=jax.ShapeDtypeStruct((n_slots, D_MODEL), F32),
        compiler_params=_cparams(("arbitrary", "arbitrary")),
        name="moe_ffn",
    )(tile_expert, tile_valid, src3, src3, x, nw.reshape(1, D_MODEL), wg, wu, wd)


def _moe_combine_kernel(pos_ref, npos_ref, x_ref, gate_ref, y_hbm, fw_ref, o_ref, ybuf, sem, *, tm, final_norm):
    i = pl.program_id(0)
    n_tiles = pl.num_programs(0)
    cur = i % 2

    def row_copy(idx_ref, buf, r, k):
        slot = idx_ref[0, 0, r * TOP_K + k]
        return pltpu.make_async_copy(y_hbm.at[pl.ds(slot, 1)], ybuf.at[buf, k, pl.ds(r, 1)], sem.at[buf])

    def gather(idx_ref, buf):
        def start(r, c):
            for k in range(TOP_K):
                row_copy(idx_ref, buf, r, k).start()
            return c

        lax.fori_loop(0, tm, start, 0, unroll=8)

    @pl.when(i == 0)
    def _():
        gather(pos_ref, 0)

    @pl.when(i + 1 < n_tiles)
    def _():
        gather(npos_ref, 1 - cur)

    for k in range(TOP_K):
        pltpu.make_async_copy(y_hbm.at[pl.ds(0, tm)], ybuf.at[cur, k], sem.at[cur]).wait()
    gates = gate_ref[...]
    out = x_ref[...]
    for k in range(TOP_K):
        out = out + gates[:, k:k + 1] * ybuf[cur, k]
    if final_norm:
        out = _rms(out, fw_ref[...])
    o_ref[...] = out


def moe_combine(x, gates, y, pos, final_w, tm):
    n = x.shape[0]
    final_norm = final_w is not None
    fw = final_w if final_norm else jnp.ones((D_MODEL,), F32)
    n_tiles = n // tm
    pos3 = pos.reshape(n_tiles, 1, tm * TOP_K)
    return pl.pallas_call(
        functools.partial(_moe_combine_kernel, tm=tm, final_norm=final_norm),
        grid=(n_tiles,),
        in_specs=[
            pl.BlockSpec((1, 1, tm * TOP_K), lambda i: (i, 0, 0), memory_space=pltpu.SMEM),
            pl.BlockSpec((1, 1, tm * TOP_K), lambda i: (jnp.minimum(i + 1, n_tiles - 1), 0, 0),
                         memory_space=pltpu.SMEM),
            pl.BlockSpec((tm, D_MODEL), lambda i: (i, 0)),
            pl.BlockSpec((tm, TOP_K), lambda i: (i, 0)),
            pl.BlockSpec(memory_space=pl.ANY),
            pl.BlockSpec((1, D_MODEL), lambda i: (0, 0)),
        ],
        out_specs=pl.BlockSpec((tm, D_MODEL), lambda i: (i, 0)),
        out_shape=jax.ShapeDtypeStruct((n, D_MODEL), F32),
        scratch_shapes=[pltpu.VMEM((2, TOP_K, tm, D_MODEL), F32), pltpu.SemaphoreType.DMA((2,))],
        compiler_params=_cparams(("arbitrary",)),
        name="moe_combine",
    )(pos3, pos3, x, gates, y, fw.reshape(1, D_MODEL))


def _rmsnorm_kernel(x_ref, w_ref, o_ref):
    o_ref[...] = _rms(x_ref[...], w_ref[...])


def rmsnorm_rows(x, w, tm):
    n = x.shape[0]
    return pl.pallas_call(
        _rmsnorm_kernel,
        grid=(n // tm,),
        in_specs=[pl.BlockSpec((tm, D_MODEL), lambda i: (i, 0)),
                  pl.BlockSpec((1, D_MODEL), lambda i: (0, 0))],
        out_specs=pl.BlockSpec((tm, D_MODEL), lambda i: (i, 0)),
        out_shape=jax.ShapeDtypeStruct((n, D_MODEL), F32),
        compiler_params=_cparams(("parallel",)),
        name="final_norm",
    )(x, w.reshape(1, D_MODEL))


CONV_PAD = 8


def _conv_kernel(x_ref, w_ref, b_ref, o_ref, pad_ref, *, t):
    half = SSM_CONV // 2
    zeros = jnp.zeros((CONV_PAD, pad_ref.shape[1]), F32)
    pad_ref[0:CONV_PAD, :] = zeros
    pad_ref[CONV_PAD + t:CONV_PAD + t + CONV_PAD, :] = zeros
    pad_ref[CONV_PAD:CONV_PAD + t, :] = x_ref[...].astype(F32)
    acc = jnp.zeros(o_ref.shape, F32) + b_ref[...]
    for k in range(SSM_CONV):
        acc = acc + pad_ref[pl.ds(CONV_PAD - half + k, t), :] * w_ref[k:k + 1, :]
    o_ref[...] = (acc * jax.nn.sigmoid(acc)).astype(o_ref.dtype)


def conv_silu(proj, conv_w, conv_b, b, t):
    tc = 256
    nblk = SSM_CONV_CH // tc
    col0 = COL_XBC * SLAB // tc
    return pl.pallas_call(
        functools.partial(_conv_kernel, t=t),
        grid=(b, nblk),
        in_specs=[
            pl.BlockSpec((t, tc), lambda i, j: (i, col0 + j)),
            pl.BlockSpec((SSM_CONV, tc), lambda i, j: (0, j)),
            pl.BlockSpec((1, tc), lambda i, j: (0, j)),
        ],
        out_specs=pl.BlockSpec((t, tc), lambda i, j: (i, j)),
        out_shape=jax.ShapeDtypeStruct((b * t, SSM_CONV_CH), BF16),
        scratch_shapes=[pltpu.VMEM((t + 2 * CONV_PAD, tc), F32)],
        compiler_params=_cparams(("parallel", "parallel")),
        name="conv_silu",
    )(proj, conv_w, conv_b.reshape(1, SSM_CONV_CH))


def _ssd_stage1(xbc_ref, tail_ref, bias_ref, alog_ref, tri_ref, sel_ref, state_ref, direction):
    q = SSM_CHUNK
    dt = jax.nn.softplus(tail_ref[...] + bias_ref[...])
    da = dt * (-jnp.exp(alog_ref[...]))
    cs = jnp.dot(tri_ref[...], da, preferred_element_type=F32, precision=lax.Precision.HIGHEST)
    total = cs[q - 1:q, :]
    if direction == 0:
        e_out = cs
        e_in = total - cs
        e_seg = cs
    else:
        ex = cs - da
        e_out = total - ex
        e_in = ex
        e_seg = -ex
    dec_out_b = jnp.exp(e_out).astype(BF16)
    dec_in_dt_b = (jnp.exp(e_in) * dt).astype(BF16)
    pairs_per_group = N_PAIRS // SSM_GROUPS
    cbs, y_offs, in_scales = [], [], []
    for g in range(SSM_GROUPS):
        bm = xbc_ref[:, SSM_INNER + g * SSM_STATE:SSM_INNER + (g + 1) * SSM_STATE]
        cm = xbc_ref[:, SSM_INNER + (SSM_GROUPS + g) * SSM_STATE:SSM_INNER + (SSM_GROUPS + g + 1) * SSM_STATE]
        cbs.append(_dot_nt(cm, bm))
        for pair in range(g * pairs_per_group, (g + 1) * pairs_per_group):
            sel = sel_ref[direction, pair]
            y_offs.append(_dot(dec_out_b, sel) * _dot(cm, state_ref[pair].astype(BF16)))
            in_scales.append(_dot(dec_in_dt_b, sel))
    return dict(e_seg=e_seg, e_seg_t=jnp.transpose(e_seg), dt_t=jnp.transpose(dt), dec_tot=jnp.exp(total),
                cbs=cbs, y_offs=y_offs, in_scales=in_scales)


def _ssd_stage2(ctx, xbc_ref, y_ref, state_ref, direction):
    q = SSM_CHUNK
    row = lax.broadcasted_iota(jnp.int32, (q, q), 0)
    col = lax.broadcasted_iota(jnp.int32, (q, q), 1)
    keep = (row >= col) if direction == 0 else (col >= row)
    first_half = _lane_lt((q, HEAD_PAIR), HEAD_DIM)
    pairs_per_group = N_PAIRS // SSM_GROUPS
    for pair in range(N_PAIRS):
        g = pair // pairs_per_group
        bm = xbc_ref[:, SSM_INNER + g * SSM_STATE:SSM_INNER + (g + 1) * SSM_STATE]
        xs_pair = xbc_ref[:, pair * HEAD_PAIR:(pair + 1) * HEAD_PAIR]
        ys, decs = [], []
        for hh in range(2):
            lane = TAIL_DT + direction * SSM_HEADS + pair * 2 + hh
            seg = ctx["e_seg"][:, lane:lane + 1] - ctx["e_seg_t"][lane:lane + 1, :]
            lmat = jnp.where(keep, jnp.exp(seg), 0.0)
            w = (ctx["cbs"][g] * lmat * ctx["dt_t"][lane:lane + 1, :]).astype(BF16)
            ys.append(_dot(w, xs_pair))
            decs.append(ctx["dec_tot"][:, lane:lane + 1])
        y_ref[:, pair * HEAD_PAIR:(pair + 1) * HEAD_PAIR] = jnp.where(first_half, ys[0], ys[1]) + ctx["y_offs"][pair]
        st = state_ref[pair]
        first_half_s = _lane_lt(st.shape, HEAD_DIM)
        state_ref[pair] = (st * jnp.where(first_half_s, decs[0], decs[1])
                           + _dot_tn(bm, (xs_pair * ctx["in_scales"][pair]).astype(BF16)))


def _ssd_kernel(xbc_f_ref, tail_f_ref, xbc_b_ref, tail_b_ref, bias_ref, alog_ref, tri_ref, sel_ref,
                y_f_ref, y_b_ref, state_ref):
    @pl.when(pl.program_id(1) == 0)
    def _():
        state_ref[...] = jnp.zeros_like(state_ref)

    ins = ((xbc_f_ref, tail_f_ref, y_f_ref), (xbc_b_ref, tail_b_ref, y_b_ref))
    ctxs = [_ssd_stage1(xbc_ref, tail_ref, bias_ref, alog_ref, tri_ref, sel_ref, state_ref.at[d], d)
            for d, (xbc_ref, tail_ref, _) in enumerate(ins)]
    for d, (xbc_ref, _, y_ref) in enumerate(ins):
        _ssd_stage2(ctxs[d], xbc_ref, y_ref, state_ref.at[d], d)


def _ssd_lane_selectors():
    sel = np.zeros((2, N_PAIRS, LANES, HEAD_PAIR), np.float32)
    for d in range(2):
        for pair in range(N_PAIRS):
            for hh in range(2):
                sel[d, pair, TAIL_DT + d * SSM_HEADS + pair * 2 + hh, hh * HEAD_DIM:(hh + 1) * HEAD_DIM] = 1.0
    return jnp.asarray(sel, BF16)


def ssd_scan(xbc, tail, dt_bias, a_log, b, t):
    q = SSM_CHUNK
    nc = t // q
    bias_row = jnp.zeros((1, LANES), F32).at[0, TAIL_DT:TAIL_DT + 2 * SSM_HEADS].set(dt_bias.reshape(-1))
    alog_row = jnp.zeros((1, LANES), F32).at[0, TAIL_DT:TAIL_DT + 2 * SSM_HEADS].set(a_log.reshape(-1))
    tri = jnp.asarray(np.tril(np.ones((q, q), np.float32)))
    fwd = lambda i, c: (i * nc + c, 0)
    bwd = lambda i, c: (i * nc + nc - 1 - c, 0)
    const = lambda i, c: (0, 0)
    return pl.pallas_call(
        _ssd_kernel,
        grid=(b, nc),
        in_specs=[
            pl.BlockSpec((q, SSM_CONV_CH), fwd), pl.BlockSpec((q, LANES), fwd),
            pl.BlockSpec((q, SSM_CONV_CH), bwd), pl.BlockSpec((q, LANES), bwd),
            pl.BlockSpec((1, LANES), const), pl.BlockSpec((1, LANES), const), pl.BlockSpec((q, q), const),
            pl.BlockSpec((2, N_PAIRS, LANES, HEAD_PAIR), lambda i, c: (0, 0, 0, 0)),
        ],
        out_specs=[pl.BlockSpec((q, SSM_INNER), fwd), pl.BlockSpec((q, SSM_INNER), bwd)],
        out_shape=[jax.ShapeDtypeStruct((b * t, SSM_INNER), F32)] * 2,
        scratch_shapes=[pltpu.VMEM((2, N_PAIRS, SSM_STATE, HEAD_PAIR), F32)],
        compiler_params=_cparams(("parallel", "arbitrary")),
        name="ssd_scan",
    )(xbc, tail, xbc, tail, bias_row, alog_row, tri, _ssd_lane_selectors())


def _ssd_combine_kernel(yf_ref, yb_ref, xs_ref, z_ref, d_ref, nw_ref, o_ref):
    y = yf_ref[...] + yb_ref[...] + xs_ref[...].astype(F32) * d_ref[...]
    z = z_ref[...].astype(F32)
    o_ref[...] = _rms(y * (z * jax.nn.sigmoid(z)), nw_ref[...]).astype(o_ref.dtype)


def ssd_combine(y_f, y_b, xbc, proj, d_skip, norm_w, tm):
    n = y_f.shape[0]
    d_row = jnp.repeat(d_skip, SSM_HEAD_DIM).reshape(1, SSM_INNER)
    row = lambda i: (i, 0)
    return pl.pallas_call(
        _ssd_combine_kernel,
        grid=(n // tm,),
        in_specs=[
            pl.BlockSpec((tm, SSM_INNER), row),
            pl.BlockSpec((tm, SSM_INNER), row),
            pl.BlockSpec((tm, SSM_INNER), row),
            pl.BlockSpec((tm, SLAB), lambda i: (i, COL_Z)),
            pl.BlockSpec((1, SSM_INNER), lambda i: (0, 0)),
            pl.BlockSpec((1, SSM_INNER), lambda i: (0, 0)),
        ],
        out_specs=pl.BlockSpec((tm, SSM_INNER), row),
        out_shape=jax.ShapeDtypeStruct((n, SSM_INNER), BF16),
        compiler_params=_cparams(("parallel",)),
        name="ssd_combine",
    )(y_f, y_b, xbc, proj, d_row, norm_w.reshape(1, SSM_INNER))


def _pair_scores(q2, k2):
    first_q = _lane_lt(q2.shape, HEAD_DIM)
    zero = jnp.zeros_like(q2)
    qs = jnp.concatenate([jnp.where(first_q, q2, zero), jnp.where(first_q, zero, q2)], axis=0)
    return _dot_nt(qs, k2)


def _pair_attend(s, v2):
    tq = s.shape[0] // 2
    m = jnp.max(s, axis=-1, keepdims=True)
    p = jnp.exp(s - m)
    l = jnp.sum(p, axis=-1, keepdims=True)
    o = _dot(p.astype(BF16), v2) / l
    lse = m + jnp.log(l)
    first_o = _lane_lt((tq, HEAD_PAIR), HEAD_DIM)
    return jnp.where(first_o, o[:tq], o[tq:]), jnp.where(first_o, lse[:tq], lse[tq:])


def na_bias_tables(rpb, rows):
    kr = min(NA_WIN_ROWS, rows)
    qc = np.arange(GRID_W)
    kc = np.arange(GRID_W)
    q_start = np.clip(qc - NA_WIN_COLS // 2, 0, GRID_W - NA_WIN_COLS)
    col_in = (kc[None, :] >= q_start[:, None]) & (kc[None, :] < q_start[:, None] + NA_WIN_COLS)
    col_off = np.clip(kc[None, :] - qc[:, None] + NA_WIN_COLS - 1, 0, 2 * NA_WIN_COLS - 2)
    onehot = (col_off[None] == np.arange(2 * NA_WIN_COLS - 1)[:, None, None]).astype(np.float32)
    expanded = jnp.einsum("hrc,cqk->hqrk", rpb, jnp.asarray(onehot), precision=lax.Precision.HIGHEST)
    expanded = jnp.where(jnp.asarray(col_in)[None, :, None, :], expanded, NEG_INF)

    def table(r):
        row_start = int(np.clip(r - kr // 2, 0, rows - kr))
        ro0 = row_start - r + NA_WIN_ROWS - 1
        return expanded[:, :, ro0:ro0 + kr, :].reshape(N_PAIRS, 2 * GRID_W, kr * GRID_W)

    rs = NA_ROWS_PER_STEP
    lo = [table(r) for r in range(rs)]
    mid = [table(min(rs, rows - 1))] * rs
    hi = [table(r) for r in range(rows - rs, rows)]
    return jnp.stack([jnp.stack(lo), jnp.stack(mid), jnp.stack(hi)])


def _na_kernel(q_ref, k_ref, v_ref, bias_ref, o_ref, *, rows, kr):
    step = pl.program_id(1)
    rs = NA_ROWS_PER_STEP
    for rr in range(rs):
        r = step * rs + rr
        row_start = jnp.clip(r - kr // 2, 0, rows - kr)
        k0 = pl.multiple_of(row_start * GRID_W, GRID_W)
        pair_cols = [slice(pair * HEAD_PAIR, (pair + 1) * HEAD_PAIR) for pair in range(N_PAIRS)]
        scores = []
        for pair, cols in enumerate(pair_cols):
            q2 = q_ref[rr * GRID_W:(rr + 1) * GRID_W, cols] * jnp.asarray(HEAD_DIM ** -0.5, BF16)
            scores.append(_pair_scores(q2, k_ref[pl.ds(k0, kr * GRID_W), cols]) + bias_ref[0, rr, pair])
        for s, cols in zip(scores, pair_cols):
            o, _ = _pair_attend(s, v_ref[pl.ds(k0, kr * GRID_W), cols])
            o_ref[rr * GRID_W:(rr + 1) * GRID_W, cols] = o.astype(o_ref.dtype)


def na_attention(proj, rpb, b, t):
    rows = t // GRID_W
    kr = min(NA_WIN_ROWS, rows)
    rs = NA_ROWS_PER_STEP
    nsteps = rows // rs
    bias = na_bias_tables(rpb, rows)

    def kind(i, s):
        return jnp.where(s == 0, 0, jnp.where(s == nsteps - 1, 2, 1))

    return pl.pallas_call(
        functools.partial(_na_kernel, rows=rows, kr=kr),
        grid=(b, nsteps),
        in_specs=[
            pl.BlockSpec((rs * GRID_W, SLAB), lambda i, s: (i * nsteps + s, COL_NAQ)),
            pl.BlockSpec((t, SLAB), lambda i, s: (i, COL_NAK)),
            pl.BlockSpec((t, SLAB), lambda i, s: (i, COL_NAV)),
            pl.BlockSpec((1, rs, N_PAIRS, 2 * GRID_W, kr * GRID_W), lambda i, s: (kind(i, s), 0, 0, 0, 0)),
        ],
        out_specs=pl.BlockSpec((rs * GRID_W, SLAB), lambda i, s: (i * nsteps + s, 0)),
        out_shape=jax.ShapeDtypeStruct((b * t, SLAB), BF16),
        compiler_params=_cparams(("parallel", "arbitrary")),
        name="na_attention",
    )(proj, proj, proj, bias)


def _rope_angles(t, d):
    inv = ROPE_THETA ** (-np.arange(0, d, 2, dtype=np.float32) / d)
    return np.arange(t, dtype=np.float32)[:, None] * inv[None, :]


def rope_tables_pair(t):
    ang = _rope_angles(t, HEAD_DIM)
    cos = np.tile(np.cos(ang), (1, 4))
    sin = np.tile(np.concatenate([-np.sin(ang), np.sin(ang)], axis=1), (1, 2))
    return jnp.asarray(cos, F32), jnp.asarray(sin, F32)


FOLD_CHUNK = 256
FOLD_DILS = tuple(d for _, d in DIL_PAIRS if d > 1)


def fold_permutation(dil):
    per = FOLD_CHUNK // dil
    perm = np.zeros((FOLD_CHUNK, FOLD_CHUNK), np.float32)
    dst = np.arange(FOLD_CHUNK)
    perm[dst, (dst % per) * dil + dst // per] = 1.0
    return jnp.asarray(perm, BF16)


def _rope_qkv_kernel(x_ref, v_ref, cos_ref, sin_ref, *rest):
    nd = len(FOLD_DILS)
    perm_refs, o_ref, fold_refs = rest[:nd], rest[nd], rest[nd + 1:]
    cos = cos_ref[...]
    sin = sin_ref[...]
    half = HEAD_DIM // 2
    for c in range(x_ref.shape[1] // LANES):
        x = x_ref[:, c * LANES:(c + 1) * LANES].astype(F32)
        rot = jnp.where(_lane_lt(x.shape, half, HEAD_DIM),
                        pltpu.roll(x, LANES - half, 1), pltpu.roll(x, half, 1))
        y = x * cos + rot * sin
        if c < N_PAIRS:
            y = y * (HEAD_DIM ** -0.5)
        o_ref[0, :, c * LANES:(c + 1) * LANES] = y.astype(o_ref.dtype)
    o_ref[0, :, 2 * SLAB:3 * SLAB] = v_ref[...]
    tm = x_ref.shape[0]
    for dil, perm_ref, f_ref in zip(FOLD_DILS, perm_refs, fold_refs):
        per = FOLD_CHUNK // dil
        for c in range(tm // FOLD_CHUNK):
            folded = _dot(perm_ref[...], o_ref[0, c * FOLD_CHUNK:(c + 1) * FOLD_CHUNK, :]).astype(f_ref.dtype)
            for p in range(dil):
                f_ref[p, c * per:(c + 1) * per, :] = folded[p * per:(p + 1) * per, :]


def rope_qkv(proj, b, t, tm):
    n = b * t
    cos, sin = rope_tables_pair(t)
    nb = t // tm
    fold_spec = lambda d: pl.BlockSpec((None, d, tm // d, 3 * SLAB), lambda i: (i // nb, 0, i % nb, 0))
    outs = pl.pallas_call(
        _rope_qkv_kernel,
        grid=(n // tm,),
        in_specs=[
            pl.BlockSpec((tm, 2 * SLAB), lambda i: (i, COL_DLQ // 2)),
            pl.BlockSpec((tm, SLAB), lambda i: (i, COL_DLV)),
            pl.BlockSpec((tm, LANES), lambda i: (i % nb, 0)),
            pl.BlockSpec((tm, LANES), lambda i: (i % nb, 0)),
        ] + [pl.BlockSpec((FOLD_CHUNK, FOLD_CHUNK), lambda i: (0, 0))] * len(FOLD_DILS),
        out_specs=[fold_spec(1)] + [fold_spec(d) for d in FOLD_DILS],
        out_shape=[jax.ShapeDtypeStruct((b, d, t // d, 3 * SLAB), BF16) for d in (1,) + FOLD_DILS],
        compiler_params=_cparams(("parallel",)),
        name="rope_qkv",
    )(proj, proj, cos, sin, *[fold_permutation(d) for d in FOLD_DILS])
    by_dil = dict(zip((1,) + FOLD_DILS, outs))
    return [by_dil[d] for _, d in DIL_PAIRS]


def _band_kernel(q_ref, k_ref, v_ref, o_ref, lse_ref, *, sub, half, span):
    tq = DIL_QBLOCK
    blocks = q_ref.shape[0] // tq
    for blk in range(blocks):
        qb = pl.program_id(2) * blocks + blk
        rows = slice(blk * tq, (blk + 1) * tq)
        start = jnp.clip(qb * tq - half, 0, sub - span)
        start = pl.multiple_of(start, half)
        q_pos = qb * tq + lax.broadcasted_iota(jnp.int32, (2 * tq, span), 0) % tq
        k_pos = start + lax.broadcasted_iota(jnp.int32, (2 * tq, span), 1)
        valid = jnp.abs(k_pos - q_pos) <= half
        pair_cols = [slice(pair * HEAD_PAIR, (pair + 1) * HEAD_PAIR) for pair in range(N_PAIRS)]
        scores = [jnp.where(valid, _pair_scores(q_ref[rows, cols], k_ref[pl.ds(start, span), cols]), NEG_INF)
                  for cols in pair_cols]
        for s, cols in zip(scores, pair_cols):
            o, lse = _pair_attend(s, v_ref[pl.ds(start, span), cols])
            o_ref[rows, cols] = o
            lse_ref[rows, cols] = lse


def band_attention(qkv, window, dil):
    b, _, sub, _ = qkv.shape
    half = window // (2 * dil)
    span = DIL_QBLOCK + 2 * half
    tq = DIL_QBLOCK * min(BAND_BLOCKS_PER_STEP, sub // DIL_QBLOCK)
    nqb = sub // tq
    return pl.pallas_call(
        functools.partial(_band_kernel, sub=sub, half=half, span=span),
        grid=(b, dil, nqb),
        in_specs=[
            pl.BlockSpec((None, None, tq, SLAB), lambda i, p, s: (i, p, s, 0)),
            pl.BlockSpec((None, None, sub, SLAB), lambda i, p, s: (i, p, 0, 1)),
            pl.BlockSpec((None, None, sub, SLAB), lambda i, p, s: (i, p, 0, 2)),
        ],
        out_specs=[pl.BlockSpec((None, None, tq, SLAB), lambda i, p, s: (i, p, s, 0))] * 2,
        out_shape=[jax.ShapeDtypeStruct((b, dil, sub, SLAB), F32)] * 2,
        compiler_params=_cparams(("parallel", "parallel", "arbitrary")),
        name="band_attention_d%d" % dil,
    )(qkv, qkv, qkv)


def _dil_combine_kernel(*refs):
    nbr = len(DIL_PAIRS)
    o_refs, l_refs, out_ref = refs[:nbr], refs[nbr:2 * nbr], refs[2 * nbr]
    scratch = iter(refs[2 * nbr + 1:])

    def token_order(ref):
        dil = ref.shape[0]
        if dil == 1:
            return ref[0]
        buf = next(scratch)
        per = ref.shape[1]
        for p in range(dil):
            for c in range(SLAB // LANES):
                buf[c, pl.ds(p, per, stride=dil), :] = ref[p, :, c * LANES:(c + 1) * LANES]
        return jnp.concatenate([buf[c] for c in range(SLAB // LANES)], axis=1)

    os = [token_order(r) for r in o_refs]
    lses = [token_order(r) for r in l_refs]
    m = functools.reduce(jnp.maximum, lses)
    ws = [jnp.exp(l - m) for l in lses]
    den = functools.reduce(jnp.add, ws)
    acc = functools.reduce(jnp.add, [(w / den) * o for w, o in zip(ws, os)])
    out_ref[...] = acc.astype(out_ref.dtype)


def dil_combine(outs, lses, tm):
    b, _, t, _ = outs[0].shape
    n = b * t
    nb = t // tm
    spec = lambda a: pl.BlockSpec((None, a.shape[1], tm // a.shape[1], SLAB), lambda i: (i // nb, 0, i % nb, 0))
    n_folded = sum(1 for a in outs + lses if a.shape[1] > 1)
    return pl.pallas_call(
        _dil_combine_kernel,
        grid=(n // tm,),
        in_specs=[spec(a) for a in outs + lses],
        out_specs=pl.BlockSpec((tm, SLAB), lambda i: (i, 0)),
        out_shape=jax.ShapeDtypeStruct((n, SLAB), BF16),
        scratch_shapes=[pltpu.VMEM((SLAB // LANES, tm, LANES), F32)] * n_folded,
        compiler_params=_cparams(("parallel",)),
        name="dil_combine",
    )(*outs, *lses)


MLA_QK = MLA_NOPE + MLA_ROPE


def mla_tables(t):
    ang = _rope_angles(t, MLA_ROPE)
    cos2 = np.concatenate([np.cos(ang), np.cos(ang)], axis=1)
    sin2 = np.concatenate([np.sin(ang), np.sin(ang)], axis=1)
    z = lambda w: np.zeros((t, w), np.float32)
    q_cos = np.concatenate([np.ones((t, MLA_NOPE), np.float32), cos2, z(LANES - MLA_QK)], axis=1)
    q_sin = np.concatenate([z(MLA_NOPE), sin2, z(LANES - MLA_QK)], axis=1)
    k_cos = np.concatenate([cos2, z(LANES - MLA_ROPE)], axis=1)
    k_sin = np.concatenate([-sin2[:, :MLA_ROPE // 2], sin2[:, MLA_ROPE // 2:], z(LANES - MLA_ROPE)], axis=1)
    return tuple(jnp.asarray(a, F32) for a in (q_cos, q_sin, k_cos, k_sin))


def mla_weights(w_uq, w_ukv):
    hq = w_uq.reshape(MLA_Q_RANK, MLA_HEADS, MLA_QK)
    nope, pe = hq[..., :MLA_NOPE], hq[..., MLA_NOPE:]
    pe_rot = jnp.concatenate([-pe[..., MLA_ROPE // 2:], pe[..., :MLA_ROPE // 2]], axis=-1)
    zq = jnp.zeros((MLA_Q_RANK, MLA_HEADS, LANES - MLA_QK), w_uq.dtype)
    w1 = jnp.concatenate([nope, pe, zq], axis=-1).reshape(MLA_Q_RANK, MLA_HEADS * LANES)
    w2 = jnp.concatenate([jnp.zeros_like(nope), pe_rot, zq], axis=-1).reshape(MLA_Q_RANK, MLA_HEADS * LANES)
    hkv = w_ukv.reshape(MLA_KV_RANK, MLA_HEADS, MLA_NOPE + MLA_V)
    k_nope, v = hkv[..., :MLA_NOPE], hkv[..., MLA_NOPE:]
    zk = jnp.zeros((MLA_KV_RANK, MLA_HEADS, LANES - MLA_NOPE), w_ukv.dtype)
    wk = jnp.concatenate([k_nope, zk], axis=-1).reshape(MLA_KV_RANK, MLA_HEADS * LANES)
    zv = jnp.zeros((MLA_KV_RANK, MLA_HEADS, LANES - MLA_V), w_ukv.dtype)
    wv = jnp.concatenate([v, zv], axis=-1).reshape(MLA_KV_RANK, MLA_HEADS * LANES)
    place = np.zeros((LANES, MLA_HEADS * LANES), np.float32)
    ones = np.zeros((1, MLA_HEADS * LANES), np.float32)
    for h in range(MLA_HEADS):
        place[np.arange(MLA_ROPE), h * LANES + MLA_NOPE + np.arange(MLA_ROPE)] = 1.0
        ones[0, h * LANES + MLA_V] = 1.0
    return (w1.astype(BF16), w2.astype(BF16), wk.astype(BF16), wv.astype(BF16), jnp.asarray(place, BF16),
            jnp.asarray(ones, F32))


def _mla_q_kernel(c_ref, nw_ref, w1_ref, w2_ref, cos_ref, sin_ref, o_ref):
    cn = _rms(c_ref[...].astype(F32), nw_ref[...]).astype(BF16)
    cos = jnp.tile(cos_ref[...], (1, MLA_HEADS))
    sin = jnp.tile(sin_ref[...], (1, MLA_HEADS))
    q = _dot(cn, w1_ref[...]) * cos + _dot(cn, w2_ref[...]) * sin
    o_ref[...] = (q * (MLA_QK ** -0.5 * math.log2(math.e))).astype(o_ref.dtype)


def _mla_kv_kernel(c_ref, tail_ref, nw_ref, wk_ref, wv_ref, place_ref, ones_ref, cos_ref, sin_ref,
                   k_ref, v_ref):
    cn = _rms(c_ref[...].astype(F32), nw_ref[...]).astype(BF16)
    kr = tail_ref[...]
    half = MLA_ROPE // 2
    rot = jnp.where(_lane_lt(kr.shape, half), pltpu.roll(kr, LANES - half, 1), pltpu.roll(kr, half, 1))
    k_pe = (kr * cos_ref[...] + rot * sin_ref[...]).astype(BF16)
    k_ref[...] = (_dot(cn, wk_ref[...]) + _dot(k_pe, place_ref[...])).astype(k_ref.dtype)
    v_ref[...] = (_dot(cn, wv_ref[...]) + ones_ref[...]).astype(v_ref.dtype)


def mla_project(proj, tail, q_norm_w, kv_norm_w, w_uq, w_ukv, b, t, tm):
    n = b * t
    nb = t // tm
    w1, w2, wk, wv, place, ones = mla_weights(w_uq, w_ukv)
    q_cos, q_sin, k_cos, k_sin = mla_tables(t)
    wide = MLA_HEADS * LANES
    full = lambda shape: pl.BlockSpec(shape, lambda i: (0, 0))
    tab = pl.BlockSpec((tm, LANES), lambda i: (i % nb, 0))
    qf = pl.pallas_call(
        _mla_q_kernel,
        grid=(n // tm,),
        in_specs=[pl.BlockSpec((tm, SLAB), lambda i: (i, COL_CQ)), full((1, MLA_Q_RANK)),
                  full((MLA_Q_RANK, wide)), full((MLA_Q_RANK, wide)), tab, tab],
        out_specs=pl.BlockSpec((tm, wide), lambda i: (i, 0)),
        out_shape=jax.ShapeDtypeStruct((n, wide), BF16),
        compiler_params=_cparams(("parallel",)),
        name="mla_q_proj",
    )(proj, q_norm_w.reshape(1, MLA_Q_RANK), w1, w2, q_cos, q_sin)
    kf, vf = pl.pallas_call(
        _mla_kv_kernel,
        grid=(n // tm,),
        in_specs=[pl.BlockSpec((tm, SLAB), lambda i: (i, COL_CKV)),
                  pl.BlockSpec((tm, LANES), lambda i: (i, 0)), full((1, MLA_KV_RANK)),
                  full((MLA_KV_RANK, wide)), full((MLA_KV_RANK, wide)), full((LANES, wide)), full((1, wide)),
                  tab, tab],
        out_specs=[pl.BlockSpec((tm, wide), lambda i: (i, 0))] * 2,
        out_shape=[jax.ShapeDtypeStruct((n, wide), BF16)] * 2,
        compiler_params=_cparams(("parallel",)),
        name="mla_kv_proj",
    )(proj, tail, kv_norm_w.reshape(1, MLA_KV_RANK), wk, wv, place, ones, k_cos, k_sin)
    return qf, kf, vf


def _mla_attn_kernel(q_ref, k_ref, v_ref, *rest, t, tk, n_cast):
    cast_in, o_ref, cast_out = rest[:n_cast], rest[n_cast], rest[n_cast + 1:]
    for src_ref, dst_ref in zip(cast_in, cast_out):
        dst_ref[...] = src_ref[...].astype(dst_ref.dtype)
    tq = q_ref.shape[0]
    groups = [slice(hh * LANES, (hh + 1) * LANES) for hh in range(2)]
    qs = [q_ref[:, grp] for grp in groups]

    def scores(c):
        return [_dot_nt(q, k_ref[c * tk:(c + 1) * tk, grp]) for q, grp in zip(qs, groups)]

    n_chunks = t // tk
    ms = [jnp.full((tq, 1), -jnp.inf, F32)] * 2
    accs = [jnp.zeros((tq, LANES), F32)] * 2
    s_next = scores(0)
    for c in range(n_chunks):
        s_cur = s_next
        if c + 1 < n_chunks:
            s_next = scores(c + 1)
        for hh, grp in enumerate(groups):
            m_new = jnp.maximum(ms[hh], jnp.max(s_cur[hh], axis=-1, keepdims=True))
            p = jnp.exp2((s_cur[hh] - m_new).astype(BF16))
            accs[hh] = jnp.exp2(ms[hh] - m_new) * accs[hh] + _dot(p, v_ref[c * tk:(c + 1) * tk, grp])
            ms[hh] = m_new
    outs = [acc / acc[:, MLA_V:MLA_V + 1] for acc in accs]
    first = _lane_lt((tq, LANES), MLA_V)
    o_ref[...] = jnp.where(first, outs[0], pltpu.roll(outs[1], MLA_V, 1)).astype(o_ref.dtype)


def mla_cast_rows(w, b, t, tq):
    steps = b * N_PAIRS * (t // tq)
    rows = int(np.prod(w.shape[:-1]))
    per = rows // steps
    return per if rows % steps == 0 and per % 16 == 0 else None


def mla_attention(qf, kf, vf, b, t, tq, tk, cast=()):
    n = b * t
    nq = t // tq
    step = lambda i, p, s: ((i * N_PAIRS + p) * nq + s, 0)
    cast2d = [w.reshape(-1, w.shape[-1]) for w in cast]
    cast_specs = [pl.BlockSpec((mla_cast_rows(w, b, t, tq), w2.shape[1]), step) for w, w2 in zip(cast, cast2d)]
    outs = pl.pallas_call(
        functools.partial(_mla_attn_kernel, t=t, tk=tk, n_cast=len(cast)),
        grid=(b, N_PAIRS, nq),
        in_specs=[
            pl.BlockSpec((tq, 2 * LANES), lambda i, p, s: (i * nq + s, p)),
            pl.BlockSpec((t, 2 * LANES), lambda i, p, s: (i, p)),
            pl.BlockSpec((t, 2 * LANES), lambda i, p, s: (i, p)),
        ] + cast_specs,
        out_specs=[pl.BlockSpec((tq, HEAD_PAIR), lambda i, p, s: (i * nq + s, p))] + cast_specs,
        out_shape=[jax.ShapeDtypeStruct((n, SLAB), BF16)]
        + [jax.ShapeDtypeStruct(w2.shape, BF16) for w2 in cast2d],
        compiler_params=_cparams(("parallel", "parallel", "arbitrary")),
        name="mla_attention",
    )(qf, kf, vf, *cast2d)
    return outs[0], [o.reshape(w.shape) for o, w in zip(outs[1:], cast)]


def _in_proj_segments():
    sizes = (SSM_INNER, SSM_CONV_CH, 2 * SSM_HEADS, SLAB, SLAB, SLAB, MLA_Q_RANK, MLA_KV_RANK, MLA_ROPE,
             SLAB, SLAB, SLAB)
    off = [int(v) for v in np.concatenate([[0], np.cumsum(sizes)])]
    main = ((off[0], off[2]), (off[3], off[8]), (off[9], off[12]))
    tail = ((off[8], off[9]), (off[2], off[3]))
    return main, tail


def _in_proj_columns():
    main, tail = _in_proj_segments()
    cols = lambda segs: np.concatenate([np.arange(a, b) for a, b in segs])
    return cols(main), cols(tail)


def in_proj_weights(w_in_l):
    main, tail = _in_proj_segments()
    w_main = jnp.concatenate([w_in_l[:, a:b] for a, b in main], axis=1).astype(BF16)
    pad = jnp.zeros((D_MODEL, LANES - sum(b - a for a, b in tail)), w_in_l.dtype)
    w_tail = jnp.concatenate([w_in_l[:, a:b] for a, b in tail] + [pad], axis=1).astype(BF16)
    return w_main, w_tail


MLA_TQ = 512
MLA_CASTS_PER_CALL = 2


def mixers(proj, tail, p, l, b, t, cast):
    xbc = conv_silu(proj, p["conv_w"][l], p["conv_b"][l], b, t)
    y_f, y_b = ssd_scan(xbc, tail, p["dt_bias"][l], p["a_log"][l], b, t)
    y_ssm = ssd_combine(y_f, y_b, xbc, proj, p["d_skip"][l], p["ssm_norm_w"][l], 1024)

    y_na = na_attention(proj, p["na_rpb"][l], b, t)

    qf, kf, vf = mla_project(proj, tail, p["mla_q_norm_w"][l], p["mla_kv_norm_w"][l],
                             p["mla_w_uq"][l], p["mla_w_ukv"][l], b, t, 512)
    y_mla, cast_out = mla_attention(qf, kf, vf, b, t, MLA_TQ, 512, cast)

    qkvs = rope_qkv(proj, b, t, 1024)
    outs, lses = zip(*[band_attention(qkv, w, d) for qkv, (w, d) in zip(qkvs, DIL_PAIRS)])
    y_dil = dil_combine(outs, lses, 1024)
    return (y_ssm, y_na, y_mla, y_dil), cast_out


def kernel(x, attn_norm_w, w_in, conv_w, conv_b, a_log, dt_bias, d_skip, ssm_norm_w, na_rpb,
           mla_q_norm_w, mla_kv_norm_w, mla_w_uq, mla_w_ukv, w_o, ffn_norm_w, ffn_w_gate, ffn_w_up,
           ffn_w_down, router_w, exp_w_gate, exp_w_up, exp_w_down, final_norm_w):
    b, t, _ = x.shape
    n = b * t
    depth = w_in.shape[0]
    p = dict(conv_w=conv_w, conv_b=conv_b, a_log=a_log, dt_bias=dt_bias, d_skip=d_skip,
             ssm_norm_w=ssm_norm_w, na_rpb=na_rpb, mla_q_norm_w=mla_q_norm_w,
             mla_kv_norm_w=mla_kv_norm_w, mla_w_uq=mla_w_uq, mla_w_ukv=mla_w_ukv)
    x = x.reshape(n, D_MODEL)
    cast_rows = 256
    w_o_b = cast_bf16(w_o, cast_rows)
    ffn_b = [cast_bf16(w, cast_rows) for w in (ffn_w_gate, ffn_w_up, ffn_w_down)]
    exp_f32 = [exp_w_gate, exp_w_up, exp_w_down]
    exp_b = [None] * len(exp_f32)
    pending = [k for k, w in enumerate(exp_f32) if mla_cast_rows(w, b, t, MLA_TQ) is not None]
    moe_tm = 512
    normed = False
    for l in range(depth):
        w_main, w_tail = in_proj_weights(w_in[l])
        proj, tail = in_proj(x, attn_norm_w[l], w_main, w_tail, 1024, PROJ_MAIN // 4)
        jobs, pending = pending[:MLA_CASTS_PER_CALL], pending[MLA_CASTS_PER_CALL:]
        mix, cast_out = mixers(proj, tail, p, l, b, t, [exp_f32[k] for k in jobs])
        for k, w_b in zip(jobs, cast_out):
            exp_b[k] = w_b
        x = out_proj(mix, w_o_b, l, x, 1024, 1024)
        j = l // 2
        if l % 2 == 0:
            x = ffn_dense(x, ffn_norm_w[l], *ffn_b, j, 1024, 512)
        else:
            pending = []
            exp_b = [cast_bf16(w, cast_rows) if w_b is None else w_b for w, w_b in zip(exp_f32, exp_b)]
            top_i, gates = moe_router(x, ffn_norm_w[l], router_w[j], 512)
            src, pos, tile_expert, tile_valid = moe_plan(top_i, moe_tm)
            y = moe_ffn(x, ffn_norm_w[l], *exp_b, j, src, tile_expert, tile_valid, moe_tm, 512)
            last = l == depth - 1
            x = moe_combine(x, gates, y, pos, final_norm_w if last else None, 256)
            normed = last
    if not normed:
        x = rmsnorm_rows(x, final_norm_w, 1024)
    return x.reshape(b, t, D_MODEL)
```

```python
import functools
import math

import numpy as np
import jax
import jax.numpy as jnp
from jax import lax
from jax.experimental import pallas as pl
from jax.experimental.pallas import tpu as pltpu

F32 = jnp.float32
BF16 = jnp.bfloat16

D_MODEL = 2048
GRID_W = 64
HEAD_DIM = 64
ROPE_THETA = 10000.0
NORM_EPS = 1e-6
NEG_INF = -1e30

SSM_HEADS = 8
SSM_HEAD_DIM = 64
SSM_INNER = SSM_HEADS * SSM_HEAD_DIM
SSM_GROUPS = 2
SSM_STATE = 128
SSM_CONV = 5
SSM_CHUNK = 128
SSM_CONV_CH = SSM_INNER + 2 * SSM_GROUPS * SSM_STATE

NA_HEADS = 8
NA_WIN_ROWS = 8
NA_WIN_COLS = 16
NA_COL_BLOCK = 16
NA_KEY_COLS = 32
NA_ROWS_PER_STEP = 4

MLA_HEADS = 8
MLA_Q_RANK = 512
MLA_KV_RANK = 512
MLA_NOPE = 64
MLA_ROPE = 32
MLA_V = 64

DIL_HEADS = 8
DIL_PAIRS = ((128, 1), (512, 4), (2048, 16))
DIL_QBLOCK = 128
BAND_BLOCKS_PER_STEP = 4

N_EXPERTS = 8
TOP_K = 2
SPLIT_PARTS = 3

LANES = 128
HEAD_PAIR = 2 * HEAD_DIM
N_PAIRS = 4
SLAB = 512

COL_Z, COL_XBC, COL_NAQ, COL_NAK, COL_NAV, COL_CQ, COL_CKV, COL_DLQ, COL_DLK, COL_DLV = (
    0, 1, 3, 4, 5, 6, 7, 8, 9, 10)
PROJ_MAIN = 11 * SLAB
TAIL_DT = 32

VMEM_LIMIT = 56 * 1024 * 1024


def _cparams(sem, vmem=VMEM_LIMIT):
    return pltpu.CompilerParams(dimension_semantics=sem, vmem_limit_bytes=vmem)


def _lane_lt(shape, bound, period=None):
    lane = lax.broadcasted_iota(jnp.int32, shape, len(shape) - 1)
    if period is not None:
        lane = lane % period
    return lane < bound


def _rms(x, w):
    ms = jnp.mean(x * x, axis=-1, keepdims=True)
    return x * lax.rsqrt(ms + NORM_EPS) * w


def _dot(a, b):
    return jnp.dot(a, b, preferred_element_type=F32)


def _dot_nt(a, b):
    return lax.dot_general(a, b, (((1,), (1,)), ((), ())), preferred_element_type=F32)


def _dot_tn(a, b):
    return lax.dot_general(a, b, (((0,), (0,)), ((), ())), preferred_element_type=F32)


def _cast_kernel(x_ref, o_ref):
    o_ref[...] = x_ref[...].astype(o_ref.dtype)


def cast_bf16(w, tr):
    shape = w.shape
    w2 = w.reshape(-1, shape[-1])
    r, c = w2.shape
    out = pl.pallas_call(
        _cast_kernel,
        grid=(r // tr,),
        in_specs=[pl.BlockSpec((tr, c), lambda i: (i, 0))],
        out_specs=pl.BlockSpec((tr, c), lambda i: (i, 0)),
        out_shape=jax.ShapeDtypeStruct((r, c), BF16),
        compiler_params=_cparams(("parallel",)),
        name="cast_bf16",
    )(w2)
    return out.reshape(shape)


def _in_proj_kernel(x_ref, nw_ref, w_ref, wt_ref, o_ref, t_ref, h_ref):
    @pl.when(pl.program_id(1) == 0)
    def _():
        h = _rms(x_ref[...], nw_ref[...]).astype(BF16)
        h_ref[...] = h
        t_ref[...] = _dot(h, wt_ref[...])

    o_ref[...] = _dot(h_ref[...], w_ref[...]).astype(o_ref.dtype)


def in_proj(x, nw, w_main, w_tail, tm, tn):
    n, k = x.shape
    nout = w_main.shape[1]
    return pl.pallas_call(
        _in_proj_kernel,
        grid=(n // tm, nout // tn),
        in_specs=[
            pl.BlockSpec((tm, k), lambda i, j: (i, 0)),
            pl.BlockSpec((1, k), lambda i, j: (0, 0)),
            pl.BlockSpec((k, tn), lambda i, j: (0, j)),
            pl.BlockSpec((k, LANES), lambda i, j: (0, 0)),
        ],
        out_specs=[pl.BlockSpec((tm, tn), lambda i, j: (i, j)), pl.BlockSpec((tm, LANES), lambda i, j: (i, 0))],
        out_shape=[jax.ShapeDtypeStruct((n, nout), BF16), jax.ShapeDtypeStruct((n, LANES), F32)],
        scratch_shapes=[pltpu.VMEM((tm, k), BF16)],
        compiler_params=_cparams(("parallel", "arbitrary")),
        name="in_proj",
    )(x, nw.reshape(1, k), w_main, w_tail)


def _out_proj_kernel(a0_ref, a1_ref, a2_ref, a3_ref, w_ref, r_ref, o_ref):
    acc = r_ref[...]
    for s, a_ref in enumerate((a0_ref, a1_ref, a2_ref, a3_ref)):
        acc = acc + _dot(a_ref[...], w_ref[s * SLAB:(s + 1) * SLAB, :])
    o_ref[...] = acc


def out_proj(mix, w, layer, res, tm, tn):
    n = res.shape[0]
    return pl.pallas_call(
        _out_proj_kernel,
        grid=(n // tm, D_MODEL // tn),
        in_specs=[pl.BlockSpec((tm, SLAB), lambda i, j: (i, 0))] * 4 + [
            pl.BlockSpec((None, 4 * SLAB, tn), lambda i, j: (layer, 0, j)),
            pl.BlockSpec((tm, tn), lambda i, j: (i, j)),
        ],
        out_specs=pl.BlockSpec((tm, tn), lambda i, j: (i, j)),
        out_shape=jax.ShapeDtypeStruct((n, D_MODEL), F32),
        compiler_params=_cparams(("parallel", "arbitrary")),
        name="out_proj",
    )(*mix, w, res)


def _ffn_kernel(x_ref, nw_ref, wg_ref, wu_ref, wd_ref, o_ref, h_ref):
    @pl.when(pl.program_id(1) == 0)
    def _():
        x = x_ref[...]
        h_ref[...] = _rms(x, nw_ref[...]).astype(BF16)
        o_ref[...] = x

    _swiglu_rows(h_ref, wg_ref, wu_ref, wd_ref, o_ref)


FFN_ROW_CHUNK = 512


def _swiglu_rows(h_ref, wg_ref, wu_ref, wd_ref, o_ref):
    tm = h_ref.shape[0]
    for r0 in range(0, tm, FFN_ROW_CHUNK):
        rows = slice(r0, r0 + FFN_ROW_CHUNK)
        h = h_ref[rows, :]
        g = _dot(h, wg_ref[...])
        u = _dot(h, wu_ref[...])
        a = (g * jax.nn.sigmoid(g) * u).astype(BF16)
        o_ref[rows, :] += _dot(a, wd_ref[...])


def ffn_dense(x, nw, wg, wu, wd, layer, tm, tf):
    n = x.shape[0]
    d_ff = wg.shape[-1]
    return pl.pallas_call(
        _ffn_kernel,
        grid=(n // tm, d_ff // tf),
        in_specs=[
            pl.BlockSpec((tm, D_MODEL), lambda i, j: (i, 0)),
            pl.BlockSpec((1, D_MODEL), lambda i, j: (0, 0)),
            pl.BlockSpec((None, D_MODEL, tf), lambda i, j: (layer, 0, j)),
            pl.BlockSpec((None, D_MODEL, tf), lambda i, j: (layer, 0, j)),
            pl.BlockSpec((None, tf, D_MODEL), lambda i, j: (layer, j, 0)),
        ],
        out_specs=pl.BlockSpec((tm, D_MODEL), lambda i, j: (i, 0)),
        out_shape=jax.ShapeDtypeStruct((n, D_MODEL), F32),
        scratch_shapes=[pltpu.VMEM((tm, D_MODEL), BF16)],
        compiler_params=_cparams(("parallel", "arbitrary")),
        name="ffn_dense",
    )(x, nw.reshape(1, D_MODEL), wg, wu, wd)


def _router_kernel(x_ref, nw_ref, rw_ref, idx_ref, gate_ref):
    h = _rms(x_ref[...], nw_ref[...])
    acc = jnp.zeros((h.shape[0], LANES), F32)
    rem = h
    for _ in range(SPLIT_PARTS):
        part = rem.astype(BF16)
        acc = acc + _dot(part, rw_ref[...])
        rem = rem - part.astype(F32)
    logits = acc
    for k in range(1, SPLIT_PARTS):
        logits = logits + pltpu.roll(acc, LANES - k * N_EXPERTS, 1)
    lane = lax.broadcasted_iota(jnp.int32, logits.shape, 1)
    logits = jnp.where(lane < N_EXPERTS, logits, -jnp.inf)
    m1 = jnp.max(logits, axis=-1, keepdims=True)
    i1 = jnp.min(jnp.where(logits == m1, lane, LANES), axis=-1, keepdims=True)
    rest = jnp.where(lane == i1, -jnp.inf, logits)
    m2 = jnp.max(rest, axis=-1, keepdims=True)
    i2 = jnp.min(jnp.where(rest == m2, lane, LANES), axis=-1, keepdims=True)
    e2 = jnp.exp(m2 - m1)
    g1 = 1.0 / (1.0 + e2)
    g2 = e2 / (1.0 + e2)
    idx_ref[...] = jnp.where(lane == 0, i1, i2)[:, :TOP_K]
    gate_ref[...] = jnp.where(lane == 0, g1, g2)[:, :TOP_K]


def moe_router(x, nw, router_w, tm):
    n = x.shape[0]
    parts, rem = [], router_w
    for _ in range(SPLIT_PARTS):
        parts.append(rem.astype(BF16))
        rem = rem - parts[-1].astype(F32)
    pad = jnp.zeros((D_MODEL, LANES - SPLIT_PARTS * N_EXPERTS), BF16)
    rw = jnp.concatenate(parts + [pad], axis=1)
    return pl.pallas_call(
        _router_kernel,
        grid=(n // tm,),
        in_specs=[
            pl.BlockSpec((tm, D_MODEL), lambda i: (i, 0)),
            pl.BlockSpec((1, D_MODEL), lambda i: (0, 0)),
            pl.BlockSpec((D_MODEL, LANES), lambda i: (0, 0)),
        ],
        out_specs=[pl.BlockSpec((tm, TOP_K), lambda i: (i, 0)),
                   pl.BlockSpec((tm, TOP_K), lambda i: (i, 0))],
        out_shape=[jax.ShapeDtypeStruct((n, TOP_K), jnp.int32),
                   jax.ShapeDtypeStruct((n, TOP_K), F32)],
        compiler_params=_cparams(("parallel",)),
        name="moe_router",
    )(x, nw.reshape(1, D_MODEL), rw)


def moe_plan(top_i, tm):
    n = top_i.shape[0]
    flat_e = top_i.reshape(-1)
    onehot = (flat_e[:, None] == jnp.arange(N_EXPERTS, dtype=jnp.int32)[None, :]).astype(jnp.int32)
    csum = jnp.cumsum(onehot, axis=0)
    counts = csum[-1]
    rank = jnp.sum(onehot * csum, axis=1) - 1
    padded = ((counts + tm - 1) // tm) * tm
    pend = jnp.cumsum(padded)
    pstart = pend - padded
    pos = pstart[flat_e] + rank
    n_slots = n * TOP_K + N_EXPERTS * tm
    n_tiles = n_slots // tm
    src = jnp.zeros((n_slots,), jnp.int32).at[pos].set(jnp.arange(n * TOP_K, dtype=jnp.int32) // TOP_K)
    tile_start = jnp.arange(n_tiles, dtype=jnp.int32) * tm
    tile_expert = jnp.sum((tile_start[:, None] >= pend[None, :]).astype(jnp.int32), axis=1)
    tile_valid = (tile_start < pend[-1]).astype(jnp.int32)
    last_valid = jnp.maximum(pend[-1] // tm - 1, 0)
    tile_expert = jnp.where(tile_valid == 1, tile_expert, tile_expert[last_valid]).astype(jnp.int32)
    return src, pos.reshape(n, TOP_K).astype(jnp.int32), tile_expert, tile_valid


def _moe_ffn_kernel(te_ref, tv_ref, src_ref, nsrc_ref, x_hbm, nw_ref, wg_ref, wu_ref, wd_ref, y_ref,
                    xbuf, h_ref, sem, *, tm, rows_per_step):
    i = pl.program_id(0)
    j = pl.program_id(1)
    issued = xbuf.shape[0]
    valid = tv_ref[i] == 1
    has_rows = jnp.logical_or(i == 0, tv_ref[jnp.maximum(i - 1, 0)] == 1)

    def start_row(idx_ref, row):
        tok = idx_ref[0, 0, jnp.minimum(row, tm - 1)]
        pltpu.make_async_copy(x_hbm.at[pl.ds(tok, 1)], xbuf.at[pl.ds(row, 1)], sem).start()

    @pl.when(j == 0)
    def _():
        y_ref[...] = jnp.zeros_like(y_ref)

    @pl.when(jnp.logical_and(j == 0, i == 0))
    def _():
        def start(r, c):
            start_row(src_ref, r)
            return c

        lax.fori_loop(0, issued, start, 0)

    @pl.when(jnp.logical_and(j == 0, has_rows))
    def _():
        pltpu.make_async_copy(x_hbm.at[pl.ds(0, issued)], xbuf.at[pl.ds(0, issued)], sem).wait()
        h_ref[...] = _rms(xbuf[0:tm, :], nw_ref[...]).astype(BF16)

    @pl.when(valid)
    def _():
        for r in range(rows_per_step):
            start_row(nsrc_ref, j * rows_per_step + r)
        _swiglu_rows(h_ref, wg_ref, wu_ref, wd_ref, y_ref)


def moe_ffn(x, nw, wg, wu, wd, layer, src, tile_expert, tile_valid, tm, tf):
    n_slots = src.shape[0]
    n_tiles = n_slots // tm + 1
    d_ff = wg.shape[-1]
    nf = d_ff // tf
    sublanes = 8
    rows_per_step = -(-tm // (nf * sublanes)) * sublanes
    buf_rows = rows_per_step * nf
    tile_expert = jnp.concatenate([tile_expert, tile_expert[-1:]])
    tile_valid = jnp.concatenate([tile_valid, jnp.zeros((1,), tile_valid.dtype)])
    src3 = jnp.concatenate([src, jnp.zeros((tm,), src.dtype)]).reshape(n_tiles, 1, tm)

    def wcol(i, j, te_ref, tv_ref):
        return (layer, te_ref[i], 0, jnp.where(tv_ref[i] == 1, j, nf - 1))

    def wrow(i, j, te_ref, tv_ref):
        return (layer, te_ref[i], jnp.where(tv_ref[i] == 1, j, nf - 1), 0)

    grid_spec = pltpu.PrefetchScalarGridSpec(
        num_scalar_prefetch=2,
        grid=(n_tiles, nf),
        in_specs=[
            pl.BlockSpec((1, 1, tm), lambda i, j, *_: (i, 0, 0), memory_space=pltpu.SMEM),
            pl.BlockSpec((1, 1, tm), lambda i, j, *_: (jnp.minimum(i + 1, n_tiles - 1), 0, 0),
                         memory_space=pltpu.SMEM),
            pl.BlockSpec(memory_space=pl.ANY),
            pl.BlockSpec((1, D_MODEL), lambda i, j, *_: (0, 0)),
            pl.BlockSpec((None, None, D_MODEL, tf), wcol),
            pl.BlockSpec((None, None, D_MODEL, tf), wcol),
            pl.BlockSpec((None, None, tf, D_MODEL), wrow),
        ],
        out_specs=pl.BlockSpec((tm, D_MODEL), lambda i, j, *_: (i, 0)),
        scratch_shapes=[pltpu.VMEM((buf_rows, D_MODEL), F32), pltpu.VMEM((tm, D_MODEL), BF16),
                        pltpu.SemaphoreType.DMA],
    )
    return pl.pallas_call(
        functools.partial(_moe_ffn_kernel, tm=tm, rows_per_step=rows_per_step),
        grid_spec=grid_spec,
        out_shape=jax.ShapeDtypeStruct((n_tiles * tm, D_MODEL), F32),
        compiler_params=_cparams(("arbitrary", "arbitrary")),
        name="moe_ffn",
    )(tile_expert, tile_valid, src3, src3, x, nw.reshape(1, D_MODEL), wg, wu, wd)


def _moe_combine_kernel(pos_ref, npos_ref, x_ref, gate_ref, y_hbm, fw_ref, o_ref, ybuf, sem, *, tm, final_norm):
    i = pl.program_id(0)
    n_tiles = pl.num_programs(0)
    cur = i % 2

    def row_copy(idx_ref, buf, r, k):
        slot = idx_ref[0, 0, r * TOP_K + k]
        return pltpu.make_async_copy(y_hbm.at[pl.ds(slot, 1)], ybuf.at[buf, k, pl.ds(r, 1)], sem.at[buf])

    def gather(idx_ref, buf):
        def start(r, c):
            for k in range(TOP_K):
                row_copy(idx_ref, buf, r, k).start()
            return c

        lax.fori_loop(0, tm, start, 0, unroll=8)

    @pl.when(i == 0)
    def _():
        gather(pos_ref, 0)

    @pl.when(i + 1 < n_tiles)
    def _():
        gather(npos_ref, 1 - cur)

    for k in range(TOP_K):
        pltpu.make_async_copy(y_hbm.at[pl.ds(0, tm)], ybuf.at[cur, k], sem.at[cur]).wait()
    gates = gate_ref[...]
    out = x_ref[...]
    for k in range(TOP_K):
        out = out + gates[:, k:k + 1] * ybuf[cur, k]
    if final_norm:
        out = _rms(out, fw_ref[...])
    o_ref[...] = out


def moe_combine(x, gates, y, pos, final_w, tm):
    n = x.shape[0]
    final_norm = final_w is not None
    fw = final_w if final_norm else jnp.ones((D_MODEL,), F32)
    n_tiles = n // tm
    pos3 = pos.reshape(n_tiles, 1, tm * TOP_K)
    return pl.pallas_call(
        functools.partial(_moe_combine_kernel, tm=tm, final_norm=final_norm),
        grid=(n_tiles,),
        in_specs=[
            pl.BlockSpec((1, 1, tm * TOP_K), lambda i: (i, 0, 0), memory_space=pltpu.SMEM),
            pl.BlockSpec((1, 1, tm * TOP_K), lambda i: (jnp.minimum(i + 1, n_tiles - 1), 0, 0),
                         memory_space=pltpu.SMEM),
            pl.BlockSpec((tm, D_MODEL), lambda i: (i, 0)),
            pl.BlockSpec((tm, TOP_K), lambda i: (i, 0)),
            pl.BlockSpec(memory_space=pl.ANY),
            pl.BlockSpec((1, D_MODEL), lambda i: (0, 0)),
        ],
        out_specs=pl.BlockSpec((tm, D_MODEL), lambda i: (i, 0)),
        out_shape=jax.ShapeDtypeStruct((n, D_MODEL), F32),
        scratch_shapes=[pltpu.VMEM((2, TOP_K, tm, D_MODEL), F32), pltpu.SemaphoreType.DMA((2,))],
        compiler_params=_cparams(("arbitrary",)),
        name="moe_combine",
    )(pos3, pos3, x, gates, y, fw.reshape(1, D_MODEL))


def _rmsnorm_kernel(x_ref, w_ref, o_ref):
    o_ref[...] = _rms(x_ref[...], w_ref[...])


def rmsnorm_rows(x, w, tm):
    n = x.shape[0]
    return pl.pallas_call(
        _rmsnorm_kernel,
        grid=(n // tm,),
        in_specs=[pl.BlockSpec((tm, D_MODEL), lambda i: (i, 0)),
                  pl.BlockSpec((1, D_MODEL), lambda i: (0, 0))],
        out_specs=pl.BlockSpec((tm, D_MODEL), lambda i: (i, 0)),
        out_shape=jax.ShapeDtypeStruct((n, D_MODEL), F32),
        compiler_params=_cparams(("parallel",)),
        name="final_norm",
    )(x, w.reshape(1, D_MODEL))


CONV_PAD = 8


def _conv_kernel(x_ref, w_ref, b_ref, o_ref, pad_ref, *, t):
    half = SSM_CONV // 2
    zeros = jnp.zeros((CONV_PAD, pad_ref.shape[1]), F32)
    pad_ref[0:CONV_PAD, :] = zeros
    pad_ref[CONV_PAD + t:CONV_PAD + t + CONV_PAD, :] = zeros
    pad_ref[CONV_PAD:CONV_PAD + t, :] = x_ref[...].astype(F32)
    acc = jnp.zeros(o_ref.shape, F32) + b_ref[...]
    for k in range(SSM_CONV):
        acc = acc + pad_ref[pl.ds(CONV_PAD - half + k, t), :] * w_ref[k:k + 1, :]
    o_ref[...] = (acc * jax.nn.sigmoid(acc)).astype(o_ref.dtype)


def conv_silu(proj, conv_w, conv_b, b, t):
    tc = 256
    nblk = SSM_CONV_CH // tc
    col0 = COL_XBC * SLAB // tc
    return pl.pallas_call(
        functools.partial(_conv_kernel, t=t),
        grid=(b, nblk),
        in_specs=[
            pl.BlockSpec((t, tc), lambda i, j: (i, col0 + j)),
            pl.BlockSpec((SSM_CONV, tc), lambda i, j: (0, j)),
            pl.BlockSpec((1, tc), lambda i, j: (0, j)),
        ],
        out_specs=pl.BlockSpec((t, tc), lambda i, j: (i, j)),
        out_shape=jax.ShapeDtypeStruct((b * t, SSM_CONV_CH), BF16),
        scratch_shapes=[pltpu.VMEM((t + 2 * CONV_PAD, tc), F32)],
        compiler_params=_cparams(("parallel", "parallel")),
        name="conv_silu",
    )(proj, conv_w, conv_b.reshape(1, SSM_CONV_CH))


def _ssd_stage1(xbc_ref, tail_ref, bias_ref, alog_ref, tri_ref, sel_ref, state_ref, direction):
    q = SSM_CHUNK
    dt = jax.nn.softplus(tail_ref[...] + bias_ref[...])
    da = dt * (-jnp.exp(alog_ref[...]))
    cs = jnp.dot(tri_ref[...], da, preferred_element_type=F32, precision=lax.Precision.HIGHEST)
    total = cs[q - 1:q, :]
    if direction == 0:
        e_out = cs
        e_in = total - cs
        e_seg = cs
    else:
        ex = cs - da
        e_out = total - ex
        e_in = ex
        e_seg = -ex
    dec_out_b = jnp.exp(e_out).astype(BF16)
    dec_in_dt_b = (jnp.exp(e_in) * dt).astype(BF16)
    pairs_per_group = N_PAIRS // SSM_GROUPS
    cbs, y_offs, in_scales = [], [], []
    for g in range(SSM_GROUPS):
        bm = xbc_ref[:, SSM_INNER + g * SSM_STATE:SSM_INNER + (g + 1) * SSM_STATE]
        cm = xbc_ref[:, SSM_INNER + (SSM_GROUPS + g) * SSM_STATE:SSM_INNER + (SSM_GROUPS + g + 1) * SSM_STATE]
        cbs.append(_dot_nt(cm, bm))
        for pair in range(g * pairs_per_group, (g + 1) * pairs_per_group):
            sel = sel_ref[direction, pair]
            y_offs.append(_dot(dec_out_b, sel) * _dot(cm, state_ref[pair].astype(BF16)))
            in_scales.append(_dot(dec_in_dt_b, sel))
    return dict(e_seg=e_seg, e_seg_t=jnp.transpose(e_seg), dt_t=jnp.transpose(dt), dec_tot=jnp.exp(total),
                cbs=cbs, y_offs=y_offs, in_scales=in_scales)


def _ssd_stage2(ctx, xbc_ref, y_ref, state_ref, direction):
    q = SSM_CHUNK
    row = lax.broadcasted_iota(jnp.int32, (q, q), 0)
    col = lax.broadcasted_iota(jnp.int32, (q, q), 1)
    keep = (row >= col) if direction == 0 else (col >= row)
    first_half = _lane_lt((q, HEAD_PAIR), HEAD_DIM)
    pairs_per_group = N_PAIRS // SSM_GROUPS
    for pair in range(N_PAIRS):
        g = pair // pairs_per_group
        bm = xbc_ref[:, SSM_INNER + g * SSM_STATE:SSM_INNER + (g + 1) * SSM_STATE]
        xs_pair = xbc_ref[:, pair * HEAD_PAIR:(pair + 1) * HEAD_PAIR]
        ys, decs = [], []
        for hh in range(2):
            lane = TAIL_DT + direction * SSM_HEADS + pair * 2 + hh
            seg = ctx["e_seg"][:, lane:lane + 1] - ctx["e_seg_t"][lane:lane + 1, :]
            lmat = jnp.where(keep, jnp.exp(seg), 0.0)
            w = (ctx["cbs"][g] * lmat * ctx["dt_t"][lane:lane + 1, :]).astype(BF16)
            ys.append(_dot(w, xs_pair))
            decs.append(ctx["dec_tot"][:, lane:lane + 1])
        y_ref[:, pair * HEAD_PAIR:(pair + 1) * HEAD_PAIR] = jnp.where(first_half, ys[0], ys[1]) + ctx["y_offs"][pair]
        st = state_ref[pair]
        first_half_s = _lane_lt(st.shape, HEAD_DIM)
        state_ref[pair] = (st * jnp.where(first_half_s, decs[0], decs[1])
                           + _dot_tn(bm, (xs_pair * ctx["in_scales"][pair]).astype(BF16)))


def _ssd_kernel(xbc_f_ref, tail_f_ref, xbc_b_ref, tail_b_ref, bias_ref, alog_ref, tri_ref, sel_ref,
                y_f_ref, y_b_ref, state_ref):
    @pl.when(pl.program_id(1) == 0)
    def _():
        state_ref[...] = jnp.zeros_like(state_ref)

    ins = ((xbc_f_ref, tail_f_ref, y_f_ref), (xbc_b_ref, tail_b_ref, y_b_ref))
    ctxs = [_ssd_stage1(xbc_ref, tail_ref, bias_ref, alog_ref, tri_ref, sel_ref, state_ref.at[d], d)
            for d, (xbc_ref, tail_ref, _) in enumerate(ins)]
    for d, (xbc_ref, _, y_ref) in enumerate(ins):
        _ssd_stage2(ctxs[d], xbc_ref, y_ref, state_ref.at[d], d)


def _ssd_lane_selectors():
    sel = np.zeros((2, N_PAIRS, LANES, HEAD_PAIR), np.float32)
    for d in range(2):
        for pair in range(N_PAIRS):
            for hh in range(2):
                sel[d, pair, TAIL_DT + d * SSM_HEADS + pair * 2 + hh, hh * HEAD_DIM:(hh + 1) * HEAD_DIM] = 1.0
    return jnp.asarray(sel, BF16)


def ssd_scan(xbc, tail, dt_bias, a_log, b, t):
    q = SSM_CHUNK
    nc = t // q
    bias_row = jnp.zeros((1, LANES), F32).at[0, TAIL_DT:TAIL_DT + 2 * SSM_HEADS].set(dt_bias.reshape(-1))
    alog_row = jnp.zeros((1, LANES), F32).at[0, TAIL_DT:TAIL_DT + 2 * SSM_HEADS].set(a_log.reshape(-1))
    tri = jnp.asarray(np.tril(np.ones((q, q), np.float32)))
    fwd = lambda i, c: (i * nc + c, 0)
    bwd = lambda i, c: (i * nc + nc - 1 - c, 0)
    const = lambda i, c: (0, 0)
    return pl.pallas_call(
        _ssd_kernel,
        grid=(b, nc),
        in_specs=[
            pl.BlockSpec((q, SSM_CONV_CH), fwd), pl.BlockSpec((q, LANES), fwd),
            pl.BlockSpec((q, SSM_CONV_CH), bwd), pl.BlockSpec((q, LANES), bwd),
            pl.BlockSpec((1, LANES), const), pl.BlockSpec((1, LANES), const), pl.BlockSpec((q, q), const),
            pl.BlockSpec((2, N_PAIRS, LANES, HEAD_PAIR), lambda i, c: (0, 0, 0, 0)),
        ],
        out_specs=[pl.BlockSpec((q, SSM_INNER), fwd), pl.BlockSpec((q, SSM_INNER), bwd)],
        out_shape=[jax.ShapeDtypeStruct((b * t, SSM_INNER), F32)] * 2,
        scratch_shapes=[pltpu.VMEM((2, N_PAIRS, SSM_STATE, HEAD_PAIR), F32)],
        compiler_params=_cparams(("parallel", "arbitrary")),
        name="ssd_scan",
    )(xbc, tail, xbc, tail, bias_row, alog_row, tri, _ssd_lane_selectors())


def _ssd_combine_kernel(yf_ref, yb_ref, xs_ref, z_ref, d_ref, nw_ref, o_ref):
    y = yf_ref[...] + yb_ref[...] + xs_ref[...].astype(F32) * d_ref[...]
    z = z_ref[...].astype(F32)
    o_ref[...] = _rms(y * (z * jax.nn.sigmoid(z)), nw_ref[...]).astype(o_ref.dtype)


def ssd_combine(y_f, y_b, xbc, proj, d_skip, norm_w, tm):
    n = y_f.shape[0]
    d_row = jnp.repeat(d_skip, SSM_HEAD_DIM).reshape(1, SSM_INNER)
    row = lambda i: (i, 0)
    return pl.pallas_call(
        _ssd_combine_kernel,
        grid=(n // tm,),
        in_specs=[
            pl.BlockSpec((tm, SSM_INNER), row),
            pl.BlockSpec((tm, SSM_INNER), row),
            pl.BlockSpec((tm, SSM_INNER), row),
            pl.BlockSpec((tm, SLAB), lambda i: (i, COL_Z)),
            pl.BlockSpec((1, SSM_INNER), lambda i: (0, 0)),
            pl.BlockSpec((1, SSM_INNER), lambda i: (0, 0)),
        ],
        out_specs=pl.BlockSpec((tm, SSM_INNER), row),
        out_shape=jax.ShapeDtypeStruct((n, SSM_INNER), BF16),
        compiler_params=_cparams(("parallel",)),
        name="ssd_combine",
    )(y_f, y_b, xbc, proj, d_row, norm_w.reshape(1, SSM_INNER))


def _pair_scores(q2, k2):
    first_q = _lane_lt(q2.shape, HEAD_DIM)
    zero = jnp.zeros_like(q2)
    qs = jnp.concatenate([jnp.where(first_q, q2, zero), jnp.where(first_q, zero, q2)], axis=0)
    return _dot_nt(qs, k2)


def _pair_attend(s, v2):
    tq = s.shape[0] // 2
    m = jnp.max(s, axis=-1, keepdims=True)
    p = jnp.exp(s - m)
    l = jnp.sum(p, axis=-1, keepdims=True)
    o = _dot(p.astype(BF16), v2) / l
    lse = m + jnp.log(l)
    first_o = _lane_lt((tq, HEAD_PAIR), HEAD_DIM)
    return jnp.where(first_o, o[:tq], o[tq:]), jnp.where(first_o, lse[:tq], lse[tq:])


def na_bias_tables(rpb, rows):
    kr = min(NA_WIN_ROWS, rows)
    qc = np.arange(GRID_W)
    kc = np.arange(GRID_W)
    q_start = np.clip(qc - NA_WIN_COLS // 2, 0, GRID_W - NA_WIN_COLS)
    col_in = (kc[None, :] >= q_start[:, None]) & (kc[None, :] < q_start[:, None] + NA_WIN_COLS)
    col_off = np.clip(kc[None, :] - qc[:, None] + NA_WIN_COLS - 1, 0, 2 * NA_WIN_COLS - 2)
    onehot = (col_off[None] == np.arange(2 * NA_WIN_COLS - 1)[:, None, None]).astype(np.float32)
    expanded = jnp.einsum("hrc,cqk->hqrk", rpb, jnp.asarray(onehot), precision=lax.Precision.HIGHEST)
    expanded = jnp.where(jnp.asarray(col_in)[None, :, None, :], expanded, NEG_INF)

    def table(r):
        row_start = int(np.clip(r - kr // 2, 0, rows - kr))
        ro0 = row_start - r + NA_WIN_ROWS - 1
        return expanded[:, :, ro0:ro0 + kr, :].reshape(N_PAIRS, 2 * GRID_W, kr * GRID_W)

    rs = NA_ROWS_PER_STEP
    lo = [table(r) for r in range(rs)]
    mid = [table(min(rs, rows - 1))] * rs
    hi = [table(r) for r in range(rows - rs, rows)]
    return jnp.stack([jnp.stack(lo), jnp.stack(mid), jnp.stack(hi)])


def _na_kernel(q_ref, k_ref, v_ref, bias_ref, o_ref, *, rows, kr):
    step = pl.program_id(1)
    rs = NA_ROWS_PER_STEP
    for rr in range(rs):
        r = step * rs + rr
        row_start = jnp.clip(r - kr // 2, 0, rows - kr)
        k0 = pl.multiple_of(row_start * GRID_W, GRID_W)
        pair_cols = [slice(pair * HEAD_PAIR, (pair + 1) * HEAD_PAIR) for pair in range(N_PAIRS)]
        scores = []
        for pair, cols in enumerate(pair_cols):
            q2 = q_ref[rr * GRID_W:(rr + 1) * GRID_W, cols] * jnp.asarray(HEAD_DIM ** -0.5, BF16)
            scores.append(_pair_scores(q2, k_ref[pl.ds(k0, kr * GRID_W), cols]) + bias_ref[0, rr, pair])
        for s, cols in zip(scores, pair_cols):
            o, _ = _pair_attend(s, v_ref[pl.ds(k0, kr * GRID_W), cols])
            o_ref[rr * GRID_W:(rr + 1) * GRID_W, cols] = o.astype(o_ref.dtype)


def na_attention(proj, rpb, b, t):
    rows = t // GRID_W
    kr = min(NA_WIN_ROWS, rows)
    rs = NA_ROWS_PER_STEP
    nsteps = rows // rs
    bias = na_bias_tables(rpb, rows)

    def kind(i, s):
        return jnp.where(s == 0, 0, jnp.where(s == nsteps - 1, 2, 1))

    return pl.pallas_call(
        functools.partial(_na_kernel, rows=rows, kr=kr),
        grid=(b, nsteps),
        in_specs=[
            pl.BlockSpec((rs * GRID_W, SLAB), lambda i, s: (i * nsteps + s, COL_NAQ)),
            pl.BlockSpec((t, SLAB), lambda i, s: (i, COL_NAK)),
            pl.BlockSpec((t, SLAB), lambda i, s: (i, COL_NAV)),
            pl.BlockSpec((1, rs, N_PAIRS, 2 * GRID_W, kr * GRID_W), lambda i, s: (kind(i, s), 0, 0, 0, 0)),
        ],
        out_specs=pl.BlockSpec((rs * GRID_W, SLAB), lambda i, s: (i * nsteps + s, 0)),
        out_shape=jax.ShapeDtypeStruct((b * t, SLAB), BF16),
        compiler_params=_cparams(("parallel", "arbitrary")),
        name="na_attention",
    )(proj, proj, proj, bias)


def _rope_angles(t, d):
    inv = ROPE_THETA ** (-np.arange(0, d, 2, dtype=np.float32) / d)
    return np.arange(t, dtype=np.float32)[:, None] * inv[None, :]


def rope_tables_pair(t):
    ang = _rope_angles(t, HEAD_DIM)
    cos = np.tile(np.cos(ang), (1, 4))
    sin = np.tile(np.concatenate([-np.sin(ang), np.sin(ang)], axis=1), (1, 2))
    return jnp.asarray(cos, F32), jnp.asarray(sin, F32)


FOLD_CHUNK = 256
FOLD_DILS = tuple(d for _, d in DIL_PAIRS if d > 1)


def fold_permutation(dil):
    per = FOLD_CHUNK // dil
    perm = np.zeros((FOLD_CHUNK, FOLD_CHUNK), np.float32)
    dst = np.arange(FOLD_CHUNK)
    perm[dst, (dst % per) * dil + dst // per] = 1.0
    return jnp.asarray(perm, BF16)


def _rope_qkv_kernel(x_ref, v_ref, cos_ref, sin_ref, *rest):
    nd = len(FOLD_DILS)
    perm_refs, o_ref, fold_refs = rest[:nd], rest[nd], rest[nd + 1:]
    cos = cos_ref[...]
    sin = sin_ref[...]
    half = HEAD_DIM // 2
    for c in range(x_ref.shape[1] // LANES):
        x = x_ref[:, c * LANES:(c + 1) * LANES].astype(F32)
        rot = jnp.where(_lane_lt(x.shape, half, HEAD_DIM),
                        pltpu.roll(x, LANES - half, 1), pltpu.roll(x, half, 1))
        y = x * cos + rot * sin
        if c < N_PAIRS:
            y = y * (HEAD_DIM ** -0.5)
        o_ref[0, :, c * LANES:(c + 1) * LANES] = y.astype(o_ref.dtype)
    o_ref[0, :, 2 * SLAB:3 * SLAB] = v_ref[...]
    tm = x_ref.shape[0]
    for dil, perm_ref, f_ref in zip(FOLD_DILS, perm_refs, fold_refs):
        per = FOLD_CHUNK // dil
        for c in range(tm // FOLD_CHUNK):
            folded = _dot(perm_ref[...], o_ref[0, c * FOLD_CHUNK:(c + 1) * FOLD_CHUNK, :]).astype(f_ref.dtype)
            for p in range(dil):
                f_ref[p, c * per:(c + 1) * per, :] = folded[p * per:(p + 1) * per, :]


def rope_qkv(proj, b, t, tm):
    n = b * t
    cos, sin = rope_tables_pair(t)
    nb = t // tm
    fold_spec = lambda d: pl.BlockSpec((None, d, tm // d, 3 * SLAB), lambda i: (i // nb, 0, i % nb, 0))
    outs = pl.pallas_call(
        _rope_qkv_kernel,
        grid=(n // tm,),
        in_specs=[
            pl.BlockSpec((tm, 2 * SLAB), lambda i: (i, COL_DLQ // 2)),
            pl.BlockSpec((tm, SLAB), lambda i: (i, COL_DLV)),
            pl.BlockSpec((tm, LANES), lambda i: (i % nb, 0)),
            pl.BlockSpec((tm, LANES), lambda i: (i % nb, 0)),
        ] + [pl.BlockSpec((FOLD_CHUNK, FOLD_CHUNK), lambda i: (0, 0))] * len(FOLD_DILS),
        out_specs=[fold_spec(1)] + [fold_spec(d) for d in FOLD_DILS],
        out_shape=[jax.ShapeDtypeStruct((b, d, t // d, 3 * SLAB), BF16) for d in (1,) + FOLD_DILS],
        compiler_params=_cparams(("parallel",)),
        name="rope_qkv",
    )(proj, proj, cos, sin, *[fold_permutation(d) for d in FOLD_DILS])
    by_dil = dict(zip((1,) + FOLD_DILS, outs))
    return [by_dil[d] for _, d in DIL_PAIRS]


def _band_kernel(q_ref, k_ref, v_ref, o_ref, lse_ref, *, sub, half, span):
    tq = DIL_QBLOCK
    blocks = q_ref.shape[0] // tq
    for blk in range(blocks):
        qb = pl.program_id(2) * blocks + blk
        rows = slice(blk * tq, (blk + 1) * tq)
        start = jnp.clip(qb * tq - half, 0, sub - span)
        start = pl.multiple_of(start, half)
        q_pos = qb * tq + lax.broadcasted_iota(jnp.int32, (2 * tq, span), 0) % tq
        k_pos = start + lax.broadcasted_iota(jnp.int32, (2 * tq, span), 1)
        valid = jnp.abs(k_pos - q_pos) <= half
        pair_cols = [slice(pair * HEAD_PAIR, (pair + 1) * HEAD_PAIR) for pair in range(N_PAIRS)]
        scores = [jnp.where(valid, _pair_scores(q_ref[rows, cols], k_ref[pl.ds(start, span), cols]), NEG_INF)
                  for cols in pair_cols]
        for s, cols in zip(scores, pair_cols):
            o, lse = _pair_attend(s, v_ref[pl.ds(start, span), cols])
            o_ref[rows, cols] = o
            lse_ref[rows, cols] = lse


def band_attention(qkv, window, dil):
    b, _, sub, _ = qkv.shape
    half = window // (2 * dil)
    span = DIL_QBLOCK + 2 * half
    tq = DIL_QBLOCK * min(BAND_BLOCKS_PER_STEP, sub // DIL_QBLOCK)
    nqb = sub // tq
    return pl.pallas_call(
        functools.partial(_band_kernel, sub=sub, half=half, span=span),
        grid=(b, dil, nqb),
        in_specs=[
            pl.BlockSpec((None, None, tq, SLAB), lambda i, p, s: (i, p, s, 0)),
            pl.BlockSpec((None, None, sub, SLAB), lambda i, p, s: (i, p, 0, 1)),
            pl.BlockSpec((None, None, sub, SLAB), lambda i, p, s: (i, p, 0, 2)),
        ],
        out_specs=[pl.BlockSpec((None, None, tq, SLAB), lambda i, p, s: (i, p, s, 0))] * 2,
        out_shape=[jax.ShapeDtypeStruct((b, dil, sub, SLAB), F32)] * 2,
        compiler_params=_cparams(("parallel", "parallel", "arbitrary")),
        name="band_attention_d%d" % dil,
    )(qkv, qkv, qkv)


def _dil_combine_kernel(*refs):
    nbr = len(DIL_PAIRS)
    o_refs, l_refs, out_ref = refs[:nbr], refs[nbr:2 * nbr], refs[2 * nbr]
    scratch = iter(refs[2 * nbr + 1:])

    def token_order(ref):
        dil = ref.shape[0]
        if dil == 1:
            return ref[0]
        buf = next(scratch)
        per = ref.shape[1]
        for p in range(dil):
            for c in range(SLAB // LANES):
                buf[c, pl.ds(p, per, stride=dil), :] = ref[p, :, c * LANES:(c + 1) * LANES]
        return jnp.concatenate([buf[c] for c in range(SLAB // LANES)], axis=1)

    os = [token_order(r) for r in o_refs]
    lses = [token_order(r) for r in l_refs]
    m = functools.reduce(jnp.maximum, lses)
    ws = [jnp.exp(l - m) for l in lses]
    den = functools.reduce(jnp.add, ws)
    acc = functools.reduce(jnp.add, [(w / den) * o for w, o in zip(ws, os)])
    out_ref[...] = acc.astype(out_ref.dtype)


def dil_combine(outs, lses, tm):
    b, _, t, _ = outs[0].shape
    n = b * t
    nb = t // tm
    spec = lambda a: pl.BlockSpec((None, a.shape[1], tm // a.shape[1], SLAB), lambda i: (i // nb, 0, i % nb, 0))
    n_folded = sum(1 for a in outs + lses if a.shape[1] > 1)
    return pl.pallas_call(
        _dil_combine_kernel,
        grid=(n // tm,),
        in_specs=[spec(a) for a in outs + lses],
        out_specs=pl.BlockSpec((tm, SLAB), lambda i: (i, 0)),
        out_shape=jax.ShapeDtypeStruct((n, SLAB), BF16),
        scratch_shapes=[pltpu.VMEM((SLAB // LANES, tm, LANES), F32)] * n_folded,
        compiler_params=_cparams(("parallel",)),
        name="dil_combine",
    )(*outs, *lses)


MLA_QK = MLA_NOPE + MLA_ROPE


def mla_tables(t):
    ang = _rope_angles(t, MLA_ROPE)
    cos2 = np.concatenate([np.cos(ang), np.cos(ang)], axis=1)
    sin2 = np.concatenate([np.sin(ang), np.sin(ang)], axis=1)
    z = lambda w: np.zeros((t, w), np.float32)
    q_cos = np.concatenate([np.ones((t, MLA_NOPE), np.float32), cos2, z(LANES - MLA_QK)], axis=1)
    q_sin = np.concatenate([z(MLA_NOPE), sin2, z(LANES - MLA_QK)], axis=1)
    k_cos = np.concatenate([cos2, z(LANES - MLA_ROPE)], axis=1)
    k_sin = np.concatenate([-sin2[:, :MLA_ROPE // 2], sin2[:, MLA_ROPE // 2:], z(LANES - MLA_ROPE)], axis=1)
    return tuple(jnp.asarray(a, F32) for a in (q_cos, q_sin, k_cos, k_sin))


def mla_weights(w_uq, w_ukv):
    hq = w_uq.reshape(MLA_Q_RANK, MLA_HEADS, MLA_QK)
    nope, pe = hq[..., :MLA_NOPE], hq[..., MLA_NOPE:]
    pe_rot = jnp.concatenate([-pe[..., MLA_ROPE // 2:], pe[..., :MLA_ROPE // 2]], axis=-1)
    zq = jnp.zeros((MLA_Q_RANK, MLA_HEADS, LANES - MLA_QK), w_uq.dtype)
    w1 = jnp.concatenate([nope, pe, zq], axis=-1).reshape(MLA_Q_RANK, MLA_HEADS * LANES)
    w2 = jnp.concatenate([jnp.zeros_like(nope), pe_rot, zq], axis=-1).reshape(MLA_Q_RANK, MLA_HEADS * LANES)
    hkv = w_ukv.reshape(MLA_KV_RANK, MLA_HEADS, MLA_NOPE + MLA_V)
    k_nope, v = hkv[..., :MLA_NOPE], hkv[..., MLA_NOPE:]
    zk = jnp.zeros((MLA_KV_RANK, MLA_HEADS, LANES - MLA_NOPE), w_ukv.dtype)
    wk = jnp.concatenate([k_nope, zk], axis=-1).reshape(MLA_KV_RANK, MLA_HEADS * LANES)
    zv = jnp.zeros((MLA_KV_RANK, MLA_HEADS, LANES - MLA_V), w_ukv.dtype)
    wv = jnp.concatenate([v, zv], axis=-1).reshape(MLA_KV_RANK, MLA_HEADS * LANES)
    place = np.zeros((LANES, MLA_HEADS * LANES), np.float32)
    ones = np.zeros((1, MLA_HEADS * LANES), np.float32)
    for h in range(MLA_HEADS):
        place[np.arange(MLA_ROPE), h * LANES + MLA_NOPE + np.arange(MLA_ROPE)] = 1.0
        ones[0, h * LANES + MLA_V] = 1.0
    return (w1.astype(BF16), w2.astype(BF16), wk.astype(BF16), wv.astype(BF16), jnp.asarray(place, BF16),
            jnp.asarray(ones, F32))


def _mla_q_kernel(c_ref, nw_ref, w1_ref, w2_ref, cos_ref, sin_ref, o_ref):
    cn = _rms(c_ref[...].astype(F32), nw_ref[...]).astype(BF16)
    cos = jnp.tile(cos_ref[...], (1, MLA_HEADS))
    sin = jnp.tile(sin_ref[...], (1, MLA_HEADS))
    q = _dot(cn, w1_ref[...]) * cos + _dot(cn, w2_ref[...]) * sin
    o_ref[...] = (q * (MLA_QK ** -0.5 * math.log2(math.e))).astype(o_ref.dtype)


def _mla_kv_kernel(c_ref, tail_ref, nw_ref, wk_ref, wv_ref, place_ref, ones_ref, cos_ref, sin_ref,
                   k_ref, v_ref):
    cn = _rms(c_ref[...].astype(F32), nw_ref[...]).astype(BF16)
    kr = tail_ref[...]
    half = MLA_ROPE // 2
    rot = jnp.where(_lane_lt(kr.shape, half), pltpu.roll(kr, LANES - half, 1), pltpu.roll(kr, half, 1))
    k_pe = (kr * cos_ref[...] + rot * sin_ref[...]).astype(BF16)
    k_ref[...] = (_dot(cn, wk_ref[...]) + _dot(k_pe, place_ref[...])).astype(k_ref.dtype)
    v_ref[...] = (_dot(cn, wv_ref[...]) + ones_ref[...]).astype(v_ref.dtype)


def mla_project(proj, tail, q_norm_w, kv_norm_w, w_uq, w_ukv, b, t, tm):
    n = b * t
    nb = t // tm
    w1, w2, wk, wv, place, ones = mla_weights(w_uq, w_ukv)
    q_cos, q_sin, k_cos, k_sin = mla_tables(t)
    wide = MLA_HEADS * LANES
    full = lambda shape: pl.BlockSpec(shape, lambda i: (0, 0))
    tab = pl.BlockSpec((tm, LANES), lambda i: (i % nb, 0))
    qf = pl.pallas_call(
        _mla_q_kernel,
        grid=(n // tm,),
        in_specs=[pl.BlockSpec((tm, SLAB), lambda i: (i, COL_CQ)), full((1, MLA_Q_RANK)),
                  full((MLA_Q_RANK, wide)), full((MLA_Q_RANK, wide)), tab, tab],
        out_specs=pl.BlockSpec((tm, wide), lambda i: (i, 0)),
        out_shape=jax.ShapeDtypeStruct((n, wide), BF16),
        compiler_params=_cparams(("parallel",)),
        name="mla_q_proj",
    )(proj, q_norm_w.reshape(1, MLA_Q_RANK), w1, w2, q_cos, q_sin)
    kf, vf = pl.pallas_call(
        _mla_kv_kernel,
        grid=(n // tm,),
        in_specs=[pl.BlockSpec((tm, SLAB), lambda i: (i, COL_CKV)),
                  pl.BlockSpec((tm, LANES), lambda i: (i, 0)), full((1, MLA_KV_RANK)),
                  full((MLA_KV_RANK, wide)), full((MLA_KV_RANK, wide)), full((LANES, wide)), full((1, wide)),
                  tab, tab],
        out_specs=[pl.BlockSpec((tm, wide), lambda i: (i, 0))] * 2,
        out_shape=[jax.ShapeDtypeStruct((n, wide), BF16)] * 2,
        compiler_params=_cparams(("parallel",)),
        name="mla_kv_proj",
    )(proj, tail, kv_norm_w.reshape(1, MLA_KV_RANK), wk, wv, place, ones, k_cos, k_sin)
    return qf, kf, vf


def _mla_attn_kernel(q_ref, k_ref, v_ref, *rest, t, tk, n_cast):
    cast_in, o_ref, cast_out = rest[:n_cast], rest[n_cast], rest[n_cast + 1:]
    for src_ref, dst_ref in zip(cast_in, cast_out):
        dst_ref[...] = src_ref[...].astype(dst_ref.dtype)
    tq = q_ref.shape[0]
    groups = [slice(hh * LANES, (hh + 1) * LANES) for hh in range(2)]
    qs = [q_ref[:, grp] for grp in groups]

    def scores(c):
        return [_dot_nt(q, k_ref[c * tk:(c + 1) * tk, grp]) for q, grp in zip(qs, groups)]

    n_chunks = t // tk
    ms = [jnp.full((tq, 1), -jnp.inf, F32)] * 2
    accs = [jnp.zeros((tq, LANES), F32)] * 2
    s_next = scores(0)
    for c in range(n_chunks):
        s_cur = s_next
        if c + 1 < n_chunks:
            s_next = scores(c + 1)
        for hh, grp in enumerate(groups):
            m_new = jnp.maximum(ms[hh], jnp.max(s_cur[hh], axis=-1, keepdims=True))
            p = jnp.exp2((s_cur[hh] - m_new).astype(BF16))
            accs[hh] = jnp.exp2(ms[hh] - m_new) * accs[hh] + _dot(p, v_ref[c * tk:(c + 1) * tk, grp])
            ms[hh] = m_new
    outs = [acc / acc[:, MLA_V:MLA_V + 1] for acc in accs]
    first = _lane_lt((tq, LANES), MLA_V)
    o_ref[...] = jnp.where(first, outs[0], pltpu.roll(outs[1], MLA_V, 1)).astype(o_ref.dtype)


def mla_cast_rows(w, b, t, tq):
    steps = b * N_PAIRS * (t // tq)
    rows = int(np.prod(w.shape[:-1]))
    per = rows // steps
    return per if rows % steps == 0 and per % 16 == 0 else None


def mla_attention(qf, kf, vf, b, t, tq, tk, cast=()):
    n = b * t
    nq = t // tq
    step = lambda i, p, s: ((i * N_PAIRS + p) * nq + s, 0)
    cast2d = [w.reshape(-1, w.shape[-1]) for w in cast]
    cast_specs = [pl.BlockSpec((mla_cast_rows(w, b, t, tq), w2.shape[1]), step) for w, w2 in zip(cast, cast2d)]
    outs = pl.pallas_call(
        functools.partial(_mla_attn_kernel, t=t, tk=tk, n_cast=len(cast)),
        grid=(b, N_PAIRS, nq),
        in_specs=[
            pl.BlockSpec((tq, 2 * LANES), lambda i, p, s: (i * nq + s, p)),
            pl.BlockSpec((t, 2 * LANES), lambda i, p, s: (i, p)),
            pl.BlockSpec((t, 2 * LANES), lambda i, p, s: (i, p)),
        ] + cast_specs,
        out_specs=[pl.BlockSpec((tq, HEAD_PAIR), lambda i, p, s: (i * nq + s, p))] + cast_specs,
        out_shape=[jax.ShapeDtypeStruct((n, SLAB), BF16)]
        + [jax.ShapeDtypeStruct(w2.shape, BF16) for w2 in cast2d],
        compiler_params=_cparams(("parallel", "parallel", "arbitrary")),
        name="mla_attention",
    )(qf, kf, vf, *cast2d)
    return outs[0], [o.reshape(w.shape) for o, w in zip(outs[1:], cast)]


def _in_proj_segments():
    sizes = (SSM_INNER, SSM_CONV_CH, 2 * SSM_HEADS, SLAB, SLAB, SLAB, MLA_Q_RANK, MLA_KV_RANK, MLA_ROPE,
             SLAB, SLAB, SLAB)
    off = [int(v) for v in np.concatenate([[0], np.cumsum(sizes)])]
    main = ((off[0], off[2]), (off[3], off[8]), (off[9], off[12]))
    tail = ((off[8], off[9]), (off[2], off[3]))
    return main, tail


def _in_proj_columns():
    main, tail = _in_proj_segments()
    cols = lambda segs: np.concatenate([np.arange(a, b) for a, b in segs])
    return cols(main), cols(tail)


def in_proj_weights(w_in_l):
    main, tail = _in_proj_segments()
    w_main = jnp.concatenate([w_in_l[:, a:b] for a, b in main], axis=1).astype(BF16)
    pad = jnp.zeros((D_MODEL, LANES - sum(b - a for a, b in tail)), w_in_l.dtype)
    w_tail = jnp.concatenate([w_in_l[:, a:b] for a, b in tail] + [pad], axis=1).astype(BF16)
    return w_main, w_tail


MLA_TQ = 1024
MLA_CASTS_PER_CALL = 2


def mixers(proj, tail, p, l, b, t, cast):
    xbc = conv_silu(proj, p["conv_w"][l], p["conv_b"][l], b, t)
    y_f, y_b = ssd_scan(xbc, tail, p["dt_bias"][l], p["a_log"][l], b, t)
    y_ssm = ssd_combine(y_f, y_b, xbc, proj, p["d_skip"][l], p["ssm_norm_w"][l], 1024)

    y_na = na_attention(proj, p["na_rpb"][l], b, t)

    qf, kf, vf = mla_project(proj, tail, p["mla_q_norm_w"][l], p["mla_kv_norm_w"][l],
                             p["mla_w_uq"][l], p["mla_w_ukv"][l], b, t, 512)
    y_mla, cast_out = mla_attention(qf, kf, vf, b, t, MLA_TQ, 512, cast)

    qkvs = rope_qkv(proj, b, t, 1024)
    outs, lses = zip(*[band_attention(qkv, w, d) for qkv, (w, d) in zip(qkvs, DIL_PAIRS)])
    y_dil = dil_combine(outs, lses, 1024)
    return (y_ssm, y_na, y_mla, y_dil), cast_out


def kernel(x, attn_norm_w, w_in, conv_w, conv_b, a_log, dt_bias, d_skip, ssm_norm_w, na_rpb,
           mla_q_norm_w, mla_kv_norm_w, mla_w_uq, mla_w_ukv, w_o, ffn_norm_w, ffn_w_gate, ffn_w_up,
           ffn_w_down, router_w, exp_w_gate, exp_w_up, exp_w_down, final_norm_w):
    b, t, _ = x.shape
    n = b * t
    depth = w_in.shape[0]
    p = dict(conv_w=conv_w, conv_b=conv_b, a_log=a_log, dt_bias=dt_bias, d_skip=d_skip,
             ssm_norm_w=ssm_norm_w, na_rpb=na_rpb, mla_q_norm_w=mla_q_norm_w,
             mla_kv_norm_w=mla_kv_norm_w, mla_w_uq=mla_w_uq, mla_w_ukv=mla_w_ukv)
    x = x.reshape(n, D_MODEL)
    cast_rows = 256
    w_o_b = cast_bf16(w_o, cast_rows)
    ffn_b = [cast_bf16(w, cast_rows) for w in (ffn_w_gate, ffn_w_up, ffn_w_down)]
    exp_f32 = [exp_w_gate, exp_w_up, exp_w_down]
    exp_b = [None] * len(exp_f32)
    pending = [k for k, w in enumerate(exp_f32) if mla_cast_rows(w, b, t, MLA_TQ) is not None]
    moe_tm = 512
    normed = False
    for l in range(depth):
        w_main, w_tail = in_proj_weights(w_in[l])
        proj, tail = in_proj(x, attn_norm_w[l], w_main, w_tail, 512, PROJ_MAIN // 2)
        jobs, pending = pending[:MLA_CASTS_PER_CALL], pending[MLA_CASTS_PER_CALL:]
        mix, cast_out = mixers(proj, tail, p, l, b, t, [exp_f32[k] for k in jobs])
        for k, w_b in zip(jobs, cast_out):
            exp_b[k] = w_b
        x = out_proj(mix, w_o_b, l, x, 1024, 1024)
        j = l // 2
        if l % 2 == 0:
            x = ffn_dense(x, ffn_norm_w[l], *ffn_b, j, 1024, 512)
        else:
            pending = []
            exp_b = [cast_bf16(w, cast_rows) if w_b is None else w_b for w, w_b in zip(exp_f32, exp_b)]
            top_i, gates = moe_router(x, ffn_norm_w[l], router_w[j], 512)
            src, pos, tile_expert, tile_valid = moe_plan(top_i, moe_tm)
            y = moe_ffn(x, ffn_norm_w[l], *exp_b, j, src, tile_expert, tile_valid, moe_tm, 512)
            last = l == depth - 1
            x = moe_combine(x, gates, y, pos, final_norm_w if last else None, 256)
            normed = last
    if not normed:
        x = rmsnorm_rows(x, final_norm_w, 1024)
    return x.reshape(b, t, D_MODEL)
```

```python
import functools
import math

import numpy as np
import jax
import jax.numpy as jnp
from jax import lax
from jax.experimental import pallas as pl
from jax.experimental.pallas import tpu as pltpu

F32 = jnp.float32
BF16 = jnp.bfloat16

D_MODEL = 2048
GRID_W = 64
HEAD_DIM = 64
ROPE_THETA = 10000.0
NORM_EPS = 1e-6
NEG_INF = -1e30

SSM_HEADS = 8
SSM_HEAD_DIM = 64
SSM_INNER = SSM_HEADS * SSM_HEAD_DIM
SSM_GROUPS = 2
SSM_STATE = 128
SSM_CONV = 5
SSM_CHUNK = 128
SSM_CONV_CH = SSM_INNER + 2 * SSM_GROUPS * SSM_STATE

NA_HEADS = 8
NA_WIN_ROWS = 8
NA_WIN_COLS = 16
NA_COL_BLOCK = 16
NA_KEY_COLS = 32
NA_ROWS_PER_STEP = 4

MLA_HEADS = 8
MLA_Q_RANK = 512
MLA_KV_RANK = 512
MLA_NOPE = 64
MLA_ROPE = 32
MLA_V = 64

DIL_HEADS = 8
DIL_PAIRS = ((128, 1), (512, 4), (2048, 16))
DIL_QBLOCK = 128
BAND_BLOCKS_PER_STEP = 4

N_EXPERTS = 8
TOP_K = 2
SPLIT_PARTS = 3

LANES = 128
HEAD_PAIR = 2 * HEAD_DIM
N_PAIRS = 4
SLAB = 512

COL_Z, COL_XBC, COL_NAQ, COL_NAK, COL_NAV, COL_CQ, COL_CKV, COL_DLQ, COL_DLK, COL_DLV = (
    0, 1, 3, 4, 5, 6, 7, 8, 9, 10)
PROJ_MAIN = 11 * SLAB
TAIL_DT = 32

VMEM_LIMIT = 56 * 1024 * 1024


def _cparams(sem, vmem=VMEM_LIMIT):
    return pltpu.CompilerParams(dimension_semantics=sem, vmem_limit_bytes=vmem)


def _lane_lt(shape, bound, period=None):
    lane = lax.broadcasted_iota(jnp.int32, shape, len(shape) - 1)
    if period is not None:
        lane = lane % period
    return lane < bound


def _rms(x, w):
    ms = jnp.mean(x * x, axis=-1, keepdims=True)
    return x * lax.rsqrt(ms + NORM_EPS) * w


def _dot(a, b):
    return jnp.dot(a, b, preferred_element_type=F32)


def _dot_nt(a, b):
    return lax.dot_general(a, b, (((1,), (1,)), ((), ())), preferred_element_type=F32)


def _dot_tn(a, b):
    return lax.dot_general(a, b, (((0,), (0,)), ((), ())), preferred_element_type=F32)


def _cast_kernel(x_ref, o_ref):
    o_ref[...] = x_ref[...].astype(o_ref.dtype)


def cast_bf16(w, tr):
    shape = w.shape
    w2 = w.reshape(-1, shape[-1])
    r, c = w2.shape
    out = pl.pallas_call(
        _cast_kernel,
        grid=(r // tr,),
        in_specs=[pl.BlockSpec((tr, c), lambda i: (i, 0))],
        out_specs=pl.BlockSpec((tr, c), lambda i: (i, 0)),
        out_shape=jax.ShapeDtypeStruct((r, c), BF16),
        compiler_params=_cparams(("parallel",)),
        name="cast_bf16",
    )(w2)
    return out.reshape(shape)


def _in_proj_kernel(x_ref, nw_ref, w_ref, wt_ref, o_ref, t_ref, h_ref):
    @pl.when(pl.program_id(1) == 0)
    def _():
        h = _rms(x_ref[...], nw_ref[...]).astype(BF16)
        h_ref[...] = h
        t_ref[...] = _dot(h, wt_ref[...])

    o_ref[...] = _dot(h_ref[...], w_ref[...]).astype(o_ref.dtype)


def in_proj(x, nw, w_main, w_tail, tm, tn):
    n, k = x.shape
    nout = w_main.shape[1]
    return pl.pallas_call(
        _in_proj_kernel,
        grid=(n // tm, nout // tn),
        in_specs=[
            pl.BlockSpec((tm, k), lambda i, j: (i, 0)),
            pl.BlockSpec((1, k), lambda i, j: (0, 0)),
            pl.BlockSpec((k, tn), lambda i, j: (0, j)),
            pl.BlockSpec((k, LANES), lambda i, j: (0, 0)),
        ],
        out_specs=[pl.BlockSpec((tm, tn), lambda i, j: (i, j)), pl.BlockSpec((tm, LANES), lambda i, j: (i, 0))],
        out_shape=[jax.ShapeDtypeStruct((n, nout), BF16), jax.ShapeDtypeStruct((n, LANES), F32)],
        scratch_shapes=[pltpu.VMEM((tm, k), BF16)],
        compiler_params=_cparams(("parallel", "arbitrary")),
        name="in_proj",
    )(x, nw.reshape(1, k), w_main, w_tail)


def _out_proj_kernel(a0_ref, a1_ref, a2_ref, a3_ref, w_ref, r_ref, o_ref):
    acc = r_ref[...]
    for s, a_ref in enumerate((a0_ref, a1_ref, a2_ref, a3_ref)):
        acc = acc + _dot(a_ref[...], w_ref[s * SLAB:(s + 1) * SLAB, :])
    o_ref[...] = acc


def out_proj(mix, w, layer, res, tm, tn):
    n = res.shape[0]
    return pl.pallas_call(
        _out_proj_kernel,
        grid=(n // tm, D_MODEL // tn),
        in_specs=[pl.BlockSpec((tm, SLAB), lambda i, j: (i, 0))] * 4 + [
            pl.BlockSpec((None, 4 * SLAB, tn), lambda i, j: (layer, 0, j)),
            pl.BlockSpec((tm, tn), lambda i, j: (i, j)),
        ],
        out_specs=pl.BlockSpec((tm, tn), lambda i, j: (i, j)),
        out_shape=jax.ShapeDtypeStruct((n, D_MODEL), F32),
        compiler_params=_cparams(("parallel", "arbitrary")),
        name="out_proj",
    )(*mix, w, res)


def _ffn_kernel(x_ref, nw_ref, wg_ref, wu_ref, wd_ref, o_ref, h_ref):
    @pl.when(pl.program_id(1) == 0)
    def _():
        x = x_ref[...]
        h_ref[...] = _rms(x, nw_ref[...]).astype(BF16)
        o_ref[...] = x

    _swiglu_rows(h_ref, wg_ref, wu_ref, wd_ref, o_ref)


FFN_ROW_CHUNK = 512


def _swiglu_rows(h_ref, wg_ref, wu_ref, wd_ref, o_ref):
    tm = h_ref.shape[0]
    for r0 in range(0, tm, FFN_ROW_CHUNK):
        rows = slice(r0, r0 + FFN_ROW_CHUNK)
        h = h_ref[rows, :]
        g = _dot(h, wg_ref[...])
        u = _dot(h, wu_ref[...])
        a = (g * jax.nn.sigmoid(g) * u).astype(BF16)
        o_ref[rows, :] += _dot(a, wd_ref[...])


def ffn_dense(x, nw, wg, wu, wd, layer, tm, tf):
    n = x.shape[0]
    d_ff = wg.shape[-1]
    return pl.pallas_call(
        _ffn_kernel,
        grid=(n // tm, d_ff // tf),
        in_specs=[
            pl.BlockSpec((tm, D_MODEL), lambda i, j: (i, 0)),
            pl.BlockSpec((1, D_MODEL), lambda i, j: (0, 0)),
            pl.BlockSpec((None, D_MODEL, tf), lambda i, j: (layer, 0, j)),
            pl.BlockSpec((None, D_MODEL, tf), lambda i, j: (layer, 0, j)),
            pl.BlockSpec((None, tf, D_MODEL), lambda i, j: (layer, j, 0)),
        ],
        out_specs=pl.BlockSpec((tm, D_MODEL), lambda i, j: (i, 0)),
        out_shape=jax.ShapeDtypeStruct((n, D_MODEL), F32),
        scratch_shapes=[pltpu.VMEM((tm, D_MODEL), BF16)],
        compiler_params=_cparams(("parallel", "arbitrary")),
        name="ffn_dense",
    )(x, nw.reshape(1, D_MODEL), wg, wu, wd)


def _router_kernel(x_ref, nw_ref, rw_ref, idx_ref, gate_ref):
    h = _rms(x_ref[...], nw_ref[...])
    acc = jnp.zeros((h.shape[0], LANES), F32)
    rem = h
    for _ in range(SPLIT_PARTS):
        part = rem.astype(BF16)
        acc = acc + _dot(part, rw_ref[...])
        rem = rem - part.astype(F32)
    logits = acc
    for k in range(1, SPLIT_PARTS):
        logits = logits + pltpu.roll(acc, LANES - k * N_EXPERTS, 1)
    lane = lax.broadcasted_iota(jnp.int32, logits.shape, 1)
    logits = jnp.where(lane < N_EXPERTS, logits, -jnp.inf)
    m1 = jnp.max(logits, axis=-1, keepdims=True)
    i1 = jnp.min(jnp.where(logits == m1, lane, LANES), axis=-1, keepdims=True)
    rest = jnp.where(lane == i1, -jnp.inf, logits)
    m2 = jnp.max(rest, axis=-1, keepdims=True)
    i2 = jnp.min(jnp.where(rest == m2, lane, LANES), axis=-1, keepdims=True)
    e2 = jnp.exp(m2 - m1)
    g1 = 1.0 / (1.0 + e2)
    g2 = e2 / (1.0 + e2)
    idx_ref[...] = jnp.where(lane == 0, i1, i2)[:, :TOP_K]
    gate_ref[...] = jnp.where(lane == 0, g1, g2)[:, :TOP_K]


def moe_router(x, nw, router_w, tm):
    n = x.shape[0]
    parts, rem = [], router_w
    for _ in range(SPLIT_PARTS):
        parts.append(rem.astype(BF16))
        rem = rem - parts[-1].astype(F32)
    pad = jnp.zeros((D_MODEL, LANES - SPLIT_PARTS * N_EXPERTS), BF16)
    rw = jnp.concatenate(parts + [pad], axis=1)
    return pl.pallas_call(
        _router_kernel,
        grid=(n // tm,),
        in_specs=[
            pl.BlockSpec((tm, D_MODEL), lambda i: (i, 0)),
            pl.BlockSpec((1, D_MODEL), lambda i: (0, 0)),
            pl.BlockSpec((D_MODEL, LANES), lambda i: (0, 0)),
        ],
        out_specs=[pl.BlockSpec((tm, TOP_K), lambda i: (i, 0)),
                   pl.BlockSpec((tm, TOP_K), lambda i: (i, 0))],
        out_shape=[jax.ShapeDtypeStruct((n, TOP_K), jnp.int32),
                   jax.ShapeDtypeStruct((n, TOP_K), F32)],
        compiler_params=_cparams(("parallel",)),
        name="moe_router",
    )(x, nw.reshape(1, D_MODEL), rw)


def moe_plan(top_i, tm):
    n = top_i.shape[0]
    flat_e = top_i.reshape(-1)
    onehot = (flat_e[:, None] == jnp.arange(N_EXPERTS, dtype=jnp.int32)[None, :]).astype(jnp.int32)
    csum = jnp.cumsum(onehot, axis=0)
    counts = csum[-1]
    rank = jnp.sum(onehot * csum, axis=1) - 1
    padded = ((counts + tm - 1) // tm) * tm
    pend = jnp.cumsum(padded)
    pstart = pend - padded
    pos = pstart[flat_e] + rank
    n_slots = n * TOP_K + N_EXPERTS * tm
    n_tiles = n_slots // tm
    src = jnp.zeros((n_slots,), jnp.int32).at[pos].set(jnp.arange(n * TOP_K, dtype=jnp.int32) // TOP_K)
    tile_start = jnp.arange(n_tiles, dtype=jnp.int32) * tm
    tile_expert = jnp.sum((tile_start[:, None] >= pend[None, :]).astype(jnp.int32), axis=1)
    tile_valid = (tile_start < pend[-1]).astype(jnp.int32)
    last_valid = jnp.maximum(pend[-1] // tm - 1, 0)
    tile_expert = jnp.where(tile_valid == 1, tile_expert, tile_expert[last_valid]).astype(jnp.int32)
    return src, pos.reshape(n, TOP_K).astype(jnp.int32), tile_expert, tile_valid


def _moe_ffn_kernel(te_ref, tv_ref, src_ref, nsrc_ref, x_hbm, nw_ref, wg_ref, wu_ref, wd_ref, y_ref,
                    xbuf, h_ref, sem, *, tm, rows_per_step):
    i = pl.program_id(0)
    j = pl.program_id(1)
    issued = xbuf.shape[0]
    valid = tv_ref[i] == 1
    has_rows = jnp.logical_or(i == 0, tv_ref[jnp.maximum(i - 1, 0)] == 1)

    def start_row(idx_ref, row):
        tok = idx_ref[0, 0, jnp.minimum(row, tm - 1)]
        pltpu.make_async_copy(x_hbm.at[pl.ds(tok, 1)], xbuf.at[pl.ds(row, 1)], sem).start(priority=1)

    @pl.when(j == 0)
    def _():
        y_ref[...] = jnp.zeros_like(y_ref)

    @pl.when(jnp.logical_and(j == 0, i == 0))
    def _():
        def start(r, c):
            start_row(src_ref, r)
            return c

        lax.fori_loop(0, issued, start, 0)

    @pl.when(jnp.logical_and(j == 0, has_rows))
    def _():
        pltpu.make_async_copy(x_hbm.at[pl.ds(0, issued)], xbuf.at[pl.ds(0, issued)], sem).wait()
        h_ref[...] = _rms(xbuf[0:tm, :], nw_ref[...]).astype(BF16)

    @pl.when(valid)
    def _():
        for r in range(rows_per_step):
            start_row(nsrc_ref, j * rows_per_step + r)
        _swiglu_rows(h_ref, wg_ref, wu_ref, wd_ref, y_ref)


def moe_ffn(x, nw, wg, wu, wd, layer, src, tile_expert, tile_valid, tm, tf):
    n_slots = src.shape[0]
    n_tiles = n_slots // tm + 1
    d_ff = wg.shape[-1]
    nf = d_ff // tf
    sublanes = 8
    rows_per_step = -(-tm // (nf * sublanes)) * sublanes
    buf_rows = rows_per_step * nf
    tile_expert = jnp.concatenate([tile_expert, tile_expert[-1:]])
    tile_valid = jnp.concatenate([tile_valid, jnp.zeros((1,), tile_valid.dtype)])
    src3 = jnp.concatenate([src, jnp.zeros((tm,), src.dtype)]).reshape(n_tiles, 1, tm)

    def wcol(i, j, te_ref, tv_ref):
        return (layer, te_ref[i], 0, jnp.where(tv_ref[i] == 1, j, nf - 1))

    def wrow(i, j, te_ref, tv_ref):
        return (layer, te_ref[i], jnp.where(tv_ref[i] == 1, j, nf - 1), 0)

    grid_spec = pltpu.PrefetchScalarGridSpec(
        num_scalar_prefetch=2,
        grid=(n_tiles, nf),
        in_specs=[
            pl.BlockSpec((1, 1, tm), lambda i, j, *_: (i, 0, 0), memory_space=pltpu.SMEM),
            pl.BlockSpec((1, 1, tm), lambda i, j, *_: (jnp.minimum(i + 1, n_tiles - 1), 0, 0),
                         memory_space=pltpu.SMEM),
            pl.BlockSpec(memory_space=pl.ANY),
            pl.BlockSpec((1, D_MODEL), lambda i, j, *_: (0, 0)),
            pl.BlockSpec((None, None, D_MODEL, tf), wcol),
            pl.BlockSpec((None, None, D_MODEL, tf), wcol),
            pl.BlockSpec((None, None, tf, D_MODEL), wrow),
        ],
        out_specs=pl.BlockSpec((tm, D_MODEL), lambda i, j, *_: (i, 0)),
        scratch_shapes=[pltpu.VMEM((buf_rows, D_MODEL), F32), pltpu.VMEM((tm, D_MODEL), BF16),
                        pltpu.SemaphoreType.DMA],
    )
    return pl.pallas_call(
        functools.partial(_moe_ffn_kernel, tm=tm, rows_per_step=rows_per_step),
        grid_spec=grid_spec,
        out_shape=jax.ShapeDtypeStruct((n_tiles * tm, D_MODEL), F32),
        compiler_params=_cparams(("arbitrary", "arbitrary")),
        name="moe_ffn",
    )(tile_expert, tile_valid, src3, src3, x, nw.reshape(1, D_MODEL), wg, wu, wd)


def _moe_combine_kernel(pos_ref, npos_ref, x_ref, gate_ref, y_hbm, fw_ref, o_ref, ybuf, sem, *, tm, final_norm):
    i = pl.program_id(0)
    n_tiles = pl.num_programs(0)
    cur = i % 2

    def row_copy(idx_ref, buf, r, k):
        slot = idx_ref[0, 0, r * TOP_K + k]
        return pltpu.make_async_copy(y_hbm.at[pl.ds(slot, 1)], ybuf.at[buf, k, pl.ds(r, 1)], sem.at[buf])

    def gather(idx_ref, buf):
        def start(r, c):
            for k in range(TOP_K):
                row_copy(idx_ref, buf, r, k).start(priority=k % 2)
            return c

        lax.fori_loop(0, tm, start, 0, unroll=8)

    @pl.when(i == 0)
    def _():
        gather(pos_ref, 0)

    @pl.when(i + 1 < n_tiles)
    def _():
        gather(npos_ref, 1 - cur)

    for k in range(TOP_K):
        pltpu.make_async_copy(y_hbm.at[pl.ds(0, tm)], ybuf.at[cur, k], sem.at[cur]).wait()
    gates = gate_ref[...]
    out = x_ref[...]
    for k in range(TOP_K):
        out = out + gates[:, k:k + 1] * ybuf[cur, k]
    if final_norm:
        out = _rms(out, fw_ref[...])
    o_ref[...] = out


def moe_combine(x, gates, y, pos, final_w, tm):
    n = x.shape[0]
    final_norm = final_w is not None
    fw = final_w if final_norm else jnp.ones((D_MODEL,), F32)
    n_tiles = n // tm
    pos3 = pos.reshape(n_tiles, 1, tm * TOP_K)
    return pl.pallas_call(
        functools.partial(_moe_combine_kernel, tm=tm, final_norm=final_norm),
        grid=(n_tiles,),
        in_specs=[
            pl.BlockSpec((1, 1, tm * TOP_K), lambda i: (i, 0, 0), memory_space=pltpu.SMEM),
            pl.BlockSpec((1, 1, tm * TOP_K), lambda i: (jnp.minimum(i + 1, n_tiles - 1), 0, 0),
                         memory_space=pltpu.SMEM),
            pl.BlockSpec((tm, D_MODEL), lambda i: (i, 0)),
            pl.BlockSpec((tm, TOP_K), lambda i: (i, 0)),
            pl.BlockSpec(memory_space=pl.ANY),
            pl.BlockSpec((1, D_MODEL), lambda i: (0, 0)),
        ],
        out_specs=pl.BlockSpec((tm, D_MODEL), lambda i: (i, 0)),
        out_shape=jax.ShapeDtypeStruct((n, D_MODEL), F32),
        scratch_shapes=[pltpu.VMEM((2, TOP_K, tm, D_MODEL), F32), pltpu.SemaphoreType.DMA((2,))],
        compiler_params=_cparams(("arbitrary",)),
        name="moe_combine",
    )(pos3, pos3, x, gates, y, fw.reshape(1, D_MODEL))


def _rmsnorm_kernel(x_ref, w_ref, o_ref):
    o_ref[...] = _rms(x_ref[...], w_ref[...])


def rmsnorm_rows(x, w, tm):
    n = x.shape[0]
    return pl.pallas_call(
        _rmsnorm_kernel,
        grid=(n // tm,),
        in_specs=[pl.BlockSpec((tm, D_MODEL), lambda i: (i, 0)),
                  pl.BlockSpec((1, D_MODEL), lambda i: (0, 0))],
        out_specs=pl.BlockSpec((tm, D_MODEL), lambda i: (i, 0)),
        out_shape=jax.ShapeDtypeStruct((n, D_MODEL), F32),
        compiler_params=_cparams(("parallel",)),
        name="final_norm",
    )(x, w.reshape(1, D_MODEL))


CONV_PAD = 8


def _conv_kernel(x_ref, w_ref, b_ref, o_ref, pad_ref, *, t):
    half = SSM_CONV // 2
    zeros = jnp.zeros((CONV_PAD, pad_ref.shape[1]), F32)
    pad_ref[0:CONV_PAD, :] = zeros
    pad_ref[CONV_PAD + t:CONV_PAD + t + CONV_PAD, :] = zeros
    pad_ref[CONV_PAD:CONV_PAD + t, :] = x_ref[...].astype(F32)
    acc = jnp.zeros(o_ref.shape, F32) + b_ref[...]
    for k in range(SSM_CONV):
        acc = acc + pad_ref[pl.ds(CONV_PAD - half + k, t), :] * w_ref[k:k + 1, :]
    o_ref[...] = (acc * jax.nn.sigmoid(acc)).astype(o_ref.dtype)


def conv_silu(proj, conv_w, conv_b, b, t):
    tc = 256
    nblk = SSM_CONV_CH // tc
    col0 = COL_XBC * SLAB // tc
    return pl.pallas_call(
        functools.partial(_conv_kernel, t=t),
        grid=(b, nblk),
        in_specs=[
            pl.BlockSpec((t, tc), lambda i, j: (i, col0 + j)),
            pl.BlockSpec((SSM_CONV, tc), lambda i, j: (0, j)),
            pl.BlockSpec((1, tc), lambda i, j: (0, j)),
        ],
        out_specs=pl.BlockSpec((t, tc), lambda i, j: (i, j)),
        out_shape=jax.ShapeDtypeStruct((b * t, SSM_CONV_CH), BF16),
        scratch_shapes=[pltpu.VMEM((t + 2 * CONV_PAD, tc), F32)],
        compiler_params=_cparams(("parallel", "parallel")),
        name="conv_silu",
    )(proj, conv_w, conv_b.reshape(1, SSM_CONV_CH))


def _ssd_stage1(xbc_ref, tail_ref, bias_ref, alog_ref, tri_ref, sel_ref, state_ref, direction):
    q = SSM_CHUNK
    dt = jax.nn.softplus(tail_ref[...] + bias_ref[...])
    da = dt * (-jnp.exp(alog_ref[...]))
    cs = jnp.dot(tri_ref[...], da, preferred_element_type=F32, precision=lax.Precision.HIGHEST)
    total = cs[q - 1:q, :]
    if direction == 0:
        e_out = cs
        e_in = total - cs
        e_seg = cs
    else:
        ex = cs - da
        e_out = total - ex
        e_in = ex
        e_seg = -ex
    dec_out_b = jnp.exp(e_out).astype(BF16)
    dec_in_dt_b = (jnp.exp(e_in) * dt).astype(BF16)
    pairs_per_group = N_PAIRS // SSM_GROUPS
    cbs, y_offs, in_scales = [], [], []
    for g in range(SSM_GROUPS):
        bm = xbc_ref[:, SSM_INNER + g * SSM_STATE:SSM_INNER + (g + 1) * SSM_STATE]
        cm = xbc_ref[:, SSM_INNER + (SSM_GROUPS + g) * SSM_STATE:SSM_INNER + (SSM_GROUPS + g + 1) * SSM_STATE]
        cbs.append(_dot_nt(cm, bm))
        for pair in range(g * pairs_per_group, (g + 1) * pairs_per_group):
            sel = sel_ref[direction, pair]
            y_offs.append(_dot(dec_out_b, sel) * _dot(cm, state_ref[pair].astype(BF16)))
            in_scales.append(_dot(dec_in_dt_b, sel))
    return dict(e_seg=e_seg, e_seg_t=jnp.transpose(e_seg), dt_t=jnp.transpose(dt), dec_tot=jnp.exp(total),
                cbs=cbs, y_offs=y_offs, in_scales=in_scales)


def _ssd_stage2(ctx, xbc_ref, y_ref, state_ref, direction):
    q = SSM_CHUNK
    row = lax.broadcasted_iota(jnp.int32, (q, q), 0)
    col = lax.broadcasted_iota(jnp.int32, (q, q), 1)
    keep = (row >= col) if direction == 0 else (col >= row)
    first_half = _lane_lt((q, HEAD_PAIR), HEAD_DIM)
    pairs_per_group = N_PAIRS // SSM_GROUPS
    for pair in range(N_PAIRS):
        g = pair // pairs_per_group
        bm = xbc_ref[:, SSM_INNER + g * SSM_STATE:SSM_INNER + (g + 1) * SSM_STATE]
        xs_pair = xbc_ref[:, pair * HEAD_PAIR:(pair + 1) * HEAD_PAIR]
        ys, decs = [], []
        for hh in range(2):
            lane = TAIL_DT + direction * SSM_HEADS + pair * 2 + hh
            seg = ctx["e_seg"][:, lane:lane + 1] - ctx["e_seg_t"][lane:lane + 1, :]
            lmat = jnp.where(keep, jnp.exp(seg), 0.0)
            w = (ctx["cbs"][g] * lmat * ctx["dt_t"][lane:lane + 1, :]).astype(BF16)
            ys.append(_dot(w, xs_pair))
            decs.append(ctx["dec_tot"][:, lane:lane + 1])
        y_ref[:, pair * HEAD_PAIR:(pair + 1) * HEAD_PAIR] = jnp.where(first_half, ys[0], ys[1]) + ctx["y_offs"][pair]
        st = state_ref[pair]
        first_half_s = _lane_lt(st.shape, HEAD_DIM)
        state_ref[pair] = (st * jnp.where(first_half_s, decs[0], decs[1])
                           + _dot_tn(bm, (xs_pair * ctx["in_scales"][pair]).astype(BF16)))


def _ssd_kernel(xbc_f_ref, tail_f_ref, xbc_b_ref, tail_b_ref, bias_ref, alog_ref, tri_ref, sel_ref,
                y_f_ref, y_b_ref, state_ref):
    @pl.when(pl.program_id(1) == 0)
    def _():
        state_ref[...] = jnp.zeros_like(state_ref)

    ins = ((xbc_f_ref, tail_f_ref, y_f_ref), (xbc_b_ref, tail_b_ref, y_b_ref))
    ctxs = [_ssd_stage1(xbc_ref, tail_ref, bias_ref, alog_ref, tri_ref, sel_ref, state_ref.at[d], d)
            for d, (xbc_ref, tail_ref, _) in enumerate(ins)]
    for d, (xbc_ref, _, y_ref) in enumerate(ins):
        _ssd_stage2(ctxs[d], xbc_ref, y_ref, state_ref.at[d], d)


def _ssd_lane_selectors():
    sel = np.zeros((2, N_PAIRS, LANES, HEAD_PAIR), np.float32)
    for d in range(2):
        for pair in range(N_PAIRS):
            for hh in range(2):
                sel[d, pair, TAIL_DT + d * SSM_HEADS + pair * 2 + hh, hh * HEAD_DIM:(hh + 1) * HEAD_DIM] = 1.0
    return jnp.asarray(sel, BF16)


def ssd_scan(xbc, tail, dt_bias, a_log, b, t):
    q = SSM_CHUNK
    nc = t // q
    bias_row = jnp.zeros((1, LANES), F32).at[0, TAIL_DT:TAIL_DT + 2 * SSM_HEADS].set(dt_bias.reshape(-1))
    alog_row = jnp.zeros((1, LANES), F32).at[0, TAIL_DT:TAIL_DT + 2 * SSM_HEADS].set(a_log.reshape(-1))
    tri = jnp.asarray(np.tril(np.ones((q, q), np.float32)))
    fwd = lambda i, c: (i * nc + c, 0)
    bwd = lambda i, c: (i * nc + nc - 1 - c, 0)
    const = lambda i, c: (0, 0)
    return pl.pallas_call(
        _ssd_kernel,
        grid=(b, nc),
        in_specs=[
            pl.BlockSpec((q, SSM_CONV_CH), fwd), pl.BlockSpec((q, LANES), fwd),
            pl.BlockSpec((q, SSM_CONV_CH), bwd), pl.BlockSpec((q, LANES), bwd),
            pl.BlockSpec((1, LANES), const), pl.BlockSpec((1, LANES), const), pl.BlockSpec((q, q), const),
            pl.BlockSpec((2, N_PAIRS, LANES, HEAD_PAIR), lambda i, c: (0, 0, 0, 0)),
        ],
        out_specs=[pl.BlockSpec((q, SSM_INNER), fwd), pl.BlockSpec((q, SSM_INNER), bwd)],
        out_shape=[jax.ShapeDtypeStruct((b * t, SSM_INNER), F32)] * 2,
        scratch_shapes=[pltpu.VMEM((2, N_PAIRS, SSM_STATE, HEAD_PAIR), F32)],
        compiler_params=_cparams(("parallel", "arbitrary")),
        name="ssd_scan",
    )(xbc, tail, xbc, tail, bias_row, alog_row, tri, _ssd_lane_selectors())


def _ssd_combine_kernel(yf_ref, yb_ref, xs_ref, z_ref, d_ref, nw_ref, o_ref):
    y = yf_ref[...] + yb_ref[...] + xs_ref[...].astype(F32) * d_ref[...]
    z = z_ref[...].astype(F32)
    o_ref[...] = _rms(y * (z * jax.nn.sigmoid(z)), nw_ref[...]).astype(o_ref.dtype)


def ssd_combine(y_f, y_b, xbc, proj, d_skip, norm_w, tm):
    n = y_f.shape[0]
    d_row = jnp.repeat(d_skip, SSM_HEAD_DIM).reshape(1, SSM_INNER)
    row = lambda i: (i, 0)
    return pl.pallas_call(
        _ssd_combine_kernel,
        grid=(n // tm,),
        in_specs=[
            pl.BlockSpec((tm, SSM_INNER), row),
            pl.BlockSpec((tm, SSM_INNER), row),
            pl.BlockSpec((tm, SSM_INNER), row),
            pl.BlockSpec((tm, SLAB), lambda i: (i, COL_Z)),
            pl.BlockSpec((1, SSM_INNER), lambda i: (0, 0)),
            pl.BlockSpec((1, SSM_INNER), lambda i: (0, 0)),
        ],
        out_specs=pl.BlockSpec((tm, SSM_INNER), row),
        out_shape=jax.ShapeDtypeStruct((n, SSM_INNER), BF16),
        compiler_params=_cparams(("parallel",)),
        name="ssd_combine",
    )(y_f, y_b, xbc, proj, d_row, norm_w.reshape(1, SSM_INNER))


def _pair_scores(q2, k2):
    first_q = _lane_lt(q2.shape, HEAD_DIM)
    zero = jnp.zeros_like(q2)
    qs = jnp.concatenate([jnp.where(first_q, q2, zero), jnp.where(first_q, zero, q2)], axis=0)
    return _dot_nt(qs, k2)


def _pair_attend(s, v2):
    tq = s.shape[0] // 2
    m = jnp.max(s, axis=-1, keepdims=True)
    p = jnp.exp(s - m)
    l = jnp.sum(p, axis=-1, keepdims=True)
    o = _dot(p.astype(BF16), v2) / l
    lse = m + jnp.log(l)
    first_o = _lane_lt((tq, HEAD_PAIR), HEAD_DIM)
    return jnp.where(first_o, o[:tq], o[tq:]), jnp.where(first_o, lse[:tq], lse[tq:])


def na_bias_tables(rpb, rows):
    kr = min(NA_WIN_ROWS, rows)
    qc = np.arange(GRID_W)
    kc = np.arange(GRID_W)
    q_start = np.clip(qc - NA_WIN_COLS // 2, 0, GRID_W - NA_WIN_COLS)
    col_in = (kc[None, :] >= q_start[:, None]) & (kc[None, :] < q_start[:, None] + NA_WIN_COLS)
    col_off = np.clip(kc[None, :] - qc[:, None] + NA_WIN_COLS - 1, 0, 2 * NA_WIN_COLS - 2)
    onehot = (col_off[None] == np.arange(2 * NA_WIN_COLS - 1)[:, None, None]).astype(np.float32)
    expanded = jnp.einsum("hrc,cqk->hqrk", rpb, jnp.asarray(onehot), precision=lax.Precision.HIGHEST)
    expanded = jnp.where(jnp.asarray(col_in)[None, :, None, :], expanded, NEG_INF)

    def table(r):
        row_start = int(np.clip(r - kr // 2, 0, rows - kr))
        ro0 = row_start - r + NA_WIN_ROWS - 1
        return expanded[:, :, ro0:ro0 + kr, :].reshape(N_PAIRS, 2 * GRID_W, kr * GRID_W)

    rs = NA_ROWS_PER_STEP
    lo = [table(r) for r in range(rs)]
    mid = [table(min(rs, rows - 1))] * rs
    hi = [table(r) for r in range(rows - rs, rows)]
    return jnp.stack([jnp.stack(lo), jnp.stack(mid), jnp.stack(hi)])


def _na_kernel(q_ref, k_ref, v_ref, bias_ref, o_ref, *, rows, kr):
    step = pl.program_id(1)
    rs = NA_ROWS_PER_STEP
    for rr in range(rs):
        r = step * rs + rr
        row_start = jnp.clip(r - kr // 2, 0, rows - kr)
        k0 = pl.multiple_of(row_start * GRID_W, GRID_W)
        pair_cols = [slice(pair * HEAD_PAIR, (pair + 1) * HEAD_PAIR) for pair in range(N_PAIRS)]
        scores = []
        for pair, cols in enumerate(pair_cols):
            q2 = q_ref[rr * GRID_W:(rr + 1) * GRID_W, cols] * jnp.asarray(HEAD_DIM ** -0.5, BF16)
            scores.append(_pair_scores(q2, k_ref[pl.ds(k0, kr * GRID_W), cols]) + bias_ref[0, rr, pair])
        for s, cols in zip(scores, pair_cols):
            o, _ = _pair_attend(s, v_ref[pl.ds(k0, kr * GRID_W), cols])
            o_ref[rr * GRID_W:(rr + 1) * GRID_W, cols] = o.astype(o_ref.dtype)


def na_attention(proj, rpb, b, t):
    rows = t // GRID_W
    kr = min(NA_WIN_ROWS, rows)
    rs = NA_ROWS_PER_STEP
    nsteps = rows // rs
    bias = na_bias_tables(rpb, rows)

    def kind(i, s):
        return jnp.where(s == 0, 0, jnp.where(s == nsteps - 1, 2, 1))

    return pl.pallas_call(
        functools.partial(_na_kernel, rows=rows, kr=kr),
        grid=(b, nsteps),
        in_specs=[
            pl.BlockSpec((rs * GRID_W, SLAB), lambda i, s: (i * nsteps + s, COL_NAQ)),
            pl.BlockSpec((t, SLAB), lambda i, s: (i, COL_NAK)),
            pl.BlockSpec((t, SLAB), lambda i, s: (i, COL_NAV)),
            pl.BlockSpec((1, rs, N_PAIRS, 2 * GRID_W, kr * GRID_W), lambda i, s: (kind(i, s), 0, 0, 0, 0)),
        ],
        out_specs=pl.BlockSpec((rs * GRID_W, SLAB), lambda i, s: (i * nsteps + s, 0)),
        out_shape=jax.ShapeDtypeStruct((b * t, SLAB), BF16),
        compiler_params=_cparams(("parallel", "arbitrary")),
        name="na_attention",
    )(proj, proj, proj, bias)


def _rope_angles(t, d):
    inv = ROPE_THETA ** (-np.arange(0, d, 2, dtype=np.float32) / d)
    return np.arange(t, dtype=np.float32)[:, None] * inv[None, :]


def rope_tables_pair(t):
    ang = _rope_angles(t, HEAD_DIM)
    cos = np.tile(np.cos(ang), (1, 4))
    sin = np.tile(np.concatenate([-np.sin(ang), np.sin(ang)], axis=1), (1, 2))
    return jnp.asarray(cos, F32), jnp.asarray(sin, F32)


FOLD_CHUNK = 256
FOLD_DILS = tuple(d for _, d in DIL_PAIRS if d > 1)


def fold_permutation(dil):
    per = FOLD_CHUNK // dil
    perm = np.zeros((FOLD_CHUNK, FOLD_CHUNK), np.float32)
    dst = np.arange(FOLD_CHUNK)
    perm[dst, (dst % per) * dil + dst // per] = 1.0
    return jnp.asarray(perm, BF16)


def _rope_qkv_kernel(x_ref, v_ref, cos_ref, sin_ref, *rest):
    nd = len(FOLD_DILS)
    perm_refs, o_ref, fold_refs = rest[:nd], rest[nd], rest[nd + 1:]
    cos = cos_ref[...]
    sin = sin_ref[...]
    half = HEAD_DIM // 2
    for c in range(x_ref.shape[1] // LANES):
        x = x_ref[:, c * LANES:(c + 1) * LANES].astype(F32)
        rot = jnp.where(_lane_lt(x.shape, half, HEAD_DIM),
                        pltpu.roll(x, LANES - half, 1), pltpu.roll(x, half, 1))
        y = x * cos + rot * sin
        if c < N_PAIRS:
            y = y * (HEAD_DIM ** -0.5)
        o_ref[0, :, c * LANES:(c + 1) * LANES] = y.astype(o_ref.dtype)
    o_ref[0, :, 2 * SLAB:3 * SLAB] = v_ref[...]
    tm = x_ref.shape[0]
    for dil, perm_ref, f_ref in zip(FOLD_DILS, perm_refs, fold_refs):
        per = FOLD_CHUNK // dil
        for c in range(tm // FOLD_CHUNK):
            folded = _dot(perm_ref[...], o_ref[0, c * FOLD_CHUNK:(c + 1) * FOLD_CHUNK, :]).astype(f_ref.dtype)
            for p in range(dil):
                f_ref[p, c * per:(c + 1) * per, :] = folded[p * per:(p + 1) * per, :]


def rope_qkv(proj, b, t, tm):
    n = b * t
    cos, sin = rope_tables_pair(t)
    nb = t // tm
    fold_spec = lambda d: pl.BlockSpec((None, d, tm // d, 3 * SLAB), lambda i: (i // nb, 0, i % nb, 0))
    outs = pl.pallas_call(
        _rope_qkv_kernel,
        grid=(n // tm,),
        in_specs=[
            pl.BlockSpec((tm, 2 * SLAB), lambda i: (i, COL_DLQ // 2)),
            pl.BlockSpec((tm, SLAB), lambda i: (i, COL_DLV)),
            pl.BlockSpec((tm, LANES), lambda i: (i % nb, 0)),
            pl.BlockSpec((tm, LANES), lambda i: (i % nb, 0)),
        ] + [pl.BlockSpec((FOLD_CHUNK, FOLD_CHUNK), lambda i: (0, 0))] * len(FOLD_DILS),
        out_specs=[fold_spec(1)] + [fold_spec(d) for d in FOLD_DILS],
        out_shape=[jax.ShapeDtypeStruct((b, d, t // d, 3 * SLAB), BF16) for d in (1,) + FOLD_DILS],
        compiler_params=_cparams(("parallel",)),
        name="rope_qkv",
    )(proj, proj, cos, sin, *[fold_permutation(d) for d in FOLD_DILS])
    by_dil = dict(zip((1,) + FOLD_DILS, outs))
    return [by_dil[d] for _, d in DIL_PAIRS]


def _band_kernel(q_ref, k_ref, v_ref, o_ref, lse_ref, *, sub, half, span):
    tq = DIL_QBLOCK
    blocks = q_ref.shape[0] // tq
    for blk in range(blocks):
        qb = pl.program_id(2) * blocks + blk
        rows = slice(blk * tq, (blk + 1) * tq)
        start = jnp.clip(qb * tq - half, 0, sub - span)
        start = pl.multiple_of(start, half)
        q_pos = qb * tq + lax.broadcasted_iota(jnp.int32, (2 * tq, span), 0) % tq
        k_pos = start + lax.broadcasted_iota(jnp.int32, (2 * tq, span), 1)
        valid = jnp.abs(k_pos - q_pos) <= half
        pair_cols = [slice(pair * HEAD_PAIR, (pair + 1) * HEAD_PAIR) for pair in range(N_PAIRS)]
        scores = [jnp.where(valid, _pair_scores(q_ref[rows, cols], k_ref[pl.ds(start, span), cols]), NEG_INF)
                  for cols in pair_cols]
        for s, cols in zip(scores, pair_cols):
            o, lse = _pair_attend(s, v_ref[pl.ds(start, span), cols])
            o_ref[rows, cols] = o
            lse_ref[rows, cols] = lse


def band_attention(qkv, window, dil):
    b, _, sub, _ = qkv.shape
    half = window // (2 * dil)
    span = DIL_QBLOCK + 2 * half
    tq = DIL_QBLOCK * min(BAND_BLOCKS_PER_STEP, sub // DIL_QBLOCK)
    nqb = sub // tq
    return pl.pallas_call(
        functools.partial(_band_kernel, sub=sub, half=half, span=span),
        grid=(b, dil, nqb),
        in_specs=[
            pl.BlockSpec((None, None, tq, SLAB), lambda i, p, s: (i, p, s, 0)),
            pl.BlockSpec((None, None, sub, SLAB), lambda i, p, s: (i, p, 0, 1)),
            pl.BlockSpec((None, None, sub, SLAB), lambda i, p, s: (i, p, 0, 2)),
        ],
        out_specs=[pl.BlockSpec((None, None, tq, SLAB), lambda i, p, s: (i, p, s, 0))] * 2,
        out_shape=[jax.ShapeDtypeStruct((b, dil, sub, SLAB), F32)] * 2,
        compiler_params=_cparams(("parallel", "parallel", "arbitrary")),
        name="band_attention_d%d" % dil,
    )(qkv, qkv, qkv)


def _dil_combine_kernel(*refs):
    nbr = len(DIL_PAIRS)
    o_refs, l_refs, out_ref = refs[:nbr], refs[nbr:2 * nbr], refs[2 * nbr]
    scratch = iter(refs[2 * nbr + 1:])

    def token_order(ref):
        dil = ref.shape[0]
        if dil == 1:
            return ref[0]
        buf = next(scratch)
        per = ref.shape[1]
        for p in range(dil):
            for c in range(SLAB // LANES):
                buf[c, pl.ds(p, per, stride=dil), :] = ref[p, :, c * LANES:(c + 1) * LANES]
        return jnp.concatenate([buf[c] for c in range(SLAB // LANES)], axis=1)

    os = [token_order(r) for r in o_refs]
    lses = [token_order(r) for r in l_refs]
    m = functools.reduce(jnp.maximum, lses)
    ws = [jnp.exp(l - m) for l in lses]
    den = functools.reduce(jnp.add, ws)
    acc = functools.reduce(jnp.add, [(w / den) * o for w, o in zip(ws, os)])
    out_ref[...] = acc.astype(out_ref.dtype)


def dil_combine(outs, lses, tm):
    b, _, t, _ = outs[0].shape
    n = b * t
    nb = t // tm
    spec = lambda a: pl.BlockSpec((None, a.shape[1], tm // a.shape[1], SLAB), lambda i: (i // nb, 0, i % nb, 0))
    n_folded = sum(1 for a in outs + lses if a.shape[1] > 1)
    return pl.pallas_call(
        _dil_combine_kernel,
        grid=(n // tm,),
        in_specs=[spec(a) for a in outs + lses],
        out_specs=pl.BlockSpec((tm, SLAB), lambda i: (i, 0)),
        out_shape=jax.ShapeDtypeStruct((n, SLAB), BF16),
        scratch_shapes=[pltpu.VMEM((SLAB // LANES, tm, LANES), F32)] * n_folded,
        compiler_params=_cparams(("parallel",)),
        name="dil_combine",
    )(*outs, *lses)


MLA_QK = MLA_NOPE + MLA_ROPE


def mla_tables(t):
    ang = _rope_angles(t, MLA_ROPE)
    cos2 = np.concatenate([np.cos(ang), np.cos(ang)], axis=1)
    sin2 = np.concatenate([np.sin(ang), np.sin(ang)], axis=1)
    z = lambda w: np.zeros((t, w), np.float32)
    q_cos = np.concatenate([np.ones((t, MLA_NOPE), np.float32), cos2, z(LANES - MLA_QK)], axis=1)
    q_sin = np.concatenate([z(MLA_NOPE), sin2, z(LANES - MLA_QK)], axis=1)
    k_cos = np.concatenate([cos2, z(LANES - MLA_ROPE)], axis=1)
    k_sin = np.concatenate([-sin2[:, :MLA_ROPE // 2], sin2[:, MLA_ROPE // 2:], z(LANES - MLA_ROPE)], axis=1)
    return tuple(jnp.asarray(a, F32) for a in (q_cos, q_sin, k_cos, k_sin))


def mla_weights(w_uq, w_ukv):
    hq = w_uq.reshape(MLA_Q_RANK, MLA_HEADS, MLA_QK)
    nope, pe = hq[..., :MLA_NOPE], hq[..., MLA_NOPE:]
    pe_rot = jnp.concatenate([-pe[..., MLA_ROPE // 2:], pe[..., :MLA_ROPE // 2]], axis=-1)
    zq = jnp.zeros((MLA_Q_RANK, MLA_HEADS, LANES - MLA_QK), w_uq.dtype)
    w1 = jnp.concatenate([nope, pe, zq], axis=-1).reshape(MLA_Q_RANK, MLA_HEADS * LANES)
    w2 = jnp.concatenate([jnp.zeros_like(nope), pe_rot, zq], axis=-1).reshape(MLA_Q_RANK, MLA_HEADS * LANES)
    hkv = w_ukv.reshape(MLA_KV_RANK, MLA_HEADS, MLA_NOPE + MLA_V)
    k_nope, v = hkv[..., :MLA_NOPE], hkv[..., MLA_NOPE:]
    zk = jnp.zeros((MLA_KV_RANK, MLA_HEADS, LANES - MLA_NOPE), w_ukv.dtype)
    wk = jnp.concatenate([k_nope, zk], axis=-1).reshape(MLA_KV_RANK, MLA_HEADS * LANES)
    zv = jnp.zeros((MLA_KV_RANK, MLA_HEADS, LANES - MLA_V), w_ukv.dtype)
    wv = jnp.concatenate([v, zv], axis=-1).reshape(MLA_KV_RANK, MLA_HEADS * LANES)
    place = np.zeros((LANES, MLA_HEADS * LANES), np.float32)
    ones = np.zeros((1, MLA_HEADS * LANES), np.float32)
    for h in range(MLA_HEADS):
        place[np.arange(MLA_ROPE), h * LANES + MLA_NOPE + np.arange(MLA_ROPE)] = 1.0
        ones[0, h * LANES + MLA_V] = 1.0
    return (w1.astype(BF16), w2.astype(BF16), wk.astype(BF16), wv.astype(BF16), jnp.asarray(place, BF16),
            jnp.asarray(ones, F32))


def _mla_q_kernel(c_ref, nw_ref, w1_ref, w2_ref, cos_ref, sin_ref, o_ref):
    cn = _rms(c_ref[...].astype(F32), nw_ref[...]).astype(BF16)
    cos = jnp.tile(cos_ref[...], (1, MLA_HEADS))
    sin = jnp.tile(sin_ref[...], (1, MLA_HEADS))
    q = _dot(cn, w1_ref[...]) * cos + _dot(cn, w2_ref[...]) * sin
    o_ref[...] = (q * (MLA_QK ** -0.5 * math.log2(math.e))).astype(o_ref.dtype)


def _mla_kv_kernel(c_ref, tail_ref, nw_ref, wk_ref, wv_ref, place_ref, ones_ref, cos_ref, sin_ref,
                   k_ref, v_ref):
    cn = _rms(c_ref[...].astype(F32), nw_ref[...]).astype(BF16)
    kr = tail_ref[...]
    half = MLA_ROPE // 2
    rot = jnp.where(_lane_lt(kr.shape, half), pltpu.roll(kr, LANES - half, 1), pltpu.roll(kr, half, 1))
    k_pe = (kr * cos_ref[...] + rot * sin_ref[...]).astype(BF16)
    k_ref[...] = (_dot(cn, wk_ref[...]) + _dot(k_pe, place_ref[...])).astype(k_ref.dtype)
    v_ref[...] = (_dot(cn, wv_ref[...]) + ones_ref[...]).astype(v_ref.dtype)


def mla_project(proj, tail, q_norm_w, kv_norm_w, w_uq, w_ukv, b, t, tm):
    n = b * t
    nb = t // tm
    w1, w2, wk, wv, place, ones = mla_weights(w_uq, w_ukv)
    q_cos, q_sin, k_cos, k_sin = mla_tables(t)
    wide = MLA_HEADS * LANES
    full = lambda shape: pl.BlockSpec(shape, lambda i: (0, 0))
    tab = pl.BlockSpec((tm, LANES), lambda i: (i % nb, 0))
    qf = pl.pallas_call(
        _mla_q_kernel,
        grid=(n // tm,),
        in_specs=[pl.BlockSpec((tm, SLAB), lambda i: (i, COL_CQ)), full((1, MLA_Q_RANK)),
                  full((MLA_Q_RANK, wide)), full((MLA_Q_RANK, wide)), tab, tab],
        out_specs=pl.BlockSpec((tm, wide), lambda i: (i, 0)),
        out_shape=jax.ShapeDtypeStruct((n, wide), BF16),
        compiler_params=_cparams(("parallel",)),
        name="mla_q_proj",
    )(proj, q_norm_w.reshape(1, MLA_Q_RANK), w1, w2, q_cos, q_sin)
    kf, vf = pl.pallas_call(
        _mla_kv_kernel,
        grid=(n // tm,),
        in_specs=[pl.BlockSpec((tm, SLAB), lambda i: (i, COL_CKV)),
                  pl.BlockSpec((tm, LANES), lambda i: (i, 0)), full((1, MLA_KV_RANK)),
                  full((MLA_KV_RANK, wide)), full((MLA_KV_RANK, wide)), full((LANES, wide)), full((1, wide)),
                  tab, tab],
        out_specs=[pl.BlockSpec((tm, wide), lambda i: (i, 0))] * 2,
        out_shape=[jax.ShapeDtypeStruct((n, wide), BF16)] * 2,
        compiler_params=_cparams(("parallel",)),
        name="mla_kv_proj",
    )(proj, tail, kv_norm_w.reshape(1, MLA_KV_RANK), wk, wv, place, ones, k_cos, k_sin)
    return qf, kf, vf


def _mla_attn_kernel(q_ref, k_ref, v_ref, *rest, t, tk, n_cast):
    cast_in, o_ref, cast_out = rest[:n_cast], rest[n_cast], rest[n_cast + 1:]
    for src_ref, dst_ref in zip(cast_in, cast_out):
        dst_ref[...] = src_ref[...].astype(dst_ref.dtype)
    tq = q_ref.shape[0]
    groups = [slice(hh * LANES, (hh + 1) * LANES) for hh in range(2)]
    qs = [q_ref[:, grp] for grp in groups]

    def scores(c):
        return [_dot_nt(q, k_ref[c * tk:(c + 1) * tk, grp]) for q, grp in zip(qs, groups)]

    n_chunks = t // tk
    ms = [jnp.full((tq, 1), -jnp.inf, F32)] * 2
    accs = [jnp.zeros((tq, LANES), F32)] * 2
    s_next = scores(0)
    for c in range(n_chunks):
        s_cur = s_next
        if c + 1 < n_chunks:
            s_next = scores(c + 1)
        for hh, grp in enumerate(groups):
            m_new = jnp.maximum(ms[hh], jnp.max(s_cur[hh], axis=-1, keepdims=True))
            p = jnp.exp2((s_cur[hh] - m_new).astype(BF16))
            accs[hh] = jnp.exp2(ms[hh] - m_new) * accs[hh] + _dot(p, v_ref[c * tk:(c + 1) * tk, grp])
            ms[hh] = m_new
    outs = [acc / acc[:, MLA_V:MLA_V + 1] for acc in accs]
    first = _lane_lt((tq, LANES), MLA_V)
    o_ref[...] = jnp.where(first, outs[0], pltpu.roll(outs[1], MLA_V, 1)).astype(o_ref.dtype)


def mla_cast_rows(w, b, t, tq):
    steps = b * N_PAIRS * (t // tq)
    rows = int(np.prod(w.shape[:-1]))
    per = rows // steps
    return per if rows % steps == 0 and per % 16 == 0 else None


def mla_attention(qf, kf, vf, b, t, tq, tk, cast=()):
    n = b * t
    nq = t // tq
    step = lambda i, p, s: ((i * N_PAIRS + p) * nq + s, 0)
    cast2d = [w.reshape(-1, w.shape[-1]) for w in cast]
    cast_specs = [pl.BlockSpec((mla_cast_rows(w, b, t, tq), w2.shape[1]), step) for w, w2 in zip(cast, cast2d)]
    outs = pl.pallas_call(
        functools.partial(_mla_attn_kernel, t=t, tk=tk, n_cast=len(cast)),
        grid=(b, N_PAIRS, nq),
        in_specs=[
            pl.BlockSpec((tq, 2 * LANES), lambda i, p, s: (i * nq + s, p)),
            pl.BlockSpec((t, 2 * LANES), lambda i, p, s: (i, p)),
            pl.BlockSpec((t, 2 * LANES), lambda i, p, s: (i, p)),
        ] + cast_specs,
        out_specs=[pl.BlockSpec((tq, HEAD_PAIR), lambda i, p, s: (i * nq + s, p))] + cast_specs,
        out_shape=[jax.ShapeDtypeStruct((n, SLAB), BF16)]
        + [jax.ShapeDtypeStruct(w2.shape, BF16) for w2 in cast2d],
        compiler_params=_cparams(("parallel", "parallel", "arbitrary")),
        name="mla_attention",
    )(qf, kf, vf, *cast2d)
    return outs[0], [o.reshape(w.shape) for o, w in zip(outs[1:], cast)]


def _in_proj_segments():
    sizes = (SSM_INNER, SSM_CONV_CH, 2 * SSM_HEADS, SLAB, SLAB, SLAB, MLA_Q_RANK, MLA_KV_RANK, MLA_ROPE,
             SLAB, SLAB, SLAB)
    off = [int(v) for v in np.concatenate([[0], np.cumsum(sizes)])]
    main = ((off[0], off[2]), (off[3], off[8]), (off[9], off[12]))
    tail = ((off[8], off[9]), (off[2], off[3]))
    return main, tail


def _in_proj_columns():
    main, tail = _in_proj_segments()
    cols = lambda segs: np.concatenate([np.arange(a, b) for a, b in segs])
    return cols(main), cols(tail)


def in_proj_weights(w_in_l):
    main, tail = _in_proj_segments()
    w_main = jnp.concatenate([w_in_l[:, a:b] for a, b in main], axis=1).astype(BF16)
    pad = jnp.zeros((D_MODEL, LANES - sum(b - a for a, b in tail)), w_in_l.dtype)
    w_tail = jnp.concatenate([w_in_l[:, a:b] for a, b in tail] + [pad], axis=1).astype(BF16)
    return w_main, w_tail


MLA_TQ = 1024
MLA_CASTS_PER_CALL = 2


def mixers(proj, tail, p, l, b, t, cast):
    xbc = conv_silu(proj, p["conv_w"][l], p["conv_b"][l], b, t)
    y_f, y_b = ssd_scan(xbc, tail, p["dt_bias"][l], p["a_log"][l], b, t)
    y_ssm = ssd_combine(y_f, y_b, xbc, proj, p["d_skip"][l], p["ssm_norm_w"][l], 1024)

    y_na = na_attention(proj, p["na_rpb"][l], b, t)

    qf, kf, vf = mla_project(proj, tail, p["mla_q_norm_w"][l], p["mla_kv_norm_w"][l],
                             p["mla_w_uq"][l], p["mla_w_ukv"][l], b, t, 512)
    y_mla, cast_out = mla_attention(qf, kf, vf, b, t, MLA_TQ, 512, cast)

    qkvs = rope_qkv(proj, b, t, 1024)
    outs, lses = zip(*[band_attention(qkv, w, d) for qkv, (w, d) in zip(qkvs, DIL_PAIRS)])
    y_dil = dil_combine(outs, lses, 1024)
    return (y_ssm, y_na, y_mla, y_dil), cast_out


def kernel(x, attn_norm_w, w_in, conv_w, conv_b, a_log, dt_bias, d_skip, ssm_norm_w, na_rpb,
           mla_q_norm_w, mla_kv_norm_w, mla_w_uq, mla_w_ukv, w_o, ffn_norm_w, ffn_w_gate, ffn_w_up,
           ffn_w_down, router_w, exp_w_gate, exp_w_up, exp_w_down, final_norm_w):
    b, t, _ = x.shape
    n = b * t
    depth = w_in.shape[0]
    p = dict(conv_w=conv_w, conv_b=conv_b, a_log=a_log, dt_bias=dt_bias, d_skip=d_skip,
             ssm_norm_w=ssm_norm_w, na_rpb=na_rpb, mla_q_norm_w=mla_q_norm_w,
             mla_kv_norm_w=mla_kv_norm_w, mla_w_uq=mla_w_uq, mla_w_ukv=mla_w_ukv)
    x = x.reshape(n, D_MODEL)
    cast_rows = 256
    w_o_b = cast_bf16(w_o, cast_rows)
    ffn_b = [cast_bf16(w, cast_rows) for w in (ffn_w_gate, ffn_w_up, ffn_w_down)]
    exp_f32 = [exp_w_gate, exp_w_up, exp_w_down]
    exp_b = [None] * len(exp_f32)
    pending = [k for k, w in enumerate(exp_f32) if mla_cast_rows(w, b, t, MLA_TQ) is not None]
    moe_tm = 512
    normed = False
    for l in range(depth):
        w_main, w_tail = in_proj_weights(w_in[l])
        proj, tail = in_proj(x, attn_norm_w[l], w_main, w_tail, 512, PROJ_MAIN // 2)
        jobs, pending = pending[:MLA_CASTS_PER_CALL], pending[MLA_CASTS_PER_CALL:]
        mix, cast_out = mixers(proj, tail, p, l, b, t, [exp_f32[k] for k in jobs])
        for k, w_b in zip(jobs, cast_out):
            exp_b[k] = w_b
        x = out_proj(mix, w_o_b, l, x, 1024, 1024)
        j = l // 2
        if l % 2 == 0:
            x = ffn_dense(x, ffn_norm_w[l], *ffn_b, j, 1024, 512)
        else:
            pending = []
            exp_b = [cast_bf16(w, cast_rows) if w_b is None else w_b for w, w_b in zip(exp_f32, exp_b)]
            top_i, gates = moe_router(x, ffn_norm_w[l], router_w[j], 512)
            src, pos, tile_expert, tile_valid = moe_plan(top_i, moe_tm)
            y = moe_ffn(x, ffn_norm_w[l], *exp_b, j, src, tile_expert, tile_valid, moe_tm, 512)
            last = l == depth - 1
            x = moe_combine(x, gates, y, pos, final_norm_w if last else None, 256)
            normed = last
    if not normed:
        x = rmsnorm_rows(x, final_norm_w, 1024)
    return x.reshape(b, t, D_MODEL)
```

```python
import functools
import math

import numpy as np
import jax
import jax.numpy as jnp
from jax import lax
from jax.experimental import pallas as pl
from jax.experimental.pallas import tpu as pltpu

F32 = jnp.float32
BF16 = jnp.bfloat16

D_MODEL = 2048
GRID_W = 64
HEAD_DIM = 64
ROPE_THETA = 10000.0
NORM_EPS = 1e-6
NEG_INF = -1e30

SSM_HEADS = 8
SSM_HEAD_DIM = 64
SSM_INNER = SSM_HEADS * SSM_HEAD_DIM
SSM_GROUPS = 2
SSM_STATE = 128
SSM_CONV = 5
SSM_CHUNK = 128
SSM_CONV_CH = SSM_INNER + 2 * SSM_GROUPS * SSM_STATE

NA_HEADS = 8
NA_WIN_ROWS = 8
NA_WIN_COLS = 16
NA_COL_BLOCK = 16
NA_KEY_COLS = 32
NA_ROWS_PER_STEP = 4

MLA_HEADS = 8
MLA_Q_RANK = 512
MLA_KV_RANK = 512
MLA_NOPE = 64
MLA_ROPE = 32
MLA_V = 64

DIL_HEADS = 8
DIL_PAIRS = ((128, 1), (512, 4), (2048, 16))
DIL_QBLOCK = 128
BAND_BLOCKS_PER_STEP = 4

N_EXPERTS = 8
TOP_K = 2
SPLIT_PARTS = 3

LANES = 128
HEAD_PAIR = 2 * HEAD_DIM
N_PAIRS = 4
SLAB = 512

COL_Z, COL_XBC, COL_NAQ, COL_NAK, COL_NAV, COL_CQ, COL_CKV, COL_DLQ, COL_DLK, COL_DLV = (
    0, 1, 3, 4, 5, 6, 7, 8, 9, 10)
PROJ_MAIN = 11 * SLAB
TAIL_DT = 32

VMEM_LIMIT = 56 * 1024 * 1024


def _cparams(sem, vmem=VMEM_LIMIT):
    return pltpu.CompilerParams(dimension_semantics=sem, vmem_limit_bytes=vmem)


def _lane_lt(shape, bound, period=None):
    lane = lax.broadcasted_iota(jnp.int32, shape, len(shape) - 1)
    if period is not None:
        lane = lane % period
    return lane < bound


def _rms(x, w):
    ms = jnp.mean(x * x, axis=-1, keepdims=True)
    return x * lax.rsqrt(ms + NORM_EPS) * w


def _dot(a, b):
    return jnp.dot(a, b, preferred_element_type=F32)


def _dot_nt(a, b):
    return lax.dot_general(a, b, (((1,), (1,)), ((), ())), preferred_element_type=F32)


def _dot_tn(a, b):
    return lax.dot_general(a, b, (((0,), (0,)), ((), ())), preferred_element_type=F32)


def _cast_kernel(x_ref, o_ref):
    o_ref[...] = x_ref[...].astype(o_ref.dtype)


def cast_bf16(w, tr):
    shape = w.shape
    w2 = w.reshape(-1, shape[-1])
    r, c = w2.shape
    out = pl.pallas_call(
        _cast_kernel,
        grid=(r // tr,),
        in_specs=[pl.BlockSpec((tr, c), lambda i: (i, 0))],
        out_specs=pl.BlockSpec((tr, c), lambda i: (i, 0)),
        out_shape=jax.ShapeDtypeStruct((r, c), BF16),
        compiler_params=_cparams(("parallel",)),
        name="cast_bf16",
    )(w2)
    return out.reshape(shape)


def _in_proj_kernel(x_ref, nw_ref, w_ref, wt_ref, o_ref, t_ref, h_ref):
    @pl.when(pl.program_id(1) == 0)
    def _():
        h = _rms(x_ref[...], nw_ref[...]).astype(BF16)
        h_ref[...] = h
        t_ref[...] = _dot(h, wt_ref[...])

    o_ref[...] = _dot(h_ref[...], w_ref[...]).astype(o_ref.dtype)


def in_proj(x, nw, w_main, w_tail, tm, tn):
    n, k = x.shape
    nout = w_main.shape[1]
    return pl.pallas_call(
        _in_proj_kernel,
        grid=(n // tm, nout // tn),
        in_specs=[
            pl.BlockSpec((tm, k), lambda i, j: (i, 0)),
            pl.BlockSpec((1, k), lambda i, j: (0, 0)),
            pl.BlockSpec((k, tn), lambda i, j: (0, j)),
            pl.BlockSpec((k, LANES), lambda i, j: (0, 0)),
        ],
        out_specs=[pl.BlockSpec((tm, tn), lambda i, j: (i, j)), pl.BlockSpec((tm, LANES), lambda i, j: (i, 0))],
        out_shape=[jax.ShapeDtypeStruct((n, nout), BF16), jax.ShapeDtypeStruct((n, LANES), F32)],
        scratch_shapes=[pltpu.VMEM((tm, k), BF16)],
        compiler_params=_cparams(("parallel", "arbitrary")),
        name="in_proj",
    )(x, nw.reshape(1, k), w_main, w_tail)


def _out_proj_kernel(a0_ref, a1_ref, a2_ref, a3_ref, w_ref, r_ref, o_ref):
    acc = r_ref[...]
    for s, a_ref in enumerate((a0_ref, a1_ref, a2_ref, a3_ref)):
        acc = acc + _dot(a_ref[...], w_ref[s * SLAB:(s + 1) * SLAB, :])
    o_ref[...] = acc


def out_proj(mix, w, layer, res, tm, tn):
    n = res.shape[0]
    return pl.pallas_call(
        _out_proj_kernel,
        grid=(n // tm, D_MODEL // tn),
        in_specs=[pl.BlockSpec((tm, SLAB), lambda i, j: (i, 0))] * 4 + [
            pl.BlockSpec((None, 4 * SLAB, tn), lambda i, j: (layer, 0, j)),
            pl.BlockSpec((tm, tn), lambda i, j: (i, j)),
        ],
        out_specs=pl.BlockSpec((tm, tn), lambda i, j: (i, j)),
        out_shape=jax.ShapeDtypeStruct((n, D_MODEL), F32),
        compiler_params=_cparams(("parallel", "arbitrary")),
        name="out_proj",
    )(*mix, w, res)


def _ffn_kernel(x_ref, nw_ref, wg_ref, wu_ref, wd_ref, o_ref, h_ref):
    @pl.when(pl.program_id(1) == 0)
    def _():
        x = x_ref[...]
        h_ref[...] = _rms(x, nw_ref[...]).astype(BF16)
        o_ref[...] = x

    _swiglu_rows(h_ref, wg_ref, wu_ref, wd_ref, o_ref)


FFN_ROW_CHUNK = 512


def _swiglu_rows(h_ref, wg_ref, wu_ref, wd_ref, o_ref):
    tm = h_ref.shape[0]
    for r0 in range(0, tm, FFN_ROW_CHUNK):
        rows = slice(r0, r0 + FFN_ROW_CHUNK)
        h = h_ref[rows, :]
        g = _dot(h, wg_ref[...])
        u = _dot(h, wu_ref[...])
        a = (g * jax.nn.sigmoid(g) * u).astype(BF16)
        o_ref[rows, :] += _dot(a, wd_ref[...])


def ffn_dense(x, nw, wg, wu, wd, layer, tm, tf):
    n = x.shape[0]
    d_ff = wg.shape[-1]
    return pl.pallas_call(
        _ffn_kernel,
        grid=(n // tm, d_ff // tf),
        in_specs=[
            pl.BlockSpec((tm, D_MODEL), lambda i, j: (i, 0)),
            pl.BlockSpec((1, D_MODEL), lambda i, j: (0, 0)),
            pl.BlockSpec((None, D_MODEL, tf), lambda i, j: (layer, 0, j)),
            pl.BlockSpec((None, D_MODEL, tf), lambda i, j: (layer, 0, j)),
            pl.BlockSpec((None, tf, D_MODEL), lambda i, j: (layer, j, 0)),
        ],
        out_specs=pl.BlockSpec((tm, D_MODEL), lambda i, j: (i, 0)),
        out_shape=jax.ShapeDtypeStruct((n, D_MODEL), F32),
        scratch_shapes=[pltpu.VMEM((tm, D_MODEL), BF16)],
        compiler_params=_cparams(("parallel", "arbitrary")),
        name="ffn_dense",
    )(x, nw.reshape(1, D_MODEL), wg, wu, wd)


def _router_kernel(x_ref, nw_ref, rw_ref, idx_ref, gate_ref):
    h = _rms(x_ref[...], nw_ref[...])
    acc = jnp.zeros((h.shape[0], LANES), F32)
    rem = h
    for _ in range(SPLIT_PARTS):
        part = rem.astype(BF16)
        acc = acc + _dot(part, rw_ref[...])
        rem = rem - part.astype(F32)
    logits = acc
    for k in range(1, SPLIT_PARTS):
        logits = logits + pltpu.roll(acc, LANES - k * N_EXPERTS, 1)
    lane = lax.broadcasted_iota(jnp.int32, logits.shape, 1)
    logits = jnp.where(lane < N_EXPERTS, logits, -jnp.inf)
    m1 = jnp.max(logits, axis=-1, keepdims=True)
    i1 = jnp.min(jnp.where(logits == m1, lane, LANES), axis=-1, keepdims=True)
    rest = jnp.where(lane == i1, -jnp.inf, logits)
    m2 = jnp.max(rest, axis=-1, keepdims=True)
    i2 = jnp.min(jnp.where(rest == m2, lane, LANES), axis=-1, keepdims=True)
    e2 = jnp.exp(m2 - m1)
    g1 = 1.0 / (1.0 + e2)
    g2 = e2 / (1.0 + e2)
    idx_ref[...] = jnp.where(lane == 0, i1, i2)[:, :TOP_K]
    gate_ref[...] = jnp.where(lane == 0, g1, g2)[:, :TOP_K]


def moe_router(x, nw, router_w, tm):
    n = x.shape[0]
    parts, rem = [], router_w
    for _ in range(SPLIT_PARTS):
        parts.append(rem.astype(BF16))
        rem = rem - parts[-1].astype(F32)
    pad = jnp.zeros((D_MODEL, LANES - SPLIT_PARTS * N_EXPERTS), BF16)
    rw = jnp.concatenate(parts + [pad], axis=1)
    return pl.pallas_call(
        _router_kernel,
        grid=(n // tm,),
        in_specs=[
            pl.BlockSpec((tm, D_MODEL), lambda i: (i, 0)),
            pl.BlockSpec((1, D_MODEL), lambda i: (0, 0)),
            pl.BlockSpec((D_MODEL, LANES), lambda i: (0, 0)),
        ],
        out_specs=[pl.BlockSpec((tm, TOP_K), lambda i: (i, 0)),
                   pl.BlockSpec((tm, TOP_K), lambda i: (i, 0))],
        out_shape=[jax.ShapeDtypeStruct((n, TOP_K), jnp.int32),
                   jax.ShapeDtypeStruct((n, TOP_K), F32)],
        compiler_params=_cparams(("parallel",)),
        name="moe_router",
    )(x, nw.reshape(1, D_MODEL), rw)


def moe_plan(top_i, tm):
    n = top_i.shape[0]
    flat_e = top_i.reshape(-1)
    onehot = (flat_e[:, None] == jnp.arange(N_EXPERTS, dtype=jnp.int32)[None, :]).astype(jnp.int32)
    csum = jnp.cumsum(onehot, axis=0)
    counts = csum[-1]
    rank = jnp.sum(onehot * csum, axis=1) - 1
    padded = ((counts + tm - 1) // tm) * tm
    pend = jnp.cumsum(padded)
    pstart = pend - padded
    pos = pstart[flat_e] + rank
    n_slots = n * TOP_K + N_EXPERTS * tm
    n_tiles = n_slots // tm
    src = jnp.zeros((n_slots,), jnp.int32).at[pos].set(jnp.arange(n * TOP_K, dtype=jnp.int32) // TOP_K)
    tile_start = jnp.arange(n_tiles, dtype=jnp.int32) * tm
    tile_expert = jnp.sum((tile_start[:, None] >= pend[None, :]).astype(jnp.int32), axis=1)
    tile_valid = (tile_start < pend[-1]).astype(jnp.int32)
    last_valid = jnp.maximum(pend[-1] // tm - 1, 0)
    tile_expert = jnp.where(tile_valid == 1, tile_expert, tile_expert[last_valid]).astype(jnp.int32)
    return src, pos.reshape(n, TOP_K).astype(jnp.int32), tile_expert, tile_valid


def _moe_ffn_kernel(te_ref, tv_ref, src_ref, nsrc_ref, x_hbm, nw_ref, wg_ref, wu_ref, wd_ref, y_ref,
                    xbuf, h_ref, sem, *, tm, rows_per_step):
    i = pl.program_id(0)
    j = pl.program_id(1)
    issued = xbuf.shape[0]
    valid = tv_ref[i] == 1
    has_rows = jnp.logical_or(i == 0, tv_ref[jnp.maximum(i - 1, 0)] == 1)

    def start_row(idx_ref, row):
        tok = idx_ref[0, 0, jnp.minimum(row, tm - 1)]
        pltpu.make_async_copy(x_hbm.at[pl.ds(tok, 1)], xbuf.at[pl.ds(row, 1)], sem).start(priority=1)

    @pl.when(j == 0)
    def _():
        y_ref[...] = jnp.zeros_like(y_ref)

    @pl.when(jnp.logical_and(j == 0, i == 0))
    def _():
        def start(r, c):
            start_row(src_ref, r)
            return c

        lax.fori_loop(0, issued, start, 0)

    @pl.when(jnp.logical_and(j == 0, has_rows))
    def _():
        pltpu.make_async_copy(x_hbm.at[pl.ds(0, issued)], xbuf.at[pl.ds(0, issued)], sem).wait()
        h_ref[...] = _rms(xbuf[0:tm, :], nw_ref[...]).astype(BF16)

    @pl.when(valid)
    def _():
        for r in range(rows_per_step):
            start_row(nsrc_ref, j * rows_per_step + r)
        _swiglu_rows(h_ref, wg_ref, wu_ref, wd_ref, y_ref)


def moe_ffn(x, nw, wg, wu, wd, layer, src, tile_expert, tile_valid, tm, tf):
    n_slots = src.shape[0]
    n_tiles = n_slots // tm + 1
    d_ff = wg.shape[-1]
    nf = d_ff // tf
    sublanes = 8
    rows_per_step = -(-tm // (nf * sublanes)) * sublanes
    buf_rows = rows_per_step * nf
    tile_expert = jnp.concatenate([tile_expert, tile_expert[-1:]])
    tile_valid = jnp.concatenate([tile_valid, jnp.zeros((1,), tile_valid.dtype)])
    src3 = jnp.concatenate([src, jnp.zeros((tm,), src.dtype)]).reshape(n_tiles, 1, tm)

    def wcol(i, j, te_ref, tv_ref):
        return (layer, te_ref[i], 0, jnp.where(tv_ref[i] == 1, j, nf - 1))

    def wrow(i, j, te_ref, tv_ref):
        return (layer, te_ref[i], jnp.where(tv_ref[i] == 1, j, nf - 1), 0)

    grid_spec = pltpu.PrefetchScalarGridSpec(
        num_scalar_prefetch=2,
        grid=(n_tiles, nf),
        in_specs=[
            pl.BlockSpec((1, 1, tm), lambda i, j, *_: (i, 0, 0), memory_space=pltpu.SMEM),
            pl.BlockSpec((1, 1, tm), lambda i, j, *_: (jnp.minimum(i + 1, n_tiles - 1), 0, 0),
                         memory_space=pltpu.SMEM),
            pl.BlockSpec(memory_space=pl.ANY),
            pl.BlockSpec((1, D_MODEL), lambda i, j, *_: (0, 0)),
            pl.BlockSpec((None, None, D_MODEL, tf), wcol),
            pl.BlockSpec((None, None, D_MODEL, tf), wcol),
            pl.BlockSpec((None, None, tf, D_MODEL), wrow),
        ],
        out_specs=pl.BlockSpec((tm, D_MODEL), lambda i, j, *_: (i, 0)),
        scratch_shapes=[pltpu.VMEM((buf_rows, D_MODEL), F32), pltpu.VMEM((tm, D_MODEL), BF16),
                        pltpu.SemaphoreType.DMA],
    )
    return pl.pallas_call(
        functools.partial(_moe_ffn_kernel, tm=tm, rows_per_step=rows_per_step),
        grid_spec=grid_spec,
        out_shape=jax.ShapeDtypeStruct((n_tiles * tm, D_MODEL), F32),
        compiler_params=_cparams(("arbitrary", "arbitrary")),
        name="moe_ffn",
    )(tile_expert, tile_valid, src3, src3, x, nw.reshape(1, D_MODEL), wg, wu, wd)


def _moe_combine_kernel(pos_ref, npos_ref, x_ref, gate_ref, y_hbm, fw_ref, o_ref, ybuf, sem, *, tm, final_norm):
    i = pl.program_id(0)
    n_tiles = pl.num_programs(0)
    cur = i % 2

    def row_copy(idx_ref, buf, r, k):
        slot = idx_ref[0, 0, r * TOP_K + k]
        return pltpu.make_async_copy(y_hbm.at[pl.ds(slot, 1)], ybuf.at[buf, k, pl.ds(r, 1)], sem.at[buf])

    def gather(idx_ref, buf):
        def start(r, c):
            for k in range(TOP_K):
                row_copy(idx_ref, buf, r, k).start(priority=k % 2)
            return c

        lax.fori_loop(0, tm, start, 0, unroll=8)

    @pl.when(i == 0)
    def _():
        gather(pos_ref, 0)

    @pl.when(i + 1 < n_tiles)
    def _():
        gather(npos_ref, 1 - cur)

    for k in range(TOP_K):
        pltpu.make_async_copy(y_hbm.at[pl.ds(0, tm)], ybuf.at[cur, k], sem.at[cur]).wait()
    gates = gate_ref[...]
    out = x_ref[...]
    for k in range(TOP_K):
        out = out + gates[:, k:k + 1] * ybuf[cur, k]
    if final_norm:
        out = _rms(out, fw_ref[...])
    o_ref[...] = out


def moe_combine(x, gates, y, pos, final_w, tm):
    n = x.shape[0]
    final_norm = final_w is not None
    fw = final_w if final_norm else jnp.ones((D_MODEL,), F32)
    n_tiles = n // tm
    pos3 = pos.reshape(n_tiles, 1, tm * TOP_K)
    return pl.pallas_call(
        functools.partial(_moe_combine_kernel, tm=tm, final_norm=final_norm),
        grid=(n_tiles,),
        in_specs=[
            pl.BlockSpec((1, 1, tm * TOP_K), lambda i: (i, 0, 0), memory_space=pltpu.SMEM),
            pl.BlockSpec((1, 1, tm * TOP_K), lambda i: (jnp.minimum(i + 1, n_tiles - 1), 0, 0),
                         memory_space=pltpu.SMEM),
            pl.BlockSpec((tm, D_MODEL), lambda i: (i, 0)),
            pl.BlockSpec((tm, TOP_K), lambda i: (i, 0)),
            pl.BlockSpec(memory_space=pl.ANY),
            pl.BlockSpec((1, D_MODEL), lambda i: (0, 0)),
        ],
        out_specs=pl.BlockSpec((tm, D_MODEL), lambda i: (i, 0)),
        out_shape=jax.ShapeDtypeStruct((n, D_MODEL), F32),
        scratch_shapes=[pltpu.VMEM((2, TOP_K, tm, D_MODEL), F32), pltpu.SemaphoreType.DMA((2,))],
        compiler_params=_cparams(("arbitrary",)),
        name="moe_combine",
    )(pos3, pos3, x, gates, y, fw.reshape(1, D_MODEL))


def _rmsnorm_kernel(x_ref, w_ref, o_ref):
    o_ref[...] = _rms(x_ref[...], w_ref[...])


def rmsnorm_rows(x, w, tm):
    n = x.shape[0]
    return pl.pallas_call(
        _rmsnorm_kernel,
        grid=(n // tm,),
        in_specs=[pl.BlockSpec((tm, D_MODEL), lambda i: (i, 0)),
                  pl.BlockSpec((1, D_MODEL), lambda i: (0, 0))],
        out_specs=pl.BlockSpec((tm, D_MODEL), lambda i: (i, 0)),
        out_shape=jax.ShapeDtypeStruct((n, D_MODEL), F32),
        compiler_params=_cparams(("parallel",)),
        name="final_norm",
    )(x, w.reshape(1, D_MODEL))


CONV_PAD = 8


def _conv_kernel(x_ref, w_ref, b_ref, o_ref, pad_ref, *, t):
    half = SSM_CONV // 2
    zeros = jnp.zeros((CONV_PAD, pad_ref.shape[1]), F32)
    pad_ref[0:CONV_PAD, :] = zeros
    pad_ref[CONV_PAD + t:CONV_PAD + t + CONV_PAD, :] = zeros
    pad_ref[CONV_PAD:CONV_PAD + t, :] = x_ref[...].astype(F32)
    acc = jnp.zeros(o_ref.shape, F32) + b_ref[...]
    for k in range(SSM_CONV):
        acc = acc + pad_ref[pl.ds(CONV_PAD - half + k, t), :] * w_ref[k:k + 1, :]
    o_ref[...] = (acc * jax.nn.sigmoid(acc)).astype(o_ref.dtype)


def conv_silu(proj, conv_w, conv_b, b, t):
    tc = 256
    nblk = SSM_CONV_CH // tc
    col0 = COL_XBC * SLAB // tc
    return pl.pallas_call(
        functools.partial(_conv_kernel, t=t),
        grid=(b, nblk),
        in_specs=[
            pl.BlockSpec((t, tc), lambda i, j: (i, col0 + j)),
            pl.BlockSpec((SSM_CONV, tc), lambda i, j: (0, j)),
            pl.BlockSpec((1, tc), lambda i, j: (0, j)),
        ],
        out_specs=pl.BlockSpec((t, tc), lambda i, j: (i, j)),
        out_shape=jax.ShapeDtypeStruct((b * t, SSM_CONV_CH), BF16),
        scratch_shapes=[pltpu.VMEM((t + 2 * CONV_PAD, tc), F32)],
        compiler_params=_cparams(("parallel", "parallel")),
        name="conv_silu",
    )(proj, conv_w, conv_b.reshape(1, SSM_CONV_CH))


def _ssd_stage1(xbc_ref, tail_ref, bias_ref, alog_ref, tri_ref, sel_ref, state_ref, direction):
    q = SSM_CHUNK
    dt = jax.nn.softplus(tail_ref[...] + bias_ref[...])
    da = dt * (-jnp.exp(alog_ref[...]))
    cs = jnp.dot(tri_ref[...], da, preferred_element_type=F32, precision=lax.Precision.HIGHEST)
    total = cs[q - 1:q, :]
    if direction == 0:
        e_out = cs
        e_in = total - cs
        e_seg = cs
    else:
        ex = cs - da
        e_out = total - ex
        e_in = ex
        e_seg = -ex
    dec_out_b = jnp.exp(e_out).astype(BF16)
    dec_in_dt_b = (jnp.exp(e_in) * dt).astype(BF16)
    pairs_per_group = N_PAIRS // SSM_GROUPS
    cbs, y_offs, in_scales = [], [], []
    for g in range(SSM_GROUPS):
        bm = xbc_ref[:, SSM_INNER + g * SSM_STATE:SSM_INNER + (g + 1) * SSM_STATE]
        cm = xbc_ref[:, SSM_INNER + (SSM_GROUPS + g) * SSM_STATE:SSM_INNER + (SSM_GROUPS + g + 1) * SSM_STATE]
        cbs.append(_dot_nt(cm, bm))
        for pair in range(g * pairs_per_group, (g + 1) * pairs_per_group):
            sel = sel_ref[direction, pair]
            y_offs.append(_dot(dec_out_b, sel) * _dot(cm, state_ref[pair].astype(BF16)))
            in_scales.append(_dot(dec_in_dt_b, sel))
    return dict(e_seg=e_seg, e_seg_t=jnp.transpose(e_seg), dt_t=jnp.transpose(dt), dec_tot=jnp.exp(total),
                cbs=cbs, y_offs=y_offs, in_scales=in_scales)


def _ssd_stage2(ctx, xbc_ref, y_ref, state_ref, direction):
    q = SSM_CHUNK
    row = lax.broadcasted_iota(jnp.int32, (q, q), 0)
    col = lax.broadcasted_iota(jnp.int32, (q, q), 1)
    keep = (row >= col) if direction == 0 else (col >= row)
    first_half = _lane_lt((q, HEAD_PAIR), HEAD_DIM)
    pairs_per_group = N_PAIRS // SSM_GROUPS
    for pair in range(N_PAIRS):
        g = pair // pairs_per_group
        bm = xbc_ref[:, SSM_INNER + g * SSM_STATE:SSM_INNER + (g + 1) * SSM_STATE]
        xs_pair = xbc_ref[:, pair * HEAD_PAIR:(pair + 1) * HEAD_PAIR]
        ys, decs = [], []
        for hh in range(2):
            lane = TAIL_DT + direction * SSM_HEADS + pair * 2 + hh
            seg = ctx["e_seg"][:, lane:lane + 1] - ctx["e_seg_t"][lane:lane + 1, :]
            lmat = jnp.where(keep, jnp.exp(seg), 0.0)
            w = (ctx["cbs"][g] * lmat * ctx["dt_t"][lane:lane + 1, :]).astype(BF16)
            ys.append(_dot(w, xs_pair))
            decs.append(ctx["dec_tot"][:, lane:lane + 1])
        y_ref[:, pair * HEAD_PAIR:(pair + 1) * HEAD_PAIR] = jnp.where(first_half, ys[0], ys[1]) + ctx["y_offs"][pair]
        st = state_ref[pair]
        first_half_s = _lane_lt(st.shape, HEAD_DIM)
        state_ref[pair] = (st * jnp.where(first_half_s, decs[0], decs[1])
                           + _dot_tn(bm, (xs_pair * ctx["in_scales"][pair]).astype(BF16)))


def _ssd_kernel(xbc_f_ref, tail_f_ref, xbc_b_ref, tail_b_ref, bias_ref, alog_ref, tri_ref, sel_ref,
                y_f_ref, y_b_ref, state_ref):
    @pl.when(pl.program_id(1) == 0)
    def _():
        state_ref[...] = jnp.zeros_like(state_ref)

    ins = ((xbc_f_ref, tail_f_ref, y_f_ref), (xbc_b_ref, tail_b_ref, y_b_ref))
    ctxs = [_ssd_stage1(xbc_ref, tail_ref, bias_ref, alog_ref, tri_ref, sel_ref, state_ref.at[d], d)
            for d, (xbc_ref, tail_ref, _) in enumerate(ins)]
    for d, (xbc_ref, _, y_ref) in enumerate(ins):
        _ssd_stage2(ctxs[d], xbc_ref, y_ref, state_ref.at[d], d)


def _ssd_lane_selectors():
    sel = np.zeros((2, N_PAIRS, LANES, HEAD_PAIR), np.float32)
    for d in range(2):
        for pair in range(N_PAIRS):
            for hh in range(2):
                sel[d, pair, TAIL_DT + d * SSM_HEADS + pair * 2 + hh, hh * HEAD_DIM:(hh + 1) * HEAD_DIM] = 1.0
    return jnp.asarray(sel, BF16)


def ssd_scan(xbc, tail, dt_bias, a_log, b, t):
    q = SSM_CHUNK
    nc = t // q
    bias_row = jnp.zeros((1, LANES), F32).at[0, TAIL_DT:TAIL_DT + 2 * SSM_HEADS].set(dt_bias.reshape(-1))
    alog_row = jnp.zeros((1, LANES), F32).at[0, TAIL_DT:TAIL_DT + 2 * SSM_HEADS].set(a_log.reshape(-1))
    tri = jnp.asarray(np.tril(np.ones((q, q), np.float32)))
    fwd = lambda i, c: (i * nc + c, 0)
    bwd = lambda i, c: (i * nc + nc - 1 - c, 0)
    const = lambda i, c: (0, 0)
    return pl.pallas_call(
        _ssd_kernel,
        grid=(b, nc),
        in_specs=[
            pl.BlockSpec((q, SSM_CONV_CH), fwd), pl.BlockSpec((q, LANES), fwd),
            pl.BlockSpec((q, SSM_CONV_CH), bwd), pl.BlockSpec((q, LANES), bwd),
            pl.BlockSpec((1, LANES), const), pl.BlockSpec((1, LANES), const), pl.BlockSpec((q, q), const),
            pl.BlockSpec((2, N_PAIRS, LANES, HEAD_PAIR), lambda i, c: (0, 0, 0, 0)),
        ],
        out_specs=[pl.BlockSpec((q, SSM_INNER), fwd), pl.BlockSpec((q, SSM_INNER), bwd)],
        out_shape=[jax.ShapeDtypeStruct((b * t, SSM_INNER), F32)] * 2,
        scratch_shapes=[pltpu.VMEM((2, N_PAIRS, SSM_STATE, HEAD_PAIR), F32)],
        compiler_params=_cparams(("parallel", "arbitrary")),
        name="ssd_scan",
    )(xbc, tail, xbc, tail, bias_row, alog_row, tri, _ssd_lane_selectors())


def _ssd_combine_kernel(yf_ref, yb_ref, xs_ref, z_ref, d_ref, nw_ref, o_ref):
    y = yf_ref[...] + yb_ref[...] + xs_ref[...].astype(F32) * d_ref[...]
    z = z_ref[...].astype(F32)
    o_ref[...] = _rms(y * (z * jax.nn.sigmoid(z)), nw_ref[...]).astype(o_ref.dtype)


def ssd_combine(y_f, y_b, xbc, proj, d_skip, norm_w, tm):
    n = y_f.shape[0]
    d_row = jnp.repeat(d_skip, SSM_HEAD_DIM).reshape(1, SSM_INNER)
    row = lambda i: (i, 0)
    return pl.pallas_call(
        _ssd_combine_kernel,
        grid=(n // tm,),
        in_specs=[
            pl.BlockSpec((tm, SSM_INNER), row),
            pl.BlockSpec((tm, SSM_INNER), row),
            pl.BlockSpec((tm, SSM_INNER), row),
            pl.BlockSpec((tm, SLAB), lambda i: (i, COL_Z)),
            pl.BlockSpec((1, SSM_INNER), lambda i: (0, 0)),
            pl.BlockSpec((1, SSM_INNER), lambda i: (0, 0)),
        ],
        out_specs=pl.BlockSpec((tm, SSM_INNER), row),
        out_shape=jax.ShapeDtypeStruct((n, SSM_INNER), BF16),
        compiler_params=_cparams(("parallel",)),
        name="ssd_combine",
    )(y_f, y_b, xbc, proj, d_row, norm_w.reshape(1, SSM_INNER))


def _pair_scores(q2, k2):
    first_q = _lane_lt(q2.shape, HEAD_DIM)
    zero = jnp.zeros_like(q2)
    qs = jnp.concatenate([jnp.where(first_q, q2, zero), jnp.where(first_q, zero, q2)], axis=0)
    return _dot_nt(qs, k2)


def _pair_attend(s, v2):
    tq = s.shape[0] // 2
    m = jnp.max(s, axis=-1, keepdims=True)
    p = jnp.exp(s - m)
    l = jnp.sum(p, axis=-1, keepdims=True)
    o = _dot(p.astype(BF16), v2) / l
    lse = m + jnp.log(l)
    first_o = _lane_lt((tq, HEAD_PAIR), HEAD_DIM)
    return jnp.where(first_o, o[:tq], o[tq:]), jnp.where(first_o, lse[:tq], lse[tq:])


def na_bias_tables(rpb, rows):
    kr = min(NA_WIN_ROWS, rows)
    qc = np.arange(GRID_W)
    kc = np.arange(GRID_W)
    q_start = np.clip(qc - NA_WIN_COLS // 2, 0, GRID_W - NA_WIN_COLS)
    col_in = (kc[None, :] >= q_start[:, None]) & (kc[None, :] < q_start[:, None] + NA_WIN_COLS)
    col_off = np.clip(kc[None, :] - qc[:, None] + NA_WIN_COLS - 1, 0, 2 * NA_WIN_COLS - 2)
    onehot = (col_off[None] == np.arange(2 * NA_WIN_COLS - 1)[:, None, None]).astype(np.float32)
    expanded = jnp.einsum("hrc,cqk->hqrk", rpb, jnp.asarray(onehot), precision=lax.Precision.HIGHEST)
    expanded = jnp.where(jnp.asarray(col_in)[None, :, None, :], expanded, NEG_INF)

    def table(r):
        row_start = int(np.clip(r - kr // 2, 0, rows - kr))
        ro0 = row_start - r + NA_WIN_ROWS - 1
        return expanded[:, :, ro0:ro0 + kr, :].reshape(N_PAIRS, 2 * GRID_W, kr * GRID_W)

    rs = NA_ROWS_PER_STEP
    lo = [table(r) for r in range(rs)]
    mid = [table(min(rs, rows - 1))] * rs
    hi = [table(r) for r in range(rows - rs, rows)]
    return jnp.stack([jnp.stack(lo), jnp.stack(mid), jnp.stack(hi)])


def _na_kernel(q_ref, k_ref, v_ref, bias_ref, o_ref, *, rows, kr):
    step = pl.program_id(1)
    rs = NA_ROWS_PER_STEP
    for rr in range(rs):
        r = step * rs + rr
        row_start = jnp.clip(r - kr // 2, 0, rows - kr)
        k0 = pl.multiple_of(row_start * GRID_W, GRID_W)
        pair_cols = [slice(pair * HEAD_PAIR, (pair + 1) * HEAD_PAIR) for pair in range(N_PAIRS)]
        scores = []
        for pair, cols in enumerate(pair_cols):
            q2 = q_ref[rr * GRID_W:(rr + 1) * GRID_W, cols] * jnp.asarray(HEAD_DIM ** -0.5, BF16)
            scores.append(_pair_scores(q2, k_ref[pl.ds(k0, kr * GRID_W), cols]) + bias_ref[0, rr, pair])
        for s, cols in zip(scores, pair_cols):
            o, _ = _pair_attend(s, v_ref[pl.ds(k0, kr * GRID_W), cols])
            o_ref[rr * GRID_W:(rr + 1) * GRID_W, cols] = o.astype(o_ref.dtype)


def na_attention(proj, rpb, b, t):
    rows = t // GRID_W
    kr = min(NA_WIN_ROWS, rows)
    rs = NA_ROWS_PER_STEP
    nsteps = rows // rs
    bias = na_bias_tables(rpb, rows)

    def kind(i, s):
        return jnp.where(s == 0, 0, jnp.where(s == nsteps - 1, 2, 1))

    return pl.pallas_call(
        functools.partial(_na_kernel, rows=rows, kr=kr),
        grid=(b, nsteps),
        in_specs=[
            pl.BlockSpec((rs * GRID_W, SLAB), lambda i, s: (i * nsteps + s, COL_NAQ)),
            pl.BlockSpec((t, SLAB), lambda i, s: (i, COL_NAK)),
            pl.BlockSpec((t, SLAB), lambda i, s: (i, COL_NAV)),
            pl.BlockSpec((1, rs, N_PAIRS, 2 * GRID_W, kr * GRID_W), lambda i, s: (kind(i, s), 0, 0, 0, 0)),
        ],
        out_specs=pl.BlockSpec((rs * GRID_W, SLAB), lambda i, s: (i * nsteps + s, 0)),
        out_shape=jax.ShapeDtypeStruct((b * t, SLAB), BF16),
        compiler_params=_cparams(("parallel", "arbitrary")),
        name="na_attention",
    )(proj, proj, proj, bias)


def _rope_angles(t, d):
    inv = ROPE_THETA ** (-np.arange(0, d, 2, dtype=np.float32) / d)
    return np.arange(t, dtype=np.float32)[:, None] * inv[None, :]


def rope_tables_pair(t):
    ang = _rope_angles(t, HEAD_DIM)
    cos = np.tile(np.cos(ang), (1, 4))
    sin = np.tile(np.concatenate([-np.sin(ang), np.sin(ang)], axis=1), (1, 2))
    return jnp.asarray(cos, F32), jnp.asarray(sin, F32)


FOLD_CHUNK = 256
FOLD_DILS = tuple(d for _, d in DIL_PAIRS if d > 1)


def fold_permutation(dil):
    per = FOLD_CHUNK // dil
    perm = np.zeros((FOLD_CHUNK, FOLD_CHUNK), np.float32)
    dst = np.arange(FOLD_CHUNK)
    perm[dst, (dst % per) * dil + dst // per] = 1.0
    return jnp.asarray(perm, BF16)


def _rope_qkv_kernel(x_ref, v_ref, cos_ref, sin_ref, *rest):
    nd = len(FOLD_DILS)
    perm_refs, o_ref, fold_refs = rest[:nd], rest[nd], rest[nd + 1:]
    cos = cos_ref[...]
    sin = sin_ref[...]
    half = HEAD_DIM // 2
    for c in range(x_ref.shape[1] // LANES):
        x = x_ref[:, c * LANES:(c + 1) * LANES].astype(F32)
        rot = jnp.where(_lane_lt(x.shape, half, HEAD_DIM),
                        pltpu.roll(x, LANES - half, 1), pltpu.roll(x, half, 1))
        y = x * cos + rot * sin
        if c < N_PAIRS:
            y = y * (HEAD_DIM ** -0.5)
        o_ref[0, :, c * LANES:(c + 1) * LANES] = y.astype(o_ref.dtype)
    o_ref[0, :, 2 * SLAB:3 * SLAB] = v_ref[...]
    tm = x_ref.shape[0]
    for dil, perm_ref, f_ref in zip(FOLD_DILS, perm_refs, fold_refs):
        per = FOLD_CHUNK // dil
        for c in range(tm // FOLD_CHUNK):
            folded = _dot(perm_ref[...], o_ref[0, c * FOLD_CHUNK:(c + 1) * FOLD_CHUNK, :]).astype(f_ref.dtype)
            for p in range(dil):
                f_ref[p, c * per:(c + 1) * per, :] = folded[p * per:(p + 1) * per, :]


def rope_qkv(proj, b, t, tm):
    n = b * t
    cos, sin = rope_tables_pair(t)
    nb = t // tm
    fold_spec = lambda d: pl.BlockSpec((None, d, tm // d, 3 * SLAB), lambda i: (i // nb, 0, i % nb, 0))
    outs = pl.pallas_call(
        _rope_qkv_kernel,
        grid=(n // tm,),
        in_specs=[
            pl.BlockSpec((tm, 2 * SLAB), lambda i: (i, COL_DLQ // 2)),
            pl.BlockSpec((tm, SLAB), lambda i: (i, COL_DLV)),
            pl.BlockSpec((tm, LANES), lambda i: (i % nb, 0)),
            pl.BlockSpec((tm, LANES), lambda i: (i % nb, 0)),
        ] + [pl.BlockSpec((FOLD_CHUNK, FOLD_CHUNK), lambda i: (0, 0))] * len(FOLD_DILS),
        out_specs=[fold_spec(1)] + [fold_spec(d) for d in FOLD_DILS],
        out_shape=[jax.ShapeDtypeStruct((b, d, t // d, 3 * SLAB), BF16) for d in (1,) + FOLD_DILS],
        compiler_params=_cparams(("parallel",)),
        name="rope_qkv",
    )(proj, proj, cos, sin, *[fold_permutation(d) for d in FOLD_DILS])
    by_dil = dict(zip((1,) + FOLD_DILS, outs))
    return [by_dil[d] for _, d in DIL_PAIRS]


def _band_kernel(q_ref, k_ref, v_ref, o_ref, lse_ref, *, sub, half, span):
    tq = DIL_QBLOCK
    blocks = q_ref.shape[0] // tq
    for blk in range(blocks):
        qb = pl.program_id(2) * blocks + blk
        rows = slice(blk * tq, (blk + 1) * tq)
        start = jnp.clip(qb * tq - half, 0, sub - span)
        start = pl.multiple_of(start, half)
        q_pos = qb * tq + lax.broadcasted_iota(jnp.int32, (2 * tq, span), 0) % tq
        k_pos = start + lax.broadcasted_iota(jnp.int32, (2 * tq, span), 1)
        valid = jnp.abs(k_pos - q_pos) <= half
        pair_cols = [slice(pair * HEAD_PAIR, (pair + 1) * HEAD_PAIR) for pair in range(N_PAIRS)]
        scores = [jnp.where(valid, _pair_scores(q_ref[rows, cols], k_ref[pl.ds(start, span), cols]), NEG_INF)
                  for cols in pair_cols]
        for s, cols in zip(scores, pair_cols):
            o, lse = _pair_attend(s, v_ref[pl.ds(start, span), cols])
            o_ref[rows, cols] = o
            lse_ref[rows, cols] = lse


def band_attention(qkv, window, dil):
    b, _, sub, _ = qkv.shape
    half = window // (2 * dil)
    span = DIL_QBLOCK + 2 * half
    tq = DIL_QBLOCK * min(BAND_BLOCKS_PER_STEP, sub // DIL_QBLOCK)
    nqb = sub // tq
    return pl.pallas_call(
        functools.partial(_band_kernel, sub=sub, half=half, span=span),
        grid=(b, dil, nqb),
        in_specs=[
            pl.BlockSpec((None, None, tq, SLAB), lambda i, p, s: (i, p, s, 0)),
            pl.BlockSpec((None, None, sub, SLAB), lambda i, p, s: (i, p, 0, 1)),
            pl.BlockSpec((None, None, sub, SLAB), lambda i, p, s: (i, p, 0, 2)),
        ],
        out_specs=[pl.BlockSpec((None, None, tq, SLAB), lambda i, p, s: (i, p, s, 0))] * 2,
        out_shape=[jax.ShapeDtypeStruct((b, dil, sub, SLAB), F32)] * 2,
        compiler_params=_cparams(("parallel", "parallel", "arbitrary")),
        name="band_attention_d%d" % dil,
    )(qkv, qkv, qkv)


def _dil_combine_kernel(*refs):
    nbr = len(DIL_PAIRS)
    o_refs, l_refs, out_ref = refs[:nbr], refs[nbr:2 * nbr], refs[2 * nbr]
    scratch = iter(refs[2 * nbr + 1:])

    def token_order(ref):
        dil = ref.shape[0]
        if dil == 1:
            return ref[0]
        buf = next(scratch)
        per = ref.shape[1]
        for p in range(dil):
            for c in range(SLAB // LANES):
                buf[c, pl.ds(p, per, stride=dil), :] = ref[p, :, c * LANES:(c + 1) * LANES]
        return jnp.concatenate([buf[c] for c in range(SLAB // LANES)], axis=1)

    os = [token_order(r) for r in o_refs]
    lses = [token_order(r) for r in l_refs]
    m = functools.reduce(jnp.maximum, lses)
    ws = [jnp.exp(l - m) for l in lses]
    den = functools.reduce(jnp.add, ws)
    acc = functools.reduce(jnp.add, [(w / den) * o for w, o in zip(ws, os)])
    out_ref[...] = acc.astype(out_ref.dtype)


def dil_combine(outs, lses, tm):
    b, _, t, _ = outs[0].shape
    n = b * t
    nb = t // tm
    spec = lambda a: pl.BlockSpec((None, a.shape[1], tm // a.shape[1], SLAB), lambda i: (i // nb, 0, i % nb, 0))
    n_folded = sum(1 for a in outs + lses if a.shape[1] > 1)
    return pl.pallas_call(
        _dil_combine_kernel,
        grid=(n // tm,),
        in_specs=[spec(a) for a in outs + lses],
        out_specs=pl.BlockSpec((tm, SLAB), lambda i: (i, 0)),
        out_shape=jax.ShapeDtypeStruct((n, SLAB), BF16),
        scratch_shapes=[pltpu.VMEM((SLAB // LANES, tm, LANES), F32)] * n_folded,
        compiler_params=_cparams(("parallel",)),
        name="dil_combine",
    )(*outs, *lses)


MLA_QK = MLA_NOPE + MLA_ROPE


def mla_tables(t):
    ang = _rope_angles(t, MLA_ROPE)
    cos2 = np.concatenate([np.cos(ang), np.cos(ang)], axis=1)
    sin2 = np.concatenate([np.sin(ang), np.sin(ang)], axis=1)
    z = lambda w: np.zeros((t, w), np.float32)
    q_cos = np.concatenate([np.ones((t, MLA_NOPE), np.float32), cos2, z(LANES - MLA_QK)], axis=1)
    q_sin = np.concatenate([z(MLA_NOPE), sin2, z(LANES - MLA_QK)], axis=1)
    k_cos = np.concatenate([cos2, z(LANES - MLA_ROPE)], axis=1)
    k_sin = np.concatenate([-sin2[:, :MLA_ROPE // 2], sin2[:, MLA_ROPE // 2:], z(LANES - MLA_ROPE)], axis=1)
    return tuple(jnp.asarray(a, F32) for a in (q_cos, q_sin, k_cos, k_sin))


def mla_weights(w_uq, w_ukv):
    hq = w_uq.reshape(MLA_Q_RANK, MLA_HEADS, MLA_QK)
    nope, pe = hq[..., :MLA_NOPE], hq[..., MLA_NOPE:]
    pe_rot = jnp.concatenate([-pe[..., MLA_ROPE // 2:], pe[..., :MLA_ROPE // 2]], axis=-1)
    zq = jnp.zeros((MLA_Q_RANK, MLA_HEADS, LANES - MLA_QK), w_uq.dtype)
    w1 = jnp.concatenate([nope, pe, zq], axis=-1).reshape(MLA_Q_RANK, MLA_HEADS * LANES)
    w2 = jnp.concatenate([jnp.zeros_like(nope), pe_rot, zq], axis=-1).reshape(MLA_Q_RANK, MLA_HEADS * LANES)
    hkv = w_ukv.reshape(MLA_KV_RANK, MLA_HEADS, MLA_NOPE + MLA_V)
    k_nope, v = hkv[..., :MLA_NOPE], hkv[..., MLA_NOPE:]
    zk = jnp.zeros((MLA_KV_RANK, MLA_HEADS, LANES - MLA_NOPE), w_ukv.dtype)
    wk = jnp.concatenate([k_nope, zk], axis=-1).reshape(MLA_KV_RANK, MLA_HEADS * LANES)
    zv = jnp.zeros((MLA_KV_RANK, MLA_HEADS, LANES - MLA_V), w_ukv.dtype)
    wv = jnp.concatenate([v, zv], axis=-1).reshape(MLA_KV_RANK, MLA_HEADS * LANES)
    place = np.zeros((LANES, MLA_HEADS * LANES), np.float32)
    ones = np.zeros((1, MLA_HEADS * LANES), np.float32)
    for h in range(MLA_HEADS):
        place[np.arange(MLA_ROPE), h * LANES + MLA_NOPE + np.arange(MLA_ROPE)] = 1.0
        ones[0, h * LANES + MLA_V] = 1.0
    return (w1.astype(BF16), w2.astype(BF16), wk.astype(BF16), wv.astype(BF16), jnp.asarray(place, BF16),
            jnp.asarray(ones, F32))


def _mla_q_kernel(c_ref, nw_ref, w1_ref, w2_ref, cos_ref, sin_ref, o_ref):
    cn = _rms(c_ref[...].astype(F32), nw_ref[...]).astype(BF16)
    cos = jnp.tile(cos_ref[...], (1, MLA_HEADS))
    sin = jnp.tile(sin_ref[...], (1, MLA_HEADS))
    q = _dot(cn, w1_ref[...]) * cos + _dot(cn, w2_ref[...]) * sin
    o_ref[...] = (q * (MLA_QK ** -0.5 * math.log2(math.e))).astype(o_ref.dtype)


def _mla_kv_kernel(c_ref, tail_ref, nw_ref, wk_ref, wv_ref, place_ref, ones_ref, cos_ref, sin_ref,
                   k_ref, v_ref):
    cn = _rms(c_ref[...].astype(F32), nw_ref[...]).astype(BF16)
    kr = tail_ref[...]
    half = MLA_ROPE // 2
    rot = jnp.where(_lane_lt(kr.shape, half), pltpu.roll(kr, LANES - half, 1), pltpu.roll(kr, half, 1))
    k_pe = (kr * cos_ref[...] + rot * sin_ref[...]).astype(BF16)
    k_ref[...] = (_dot(cn, wk_ref[...]) + _dot(k_pe, place_ref[...])).astype(k_ref.dtype)
    v_ref[...] = (_dot(cn, wv_ref[...]) + ones_ref[...]).astype(v_ref.dtype)


def mla_project(proj, tail, q_norm_w, kv_norm_w, w_uq, w_ukv, b, t, tm):
    n = b * t
    nb = t // tm
    w1, w2, wk, wv, place, ones = mla_weights(w_uq, w_ukv)
    q_cos, q_sin, k_cos, k_sin = mla_tables(t)
    wide = MLA_HEADS * LANES
    full = lambda shape: pl.BlockSpec(shape, lambda i: (0, 0))
    tab = pl.BlockSpec((tm, LANES), lambda i: (i % nb, 0))
    qf = pl.pallas_call(
        _mla_q_kernel,
        grid=(n // tm,),
        in_specs=[pl.BlockSpec((tm, SLAB), lambda i: (i, COL_CQ)), full((1, MLA_Q_RANK)),
                  full((MLA_Q_RANK, wide)), full((MLA_Q_RANK, wide)), tab, tab],
        out_specs=pl.BlockSpec((tm, wide), lambda i: (i, 0)),
        out_shape=jax.ShapeDtypeStruct((n, wide), BF16),
        compiler_params=_cparams(("parallel",)),
        name="mla_q_proj",
    )(proj, q_norm_w.reshape(1, MLA_Q_RANK), w1, w2, q_cos, q_sin)
    kf, vf = pl.pallas_call(
        _mla_kv_kernel,
        grid=(n // tm,),
        in_specs=[pl.BlockSpec((tm, SLAB), lambda i: (i, COL_CKV)),
                  pl.BlockSpec((tm, LANES), lambda i: (i, 0)), full((1, MLA_KV_RANK)),
                  full((MLA_KV_RANK, wide)), full((MLA_KV_RANK, wide)), full((LANES, wide)), full((1, wide)),
                  tab, tab],
        out_specs=[pl.BlockSpec((tm, wide), lambda i: (i, 0))] * 2,
        out_shape=[jax.ShapeDtypeStruct((n, wide), BF16)] * 2,
        compiler_params=_cparams(("parallel",)),
        name="mla_kv_proj",
    )(proj, tail, kv_norm_w.reshape(1, MLA_KV_RANK), wk, wv, place, ones, k_cos, k_sin)
    return qf, kf, vf


def _mla_attn_kernel(q_ref, k_ref, v_ref, *rest, t, tk, n_cast):
    cast_in, o_ref, cast_out = rest[:n_cast], rest[n_cast], rest[n_cast + 1:]
    for src_ref, dst_ref in zip(cast_in, cast_out):
        dst_ref[...] = src_ref[...].astype(dst_ref.dtype)
    tq = q_ref.shape[0]
    groups = [slice(hh * LANES, (hh + 1) * LANES) for hh in range(2)]
    qs = [q_ref[:, grp] for grp in groups]

    def scores(c):
        return [_dot_nt(q, k_ref[c * tk:(c + 1) * tk, grp]) for q, grp in zip(qs, groups)]

    n_chunks = t // tk
    ms = [jnp.full((tq, 1), -jnp.inf, F32)] * 2
    accs = [jnp.zeros((tq, LANES), F32)] * 2
    s_next = scores(0)
    for c in range(n_chunks):
        s_cur = s_next
        if c + 1 < n_chunks:
            s_next = scores(c + 1)
        for hh, grp in enumerate(groups):
            m_new = jnp.maximum(ms[hh], jnp.max(s_cur[hh], axis=-1, keepdims=True))
            p = jnp.exp2((s_cur[hh] - m_new).astype(BF16))
            accs[hh] = jnp.exp2(ms[hh] - m_new) * accs[hh] + _dot(p, v_ref[c * tk:(c + 1) * tk, grp])
            ms[hh] = m_new
    outs = [acc / acc[:, MLA_V:MLA_V + 1] for acc in accs]
    first = _lane_lt((tq, LANES), MLA_V)
    o_ref[...] = jnp.where(first, outs[0], pltpu.roll(outs[1], MLA_V, 1)).astype(o_ref.dtype)


def mla_cast_rows(w, b, t, tq):
    steps = b * N_PAIRS * (t // tq)
    rows = int(np.prod(w.shape[:-1]))
    per = rows // steps
    return per if rows % steps == 0 and per % 16 == 0 else None


def mla_attention(qf, kf, vf, b, t, tq, tk, cast=()):
    n = b * t
    nq = t // tq
    step = lambda i, p, s: ((i * N_PAIRS + p) * nq + s, 0)
    cast2d = [w.reshape(-1, w.shape[-1]) for w in cast]
    cast_specs = [pl.BlockSpec((mla_cast_rows(w, b, t, tq), w2.shape[1]), step) for w, w2 in zip(cast, cast2d)]
    outs = pl.pallas_call(
        functools.partial(_mla_attn_kernel, t=t, tk=tk, n_cast=len(cast)),
        grid=(b, N_PAIRS, nq),
        in_specs=[
            pl.BlockSpec((tq, 2 * LANES), lambda i, p, s: (i * nq + s, p)),
            pl.BlockSpec((t, 2 * LANES), lambda i, p, s: (i, p)),
            pl.BlockSpec((t, 2 * LANES), lambda i, p, s: (i, p)),
        ] + cast_specs,
        out_specs=[pl.BlockSpec((tq, HEAD_PAIR), lambda i, p, s: (i * nq + s, p))] + cast_specs,
        out_shape=[jax.ShapeDtypeStruct((n, SLAB), BF16)]
        + [jax.ShapeDtypeStruct(w2.shape, BF16) for w2 in cast2d],
        compiler_params=_cparams(("parallel", "parallel", "arbitrary")),
        name="mla_attention",
    )(qf, kf, vf, *cast2d)
    return outs[0], [o.reshape(w.shape) for o, w in zip(outs[1:], cast)]


def _in_proj_segments():
    sizes = (SSM_INNER, SSM_CONV_CH, 2 * SSM_HEADS, SLAB, SLAB, SLAB, MLA_Q_RANK, MLA_KV_RANK, MLA_ROPE,
             SLAB, SLAB, SLAB)
    off = [int(v) for v in np.concatenate([[0], np.cumsum(sizes)])]
    main = ((off[0], off[2]), (off[3], off[8]), (off[9], off[12]))
    tail = ((off[8], off[9]), (off[2], off[3]))
    return main, tail


def _in_proj_columns():
    main, tail = _in_proj_segments()
    cols = lambda segs: np.concatenate([np.arange(a, b) for a, b in segs])
    return cols(main), cols(tail)


def in_proj_weights(w_in_l):
    main, tail = _in_proj_segments()
    w_main = jnp.concatenate([w_in_l[:, a:b] for a, b in main], axis=1).astype(BF16)
    pad = jnp.zeros((D_MODEL, LANES - sum(b - a for a, b in tail)), w_in_l.dtype)
    w_tail = jnp.concatenate([w_in_l[:, a:b] for a, b in tail] + [pad], axis=1).astype(BF16)
    return w_main, w_tail


MLA_TQ = 1024
MLA_CASTS_PER_CALL = 2


def mixers(proj, tail, p, l, b, t, cast):
    xbc = conv_silu(proj, p["conv_w"][l], p["conv_b"][l], b, t)
    y_f, y_b = ssd_scan(xbc, tail, p["dt_bias"][l], p["a_log"][l], b, t)
    y_ssm = ssd_combine(y_f, y_b, xbc, proj, p["d_skip"][l], p["ssm_norm_w"][l], 1024)

    y_na = na_attention(proj, p["na_rpb"][l], b, t)

    qf, kf, vf = mla_project(proj, tail, p["mla_q_norm_w"][l], p["mla_kv_norm_w"][l],
                             p["mla_w_uq"][l], p["mla_w_ukv"][l], b, t, 512)
    y_mla, cast_out = mla_attention(qf, kf, vf, b, t, MLA_TQ, 512, cast)

    qkvs = rope_qkv(proj, b, t, 1024)
    outs, lses = zip(*[band_attention(qkv, w, d) for qkv, (w, d) in zip(qkvs, DIL_PAIRS)])
    y_dil = dil_combine(outs, lses, 1024)
    return (y_ssm, y_na, y_mla, y_dil), cast_out


def kernel(x, attn_norm_w, w_in, conv_w, conv_b, a_log, dt_bias, d_skip, ssm_norm_w, na_rpb,
           mla_q_norm_w, mla_kv_norm_w, mla_w_uq, mla_w_ukv, w_o, ffn_norm_w, ffn_w_gate, ffn_w_up,
           ffn_w_down, router_w, exp_w_gate, exp_w_up, exp_w_down, final_norm_w):
    b, t, _ = x.shape
    n = b * t
    depth = w_in.shape[0]
    p = dict(conv_w=conv_w, conv_b=conv_b, a_log=a_log, dt_bias=dt_bias, d_skip=d_skip,
             ssm_norm_w=ssm_norm_w, na_rpb=na_rpb, mla_q_norm_w=mla_q_norm_w,
             mla_kv_norm_w=mla_kv_norm_w, mla_w_uq=mla_w_uq, mla_w_ukv=mla_w_ukv)
    x = x.reshape(n, D_MODEL)
    cast_rows = 256
    w_o_b = cast_bf16(w_o, cast_rows)
    ffn_b = [cast_bf16(w, cast_rows) for w in (ffn_w_gate, ffn_w_up, ffn_w_down)]
    exp_f32 = [exp_w_gate, exp_w_up, exp_w_down]
    exp_b = [None] * len(exp_f32)
    pending = [k for k, w in enumerate(exp_f32) if mla_cast_rows(w, b, t, MLA_TQ) is not None]
    moe_tm = 1024
    normed = False
    for l in range(depth):
        w_main, w_tail = in_proj_weights(w_in[l])
        proj, tail = in_proj(x, attn_norm_w[l], w_main, w_tail, 512, PROJ_MAIN // 2)
        jobs, pending = pending[:MLA_CASTS_PER_CALL], pending[MLA_CASTS_PER_CALL:]
        mix, cast_out = mixers(proj, tail, p, l, b, t, [exp_f32[k] for k in jobs])
        for k, w_b in zip(jobs, cast_out):
            exp_b[k] = w_b
        x = out_proj(mix, w_o_b, l, x, 1024, 1024)
        j = l // 2
        if l % 2 == 0:
            x = ffn_dense(x, ffn_norm_w[l], *ffn_b, j, 1024, 512)
        else:
            pending = []
            exp_b = [cast_bf16(w, cast_rows) if w_b is None else w_b for w, w_b in zip(exp_f32, exp_b)]
            top_i, gates = moe_router(x, ffn_norm_w[l], router_w[j], 512)
            src, pos, tile_expert, tile_valid = moe_plan(top_i, moe_tm)
            y = moe_ffn(x, ffn_norm_w[l], *exp_b, j, src, tile_expert, tile_valid, moe_tm, 512)
            last = l == depth - 1
            x = moe_combine(x, gates, y, pos, final_norm_w if last else None, 256)
            normed = last
    if not normed:
        x = rmsnorm_rows(x, final_norm_w, 1024)
    return x.reshape(b, t, D_MODEL)
```

```python
import functools
import math

import numpy as np
import jax
import jax.numpy as jnp
from jax import lax
from jax.experimental import pallas as pl
from jax.experimental.pallas import tpu as pltpu

F32 = jnp.float32
BF16 = jnp.bfloat16

D_MODEL = 2048
GRID_W = 64
HEAD_DIM = 64
ROPE_THETA = 10000.0
NORM_EPS = 1e-6
NEG_INF = -1e30

SSM_HEADS = 8
SSM_HEAD_DIM = 64
SSM_INNER = SSM_HEADS * SSM_HEAD_DIM
SSM_GROUPS = 2
SSM_STATE = 128
SSM_CONV = 5
SSM_CHUNK = 128
SSD_CHUNKS_PER_STEP = 2
SSM_CONV_CH = SSM_INNER + 2 * SSM_GROUPS * SSM_STATE

NA_HEADS = 8
NA_WIN_ROWS = 8
NA_WIN_COLS = 16
NA_COL_BLOCK = 16
NA_KEY_COLS = 32
NA_ROWS_PER_STEP = 8

MLA_HEADS = 8
MLA_Q_RANK = 512
MLA_KV_RANK = 512
MLA_NOPE = 64
MLA_ROPE = 32
MLA_V = 64

DIL_HEADS = 8
DIL_PAIRS = ((128, 1), (512, 4), (2048, 16))
DIL_QBLOCK = 128
BAND_BLOCKS_PER_STEP = 4

N_EXPERTS = 8
TOP_K = 2
SPLIT_PARTS = 3

LANES = 128
HEAD_PAIR = 2 * HEAD_DIM
N_PAIRS = 4
SLAB = 512

COL_Z, COL_XBC, COL_NAQ, COL_NAK, COL_NAV, COL_CQ, COL_CKV, COL_DLQ, COL_DLK, COL_DLV = (
    0, 1, 3, 4, 5, 6, 7, 8, 9, 10)
PROJ_MAIN = 11 * SLAB
TAIL_DT = 32

VMEM_LIMIT = 56 * 1024 * 1024


def _cparams(sem, vmem=VMEM_LIMIT):
    return pltpu.CompilerParams(dimension_semantics=sem, vmem_limit_bytes=vmem)


def _lane_lt(shape, bound, period=None):
    lane = lax.broadcasted_iota(jnp.int32, shape, len(shape) - 1)
    if period is not None:
        lane = lane % period
    return lane < bound


def _rms(x, w):
    ms = jnp.mean(x * x, axis=-1, keepdims=True)
    return x * lax.rsqrt(ms + NORM_EPS) * w


def _dot(a, b):
    return jnp.dot(a, b, preferred_element_type=F32)


def _dot_nt(a, b):
    return lax.dot_general(a, b, (((1,), (1,)), ((), ())), preferred_element_type=F32)


def _dot_tn(a, b):
    return lax.dot_general(a, b, (((0,), (0,)), ((), ())), preferred_element_type=F32)


def _cast_kernel(x_ref, o_ref):
    o_ref[...] = x_ref[...].astype(o_ref.dtype)


def cast_bf16(w, tr):
    shape = w.shape
    w2 = w.reshape(-1, shape[-1])
    r, c = w2.shape
    out = pl.pallas_call(
        _cast_kernel,
        grid=(r // tr,),
        in_specs=[pl.BlockSpec((tr, c), lambda i: (i, 0))],
        out_specs=pl.BlockSpec((tr, c), lambda i: (i, 0)),
        out_shape=jax.ShapeDtypeStruct((r, c), BF16),
        compiler_params=_cparams(("parallel",)),
        name="cast_bf16",
    )(w2)
    return out.reshape(shape)


def _in_proj_kernel(x_ref, nw_ref, w_ref, wt_ref, o_ref, t_ref, h_ref):
    @pl.when(pl.program_id(1) == 0)
    def _():
        h = _rms(x_ref[...], nw_ref[...]).astype(BF16)
        h_ref[...] = h
        t_ref[...] = _dot(h, wt_ref[...])

    o_ref[...] = _dot(h_ref[...], w_ref[...]).astype(o_ref.dtype)


def in_proj(x, nw, w_main, w_tail, tm, tn):
    n, k = x.shape
    nout = w_main.shape[1]
    return pl.pallas_call(
        _in_proj_kernel,
        grid=(n // tm, nout // tn),
        in_specs=[
            pl.BlockSpec((tm, k), lambda i, j: (i, 0)),
            pl.BlockSpec((1, k), lambda i, j: (0, 0)),
            pl.BlockSpec((k, tn), lambda i, j: (0, j)),
            pl.BlockSpec((k, LANES), lambda i, j: (0, 0)),
        ],
        out_specs=[pl.BlockSpec((tm, tn), lambda i, j: (i, j)), pl.BlockSpec((tm, LANES), lambda i, j: (i, 0))],
        out_shape=[jax.ShapeDtypeStruct((n, nout), BF16), jax.ShapeDtypeStruct((n, LANES), F32)],
        scratch_shapes=[pltpu.VMEM((tm, k), BF16)],
        compiler_params=_cparams(("parallel", "arbitrary")),
        name="in_proj",
    )(x, nw.reshape(1, k), w_main, w_tail)


def _out_proj_kernel(a0_ref, a1_ref, a2_ref, a3_ref, w_ref, r_ref, o_ref):
    acc = r_ref[...]
    for s, a_ref in enumerate((a0_ref, a1_ref, a2_ref, a3_ref)):
        acc = acc + _dot(a_ref[...], w_ref[s * SLAB:(s + 1) * SLAB, :])
    o_ref[...] = acc


def out_proj(mix, w, layer, res, tm, tn):
    n = res.shape[0]
    return pl.pallas_call(
        _out_proj_kernel,
        grid=(n // tm, D_MODEL // tn),
        in_specs=[pl.BlockSpec((tm, SLAB), lambda i, j: (i, 0))] * 4 + [
            pl.BlockSpec((None, 4 * SLAB, tn), lambda i, j: (layer, 0, j)),
            pl.BlockSpec((tm, tn), lambda i, j: (i, j)),
        ],
        out_specs=pl.BlockSpec((tm, tn), lambda i, j: (i, j)),
        out_shape=jax.ShapeDtypeStruct((n, D_MODEL), F32),
        compiler_params=_cparams(("parallel", "arbitrary")),
        name="out_proj",
    )(*mix, w, res)


def _ffn_kernel(x_ref, nw_ref, wg_ref, wu_ref, wd_ref, o_ref, h_ref):
    @pl.when(pl.program_id(1) == 0)
    def _():
        x = x_ref[...]
        h_ref[...] = _rms(x, nw_ref[...]).astype(BF16)
        o_ref[...] = x

    _swiglu_rows(h_ref, wg_ref, wu_ref, wd_ref, o_ref)


FFN_ROW_CHUNK = 512


def _swiglu_rows(h_ref, wg_ref, wu_ref, wd_ref, o_ref):
    tm = h_ref.shape[0]
    for r0 in range(0, tm, FFN_ROW_CHUNK):
        rows = slice(r0, r0 + FFN_ROW_CHUNK)
        h = h_ref[rows, :]
        g = _dot(h, wg_ref[...])
        u = _dot(h, wu_ref[...])
        a = (g * jax.nn.sigmoid(g) * u).astype(BF16)
        o_ref[rows, :] += _dot(a, wd_ref[...])


def ffn_dense(x, nw, wg, wu, wd, layer, tm, tf):
    n = x.shape[0]
    d_ff = wg.shape[-1]
    return pl.pallas_call(
        _ffn_kernel,
        grid=(n // tm, d_ff // tf),
        in_specs=[
            pl.BlockSpec((tm, D_MODEL), lambda i, j: (i, 0)),
            pl.BlockSpec((1, D_MODEL), lambda i, j: (0, 0)),
            pl.BlockSpec((None, D_MODEL, tf), lambda i, j: (layer, 0, j)),
            pl.BlockSpec((None, D_MODEL, tf), lambda i, j: (layer, 0, j)),
            pl.BlockSpec((None, tf, D_MODEL), lambda i, j: (layer, j, 0)),
        ],
        out_specs=pl.BlockSpec((tm, D_MODEL), lambda i, j: (i, 0)),
        out_shape=jax.ShapeDtypeStruct((n, D_MODEL), F32),
        scratch_shapes=[pltpu.VMEM((tm, D_MODEL), BF16)],
        compiler_params=_cparams(("parallel", "arbitrary")),
        name="ffn_dense",
    )(x, nw.reshape(1, D_MODEL), wg, wu, wd)


def _router_kernel(x_ref, nw_ref, rw_ref, idx_ref, gate_ref):
    h = _rms(x_ref[...], nw_ref[...])
    acc = jnp.zeros((h.shape[0], LANES), F32)
    rem = h
    for _ in range(SPLIT_PARTS):
        part = rem.astype(BF16)
        acc = acc + _dot(part, rw_ref[...])
        rem = rem - part.astype(F32)
    logits = acc
    for k in range(1, SPLIT_PARTS):
        logits = logits + pltpu.roll(acc, LANES - k * N_EXPERTS, 1)
    lane = lax.broadcasted_iota(jnp.int32, logits.shape, 1)
    logits = jnp.where(lane < N_EXPERTS, logits, -jnp.inf)
    m1 = jnp.max(logits, axis=-1, keepdims=True)
    i1 = jnp.min(jnp.where(logits == m1, lane, LANES), axis=-1, keepdims=True)
    rest = jnp.where(lane == i1, -jnp.inf, logits)
    m2 = jnp.max(rest, axis=-1, keepdims=True)
    i2 = jnp.min(jnp.where(rest == m2, lane, LANES), axis=-1, keepdims=True)
    e2 = jnp.exp(m2 - m1)
    g1 = 1.0 / (1.0 + e2)
    g2 = e2 / (1.0 + e2)
    idx_ref[...] = jnp.where(lane == 0, i1, i2)[:, :TOP_K]
    gate_ref[...] = jnp.where(lane == 0, g1, g2)[:, :TOP_K]


def moe_router(x, nw, router_w, tm):
    n = x.shape[0]
    parts, rem = [], router_w
    for _ in range(SPLIT_PARTS):
        parts.append(rem.astype(BF16))
        rem = rem - parts[-1].astype(F32)
    pad = jnp.zeros((D_MODEL, LANES - SPLIT_PARTS * N_EXPERTS), BF16)
    rw = jnp.concatenate(parts + [pad], axis=1)
    return pl.pallas_call(
        _router_kernel,
        grid=(n // tm,),
        in_specs=[
            pl.BlockSpec((tm, D_MODEL), lambda i: (i, 0)),
            pl.BlockSpec((1, D_MODEL), lambda i: (0, 0)),
            pl.BlockSpec((D_MODEL, LANES), lambda i: (0, 0)),
        ],
        out_specs=[pl.BlockSpec((tm, TOP_K), lambda i: (i, 0)),
                   pl.BlockSpec((tm, TOP_K), lambda i: (i, 0))],
        out_shape=[jax.ShapeDtypeStruct((n, TOP_K), jnp.int32),
                   jax.ShapeDtypeStruct((n, TOP_K), F32)],
        compiler_params=_cparams(("parallel",)),
        name="moe_router",
    )(x, nw.reshape(1, D_MODEL), rw)


def moe_plan(top_i, tm):
    n = top_i.shape[0]
    flat_e = top_i.reshape(-1)
    onehot = (flat_e[:, None] == jnp.arange(N_EXPERTS, dtype=jnp.int32)[None, :]).astype(jnp.int32)
    csum = jnp.cumsum(onehot, axis=0)
    counts = csum[-1]
    rank = jnp.sum(onehot * csum, axis=1) - 1
    padded = ((counts + tm - 1) // tm) * tm
    pend = jnp.cumsum(padded)
    pstart = pend - padded
    pos = pstart[flat_e] + rank
    n_slots = n * TOP_K + N_EXPERTS * tm
    n_tiles = n_slots // tm
    src = jnp.zeros((n_slots,), jnp.int32).at[pos].set(jnp.arange(n * TOP_K, dtype=jnp.int32) // TOP_K)
    tile_start = jnp.arange(n_tiles, dtype=jnp.int32) * tm
    tile_expert = jnp.sum((tile_start[:, None] >= pend[None, :]).astype(jnp.int32), axis=1)
    tile_valid = (tile_start < pend[-1]).astype(jnp.int32)
    last_valid = jnp.maximum(pend[-1] // tm - 1, 0)
    tile_expert = jnp.where(tile_valid == 1, tile_expert, tile_expert[last_valid]).astype(jnp.int32)
    return src, pos.reshape(n, TOP_K).astype(jnp.int32), tile_expert, tile_valid


def _moe_ffn_kernel(te_ref, tv_ref, src_ref, nsrc_ref, x_hbm, nw_ref, wg_ref, wu_ref, wd_ref, y_ref,
                    xbuf, h_ref, sem, *, tm, rows_per_step):
    i = pl.program_id(0)
    j = pl.program_id(1)
    issued = xbuf.shape[0]
    valid = tv_ref[i] == 1
    has_rows = jnp.logical_or(i == 0, tv_ref[jnp.maximum(i - 1, 0)] == 1)

    def start_row(idx_ref, row):
        tok = idx_ref[0, 0, jnp.minimum(row, tm - 1)]
        pltpu.make_async_copy(x_hbm.at[pl.ds(tok, 1)], xbuf.at[pl.ds(row, 1)], sem).start(priority=1)

    @pl.when(j == 0)
    def _():
        y_ref[...] = jnp.zeros_like(y_ref)

    @pl.when(jnp.logical_and(j == 0, i == 0))
    def _():
        def start(r, c):
            start_row(src_ref, r)
            return c

        lax.fori_loop(0, issued, start, 0)

    @pl.when(jnp.logical_and(j == 0, has_rows))
    def _():
        pltpu.make_async_copy(x_hbm.at[pl.ds(0, issued)], xbuf.at[pl.ds(0, issued)], sem).wait()
        h_ref[...] = _rms(xbuf[0:tm, :], nw_ref[...]).astype(BF16)

    @pl.when(valid)
    def _():
        for r in range(rows_per_step):
            start_row(nsrc_ref, j * rows_per_step + r)
        _swiglu_rows(h_ref, wg_ref, wu_ref, wd_ref, y_ref)


def moe_ffn(x, nw, wg, wu, wd, layer, src, tile_expert, tile_valid, tm, tf):
    n_slots = src.shape[0]
    n_tiles = n_slots // tm + 1
    d_ff = wg.shape[-1]
    nf = d_ff // tf
    sublanes = 8
    rows_per_step = -(-tm // (nf * sublanes)) * sublanes
    buf_rows = rows_per_step * nf
    tile_expert = jnp.concatenate([tile_expert, tile_expert[-1:]])
    tile_valid = jnp.concatenate([tile_valid, jnp.zeros((1,), tile_valid.dtype)])
    src3 = jnp.concatenate([src, jnp.zeros((tm,), src.dtype)]).reshape(n_tiles, 1, tm)

    def wcol(i, j, te_ref, tv_ref):
        return (layer, te_ref[i], 0, jnp.where(tv_ref[i] == 1, j, nf - 1))

    def wrow(i, j, te_ref, tv_ref):
        return (layer, te_ref[i], jnp.where(tv_ref[i] == 1, j, nf - 1), 0)

    grid_spec = pltpu.PrefetchScalarGridSpec(
        num_scalar_prefetch=2,
        grid=(n_tiles, nf),
        in_specs=[
            pl.BlockSpec((1, 1, tm), lambda i, j, *_: (i, 0, 0), memory_space=pltpu.SMEM),
            pl.BlockSpec((1, 1, tm), lambda i, j, *_: (jnp.minimum(i + 1, n_tiles - 1), 0, 0),
                         memory_space=pltpu.SMEM),
            pl.BlockSpec(memory_space=pl.ANY),
            pl.BlockSpec((1, D_MODEL), lambda i, j, *_: (0, 0)),
            pl.BlockSpec((None, None, D_MODEL, tf), wcol),
            pl.BlockSpec((None, None, D_MODEL, tf), wcol),
            pl.BlockSpec((None, None, tf, D_MODEL), wrow),
        ],
        out_specs=pl.BlockSpec((tm, D_MODEL), lambda i, j, *_: (i, 0)),
        scratch_shapes=[pltpu.VMEM((buf_rows, D_MODEL), F32), pltpu.VMEM((tm, D_MODEL), BF16),
                        pltpu.SemaphoreType.DMA],
    )
    return pl.pallas_call(
        functools.partial(_moe_ffn_kernel, tm=tm, rows_per_step=rows_per_step),
        grid_spec=grid_spec,
        out_shape=jax.ShapeDtypeStruct((n_tiles * tm, D_MODEL), F32),
        compiler_params=_cparams(("arbitrary", "arbitrary")),
        name="moe_ffn",
    )(tile_expert, tile_valid, src3, src3, x, nw.reshape(1, D_MODEL), wg, wu, wd)


def _moe_combine_kernel(pos_ref, npos_ref, x_ref, gate_ref, y_hbm, fw_ref, o_ref, ybuf, sem, *, tm, final_norm):
    i = pl.program_id(0)
    n_tiles = pl.num_programs(0)
    cur = i % 2

    def row_copy(idx_ref, buf, r, k):
        slot = idx_ref[0, 0, r * TOP_K + k]
        return pltpu.make_async_copy(y_hbm.at[pl.ds(slot, 1)], ybuf.at[buf, k, pl.ds(r, 1)], sem.at[buf])

    def gather(idx_ref, buf):
        def start(r, c):
            for k in range(TOP_K):
                row_copy(idx_ref, buf, r, k).start(priority=k % 2)
            return c

        lax.fori_loop(0, tm, start, 0, unroll=8)

    @pl.when(i == 0)
    def _():
        gather(pos_ref, 0)

    @pl.when(i + 1 < n_tiles)
    def _():
        gather(npos_ref, 1 - cur)

    for k in range(TOP_K):
        pltpu.make_async_copy(y_hbm.at[pl.ds(0, tm)], ybuf.at[cur, k], sem.at[cur]).wait()
    gates = gate_ref[...]
    out = x_ref[...]
    for k in range(TOP_K):
        out = out + gates[:, k:k + 1] * ybuf[cur, k]
    if final_norm:
        out = _rms(out, fw_ref[...])
    o_ref[...] = out


def moe_combine(x, gates, y, pos, final_w, tm):
    n = x.shape[0]
    final_norm = final_w is not None
    fw = final_w if final_norm else jnp.ones((D_MODEL,), F32)
    n_tiles = n // tm
    pos3 = pos.reshape(n_tiles, 1, tm * TOP_K)
    return pl.pallas_call(
        functools.partial(_moe_combine_kernel, tm=tm, final_norm=final_norm),
        grid=(n_tiles,),
        in_specs=[
            pl.BlockSpec((1, 1, tm * TOP_K), lambda i: (i, 0, 0), memory_space=pltpu.SMEM),
            pl.BlockSpec((1, 1, tm * TOP_K), lambda i: (jnp.minimum(i + 1, n_tiles - 1), 0, 0),
                         memory_space=pltpu.SMEM),
            pl.BlockSpec((tm, D_MODEL), lambda i: (i, 0)),
            pl.BlockSpec((tm, TOP_K), lambda i: (i, 0)),
            pl.BlockSpec(memory_space=pl.ANY),
            pl.BlockSpec((1, D_MODEL), lambda i: (0, 0)),
        ],
        out_specs=pl.BlockSpec((tm, D_MODEL), lambda i: (i, 0)),
        out_shape=jax.ShapeDtypeStruct((n, D_MODEL), F32),
        scratch_shapes=[pltpu.VMEM((2, TOP_K, tm, D_MODEL), F32), pltpu.SemaphoreType.DMA((2,))],
        compiler_params=_cparams(("arbitrary",)),
        name="moe_combine",
    )(pos3, pos3, x, gates, y, fw.reshape(1, D_MODEL))


def _rmsnorm_kernel(x_ref, w_ref, o_ref):
    o_ref[...] = _rms(x_ref[...], w_ref[...])


def rmsnorm_rows(x, w, tm):
    n = x.shape[0]
    return pl.pallas_call(
        _rmsnorm_kernel,
        grid=(n // tm,),
        in_specs=[pl.BlockSpec((tm, D_MODEL), lambda i: (i, 0)),
                  pl.BlockSpec((1, D_MODEL), lambda i: (0, 0))],
        out_specs=pl.BlockSpec((tm, D_MODEL), lambda i: (i, 0)),
        out_shape=jax.ShapeDtypeStruct((n, D_MODEL), F32),
        compiler_params=_cparams(("parallel",)),
        name="final_norm",
    )(x, w.reshape(1, D_MODEL))


CONV_PAD = 8


def _conv_kernel(x_ref, w_ref, b_ref, o_ref, pad_ref, *, t):
    half = SSM_CONV // 2
    zeros = jnp.zeros((CONV_PAD, pad_ref.shape[1]), F32)
    pad_ref[0:CONV_PAD, :] = zeros
    pad_ref[CONV_PAD + t:CONV_PAD + t + CONV_PAD, :] = zeros
    pad_ref[CONV_PAD:CONV_PAD + t, :] = x_ref[...].astype(F32)
    acc = jnp.zeros(o_ref.shape, F32) + b_ref[...]
    for k in range(SSM_CONV):
        acc = acc + pad_ref[pl.ds(CONV_PAD - half + k, t), :] * w_ref[k:k + 1, :]
    o_ref[...] = (acc * jax.nn.sigmoid(acc)).astype(o_ref.dtype)


def conv_silu(proj, conv_w, conv_b, b, t):
    tc = 256
    nblk = SSM_CONV_CH // tc
    col0 = COL_XBC * SLAB // tc
    return pl.pallas_call(
        functools.partial(_conv_kernel, t=t),
        grid=(b, nblk),
        in_specs=[
            pl.BlockSpec((t, tc), lambda i, j: (i, col0 + j)),
            pl.BlockSpec((SSM_CONV, tc), lambda i, j: (0, j)),
            pl.BlockSpec((1, tc), lambda i, j: (0, j)),
        ],
        out_specs=pl.BlockSpec((t, tc), lambda i, j: (i, j)),
        out_shape=jax.ShapeDtypeStruct((b * t, SSM_CONV_CH), BF16),
        scratch_shapes=[pltpu.VMEM((t + 2 * CONV_PAD, tc), F32)],
        compiler_params=_cparams(("parallel", "parallel")),
        name="conv_silu",
    )(proj, conv_w, conv_b.reshape(1, SSM_CONV_CH))


def _ssd_stage1(xbc_ref, tail_ref, bias_ref, alog_ref, tri_ref, sel_ref, state_ref, direction):
    q = SSM_CHUNK
    dt = jax.nn.softplus(tail_ref[...] + bias_ref[...])
    da = dt * (-jnp.exp(alog_ref[...]))
    cs = jnp.dot(tri_ref[...], da, preferred_element_type=F32, precision=lax.Precision.HIGHEST)
    total = cs[q - 1:q, :]
    if direction == 0:
        e_out = cs
        e_in = total - cs
        e_seg = cs
    else:
        ex = cs - da
        e_out = total - ex
        e_in = ex
        e_seg = -ex
    dec_out_b = jnp.exp(e_out).astype(BF16)
    dec_in_dt_b = (jnp.exp(e_in) * dt).astype(BF16)
    pairs_per_group = N_PAIRS // SSM_GROUPS
    cbs, y_offs, in_scales = [], [], []
    for g in range(SSM_GROUPS):
        bm = xbc_ref[:, SSM_INNER + g * SSM_STATE:SSM_INNER + (g + 1) * SSM_STATE]
        cm = xbc_ref[:, SSM_INNER + (SSM_GROUPS + g) * SSM_STATE:SSM_INNER + (SSM_GROUPS + g + 1) * SSM_STATE]
        cbs.append(_dot_nt(cm, bm))
        for pair in range(g * pairs_per_group, (g + 1) * pairs_per_group):
            sel = sel_ref[direction, pair]
            y_offs.append(_dot(dec_out_b, sel) * _dot(cm, state_ref[pair].astype(BF16)))
            in_scales.append(_dot(dec_in_dt_b, sel))
    return dict(e_seg=e_seg, e_seg_t=jnp.transpose(e_seg), dt_t=jnp.transpose(dt), dec_tot=jnp.exp(total),
                cbs=cbs, y_offs=y_offs, in_scales=in_scales)


def _ssd_stage2(ctx, xbc_ref, y_ref, state_ref, direction):
    q = SSM_CHUNK
    row = lax.broadcasted_iota(jnp.int32, (q, q), 0)
    col = lax.broadcasted_iota(jnp.int32, (q, q), 1)
    keep = (row >= col) if direction == 0 else (col >= row)
    first_half = _lane_lt((q, HEAD_PAIR), HEAD_DIM)
    pairs_per_group = N_PAIRS // SSM_GROUPS
    for pair in range(N_PAIRS):
        g = pair // pairs_per_group
        bm = xbc_ref[:, SSM_INNER + g * SSM_STATE:SSM_INNER + (g + 1) * SSM_STATE]
        xs_pair = xbc_ref[:, pair * HEAD_PAIR:(pair + 1) * HEAD_PAIR]
        ys, decs = [], []
        for hh in range(2):
            lane = TAIL_DT + direction * SSM_HEADS + pair * 2 + hh
            seg = ctx["e_seg"][:, lane:lane + 1] - ctx["e_seg_t"][lane:lane + 1, :]
            lmat = jnp.where(keep, jnp.exp(seg), 0.0)
            w = (ctx["cbs"][g] * lmat * ctx["dt_t"][lane:lane + 1, :]).astype(BF16)
            ys.append(_dot(w, xs_pair))
            decs.append(ctx["dec_tot"][:, lane:lane + 1])
        y_ref[:, pair * HEAD_PAIR:(pair + 1) * HEAD_PAIR] = jnp.where(first_half, ys[0], ys[1]) + ctx["y_offs"][pair]
        st = state_ref[pair]
        first_half_s = _lane_lt(st.shape, HEAD_DIM)
        state_ref[pair] = (st * jnp.where(first_half_s, decs[0], decs[1])
                           + _dot_tn(bm, (xs_pair * ctx["in_scales"][pair]).astype(BF16)))


def _ssd_kernel(xbc_f_ref, tail_f_ref, xbc_b_ref, tail_b_ref, bias_ref, alog_ref, tri_ref, sel_ref,
                y_f_ref, y_b_ref, state_ref):
    @pl.when(pl.program_id(1) == 0)
    def _():
        state_ref[...] = jnp.zeros_like(state_ref)

    q = SSM_CHUNK
    for k in range(SSD_CHUNKS_PER_STEP):
        ins = []
        for d, refs in enumerate(((xbc_f_ref, tail_f_ref, y_f_ref), (xbc_b_ref, tail_b_ref, y_b_ref))):
            sub = k if d == 0 else SSD_CHUNKS_PER_STEP - 1 - k
            ins.append(tuple(r.at[sub * q:(sub + 1) * q] for r in refs))
        ctxs = [_ssd_stage1(xbc_ref, tail_ref, bias_ref, alog_ref, tri_ref, sel_ref, state_ref.at[d], d)
                for d, (xbc_ref, tail_ref, _) in enumerate(ins)]
        for d, (xbc_ref, _, y_ref) in enumerate(ins):
            _ssd_stage2(ctxs[d], xbc_ref, y_ref, state_ref.at[d], d)


def _ssd_lane_selectors():
    sel = np.zeros((2, N_PAIRS, LANES, HEAD_PAIR), np.float32)
    for d in range(2):
        for pair in range(N_PAIRS):
            for hh in range(2):
                sel[d, pair, TAIL_DT + d * SSM_HEADS + pair * 2 + hh, hh * HEAD_DIM:(hh + 1) * HEAD_DIM] = 1.0
    return jnp.asarray(sel, BF16)


def ssd_scan(xbc, tail, dt_bias, a_log, b, t):
    q = SSM_CHUNK
    rows = q * SSD_CHUNKS_PER_STEP
    nc = t // rows
    bias_row = jnp.zeros((1, LANES), F32).at[0, TAIL_DT:TAIL_DT + 2 * SSM_HEADS].set(dt_bias.reshape(-1))
    alog_row = jnp.zeros((1, LANES), F32).at[0, TAIL_DT:TAIL_DT + 2 * SSM_HEADS].set(a_log.reshape(-1))
    tri = jnp.asarray(np.tril(np.ones((q, q), np.float32)))
    fwd = lambda i, c: (i * nc + c, 0)
    bwd = lambda i, c: (i * nc + nc - 1 - c, 0)
    const = lambda i, c: (0, 0)
    return pl.pallas_call(
        _ssd_kernel,
        grid=(b, nc),
        in_specs=[
            pl.BlockSpec((rows, SSM_CONV_CH), fwd), pl.BlockSpec((rows, LANES), fwd),
            pl.BlockSpec((rows, SSM_CONV_CH), bwd), pl.BlockSpec((rows, LANES), bwd),
            pl.BlockSpec((1, LANES), const), pl.BlockSpec((1, LANES), const), pl.BlockSpec((q, q), const),
            pl.BlockSpec((2, N_PAIRS, LANES, HEAD_PAIR), lambda i, c: (0, 0, 0, 0)),
        ],
        out_specs=[pl.BlockSpec((rows, SSM_INNER), fwd), pl.BlockSpec((rows, SSM_INNER), bwd)],
        out_shape=[jax.ShapeDtypeStruct((b * t, SSM_INNER), F32)] * 2,
        scratch_shapes=[pltpu.VMEM((2, N_PAIRS, SSM_STATE, HEAD_PAIR), F32)],
        compiler_params=_cparams(("parallel", "arbitrary")),
        name="ssd_scan",
    )(xbc, tail, xbc, tail, bias_row, alog_row, tri, _ssd_lane_selectors())


def _ssd_combine_kernel(yf_ref, yb_ref, xs_ref, z_ref, d_ref, nw_ref, o_ref):
    y = yf_ref[...] + yb_ref[...] + xs_ref[...].astype(F32) * d_ref[...]
    z = z_ref[...].astype(F32)
    o_ref[...] = _rms(y * (z * jax.nn.sigmoid(z)), nw_ref[...]).astype(o_ref.dtype)


def ssd_combine(y_f, y_b, xbc, proj, d_skip, norm_w, tm):
    n = y_f.shape[0]
    d_row = jnp.repeat(d_skip, SSM_HEAD_DIM).reshape(1, SSM_INNER)
    row = lambda i: (i, 0)
    return pl.pallas_call(
        _ssd_combine_kernel,
        grid=(n // tm,),
        in_specs=[
            pl.BlockSpec((tm, SSM_INNER), row),
            pl.BlockSpec((tm, SSM_INNER), row),
            pl.BlockSpec((tm, SSM_INNER), row),
            pl.BlockSpec((tm, SLAB), lambda i: (i, COL_Z)),
            pl.BlockSpec((1, SSM_INNER), lambda i: (0, 0)),
            pl.BlockSpec((1, SSM_INNER), lambda i: (0, 0)),
        ],
        out_specs=pl.BlockSpec((tm, SSM_INNER), row),
        out_shape=jax.ShapeDtypeStruct((n, SSM_INNER), BF16),
        compiler_params=_cparams(("parallel",)),
        name="ssd_combine",
    )(y_f, y_b, xbc, proj, d_row, norm_w.reshape(1, SSM_INNER))


def _pair_scores(q2, k2):
    first_q = _lane_lt(q2.shape, HEAD_DIM)
    zero = jnp.zeros_like(q2)
    qs = jnp.concatenate([jnp.where(first_q, q2, zero), jnp.where(first_q, zero, q2)], axis=0)
    return _dot_nt(qs, k2)


def _pair_attend(s, v2):
    tq = s.shape[0] // 2
    m = jnp.max(s, axis=-1, keepdims=True)
    p = jnp.exp(s - m)
    l = jnp.sum(p, axis=-1, keepdims=True)
    o = _dot(p.astype(BF16), v2) / l
    lse = m + jnp.log(l)
    first_o = _lane_lt((tq, HEAD_PAIR), HEAD_DIM)
    return jnp.where(first_o, o[:tq], o[tq:]), jnp.where(first_o, lse[:tq], lse[tq:])


def na_bias_tables(rpb, rows):
    kr = min(NA_WIN_ROWS, rows)
    qc = np.arange(GRID_W)
    kc = np.arange(GRID_W)
    q_start = np.clip(qc - NA_WIN_COLS // 2, 0, GRID_W - NA_WIN_COLS)
    col_in = (kc[None, :] >= q_start[:, None]) & (kc[None, :] < q_start[:, None] + NA_WIN_COLS)
    col_off = np.clip(kc[None, :] - qc[:, None] + NA_WIN_COLS - 1, 0, 2 * NA_WIN_COLS - 2)
    onehot = (col_off[None] == np.arange(2 * NA_WIN_COLS - 1)[:, None, None]).astype(np.float32)
    expanded = jnp.einsum("hrc,cqk->hqrk", rpb, jnp.asarray(onehot), precision=lax.Precision.HIGHEST)
    expanded = jnp.where(jnp.asarray(col_in)[None, :, None, :], expanded, NEG_INF)

    def table(r):
        row_start = int(np.clip(r - kr // 2, 0, rows - kr))
        ro0 = row_start - r + NA_WIN_ROWS - 1
        return expanded[:, :, ro0:ro0 + kr, :].reshape(N_PAIRS, 2 * GRID_W, kr * GRID_W)

    rs = NA_ROWS_PER_STEP
    lo = [table(r) for r in range(rs)]
    mid = [table(min(rs, rows - 1))] * rs
    hi = [table(r) for r in range(rows - rs, rows)]
    return jnp.stack([jnp.stack(lo), jnp.stack(mid), jnp.stack(hi)])


def _na_kernel(q_ref, k_ref, v_ref, bias_ref, o_ref, *, rows, kr):
    step = pl.program_id(1)
    rs = NA_ROWS_PER_STEP
    for rr in range(rs):
        r = step * rs + rr
        row_start = jnp.clip(r - kr // 2, 0, rows - kr)
        k0 = pl.multiple_of(row_start * GRID_W, GRID_W)
        pair_cols = [slice(pair * HEAD_PAIR, (pair + 1) * HEAD_PAIR) for pair in range(N_PAIRS)]
        scores = []
        for pair, cols in enumerate(pair_cols):
            q2 = q_ref[rr * GRID_W:(rr + 1) * GRID_W, cols] * jnp.asarray(HEAD_DIM ** -0.5, BF16)
            scores.append(_pair_scores(q2, k_ref[pl.ds(k0, kr * GRID_W), cols]) + bias_ref[0, rr, pair])
        for s, cols in zip(scores, pair_cols):
            o, _ = _pair_attend(s, v_ref[pl.ds(k0, kr * GRID_W), cols])
            o_ref[rr * GRID_W:(rr + 1) * GRID_W, cols] = o.astype(o_ref.dtype)


def na_attention(proj, rpb, b, t):
    rows = t // GRID_W
    kr = min(NA_WIN_ROWS, rows)
    rs = NA_ROWS_PER_STEP
    nsteps = rows // rs
    bias = na_bias_tables(rpb, rows)

    def kind(i, s):
        return jnp.where(s == 0, 0, jnp.where(s == nsteps - 1, 2, 1))

    return pl.pallas_call(
        functools.partial(_na_kernel, rows=rows, kr=kr),
        grid=(b, nsteps),
        in_specs=[
            pl.BlockSpec((rs * GRID_W, SLAB), lambda i, s: (i * nsteps + s, COL_NAQ)),
            pl.BlockSpec((t, SLAB), lambda i, s: (i, COL_NAK)),
            pl.BlockSpec((t, SLAB), lambda i, s: (i, COL_NAV)),
            pl.BlockSpec((1, rs, N_PAIRS, 2 * GRID_W, kr * GRID_W), lambda i, s: (kind(i, s), 0, 0, 0, 0)),
        ],
        out_specs=pl.BlockSpec((rs * GRID_W, SLAB), lambda i, s: (i * nsteps + s, 0)),
        out_shape=jax.ShapeDtypeStruct((b * t, SLAB), BF16),
        compiler_params=_cparams(("parallel", "arbitrary")),
        name="na_attention",
    )(proj, proj, proj, bias)


def _rope_angles(t, d):
    inv = ROPE_THETA ** (-np.arange(0, d, 2, dtype=np.float32) / d)
    return np.arange(t, dtype=np.float32)[:, None] * inv[None, :]


def rope_tables_pair(t):
    ang = _rope_angles(t, HEAD_DIM)
    cos = np.tile(np.cos(ang), (1, 4))
    sin = np.tile(np.concatenate([-np.sin(ang), np.sin(ang)], axis=1), (1, 2))
    return jnp.asarray(cos, F32), jnp.asarray(sin, F32)


FOLD_CHUNK = 256
FOLD_DILS = tuple(d for _, d in DIL_PAIRS if d > 1)


def fold_permutation(dil):
    per = FOLD_CHUNK // dil
    perm = np.zeros((FOLD_CHUNK, FOLD_CHUNK), np.float32)
    dst = np.arange(FOLD_CHUNK)
    perm[dst, (dst % per) * dil + dst // per] = 1.0
    return jnp.asarray(perm, BF16)


def _rope_qkv_kernel(x_ref, v_ref, cos_ref, sin_ref, *rest):
    nd = len(FOLD_DILS)
    perm_refs, o_ref, fold_refs = rest[:nd], rest[nd], rest[nd + 1:]
    cos = cos_ref[...]
    sin = sin_ref[...]
    half = HEAD_DIM // 2
    for c in range(x_ref.shape[1] // LANES):
        x = x_ref[:, c * LANES:(c + 1) * LANES].astype(F32)
        rot = jnp.where(_lane_lt(x.shape, half, HEAD_DIM),
                        pltpu.roll(x, LANES - half, 1), pltpu.roll(x, half, 1))
        y = x * cos + rot * sin
        if c < N_PAIRS:
            y = y * (HEAD_DIM ** -0.5)
        o_ref[0, :, c * LANES:(c + 1) * LANES] = y.astype(o_ref.dtype)
    o_ref[0, :, 2 * SLAB:3 * SLAB] = v_ref[...]
    tm = x_ref.shape[0]
    for dil, perm_ref, f_ref in zip(FOLD_DILS, perm_refs, fold_refs):
        per = FOLD_CHUNK // dil
        for c in range(tm // FOLD_CHUNK):
            folded = _dot(perm_ref[...], o_ref[0, c * FOLD_CHUNK:(c + 1) * FOLD_CHUNK, :]).astype(f_ref.dtype)
            for p in range(dil):
                f_ref[p, c * per:(c + 1) * per, :] = folded[p * per:(p + 1) * per, :]


def rope_qkv(proj, b, t, tm):
    n = b * t
    cos, sin = rope_tables_pair(t)
    nb = t // tm
    fold_spec = lambda d: pl.BlockSpec((None, d, tm // d, 3 * SLAB), lambda i: (i // nb, 0, i % nb, 0))
    outs = pl.pallas_call(
        _rope_qkv_kernel,
        grid=(n // tm,),
        in_specs=[
            pl.BlockSpec((tm, 2 * SLAB), lambda i: (i, COL_DLQ // 2)),
            pl.BlockSpec((tm, SLAB), lambda i: (i, COL_DLV)),
            pl.BlockSpec((tm, LANES), lambda i: (i % nb, 0)),
            pl.BlockSpec((tm, LANES), lambda i: (i % nb, 0)),
        ] + [pl.BlockSpec((FOLD_CHUNK, FOLD_CHUNK), lambda i: (0, 0))] * len(FOLD_DILS),
        out_specs=[fold_spec(1)] + [fold_spec(d) for d in FOLD_DILS],
        out_shape=[jax.ShapeDtypeStruct((b, d, t // d, 3 * SLAB), BF16) for d in (1,) + FOLD_DILS],
        compiler_params=_cparams(("parallel",)),
        name="rope_qkv",
    )(proj, proj, cos, sin, *[fold_permutation(d) for d in FOLD_DILS])
    by_dil = dict(zip((1,) + FOLD_DILS, outs))
    return [by_dil[d] for _, d in DIL_PAIRS]


def _band_kernel(q_ref, k_ref, v_ref, o_ref, lse_ref, *, sub, half, span):
    tq = DIL_QBLOCK
    blocks = q_ref.shape[0] // tq
    for blk in range(blocks):
        qb = pl.program_id(2) * blocks + blk
        rows = slice(blk * tq, (blk + 1) * tq)
        start = jnp.clip(qb * tq - half, 0, sub - span)
        start = pl.multiple_of(start, half)
        q_pos = qb * tq + lax.broadcasted_iota(jnp.int32, (2 * tq, span), 0) % tq
        k_pos = start + lax.broadcasted_iota(jnp.int32, (2 * tq, span), 1)
        valid = jnp.abs(k_pos - q_pos) <= half
        pair_cols = [slice(pair * HEAD_PAIR, (pair + 1) * HEAD_PAIR) for pair in range(N_PAIRS)]
        scores = [jnp.where(valid, _pair_scores(q_ref[rows, cols], k_ref[pl.ds(start, span), cols]), NEG_INF)
                  for cols in pair_cols]
        for s, cols in zip(scores, pair_cols):
            o, lse = _pair_attend(s, v_ref[pl.ds(start, span), cols])
            o_ref[rows, cols] = o
            lse_ref[rows, cols] = lse


def band_attention(qkv, window, dil):
    b, _, sub, _ = qkv.shape
    half = window // (2 * dil)
    span = DIL_QBLOCK + 2 * half
    tq = DIL_QBLOCK * min(BAND_BLOCKS_PER_STEP, sub // DIL_QBLOCK)
    nqb = sub // tq
    return pl.pallas_call(
        functools.partial(_band_kernel, sub=sub, half=half, span=span),
        grid=(b, dil, nqb),
        in_specs=[
            pl.BlockSpec((None, None, tq, SLAB), lambda i, p, s: (i, p, s, 0)),
            pl.BlockSpec((None, None, sub, SLAB), lambda i, p, s: (i, p, 0, 1)),
            pl.BlockSpec((None, None, sub, SLAB), lambda i, p, s: (i, p, 0, 2)),
        ],
        out_specs=[pl.BlockSpec((None, None, tq, SLAB), lambda i, p, s: (i, p, s, 0))] * 2,
        out_shape=[jax.ShapeDtypeStruct((b, dil, sub, SLAB), F32)] * 2,
        compiler_params=_cparams(("parallel", "parallel", "arbitrary")),
        name="band_attention_d%d" % dil,
    )(qkv, qkv, qkv)


def _dil_combine_kernel(*refs):
    nbr = len(DIL_PAIRS)
    o_refs, l_refs, out_ref = refs[:nbr], refs[nbr:2 * nbr], refs[2 * nbr]
    scratch = iter(refs[2 * nbr + 1:])

    def token_order(ref):
        dil = ref.shape[0]
        if dil == 1:
            return ref[0]
        buf = next(scratch)
        per = ref.shape[1]
        for p in range(dil):
            for c in range(SLAB // LANES):
                buf[c, pl.ds(p, per, stride=dil), :] = ref[p, :, c * LANES:(c + 1) * LANES]
        return jnp.concatenate([buf[c] for c in range(SLAB // LANES)], axis=1)

    os = [token_order(r) for r in o_refs]
    lses = [token_order(r) for r in l_refs]
    m = functools.reduce(jnp.maximum, lses)
    ws = [jnp.exp(l - m) for l in lses]
    den = functools.reduce(jnp.add, ws)
    acc = functools.reduce(jnp.add, [(w / den) * o for w, o in zip(ws, os)])
    out_ref[...] = acc.astype(out_ref.dtype)


def dil_combine(outs, lses, tm):
    b, _, t, _ = outs[0].shape
    n = b * t
    nb = t // tm
    spec = lambda a: pl.BlockSpec((None, a.shape[1], tm // a.shape[1], SLAB), lambda i: (i // nb, 0, i % nb, 0))
    n_folded = sum(1 for a in outs + lses if a.shape[1] > 1)
    return pl.pallas_call(
        _dil_combine_kernel,
        grid=(n // tm,),
        in_specs=[spec(a) for a in outs + lses],
        out_specs=pl.BlockSpec((tm, SLAB), lambda i: (i, 0)),
        out_shape=jax.ShapeDtypeStruct((n, SLAB), BF16),
        scratch_shapes=[pltpu.VMEM((SLAB // LANES, tm, LANES), F32)] * n_folded,
        compiler_params=_cparams(("parallel",)),
        name="dil_combine",
    )(*outs, *lses)


MLA_QK = MLA_NOPE + MLA_ROPE


def mla_tables(t):
    ang = _rope_angles(t, MLA_ROPE)
    cos2 = np.concatenate([np.cos(ang), np.cos(ang)], axis=1)
    sin2 = np.concatenate([np.sin(ang), np.sin(ang)], axis=1)
    z = lambda w: np.zeros((t, w), np.float32)
    q_cos = np.concatenate([np.ones((t, MLA_NOPE), np.float32), cos2, z(LANES - MLA_QK)], axis=1)
    q_sin = np.concatenate([z(MLA_NOPE), sin2, z(LANES - MLA_QK)], axis=1)
    k_cos = np.concatenate([cos2, z(LANES - MLA_ROPE)], axis=1)
    k_sin = np.concatenate([-sin2[:, :MLA_ROPE // 2], sin2[:, MLA_ROPE // 2:], z(LANES - MLA_ROPE)], axis=1)
    return tuple(jnp.asarray(a, F32) for a in (q_cos, q_sin, k_cos, k_sin))


def mla_weights(w_uq, w_ukv):
    hq = w_uq.reshape(MLA_Q_RANK, MLA_HEADS, MLA_QK)
    nope, pe = hq[..., :MLA_NOPE], hq[..., MLA_NOPE:]
    pe_rot = jnp.concatenate([-pe[..., MLA_ROPE // 2:], pe[..., :MLA_ROPE // 2]], axis=-1)
    zq = jnp.zeros((MLA_Q_RANK, MLA_HEADS, LANES - MLA_QK), w_uq.dtype)
    w1 = jnp.concatenate([nope, pe, zq], axis=-1).reshape(MLA_Q_RANK, MLA_HEADS * LANES)
    w2 = jnp.concatenate([jnp.zeros_like(nope), pe_rot, zq], axis=-1).reshape(MLA_Q_RANK, MLA_HEADS * LANES)
    hkv = w_ukv.reshape(MLA_KV_RANK, MLA_HEADS, MLA_NOPE + MLA_V)
    k_nope, v = hkv[..., :MLA_NOPE], hkv[..., MLA_NOPE:]
    zk = jnp.zeros((MLA_KV_RANK, MLA_HEADS, LANES - MLA_NOPE), w_ukv.dtype)
    wk = jnp.concatenate([k_nope, zk], axis=-1).reshape(MLA_KV_RANK, MLA_HEADS * LANES)
    zv = jnp.zeros((MLA_KV_RANK, MLA_HEADS, LANES - MLA_V), w_ukv.dtype)
    wv = jnp.concatenate([v, zv], axis=-1).reshape(MLA_KV_RANK, MLA_HEADS * LANES)
    place = np.zeros((LANES, MLA_HEADS * LANES), np.float32)
    ones = np.zeros((1, MLA_HEADS * LANES), np.float32)
    for h in range(MLA_HEADS):
        place[np.arange(MLA_ROPE), h * LANES + MLA_NOPE + np.arange(MLA_ROPE)] = 1.0
        ones[0, h * LANES + MLA_V] = 1.0
    return (w1.astype(BF16), w2.astype(BF16), wk.astype(BF16), wv.astype(BF16), jnp.asarray(place, BF16),
            jnp.asarray(ones, F32))


def _mla_q_kernel(c_ref, nw_ref, w1_ref, w2_ref, cos_ref, sin_ref, o_ref):
    cn = _rms(c_ref[...].astype(F32), nw_ref[...]).astype(BF16)
    cos = jnp.tile(cos_ref[...], (1, MLA_HEADS))
    sin = jnp.tile(sin_ref[...], (1, MLA_HEADS))
    q = _dot(cn, w1_ref[...]) * cos + _dot(cn, w2_ref[...]) * sin
    o_ref[...] = (q * (MLA_QK ** -0.5 * math.log2(math.e))).astype(o_ref.dtype)


def _mla_kv_kernel(c_ref, tail_ref, nw_ref, wk_ref, wv_ref, place_ref, ones_ref, cos_ref, sin_ref,
                   k_ref, v_ref):
    cn = _rms(c_ref[...].astype(F32), nw_ref[...]).astype(BF16)
    kr = tail_ref[...]
    half = MLA_ROPE // 2
    rot = jnp.where(_lane_lt(kr.shape, half), pltpu.roll(kr, LANES - half, 1), pltpu.roll(kr, half, 1))
    k_pe = (kr * cos_ref[...] + rot * sin_ref[...]).astype(BF16)
    k_ref[...] = (_dot(cn, wk_ref[...]) + _dot(k_pe, place_ref[...])).astype(k_ref.dtype)
    v_ref[...] = (_dot(cn, wv_ref[...]) + ones_ref[...]).astype(v_ref.dtype)


def mla_project(proj, tail, q_norm_w, kv_norm_w, w_uq, w_ukv, b, t, tm):
    n = b * t
    nb = t // tm
    w1, w2, wk, wv, place, ones = mla_weights(w_uq, w_ukv)
    q_cos, q_sin, k_cos, k_sin = mla_tables(t)
    wide = MLA_HEADS * LANES
    full = lambda shape: pl.BlockSpec(shape, lambda i: (0, 0))
    tab = pl.BlockSpec((tm, LANES), lambda i: (i % nb, 0))
    qf = pl.pallas_call(
        _mla_q_kernel,
        grid=(n // tm,),
        in_specs=[pl.BlockSpec((tm, SLAB), lambda i: (i, COL_CQ)), full((1, MLA_Q_RANK)),
                  full((MLA_Q_RANK, wide)), full((MLA_Q_RANK, wide)), tab, tab],
        out_specs=pl.BlockSpec((tm, wide), lambda i: (i, 0)),
        out_shape=jax.ShapeDtypeStruct((n, wide), BF16),
        compiler_params=_cparams(("parallel",)),
        name="mla_q_proj",
    )(proj, q_norm_w.reshape(1, MLA_Q_RANK), w1, w2, q_cos, q_sin)
    kf, vf = pl.pallas_call(
        _mla_kv_kernel,
        grid=(n // tm,),
        in_specs=[pl.BlockSpec((tm, SLAB), lambda i: (i, COL_CKV)),
                  pl.BlockSpec((tm, LANES), lambda i: (i, 0)), full((1, MLA_KV_RANK)),
                  full((MLA_KV_RANK, wide)), full((MLA_KV_RANK, wide)), full((LANES, wide)), full((1, wide)),
                  tab, tab],
        out_specs=[pl.BlockSpec((tm, wide), lambda i: (i, 0))] * 2,
        out_shape=[jax.ShapeDtypeStruct((n, wide), BF16)] * 2,
        compiler_params=_cparams(("parallel",)),
        name="mla_kv_proj",
    )(proj, tail, kv_norm_w.reshape(1, MLA_KV_RANK), wk, wv, place, ones, k_cos, k_sin)
    return qf, kf, vf


def _mla_attn_kernel(q_ref, k_ref, v_ref, *rest, t, tk, n_cast):
    cast_in, o_ref, cast_out = rest[:n_cast], rest[n_cast], rest[n_cast + 1:]
    for src_ref, dst_ref in zip(cast_in, cast_out):
        dst_ref[...] = src_ref[...].astype(dst_ref.dtype)
    tq = q_ref.shape[0]
    groups = [slice(hh * LANES, (hh + 1) * LANES) for hh in range(2)]
    qs = [q_ref[:, grp] for grp in groups]

    def scores(c):
        return [_dot_nt(q, k_ref[c * tk:(c + 1) * tk, grp]) for q, grp in zip(qs, groups)]

    n_chunks = t // tk
    ms = [jnp.full((tq, 1), -jnp.inf, F32)] * 2
    accs = [jnp.zeros((tq, LANES), F32)] * 2
    s_next = scores(0)
    for c in range(n_chunks):
        s_cur = s_next
        if c + 1 < n_chunks:
            s_next = scores(c + 1)
        for hh, grp in enumerate(groups):
            m_new = jnp.maximum(ms[hh], jnp.max(s_cur[hh], axis=-1, keepdims=True))
            p = jnp.exp2((s_cur[hh] - m_new).astype(BF16))
            accs[hh] = jnp.exp2(ms[hh] - m_new) * accs[hh] + _dot(p, v_ref[c * tk:(c + 1) * tk, grp])
            ms[hh] = m_new
    outs = [acc / acc[:, MLA_V:MLA_V + 1] for acc in accs]
    first = _lane_lt((tq, LANES), MLA_V)
    o_ref[...] = jnp.where(first, outs[0], pltpu.roll(outs[1], MLA_V, 1)).astype(o_ref.dtype)


def mla_cast_rows(w, b, t, tq):
    steps = b * N_PAIRS * (t // tq)
    rows = int(np.prod(w.shape[:-1]))
    per = rows // steps
    return per if rows % steps == 0 and per % 16 == 0 else None


def mla_attention(qf, kf, vf, b, t, tq, tk, cast=()):
    n = b * t
    nq = t // tq
    step = lambda i, p, s: ((i * N_PAIRS + p) * nq + s, 0)
    cast2d = [w.reshape(-1, w.shape[-1]) for w in cast]
    cast_specs = [pl.BlockSpec((mla_cast_rows(w, b, t, tq), w2.shape[1]), step) for w, w2 in zip(cast, cast2d)]
    outs = pl.pallas_call(
        functools.partial(_mla_attn_kernel, t=t, tk=tk, n_cast=len(cast)),
        grid=(b, N_PAIRS, nq),
        in_specs=[
            pl.BlockSpec((tq, 2 * LANES), lambda i, p, s: (i * nq + s, p)),
            pl.BlockSpec((t, 2 * LANES), lambda i, p, s: (i, p)),
            pl.BlockSpec((t, 2 * LANES), lambda i, p, s: (i, p)),
        ] + cast_specs,
        out_specs=[pl.BlockSpec((tq, HEAD_PAIR), lambda i, p, s: (i * nq + s, p))] + cast_specs,
        out_shape=[jax.ShapeDtypeStruct((n, SLAB), BF16)]
        + [jax.ShapeDtypeStruct(w2.shape, BF16) for w2 in cast2d],
        compiler_params=_cparams(("parallel", "parallel", "arbitrary")),
        name="mla_attention",
    )(qf, kf, vf, *cast2d)
    return outs[0], [o.reshape(w.shape) for o, w in zip(outs[1:], cast)]


def _in_proj_segments():
    sizes = (SSM_INNER, SSM_CONV_CH, 2 * SSM_HEADS, SLAB, SLAB, SLAB, MLA_Q_RANK, MLA_KV_RANK, MLA_ROPE,
             SLAB, SLAB, SLAB)
    off = [int(v) for v in np.concatenate([[0], np.cumsum(sizes)])]
    main = ((off[0], off[2]), (off[3], off[8]), (off[9], off[12]))
    tail = ((off[8], off[9]), (off[2], off[3]))
    return main, tail


def _in_proj_columns():
    main, tail = _in_proj_segments()
    cols = lambda segs: np.concatenate([np.arange(a, b) for a, b in segs])
    return cols(main), cols(tail)


def in_proj_weights(w_in_l):
    main, tail = _in_proj_segments()
    w_main = jnp.concatenate([w_in_l[:, a:b] for a, b in main], axis=1).astype(BF16)
    pad = jnp.zeros((D_MODEL, LANES - sum(b - a for a, b in tail)), w_in_l.dtype)
    w_tail = jnp.concatenate([w_in_l[:, a:b] for a, b in tail] + [pad], axis=1).astype(BF16)
    return w_main, w_tail


MLA_TQ = 1024
MLA_CASTS_PER_CALL = 2


def mixers(proj, tail, p, l, b, t, cast):
    xbc = conv_silu(proj, p["conv_w"][l], p["conv_b"][l], b, t)
    y_f, y_b = ssd_scan(xbc, tail, p["dt_bias"][l], p["a_log"][l], b, t)
    y_ssm = ssd_combine(y_f, y_b, xbc, proj, p["d_skip"][l], p["ssm_norm_w"][l], 1024)

    y_na = na_attention(proj, p["na_rpb"][l], b, t)

    qf, kf, vf = mla_project(proj, tail, p["mla_q_norm_w"][l], p["mla_kv_norm_w"][l],
                             p["mla_w_uq"][l], p["mla_w_ukv"][l], b, t, 512)
    y_mla, cast_out = mla_attention(qf, kf, vf, b, t, MLA_TQ, 512, cast)

    qkvs = rope_qkv(proj, b, t, 1024)
    outs, lses = zip(*[band_attention(qkv, w, d) for qkv, (w, d) in zip(qkvs, DIL_PAIRS)])
    y_dil = dil_combine(outs, lses, 1024)
    return (y_ssm, y_na, y_mla, y_dil), cast_out


def kernel(x, attn_norm_w, w_in, conv_w, conv_b, a_log, dt_bias, d_skip, ssm_norm_w, na_rpb,
           mla_q_norm_w, mla_kv_norm_w, mla_w_uq, mla_w_ukv, w_o, ffn_norm_w, ffn_w_gate, ffn_w_up,
           ffn_w_down, router_w, exp_w_gate, exp_w_up, exp_w_down, final_norm_w):
    b, t, _ = x.shape
    n = b * t
    depth = w_in.shape[0]
    p = dict(conv_w=conv_w, conv_b=conv_b, a_log=a_log, dt_bias=dt_bias, d_skip=d_skip,
             ssm_norm_w=ssm_norm_w, na_rpb=na_rpb, mla_q_norm_w=mla_q_norm_w,
             mla_kv_norm_w=mla_kv_norm_w, mla_w_uq=mla_w_uq, mla_w_ukv=mla_w_ukv)
    x = x.reshape(n, D_MODEL)
    cast_rows = 256
    w_o_b = cast_bf16(w_o, cast_rows)
    ffn_b = [cast_bf16(w, cast_rows) for w in (ffn_w_gate, ffn_w_up, ffn_w_down)]
    exp_f32 = [exp_w_gate, exp_w_up, exp_w_down]
    exp_b = [None] * len(exp_f32)
    pending = [k for k, w in enumerate(exp_f32) if mla_cast_rows(w, b, t, MLA_TQ) is not None]
    moe_tm = 512
    normed = False
    for l in range(depth):
        w_main, w_tail = in_proj_weights(w_in[l])
        proj, tail = in_proj(x, attn_norm_w[l], w_main, w_tail, 512, PROJ_MAIN // 2)
        jobs, pending = pending[:MLA_CASTS_PER_CALL], pending[MLA_CASTS_PER_CALL:]
        mix, cast_out = mixers(proj, tail, p, l, b, t, [exp_f32[k] for k in jobs])
        for k, w_b in zip(jobs, cast_out):
            exp_b[k] = w_b
        x = out_proj(mix, w_o_b, l, x, 1024, 1024)
        j = l // 2
        if l % 2 == 0:
            x = ffn_dense(x, ffn_norm_w[l], *ffn_b, j, 1024, 512)
        else:
            pending = []
            exp_b = [cast_bf16(w, cast_rows) if w_b is None else w_b for w, w_b in zip(exp_f32, exp_b)]
            top_i, gates = moe_router(x, ffn_norm_w[l], router_w[j], 512)
            src, pos, tile_expert, tile_valid = moe_plan(top_i, moe_tm)
            y = moe_ffn(x, ffn_norm_w[l], *exp_b, j, src, tile_expert, tile_valid, moe_tm, 512)
            last = l == depth - 1
            x = moe_combine(x, gates, y, pos, final_norm_w if last else None, 256)
            normed = last
    if not normed:
        x = rmsnorm_rows(x, final_norm_w, 1024)
    return x.reshape(b, t, D_MODEL)
```

```python
import functools
import math

import numpy as np
import jax
import jax.numpy as jnp
from jax import lax
from jax.experimental import pallas as pl
from jax.experimental.pallas import tpu as pltpu

F32 = jnp.float32
BF16 = jnp.bfloat16

D_MODEL = 2048
GRID_W = 64
HEAD_DIM = 64
ROPE_THETA = 10000.0
NORM_EPS = 1e-6
NEG_INF = -1e30

SSM_HEADS = 8
SSM_HEAD_DIM = 64
SSM_INNER = SSM_HEADS * SSM_HEAD_DIM
SSM_GROUPS = 2
SSM_STATE = 128
SSM_CONV = 5
SSM_CHUNK = 128
SSD_CHUNKS_PER_STEP = 2
SSM_CONV_CH = SSM_INNER + 2 * SSM_GROUPS * SSM_STATE

NA_HEADS = 8
NA_WIN_ROWS = 8
NA_WIN_COLS = 16
NA_COL_BLOCK = 16
NA_KEY_COLS = 32
NA_ROWS_PER_STEP = 8

MLA_HEADS = 8
MLA_Q_RANK = 512
MLA_KV_RANK = 512
MLA_NOPE = 64
MLA_ROPE = 32
MLA_V = 64

DIL_HEADS = 8
DIL_PAIRS = ((128, 1), (512, 4), (2048, 16))
DIL_QBLOCK = 128
BAND_BLOCKS_PER_STEP = 4

N_EXPERTS = 8
TOP_K = 2
SPLIT_PARTS = 3

LANES = 128
HEAD_PAIR = 2 * HEAD_DIM
N_PAIRS = 4
SLAB = 512

COL_Z, COL_XBC, COL_NAQ, COL_NAK, COL_NAV, COL_CQ, COL_CKV, COL_DLQ, COL_DLK, COL_DLV = (
    0, 1, 3, 4, 5, 6, 7, 8, 9, 10)
PROJ_MAIN = 11 * SLAB
TAIL_DT = 32

VMEM_LIMIT = 56 * 1024 * 1024


def _cparams(sem, vmem=VMEM_LIMIT):
    return pltpu.CompilerParams(dimension_semantics=sem, vmem_limit_bytes=vmem)


def _lane_lt(shape, bound, period=None):
    lane = lax.broadcasted_iota(jnp.int32, shape, len(shape) - 1)
    if period is not None:
        lane = lane % period
    return lane < bound


def _rms(x, w):
    ms = jnp.mean(x * x, axis=-1, keepdims=True)
    return x * lax.rsqrt(ms + NORM_EPS) * w


def _dot(a, b):
    return jnp.dot(a, b, preferred_element_type=F32)


def _dot_nt(a, b):
    return lax.dot_general(a, b, (((1,), (1,)), ((), ())), preferred_element_type=F32)


def _dot_tn(a, b):
    return lax.dot_general(a, b, (((0,), (0,)), ((), ())), preferred_element_type=F32)


def _cast_kernel(x_ref, o_ref):
    o_ref[...] = x_ref[...].astype(o_ref.dtype)


def cast_bf16(w, tr):
    shape = w.shape
    w2 = w.reshape(-1, shape[-1])
    r, c = w2.shape
    out = pl.pallas_call(
        _cast_kernel,
        grid=(r // tr,),
        in_specs=[pl.BlockSpec((tr, c), lambda i: (i, 0))],
        out_specs=pl.BlockSpec((tr, c), lambda i: (i, 0)),
        out_shape=jax.ShapeDtypeStruct((r, c), BF16),
        compiler_params=_cparams(("parallel",)),
        name="cast_bf16",
    )(w2)
    return out.reshape(shape)


def _in_proj_kernel(x_ref, nw_ref, w_ref, wt_ref, o_ref, t_ref, h_ref):
    @pl.when(pl.program_id(1) == 0)
    def _():
        h = _rms(x_ref[...], nw_ref[...]).astype(BF16)
        h_ref[...] = h
        t_ref[...] = _dot(h, wt_ref[...])

    o_ref[...] = _dot(h_ref[...], w_ref[...]).astype(o_ref.dtype)


def in_proj(x, nw, w_main, w_tail, tm, tn):
    n, k = x.shape
    nout = w_main.shape[1]
    w_mode = pl.Buffered(1) if tn == nout else None
    return pl.pallas_call(
        _in_proj_kernel,
        grid=(n // tm, nout // tn),
        in_specs=[
            pl.BlockSpec((tm, k), lambda i, j: (i, 0)),
            pl.BlockSpec((1, k), lambda i, j: (0, 0)),
            pl.BlockSpec((k, tn), lambda i, j: (0, j), pipeline_mode=w_mode),
            pl.BlockSpec((k, LANES), lambda i, j: (0, 0)),
        ],
        out_specs=[pl.BlockSpec((tm, tn), lambda i, j: (i, j)), pl.BlockSpec((tm, LANES), lambda i, j: (i, 0))],
        out_shape=[jax.ShapeDtypeStruct((n, nout), BF16), jax.ShapeDtypeStruct((n, LANES), F32)],
        scratch_shapes=[pltpu.VMEM((tm, k), BF16)],
        compiler_params=_cparams(("parallel", "arbitrary")),
        name="in_proj",
    )(x, nw.reshape(1, k), w_main, w_tail)


def _out_proj_kernel(a0_ref, a1_ref, a2_ref, a3_ref, w_ref, r_ref, o_ref):
    acc = r_ref[...]
    for s, a_ref in enumerate((a0_ref, a1_ref, a2_ref, a3_ref)):
        acc = acc + _dot(a_ref[...], w_ref[s * SLAB:(s + 1) * SLAB, :])
    o_ref[...] = acc


def out_proj(mix, w, layer, res, tm, tn):
    n = res.shape[0]
    return pl.pallas_call(
        _out_proj_kernel,
        grid=(n // tm, D_MODEL // tn),
        in_specs=[pl.BlockSpec((tm, SLAB), lambda i, j: (i, 0))] * 4 + [
            pl.BlockSpec((None, 4 * SLAB, tn), lambda i, j: (layer, 0, j)),
            pl.BlockSpec((tm, tn), lambda i, j: (i, j)),
        ],
        out_specs=pl.BlockSpec((tm, tn), lambda i, j: (i, j)),
        out_shape=jax.ShapeDtypeStruct((n, D_MODEL), F32),
        compiler_params=_cparams(("parallel", "arbitrary")),
        name="out_proj",
    )(*mix, w, res)


def _ffn_kernel(x_ref, nw_ref, wg_ref, wu_ref, wd_ref, o_ref, h_ref):
    @pl.when(pl.program_id(1) == 0)
    def _():
        x = x_ref[...]
        h_ref[...] = _rms(x, nw_ref[...]).astype(BF16)
        o_ref[...] = x

    _swiglu_rows(h_ref, wg_ref, wu_ref, wd_ref, o_ref)


FFN_ROW_CHUNK = 512


def _swiglu_rows(h_ref, wg_ref, wu_ref, wd_ref, o_ref):
    tm = h_ref.shape[0]
    for r0 in range(0, tm, FFN_ROW_CHUNK):
        rows = slice(r0, r0 + FFN_ROW_CHUNK)
        h = h_ref[rows, :]
        g = _dot(h, wg_ref[...])
        u = _dot(h, wu_ref[...])
        a = (g * jax.nn.sigmoid(g) * u).astype(BF16)
        o_ref[rows, :] += _dot(a, wd_ref[...])


def ffn_dense(x, nw, wg, wu, wd, layer, tm, tf):
    n = x.shape[0]
    d_ff = wg.shape[-1]
    return pl.pallas_call(
        _ffn_kernel,
        grid=(n // tm, d_ff // tf),
        in_specs=[
            pl.BlockSpec((tm, D_MODEL), lambda i, j: (i, 0)),
            pl.BlockSpec((1, D_MODEL), lambda i, j: (0, 0)),
            pl.BlockSpec((None, D_MODEL, tf), lambda i, j: (layer, 0, j)),
            pl.BlockSpec((None, D_MODEL, tf), lambda i, j: (layer, 0, j)),
            pl.BlockSpec((None, tf, D_MODEL), lambda i, j: (layer, j, 0)),
        ],
        out_specs=pl.BlockSpec((tm, D_MODEL), lambda i, j: (i, 0)),
        out_shape=jax.ShapeDtypeStruct((n, D_MODEL), F32),
        scratch_shapes=[pltpu.VMEM((tm, D_MODEL), BF16)],
        compiler_params=_cparams(("parallel", "arbitrary")),
        name="ffn_dense",
    )(x, nw.reshape(1, D_MODEL), wg, wu, wd)


def _router_kernel(x_ref, nw_ref, rw_ref, idx_ref, gate_ref):
    h = _rms(x_ref[...], nw_ref[...])
    acc = jnp.zeros((h.shape[0], LANES), F32)
    rem = h
    for _ in range(SPLIT_PARTS):
        part = rem.astype(BF16)
        acc = acc + _dot(part, rw_ref[...])
        rem = rem - part.astype(F32)
    logits = acc
    for k in range(1, SPLIT_PARTS):
        logits = logits + pltpu.roll(acc, LANES - k * N_EXPERTS, 1)
    lane = lax.broadcasted_iota(jnp.int32, logits.shape, 1)
    logits = jnp.where(lane < N_EXPERTS, logits, -jnp.inf)
    m1 = jnp.max(logits, axis=-1, keepdims=True)
    i1 = jnp.min(jnp.where(logits == m1, lane, LANES), axis=-1, keepdims=True)
    rest = jnp.where(lane == i1, -jnp.inf, logits)
    m2 = jnp.max(rest, axis=-1, keepdims=True)
    i2 = jnp.min(jnp.where(rest == m2, lane, LANES), axis=-1, keepdims=True)
    e2 = jnp.exp(m2 - m1)
    g1 = 1.0 / (1.0 + e2)
    g2 = e2 / (1.0 + e2)
    idx_ref[...] = jnp.where(lane == 0, i1, i2)[:, :TOP_K]
    gate_ref[...] = jnp.where(lane == 0, g1, g2)[:, :TOP_K]


def moe_router(x, nw, router_w, tm):
    n = x.shape[0]
    parts, rem = [], router_w
    for _ in range(SPLIT_PARTS):
        parts.append(rem.astype(BF16))
        rem = rem - parts[-1].astype(F32)
    pad = jnp.zeros((D_MODEL, LANES - SPLIT_PARTS * N_EXPERTS), BF16)
    rw = jnp.concatenate(parts + [pad], axis=1)
    return pl.pallas_call(
        _router_kernel,
        grid=(n // tm,),
        in_specs=[
            pl.BlockSpec((tm, D_MODEL), lambda i: (i, 0)),
            pl.BlockSpec((1, D_MODEL), lambda i: (0, 0)),
            pl.BlockSpec((D_MODEL, LANES), lambda i: (0, 0)),
        ],
        out_specs=[pl.BlockSpec((tm, TOP_K), lambda i: (i, 0)),
                   pl.BlockSpec((tm, TOP_K), lambda i: (i, 0))],
        out_shape=[jax.ShapeDtypeStruct((n, TOP_K), jnp.int32),
                   jax.ShapeDtypeStruct((n, TOP_K), F32)],
        compiler_params=_cparams(("parallel",)),
        name="moe_router",
    )(x, nw.reshape(1, D_MODEL), rw)


def moe_plan(top_i, tm):
    n = top_i.shape[0]
    flat_e = top_i.reshape(-1)
    onehot = (flat_e[:, None] == jnp.arange(N_EXPERTS, dtype=jnp.int32)[None, :]).astype(jnp.int32)
    csum = jnp.cumsum(onehot, axis=0)
    counts = csum[-1]
    rank = jnp.sum(onehot * csum, axis=1) - 1
    padded = ((counts + tm - 1) // tm) * tm
    pend = jnp.cumsum(padded)
    pstart = pend - padded
    pos = pstart[flat_e] + rank
    n_slots = n * TOP_K + N_EXPERTS * tm
    n_tiles = n_slots // tm
    src = jnp.zeros((n_slots,), jnp.int32).at[pos].set(jnp.arange(n * TOP_K, dtype=jnp.int32) // TOP_K)
    tile_start = jnp.arange(n_tiles, dtype=jnp.int32) * tm
    tile_expert = jnp.sum((tile_start[:, None] >= pend[None, :]).astype(jnp.int32), axis=1)
    tile_valid = (tile_start < pend[-1]).astype(jnp.int32)
    last_valid = jnp.maximum(pend[-1] // tm - 1, 0)
    tile_expert = jnp.where(tile_valid == 1, tile_expert, tile_expert[last_valid]).astype(jnp.int32)
    return src, pos.reshape(n, TOP_K).astype(jnp.int32), tile_expert, tile_valid


def _moe_ffn_kernel(te_ref, tv_ref, src_ref, nsrc_ref, x_hbm, nw_ref, wg_ref, wu_ref, wd_ref, y_ref,
                    xbuf, h_ref, sem, *, tm, rows_per_step):
    i = pl.program_id(0)
    j = pl.program_id(1)
    issued = xbuf.shape[0]
    valid = tv_ref[i] == 1
    has_rows = jnp.logical_or(i == 0, tv_ref[jnp.maximum(i - 1, 0)] == 1)

    def start_row(idx_ref, row):
        tok = idx_ref[0, 0, jnp.minimum(row, tm - 1)]
        pltpu.make_async_copy(x_hbm.at[pl.ds(tok, 1)], xbuf.at[pl.ds(row, 1)], sem).start(priority=1)

    @pl.when(j == 0)
    def _():
        y_ref[...] = jnp.zeros_like(y_ref)

    @pl.when(jnp.logical_and(j == 0, i == 0))
    def _():
        def start(r, c):
            start_row(src_ref, r)
            return c

        lax.fori_loop(0, issued, start, 0)

    @pl.when(jnp.logical_and(j == 0, has_rows))
    def _():
        pltpu.make_async_copy(x_hbm.at[pl.ds(0, issued)], xbuf.at[pl.ds(0, issued)], sem).wait()
        h_ref[...] = _rms(xbuf[0:tm, :], nw_ref[...]).astype(BF16)

    @pl.when(valid)
    def _():
        for r in range(rows_per_step):
            start_row(nsrc_ref, j * rows_per_step + r)
        _swiglu_rows(h_ref, wg_ref, wu_ref, wd_ref, y_ref)


def moe_ffn(x, nw, wg, wu, wd, layer, src, tile_expert, tile_valid, tm, tf):
    n_slots = src.shape[0]
    n_tiles = n_slots // tm + 1
    d_ff = wg.shape[-1]
    nf = d_ff // tf
    sublanes = 8
    rows_per_step = -(-tm // (nf * sublanes)) * sublanes
    buf_rows = rows_per_step * nf
    tile_expert = jnp.concatenate([tile_expert, tile_expert[-1:]])
    tile_valid = jnp.concatenate([tile_valid, jnp.zeros((1,), tile_valid.dtype)])
    src3 = jnp.concatenate([src, jnp.zeros((tm,), src.dtype)]).reshape(n_tiles, 1, tm)

    def wcol(i, j, te_ref, tv_ref):
        return (layer, te_ref[i], 0, jnp.where(tv_ref[i] == 1, j, nf - 1))

    def wrow(i, j, te_ref, tv_ref):
        return (layer, te_ref[i], jnp.where(tv_ref[i] == 1, j, nf - 1), 0)

    grid_spec = pltpu.PrefetchScalarGridSpec(
        num_scalar_prefetch=2,
        grid=(n_tiles, nf),
        in_specs=[
            pl.BlockSpec((1, 1, tm), lambda i, j, *_: (i, 0, 0), memory_space=pltpu.SMEM),
            pl.BlockSpec((1, 1, tm), lambda i, j, *_: (jnp.minimum(i + 1, n_tiles - 1), 0, 0),
                         memory_space=pltpu.SMEM),
            pl.BlockSpec(memory_space=pl.ANY),
            pl.BlockSpec((1, D_MODEL), lambda i, j, *_: (0, 0)),
            pl.BlockSpec((None, None, D_MODEL, tf), wcol),
            pl.BlockSpec((None, None, D_MODEL, tf), wcol),
            pl.BlockSpec((None, None, tf, D_MODEL), wrow),
        ],
        out_specs=pl.BlockSpec((tm, D_MODEL), lambda i, j, *_: (i, 0)),
        scratch_shapes=[pltpu.VMEM((buf_rows, D_MODEL), F32), pltpu.VMEM((tm, D_MODEL), BF16),
                        pltpu.SemaphoreType.DMA],
    )
    return pl.pallas_call(
        functools.partial(_moe_ffn_kernel, tm=tm, rows_per_step=rows_per_step),
        grid_spec=grid_spec,
        out_shape=jax.ShapeDtypeStruct((n_tiles * tm, D_MODEL), F32),
        compiler_params=_cparams(("arbitrary", "arbitrary")),
        name="moe_ffn",
    )(tile_expert, tile_valid, src3, src3, x, nw.reshape(1, D_MODEL), wg, wu, wd)


def _moe_combine_kernel(pos_ref, npos_ref, x_ref, gate_ref, y_hbm, fw_ref, o_ref, ybuf, sem, *, tm, final_norm):
    i = pl.program_id(0)
    n_tiles = pl.num_programs(0)
    cur = i % 2

    def row_copy(idx_ref, buf, r, k):
        slot = idx_ref[0, 0, r * TOP_K + k]
        return pltpu.make_async_copy(y_hbm.at[pl.ds(slot, 1)], ybuf.at[buf, k, pl.ds(r, 1)], sem.at[buf])

    def gather(idx_ref, buf):
        def start(r, c):
            for k in range(TOP_K):
                row_copy(idx_ref, buf, r, k).start(priority=k % 2)
            return c

        lax.fori_loop(0, tm, start, 0, unroll=8)

    @pl.when(i == 0)
    def _():
        gather(pos_ref, 0)

    @pl.when(i + 1 < n_tiles)
    def _():
        gather(npos_ref, 1 - cur)

    for k in range(TOP_K):
        pltpu.make_async_copy(y_hbm.at[pl.ds(0, tm)], ybuf.at[cur, k], sem.at[cur]).wait()
    gates = gate_ref[...]
    out = x_ref[...]
    for k in range(TOP_K):
        out = out + gates[:, k:k + 1] * ybuf[cur, k]
    if final_norm:
        out = _rms(out, fw_ref[...])
    o_ref[...] = out


def moe_combine(x, gates, y, pos, final_w, tm):
    n = x.shape[0]
    final_norm = final_w is not None
    fw = final_w if final_norm else jnp.ones((D_MODEL,), F32)
    n_tiles = n // tm
    pos3 = pos.reshape(n_tiles, 1, tm * TOP_K)
    return pl.pallas_call(
        functools.partial(_moe_combine_kernel, tm=tm, final_norm=final_norm),
        grid=(n_tiles,),
        in_specs=[
            pl.BlockSpec((1, 1, tm * TOP_K), lambda i: (i, 0, 0), memory_space=pltpu.SMEM),
            pl.BlockSpec((1, 1, tm * TOP_K), lambda i: (jnp.minimum(i + 1, n_tiles - 1), 0, 0),
                         memory_space=pltpu.SMEM),
            pl.BlockSpec((tm, D_MODEL), lambda i: (i, 0)),
            pl.BlockSpec((tm, TOP_K), lambda i: (i, 0)),
            pl.BlockSpec(memory_space=pl.ANY),
            pl.BlockSpec((1, D_MODEL), lambda i: (0, 0)),
        ],
        out_specs=pl.BlockSpec((tm, D_MODEL), lambda i: (i, 0)),
        out_shape=jax.ShapeDtypeStruct((n, D_MODEL), F32),
        scratch_shapes=[pltpu.VMEM((2, TOP_K, tm, D_MODEL), F32), pltpu.SemaphoreType.DMA((2,))],
        compiler_params=_cparams(("arbitrary",)),
        name="moe_combine",
    )(pos3, pos3, x, gates, y, fw.reshape(1, D_MODEL))


def _rmsnorm_kernel(x_ref, w_ref, o_ref):
    o_ref[...] = _rms(x_ref[...], w_ref[...])


def rmsnorm_rows(x, w, tm):
    n = x.shape[0]
    return pl.pallas_call(
        _rmsnorm_kernel,
        grid=(n // tm,),
        in_specs=[pl.BlockSpec((tm, D_MODEL), lambda i: (i, 0)),
                  pl.BlockSpec((1, D_MODEL), lambda i: (0, 0))],
        out_specs=pl.BlockSpec((tm, D_MODEL), lambda i: (i, 0)),
        out_shape=jax.ShapeDtypeStruct((n, D_MODEL), F32),
        compiler_params=_cparams(("parallel",)),
        name="final_norm",
    )(x, w.reshape(1, D_MODEL))


CONV_PAD = 8


def _conv_kernel(x_ref, w_ref, b_ref, o_ref, pad_ref, *, t):
    half = SSM_CONV // 2
    zeros = jnp.zeros((CONV_PAD, pad_ref.shape[1]), F32)
    pad_ref[0:CONV_PAD, :] = zeros
    pad_ref[CONV_PAD + t:CONV_PAD + t + CONV_PAD, :] = zeros
    pad_ref[CONV_PAD:CONV_PAD + t, :] = x_ref[...].astype(F32)
    acc = jnp.zeros(o_ref.shape, F32) + b_ref[...]
    for k in range(SSM_CONV):
        acc = acc + pad_ref[pl.ds(CONV_PAD - half + k, t), :] * w_ref[k:k + 1, :]
    o_ref[...] = (acc * jax.nn.sigmoid(acc)).astype(o_ref.dtype)


def conv_silu(proj, conv_w, conv_b, b, t):
    tc = 256
    nblk = SSM_CONV_CH // tc
    col0 = COL_XBC * SLAB // tc
    return pl.pallas_call(
        functools.partial(_conv_kernel, t=t),
        grid=(b, nblk),
        in_specs=[
            pl.BlockSpec((t, tc), lambda i, j: (i, col0 + j)),
            pl.BlockSpec((SSM_CONV, tc), lambda i, j: (0, j)),
            pl.BlockSpec((1, tc), lambda i, j: (0, j)),
        ],
        out_specs=pl.BlockSpec((t, tc), lambda i, j: (i, j)),
        out_shape=jax.ShapeDtypeStruct((b * t, SSM_CONV_CH), BF16),
        scratch_shapes=[pltpu.VMEM((t + 2 * CONV_PAD, tc), F32)],
        compiler_params=_cparams(("parallel", "parallel")),
        name="conv_silu",
    )(proj, conv_w, conv_b.reshape(1, SSM_CONV_CH))


def _ssd_stage1(xbc_ref, tail_ref, bias_ref, alog_ref, tri_ref, sel_ref, state_ref, direction):
    q = SSM_CHUNK
    dt = jax.nn.softplus(tail_ref[...] + bias_ref[...])
    da = dt * (-jnp.exp(alog_ref[...]))
    cs = jnp.dot(tri_ref[...], da, preferred_element_type=F32, precision=lax.Precision.HIGHEST)
    total = cs[q - 1:q, :]
    if direction == 0:
        e_out = cs
        e_in = total - cs
        e_seg = cs
    else:
        ex = cs - da
        e_out = total - ex
        e_in = ex
        e_seg = -ex
    dec_out_b = jnp.exp(e_out).astype(BF16)
    dec_in_dt_b = (jnp.exp(e_in) * dt).astype(BF16)
    pairs_per_group = N_PAIRS // SSM_GROUPS
    cbs, y_offs, in_scales = [], [], []
    for g in range(SSM_GROUPS):
        bm = xbc_ref[:, SSM_INNER + g * SSM_STATE:SSM_INNER + (g + 1) * SSM_STATE]
        cm = xbc_ref[:, SSM_INNER + (SSM_GROUPS + g) * SSM_STATE:SSM_INNER + (SSM_GROUPS + g + 1) * SSM_STATE]
        cbs.append(_dot_nt(cm, bm))
        for pair in range(g * pairs_per_group, (g + 1) * pairs_per_group):
            sel = sel_ref[direction, pair]
            y_offs.append(_dot(dec_out_b, sel) * _dot(cm, state_ref[pair].astype(BF16)))
            in_scales.append(_dot(dec_in_dt_b, sel))
    return dict(e_seg=e_seg, e_seg_t=jnp.transpose(e_seg), dt_t=jnp.transpose(dt), dec_tot=jnp.exp(total),
                cbs=cbs, y_offs=y_offs, in_scales=in_scales)


def _ssd_stage2(ctx, xbc_ref, y_ref, state_ref, direction):
    q = SSM_CHUNK
    row = lax.broadcasted_iota(jnp.int32, (q, q), 0)
    col = lax.broadcasted_iota(jnp.int32, (q, q), 1)
    keep = (row >= col) if direction == 0 else (col >= row)
    first_half = _lane_lt((q, HEAD_PAIR), HEAD_DIM)
    pairs_per_group = N_PAIRS // SSM_GROUPS
    for pair in range(N_PAIRS):
        g = pair // pairs_per_group
        bm = xbc_ref[:, SSM_INNER + g * SSM_STATE:SSM_INNER + (g + 1) * SSM_STATE]
        xs_pair = xbc_ref[:, pair * HEAD_PAIR:(pair + 1) * HEAD_PAIR]
        ys, decs = [], []
        for hh in range(2):
            lane = TAIL_DT + direction * SSM_HEADS + pair * 2 + hh
            seg = ctx["e_seg"][:, lane:lane + 1] - ctx["e_seg_t"][lane:lane + 1, :]
            lmat = jnp.where(keep, jnp.exp(seg), 0.0)
            w = (ctx["cbs"][g] * lmat * ctx["dt_t"][lane:lane + 1, :]).astype(BF16)
            ys.append(_dot(w, xs_pair))
            decs.append(ctx["dec_tot"][:, lane:lane + 1])
        y_ref[:, pair * HEAD_PAIR:(pair + 1) * HEAD_PAIR] = jnp.where(first_half, ys[0], ys[1]) + ctx["y_offs"][pair]
        st = state_ref[pair]
        first_half_s = _lane_lt(st.shape, HEAD_DIM)
        state_ref[pair] = (st * jnp.where(first_half_s, decs[0], decs[1])
                           + _dot_tn(bm, (xs_pair * ctx["in_scales"][pair]).astype(BF16)))


def _ssd_kernel(xbc_f_ref, tail_f_ref, xbc_b_ref, tail_b_ref, bias_ref, alog_ref, tri_ref, sel_ref,
                y_f_ref, y_b_ref, state_ref):
    @pl.when(pl.program_id(1) == 0)
    def _():
        state_ref[...] = jnp.zeros_like(state_ref)

    q = SSM_CHUNK
    for k in range(SSD_CHUNKS_PER_STEP):
        ins = []
        for d, refs in enumerate(((xbc_f_ref, tail_f_ref, y_f_ref), (xbc_b_ref, tail_b_ref, y_b_ref))):
            sub = k if d == 0 else SSD_CHUNKS_PER_STEP - 1 - k
            ins.append(tuple(r.at[sub * q:(sub + 1) * q] for r in refs))
        ctxs = [_ssd_stage1(xbc_ref, tail_ref, bias_ref, alog_ref, tri_ref, sel_ref, state_ref.at[d], d)
                for d, (xbc_ref, tail_ref, _) in enumerate(ins)]
        for d, (xbc_ref, _, y_ref) in enumerate(ins):
            _ssd_stage2(ctxs[d], xbc_ref, y_ref, state_ref.at[d], d)


def _ssd_lane_selectors():
    sel = np.zeros((2, N_PAIRS, LANES, HEAD_PAIR), np.float32)
    for d in range(2):
        for pair in range(N_PAIRS):
            for hh in range(2):
                sel[d, pair, TAIL_DT + d * SSM_HEADS + pair * 2 + hh, hh * HEAD_DIM:(hh + 1) * HEAD_DIM] = 1.0
    return jnp.asarray(sel, BF16)


def ssd_scan(xbc, tail, dt_bias, a_log, b, t):
    q = SSM_CHUNK
    rows = q * SSD_CHUNKS_PER_STEP
    nc = t // rows
    bias_row = jnp.zeros((1, LANES), F32).at[0, TAIL_DT:TAIL_DT + 2 * SSM_HEADS].set(dt_bias.reshape(-1))
    alog_row = jnp.zeros((1, LANES), F32).at[0, TAIL_DT:TAIL_DT + 2 * SSM_HEADS].set(a_log.reshape(-1))
    tri = jnp.asarray(np.tril(np.ones((q, q), np.float32)))
    fwd = lambda i, c: (i * nc + c, 0)
    bwd = lambda i, c: (i * nc + nc - 1 - c, 0)
    const = lambda i, c: (0, 0)
    return pl.pallas_call(
        _ssd_kernel,
        grid=(b, nc),
        in_specs=[
            pl.BlockSpec((rows, SSM_CONV_CH), fwd), pl.BlockSpec((rows, LANES), fwd),
            pl.BlockSpec((rows, SSM_CONV_CH), bwd), pl.BlockSpec((rows, LANES), bwd),
            pl.BlockSpec((1, LANES), const), pl.BlockSpec((1, LANES), const), pl.BlockSpec((q, q), const),
            pl.BlockSpec((2, N_PAIRS, LANES, HEAD_PAIR), lambda i, c: (0, 0, 0, 0)),
        ],
        out_specs=[pl.BlockSpec((rows, SSM_INNER), fwd), pl.BlockSpec((rows, SSM_INNER), bwd)],
        out_shape=[jax.ShapeDtypeStruct((b * t, SSM_INNER), F32)] * 2,
        scratch_shapes=[pltpu.VMEM((2, N_PAIRS, SSM_STATE, HEAD_PAIR), F32)],
        compiler_params=_cparams(("parallel", "arbitrary")),
        name="ssd_scan",
    )(xbc, tail, xbc, tail, bias_row, alog_row, tri, _ssd_lane_selectors())


def _ssd_combine_kernel(yf_ref, yb_ref, xs_ref, z_ref, d_ref, nw_ref, o_ref):
    y = yf_ref[...] + yb_ref[...] + xs_ref[...].astype(F32) * d_ref[...]
    z = z_ref[...].astype(F32)
    o_ref[...] = _rms(y * (z * jax.nn.sigmoid(z)), nw_ref[...]).astype(o_ref.dtype)


def ssd_combine(y_f, y_b, xbc, proj, d_skip, norm_w, tm):
    n = y_f.shape[0]
    d_row = jnp.repeat(d_skip, SSM_HEAD_DIM).reshape(1, SSM_INNER)
    row = lambda i: (i, 0)
    return pl.pallas_call(
        _ssd_combine_kernel,
        grid=(n // tm,),
        in_specs=[
            pl.BlockSpec((tm, SSM_INNER), row),
            pl.BlockSpec((tm, SSM_INNER), row),
            pl.BlockSpec((tm, SSM_INNER), row),
            pl.BlockSpec((tm, SLAB), lambda i: (i, COL_Z)),
            pl.BlockSpec((1, SSM_INNER), lambda i: (0, 0)),
            pl.BlockSpec((1, SSM_INNER), lambda i: (0, 0)),
        ],
        out_specs=pl.BlockSpec((tm, SSM_INNER), row),
        out_shape=jax.ShapeDtypeStruct((n, SSM_INNER), BF16),
        compiler_params=_cparams(("parallel",)),
        name="ssd_combine",
    )(y_f, y_b, xbc, proj, d_row, norm_w.reshape(1, SSM_INNER))


def _pair_scores(q2, k2):
    first_q = _lane_lt(q2.shape, HEAD_DIM)
    zero = jnp.zeros_like(q2)
    qs = jnp.concatenate([jnp.where(first_q, q2, zero), jnp.where(first_q, zero, q2)], axis=0)
    return _dot_nt(qs, k2)


def _pair_attend(s, v2):
    tq = s.shape[0] // 2
    m = jnp.max(s, axis=-1, keepdims=True)
    p = jnp.exp(s - m)
    l = jnp.sum(p, axis=-1, keepdims=True)
    o = _dot(p.astype(BF16), v2) / l
    lse = m + jnp.log(l)
    first_o = _lane_lt((tq, HEAD_PAIR), HEAD_DIM)
    return jnp.where(first_o, o[:tq], o[tq:]), jnp.where(first_o, lse[:tq], lse[tq:])


def na_bias_tables(rpb, rows):
    kr = min(NA_WIN_ROWS, rows)
    qc = np.arange(GRID_W)
    kc = np.arange(GRID_W)
    q_start = np.clip(qc - NA_WIN_COLS // 2, 0, GRID_W - NA_WIN_COLS)
    col_in = (kc[None, :] >= q_start[:, None]) & (kc[None, :] < q_start[:, None] + NA_WIN_COLS)
    col_off = np.clip(kc[None, :] - qc[:, None] + NA_WIN_COLS - 1, 0, 2 * NA_WIN_COLS - 2)
    onehot = (col_off[None] == np.arange(2 * NA_WIN_COLS - 1)[:, None, None]).astype(np.float32)
    expanded = jnp.einsum("hrc,cqk->hqrk", rpb, jnp.asarray(onehot), precision=lax.Precision.HIGHEST)
    expanded = jnp.where(jnp.asarray(col_in)[None, :, None, :], expanded, NEG_INF)

    def table(r):
        row_start = int(np.clip(r - kr // 2, 0, rows - kr))
        ro0 = row_start - r + NA_WIN_ROWS - 1
        return expanded[:, :, ro0:ro0 + kr, :].reshape(N_PAIRS, 2 * GRID_W, kr * GRID_W)

    rs = NA_ROWS_PER_STEP
    lo = [table(r) for r in range(rs)]
    mid = [table(min(rs, rows - 1))] * rs
    hi = [table(r) for r in range(rows - rs, rows)]
    return jnp.stack([jnp.stack(lo), jnp.stack(mid), jnp.stack(hi)])


def _na_kernel(q_ref, k_ref, v_ref, bias_ref, o_ref, *, rows, kr):
    step = pl.program_id(1)
    rs = NA_ROWS_PER_STEP
    for rr in range(rs):
        r = step * rs + rr
        row_start = jnp.clip(r - kr // 2, 0, rows - kr)
        k0 = pl.multiple_of(row_start * GRID_W, GRID_W)
        pair_cols = [slice(pair * HEAD_PAIR, (pair + 1) * HEAD_PAIR) for pair in range(N_PAIRS)]
        scores = []
        for pair, cols in enumerate(pair_cols):
            q2 = q_ref[rr * GRID_W:(rr + 1) * GRID_W, cols] * jnp.asarray(HEAD_DIM ** -0.5, BF16)
            scores.append(_pair_scores(q2, k_ref[pl.ds(k0, kr * GRID_W), cols]) + bias_ref[0, rr, pair])
        for s, cols in zip(scores, pair_cols):
            o, _ = _pair_attend(s, v_ref[pl.ds(k0, kr * GRID_W), cols])
            o_ref[rr * GRID_W:(rr + 1) * GRID_W, cols] = o.astype(o_ref.dtype)


def na_attention(proj, rpb, b, t):
    rows = t // GRID_W
    kr = min(NA_WIN_ROWS, rows)
    rs = NA_ROWS_PER_STEP
    nsteps = rows // rs
    bias = na_bias_tables(rpb, rows)

    def kind(i, s):
        return jnp.where(s == 0, 0, jnp.where(s == nsteps - 1, 2, 1))

    return pl.pallas_call(
        functools.partial(_na_kernel, rows=rows, kr=kr),
        grid=(b, nsteps),
        in_specs=[
            pl.BlockSpec((rs * GRID_W, SLAB), lambda i, s: (i * nsteps + s, COL_NAQ)),
            pl.BlockSpec((t, SLAB), lambda i, s: (i, COL_NAK)),
            pl.BlockSpec((t, SLAB), lambda i, s: (i, COL_NAV)),
            pl.BlockSpec((1, rs, N_PAIRS, 2 * GRID_W, kr * GRID_W), lambda i, s: (kind(i, s), 0, 0, 0, 0)),
        ],
        out_specs=pl.BlockSpec((rs * GRID_W, SLAB), lambda i, s: (i * nsteps + s, 0)),
        out_shape=jax.ShapeDtypeStruct((b * t, SLAB), BF16),
        compiler_params=_cparams(("parallel", "arbitrary")),
        name="na_attention",
    )(proj, proj, proj, bias)


def _rope_angles(t, d):
    inv = ROPE_THETA ** (-np.arange(0, d, 2, dtype=np.float32) / d)
    return np.arange(t, dtype=np.float32)[:, None] * inv[None, :]


def rope_tables_pair(t):
    ang = _rope_angles(t, HEAD_DIM)
    cos = np.tile(np.cos(ang), (1, 4))
    sin = np.tile(np.concatenate([-np.sin(ang), np.sin(ang)], axis=1), (1, 2))
    return jnp.asarray(cos, F32), jnp.asarray(sin, F32)


FOLD_CHUNK = 256
FOLD_DILS = tuple(d for _, d in DIL_PAIRS if d > 1)


def fold_permutation(dil):
    per = FOLD_CHUNK // dil
    perm = np.zeros((FOLD_CHUNK, FOLD_CHUNK), np.float32)
    dst = np.arange(FOLD_CHUNK)
    perm[dst, (dst % per) * dil + dst // per] = 1.0
    return jnp.asarray(perm, BF16)


def _rope_qkv_kernel(x_ref, v_ref, cos_ref, sin_ref, *rest):
    nd = len(FOLD_DILS)
    perm_refs, o_ref, fold_refs = rest[:nd], rest[nd], rest[nd + 1:]
    cos = cos_ref[...]
    sin = sin_ref[...]
    half = HEAD_DIM // 2
    for c in range(x_ref.shape[1] // LANES):
        x = x_ref[:, c * LANES:(c + 1) * LANES].astype(F32)
        rot = jnp.where(_lane_lt(x.shape, half, HEAD_DIM),
                        pltpu.roll(x, LANES - half, 1), pltpu.roll(x, half, 1))
        y = x * cos + rot * sin
        if c < N_PAIRS:
            y = y * (HEAD_DIM ** -0.5)
        o_ref[0, :, c * LANES:(c + 1) * LANES] = y.astype(o_ref.dtype)
    o_ref[0, :, 2 * SLAB:3 * SLAB] = v_ref[...]
    tm = x_ref.shape[0]
    for dil, perm_ref, f_ref in zip(FOLD_DILS, perm_refs, fold_refs):
        per = FOLD_CHUNK // dil
        for c in range(tm // FOLD_CHUNK):
            folded = _dot(perm_ref[...], o_ref[0, c * FOLD_CHUNK:(c + 1) * FOLD_CHUNK, :]).astype(f_ref.dtype)
            for p in range(dil):
                f_ref[p, c * per:(c + 1) * per, :] = folded[p * per:(p + 1) * per, :]


def rope_qkv(proj, b, t, tm):
    n = b * t
    cos, sin = rope_tables_pair(t)
    nb = t // tm
    fold_spec = lambda d: pl.BlockSpec((None, d, tm // d, 3 * SLAB), lambda i: (i // nb, 0, i % nb, 0))
    outs = pl.pallas_call(
        _rope_qkv_kernel,
        grid=(n // tm,),
        in_specs=[
            pl.BlockSpec((tm, 2 * SLAB), lambda i: (i, COL_DLQ // 2)),
            pl.BlockSpec((tm, SLAB), lambda i: (i, COL_DLV)),
            pl.BlockSpec((tm, LANES), lambda i: (i % nb, 0)),
            pl.BlockSpec((tm, LANES), lambda i: (i % nb, 0)),
        ] + [pl.BlockSpec((FOLD_CHUNK, FOLD_CHUNK), lambda i: (0, 0))] * len(FOLD_DILS),
        out_specs=[fold_spec(1)] + [fold_spec(d) for d in FOLD_DILS],
        out_shape=[jax.ShapeDtypeStruct((b, d, t // d, 3 * SLAB), BF16) for d in (1,) + FOLD_DILS],
        compiler_params=_cparams(("parallel",)),
        name="rope_qkv",
    )(proj, proj, cos, sin, *[fold_permutation(d) for d in FOLD_DILS])
    by_dil = dict(zip((1,) + FOLD_DILS, outs))
    return [by_dil[d] for _, d in DIL_PAIRS]


def _band_kernel(q_ref, k_ref, v_ref, o_ref, lse_ref, *, sub, half, span):
    tq = DIL_QBLOCK
    blocks = q_ref.shape[0] // tq
    for blk in range(blocks):
        qb = pl.program_id(2) * blocks + blk
        rows = slice(blk * tq, (blk + 1) * tq)
        start = jnp.clip(qb * tq - half, 0, sub - span)
        start = pl.multiple_of(start, half)
        q_pos = qb * tq + lax.broadcasted_iota(jnp.int32, (2 * tq, span), 0) % tq
        k_pos = start + lax.broadcasted_iota(jnp.int32, (2 * tq, span), 1)
        valid = jnp.abs(k_pos - q_pos) <= half
        pair_cols = [slice(pair * HEAD_PAIR, (pair + 1) * HEAD_PAIR) for pair in range(N_PAIRS)]
        scores = [jnp.where(valid, _pair_scores(q_ref[rows, cols], k_ref[pl.ds(start, span), cols]), NEG_INF)
                  for cols in pair_cols]
        for s, cols in zip(scores, pair_cols):
            o, lse = _pair_attend(s, v_ref[pl.ds(start, span), cols])
            o_ref[rows, cols] = o
            lse_ref[rows, cols] = lse


def band_attention(qkv, window, dil):
    b, _, sub, _ = qkv.shape
    half = window // (2 * dil)
    span = DIL_QBLOCK + 2 * half
    tq = DIL_QBLOCK * min(BAND_BLOCKS_PER_STEP, sub // DIL_QBLOCK)
    nqb = sub // tq
    return pl.pallas_call(
        functools.partial(_band_kernel, sub=sub, half=half, span=span),
        grid=(b, dil, nqb),
        in_specs=[
            pl.BlockSpec((None, None, tq, SLAB), lambda i, p, s: (i, p, s, 0)),
            pl.BlockSpec((None, None, sub, SLAB), lambda i, p, s: (i, p, 0, 1)),
            pl.BlockSpec((None, None, sub, SLAB), lambda i, p, s: (i, p, 0, 2)),
        ],
        out_specs=[pl.BlockSpec((None, None, tq, SLAB), lambda i, p, s: (i, p, s, 0))] * 2,
        out_shape=[jax.ShapeDtypeStruct((b, dil, sub, SLAB), F32)] * 2,
        compiler_params=_cparams(("parallel", "parallel", "arbitrary")),
        name="band_attention_d%d" % dil,
    )(qkv, qkv, qkv)


def _dil_combine_kernel(*refs):
    nbr = len(DIL_PAIRS)
    o_refs, l_refs, out_ref = refs[:nbr], refs[nbr:2 * nbr], refs[2 * nbr]
    scratch = iter(refs[2 * nbr + 1:])

    def token_order(ref):
        dil = ref.shape[0]
        if dil == 1:
            return ref[0]
        buf = next(scratch)
        per = ref.shape[1]
        for p in range(dil):
            for c in range(SLAB // LANES):
                buf[c, pl.ds(p, per, stride=dil), :] = ref[p, :, c * LANES:(c + 1) * LANES]
        return jnp.concatenate([buf[c] for c in range(SLAB // LANES)], axis=1)

    os = [token_order(r) for r in o_refs]
    lses = [token_order(r) for r in l_refs]
    m = functools.reduce(jnp.maximum, lses)
    ws = [jnp.exp(l - m) for l in lses]
    den = functools.reduce(jnp.add, ws)
    acc = functools.reduce(jnp.add, [(w / den) * o for w, o in zip(ws, os)])
    out_ref[...] = acc.astype(out_ref.dtype)


def dil_combine(outs, lses, tm):
    b, _, t, _ = outs[0].shape
    n = b * t
    nb = t // tm
    spec = lambda a: pl.BlockSpec((None, a.shape[1], tm // a.shape[1], SLAB), lambda i: (i // nb, 0, i % nb, 0))
    n_folded = sum(1 for a in outs + lses if a.shape[1] > 1)
    return pl.pallas_call(
        _dil_combine_kernel,
        grid=(n // tm,),
        in_specs=[spec(a) for a in outs + lses],
        out_specs=pl.BlockSpec((tm, SLAB), lambda i: (i, 0)),
        out_shape=jax.ShapeDtypeStruct((n, SLAB), BF16),
        scratch_shapes=[pltpu.VMEM((SLAB // LANES, tm, LANES), F32)] * n_folded,
        compiler_params=_cparams(("parallel",)),
        name="dil_combine",
    )(*outs, *lses)


MLA_QK = MLA_NOPE + MLA_ROPE


def mla_tables(t):
    ang = _rope_angles(t, MLA_ROPE)
    cos2 = np.concatenate([np.cos(ang), np.cos(ang)], axis=1)
    sin2 = np.concatenate([np.sin(ang), np.sin(ang)], axis=1)
    z = lambda w: np.zeros((t, w), np.float32)
    q_cos = np.concatenate([np.ones((t, MLA_NOPE), np.float32), cos2, z(LANES - MLA_QK)], axis=1)
    q_sin = np.concatenate([z(MLA_NOPE), sin2, z(LANES - MLA_QK)], axis=1)
    k_cos = np.concatenate([cos2, z(LANES - MLA_ROPE)], axis=1)
    k_sin = np.concatenate([-sin2[:, :MLA_ROPE // 2], sin2[:, MLA_ROPE // 2:], z(LANES - MLA_ROPE)], axis=1)
    return tuple(jnp.asarray(a, F32) for a in (q_cos, q_sin, k_cos, k_sin))


def mla_weights(w_uq, w_ukv):
    hq = w_uq.reshape(MLA_Q_RANK, MLA_HEADS, MLA_QK)
    nope, pe = hq[..., :MLA_NOPE], hq[..., MLA_NOPE:]
    pe_rot = jnp.concatenate([-pe[..., MLA_ROPE // 2:], pe[..., :MLA_ROPE // 2]], axis=-1)
    zq = jnp.zeros((MLA_Q_RANK, MLA_HEADS, LANES - MLA_QK), w_uq.dtype)
    w1 = jnp.concatenate([nope, pe, zq], axis=-1).reshape(MLA_Q_RANK, MLA_HEADS * LANES)
    w2 = jnp.concatenate([jnp.zeros_like(nope), pe_rot, zq], axis=-1).reshape(MLA_Q_RANK, MLA_HEADS * LANES)
    hkv = w_ukv.reshape(MLA_KV_RANK, MLA_HEADS, MLA_NOPE + MLA_V)
    k_nope, v = hkv[..., :MLA_NOPE], hkv[..., MLA_NOPE:]
    zk = jnp.zeros((MLA_KV_RANK, MLA_HEADS, LANES - MLA_NOPE), w_ukv.dtype)
    wk = jnp.concatenate([k_nope, zk], axis=-1).reshape(MLA_KV_RANK, MLA_HEADS * LANES)
    zv = jnp.zeros((MLA_KV_RANK, MLA_HEADS, LANES - MLA_V), w_ukv.dtype)
    wv = jnp.concatenate([v, zv], axis=-1).reshape(MLA_KV_RANK, MLA_HEADS * LANES)
    place = np.zeros((LANES, MLA_HEADS * LANES), np.float32)
    ones = np.zeros((1, MLA_HEADS * LANES), np.float32)
    for h in range(MLA_HEADS):
        place[np.arange(MLA_ROPE), h * LANES + MLA_NOPE + np.arange(MLA_ROPE)] = 1.0
        ones[0, h * LANES + MLA_V] = 1.0
    return (w1.astype(BF16), w2.astype(BF16), wk.astype(BF16), wv.astype(BF16), jnp.asarray(place, BF16),
            jnp.asarray(ones, F32))


def _mla_q_kernel(c_ref, nw_ref, w1_ref, w2_ref, cos_ref, sin_ref, o_ref):
    cn = _rms(c_ref[...].astype(F32), nw_ref[...]).astype(BF16)
    cos = jnp.tile(cos_ref[...], (1, MLA_HEADS))
    sin = jnp.tile(sin_ref[...], (1, MLA_HEADS))
    q = _dot(cn, w1_ref[...]) * cos + _dot(cn, w2_ref[...]) * sin
    o_ref[...] = (q * (MLA_QK ** -0.5 * math.log2(math.e))).astype(o_ref.dtype)


def _mla_kv_kernel(c_ref, tail_ref, nw_ref, wk_ref, wv_ref, place_ref, ones_ref, cos_ref, sin_ref,
                   k_ref, v_ref):
    cn = _rms(c_ref[...].astype(F32), nw_ref[...]).astype(BF16)
    kr = tail_ref[...]
    half = MLA_ROPE // 2
    rot = jnp.where(_lane_lt(kr.shape, half), pltpu.roll(kr, LANES - half, 1), pltpu.roll(kr, half, 1))
    k_pe = (kr * cos_ref[...] + rot * sin_ref[...]).astype(BF16)
    k_ref[...] = (_dot(cn, wk_ref[...]) + _dot(k_pe, place_ref[...])).astype(k_ref.dtype)
    v_ref[...] = (_dot(cn, wv_ref[...]) + ones_ref[...]).astype(v_ref.dtype)


def mla_project(proj, tail, q_norm_w, kv_norm_w, w_uq, w_ukv, b, t, tm):
    n = b * t
    nb = t // tm
    w1, w2, wk, wv, place, ones = mla_weights(w_uq, w_ukv)
    q_cos, q_sin, k_cos, k_sin = mla_tables(t)
    wide = MLA_HEADS * LANES
    full = lambda shape: pl.BlockSpec(shape, lambda i: (0, 0))
    tab = pl.BlockSpec((tm, LANES), lambda i: (i % nb, 0))
    qf = pl.pallas_call(
        _mla_q_kernel,
        grid=(n // tm,),
        in_specs=[pl.BlockSpec((tm, SLAB), lambda i: (i, COL_CQ)), full((1, MLA_Q_RANK)),
                  full((MLA_Q_RANK, wide)), full((MLA_Q_RANK, wide)), tab, tab],
        out_specs=pl.BlockSpec((tm, wide), lambda i: (i, 0)),
        out_shape=jax.ShapeDtypeStruct((n, wide), BF16),
        compiler_params=_cparams(("parallel",)),
        name="mla_q_proj",
    )(proj, q_norm_w.reshape(1, MLA_Q_RANK), w1, w2, q_cos, q_sin)
    kf, vf = pl.pallas_call(
        _mla_kv_kernel,
        grid=(n // tm,),
        in_specs=[pl.BlockSpec((tm, SLAB), lambda i: (i, COL_CKV)),
                  pl.BlockSpec((tm, LANES), lambda i: (i, 0)), full((1, MLA_KV_RANK)),
                  full((MLA_KV_RANK, wide)), full((MLA_KV_RANK, wide)), full((LANES, wide)), full((1, wide)),
                  tab, tab],
        out_specs=[pl.BlockSpec((tm, wide), lambda i: (i, 0))] * 2,
        out_shape=[jax.ShapeDtypeStruct((n, wide), BF16)] * 2,
        compiler_params=_cparams(("parallel",)),
        name="mla_kv_proj",
    )(proj, tail, kv_norm_w.reshape(1, MLA_KV_RANK), wk, wv, place, ones, k_cos, k_sin)
    return qf, kf, vf


def _mla_attn_kernel(q_ref, k_ref, v_ref, *rest, t, tk, n_cast):
    cast_in, o_ref, cast_out = rest[:n_cast], rest[n_cast], rest[n_cast + 1:]
    for src_ref, dst_ref in zip(cast_in, cast_out):
        dst_ref[...] = src_ref[...].astype(dst_ref.dtype)
    tq = q_ref.shape[0]
    groups = [slice(hh * LANES, (hh + 1) * LANES) for hh in range(2)]
    qs = [q_ref[:, grp] for grp in groups]

    def scores(c):
        return [_dot_nt(q, k_ref[c * tk:(c + 1) * tk, grp]) for q, grp in zip(qs, groups)]

    n_chunks = t // tk
    ms = [jnp.full((tq, 1), -jnp.inf, F32)] * 2
    accs = [jnp.zeros((tq, LANES), F32)] * 2
    s_next = scores(0)
    for c in range(n_chunks):
        s_cur = s_next
        if c + 1 < n_chunks:
            s_next = scores(c + 1)
        for hh, grp in enumerate(groups):
            m_new = jnp.maximum(ms[hh], jnp.max(s_cur[hh], axis=-1, keepdims=True))
            p = jnp.exp2((s_cur[hh] - m_new).astype(BF16))
            accs[hh] = jnp.exp2(ms[hh] - m_new) * accs[hh] + _dot(p, v_ref[c * tk:(c + 1) * tk, grp])
            ms[hh] = m_new
    outs = [acc / acc[:, MLA_V:MLA_V + 1] for acc in accs]
    first = _lane_lt((tq, LANES), MLA_V)
    o_ref[...] = jnp.where(first, outs[0], pltpu.roll(outs[1], MLA_V, 1)).astype(o_ref.dtype)


def mla_cast_rows(w, b, t, tq):
    steps = b * N_PAIRS * (t // tq)
    rows = int(np.prod(w.shape[:-1]))
    per = rows // steps
    return per if rows % steps == 0 and per % 16 == 0 else None


def mla_attention(qf, kf, vf, b, t, tq, tk, cast=()):
    n = b * t
    nq = t // tq
    step = lambda i, p, s: ((i * N_PAIRS + p) * nq + s, 0)
    cast2d = [w.reshape(-1, w.shape[-1]) for w in cast]
    cast_specs = [pl.BlockSpec((mla_cast_rows(w, b, t, tq), w2.shape[1]), step) for w, w2 in zip(cast, cast2d)]
    outs = pl.pallas_call(
        functools.partial(_mla_attn_kernel, t=t, tk=tk, n_cast=len(cast)),
        grid=(b, N_PAIRS, nq),
        in_specs=[
            pl.BlockSpec((tq, 2 * LANES), lambda i, p, s: (i * nq + s, p)),
            pl.BlockSpec((t, 2 * LANES), lambda i, p, s: (i, p)),
            pl.BlockSpec((t, 2 * LANES), lambda i, p, s: (i, p)),
        ] + cast_specs,
        out_specs=[pl.BlockSpec((tq, HEAD_PAIR), lambda i, p, s: (i * nq + s, p))] + cast_specs,
        out_shape=[jax.ShapeDtypeStruct((n, SLAB), BF16)]
        + [jax.ShapeDtypeStruct(w2.shape, BF16) for w2 in cast2d],
        compiler_params=_cparams(("parallel", "parallel", "arbitrary")),
        name="mla_attention",
    )(qf, kf, vf, *cast2d)
    return outs[0], [o.reshape(w.shape) for o, w in zip(outs[1:], cast)]


def _in_proj_segments():
    sizes = (SSM_INNER, SSM_CONV_CH, 2 * SSM_HEADS, SLAB, SLAB, SLAB, MLA_Q_RANK, MLA_KV_RANK, MLA_ROPE,
             SLAB, SLAB, SLAB)
    off = [int(v) for v in np.concatenate([[0], np.cumsum(sizes)])]
    main = ((off[0], off[2]), (off[3], off[8]), (off[9], off[12]))
    tail = ((off[8], off[9]), (off[2], off[3]))
    return main, tail


def _in_proj_columns():
    main, tail = _in_proj_segments()
    cols = lambda segs: np.concatenate([np.arange(a, b) for a, b in segs])
    return cols(main), cols(tail)


def in_proj_weights(w_in_l):
    main, tail = _in_proj_segments()
    w_main = jnp.concatenate([w_in_l[:, a:b] for a, b in main], axis=1).astype(BF16)
    pad = jnp.zeros((D_MODEL, LANES - sum(b - a for a, b in tail)), w_in_l.dtype)
    w_tail = jnp.concatenate([w_in_l[:, a:b] for a, b in tail] + [pad], axis=1).astype(BF16)
    return w_main, w_tail


MLA_TQ = 1024
MLA_CASTS_PER_CALL = 2


def mixers(proj, tail, p, l, b, t, cast):
    xbc = conv_silu(proj, p["conv_w"][l], p["conv_b"][l], b, t)
    y_f, y_b = ssd_scan(xbc, tail, p["dt_bias"][l], p["a_log"][l], b, t)
    y_ssm = ssd_combine(y_f, y_b, xbc, proj, p["d_skip"][l], p["ssm_norm_w"][l], 1024)

    y_na = na_attention(proj, p["na_rpb"][l], b, t)

    qf, kf, vf = mla_project(proj, tail, p["mla_q_norm_w"][l], p["mla_kv_norm_w"][l],
                             p["mla_w_uq"][l], p["mla_w_ukv"][l], b, t, 512)
    y_mla, cast_out = mla_attention(qf, kf, vf, b, t, MLA_TQ, 512, cast)

    qkvs = rope_qkv(proj, b, t, 1024)
    outs, lses = zip(*[band_attention(qkv, w, d) for qkv, (w, d) in zip(qkvs, DIL_PAIRS)])
    y_dil = dil_combine(outs, lses, 1024)
    return (y_ssm, y_na, y_mla, y_dil), cast_out


def kernel(x, attn_norm_w, w_in, conv_w, conv_b, a_log, dt_bias, d_skip, ssm_norm_w, na_rpb,
           mla_q_norm_w, mla_kv_norm_w, mla_w_uq, mla_w_ukv, w_o, ffn_norm_w, ffn_w_gate, ffn_w_up,
           ffn_w_down, router_w, exp_w_gate, exp_w_up, exp_w_down, final_norm_w):
    b, t, _ = x.shape
    n = b * t
    depth = w_in.shape[0]
    p = dict(conv_w=conv_w, conv_b=conv_b, a_log=a_log, dt_bias=dt_bias, d_skip=d_skip,
             ssm_norm_w=ssm_norm_w, na_rpb=na_rpb, mla_q_norm_w=mla_q_norm_w,
             mla_kv_norm_w=mla_kv_norm_w, mla_w_uq=mla_w_uq, mla_w_ukv=mla_w_ukv)
    x = x.reshape(n, D_MODEL)
    cast_rows = 256
    w_o_b = cast_bf16(w_o, cast_rows)
    ffn_b = [cast_bf16(w, cast_rows) for w in (ffn_w_gate, ffn_w_up, ffn_w_down)]
    exp_f32 = [exp_w_gate, exp_w_up, exp_w_down]
    exp_b = [None] * len(exp_f32)
    pending = [k for k, w in enumerate(exp_f32) if mla_cast_rows(w, b, t, MLA_TQ) is not None]
    moe_tm = 512
    normed = False
    for l in range(depth):
        w_main, w_tail = in_proj_weights(w_in[l])
        proj, tail = in_proj(x, attn_norm_w[l], w_main, w_tail, 512, PROJ_MAIN)
        jobs, pending = pending[:MLA_CASTS_PER_CALL], pending[MLA_CASTS_PER_CALL:]
        mix, cast_out = mixers(proj, tail, p, l, b, t, [exp_f32[k] for k in jobs])
        for k, w_b in zip(jobs, cast_out):
            exp_b[k] = w_b
        x = out_proj(mix, w_o_b, l, x, 512, D_MODEL)
        j = l // 2
        if l % 2 == 0:
            x = ffn_dense(x, ffn_norm_w[l], *ffn_b, j, 1024, 512)
        else:
            pending = []
            exp_b = [cast_bf16(w, cast_rows) if w_b is None else w_b for w, w_b in zip(exp_f32, exp_b)]
            top_i, gates = moe_router(x, ffn_norm_w[l], router_w[j], 512)
            src, pos, tile_expert, tile_valid = moe_plan(top_i, moe_tm)
            y = moe_ffn(x, ffn_norm_w[l], *exp_b, j, src, tile_expert, tile_valid, moe_tm, 512)
            last = l == depth - 1
            x = moe_combine(x, gates, y, pos, final_norm_w if last else None, 256)
            normed = last
    if not normed:
        x = rmsnorm_rows(x, final_norm_w, 1024)
    return x.reshape(b, t, D_MODEL)
```

```python
import functools
import math

import numpy as np
import jax
import jax.numpy as jnp
from jax import lax
from jax.experimental import pallas as pl
from jax.experimental.pallas import tpu as pltpu

F32 = jnp.float32
BF16 = jnp.bfloat16

D_MODEL = 2048
GRID_W = 64
HEAD_DIM = 64
ROPE_THETA = 10000.0
NORM_EPS = 1e-6
NEG_INF = -1e30

SSM_HEADS = 8
SSM_HEAD_DIM = 64
SSM_INNER = SSM_HEADS * SSM_HEAD_DIM
SSM_GROUPS = 2
SSM_STATE = 128
SSM_CONV = 5
SSM_CHUNK = 128
SSD_CHUNKS_PER_STEP = 2
SSM_CONV_CH = SSM_INNER + 2 * SSM_GROUPS * SSM_STATE

NA_HEADS = 8
NA_WIN_ROWS = 8
NA_WIN_COLS = 16
NA_COL_BLOCK = 16
NA_KEY_COLS = 32
NA_ROWS_PER_STEP = 8

MLA_HEADS = 8
MLA_Q_RANK = 512
MLA_KV_RANK = 512
MLA_NOPE = 64
MLA_ROPE = 32
MLA_V = 64

DIL_HEADS = 8
DIL_PAIRS = ((128, 1), (512, 4), (2048, 16))
DIL_QBLOCK = 128
BAND_BLOCKS_PER_STEP = 4

N_EXPERTS = 8
TOP_K = 2
SPLIT_PARTS = 3

LANES = 128
HEAD_PAIR = 2 * HEAD_DIM
N_PAIRS = 4
SLAB = 512

COL_Z, COL_XBC, COL_NAQ, COL_NAK, COL_NAV, COL_CQ, COL_CKV, COL_DLQ, COL_DLK, COL_DLV = (
    0, 1, 3, 4, 5, 6, 7, 8, 9, 10)
PROJ_MAIN = 11 * SLAB
TAIL_DT = 32

VMEM_LIMIT = 56 * 1024 * 1024


def _cparams(sem, vmem=VMEM_LIMIT):
    return pltpu.CompilerParams(dimension_semantics=sem, vmem_limit_bytes=vmem)


def _lane_lt(shape, bound, period=None):
    lane = lax.broadcasted_iota(jnp.int32, shape, len(shape) - 1)
    if period is not None:
        lane = lane % period
    return lane < bound


def _rms(x, w):
    ms = jnp.mean(x * x, axis=-1, keepdims=True)
    return x * lax.rsqrt(ms + NORM_EPS) * w


def _dot(a, b):
    return jnp.dot(a, b, preferred_element_type=F32)


def _dot_nt(a, b):
    return lax.dot_general(a, b, (((1,), (1,)), ((), ())), preferred_element_type=F32)


def _dot_tn(a, b):
    return lax.dot_general(a, b, (((0,), (0,)), ((), ())), preferred_element_type=F32)


def _cast_kernel(x_ref, o_ref):
    o_ref[...] = x_ref[...].astype(o_ref.dtype)


def cast_bf16(w, tr):
    shape = w.shape
    w2 = w.reshape(-1, shape[-1])
    r, c = w2.shape
    out = pl.pallas_call(
        _cast_kernel,
        grid=(r // tr,),
        in_specs=[pl.BlockSpec((tr, c), lambda i: (i, 0))],
        out_specs=pl.BlockSpec((tr, c), lambda i: (i, 0)),
        out_shape=jax.ShapeDtypeStruct((r, c), BF16),
        compiler_params=_cparams(("parallel",)),
        name="cast_bf16",
    )(w2)
    return out.reshape(shape)


def _in_proj_kernel(x_ref, nw_ref, w_ref, wt_ref, o_ref, t_ref, h_ref):
    @pl.when(pl.program_id(1) == 0)
    def _():
        h = _rms(x_ref[...], nw_ref[...]).astype(BF16)
        h_ref[...] = h
        t_ref[...] = _dot(h, wt_ref[...])

    o_ref[...] = _dot(h_ref[...], w_ref[...]).astype(o_ref.dtype)


def in_proj(x, nw, w_main, w_tail, tm, tn):
    n, k = x.shape
    nout = w_main.shape[1]
    w_mode = pl.Buffered(1) if tn == nout else None
    return pl.pallas_call(
        _in_proj_kernel,
        grid=(n // tm, nout // tn),
        in_specs=[
            pl.BlockSpec((tm, k), lambda i, j: (i, 0)),
            pl.BlockSpec((1, k), lambda i, j: (0, 0)),
            pl.BlockSpec((k, tn), lambda i, j: (0, j), pipeline_mode=w_mode),
            pl.BlockSpec((k, LANES), lambda i, j: (0, 0)),
        ],
        out_specs=[pl.BlockSpec((tm, tn), lambda i, j: (i, j)), pl.BlockSpec((tm, LANES), lambda i, j: (i, 0))],
        out_shape=[jax.ShapeDtypeStruct((n, nout), BF16), jax.ShapeDtypeStruct((n, LANES), F32)],
        scratch_shapes=[pltpu.VMEM((tm, k), BF16)],
        compiler_params=_cparams(("parallel", "arbitrary")),
        name="in_proj",
    )(x, nw.reshape(1, k), w_main, w_tail)


def _out_proj_kernel(a0_ref, a1_ref, a2_ref, a3_ref, w_ref, r_ref, o_ref):
    acc = r_ref[...]
    for s, a_ref in enumerate((a0_ref, a1_ref, a2_ref, a3_ref)):
        acc = acc + _dot(a_ref[...], w_ref[s * SLAB:(s + 1) * SLAB, :])
    o_ref[...] = acc


def out_proj(mix, w, layer, res, tm, tn):
    n = res.shape[0]
    return pl.pallas_call(
        _out_proj_kernel,
        grid=(n // tm, D_MODEL // tn),
        in_specs=[pl.BlockSpec((tm, SLAB), lambda i, j: (i, 0))] * 4 + [
            pl.BlockSpec((None, 4 * SLAB, tn), lambda i, j: (layer, 0, j)),
            pl.BlockSpec((tm, tn), lambda i, j: (i, j)),
        ],
        out_specs=pl.BlockSpec((tm, tn), lambda i, j: (i, j)),
        out_shape=jax.ShapeDtypeStruct((n, D_MODEL), F32),
        compiler_params=_cparams(("parallel", "arbitrary")),
        name="out_proj",
    )(*mix, w, res)


def _ffn_kernel(x_ref, nw_ref, wg_ref, wu_ref, wd_ref, o_ref, h_ref):
    @pl.when(pl.program_id(1) == 0)
    def _():
        x = x_ref[...]
        h_ref[...] = _rms(x, nw_ref[...]).astype(BF16)
        o_ref[...] = x

    _swiglu_rows(h_ref, wg_ref, wu_ref, wd_ref, o_ref)


FFN_ROW_CHUNK = 512


def _swiglu_rows(h_ref, wg_ref, wu_ref, wd_ref, o_ref):
    tm = h_ref.shape[0]
    for r0 in range(0, tm, FFN_ROW_CHUNK):
        rows = slice(r0, r0 + FFN_ROW_CHUNK)
        h = h_ref[rows, :]
        g = _dot(h, wg_ref[...])
        u = _dot(h, wu_ref[...])
        a = (g * jax.nn.sigmoid(g) * u).astype(BF16)
        o_ref[rows, :] += _dot(a, wd_ref[...])


def ffn_dense(x, nw, wg, wu, wd, layer, tm, tf):
    n = x.shape[0]
    d_ff = wg.shape[-1]
    return pl.pallas_call(
        _ffn_kernel,
        grid=(n // tm, d_ff // tf),
        in_specs=[
            pl.BlockSpec((tm, D_MODEL), lambda i, j: (i, 0)),
            pl.BlockSpec((1, D_MODEL), lambda i, j: (0, 0)),
            pl.BlockSpec((None, D_MODEL, tf), lambda i, j: (layer, 0, j)),
            pl.BlockSpec((None, D_MODEL, tf), lambda i, j: (layer, 0, j)),
            pl.BlockSpec((None, tf, D_MODEL), lambda i, j: (layer, j, 0)),
        ],
        out_specs=pl.BlockSpec((tm, D_MODEL), lambda i, j: (i, 0)),
        out_shape=jax.ShapeDtypeStruct((n, D_MODEL), F32),
        scratch_shapes=[pltpu.VMEM((tm, D_MODEL), BF16)],
        compiler_params=_cparams(("parallel", "arbitrary")),
        name="ffn_dense",
    )(x, nw.reshape(1, D_MODEL), wg, wu, wd)


def _router_kernel(x_ref, nw_ref, rw_ref, idx_ref, gate_ref):
    h = _rms(x_ref[...], nw_ref[...])
    acc = jnp.zeros((h.shape[0], LANES), F32)
    rem = h
    for _ in range(SPLIT_PARTS):
        part = rem.astype(BF16)
        acc = acc + _dot(part, rw_ref[...])
        rem = rem - part.astype(F32)
    logits = acc
    for k in range(1, SPLIT_PARTS):
        logits = logits + pltpu.roll(acc, LANES - k * N_EXPERTS, 1)
    lane = lax.broadcasted_iota(jnp.int32, logits.shape, 1)
    logits = jnp.where(lane < N_EXPERTS, logits, -jnp.inf)
    m1 = jnp.max(logits, axis=-1, keepdims=True)
    i1 = jnp.min(jnp.where(logits == m1, lane, LANES), axis=-1, keepdims=True)
    rest = jnp.where(lane == i1, -jnp.inf, logits)
    m2 = jnp.max(rest, axis=-1, keepdims=True)
    i2 = jnp.min(jnp.where(rest == m2, lane, LANES), axis=-1, keepdims=True)
    e2 = jnp.exp(m2 - m1)
    g1 = 1.0 / (1.0 + e2)
    g2 = e2 / (1.0 + e2)
    idx_ref[...] = jnp.where(lane == 0, i1, i2)[:, :TOP_K]
    gate_ref[...] = jnp.where(lane == 0, g1, g2)[:, :TOP_K]


def moe_router(x, nw, router_w, tm):
    n = x.shape[0]
    parts, rem = [], router_w
    for _ in range(SPLIT_PARTS):
        parts.append(rem.astype(BF16))
        rem = rem - parts[-1].astype(F32)
    pad = jnp.zeros((D_MODEL, LANES - SPLIT_PARTS * N_EXPERTS), BF16)
    rw = jnp.concatenate(parts + [pad], axis=1)
    return pl.pallas_call(
        _router_kernel,
        grid=(n // tm,),
        in_specs=[
            pl.BlockSpec((tm, D_MODEL), lambda i: (i, 0)),
            pl.BlockSpec((1, D_MODEL), lambda i: (0, 0)),
            pl.BlockSpec((D_MODEL, LANES), lambda i: (0, 0)),
        ],
        out_specs=[pl.BlockSpec((tm, TOP_K), lambda i: (i, 0)),
                   pl.BlockSpec((tm, TOP_K), lambda i: (i, 0))],
        out_shape=[jax.ShapeDtypeStruct((n, TOP_K), jnp.int32),
                   jax.ShapeDtypeStruct((n, TOP_K), F32)],
        compiler_params=_cparams(("parallel",)),
        name="moe_router",
    )(x, nw.reshape(1, D_MODEL), rw)


def moe_plan(top_i, tm):
    n = top_i.shape[0]
    flat_e = top_i.reshape(-1)
    onehot = (flat_e[:, None] == jnp.arange(N_EXPERTS, dtype=jnp.int32)[None, :]).astype(jnp.int32)
    csum = jnp.cumsum(onehot, axis=0)
    counts = csum[-1]
    rank = jnp.sum(onehot * csum, axis=1) - 1
    padded = ((counts + tm - 1) // tm) * tm
    pend = jnp.cumsum(padded)
    pstart = pend - padded
    pos = pstart[flat_e] + rank
    n_slots = n * TOP_K + N_EXPERTS * tm
    n_tiles = n_slots // tm
    src = jnp.zeros((n_slots,), jnp.int32).at[pos].set(jnp.arange(n * TOP_K, dtype=jnp.int32) // TOP_K)
    tile_start = jnp.arange(n_tiles, dtype=jnp.int32) * tm
    tile_expert = jnp.sum((tile_start[:, None] >= pend[None, :]).astype(jnp.int32), axis=1)
    tile_valid = (tile_start < pend[-1]).astype(jnp.int32)
    last_valid = jnp.maximum(pend[-1] // tm - 1, 0)
    tile_expert = jnp.where(tile_valid == 1, tile_expert, tile_expert[last_valid]).astype(jnp.int32)
    return src, pos.reshape(n, TOP_K).astype(jnp.int32), tile_expert, tile_valid


def _moe_ffn_kernel(te_ref, tv_ref, src_ref, nsrc_ref, x_hbm, nw_ref, wg_ref, wu_ref, wd_ref, y_ref,
                    xbuf, h_ref, sem, *, tm, rows_per_step):
    i = pl.program_id(0)
    j = pl.program_id(1)
    issued = xbuf.shape[0]
    valid = tv_ref[i] == 1
    has_rows = jnp.logical_or(i == 0, tv_ref[jnp.maximum(i - 1, 0)] == 1)

    def start_row(idx_ref, row):
        tok = idx_ref[0, 0, jnp.minimum(row, tm - 1)]
        pltpu.make_async_copy(x_hbm.at[pl.ds(tok, 1)], xbuf.at[pl.ds(row, 1)], sem).start(priority=1)

    @pl.when(j == 0)
    def _():
        y_ref[...] = jnp.zeros_like(y_ref)

    @pl.when(jnp.logical_and(j == 0, i == 0))
    def _():
        def start(r, c):
            start_row(src_ref, r)
            return c

        lax.fori_loop(0, issued, start, 0)

    @pl.when(jnp.logical_and(j == 0, has_rows))
    def _():
        pltpu.make_async_copy(x_hbm.at[pl.ds(0, issued)], xbuf.at[pl.ds(0, issued)], sem).wait()
        h_ref[...] = _rms(xbuf[0:tm, :], nw_ref[...]).astype(BF16)

    @pl.when(valid)
    def _():
        for r in range(rows_per_step):
            start_row(nsrc_ref, j * rows_per_step + r)
        _swiglu_rows(h_ref, wg_ref, wu_ref, wd_ref, y_ref)


def moe_ffn(x, nw, wg, wu, wd, layer, src, tile_expert, tile_valid, tm, tf):
    n_slots = src.shape[0]
    n_tiles = n_slots // tm + 1
    d_ff = wg.shape[-1]
    nf = d_ff // tf
    sublanes = 8
    rows_per_step = -(-tm // (nf * sublanes)) * sublanes
    buf_rows = rows_per_step * nf
    tile_expert = jnp.concatenate([tile_expert, tile_expert[-1:]])
    tile_valid = jnp.concatenate([tile_valid, jnp.zeros((1,), tile_valid.dtype)])
    src3 = jnp.concatenate([src, jnp.zeros((tm,), src.dtype)]).reshape(n_tiles, 1, tm)

    def wcol(i, j, te_ref, tv_ref):
        return (layer, te_ref[i], 0, jnp.where(tv_ref[i] == 1, j, nf - 1))

    def wrow(i, j, te_ref, tv_ref):
        return (layer, te_ref[i], jnp.where(tv_ref[i] == 1, j, nf - 1), 0)

    grid_spec = pltpu.PrefetchScalarGridSpec(
        num_scalar_prefetch=2,
        grid=(n_tiles, nf),
        in_specs=[
            pl.BlockSpec((1, 1, tm), lambda i, j, *_: (i, 0, 0), memory_space=pltpu.SMEM),
            pl.BlockSpec((1, 1, tm), lambda i, j, *_: (jnp.minimum(i + 1, n_tiles - 1), 0, 0),
                         memory_space=pltpu.SMEM),
            pl.BlockSpec(memory_space=pl.ANY),
            pl.BlockSpec((1, D_MODEL), lambda i, j, *_: (0, 0)),
            pl.BlockSpec((None, None, D_MODEL, tf), wcol),
            pl.BlockSpec((None, None, D_MODEL, tf), wcol),
            pl.BlockSpec((None, None, tf, D_MODEL), wrow),
        ],
        out_specs=pl.BlockSpec((tm, D_MODEL), lambda i, j, *_: (i, 0)),
        scratch_shapes=[pltpu.VMEM((buf_rows, D_MODEL), F32), pltpu.VMEM((tm, D_MODEL), BF16),
                        pltpu.SemaphoreType.DMA],
    )
    return pl.pallas_call(
        functools.partial(_moe_ffn_kernel, tm=tm, rows_per_step=rows_per_step),
        grid_spec=grid_spec,
        out_shape=jax.ShapeDtypeStruct((n_tiles * tm, D_MODEL), F32),
        compiler_params=_cparams(("arbitrary", "arbitrary")),
        name="moe_ffn",
    )(tile_expert, tile_valid, src3, src3, x, nw.reshape(1, D_MODEL), wg, wu, wd)


def _moe_combine_kernel(pos_ref, npos_ref, x_ref, gate_ref, y_hbm, fw_ref, o_ref, ybuf, sem, *, tm, final_norm):
    i = pl.program_id(0)
    n_tiles = pl.num_programs(0)
    cur = i % 2

    def row_copy(idx_ref, buf, r, k):
        slot = idx_ref[0, 0, r * TOP_K + k]
        return pltpu.make_async_copy(y_hbm.at[pl.ds(slot, 1)], ybuf.at[buf, k, pl.ds(r, 1)], sem.at[buf])

    def gather(idx_ref, buf):
        def start(r, c):
            for k in range(TOP_K):
                row_copy(idx_ref, buf, r, k).start(priority=k % 2)
            return c

        lax.fori_loop(0, tm, start, 0, unroll=8)

    @pl.when(i == 0)
    def _():
        gather(pos_ref, 0)

    @pl.when(i + 1 < n_tiles)
    def _():
        gather(npos_ref, 1 - cur)

    for k in range(TOP_K):
        pltpu.make_async_copy(y_hbm.at[pl.ds(0, tm)], ybuf.at[cur, k], sem.at[cur]).wait()
    gates = gate_ref[...]
    out = x_ref[...]
    for k in range(TOP_K):
        out = out + gates[:, k:k + 1] * ybuf[cur, k]
    if final_norm:
        out = _rms(out, fw_ref[...])
    o_ref[...] = out


def moe_combine(x, gates, y, pos, final_w, tm):
    n = x.shape[0]
    final_norm = final_w is not None
    fw = final_w if final_norm else jnp.ones((D_MODEL,), F32)
    n_tiles = n // tm
    pos3 = pos.reshape(n_tiles, 1, tm * TOP_K)
    return pl.pallas_call(
        functools.partial(_moe_combine_kernel, tm=tm, final_norm=final_norm),
        grid=(n_tiles,),
        in_specs=[
            pl.BlockSpec((1, 1, tm * TOP_K), lambda i: (i, 0, 0), memory_space=pltpu.SMEM),
            pl.BlockSpec((1, 1, tm * TOP_K), lambda i: (jnp.minimum(i + 1, n_tiles - 1), 0, 0),
                         memory_space=pltpu.SMEM),
            pl.BlockSpec((tm, D_MODEL), lambda i: (i, 0)),
            pl.BlockSpec((tm, TOP_K), lambda i: (i, 0)),
            pl.BlockSpec(memory_space=pl.ANY),
            pl.BlockSpec((1, D_MODEL), lambda i: (0, 0)),
        ],
        out_specs=pl.BlockSpec((tm, D_MODEL), lambda i: (i, 0)),
        out_shape=jax.ShapeDtypeStruct((n, D_MODEL), F32),
        scratch_shapes=[pltpu.VMEM((2, TOP_K, tm, D_MODEL), F32), pltpu.SemaphoreType.DMA((2,))],
        compiler_params=_cparams(("arbitrary",)),
        name="moe_combine",
    )(pos3, pos3, x, gates, y, fw.reshape(1, D_MODEL))


def _rmsnorm_kernel(x_ref, w_ref, o_ref):
    o_ref[...] = _rms(x_ref[...], w_ref[...])


def rmsnorm_rows(x, w, tm):
    n = x.shape[0]
    return pl.pallas_call(
        _rmsnorm_kernel,
        grid=(n // tm,),
        in_specs=[pl.BlockSpec((tm, D_MODEL), lambda i: (i, 0)),
                  pl.BlockSpec((1, D_MODEL), lambda i: (0, 0))],
        out_specs=pl.BlockSpec((tm, D_MODEL), lambda i: (i, 0)),
        out_shape=jax.ShapeDtypeStruct((n, D_MODEL), F32),
        compiler_params=_cparams(("parallel",)),
        name="final_norm",
    )(x, w.reshape(1, D_MODEL))


CONV_PAD = 8


def _conv_kernel(x_ref, w_ref, b_ref, o_ref, pad_ref, *, t):
    half = SSM_CONV // 2
    zeros = jnp.zeros((CONV_PAD, pad_ref.shape[1]), F32)
    pad_ref[0:CONV_PAD, :] = zeros
    pad_ref[CONV_PAD + t:CONV_PAD + t + CONV_PAD, :] = zeros
    pad_ref[CONV_PAD:CONV_PAD + t, :] = x_ref[...].astype(F32)
    acc = jnp.zeros(o_ref.shape, F32) + b_ref[...]
    for k in range(SSM_CONV):
        acc = acc + pad_ref[pl.ds(CONV_PAD - half + k, t), :] * w_ref[k:k + 1, :]
    o_ref[...] = (acc * jax.nn.sigmoid(acc)).astype(o_ref.dtype)


def conv_silu(proj, conv_w, conv_b, b, t):
    tc = 256
    nblk = SSM_CONV_CH // tc
    col0 = COL_XBC * SLAB // tc
    return pl.pallas_call(
        functools.partial(_conv_kernel, t=t),
        grid=(b, nblk),
        in_specs=[
            pl.BlockSpec((t, tc), lambda i, j: (i, col0 + j)),
            pl.BlockSpec((SSM_CONV, tc), lambda i, j: (0, j)),
            pl.BlockSpec((1, tc), lambda i, j: (0, j)),
        ],
        out_specs=pl.BlockSpec((t, tc), lambda i, j: (i, j)),
        out_shape=jax.ShapeDtypeStruct((b * t, SSM_CONV_CH), BF16),
        scratch_shapes=[pltpu.VMEM((t + 2 * CONV_PAD, tc), F32)],
        compiler_params=_cparams(("parallel", "parallel")),
        name="conv_silu",
    )(proj, conv_w, conv_b.reshape(1, SSM_CONV_CH))


def _ssd_stage1(xbc_ref, tail_ref, bias_ref, alog_ref, tri_ref, sel_ref, state_ref, direction):
    q = SSM_CHUNK
    dt = jax.nn.softplus(tail_ref[...] + bias_ref[...])
    da = dt * (-jnp.exp(alog_ref[...]))
    cs = jnp.dot(tri_ref[...], da, preferred_element_type=F32, precision=lax.Precision.HIGHEST)
    total = cs[q - 1:q, :]
    if direction == 0:
        e_out = cs
        e_in = total - cs
        e_seg = cs
    else:
        ex = cs - da
        e_out = total - ex
        e_in = ex
        e_seg = -ex
    dec_out_b = jnp.exp(e_out).astype(BF16)
    dec_in_dt_b = (jnp.exp(e_in) * dt).astype(BF16)
    pairs_per_group = N_PAIRS // SSM_GROUPS
    cbs, y_offs, in_scales = [], [], []
    for g in range(SSM_GROUPS):
        bm = xbc_ref[:, SSM_INNER + g * SSM_STATE:SSM_INNER + (g + 1) * SSM_STATE]
        cm = xbc_ref[:, SSM_INNER + (SSM_GROUPS + g) * SSM_STATE:SSM_INNER + (SSM_GROUPS + g + 1) * SSM_STATE]
        cbs.append(_dot_nt(cm, bm))
        for pair in range(g * pairs_per_group, (g + 1) * pairs_per_group):
            sel = sel_ref[direction, pair]
            y_offs.append(_dot(dec_out_b, sel) * _dot(cm, state_ref[pair].astype(BF16)))
            in_scales.append(_dot(dec_in_dt_b, sel))
    return dict(e_seg=e_seg, e_seg_t=jnp.transpose(e_seg), dt_t=jnp.transpose(dt), dec_tot=jnp.exp(total),
                cbs=cbs, y_offs=y_offs, in_scales=in_scales)


def _ssd_stage2(ctx, xbc_ref, y_ref, state_ref, direction):
    q = SSM_CHUNK
    row = lax.broadcasted_iota(jnp.int32, (q, q), 0)
    col = lax.broadcasted_iota(jnp.int32, (q, q), 1)
    keep = (row >= col) if direction == 0 else (col >= row)
    first_half = _lane_lt((q, HEAD_PAIR), HEAD_DIM)
    pairs_per_group = N_PAIRS // SSM_GROUPS
    for pair in range(N_PAIRS):
        g = pair // pairs_per_group
        bm = xbc_ref[:, SSM_INNER + g * SSM_STATE:SSM_INNER + (g + 1) * SSM_STATE]
        xs_pair = xbc_ref[:, pair * HEAD_PAIR:(pair + 1) * HEAD_PAIR]
        ys, decs = [], []
        for hh in range(2):
            lane = TAIL_DT + direction * SSM_HEADS + pair * 2 + hh
            seg = ctx["e_seg"][:, lane:lane + 1] - ctx["e_seg_t"][lane:lane + 1, :]
            lmat = jnp.where(keep, jnp.exp(seg), 0.0)
            w = (ctx["cbs"][g] * lmat * ctx["dt_t"][lane:lane + 1, :]).astype(BF16)
            ys.append(_dot(w, xs_pair))
            decs.append(ctx["dec_tot"][:, lane:lane + 1])
        y_ref[:, pair * HEAD_PAIR:(pair + 1) * HEAD_PAIR] = jnp.where(first_half, ys[0], ys[1]) + ctx["y_offs"][pair]
        st = state_ref[pair]
        first_half_s = _lane_lt(st.shape, HEAD_DIM)
        state_ref[pair] = (st * jnp.where(first_half_s, decs[0], decs[1])
                           + _dot_tn(bm, (xs_pair * ctx["in_scales"][pair]).astype(BF16)))


def _ssd_kernel(xbc_f_ref, tail_f_ref, xbc_b_ref, tail_b_ref, bias_ref, alog_ref, tri_ref, sel_ref,
                y_f_ref, y_b_ref, state_ref):
    @pl.when(pl.program_id(1) == 0)
    def _():
        state_ref[...] = jnp.zeros_like(state_ref)

    q = SSM_CHUNK
    for k in range(SSD_CHUNKS_PER_STEP):
        ins = []
        for d, refs in enumerate(((xbc_f_ref, tail_f_ref, y_f_ref), (xbc_b_ref, tail_b_ref, y_b_ref))):
            sub = k if d == 0 else SSD_CHUNKS_PER_STEP - 1 - k
            ins.append(tuple(r.at[sub * q:(sub + 1) * q] for r in refs))
        ctxs = [_ssd_stage1(xbc_ref, tail_ref, bias_ref, alog_ref, tri_ref, sel_ref, state_ref.at[d], d)
                for d, (xbc_ref, tail_ref, _) in enumerate(ins)]
        for d, (xbc_ref, _, y_ref) in enumerate(ins):
            _ssd_stage2(ctxs[d], xbc_ref, y_ref, state_ref.at[d], d)


def _ssd_lane_selectors():
    sel = np.zeros((2, N_PAIRS, LANES, HEAD_PAIR), np.float32)
    for d in range(2):
        for pair in range(N_PAIRS):
            for hh in range(2):
                sel[d, pair, TAIL_DT + d * SSM_HEADS + pair * 2 + hh, hh * HEAD_DIM:(hh + 1) * HEAD_DIM] = 1.0
    return jnp.asarray(sel, BF16)


def ssd_scan(xbc, tail, dt_bias, a_log, b, t):
    q = SSM_CHUNK
    rows = q * SSD_CHUNKS_PER_STEP
    nc = t // rows
    bias_row = jnp.zeros((1, LANES), F32).at[0, TAIL_DT:TAIL_DT + 2 * SSM_HEADS].set(dt_bias.reshape(-1))
    alog_row = jnp.zeros((1, LANES), F32).at[0, TAIL_DT:TAIL_DT + 2 * SSM_HEADS].set(a_log.reshape(-1))
    tri = jnp.asarray(np.tril(np.ones((q, q), np.float32)))
    fwd = lambda i, c: (i * nc + c, 0)
    bwd = lambda i, c: (i * nc + nc - 1 - c, 0)
    const = lambda i, c: (0, 0)
    return pl.pallas_call(
        _ssd_kernel,
        grid=(b, nc),
        in_specs=[
            pl.BlockSpec((rows, SSM_CONV_CH), fwd), pl.BlockSpec((rows, LANES), fwd),
            pl.BlockSpec((rows, SSM_CONV_CH), bwd), pl.BlockSpec((rows, LANES), bwd),
            pl.BlockSpec((1, LANES), const), pl.BlockSpec((1, LANES), const), pl.BlockSpec((q, q), const),
            pl.BlockSpec((2, N_PAIRS, LANES, HEAD_PAIR), lambda i, c: (0, 0, 0, 0)),
        ],
        out_specs=[pl.BlockSpec((rows, SSM_INNER), fwd), pl.BlockSpec((rows, SSM_INNER), bwd)],
        out_shape=[jax.ShapeDtypeStruct((b * t, SSM_INNER), F32)] * 2,
        scratch_shapes=[pltpu.VMEM((2, N_PAIRS, SSM_STATE, HEAD_PAIR), F32)],
        compiler_params=_cparams(("parallel", "arbitrary")),
        name="ssd_scan",
    )(xbc, tail, xbc, tail, bias_row, alog_row, tri, _ssd_lane_selectors())


def _ssd_combine_kernel(yf_ref, yb_ref, xs_ref, z_ref, d_ref, nw_ref, o_ref):
    y = yf_ref[...] + yb_ref[...] + xs_ref[...].astype(F32) * d_ref[...]
    z = z_ref[...].astype(F32)
    o_ref[...] = _rms(y * (z * jax.nn.sigmoid(z)), nw_ref[...]).astype(o_ref.dtype)


def ssd_combine(y_f, y_b, xbc, proj, d_skip, norm_w, tm):
    n = y_f.shape[0]
    d_row = jnp.repeat(d_skip, SSM_HEAD_DIM).reshape(1, SSM_INNER)
    row = lambda i: (i, 0)
    return pl.pallas_call(
        _ssd_combine_kernel,
        grid=(n // tm,),
        in_specs=[
            pl.BlockSpec((tm, SSM_INNER), row),
            pl.BlockSpec((tm, SSM_INNER), row),
            pl.BlockSpec((tm, SSM_INNER), row),
            pl.BlockSpec((tm, SLAB), lambda i: (i, COL_Z)),
            pl.BlockSpec((1, SSM_INNER), lambda i: (0, 0)),
            pl.BlockSpec((1, SSM_INNER), lambda i: (0, 0)),
        ],
        out_specs=pl.BlockSpec((tm, SSM_INNER), row),
        out_shape=jax.ShapeDtypeStruct((n, SSM_INNER), BF16),
        compiler_params=_cparams(("parallel",)),
        name="ssd_combine",
    )(y_f, y_b, xbc, proj, d_row, norm_w.reshape(1, SSM_INNER))


def _pair_scores(q2, k2):
    first_q = _lane_lt(q2.shape, HEAD_DIM)
    zero = jnp.zeros_like(q2)
    qs = jnp.concatenate([jnp.where(first_q, q2, zero), jnp.where(first_q, zero, q2)], axis=0)
    return _dot_nt(qs, k2)


def _pair_attend(s, v2):
    tq = s.shape[0] // 2
    m = jnp.max(s, axis=-1, keepdims=True)
    p = jnp.exp(s - m)
    l = jnp.sum(p, axis=-1, keepdims=True)
    o = _dot(p.astype(BF16), v2) / l
    lse = m + jnp.log(l)
    first_o = _lane_lt((tq, HEAD_PAIR), HEAD_DIM)
    return jnp.where(first_o, o[:tq], o[tq:]), jnp.where(first_o, lse[:tq], lse[tq:])


def na_bias_tables(rpb, rows):
    kr = min(NA_WIN_ROWS, rows)
    qc = np.arange(GRID_W)
    kc = np.arange(GRID_W)
    q_start = np.clip(qc - NA_WIN_COLS // 2, 0, GRID_W - NA_WIN_COLS)
    col_in = (kc[None, :] >= q_start[:, None]) & (kc[None, :] < q_start[:, None] + NA_WIN_COLS)
    col_off = np.clip(kc[None, :] - qc[:, None] + NA_WIN_COLS - 1, 0, 2 * NA_WIN_COLS - 2)
    onehot = (col_off[None] == np.arange(2 * NA_WIN_COLS - 1)[:, None, None]).astype(np.float32)
    expanded = jnp.einsum("hrc,cqk->hqrk", rpb, jnp.asarray(onehot), precision=lax.Precision.HIGHEST)
    expanded = jnp.where(jnp.asarray(col_in)[None, :, None, :], expanded, NEG_INF)

    def table(r):
        row_start = int(np.clip(r - kr // 2, 0, rows - kr))
        ro0 = row_start - r + NA_WIN_ROWS - 1
        return expanded[:, :, ro0:ro0 + kr, :].reshape(N_PAIRS, 2 * GRID_W, kr * GRID_W)

    rs = NA_ROWS_PER_STEP
    lo = [table(r) for r in range(rs)]
    mid = [table(min(rs, rows - 1))] * rs
    hi = [table(r) for r in range(rows - rs, rows)]
    return jnp.stack([jnp.stack(lo), jnp.stack(mid), jnp.stack(hi)])


def _na_kernel(q_ref, k_ref, v_ref, bias_ref, o_ref, *, rows, kr):
    step = pl.program_id(1)
    rs = NA_ROWS_PER_STEP
    for rr in range(rs):
        r = step * rs + rr
        row_start = jnp.clip(r - kr // 2, 0, rows - kr)
        k0 = pl.multiple_of(row_start * GRID_W, GRID_W)
        pair_cols = [slice(pair * HEAD_PAIR, (pair + 1) * HEAD_PAIR) for pair in range(N_PAIRS)]
        scores = []
        for pair, cols in enumerate(pair_cols):
            q2 = q_ref[rr * GRID_W:(rr + 1) * GRID_W, cols] * jnp.asarray(HEAD_DIM ** -0.5, BF16)
            scores.append(_pair_scores(q2, k_ref[pl.ds(k0, kr * GRID_W), cols]) + bias_ref[0, rr, pair])
        for s, cols in zip(scores, pair_cols):
            o, _ = _pair_attend(s, v_ref[pl.ds(k0, kr * GRID_W), cols])
            o_ref[rr * GRID_W:(rr + 1) * GRID_W, cols] = o.astype(o_ref.dtype)


def na_attention(proj, rpb, b, t):
    rows = t // GRID_W
    kr = min(NA_WIN_ROWS, rows)
    rs = NA_ROWS_PER_STEP
    nsteps = rows // rs
    bias = na_bias_tables(rpb, rows)

    def kind(i, s):
        return jnp.where(s == 0, 0, jnp.where(s == nsteps - 1, 2, 1))

    return pl.pallas_call(
        functools.partial(_na_kernel, rows=rows, kr=kr),
        grid=(b, nsteps),
        in_specs=[
            pl.BlockSpec((rs * GRID_W, SLAB), lambda i, s: (i * nsteps + s, COL_NAQ)),
            pl.BlockSpec((t, SLAB), lambda i, s: (i, COL_NAK)),
            pl.BlockSpec((t, SLAB), lambda i, s: (i, COL_NAV)),
            pl.BlockSpec((1, rs, N_PAIRS, 2 * GRID_W, kr * GRID_W), lambda i, s: (kind(i, s), 0, 0, 0, 0)),
        ],
        out_specs=pl.BlockSpec((rs * GRID_W, SLAB), lambda i, s: (i * nsteps + s, 0)),
        out_shape=jax.ShapeDtypeStruct((b * t, SLAB), BF16),
        compiler_params=_cparams(("parallel", "arbitrary")),
        name="na_attention",
    )(proj, proj, proj, bias)


def _rope_angles(t, d):
    inv = ROPE_THETA ** (-np.arange(0, d, 2, dtype=np.float32) / d)
    return np.arange(t, dtype=np.float32)[:, None] * inv[None, :]


def rope_tables_pair(t):
    ang = _rope_angles(t, HEAD_DIM)
    cos = np.tile(np.cos(ang), (1, 4))
    sin = np.tile(np.concatenate([-np.sin(ang), np.sin(ang)], axis=1), (1, 2))
    return jnp.asarray(cos, F32), jnp.asarray(sin, F32)


FOLD_CHUNK = 256
FOLD_DILS = tuple(d for _, d in DIL_PAIRS if d > 1)


def fold_permutation(dil):
    per = FOLD_CHUNK // dil
    perm = np.zeros((FOLD_CHUNK, FOLD_CHUNK), np.float32)
    dst = np.arange(FOLD_CHUNK)
    perm[dst, (dst % per) * dil + dst // per] = 1.0
    return jnp.asarray(perm, BF16)


def _rope_qkv_kernel(x_ref, v_ref, cos_ref, sin_ref, *rest):
    nd = len(FOLD_DILS)
    perm_refs, o_ref, fold_refs = rest[:nd], rest[nd], rest[nd + 1:]
    cos = cos_ref[...]
    sin = sin_ref[...]
    half = HEAD_DIM // 2
    for c in range(x_ref.shape[1] // LANES):
        x = x_ref[:, c * LANES:(c + 1) * LANES].astype(F32)
        rot = jnp.where(_lane_lt(x.shape, half, HEAD_DIM),
                        pltpu.roll(x, LANES - half, 1), pltpu.roll(x, half, 1))
        y = x * cos + rot * sin
        if c < N_PAIRS:
            y = y * (HEAD_DIM ** -0.5)
        o_ref[0, :, c * LANES:(c + 1) * LANES] = y.astype(o_ref.dtype)
    o_ref[0, :, 2 * SLAB:3 * SLAB] = v_ref[...]
    tm = x_ref.shape[0]
    for dil, perm_ref, f_ref in zip(FOLD_DILS, perm_refs, fold_refs):
        per = FOLD_CHUNK // dil
        for c in range(tm // FOLD_CHUNK):
            folded = _dot(perm_ref[...], o_ref[0, c * FOLD_CHUNK:(c + 1) * FOLD_CHUNK, :]).astype(f_ref.dtype)
            for p in range(dil):
                f_ref[p, c * per:(c + 1) * per, :] = folded[p * per:(p + 1) * per, :]


def rope_qkv(proj, b, t, tm):
    n = b * t
    cos, sin = rope_tables_pair(t)
    nb = t // tm
    fold_spec = lambda d: pl.BlockSpec((None, d, tm // d, 3 * SLAB), lambda i: (i // nb, 0, i % nb, 0))
    outs = pl.pallas_call(
        _rope_qkv_kernel,
        grid=(n // tm,),
        in_specs=[
            pl.BlockSpec((tm, 2 * SLAB), lambda i: (i, COL_DLQ // 2)),
            pl.BlockSpec((tm, SLAB), lambda i: (i, COL_DLV)),
            pl.BlockSpec((tm, LANES), lambda i: (i % nb, 0)),
            pl.BlockSpec((tm, LANES), lambda i: (i % nb, 0)),
        ] + [pl.BlockSpec((FOLD_CHUNK, FOLD_CHUNK), lambda i: (0, 0))] * len(FOLD_DILS),
        out_specs=[fold_spec(1)] + [fold_spec(d) for d in FOLD_DILS],
        out_shape=[jax.ShapeDtypeStruct((b, d, t // d, 3 * SLAB), BF16) for d in (1,) + FOLD_DILS],
        compiler_params=_cparams(("parallel",)),
        name="rope_qkv",
    )(proj, proj, cos, sin, *[fold_permutation(d) for d in FOLD_DILS])
    by_dil = dict(zip((1,) + FOLD_DILS, outs))
    return [by_dil[d] for _, d in DIL_PAIRS]


def _band_kernel(q_ref, k_ref, v_ref, o_ref, lse_ref, *, sub, half, span):
    tq = DIL_QBLOCK
    blocks = q_ref.shape[0] // tq
    for blk in range(blocks):
        qb = pl.program_id(2) * blocks + blk
        rows = slice(blk * tq, (blk + 1) * tq)
        start = jnp.clip(qb * tq - half, 0, sub - span)
        start = pl.multiple_of(start, half)
        q_pos = qb * tq + lax.broadcasted_iota(jnp.int32, (2 * tq, span), 0) % tq
        k_pos = start + lax.broadcasted_iota(jnp.int32, (2 * tq, span), 1)
        valid = jnp.abs(k_pos - q_pos) <= half
        pair_cols = [slice(pair * HEAD_PAIR, (pair + 1) * HEAD_PAIR) for pair in range(N_PAIRS)]
        scores = [jnp.where(valid, _pair_scores(q_ref[rows, cols], k_ref[pl.ds(start, span), cols]), NEG_INF)
                  for cols in pair_cols]
        for s, cols in zip(scores, pair_cols):
            o, lse = _pair_attend(s, v_ref[pl.ds(start, span), cols])
            o_ref[rows, cols] = o.astype(o_ref.dtype)
            lse_ref[rows, cols] = lse


def band_attention(qkv, window, dil):
    b, _, sub, _ = qkv.shape
    half = window // (2 * dil)
    span = DIL_QBLOCK + 2 * half
    tq = DIL_QBLOCK * min(BAND_BLOCKS_PER_STEP, sub // DIL_QBLOCK)
    nqb = sub // tq
    return pl.pallas_call(
        functools.partial(_band_kernel, sub=sub, half=half, span=span),
        grid=(b, dil, nqb),
        in_specs=[
            pl.BlockSpec((None, None, tq, SLAB), lambda i, p, s: (i, p, s, 0)),
            pl.BlockSpec((None, None, sub, SLAB), lambda i, p, s: (i, p, 0, 1)),
            pl.BlockSpec((None, None, sub, SLAB), lambda i, p, s: (i, p, 0, 2)),
        ],
        out_specs=[pl.BlockSpec((None, None, tq, SLAB), lambda i, p, s: (i, p, s, 0))] * 2,
        out_shape=[jax.ShapeDtypeStruct((b, dil, sub, SLAB), BF16), jax.ShapeDtypeStruct((b, dil, sub, SLAB), F32)],
        compiler_params=_cparams(("parallel", "parallel", "arbitrary")),
        name="band_attention_d%d" % dil,
    )(qkv, qkv, qkv)


def _dil_combine_kernel(*refs):
    nbr = len(DIL_PAIRS)
    o_refs, l_refs, out_ref = refs[:nbr], refs[nbr:2 * nbr], refs[2 * nbr]
    scratch = iter(refs[2 * nbr + 1:])

    def token_order(ref):
        dil = ref.shape[0]
        if dil == 1:
            return ref[0].astype(F32)
        buf = next(scratch)
        per = ref.shape[1]
        for p in range(dil):
            for c in range(SLAB // LANES):
                buf[c, pl.ds(p, per, stride=dil), :] = ref[p, :, c * LANES:(c + 1) * LANES].astype(F32)
        return jnp.concatenate([buf[c] for c in range(SLAB // LANES)], axis=1)

    os = [token_order(r) for r in o_refs]
    lses = [token_order(r) for r in l_refs]
    m = functools.reduce(jnp.maximum, lses)
    ws = [jnp.exp(l - m) for l in lses]
    den = functools.reduce(jnp.add, ws)
    acc = functools.reduce(jnp.add, [(w / den) * o for w, o in zip(ws, os)])
    out_ref[...] = acc.astype(out_ref.dtype)


def dil_combine(outs, lses, tm):
    b, _, t, _ = outs[0].shape
    n = b * t
    nb = t // tm
    spec = lambda a: pl.BlockSpec((None, a.shape[1], tm // a.shape[1], SLAB), lambda i: (i // nb, 0, i % nb, 0))
    n_folded = sum(1 for a in outs + lses if a.shape[1] > 1)
    return pl.pallas_call(
        _dil_combine_kernel,
        grid=(n // tm,),
        in_specs=[spec(a) for a in outs + lses],
        out_specs=pl.BlockSpec((tm, SLAB), lambda i: (i, 0)),
        out_shape=jax.ShapeDtypeStruct((n, SLAB), BF16),
        scratch_shapes=[pltpu.VMEM((SLAB // LANES, tm, LANES), F32)] * n_folded,
        compiler_params=_cparams(("parallel",)),
        name="dil_combine",
    )(*outs, *lses)


MLA_QK = MLA_NOPE + MLA_ROPE


def mla_tables(t):
    ang = _rope_angles(t, MLA_ROPE)
    cos2 = np.concatenate([np.cos(ang), np.cos(ang)], axis=1)
    sin2 = np.concatenate([np.sin(ang), np.sin(ang)], axis=1)
    z = lambda w: np.zeros((t, w), np.float32)
    q_cos = np.concatenate([np.ones((t, MLA_NOPE), np.float32), cos2, z(LANES - MLA_QK)], axis=1)
    q_sin = np.concatenate([z(MLA_NOPE), sin2, z(LANES - MLA_QK)], axis=1)
    k_cos = np.concatenate([cos2, z(LANES - MLA_ROPE)], axis=1)
    k_sin = np.concatenate([-sin2[:, :MLA_ROPE // 2], sin2[:, MLA_ROPE // 2:], z(LANES - MLA_ROPE)], axis=1)
    return tuple(jnp.asarray(a, F32) for a in (q_cos, q_sin, k_cos, k_sin))


def mla_weights(w_uq, w_ukv):
    hq = w_uq.reshape(MLA_Q_RANK, MLA_HEADS, MLA_QK)
    nope, pe = hq[..., :MLA_NOPE], hq[..., MLA_NOPE:]
    pe_rot = jnp.concatenate([-pe[..., MLA_ROPE // 2:], pe[..., :MLA_ROPE // 2]], axis=-1)
    zq = jnp.zeros((MLA_Q_RANK, MLA_HEADS, LANES - MLA_QK), w_uq.dtype)
    w1 = jnp.concatenate([nope, pe, zq], axis=-1).reshape(MLA_Q_RANK, MLA_HEADS * LANES)
    w2 = jnp.concatenate([jnp.zeros_like(nope), pe_rot, zq], axis=-1).reshape(MLA_Q_RANK, MLA_HEADS * LANES)
    hkv = w_ukv.reshape(MLA_KV_RANK, MLA_HEADS, MLA_NOPE + MLA_V)
    k_nope, v = hkv[..., :MLA_NOPE], hkv[..., MLA_NOPE:]
    zk = jnp.zeros((MLA_KV_RANK, MLA_HEADS, LANES - MLA_NOPE), w_ukv.dtype)
    wk = jnp.concatenate([k_nope, zk], axis=-1).reshape(MLA_KV_RANK, MLA_HEADS * LANES)
    zv = jnp.zeros((MLA_KV_RANK, MLA_HEADS, LANES - MLA_V), w_ukv.dtype)
    wv = jnp.concatenate([v, zv], axis=-1).reshape(MLA_KV_RANK, MLA_HEADS * LANES)
    place = np.zeros((LANES, MLA_HEADS * LANES), np.float32)
    ones = np.zeros((1, MLA_HEADS * LANES), np.float32)
    for h in range(MLA_HEADS):
        place[np.arange(MLA_ROPE), h * LANES + MLA_NOPE + np.arange(MLA_ROPE)] = 1.0
        ones[0, h * LANES + MLA_V] = 1.0
    return (w1.astype(BF16), w2.astype(BF16), wk.astype(BF16), wv.astype(BF16), jnp.asarray(place, BF16),
            jnp.asarray(ones, F32))


def _mla_q_kernel(c_ref, nw_ref, w1_ref, w2_ref, cos_ref, sin_ref, o_ref):
    cn = _rms(c_ref[...].astype(F32), nw_ref[...]).astype(BF16)
    cos = jnp.tile(cos_ref[...], (1, MLA_HEADS))
    sin = jnp.tile(sin_ref[...], (1, MLA_HEADS))
    q = _dot(cn, w1_ref[...]) * cos + _dot(cn, w2_ref[...]) * sin
    o_ref[...] = (q * (MLA_QK ** -0.5 * math.log2(math.e))).astype(o_ref.dtype)


def _mla_kv_kernel(c_ref, tail_ref, nw_ref, wk_ref, wv_ref, place_ref, ones_ref, cos_ref, sin_ref,
                   k_ref, v_ref):
    cn = _rms(c_ref[...].astype(F32), nw_ref[...]).astype(BF16)
    kr = tail_ref[...]
    half = MLA_ROPE // 2
    rot = jnp.where(_lane_lt(kr.shape, half), pltpu.roll(kr, LANES - half, 1), pltpu.roll(kr, half, 1))
    k_pe = (kr * cos_ref[...] + rot * sin_ref[...]).astype(BF16)
    k_ref[...] = (_dot(cn, wk_ref[...]) + _dot(k_pe, place_ref[...])).astype(k_ref.dtype)
    v_ref[...] = (_dot(cn, wv_ref[...]) + ones_ref[...]).astype(v_ref.dtype)


def mla_project(proj, tail, q_norm_w, kv_norm_w, w_uq, w_ukv, b, t, tm):
    n = b * t
    nb = t // tm
    w1, w2, wk, wv, place, ones = mla_weights(w_uq, w_ukv)
    q_cos, q_sin, k_cos, k_sin = mla_tables(t)
    wide = MLA_HEADS * LANES
    full = lambda shape: pl.BlockSpec(shape, lambda i: (0, 0))
    tab = pl.BlockSpec((tm, LANES), lambda i: (i % nb, 0))
    qf = pl.pallas_call(
        _mla_q_kernel,
        grid=(n // tm,),
        in_specs=[pl.BlockSpec((tm, SLAB), lambda i: (i, COL_CQ)), full((1, MLA_Q_RANK)),
                  full((MLA_Q_RANK, wide)), full((MLA_Q_RANK, wide)), tab, tab],
        out_specs=pl.BlockSpec((tm, wide), lambda i: (i, 0)),
        out_shape=jax.ShapeDtypeStruct((n, wide), BF16),
        compiler_params=_cparams(("parallel",)),
        name="mla_q_proj",
    )(proj, q_norm_w.reshape(1, MLA_Q_RANK), w1, w2, q_cos, q_sin)
    kf, vf = pl.pallas_call(
        _mla_kv_kernel,
        grid=(n // tm,),
        in_specs=[pl.BlockSpec((tm, SLAB), lambda i: (i, COL_CKV)),
                  pl.BlockSpec((tm, LANES), lambda i: (i, 0)), full((1, MLA_KV_RANK)),
                  full((MLA_KV_RANK, wide)), full((MLA_KV_RANK, wide)), full((LANES, wide)), full((1, wide)),
                  tab, tab],
        out_specs=[pl.BlockSpec((tm, wide), lambda i: (i, 0))] * 2,
        out_shape=[jax.ShapeDtypeStruct((n, wide), BF16)] * 2,
        compiler_params=_cparams(("parallel",)),
        name="mla_kv_proj",
    )(proj, tail, kv_norm_w.reshape(1, MLA_KV_RANK), wk, wv, place, ones, k_cos, k_sin)
    return qf, kf, vf


def _mla_attn_kernel(q_ref, k_ref, v_ref, *rest, t, tk, n_cast):
    cast_in, o_ref, cast_out = rest[:n_cast], rest[n_cast], rest[n_cast + 1:]
    for src_ref, dst_ref in zip(cast_in, cast_out):
        dst_ref[...] = src_ref[...].astype(dst_ref.dtype)
    tq = q_ref.shape[0]
    groups = [slice(hh * LANES, (hh + 1) * LANES) for hh in range(2)]
    qs = [q_ref[:, grp] for grp in groups]

    def scores(c):
        return [_dot_nt(q, k_ref[c * tk:(c + 1) * tk, grp]) for q, grp in zip(qs, groups)]

    n_chunks = t // tk
    ms = [jnp.full((tq, 1), -jnp.inf, F32)] * 2
    accs = [jnp.zeros((tq, LANES), F32)] * 2
    s_next = scores(0)
    for c in range(n_chunks):
        s_cur = s_next
        if c + 1 < n_chunks:
            s_next = scores(c + 1)
        for hh, grp in enumerate(groups):
            m_new = jnp.maximum(ms[hh], jnp.max(s_cur[hh], axis=-1, keepdims=True))
            p = jnp.exp2((s_cur[hh] - m_new).astype(BF16))
            accs[hh] = jnp.exp2(ms[hh] - m_new) * accs[hh] + _dot(p, v_ref[c * tk:(c + 1) * tk, grp])
            ms[hh] = m_new
    outs = [acc / acc[:, MLA_V:MLA_V + 1] for acc in accs]
    first = _lane_lt((tq, LANES), MLA_V)
    o_ref[...] = jnp.where(first, outs[0], pltpu.roll(outs[1], MLA_V, 1)).astype(o_ref.dtype)


def mla_cast_rows(w, b, t, tq):
    steps = b * N_PAIRS * (t // tq)
    rows = int(np.prod(w.shape[:-1]))
    per = rows // steps
    return per if rows % steps == 0 and per % 16 == 0 else None


def mla_attention(qf, kf, vf, b, t, tq, tk, cast=()):
    n = b * t
    nq = t // tq
    step = lambda i, p, s: ((i * N_PAIRS + p) * nq + s, 0)
    cast2d = [w.reshape(-1, w.shape[-1]) for w in cast]
    cast_specs = [pl.BlockSpec((mla_cast_rows(w, b, t, tq), w2.shape[1]), step) for w, w2 in zip(cast, cast2d)]
    outs = pl.pallas_call(
        functools.partial(_mla_attn_kernel, t=t, tk=tk, n_cast=len(cast)),
        grid=(b, N_PAIRS, nq),
        in_specs=[
            pl.BlockSpec((tq, 2 * LANES), lambda i, p, s: (i * nq + s, p)),
            pl.BlockSpec((t, 2 * LANES), lambda i, p, s: (i, p)),
            pl.BlockSpec((t, 2 * LANES), lambda i, p, s: (i, p)),
        ] + cast_specs,
        out_specs=[pl.BlockSpec((tq, HEAD_PAIR), lambda i, p, s: (i * nq + s, p))] + cast_specs,
        out_shape=[jax.ShapeDtypeStruct((n, SLAB), BF16)]
        + [jax.ShapeDtypeStruct(w2.shape, BF16) for w2 in cast2d],
        compiler_params=_cparams(("parallel", "parallel", "arbitrary")),
        name="mla_attention",
    )(qf, kf, vf, *cast2d)
    return outs[0], [o.reshape(w.shape) for o, w in zip(outs[1:], cast)]


def _in_proj_segments():
    sizes = (SSM_INNER, SSM_CONV_CH, 2 * SSM_HEADS, SLAB, SLAB, SLAB, MLA_Q_RANK, MLA_KV_RANK, MLA_ROPE,
             SLAB, SLAB, SLAB)
    off = [int(v) for v in np.concatenate([[0], np.cumsum(sizes)])]
    main = ((off[0], off[2]), (off[3], off[8]), (off[9], off[12]))
    tail = ((off[8], off[9]), (off[2], off[3]))
    return main, tail


def _in_proj_columns():
    main, tail = _in_proj_segments()
    cols = lambda segs: np.concatenate([np.arange(a, b) for a, b in segs])
    return cols(main), cols(tail)


def in_proj_weights(w_in_l):
    main, tail = _in_proj_segments()
    w_main = jnp.concatenate([w_in_l[:, a:b] for a, b in main], axis=1).astype(BF16)
    pad = jnp.zeros((D_MODEL, LANES - sum(b - a for a, b in tail)), w_in_l.dtype)
    w_tail = jnp.concatenate([w_in_l[:, a:b] for a, b in tail] + [pad], axis=1).astype(BF16)
    return w_main, w_tail


MLA_TQ = 1024
MLA_CASTS_PER_CALL = 2


def mixers(proj, tail, p, l, b, t, cast):
    xbc = conv_silu(proj, p["conv_w"][l], p["conv_b"][l], b, t)
    y_f, y_b = ssd_scan(xbc, tail, p["dt_bias"][l], p["a_log"][l], b, t)
    y_ssm = ssd_combine(y_f, y_b, xbc, proj, p["d_skip"][l], p["ssm_norm_w"][l], 1024)

    y_na = na_attention(proj, p["na_rpb"][l], b, t)

    qf, kf, vf = mla_project(proj, tail, p["mla_q_norm_w"][l], p["mla_kv_norm_w"][l],
                             p["mla_w_uq"][l], p["mla_w_ukv"][l], b, t, 512)
    y_mla, cast_out = mla_attention(qf, kf, vf, b, t, MLA_TQ, 512, cast)

    qkvs = rope_qkv(proj, b, t, 1024)
    outs, lses = zip(*[band_attention(qkv, w, d) for qkv, (w, d) in zip(qkvs, DIL_PAIRS)])
    y_dil = dil_combine(outs, lses, 1024)
    return (y_ssm, y_na, y_mla, y_dil), cast_out


def kernel(x, attn_norm_w, w_in, conv_w, conv_b, a_log, dt_bias, d_skip, ssm_norm_w, na_rpb,
           mla_q_norm_w, mla_kv_norm_w, mla_w_uq, mla_w_ukv, w_o, ffn_norm_w, ffn_w_gate, ffn_w_up,
           ffn_w_down, router_w, exp_w_gate, exp_w_up, exp_w_down, final_norm_w):
    b, t, _ = x.shape
    n = b * t
    depth = w_in.shape[0]
    p = dict(conv_w=conv_w, conv_b=conv_b, a_log=a_log, dt_bias=dt_bias, d_skip=d_skip,
             ssm_norm_w=ssm_norm_w, na_rpb=na_rpb, mla_q_norm_w=mla_q_norm_w,
             mla_kv_norm_w=mla_kv_norm_w, mla_w_uq=mla_w_uq, mla_w_ukv=mla_w_ukv)
    x = x.reshape(n, D_MODEL)
    cast_rows = 256
    w_o_b = cast_bf16(w_o, cast_rows)
    ffn_b = [cast_bf16(w, cast_rows) for w in (ffn_w_gate, ffn_w_up, ffn_w_down)]
    exp_f32 = [exp_w_gate, exp_w_up, exp_w_down]
    exp_b = [None] * len(exp_f32)
    pending = [k for k, w in enumerate(exp_f32) if mla_cast_rows(w, b, t, MLA_TQ) is not None]
    moe_tm = 512
    normed = False
    for l in range(depth):
        w_main, w_tail = in_proj_weights(w_in[l])
        proj, tail = in_proj(x, attn_norm_w[l], w_main, w_tail, 512, PROJ_MAIN)
        jobs, pending = pending[:MLA_CASTS_PER_CALL], pending[MLA_CASTS_PER_CALL:]
        mix, cast_out = mixers(proj, tail, p, l, b, t, [exp_f32[k] for k in jobs])
        for k, w_b in zip(jobs, cast_out):
            exp_b[k] = w_b
        x = out_proj(mix, w_o_b, l, x, 512, D_MODEL)
        j = l // 2
        if l % 2 == 0:
            x = ffn_dense(x, ffn_norm_w[l], *ffn_b, j, 1024, 512)
        else:
            pending = []
            exp_b = [cast_bf16(w, cast_rows) if w_b is None else w_b for w, w_b in zip(exp_f32, exp_b)]
            top_i, gates = moe_router(x, ffn_norm_w[l], router_w[j], 512)
            src, pos, tile_expert, tile_valid = moe_plan(top_i, moe_tm)
            y = moe_ffn(x, ffn_norm_w[l], *exp_b, j, src, tile_expert, tile_valid, moe_tm, 512)
            last = l == depth - 1
            x = moe_combine(x, gates, y, pos, final_norm_w if last else None, 512)
            normed = last
    if not normed:
        x = rmsnorm_rows(x, final_norm_w, 1024)
    return x.reshape(b, t, D_MODEL)
```

```python
import functools
import math

import numpy as np
import jax
import jax.numpy as jnp
from jax import lax
from jax.experimental import pallas as pl
from jax.experimental.pallas import tpu as pltpu

F32 = jnp.float32
BF16 = jnp.bfloat16

D_MODEL = 2048
GRID_W = 64
HEAD_DIM = 64
ROPE_THETA = 10000.0
NORM_EPS = 1e-6
NEG_INF = -1e30

SSM_HEADS = 8
SSM_HEAD_DIM = 64
SSM_INNER = SSM_HEADS * SSM_HEAD_DIM
SSM_GROUPS = 2
SSM_STATE = 128
SSM_CONV = 5
SSM_CHUNK = 128
SSD_CHUNKS_PER_STEP = 2
SSM_CONV_CH = SSM_INNER + 2 * SSM_GROUPS * SSM_STATE

NA_HEADS = 8
NA_WIN_ROWS = 8
NA_WIN_COLS = 16
NA_COL_BLOCK = 16
NA_KEY_COLS = 32
NA_ROWS_PER_STEP = 8

MLA_HEADS = 8
MLA_Q_RANK = 512
MLA_KV_RANK = 512
MLA_NOPE = 64
MLA_ROPE = 32
MLA_V = 64

DIL_HEADS = 8
DIL_PAIRS = ((128, 1), (512, 4), (2048, 16))
DIL_QBLOCK = 128
BAND_BLOCKS_PER_STEP = 4
LSE_LANES = 16

N_EXPERTS = 8
TOP_K = 2
SPLIT_PARTS = 3

LANES = 128
HEAD_PAIR = 2 * HEAD_DIM
N_PAIRS = 4
SLAB = 512

COL_Z, COL_XBC, COL_NAQ, COL_NAK, COL_NAV, COL_CQ, COL_CKV, COL_DLQ, COL_DLK, COL_DLV = (
    0, 1, 3, 4, 5, 6, 7, 8, 9, 10)
PROJ_MAIN = 11 * SLAB
TAIL_DT = 32

VMEM_LIMIT = 56 * 1024 * 1024


def _cparams(sem, vmem=VMEM_LIMIT):
    return pltpu.CompilerParams(dimension_semantics=sem, vmem_limit_bytes=vmem)


def _lane_lt(shape, bound, period=None):
    lane = lax.broadcasted_iota(jnp.int32, shape, len(shape) - 1)
    if period is not None:
        lane = lane % period
    return lane < bound


def _rms(x, w):
    ms = jnp.mean(x * x, axis=-1, keepdims=True)
    return x * lax.rsqrt(ms + NORM_EPS) * w


def _dot(a, b):
    return jnp.dot(a, b, preferred_element_type=F32)


def _dot_nt(a, b):
    return lax.dot_general(a, b, (((1,), (1,)), ((), ())), preferred_element_type=F32)


def _dot_tn(a, b):
    return lax.dot_general(a, b, (((0,), (0,)), ((), ())), preferred_element_type=F32)


def _cast_kernel(x_ref, o_ref):
    o_ref[...] = x_ref[...].astype(o_ref.dtype)


def cast_bf16(w, tr):
    shape = w.shape
    w2 = w.reshape(-1, shape[-1])
    r, c = w2.shape
    out = pl.pallas_call(
        _cast_kernel,
        grid=(r // tr,),
        in_specs=[pl.BlockSpec((tr, c), lambda i: (i, 0))],
        out_specs=pl.BlockSpec((tr, c), lambda i: (i, 0)),
        out_shape=jax.ShapeDtypeStruct((r, c), BF16),
        compiler_params=_cparams(("parallel",)),
        name="cast_bf16",
    )(w2)
    return out.reshape(shape)


def _in_proj_kernel(x_ref, nw_ref, w_ref, wt_ref, o_ref, t_ref, h_ref):
    @pl.when(pl.program_id(1) == 0)
    def _():
        h = _rms(x_ref[...], nw_ref[...]).astype(BF16)
        h_ref[...] = h
        t_ref[...] = _dot(h, wt_ref[...])

    o_ref[...] = _dot(h_ref[...], w_ref[...]).astype(o_ref.dtype)


def in_proj(x, nw, w_main, w_tail, tm, tn):
    n, k = x.shape
    nout = w_main.shape[1]
    w_mode = pl.Buffered(1) if tn == nout else None
    return pl.pallas_call(
        _in_proj_kernel,
        grid=(n // tm, nout // tn),
        in_specs=[
            pl.BlockSpec((tm, k), lambda i, j: (i, 0)),
            pl.BlockSpec((1, k), lambda i, j: (0, 0)),
            pl.BlockSpec((k, tn), lambda i, j: (0, j), pipeline_mode=w_mode),
            pl.BlockSpec((k, LANES), lambda i, j: (0, 0)),
        ],
        out_specs=[pl.BlockSpec((tm, tn), lambda i, j: (i, j)), pl.BlockSpec((tm, LANES), lambda i, j: (i, 0))],
        out_shape=[jax.ShapeDtypeStruct((n, nout), BF16), jax.ShapeDtypeStruct((n, LANES), F32)],
        scratch_shapes=[pltpu.VMEM((tm, k), BF16)],
        compiler_params=_cparams(("parallel", "arbitrary")),
        name="in_proj",
    )(x, nw.reshape(1, k), w_main, w_tail)


def _out_proj_kernel(a0_ref, a1_ref, a2_ref, a3_ref, w_ref, r_ref, o_ref):
    acc = r_ref[...]
    for s, a_ref in enumerate((a0_ref, a1_ref, a2_ref, a3_ref)):
        acc = acc + _dot(a_ref[...], w_ref[s * SLAB:(s + 1) * SLAB, :])
    o_ref[...] = acc


def out_proj(mix, w, layer, res, tm, tn):
    n = res.shape[0]
    return pl.pallas_call(
        _out_proj_kernel,
        grid=(n // tm, D_MODEL // tn),
        in_specs=[pl.BlockSpec((tm, SLAB), lambda i, j: (i, 0))] * 4 + [
            pl.BlockSpec((None, 4 * SLAB, tn), lambda i, j: (layer, 0, j)),
            pl.BlockSpec((tm, tn), lambda i, j: (i, j)),
        ],
        out_specs=pl.BlockSpec((tm, tn), lambda i, j: (i, j)),
        out_shape=jax.ShapeDtypeStruct((n, D_MODEL), F32),
        compiler_params=_cparams(("parallel", "arbitrary")),
        name="out_proj",
    )(*mix, w, res)


def _ffn_kernel(x_ref, nw_ref, wg_ref, wu_ref, wd_ref, o_ref, h_ref):
    @pl.when(pl.program_id(1) == 0)
    def _():
        x = x_ref[...]
        h_ref[...] = _rms(x, nw_ref[...]).astype(BF16)
        o_ref[...] = x

    _swiglu_rows(h_ref, wg_ref, wu_ref, wd_ref, o_ref)


FFN_ROW_CHUNK = 512


def _swiglu_rows(h_ref, wg_ref, wu_ref, wd_ref, o_ref):
    tm = h_ref.shape[0]
    for r0 in range(0, tm, FFN_ROW_CHUNK):
        rows = slice(r0, r0 + FFN_ROW_CHUNK)
        h = h_ref[rows, :]
        g = _dot(h, wg_ref[...])
        u = _dot(h, wu_ref[...])
        a = (g * jax.nn.sigmoid(g) * u).astype(BF16)
        o_ref[rows, :] += _dot(a, wd_ref[...])


def ffn_dense(x, nw, wg, wu, wd, layer, tm, tf):
    n = x.shape[0]
    d_ff = wg.shape[-1]
    return pl.pallas_call(
        _ffn_kernel,
        grid=(n // tm, d_ff // tf),
        in_specs=[
            pl.BlockSpec((tm, D_MODEL), lambda i, j: (i, 0)),
            pl.BlockSpec((1, D_MODEL), lambda i, j: (0, 0)),
            pl.BlockSpec((None, D_MODEL, tf), lambda i, j: (layer, 0, j)),
            pl.BlockSpec((None, D_MODEL, tf), lambda i, j: (layer, 0, j)),
            pl.BlockSpec((None, tf, D_MODEL), lambda i, j: (layer, j, 0)),
        ],
        out_specs=pl.BlockSpec((tm, D_MODEL), lambda i, j: (i, 0)),
        out_shape=jax.ShapeDtypeStruct((n, D_MODEL), F32),
        scratch_shapes=[pltpu.VMEM((tm, D_MODEL), BF16)],
        compiler_params=_cparams(("parallel", "arbitrary")),
        name="ffn_dense",
    )(x, nw.reshape(1, D_MODEL), wg, wu, wd)


def _router_kernel(x_ref, nw_ref, rw_ref, idx_ref, gate_ref):
    h = _rms(x_ref[...], nw_ref[...])
    acc = jnp.zeros((h.shape[0], LANES), F32)
    rem = h
    for _ in range(SPLIT_PARTS):
        part = rem.astype(BF16)
        acc = acc + _dot(part, rw_ref[...])
        rem = rem - part.astype(F32)
    logits = acc
    for k in range(1, SPLIT_PARTS):
        logits = logits + pltpu.roll(acc, LANES - k * N_EXPERTS, 1)
    lane = lax.broadcasted_iota(jnp.int32, logits.shape, 1)
    logits = jnp.where(lane < N_EXPERTS, logits, -jnp.inf)
    m1 = jnp.max(logits, axis=-1, keepdims=True)
    i1 = jnp.min(jnp.where(logits == m1, lane, LANES), axis=-1, keepdims=True)
    rest = jnp.where(lane == i1, -jnp.inf, logits)
    m2 = jnp.max(rest, axis=-1, keepdims=True)
    i2 = jnp.min(jnp.where(rest == m2, lane, LANES), axis=-1, keepdims=True)
    e2 = jnp.exp(m2 - m1)
    g1 = 1.0 / (1.0 + e2)
    g2 = e2 / (1.0 + e2)
    idx_ref[...] = jnp.where(lane == 0, i1, i2)[:, :TOP_K]
    gate_ref[...] = jnp.where(lane == 0, g1, g2)[:, :TOP_K]


def moe_router(x, nw, router_w, tm):
    n = x.shape[0]
    parts, rem = [], router_w
    for _ in range(SPLIT_PARTS):
        parts.append(rem.astype(BF16))
        rem = rem - parts[-1].astype(F32)
    pad = jnp.zeros((D_MODEL, LANES - SPLIT_PARTS * N_EXPERTS), BF16)
    rw = jnp.concatenate(parts + [pad], axis=1)
    return pl.pallas_call(
        _router_kernel,
        grid=(n // tm,),
        in_specs=[
            pl.BlockSpec((tm, D_MODEL), lambda i: (i, 0)),
            pl.BlockSpec((1, D_MODEL), lambda i: (0, 0)),
            pl.BlockSpec((D_MODEL, LANES), lambda i: (0, 0)),
        ],
        out_specs=[pl.BlockSpec((tm, TOP_K), lambda i: (i, 0)),
                   pl.BlockSpec((tm, TOP_K), lambda i: (i, 0))],
        out_shape=[jax.ShapeDtypeStruct((n, TOP_K), jnp.int32),
                   jax.ShapeDtypeStruct((n, TOP_K), F32)],
        compiler_params=_cparams(("parallel",)),
        name="moe_router",
    )(x, nw.reshape(1, D_MODEL), rw)


def moe_plan(top_i, tm):
    n = top_i.shape[0]
    flat_e = top_i.reshape(-1)
    onehot = (flat_e[:, None] == jnp.arange(N_EXPERTS, dtype=jnp.int32)[None, :]).astype(jnp.int32)
    csum = jnp.cumsum(onehot, axis=0)
    counts = csum[-1]
    rank = jnp.sum(onehot * csum, axis=1) - 1
    padded = ((counts + tm - 1) // tm) * tm
    pend = jnp.cumsum(padded)
    pstart = pend - padded
    pos = pstart[flat_e] + rank
    n_slots = n * TOP_K + N_EXPERTS * tm
    n_tiles = n_slots // tm
    src = jnp.zeros((n_slots,), jnp.int32).at[pos].set(jnp.arange(n * TOP_K, dtype=jnp.int32) // TOP_K)
    tile_start = jnp.arange(n_tiles, dtype=jnp.int32) * tm
    tile_expert = jnp.sum((tile_start[:, None] >= pend[None, :]).astype(jnp.int32), axis=1)
    tile_valid = (tile_start < pend[-1]).astype(jnp.int32)
    last_valid = jnp.maximum(pend[-1] // tm - 1, 0)
    tile_expert = jnp.where(tile_valid == 1, tile_expert, tile_expert[last_valid]).astype(jnp.int32)
    return src, pos.reshape(n, TOP_K).astype(jnp.int32), tile_expert, tile_valid


def _moe_ffn_kernel(te_ref, tv_ref, src_ref, nsrc_ref, x_hbm, nw_ref, wg_ref, wu_ref, wd_ref, y_ref,
                    xbuf, h_ref, sem, *, tm, rows_per_step):
    i = pl.program_id(0)
    j = pl.program_id(1)
    issued = xbuf.shape[0]
    valid = tv_ref[i] == 1
    has_rows = jnp.logical_or(i == 0, tv_ref[jnp.maximum(i - 1, 0)] == 1)

    def start_row(idx_ref, row):
        tok = idx_ref[0, 0, jnp.minimum(row, tm - 1)]
        pltpu.make_async_copy(x_hbm.at[pl.ds(tok, 1)], xbuf.at[pl.ds(row, 1)], sem).start(priority=1)

    @pl.when(j == 0)
    def _():
        y_ref[...] = jnp.zeros_like(y_ref)

    @pl.when(jnp.logical_and(j == 0, i == 0))
    def _():
        def start(r, c):
            start_row(src_ref, r)
            return c

        lax.fori_loop(0, issued, start, 0)

    @pl.when(jnp.logical_and(j == 0, has_rows))
    def _():
        pltpu.make_async_copy(x_hbm.at[pl.ds(0, issued)], xbuf.at[pl.ds(0, issued)], sem).wait()
        h_ref[...] = _rms(xbuf[0:tm, :], nw_ref[...]).astype(BF16)

    @pl.when(valid)
    def _():
        for r in range(rows_per_step):
            start_row(nsrc_ref, j * rows_per_step + r)
        _swiglu_rows(h_ref, wg_ref, wu_ref, wd_ref, y_ref)


def moe_ffn(x, nw, wg, wu, wd, layer, src, tile_expert, tile_valid, tm, tf):
    n_slots = src.shape[0]
    n_tiles = n_slots // tm + 1
    d_ff = wg.shape[-1]
    nf = d_ff // tf
    sublanes = 8
    rows_per_step = -(-tm // (nf * sublanes)) * sublanes
    buf_rows = rows_per_step * nf
    tile_expert = jnp.concatenate([tile_expert, tile_expert[-1:]])
    tile_valid = jnp.concatenate([tile_valid, jnp.zeros((1,), tile_valid.dtype)])
    src3 = jnp.concatenate([src, jnp.zeros((tm,), src.dtype)]).reshape(n_tiles, 1, tm)

    def wcol(i, j, te_ref, tv_ref):
        return (layer, te_ref[i], 0, jnp.where(tv_ref[i] == 1, j, nf - 1))

    def wrow(i, j, te_ref, tv_ref):
        return (layer, te_ref[i], jnp.where(tv_ref[i] == 1, j, nf - 1), 0)

    grid_spec = pltpu.PrefetchScalarGridSpec(
        num_scalar_prefetch=2,
        grid=(n_tiles, nf),
        in_specs=[
            pl.BlockSpec((1, 1, tm), lambda i, j, *_: (i, 0, 0), memory_space=pltpu.SMEM),
            pl.BlockSpec((1, 1, tm), lambda i, j, *_: (jnp.minimum(i + 1, n_tiles - 1), 0, 0),
                         memory_space=pltpu.SMEM),
            pl.BlockSpec(memory_space=pl.ANY),
            pl.BlockSpec((1, D_MODEL), lambda i, j, *_: (0, 0)),
            pl.BlockSpec((None, None, D_MODEL, tf), wcol),
            pl.BlockSpec((None, None, D_MODEL, tf), wcol),
            pl.BlockSpec((None, None, tf, D_MODEL), wrow),
        ],
        out_specs=pl.BlockSpec((tm, D_MODEL), lambda i, j, *_: (i, 0)),
        scratch_shapes=[pltpu.VMEM((buf_rows, D_MODEL), F32), pltpu.VMEM((tm, D_MODEL), BF16),
                        pltpu.SemaphoreType.DMA],
    )
    return pl.pallas_call(
        functools.partial(_moe_ffn_kernel, tm=tm, rows_per_step=rows_per_step),
        grid_spec=grid_spec,
        out_shape=jax.ShapeDtypeStruct((n_tiles * tm, D_MODEL), F32),
        compiler_params=_cparams(("arbitrary", "arbitrary")),
        name="moe_ffn",
    )(tile_expert, tile_valid, src3, src3, x, nw.reshape(1, D_MODEL), wg, wu, wd)


def _moe_combine_kernel(pos_ref, npos_ref, x_ref, gate_ref, y_hbm, fw_ref, o_ref, ybuf, sem, *, tm, final_norm):
    i = pl.program_id(0)
    n_tiles = pl.num_programs(0)
    cur = i % 2

    def row_copy(idx_ref, buf, r, k):
        slot = idx_ref[0, 0, r * TOP_K + k]
        return pltpu.make_async_copy(y_hbm.at[pl.ds(slot, 1)], ybuf.at[buf, k, pl.ds(r, 1)], sem.at[buf])

    def gather(idx_ref, buf):
        def start(r, c):
            for k in range(TOP_K):
                row_copy(idx_ref, buf, r, k).start(priority=k % 2)
            return c

        lax.fori_loop(0, tm, start, 0, unroll=8)

    @pl.when(i == 0)
    def _():
        gather(pos_ref, 0)

    @pl.when(i + 1 < n_tiles)
    def _():
        gather(npos_ref, 1 - cur)

    for k in range(TOP_K):
        pltpu.make_async_copy(y_hbm.at[pl.ds(0, tm)], ybuf.at[cur, k], sem.at[cur]).wait()
    gates = gate_ref[...]
    out = x_ref[...]
    for k in range(TOP_K):
        out = out + gates[:, k:k + 1] * ybuf[cur, k]
    if final_norm:
        out = _rms(out, fw_ref[...])
    o_ref[...] = out


def moe_combine(x, gates, y, pos, final_w, tm):
    n = x.shape[0]
    final_norm = final_w is not None
    fw = final_w if final_norm else jnp.ones((D_MODEL,), F32)
    n_tiles = n // tm
    pos3 = pos.reshape(n_tiles, 1, tm * TOP_K)
    return pl.pallas_call(
        functools.partial(_moe_combine_kernel, tm=tm, final_norm=final_norm),
        grid=(n_tiles,),
        in_specs=[
            pl.BlockSpec((1, 1, tm * TOP_K), lambda i: (i, 0, 0), memory_space=pltpu.SMEM),
            pl.BlockSpec((1, 1, tm * TOP_K), lambda i: (jnp.minimum(i + 1, n_tiles - 1), 0, 0),
                         memory_space=pltpu.SMEM),
            pl.BlockSpec((tm, D_MODEL), lambda i: (i, 0)),
            pl.BlockSpec((tm, TOP_K), lambda i: (i, 0)),
            pl.BlockSpec(memory_space=pl.ANY),
            pl.BlockSpec((1, D_MODEL), lambda i: (0, 0)),
        ],
        out_specs=pl.BlockSpec((tm, D_MODEL), lambda i: (i, 0)),
        out_shape=jax.ShapeDtypeStruct((n, D_MODEL), F32),
        scratch_shapes=[pltpu.VMEM((2, TOP_K, tm, D_MODEL), F32), pltpu.SemaphoreType.DMA((2,))],
        compiler_params=_cparams(("arbitrary",)),
        name="moe_combine",
    )(pos3, pos3, x, gates, y, fw.reshape(1, D_MODEL))


def _rmsnorm_kernel(x_ref, w_ref, o_ref):
    o_ref[...] = _rms(x_ref[...], w_ref[...])


def rmsnorm_rows(x, w, tm):
    n = x.shape[0]
    return pl.pallas_call(
        _rmsnorm_kernel,
        grid=(n // tm,),
        in_specs=[pl.BlockSpec((tm, D_MODEL), lambda i: (i, 0)),
                  pl.BlockSpec((1, D_MODEL), lambda i: (0, 0))],
        out_specs=pl.BlockSpec((tm, D_MODEL), lambda i: (i, 0)),
        out_shape=jax.ShapeDtypeStruct((n, D_MODEL), F32),
        compiler_params=_cparams(("parallel",)),
        name="final_norm",
    )(x, w.reshape(1, D_MODEL))


CONV_PAD = 8


def _conv_kernel(x_ref, w_ref, b_ref, o_ref, pad_ref, *, t):
    half = SSM_CONV // 2
    zeros = jnp.zeros((CONV_PAD, pad_ref.shape[1]), F32)
    pad_ref[0:CONV_PAD, :] = zeros
    pad_ref[CONV_PAD + t:CONV_PAD + t + CONV_PAD, :] = zeros
    pad_ref[CONV_PAD:CONV_PAD + t, :] = x_ref[...].astype(F32)
    acc = jnp.zeros(o_ref.shape, F32) + b_ref[...]
    for k in range(SSM_CONV):
        acc = acc + pad_ref[pl.ds(CONV_PAD - half + k, t), :] * w_ref[k:k + 1, :]
    o_ref[...] = (acc * jax.nn.sigmoid(acc)).astype(o_ref.dtype)


def conv_silu(proj, conv_w, conv_b, b, t):
    tc = 256
    nblk = SSM_CONV_CH // tc
    col0 = COL_XBC * SLAB // tc
    return pl.pallas_call(
        functools.partial(_conv_kernel, t=t),
        grid=(b, nblk),
        in_specs=[
            pl.BlockSpec((t, tc), lambda i, j: (i, col0 + j)),
            pl.BlockSpec((SSM_CONV, tc), lambda i, j: (0, j)),
            pl.BlockSpec((1, tc), lambda i, j: (0, j)),
        ],
        out_specs=pl.BlockSpec((t, tc), lambda i, j: (i, j)),
        out_shape=jax.ShapeDtypeStruct((b * t, SSM_CONV_CH), BF16),
        scratch_shapes=[pltpu.VMEM((t + 2 * CONV_PAD, tc), F32)],
        compiler_params=_cparams(("parallel", "parallel")),
        name="conv_silu",
    )(proj, conv_w, conv_b.reshape(1, SSM_CONV_CH))


def _ssd_stage1(xbc_ref, tail_ref, bias_ref, alog_ref, tri_ref, sel_ref, state_ref, direction):
    q = SSM_CHUNK
    dt = jax.nn.softplus(tail_ref[...] + bias_ref[...])
    da = dt * (-jnp.exp(alog_ref[...]))
    cs = jnp.dot(tri_ref[...], da, preferred_element_type=F32, precision=lax.Precision.HIGHEST)
    total = cs[q - 1:q, :]
    if direction == 0:
        e_out = cs
        e_in = total - cs
        e_seg = cs
    else:
        ex = cs - da
        e_out = total - ex
        e_in = ex
        e_seg = -ex
    dec_out_b = jnp.exp(e_out).astype(BF16)
    dec_in_dt_b = (jnp.exp(e_in) * dt).astype(BF16)
    pairs_per_group = N_PAIRS // SSM_GROUPS
    cbs, y_offs, in_scales = [], [], []
    for g in range(SSM_GROUPS):
        bm = xbc_ref[:, SSM_INNER + g * SSM_STATE:SSM_INNER + (g + 1) * SSM_STATE]
        cm = xbc_ref[:, SSM_INNER + (SSM_GROUPS + g) * SSM_STATE:SSM_INNER + (SSM_GROUPS + g + 1) * SSM_STATE]
        cbs.append(_dot_nt(cm, bm))
        for pair in range(g * pairs_per_group, (g + 1) * pairs_per_group):
            sel = sel_ref[direction, pair]
            y_offs.append(_dot(dec_out_b, sel) * _dot(cm, state_ref[pair].astype(BF16)))
            in_scales.append(_dot(dec_in_dt_b, sel))
    return dict(e_seg=e_seg, e_seg_t=jnp.transpose(e_seg), dt_t=jnp.transpose(dt), dec_tot=jnp.exp(total),
                cbs=cbs, y_offs=y_offs, in_scales=in_scales)


def _ssd_stage2(ctx, xbc_ref, y_ref, state_ref, direction):
    q = SSM_CHUNK
    row = lax.broadcasted_iota(jnp.int32, (q, q), 0)
    col = lax.broadcasted_iota(jnp.int32, (q, q), 1)
    keep = (row >= col) if direction == 0 else (col >= row)
    first_half = _lane_lt((q, HEAD_PAIR), HEAD_DIM)
    pairs_per_group = N_PAIRS // SSM_GROUPS
    for pair in range(N_PAIRS):
        g = pair // pairs_per_group
        bm = xbc_ref[:, SSM_INNER + g * SSM_STATE:SSM_INNER + (g + 1) * SSM_STATE]
        xs_pair = xbc_ref[:, pair * HEAD_PAIR:(pair + 1) * HEAD_PAIR]
        ys, decs = [], []
        for hh in range(2):
            lane = TAIL_DT + direction * SSM_HEADS + pair * 2 + hh
            seg = ctx["e_seg"][:, lane:lane + 1] - ctx["e_seg_t"][lane:lane + 1, :]
            lmat = jnp.where(keep, jnp.exp(seg), 0.0)
            w = (ctx["cbs"][g] * lmat * ctx["dt_t"][lane:lane + 1, :]).astype(BF16)
            ys.append(_dot(w, xs_pair))
            decs.append(ctx["dec_tot"][:, lane:lane + 1])
        y_ref[:, pair * HEAD_PAIR:(pair + 1) * HEAD_PAIR] = jnp.where(first_half, ys[0], ys[1]) + ctx["y_offs"][pair]
        st = state_ref[pair]
        first_half_s = _lane_lt(st.shape, HEAD_DIM)
        state_ref[pair] = (st * jnp.where(first_half_s, decs[0], decs[1])
                           + _dot_tn(bm, (xs_pair * ctx["in_scales"][pair]).astype(BF16)))


def _ssd_kernel(xbc_f_ref, tail_f_ref, xbc_b_ref, tail_b_ref, bias_ref, alog_ref, tri_ref, sel_ref,
                y_f_ref, y_b_ref, state_ref):
    @pl.when(pl.program_id(1) == 0)
    def _():
        state_ref[...] = jnp.zeros_like(state_ref)

    q = SSM_CHUNK
    for k in range(SSD_CHUNKS_PER_STEP):
        ins = []
        for d, refs in enumerate(((xbc_f_ref, tail_f_ref, y_f_ref), (xbc_b_ref, tail_b_ref, y_b_ref))):
            sub = k if d == 0 else SSD_CHUNKS_PER_STEP - 1 - k
            ins.append(tuple(r.at[sub * q:(sub + 1) * q] for r in refs))
        ctxs = [_ssd_stage1(xbc_ref, tail_ref, bias_ref, alog_ref, tri_ref, sel_ref, state_ref.at[d], d)
                for d, (xbc_ref, tail_ref, _) in enumerate(ins)]
        for d, (xbc_ref, _, y_ref) in enumerate(ins):
            _ssd_stage2(ctxs[d], xbc_ref, y_ref, state_ref.at[d], d)


def _ssd_lane_selectors():
    sel = np.zeros((2, N_PAIRS, LANES, HEAD_PAIR), np.float32)
    for d in range(2):
        for pair in range(N_PAIRS):
            for hh in range(2):
                sel[d, pair, TAIL_DT + d * SSM_HEADS + pair * 2 + hh, hh * HEAD_DIM:(hh + 1) * HEAD_DIM] = 1.0
    return jnp.asarray(sel, BF16)


def ssd_scan(xbc, tail, dt_bias, a_log, b, t):
    q = SSM_CHUNK
    rows = q * SSD_CHUNKS_PER_STEP
    nc = t // rows
    bias_row = jnp.zeros((1, LANES), F32).at[0, TAIL_DT:TAIL_DT + 2 * SSM_HEADS].set(dt_bias.reshape(-1))
    alog_row = jnp.zeros((1, LANES), F32).at[0, TAIL_DT:TAIL_DT + 2 * SSM_HEADS].set(a_log.reshape(-1))
    tri = jnp.asarray(np.tril(np.ones((q, q), np.float32)))
    fwd = lambda i, c: (i * nc + c, 0)
    bwd = lambda i, c: (i * nc + nc - 1 - c, 0)
    const = lambda i, c: (0, 0)
    return pl.pallas_call(
        _ssd_kernel,
        grid=(b, nc),
        in_specs=[
            pl.BlockSpec((rows, SSM_CONV_CH), fwd), pl.BlockSpec((rows, LANES), fwd),
            pl.BlockSpec((rows, SSM_CONV_CH), bwd), pl.BlockSpec((rows, LANES), bwd),
            pl.BlockSpec((1, LANES), const), pl.BlockSpec((1, LANES), const), pl.BlockSpec((q, q), const),
            pl.BlockSpec((2, N_PAIRS, LANES, HEAD_PAIR), lambda i, c: (0, 0, 0, 0)),
        ],
        out_specs=[pl.BlockSpec((rows, SSM_INNER), fwd), pl.BlockSpec((rows, SSM_INNER), bwd)],
        out_shape=[jax.ShapeDtypeStruct((b * t, SSM_INNER), F32)] * 2,
        scratch_shapes=[pltpu.VMEM((2, N_PAIRS, SSM_STATE, HEAD_PAIR), F32)],
        compiler_params=_cparams(("parallel", "arbitrary")),
        name="ssd_scan",
    )(xbc, tail, xbc, tail, bias_row, alog_row, tri, _ssd_lane_selectors())


def _ssd_combine_kernel(yf_ref, yb_ref, xs_ref, z_ref, d_ref, nw_ref, o_ref):
    y = yf_ref[...] + yb_ref[...] + xs_ref[...].astype(F32) * d_ref[...]
    z = z_ref[...].astype(F32)
    o_ref[...] = _rms(y * (z * jax.nn.sigmoid(z)), nw_ref[...]).astype(o_ref.dtype)


def ssd_combine(y_f, y_b, xbc, proj, d_skip, norm_w, tm):
    n = y_f.shape[0]
    d_row = jnp.repeat(d_skip, SSM_HEAD_DIM).reshape(1, SSM_INNER)
    row = lambda i: (i, 0)
    return pl.pallas_call(
        _ssd_combine_kernel,
        grid=(n // tm,),
        in_specs=[
            pl.BlockSpec((tm, SSM_INNER), row),
            pl.BlockSpec((tm, SSM_INNER), row),
            pl.BlockSpec((tm, SSM_INNER), row),
            pl.BlockSpec((tm, SLAB), lambda i: (i, COL_Z)),
            pl.BlockSpec((1, SSM_INNER), lambda i: (0, 0)),
            pl.BlockSpec((1, SSM_INNER), lambda i: (0, 0)),
        ],
        out_specs=pl.BlockSpec((tm, SSM_INNER), row),
        out_shape=jax.ShapeDtypeStruct((n, SSM_INNER), BF16),
        compiler_params=_cparams(("parallel",)),
        name="ssd_combine",
    )(y_f, y_b, xbc, proj, d_row, norm_w.reshape(1, SSM_INNER))


def _pair_scores(q2, k2):
    first_q = _lane_lt(q2.shape, HEAD_DIM)
    zero = jnp.zeros_like(q2)
    qs = jnp.concatenate([jnp.where(first_q, q2, zero), jnp.where(first_q, zero, q2)], axis=0)
    return _dot_nt(qs, k2)


def _pair_attend(s, v2):
    tq = s.shape[0] // 2
    m = jnp.max(s, axis=-1, keepdims=True)
    p = jnp.exp(s - m)
    l = jnp.sum(p, axis=-1, keepdims=True)
    o = _dot(p.astype(BF16), v2) / l
    lse = m + jnp.log(l)
    first_o = _lane_lt((tq, HEAD_PAIR), HEAD_DIM)
    return jnp.where(first_o, o[:tq], o[tq:]), (lse[:tq], lse[tq:])


def na_bias_tables(rpb, rows):
    kr = min(NA_WIN_ROWS, rows)
    qc = np.arange(GRID_W)
    kc = np.arange(GRID_W)
    q_start = np.clip(qc - NA_WIN_COLS // 2, 0, GRID_W - NA_WIN_COLS)
    col_in = (kc[None, :] >= q_start[:, None]) & (kc[None, :] < q_start[:, None] + NA_WIN_COLS)
    col_off = np.clip(kc[None, :] - qc[:, None] + NA_WIN_COLS - 1, 0, 2 * NA_WIN_COLS - 2)
    onehot = (col_off[None] == np.arange(2 * NA_WIN_COLS - 1)[:, None, None]).astype(np.float32)
    expanded = jnp.einsum("hrc,cqk->hqrk", rpb, jnp.asarray(onehot), precision=lax.Precision.HIGHEST)
    expanded = jnp.where(jnp.asarray(col_in)[None, :, None, :], expanded, NEG_INF)

    def table(r):
        row_start = int(np.clip(r - kr // 2, 0, rows - kr))
        ro0 = row_start - r + NA_WIN_ROWS - 1
        return expanded[:, :, ro0:ro0 + kr, :].reshape(N_PAIRS, 2 * GRID_W, kr * GRID_W)

    rs = NA_ROWS_PER_STEP
    lo = [table(r) for r in range(rs)]
    mid = [table(min(rs, rows - 1))] * rs
    hi = [table(r) for r in range(rows - rs, rows)]
    return jnp.stack([jnp.stack(lo), jnp.stack(mid), jnp.stack(hi)])


def _na_kernel(q_ref, k_ref, v_ref, bias_ref, o_ref, *, rows, kr):
    step = pl.program_id(1)
    rs = NA_ROWS_PER_STEP
    for rr in range(rs):
        r = step * rs + rr
        row_start = jnp.clip(r - kr // 2, 0, rows - kr)
        k0 = pl.multiple_of(row_start * GRID_W, GRID_W)
        pair_cols = [slice(pair * HEAD_PAIR, (pair + 1) * HEAD_PAIR) for pair in range(N_PAIRS)]
        scores = []
        for pair, cols in enumerate(pair_cols):
            q2 = q_ref[rr * GRID_W:(rr + 1) * GRID_W, cols] * jnp.asarray(HEAD_DIM ** -0.5, BF16)
            scores.append(_pair_scores(q2, k_ref[pl.ds(k0, kr * GRID_W), cols]) + bias_ref[0, rr, pair])
        for s, cols in zip(scores, pair_cols):
            o, _ = _pair_attend(s, v_ref[pl.ds(k0, kr * GRID_W), cols])
            o_ref[rr * GRID_W:(rr + 1) * GRID_W, cols] = o.astype(o_ref.dtype)


def na_attention(proj, rpb, b, t):
    rows = t // GRID_W
    kr = min(NA_WIN_ROWS, rows)
    rs = NA_ROWS_PER_STEP
    nsteps = rows // rs
    bias = na_bias_tables(rpb, rows)

    def kind(i, s):
        return jnp.where(s == 0, 0, jnp.where(s == nsteps - 1, 2, 1))

    return pl.pallas_call(
        functools.partial(_na_kernel, rows=rows, kr=kr),
        grid=(b, nsteps),
        in_specs=[
            pl.BlockSpec((rs * GRID_W, SLAB), lambda i, s: (i * nsteps + s, COL_NAQ)),
            pl.BlockSpec((t, SLAB), lambda i, s: (i, COL_NAK)),
            pl.BlockSpec((t, SLAB), lambda i, s: (i, COL_NAV)),
            pl.BlockSpec((1, rs, N_PAIRS, 2 * GRID_W, kr * GRID_W), lambda i, s: (kind(i, s), 0, 0, 0, 0)),
        ],
        out_specs=pl.BlockSpec((rs * GRID_W, SLAB), lambda i, s: (i * nsteps + s, 0)),
        out_shape=jax.ShapeDtypeStruct((b * t, SLAB), BF16),
        compiler_params=_cparams(("parallel", "arbitrary")),
        name="na_attention",
    )(proj, proj, proj, bias)


def _rope_angles(t, d):
    inv = ROPE_THETA ** (-np.arange(0, d, 2, dtype=np.float32) / d)
    return np.arange(t, dtype=np.float32)[:, None] * inv[None, :]


def rope_tables_pair(t):
    ang = _rope_angles(t, HEAD_DIM)
    cos = np.tile(np.cos(ang), (1, 4))
    sin = np.tile(np.concatenate([-np.sin(ang), np.sin(ang)], axis=1), (1, 2))
    return jnp.asarray(cos, F32), jnp.asarray(sin, F32)


FOLD_CHUNK = 256
FOLD_DILS = tuple(d for _, d in DIL_PAIRS if d > 1)


def fold_permutation(dil):
    per = FOLD_CHUNK // dil
    perm = np.zeros((FOLD_CHUNK, FOLD_CHUNK), np.float32)
    dst = np.arange(FOLD_CHUNK)
    perm[dst, (dst % per) * dil + dst // per] = 1.0
    return jnp.asarray(perm, BF16)


def _rope_qkv_kernel(x_ref, v_ref, cos_ref, sin_ref, *rest):
    nd = len(FOLD_DILS)
    perm_refs, o_ref, fold_refs = rest[:nd], rest[nd], rest[nd + 1:]
    cos = cos_ref[...]
    sin = sin_ref[...]
    half = HEAD_DIM // 2
    for c in range(x_ref.shape[1] // LANES):
        x = x_ref[:, c * LANES:(c + 1) * LANES].astype(F32)
        rot = jnp.where(_lane_lt(x.shape, half, HEAD_DIM),
                        pltpu.roll(x, LANES - half, 1), pltpu.roll(x, half, 1))
        y = x * cos + rot * sin
        if c < N_PAIRS:
            y = y * (HEAD_DIM ** -0.5)
        o_ref[0, :, c * LANES:(c + 1) * LANES] = y.astype(o_ref.dtype)
    o_ref[0, :, 2 * SLAB:3 * SLAB] = v_ref[...]
    tm = x_ref.shape[0]
    for dil, perm_ref, f_ref in zip(FOLD_DILS, perm_refs, fold_refs):
        per = FOLD_CHUNK // dil
        for c in range(tm // FOLD_CHUNK):
            folded = _dot(perm_ref[...], o_ref[0, c * FOLD_CHUNK:(c + 1) * FOLD_CHUNK, :]).astype(f_ref.dtype)
            for p in range(dil):
                f_ref[p, c * per:(c + 1) * per, :] = folded[p * per:(p + 1) * per, :]


def rope_qkv(proj, b, t, tm):
    n = b * t
    cos, sin = rope_tables_pair(t)
    nb = t // tm
    fold_spec = lambda d: pl.BlockSpec((None, d, tm // d, 3 * SLAB), lambda i: (i // nb, 0, i % nb, 0))
    outs = pl.pallas_call(
        _rope_qkv_kernel,
        grid=(n // tm,),
        in_specs=[
            pl.BlockSpec((tm, 2 * SLAB), lambda i: (i, COL_DLQ // 2)),
            pl.BlockSpec((tm, SLAB), lambda i: (i, COL_DLV)),
            pl.BlockSpec((tm, LANES), lambda i: (i % nb, 0)),
            pl.BlockSpec((tm, LANES), lambda i: (i % nb, 0)),
        ] + [pl.BlockSpec((FOLD_CHUNK, FOLD_CHUNK), lambda i: (0, 0))] * len(FOLD_DILS),
        out_specs=[fold_spec(1)] + [fold_spec(d) for d in FOLD_DILS],
        out_shape=[jax.ShapeDtypeStruct((b, d, t // d, 3 * SLAB), BF16) for d in (1,) + FOLD_DILS],
        compiler_params=_cparams(("parallel",)),
        name="rope_qkv",
    )(proj, proj, cos, sin, *[fold_permutation(d) for d in FOLD_DILS])
    by_dil = dict(zip((1,) + FOLD_DILS, outs))
    return [by_dil[d] for _, d in DIL_PAIRS]


def _band_kernel(q_ref, k_ref, v_ref, o_ref, lse_ref, *, sub, half, span):
    tq = DIL_QBLOCK
    blocks = q_ref.shape[0] // tq
    for blk in range(blocks):
        qb = pl.program_id(2) * blocks + blk
        rows = slice(blk * tq, (blk + 1) * tq)
        start = jnp.clip(qb * tq - half, 0, sub - span)
        start = pl.multiple_of(start, half)
        q_pos = qb * tq + lax.broadcasted_iota(jnp.int32, (2 * tq, span), 0) % tq
        k_pos = start + lax.broadcasted_iota(jnp.int32, (2 * tq, span), 1)
        valid = jnp.abs(k_pos - q_pos) <= half
        pair_cols = [slice(pair * HEAD_PAIR, (pair + 1) * HEAD_PAIR) for pair in range(N_PAIRS)]
        scores = [jnp.where(valid, _pair_scores(q_ref[rows, cols], k_ref[pl.ds(start, span), cols]), NEG_INF)
                  for cols in pair_cols]
        head_slot = lax.broadcasted_iota(jnp.int32, (tq, LANES), 1) // LSE_LANES
        lse_all = jnp.zeros((tq, LANES), F32)
        for pair, (s, cols) in enumerate(zip(scores, pair_cols)):
            o, lse_heads = _pair_attend(s, v_ref[pl.ds(start, span), cols])
            o_ref[rows, cols] = o.astype(o_ref.dtype)
            for hh, lse in enumerate(lse_heads):
                lse_all = jnp.where(head_slot == pair * 2 + hh, lse, lse_all)
        lse_ref[rows, :] = lse_all


def band_attention(qkv, window, dil):
    b, _, sub, _ = qkv.shape
    half = window // (2 * dil)
    span = DIL_QBLOCK + 2 * half
    tq = DIL_QBLOCK * min(BAND_BLOCKS_PER_STEP, sub // DIL_QBLOCK)
    nqb = sub // tq
    return pl.pallas_call(
        functools.partial(_band_kernel, sub=sub, half=half, span=span),
        grid=(b, dil, nqb),
        in_specs=[
            pl.BlockSpec((None, None, tq, SLAB), lambda i, p, s: (i, p, s, 0)),
            pl.BlockSpec((None, None, sub, SLAB), lambda i, p, s: (i, p, 0, 1)),
            pl.BlockSpec((None, None, sub, SLAB), lambda i, p, s: (i, p, 0, 2)),
        ],
        out_specs=[pl.BlockSpec((None, None, tq, SLAB), lambda i, p, s: (i, p, s, 0)),
                   pl.BlockSpec((None, None, tq, LANES), lambda i, p, s: (i, p, s, 0))],
        out_shape=[jax.ShapeDtypeStruct((b, dil, sub, SLAB), BF16), jax.ShapeDtypeStruct((b, dil, sub, LANES), F32)],
        compiler_params=_cparams(("parallel", "parallel", "arbitrary")),
        name="band_attention_d%d" % dil,
    )(qkv, qkv, qkv)


def _dil_combine_kernel(*refs):
    nbr = len(DIL_PAIRS)
    o_refs, l_refs, expand_ref, out_ref = refs[:nbr], refs[nbr:2 * nbr], refs[2 * nbr], refs[2 * nbr + 1]
    scratch = iter(refs[2 * nbr + 2:])

    def token_order(ref):
        dil, per, width = ref.shape
        if dil == 1:
            return ref[0].astype(F32)
        buf = next(scratch)
        for p in range(dil):
            for c in range(width // LANES):
                buf[c, pl.ds(p, per, stride=dil), :] = ref[p, :, c * LANES:(c + 1) * LANES].astype(F32)
        return jnp.concatenate([buf[c] for c in range(width // LANES)], axis=1)

    def head_lanes(lse):
        full = jnp.zeros((lse.shape[0], SLAB), F32)
        rem = lse
        for _ in range(SPLIT_PARTS):
            part = rem.astype(BF16)
            full = full + _dot(part, expand_ref[...])
            rem = rem - part.astype(F32)
        return full

    os = [token_order(r) for r in o_refs]
    lses = [head_lanes(token_order(r)) for r in l_refs]
    m = functools.reduce(jnp.maximum, lses)
    ws = [jnp.exp(l - m) for l in lses]
    den = functools.reduce(jnp.add, ws)
    acc = functools.reduce(jnp.add, [(w / den) * o for w, o in zip(ws, os)])
    out_ref[...] = acc.astype(out_ref.dtype)


def dil_combine(outs, lses, tm):
    b, _, t, _ = outs[0].shape
    n = b * t
    nb = t // tm
    spec = lambda a: pl.BlockSpec((None, a.shape[1], tm // a.shape[1], a.shape[3]),
                                  lambda i: (i // nb, 0, i % nb, 0))
    arrays = tuple(outs) + tuple(lses)
    expand = np.zeros((LANES, SLAB), np.float32)
    for h in range(DIL_HEADS):
        expand[h * LSE_LANES, h * HEAD_DIM:(h + 1) * HEAD_DIM] = 1.0
    return pl.pallas_call(
        _dil_combine_kernel,
        grid=(n // tm,),
        in_specs=[spec(a) for a in arrays] + [pl.BlockSpec((LANES, SLAB), lambda i: (0, 0))],
        out_specs=pl.BlockSpec((tm, SLAB), lambda i: (i, 0)),
        out_shape=jax.ShapeDtypeStruct((n, SLAB), BF16),
        scratch_shapes=[pltpu.VMEM((a.shape[3] // LANES, tm, LANES), F32) for a in arrays if a.shape[1] > 1],
        compiler_params=_cparams(("parallel",)),
        name="dil_combine",
    )(*arrays, jnp.asarray(expand, BF16))


MLA_QK = MLA_NOPE + MLA_ROPE


def mla_tables(t):
    ang = _rope_angles(t, MLA_ROPE)
    cos2 = np.concatenate([np.cos(ang), np.cos(ang)], axis=1)
    sin2 = np.concatenate([np.sin(ang), np.sin(ang)], axis=1)
    z = lambda w: np.zeros((t, w), np.float32)
    q_cos = np.concatenate([np.ones((t, MLA_NOPE), np.float32), cos2, z(LANES - MLA_QK)], axis=1)
    q_sin = np.concatenate([z(MLA_NOPE), sin2, z(LANES - MLA_QK)], axis=1)
    k_cos = np.concatenate([cos2, z(LANES - MLA_ROPE)], axis=1)
    k_sin = np.concatenate([-sin2[:, :MLA_ROPE // 2], sin2[:, MLA_ROPE // 2:], z(LANES - MLA_ROPE)], axis=1)
    return tuple(jnp.asarray(a, F32) for a in (q_cos, q_sin, k_cos, k_sin))


def mla_weights(w_uq, w_ukv):
    hq = w_uq.reshape(MLA_Q_RANK, MLA_HEADS, MLA_QK)
    nope, pe = hq[..., :MLA_NOPE], hq[..., MLA_NOPE:]
    pe_rot = jnp.concatenate([-pe[..., MLA_ROPE // 2:], pe[..., :MLA_ROPE // 2]], axis=-1)
    zq = jnp.zeros((MLA_Q_RANK, MLA_HEADS, LANES - MLA_QK), w_uq.dtype)
    w1 = jnp.concatenate([nope, pe, zq], axis=-1).reshape(MLA_Q_RANK, MLA_HEADS * LANES)
    w2 = jnp.concatenate([jnp.zeros_like(nope), pe_rot, zq], axis=-1).reshape(MLA_Q_RANK, MLA_HEADS * LANES)
    hkv = w_ukv.reshape(MLA_KV_RANK, MLA_HEADS, MLA_NOPE + MLA_V)
    k_nope, v = hkv[..., :MLA_NOPE], hkv[..., MLA_NOPE:]
    zk = jnp.zeros((MLA_KV_RANK, MLA_HEADS, LANES - MLA_NOPE), w_ukv.dtype)
    wk = jnp.concatenate([k_nope, zk], axis=-1).reshape(MLA_KV_RANK, MLA_HEADS * LANES)
    zv = jnp.zeros((MLA_KV_RANK, MLA_HEADS, LANES - MLA_V), w_ukv.dtype)
    wv = jnp.concatenate([v, zv], axis=-1).reshape(MLA_KV_RANK, MLA_HEADS * LANES)
    place = np.zeros((LANES, MLA_HEADS * LANES), np.float32)
    ones = np.zeros((1, MLA_HEADS * LANES), np.float32)
    for h in range(MLA_HEADS):
        place[np.arange(MLA_ROPE), h * LANES + MLA_NOPE + np.arange(MLA_ROPE)] = 1.0
        ones[0, h * LANES + MLA_V] = 1.0
    return (w1.astype(BF16), w2.astype(BF16), wk.astype(BF16), wv.astype(BF16), jnp.asarray(place, BF16),
            jnp.asarray(ones, F32))


def _mla_q_kernel(c_ref, nw_ref, w1_ref, w2_ref, cos_ref, sin_ref, o_ref):
    cn = _rms(c_ref[...].astype(F32), nw_ref[...]).astype(BF16)
    cos = jnp.tile(cos_ref[...], (1, MLA_HEADS))
    sin = jnp.tile(sin_ref[...], (1, MLA_HEADS))
    q = _dot(cn, w1_ref[...]) * cos + _dot(cn, w2_ref[...]) * sin
    o_ref[...] = (q * (MLA_QK ** -0.5 * math.log2(math.e))).astype(o_ref.dtype)


def _mla_kv_kernel(c_ref, tail_ref, nw_ref, wk_ref, wv_ref, place_ref, ones_ref, cos_ref, sin_ref,
                   k_ref, v_ref):
    cn = _rms(c_ref[...].astype(F32), nw_ref[...]).astype(BF16)
    kr = tail_ref[...]
    half = MLA_ROPE // 2
    rot = jnp.where(_lane_lt(kr.shape, half), pltpu.roll(kr, LANES - half, 1), pltpu.roll(kr, half, 1))
    k_pe = (kr * cos_ref[...] + rot * sin_ref[...]).astype(BF16)
    k_ref[...] = (_dot(cn, wk_ref[...]) + _dot(k_pe, place_ref[...])).astype(k_ref.dtype)
    v_ref[...] = (_dot(cn, wv_ref[...]) + ones_ref[...]).astype(v_ref.dtype)


def mla_project(proj, tail, q_norm_w, kv_norm_w, w_uq, w_ukv, b, t, tm):
    n = b * t
    nb = t // tm
    w1, w2, wk, wv, place, ones = mla_weights(w_uq, w_ukv)
    q_cos, q_sin, k_cos, k_sin = mla_tables(t)
    wide = MLA_HEADS * LANES
    full = lambda shape: pl.BlockSpec(shape, lambda i: (0, 0))
    tab = pl.BlockSpec((tm, LANES), lambda i: (i % nb, 0))
    qf = pl.pallas_call(
        _mla_q_kernel,
        grid=(n // tm,),
        in_specs=[pl.BlockSpec((tm, SLAB), lambda i: (i, COL_CQ)), full((1, MLA_Q_RANK)),
                  full((MLA_Q_RANK, wide)), full((MLA_Q_RANK, wide)), tab, tab],
        out_specs=pl.BlockSpec((tm, wide), lambda i: (i, 0)),
        out_shape=jax.ShapeDtypeStruct((n, wide), BF16),
        compiler_params=_cparams(("parallel",)),
        name="mla_q_proj",
    )(proj, q_norm_w.reshape(1, MLA_Q_RANK), w1, w2, q_cos, q_sin)
    kf, vf = pl.pallas_call(
        _mla_kv_kernel,
        grid=(n // tm,),
        in_specs=[pl.BlockSpec((tm, SLAB), lambda i: (i, COL_CKV)),
                  pl.BlockSpec((tm, LANES), lambda i: (i, 0)), full((1, MLA_KV_RANK)),
                  full((MLA_KV_RANK, wide)), full((MLA_KV_RANK, wide)), full((LANES, wide)), full((1, wide)),
                  tab, tab],
        out_specs=[pl.BlockSpec((tm, wide), lambda i: (i, 0))] * 2,
        out_shape=[jax.ShapeDtypeStruct((n, wide), BF16)] * 2,
        compiler_params=_cparams(("parallel",)),
        name="mla_kv_proj",
    )(proj, tail, kv_norm_w.reshape(1, MLA_KV_RANK), wk, wv, place, ones, k_cos, k_sin)
    return qf, kf, vf


def _mla_attn_kernel(q_ref, k_ref, v_ref, *rest, t, tk, n_cast):
    cast_in, o_ref, cast_out = rest[:n_cast], rest[n_cast], rest[n_cast + 1:]
    for src_ref, dst_ref in zip(cast_in, cast_out):
        dst_ref[...] = src_ref[...].astype(dst_ref.dtype)
    tq = q_ref.shape[0]
    groups = [slice(hh * LANES, (hh + 1) * LANES) for hh in range(2)]
    qs = [q_ref[:, grp] for grp in groups]

    def scores(c):
        return [_dot_nt(q, k_ref[c * tk:(c + 1) * tk, grp]) for q, grp in zip(qs, groups)]

    n_chunks = t // tk
    ms = [jnp.full((tq, 1), -jnp.inf, F32)] * 2
    accs = [jnp.zeros((tq, LANES), F32)] * 2
    s_next = scores(0)
    for c in range(n_chunks):
        s_cur = s_next
        if c + 1 < n_chunks:
            s_next = scores(c + 1)
        for hh, grp in enumerate(groups):
            m_new = jnp.maximum(ms[hh], jnp.max(s_cur[hh], axis=-1, keepdims=True))
            p = jnp.exp2((s_cur[hh] - m_new).astype(BF16))
            accs[hh] = jnp.exp2(ms[hh] - m_new) * accs[hh] + _dot(p, v_ref[c * tk:(c + 1) * tk, grp])
            ms[hh] = m_new
    outs = [acc / acc[:, MLA_V:MLA_V + 1] for acc in accs]
    first = _lane_lt((tq, LANES), MLA_V)
    o_ref[...] = jnp.where(first, outs[0], pltpu.roll(outs[1], MLA_V, 1)).astype(o_ref.dtype)


def mla_cast_rows(w, b, t, tq):
    steps = b * N_PAIRS * (t // tq)
    rows = int(np.prod(w.shape[:-1]))
    per = rows // steps
    return per if rows % steps == 0 and per % 16 == 0 else None


def mla_attention(qf, kf, vf, b, t, tq, tk, cast=()):
    n = b * t
    nq = t // tq
    step = lambda i, p, s: ((i * N_PAIRS + p) * nq + s, 0)
    cast2d = [w.reshape(-1, w.shape[-1]) for w in cast]
    cast_specs = [pl.BlockSpec((mla_cast_rows(w, b, t, tq), w2.shape[1]), step) for w, w2 in zip(cast, cast2d)]
    outs = pl.pallas_call(
        functools.partial(_mla_attn_kernel, t=t, tk=tk, n_cast=len(cast)),
        grid=(b, N_PAIRS, nq),
        in_specs=[
            pl.BlockSpec((tq, 2 * LANES), lambda i, p, s: (i * nq + s, p)),
            pl.BlockSpec((t, 2 * LANES), lambda i, p, s: (i, p)),
            pl.BlockSpec((t, 2 * LANES), lambda i, p, s: (i, p)),
        ] + cast_specs,
        out_specs=[pl.BlockSpec((tq, HEAD_PAIR), lambda i, p, s: (i * nq + s, p))] + cast_specs,
        out_shape=[jax.ShapeDtypeStruct((n, SLAB), BF16)]
        + [jax.ShapeDtypeStruct(w2.shape, BF16) for w2 in cast2d],
        compiler_params=_cparams(("parallel", "parallel", "arbitrary")),
        name="mla_attention",
    )(qf, kf, vf, *cast2d)
    return outs[0], [o.reshape(w.shape) for o, w in zip(outs[1:], cast)]


def _in_proj_segments():
    sizes = (SSM_INNER, SSM_CONV_CH, 2 * SSM_HEADS, SLAB, SLAB, SLAB, MLA_Q_RANK, MLA_KV_RANK, MLA_ROPE,
             SLAB, SLAB, SLAB)
    off = [int(v) for v in np.concatenate([[0], np.cumsum(sizes)])]
    main = ((off[0], off[2]), (off[3], off[8]), (off[9], off[12]))
    tail = ((off[8], off[9]), (off[2], off[3]))
    return main, tail


def _in_proj_columns():
    main, tail = _in_proj_segments()
    cols = lambda segs: np.concatenate([np.arange(a, b) for a, b in segs])
    return cols(main), cols(tail)


def in_proj_weights(w_in_l):
    main, tail = _in_proj_segments()
    w_main = jnp.concatenate([w_in_l[:, a:b] for a, b in main], axis=1).astype(BF16)
    pad = jnp.zeros((D_MODEL, LANES - sum(b - a for a, b in tail)), w_in_l.dtype)
    w_tail = jnp.concatenate([w_in_l[:, a:b] for a, b in tail] + [pad], axis=1).astype(BF16)
    return w_main, w_tail


MLA_TQ = 1024
MLA_CASTS_PER_CALL = 2


def mixers(proj, tail, p, l, b, t, cast):
    xbc = conv_silu(proj, p["conv_w"][l], p["conv_b"][l], b, t)
    y_f, y_b = ssd_scan(xbc, tail, p["dt_bias"][l], p["a_log"][l], b, t)
    y_ssm = ssd_combine(y_f, y_b, xbc, proj, p["d_skip"][l], p["ssm_norm_w"][l], 1024)

    y_na = na_attention(proj, p["na_rpb"][l], b, t)

    qf, kf, vf = mla_project(proj, tail, p["mla_q_norm_w"][l], p["mla_kv_norm_w"][l],
                             p["mla_w_uq"][l], p["mla_w_ukv"][l], b, t, 512)
    y_mla, cast_out = mla_attention(qf, kf, vf, b, t, MLA_TQ, 512, cast)

    qkvs = rope_qkv(proj, b, t, 1024)
    outs, lses = zip(*[band_attention(qkv, w, d) for qkv, (w, d) in zip(qkvs, DIL_PAIRS)])
    y_dil = dil_combine(outs, lses, 1024)
    return (y_ssm, y_na, y_mla, y_dil), cast_out


def kernel(x, attn_norm_w, w_in, conv_w, conv_b, a_log, dt_bias, d_skip, ssm_norm_w, na_rpb,
           mla_q_norm_w, mla_kv_norm_w, mla_w_uq, mla_w_ukv, w_o, ffn_norm_w, ffn_w_gate, ffn_w_up,
           ffn_w_down, router_w, exp_w_gate, exp_w_up, exp_w_down, final_norm_w):
    b, t, _ = x.shape
    n = b * t
    depth = w_in.shape[0]
    p = dict(conv_w=conv_w, conv_b=conv_b, a_log=a_log, dt_bias=dt_bias, d_skip=d_skip,
             ssm_norm_w=ssm_norm_w, na_rpb=na_rpb, mla_q_norm_w=mla_q_norm_w,
             mla_kv_norm_w=mla_kv_norm_w, mla_w_uq=mla_w_uq, mla_w_ukv=mla_w_ukv)
    x = x.reshape(n, D_MODEL)
    cast_rows = 256
    w_o_b = cast_bf16(w_o, cast_rows)
    ffn_b = [cast_bf16(w, cast_rows) for w in (ffn_w_gate, ffn_w_up, ffn_w_down)]
    exp_f32 = [exp_w_gate, exp_w_up, exp_w_down]
    exp_b = [None] * len(exp_f32)
    pending = [k for k, w in enumerate(exp_f32) if mla_cast_rows(w, b, t, MLA_TQ) is not None]
    moe_tm = 512
    normed = False
    for l in range(depth):
        w_main, w_tail = in_proj_weights(w_in[l])
        proj, tail = in_proj(x, attn_norm_w[l], w_main, w_tail, 512, PROJ_MAIN)
        jobs, pending = pending[:MLA_CASTS_PER_CALL], pending[MLA_CASTS_PER_CALL:]
        mix, cast_out = mixers(proj, tail, p, l, b, t, [exp_f32[k] for k in jobs])
        for k, w_b in zip(jobs, cast_out):
            exp_b[k] = w_b
        x = out_proj(mix, w_o_b, l, x, 512, D_MODEL)
        j = l // 2
        if l % 2 == 0:
            x = ffn_dense(x, ffn_norm_w[l], *ffn_b, j, 1024, 512)
        else:
            pending = []
            exp_b = [cast_bf16(w, cast_rows) if w_b is None else w_b for w, w_b in zip(exp_f32, exp_b)]
            top_i, gates = moe_router(x, ffn_norm_w[l], router_w[j], 512)
            src, pos, tile_expert, tile_valid = moe_plan(top_i, moe_tm)
            y = moe_ffn(x, ffn_norm_w[l], *exp_b, j, src, tile_expert, tile_valid, moe_tm, 512)
            last = l == depth - 1
            x = moe_combine(x, gates, y, pos, final_norm_w if last else None, 512)
            normed = last
    if not normed:
        x = rmsnorm_rows(x, final_norm_w, 1024)
    return x.reshape(b, t, D_MODEL)
```

```python
import functools
import math

import numpy as np
import jax
import jax.numpy as jnp
from jax import lax
from jax.experimental import pallas as pl
from jax.experimental.pallas import tpu as pltpu

F32 = jnp.float32
BF16 = jnp.bfloat16

D_MODEL = 2048
GRID_W = 64
HEAD_DIM = 64
ROPE_THETA = 10000.0
NORM_EPS = 1e-6
NEG_INF = -1e30

SSM_HEADS = 8
SSM_HEAD_DIM = 64
SSM_INNER = SSM_HEADS * SSM_HEAD_DIM
SSM_GROUPS = 2
SSM_STATE = 128
SSM_CONV = 5
SSM_CHUNK = 128
SSD_CHUNKS_PER_STEP = 2
SSM_CONV_CH = SSM_INNER + 2 * SSM_GROUPS * SSM_STATE

NA_HEADS = 8
NA_WIN_ROWS = 8
NA_WIN_COLS = 16
NA_COL_BLOCK = 16
NA_KEY_COLS = 32
NA_ROWS_PER_STEP = 8

MLA_HEADS = 8
MLA_Q_RANK = 512
MLA_KV_RANK = 512
MLA_NOPE = 64
MLA_ROPE = 32
MLA_V = 64

DIL_HEADS = 8
DIL_PAIRS = ((128, 1), (512, 4), (2048, 16))
DIL_QBLOCK = 128
BAND_BLOCKS_PER_STEP = 4
LSE_LANES = 16

N_EXPERTS = 8
TOP_K = 2
SPLIT_PARTS = 3

LANES = 128
HEAD_PAIR = 2 * HEAD_DIM
N_PAIRS = 4
SLAB = 512

COL_Z, COL_XBC, COL_NAQ, COL_NAK, COL_NAV, COL_CQ, COL_CKV, COL_DLQ, COL_DLK, COL_DLV = (
    0, 1, 3, 4, 5, 6, 7, 8, 9, 10)
PROJ_MAIN = 11 * SLAB
TAIL_DT = 32

VMEM_LIMIT = 56 * 1024 * 1024


def _cparams(sem, vmem=VMEM_LIMIT):
    return pltpu.CompilerParams(dimension_semantics=sem, vmem_limit_bytes=vmem)


def _lane_lt(shape, bound, period=None):
    lane = lax.broadcasted_iota(jnp.int32, shape, len(shape) - 1)
    if period is not None:
        lane = lane % period
    return lane < bound


def _rms(x, w):
    ms = jnp.mean(x * x, axis=-1, keepdims=True)
    return x * lax.rsqrt(ms + NORM_EPS) * w


def _dot(a, b):
    return jnp.dot(a, b, preferred_element_type=F32)


def _dot_nt(a, b):
    return lax.dot_general(a, b, (((1,), (1,)), ((), ())), preferred_element_type=F32)


def _dot_tn(a, b):
    return lax.dot_general(a, b, (((0,), (0,)), ((), ())), preferred_element_type=F32)


def _cast_kernel(x_ref, o_ref):
    o_ref[...] = x_ref[...].astype(o_ref.dtype)


def cast_bf16(w, tr):
    shape = w.shape
    w2 = w.reshape(-1, shape[-1])
    r, c = w2.shape
    out = pl.pallas_call(
        _cast_kernel,
        grid=(r // tr,),
        in_specs=[pl.BlockSpec((tr, c), lambda i: (i, 0))],
        out_specs=pl.BlockSpec((tr, c), lambda i: (i, 0)),
        out_shape=jax.ShapeDtypeStruct((r, c), BF16),
        compiler_params=_cparams(("parallel",)),
        name="cast_bf16",
    )(w2)
    return out.reshape(shape)


def _in_proj_kernel(x_ref, nw_ref, w_ref, wt_ref, o_ref, t_ref, h_ref):
    @pl.when(pl.program_id(1) == 0)
    def _():
        h = _rms(x_ref[...], nw_ref[...]).astype(BF16)
        h_ref[...] = h
        t_ref[...] = _dot(h, wt_ref[...])

    o_ref[...] = _dot(h_ref[...], w_ref[...]).astype(o_ref.dtype)


def in_proj(x, nw, w_main, w_tail, tm, tn):
    n, k = x.shape
    nout = w_main.shape[1]
    w_mode = pl.Buffered(1) if tn == nout else None
    return pl.pallas_call(
        _in_proj_kernel,
        grid=(n // tm, nout // tn),
        in_specs=[
            pl.BlockSpec((tm, k), lambda i, j: (i, 0)),
            pl.BlockSpec((1, k), lambda i, j: (0, 0)),
            pl.BlockSpec((k, tn), lambda i, j: (0, j), pipeline_mode=w_mode),
            pl.BlockSpec((k, LANES), lambda i, j: (0, 0)),
        ],
        out_specs=[pl.BlockSpec((tm, tn), lambda i, j: (i, j)), pl.BlockSpec((tm, LANES), lambda i, j: (i, 0))],
        out_shape=[jax.ShapeDtypeStruct((n, nout), BF16), jax.ShapeDtypeStruct((n, LANES), F32)],
        scratch_shapes=[pltpu.VMEM((tm, k), BF16)],
        compiler_params=_cparams(("parallel", "arbitrary")),
        name="in_proj",
    )(x, nw.reshape(1, k), w_main, w_tail)


def _out_proj_kernel(a0_ref, a1_ref, a2_ref, a3_ref, w_ref, r_ref, o_ref):
    acc = r_ref[...]
    for s, a_ref in enumerate((a0_ref, a1_ref, a2_ref, a3_ref)):
        acc = acc + _dot(a_ref[...], w_ref[s * SLAB:(s + 1) * SLAB, :])
    o_ref[...] = acc


def out_proj(mix, w, layer, res, tm, tn):
    n = res.shape[0]
    return pl.pallas_call(
        _out_proj_kernel,
        grid=(n // tm, D_MODEL // tn),
        in_specs=[pl.BlockSpec((tm, SLAB), lambda i, j: (i, 0))] * 4 + [
            pl.BlockSpec((None, 4 * SLAB, tn), lambda i, j: (layer, 0, j)),
            pl.BlockSpec((tm, tn), lambda i, j: (i, j)),
        ],
        out_specs=pl.BlockSpec((tm, tn), lambda i, j: (i, j)),
        out_shape=jax.ShapeDtypeStruct((n, D_MODEL), F32),
        compiler_params=_cparams(("parallel", "arbitrary")),
        name="out_proj",
    )(*mix, w, res)


def _ffn_kernel(x_ref, nw_ref, wg_ref, wu_ref, wd_ref, o_ref, h_ref):
    @pl.when(pl.program_id(1) == 0)
    def _():
        x = x_ref[...]
        h_ref[...] = _rms(x, nw_ref[...]).astype(BF16)
        o_ref[...] = x

    _swiglu_rows(h_ref, wg_ref, wu_ref, wd_ref, o_ref)


FFN_ROW_CHUNK = 512


def _swiglu_rows(h_ref, wg_ref, wu_ref, wd_ref, o_ref):
    tm = h_ref.shape[0]
    for r0 in range(0, tm, FFN_ROW_CHUNK):
        rows = slice(r0, r0 + FFN_ROW_CHUNK)
        h = h_ref[rows, :]
        g = _dot(h, wg_ref[...])
        u = _dot(h, wu_ref[...])
        a = (g * jax.nn.sigmoid(g) * u).astype(BF16)
        o_ref[rows, :] += _dot(a, wd_ref[...])


def ffn_dense(x, nw, wg, wu, wd, layer, tm, tf):
    n = x.shape[0]
    d_ff = wg.shape[-1]
    return pl.pallas_call(
        _ffn_kernel,
        grid=(n // tm, d_ff // tf),
        in_specs=[
            pl.BlockSpec((tm, D_MODEL), lambda i, j: (i, 0)),
            pl.BlockSpec((1, D_MODEL), lambda i, j: (0, 0)),
            pl.BlockSpec((None, D_MODEL, tf), lambda i, j: (layer, 0, j)),
            pl.BlockSpec((None, D_MODEL, tf), lambda i, j: (layer, 0, j)),
            pl.BlockSpec((None, tf, D_MODEL), lambda i, j: (layer, j, 0)),
        ],
        out_specs=pl.BlockSpec((tm, D_MODEL), lambda i, j: (i, 0)),
        out_shape=jax.ShapeDtypeStruct((n, D_MODEL), F32),
        scratch_shapes=[pltpu.VMEM((tm, D_MODEL), BF16)],
        compiler_params=_cparams(("parallel", "arbitrary")),
        name="ffn_dense",
    )(x, nw.reshape(1, D_MODEL), wg, wu, wd)


def _router_kernel(x_ref, nw_ref, rw_ref, idx_ref, gate_ref):
    h = _rms(x_ref[...], nw_ref[...])
    acc = jnp.zeros((h.shape[0], LANES), F32)
    rem = h
    for _ in range(SPLIT_PARTS):
        part = rem.astype(BF16)
        acc = acc + _dot(part, rw_ref[...])
        rem = rem - part.astype(F32)
    logits = acc
    for k in range(1, SPLIT_PARTS):
        logits = logits + pltpu.roll(acc, LANES - k * N_EXPERTS, 1)
    lane = lax.broadcasted_iota(jnp.int32, logits.shape, 1)
    logits = jnp.where(lane < N_EXPERTS, logits, -jnp.inf)
    m1 = jnp.max(logits, axis=-1, keepdims=True)
    i1 = jnp.min(jnp.where(logits == m1, lane, LANES), axis=-1, keepdims=True)
    rest = jnp.where(lane == i1, -jnp.inf, logits)
    m2 = jnp.max(rest, axis=-1, keepdims=True)
    i2 = jnp.min(jnp.where(rest == m2, lane, LANES), axis=-1, keepdims=True)
    e2 = jnp.exp(m2 - m1)
    g1 = 1.0 / (1.0 + e2)
    g2 = e2 / (1.0 + e2)
    idx_ref[...] = jnp.where(lane == 0, i1, i2)[:, :TOP_K]
    gate_ref[...] = jnp.where(lane == 0, g1, g2)[:, :TOP_K]


def moe_router(x, nw, router_w, tm):
    n = x.shape[0]
    parts, rem = [], router_w
    for _ in range(SPLIT_PARTS):
        parts.append(rem.astype(BF16))
        rem = rem - parts[-1].astype(F32)
    pad = jnp.zeros((D_MODEL, LANES - SPLIT_PARTS * N_EXPERTS), BF16)
    rw = jnp.concatenate(parts + [pad], axis=1)
    return pl.pallas_call(
        _router_kernel,
        grid=(n // tm,),
        in_specs=[
            pl.BlockSpec((tm, D_MODEL), lambda i: (i, 0)),
            pl.BlockSpec((1, D_MODEL), lambda i: (0, 0)),
            pl.BlockSpec((D_MODEL, LANES), lambda i: (0, 0)),
        ],
        out_specs=[pl.BlockSpec((tm, TOP_K), lambda i: (i, 0)),
                   pl.BlockSpec((tm, TOP_K), lambda i: (i, 0))],
        out_shape=[jax.ShapeDtypeStruct((n, TOP_K), jnp.int32),
                   jax.ShapeDtypeStruct((n, TOP_K), F32)],
        compiler_params=_cparams(("parallel",)),
        name="moe_router",
    )(x, nw.reshape(1, D_MODEL), rw)


def moe_plan(top_i, tm):
    n = top_i.shape[0]
    flat_e = top_i.reshape(-1)
    onehot = (flat_e[:, None] == jnp.arange(N_EXPERTS, dtype=jnp.int32)[None, :]).astype(jnp.int32)
    csum = jnp.cumsum(onehot, axis=0)
    counts = csum[-1]
    rank = jnp.sum(onehot * csum, axis=1) - 1
    padded = ((counts + tm - 1) // tm) * tm
    pend = jnp.cumsum(padded)
    pstart = pend - padded
    pos = pstart[flat_e] + rank
    n_slots = n * TOP_K + N_EXPERTS * tm
    n_tiles = n_slots // tm
    src = jnp.zeros((n_slots,), jnp.int32).at[pos].set(jnp.arange(n * TOP_K, dtype=jnp.int32) // TOP_K)
    tile_start = jnp.arange(n_tiles, dtype=jnp.int32) * tm
    tile_expert = jnp.sum((tile_start[:, None] >= pend[None, :]).astype(jnp.int32), axis=1)
    tile_valid = (tile_start < pend[-1]).astype(jnp.int32)
    last_valid = jnp.maximum(pend[-1] // tm - 1, 0)
    tile_expert = jnp.where(tile_valid == 1, tile_expert, tile_expert[last_valid]).astype(jnp.int32)
    return src, pos.reshape(n, TOP_K).astype(jnp.int32), tile_expert, tile_valid


def _moe_ffn_kernel(te_ref, tv_ref, src_ref, nsrc_ref, x_hbm, nw_ref, wg_ref, wu_ref, wd_ref, y_ref,
                    xbuf, h_ref, sem, *, tm, rows_per_step):
    i = pl.program_id(0)
    j = pl.program_id(1)
    issued = xbuf.shape[0]
    valid = tv_ref[i] == 1
    has_rows = jnp.logical_or(i == 0, tv_ref[jnp.maximum(i - 1, 0)] == 1)

    def start_row(idx_ref, row):
        tok = idx_ref[0, 0, jnp.minimum(row, tm - 1)]
        pltpu.make_async_copy(x_hbm.at[pl.ds(tok, 1)], xbuf.at[pl.ds(row, 1)], sem).start(priority=1)

    @pl.when(j == 0)
    def _():
        y_ref[...] = jnp.zeros_like(y_ref)

    @pl.when(jnp.logical_and(j == 0, i == 0))
    def _():
        def start(r, c):
            start_row(src_ref, r)
            return c

        lax.fori_loop(0, issued, start, 0)

    @pl.when(jnp.logical_and(j == 0, has_rows))
    def _():
        pltpu.make_async_copy(x_hbm.at[pl.ds(0, issued)], xbuf.at[pl.ds(0, issued)], sem).wait()
        h_ref[...] = _rms(xbuf[0:tm, :], nw_ref[...]).astype(BF16)

    @pl.when(valid)
    def _():
        for r in range(rows_per_step):
            start_row(nsrc_ref, j * rows_per_step + r)
        _swiglu_rows(h_ref, wg_ref, wu_ref, wd_ref, y_ref)


def moe_ffn(x, nw, wg, wu, wd, layer, src, tile_expert, tile_valid, tm, tf):
    n_slots = src.shape[0]
    n_tiles = n_slots // tm + 1
    d_ff = wg.shape[-1]
    nf = d_ff // tf
    sublanes = 8
    rows_per_step = -(-tm // (nf * sublanes)) * sublanes
    buf_rows = rows_per_step * nf
    tile_expert = jnp.concatenate([tile_expert, tile_expert[-1:]])
    tile_valid = jnp.concatenate([tile_valid, jnp.zeros((1,), tile_valid.dtype)])
    src3 = jnp.concatenate([src, jnp.zeros((tm,), src.dtype)]).reshape(n_tiles, 1, tm)

    def wcol(i, j, te_ref, tv_ref):
        return (layer, te_ref[i], 0, jnp.where(tv_ref[i] == 1, j, nf - 1))

    def wrow(i, j, te_ref, tv_ref):
        return (layer, te_ref[i], jnp.where(tv_ref[i] == 1, j, nf - 1), 0)

    grid_spec = pltpu.PrefetchScalarGridSpec(
        num_scalar_prefetch=2,
        grid=(n_tiles, nf),
        in_specs=[
            pl.BlockSpec((1, 1, tm), lambda i, j, *_: (i, 0, 0), memory_space=pltpu.SMEM),
            pl.BlockSpec((1, 1, tm), lambda i, j, *_: (jnp.minimum(i + 1, n_tiles - 1), 0, 0),
                         memory_space=pltpu.SMEM),
            pl.BlockSpec(memory_space=pl.ANY),
            pl.BlockSpec((1, D_MODEL), lambda i, j, *_: (0, 0)),
            pl.BlockSpec((None, None, D_MODEL, tf), wcol),
            pl.BlockSpec((None, None, D_MODEL, tf), wcol),
            pl.BlockSpec((None, None, tf, D_MODEL), wrow),
        ],
        out_specs=pl.BlockSpec((tm, D_MODEL), lambda i, j, *_: (i, 0)),
        scratch_shapes=[pltpu.VMEM((buf_rows, D_MODEL), F32), pltpu.VMEM((tm, D_MODEL), BF16),
                        pltpu.SemaphoreType.DMA],
    )
    return pl.pallas_call(
        functools.partial(_moe_ffn_kernel, tm=tm, rows_per_step=rows_per_step),
        grid_spec=grid_spec,
        out_shape=jax.ShapeDtypeStruct((n_tiles * tm, D_MODEL), F32),
        compiler_params=_cparams(("arbitrary", "arbitrary")),
        name="moe_ffn",
    )(tile_expert, tile_valid, src3, src3, x, nw.reshape(1, D_MODEL), wg, wu, wd)


def _moe_combine_kernel(pos_ref, npos_ref, x_ref, gate_ref, y_hbm, fw_ref, o_ref, ybuf, sem, *, tm, final_norm):
    i = pl.program_id(0)
    n_tiles = pl.num_programs(0)
    cur = i % 2

    def row_copy(idx_ref, buf, r, k):
        slot = idx_ref[0, 0, r * TOP_K + k]
        return pltpu.make_async_copy(y_hbm.at[pl.ds(slot, 1)], ybuf.at[buf, k, pl.ds(r, 1)], sem.at[buf])

    def gather(idx_ref, buf):
        def start(r, c):
            for k in range(TOP_K):
                row_copy(idx_ref, buf, r, k).start(priority=k % 2)
            return c

        lax.fori_loop(0, tm, start, 0, unroll=8)

    @pl.when(i == 0)
    def _():
        gather(pos_ref, 0)

    @pl.when(i + 1 < n_tiles)
    def _():
        gather(npos_ref, 1 - cur)

    for k in range(TOP_K):
        pltpu.make_async_copy(y_hbm.at[pl.ds(0, tm)], ybuf.at[cur, k], sem.at[cur]).wait()
    gates = gate_ref[...]
    out = x_ref[...]
    for k in range(TOP_K):
        out = out + gates[:, k:k + 1] * ybuf[cur, k]
    if final_norm:
        out = _rms(out, fw_ref[...])
    o_ref[...] = out


def moe_combine(x, gates, y, pos, final_w, tm):
    n = x.shape[0]
    final_norm = final_w is not None
    fw = final_w if final_norm else jnp.ones((D_MODEL,), F32)
    n_tiles = n // tm
    pos3 = pos.reshape(n_tiles, 1, tm * TOP_K)
    return pl.pallas_call(
        functools.partial(_moe_combine_kernel, tm=tm, final_norm=final_norm),
        grid=(n_tiles,),
        in_specs=[
            pl.BlockSpec((1, 1, tm * TOP_K), lambda i: (i, 0, 0), memory_space=pltpu.SMEM),
            pl.BlockSpec((1, 1, tm * TOP_K), lambda i: (jnp.minimum(i + 1, n_tiles - 1), 0, 0),
                         memory_space=pltpu.SMEM),
            pl.BlockSpec((tm, D_MODEL), lambda i: (i, 0)),
            pl.BlockSpec((tm, TOP_K), lambda i: (i, 0)),
            pl.BlockSpec(memory_space=pl.ANY),
            pl.BlockSpec((1, D_MODEL), lambda i: (0, 0)),
        ],
        out_specs=pl.BlockSpec((tm, D_MODEL), lambda i: (i, 0)),
        out_shape=jax.ShapeDtypeStruct((n, D_MODEL), F32),
        scratch_shapes=[pltpu.VMEM((2, TOP_K, tm, D_MODEL), F32), pltpu.SemaphoreType.DMA((2,))],
        compiler_params=_cparams(("arbitrary",)),
        name="moe_combine",
    )(pos3, pos3, x, gates, y, fw.reshape(1, D_MODEL))


def _rmsnorm_kernel(x_ref, w_ref, o_ref):
    o_ref[...] = _rms(x_ref[...], w_ref[...])


def rmsnorm_rows(x, w, tm):
    n = x.shape[0]
    return pl.pallas_call(
        _rmsnorm_kernel,
        grid=(n // tm,),
        in_specs=[pl.BlockSpec((tm, D_MODEL), lambda i: (i, 0)),
                  pl.BlockSpec((1, D_MODEL), lambda i: (0, 0))],
        out_specs=pl.BlockSpec((tm, D_MODEL), lambda i: (i, 0)),
        out_shape=jax.ShapeDtypeStruct((n, D_MODEL), F32),
        compiler_params=_cparams(("parallel",)),
        name="final_norm",
    )(x, w.reshape(1, D_MODEL))


CONV_PAD = 8


def _conv_kernel(x_ref, w_ref, b_ref, o_ref, pad_ref, *, t):
    half = SSM_CONV // 2
    zeros = jnp.zeros((CONV_PAD, pad_ref.shape[1]), F32)
    pad_ref[0:CONV_PAD, :] = zeros
    pad_ref[CONV_PAD + t:CONV_PAD + t + CONV_PAD, :] = zeros
    pad_ref[CONV_PAD:CONV_PAD + t, :] = x_ref[...].astype(F32)
    acc = jnp.zeros(o_ref.shape, F32) + b_ref[...]
    for k in range(SSM_CONV):
        acc = acc + pad_ref[pl.ds(CONV_PAD - half + k, t), :] * w_ref[k:k + 1, :]
    o_ref[...] = (acc * jax.nn.sigmoid(acc)).astype(o_ref.dtype)


def conv_silu(proj, conv_w, conv_b, b, t):
    tc = 256
    nblk = SSM_CONV_CH // tc
    col0 = COL_XBC * SLAB // tc
    return pl.pallas_call(
        functools.partial(_conv_kernel, t=t),
        grid=(b, nblk),
        in_specs=[
            pl.BlockSpec((t, tc), lambda i, j: (i, col0 + j)),
            pl.BlockSpec((SSM_CONV, tc), lambda i, j: (0, j)),
            pl.BlockSpec((1, tc), lambda i, j: (0, j)),
        ],
        out_specs=pl.BlockSpec((t, tc), lambda i, j: (i, j)),
        out_shape=jax.ShapeDtypeStruct((b * t, SSM_CONV_CH), BF16),
        scratch_shapes=[pltpu.VMEM((t + 2 * CONV_PAD, tc), F32)],
        compiler_params=_cparams(("parallel", "parallel")),
        name="conv_silu",
    )(proj, conv_w, conv_b.reshape(1, SSM_CONV_CH))


def _ssd_stage1(xbc_ref, tail_ref, bias_ref, alog_ref, tri_ref, sel_ref, state_ref, direction):
    q = SSM_CHUNK
    dt = jax.nn.softplus(tail_ref[...] + bias_ref[...])
    da = dt * (-jnp.exp(alog_ref[...]))
    cs = jnp.dot(tri_ref[...], da, preferred_element_type=F32, precision=lax.Precision.HIGHEST)
    total = cs[q - 1:q, :]
    if direction == 0:
        e_out = cs
        e_in = total - cs
        e_seg = cs
    else:
        ex = cs - da
        e_out = total - ex
        e_in = ex
        e_seg = -ex
    dec_out_b = jnp.exp(e_out).astype(BF16)
    dec_in_dt_b = (jnp.exp(e_in) * dt).astype(BF16)
    pairs_per_group = N_PAIRS // SSM_GROUPS
    cbs, y_offs, in_scales = [], [], []
    for g in range(SSM_GROUPS):
        bm = xbc_ref[:, SSM_INNER + g * SSM_STATE:SSM_INNER + (g + 1) * SSM_STATE]
        cm = xbc_ref[:, SSM_INNER + (SSM_GROUPS + g) * SSM_STATE:SSM_INNER + (SSM_GROUPS + g + 1) * SSM_STATE]
        cbs.append(_dot_nt(cm, bm))
        for pair in range(g * pairs_per_group, (g + 1) * pairs_per_group):
            sel = sel_ref[direction, pair]
            y_offs.append(_dot(dec_out_b, sel) * _dot(cm, state_ref[pair].astype(BF16)))
            in_scales.append(_dot(dec_in_dt_b, sel))
    return dict(e_seg=e_seg, e_seg_t=jnp.transpose(e_seg), dt_t=jnp.transpose(dt), dec_tot=jnp.exp(total),
                cbs=cbs, y_offs=y_offs, in_scales=in_scales)


def _ssd_stage2(ctx, xbc_ref, y_ref, state_ref, direction):
    q = SSM_CHUNK
    row = lax.broadcasted_iota(jnp.int32, (q, q), 0)
    col = lax.broadcasted_iota(jnp.int32, (q, q), 1)
    keep = (row >= col) if direction == 0 else (col >= row)
    first_half = _lane_lt((q, HEAD_PAIR), HEAD_DIM)
    pairs_per_group = N_PAIRS // SSM_GROUPS
    for pair in range(N_PAIRS):
        g = pair // pairs_per_group
        bm = xbc_ref[:, SSM_INNER + g * SSM_STATE:SSM_INNER + (g + 1) * SSM_STATE]
        xs_pair = xbc_ref[:, pair * HEAD_PAIR:(pair + 1) * HEAD_PAIR]
        ys, decs = [], []
        for hh in range(2):
            lane = TAIL_DT + direction * SSM_HEADS + pair * 2 + hh
            seg = ctx["e_seg"][:, lane:lane + 1] - ctx["e_seg_t"][lane:lane + 1, :]
            lmat = jnp.where(keep, jnp.exp(seg), 0.0)
            w = (ctx["cbs"][g] * lmat * ctx["dt_t"][lane:lane + 1, :]).astype(BF16)
            ys.append(_dot(w, xs_pair))
            decs.append(ctx["dec_tot"][:, lane:lane + 1])
        y_ref[:, pair * HEAD_PAIR:(pair + 1) * HEAD_PAIR] = jnp.where(first_half, ys[0], ys[1]) + ctx["y_offs"][pair]
        st = state_ref[pair]
        first_half_s = _lane_lt(st.shape, HEAD_DIM)
        state_ref[pair] = (st * jnp.where(first_half_s, decs[0], decs[1])
                           + _dot_tn(bm, (xs_pair * ctx["in_scales"][pair]).astype(BF16)))


def _ssd_kernel(xbc_f_ref, tail_f_ref, xbc_b_ref, tail_b_ref, bias_ref, alog_ref, tri_ref, sel_ref,
                y_f_ref, y_b_ref, state_ref):
    @pl.when(pl.program_id(1) == 0)
    def _():
        state_ref[...] = jnp.zeros_like(state_ref)

    q = SSM_CHUNK
    for k in range(SSD_CHUNKS_PER_STEP):
        ins = []
        for d, refs in enumerate(((xbc_f_ref, tail_f_ref, y_f_ref), (xbc_b_ref, tail_b_ref, y_b_ref))):
            sub = k if d == 0 else SSD_CHUNKS_PER_STEP - 1 - k
            ins.append(tuple(r.at[sub * q:(sub + 1) * q] for r in refs))
        ctxs = [_ssd_stage1(xbc_ref, tail_ref, bias_ref, alog_ref, tri_ref, sel_ref, state_ref.at[d], d)
                for d, (xbc_ref, tail_ref, _) in enumerate(ins)]
        for d, (xbc_ref, _, y_ref) in enumerate(ins):
            _ssd_stage2(ctxs[d], xbc_ref, y_ref, state_ref.at[d], d)


def _ssd_lane_selectors():
    sel = np.zeros((2, N_PAIRS, LANES, HEAD_PAIR), np.float32)
    for d in range(2):
        for pair in range(N_PAIRS):
            for hh in range(2):
                sel[d, pair, TAIL_DT + d * SSM_HEADS + pair * 2 + hh, hh * HEAD_DIM:(hh + 1) * HEAD_DIM] = 1.0
    return jnp.asarray(sel, BF16)


def ssd_scan(xbc, tail, dt_bias, a_log, b, t):
    q = SSM_CHUNK
    rows = q * SSD_CHUNKS_PER_STEP
    nc = t // rows
    bias_row = jnp.zeros((1, LANES), F32).at[0, TAIL_DT:TAIL_DT + 2 * SSM_HEADS].set(dt_bias.reshape(-1))
    alog_row = jnp.zeros((1, LANES), F32).at[0, TAIL_DT:TAIL_DT + 2 * SSM_HEADS].set(a_log.reshape(-1))
    tri = jnp.asarray(np.tril(np.ones((q, q), np.float32)))
    fwd = lambda i, c: (i * nc + c, 0)
    bwd = lambda i, c: (i * nc + nc - 1 - c, 0)
    const = lambda i, c: (0, 0)
    return pl.pallas_call(
        _ssd_kernel,
        grid=(b, nc),
        in_specs=[
            pl.BlockSpec((rows, SSM_CONV_CH), fwd), pl.BlockSpec((rows, LANES), fwd),
            pl.BlockSpec((rows, SSM_CONV_CH), bwd), pl.BlockSpec((rows, LANES), bwd),
            pl.BlockSpec((1, LANES), const), pl.BlockSpec((1, LANES), const), pl.BlockSpec((q, q), const),
            pl.BlockSpec((2, N_PAIRS, LANES, HEAD_PAIR), lambda i, c: (0, 0, 0, 0)),
        ],
        out_specs=[pl.BlockSpec((rows, SSM_INNER), fwd), pl.BlockSpec((rows, SSM_INNER), bwd)],
        out_shape=[jax.ShapeDtypeStruct((b * t, SSM_INNER), F32)] * 2,
        scratch_shapes=[pltpu.VMEM((2, N_PAIRS, SSM_STATE, HEAD_PAIR), F32)],
        compiler_params=_cparams(("parallel", "arbitrary")),
        name="ssd_scan",
    )(xbc, tail, xbc, tail, bias_row, alog_row, tri, _ssd_lane_selectors())


def _ssd_combine_kernel(yf_ref, yb_ref, xs_ref, z_ref, d_ref, nw_ref, o_ref):
    y = yf_ref[...] + yb_ref[...] + xs_ref[...].astype(F32) * d_ref[...]
    z = z_ref[...].astype(F32)
    o_ref[...] = _rms(y * (z * jax.nn.sigmoid(z)), nw_ref[...]).astype(o_ref.dtype)


def ssd_combine(y_f, y_b, xbc, proj, d_skip, norm_w, tm):
    n = y_f.shape[0]
    d_row = jnp.repeat(d_skip, SSM_HEAD_DIM).reshape(1, SSM_INNER)
    row = lambda i: (i, 0)
    return pl.pallas_call(
        _ssd_combine_kernel,
        grid=(n // tm,),
        in_specs=[
            pl.BlockSpec((tm, SSM_INNER), row),
            pl.BlockSpec((tm, SSM_INNER), row),
            pl.BlockSpec((tm, SSM_INNER), row),
            pl.BlockSpec((tm, SLAB), lambda i: (i, COL_Z)),
            pl.BlockSpec((1, SSM_INNER), lambda i: (0, 0)),
            pl.BlockSpec((1, SSM_INNER), lambda i: (0, 0)),
        ],
        out_specs=pl.BlockSpec((tm, SSM_INNER), row),
        out_shape=jax.ShapeDtypeStruct((n, SSM_INNER), BF16),
        compiler_params=_cparams(("parallel",)),
        name="ssd_combine",
    )(y_f, y_b, xbc, proj, d_row, norm_w.reshape(1, SSM_INNER))


def _pair_scores(q2, k2):
    first_q = _lane_lt(q2.shape, HEAD_DIM)
    zero = jnp.zeros_like(q2)
    qs = jnp.concatenate([jnp.where(first_q, q2, zero), jnp.where(first_q, zero, q2)], axis=0)
    return _dot_nt(qs, k2)


def _pair_attend(s, v2):
    tq = s.shape[0] // 2
    m = jnp.max(s, axis=-1, keepdims=True)
    p = jnp.exp(s - m)
    l = jnp.sum(p, axis=-1, keepdims=True)
    o = _dot(p.astype(BF16), v2) / l
    lse = m + jnp.log(l)
    first_o = _lane_lt((tq, HEAD_PAIR), HEAD_DIM)
    return jnp.where(first_o, o[:tq], o[tq:]), (lse[:tq], lse[tq:])


def na_bias_tables(rpb, rows):
    kr = min(NA_WIN_ROWS, rows)
    qc = np.arange(GRID_W)
    kc = np.arange(GRID_W)
    q_start = np.clip(qc - NA_WIN_COLS // 2, 0, GRID_W - NA_WIN_COLS)
    col_in = (kc[None, :] >= q_start[:, None]) & (kc[None, :] < q_start[:, None] + NA_WIN_COLS)
    col_off = np.clip(kc[None, :] - qc[:, None] + NA_WIN_COLS - 1, 0, 2 * NA_WIN_COLS - 2)
    onehot = (col_off[None] == np.arange(2 * NA_WIN_COLS - 1)[:, None, None]).astype(np.float32)
    expanded = jnp.einsum("hrc,cqk->hqrk", rpb, jnp.asarray(onehot), precision=lax.Precision.HIGHEST)
    expanded = jnp.where(jnp.asarray(col_in)[None, :, None, :], expanded, NEG_INF)

    def table(r):
        row_start = int(np.clip(r - kr // 2, 0, rows - kr))
        ro0 = row_start - r + NA_WIN_ROWS - 1
        return expanded[:, :, ro0:ro0 + kr, :].reshape(N_PAIRS, 2 * GRID_W, kr * GRID_W)

    rs = NA_ROWS_PER_STEP
    lo = [table(r) for r in range(rs)]
    mid = [table(min(rs, rows - 1))] * rs
    hi = [table(r) for r in range(rows - rs, rows)]
    return jnp.stack([jnp.stack(lo), jnp.stack(mid), jnp.stack(hi)])


def _na_kernel(q_ref, k_ref, v_ref, bias_ref, o_ref, *, rows, kr):
    step = pl.program_id(1)
    rs = NA_ROWS_PER_STEP
    for rr in range(rs):
        r = step * rs + rr
        row_start = jnp.clip(r - kr // 2, 0, rows - kr)
        k0 = pl.multiple_of(row_start * GRID_W, GRID_W)
        pair_cols = [slice(pair * HEAD_PAIR, (pair + 1) * HEAD_PAIR) for pair in range(N_PAIRS)]
        scores = []
        for pair, cols in enumerate(pair_cols):
            q2 = q_ref[rr * GRID_W:(rr + 1) * GRID_W, cols] * jnp.asarray(HEAD_DIM ** -0.5, BF16)
            scores.append(_pair_scores(q2, k_ref[pl.ds(k0, kr * GRID_W), cols]) + bias_ref[0, rr, pair])
        for s, cols in zip(scores, pair_cols):
            o, _ = _pair_attend(s, v_ref[pl.ds(k0, kr * GRID_W), cols])
            o_ref[rr * GRID_W:(rr + 1) * GRID_W, cols] = o.astype(o_ref.dtype)


def na_attention(proj, rpb, b, t):
    rows = t // GRID_W
    kr = min(NA_WIN_ROWS, rows)
    rs = NA_ROWS_PER_STEP
    nsteps = rows // rs
    bias = na_bias_tables(rpb, rows)

    def kind(i, s):
        return jnp.where(s == 0, 0, jnp.where(s == nsteps - 1, 2, 1))

    return pl.pallas_call(
        functools.partial(_na_kernel, rows=rows, kr=kr),
        grid=(b, nsteps),
        in_specs=[
            pl.BlockSpec((rs * GRID_W, SLAB), lambda i, s: (i * nsteps + s, COL_NAQ)),
            pl.BlockSpec((t, SLAB), lambda i, s: (i, COL_NAK)),
            pl.BlockSpec((t, SLAB), lambda i, s: (i, COL_NAV)),
            pl.BlockSpec((1, rs, N_PAIRS, 2 * GRID_W, kr * GRID_W), lambda i, s: (kind(i, s), 0, 0, 0, 0)),
        ],
        out_specs=pl.BlockSpec((rs * GRID_W, SLAB), lambda i, s: (i * nsteps + s, 0)),
        out_shape=jax.ShapeDtypeStruct((b * t, SLAB), BF16),
        compiler_params=_cparams(("parallel", "arbitrary")),
        name="na_attention",
    )(proj, proj, proj, bias)


def _rope_angles(t, d):
    inv = ROPE_THETA ** (-np.arange(0, d, 2, dtype=np.float32) / d)
    return np.arange(t, dtype=np.float32)[:, None] * inv[None, :]


def rope_tables_pair(t):
    ang = _rope_angles(t, HEAD_DIM)
    cos = np.tile(np.cos(ang), (1, 4))
    sin = np.tile(np.concatenate([-np.sin(ang), np.sin(ang)], axis=1), (1, 2))
    return jnp.asarray(cos, F32), jnp.asarray(sin, F32)


FOLD_CHUNK = 256
FOLD_DILS = tuple(d for _, d in DIL_PAIRS if d > 1)


def fold_permutation(dil):
    per = FOLD_CHUNK // dil
    perm = np.zeros((FOLD_CHUNK, FOLD_CHUNK), np.float32)
    dst = np.arange(FOLD_CHUNK)
    perm[dst, (dst % per) * dil + dst // per] = 1.0
    return jnp.asarray(perm, BF16)


def _rope_qkv_kernel(x_ref, v_ref, cos_ref, sin_ref, *rest):
    nd = len(FOLD_DILS)
    perm_refs, o_ref, fold_refs = rest[:nd], rest[nd], rest[nd + 1:]
    cos = cos_ref[...]
    sin = sin_ref[...]
    half = HEAD_DIM // 2
    for c in range(x_ref.shape[1] // LANES):
        x = x_ref[:, c * LANES:(c + 1) * LANES].astype(F32)
        rot = jnp.where(_lane_lt(x.shape, half, HEAD_DIM),
                        pltpu.roll(x, LANES - half, 1), pltpu.roll(x, half, 1))
        y = x * cos + rot * sin
        if c < N_PAIRS:
            y = y * (HEAD_DIM ** -0.5)
        o_ref[0, :, c * LANES:(c + 1) * LANES] = y.astype(o_ref.dtype)
    o_ref[0, :, 2 * SLAB:3 * SLAB] = v_ref[...]
    tm = x_ref.shape[0]
    for dil, perm_ref, f_ref in zip(FOLD_DILS, perm_refs, fold_refs):
        per = FOLD_CHUNK // dil
        for c in range(tm // FOLD_CHUNK):
            folded = _dot(perm_ref[...], o_ref[0, c * FOLD_CHUNK:(c + 1) * FOLD_CHUNK, :]).astype(f_ref.dtype)
            for p in range(dil):
                f_ref[p, c * per:(c + 1) * per, :] = folded[p * per:(p + 1) * per, :]


def rope_qkv(proj, b, t, tm):
    n = b * t
    cos, sin = rope_tables_pair(t)
    nb = t // tm
    fold_spec = lambda d: pl.BlockSpec((None, d, tm // d, 3 * SLAB), lambda i: (i // nb, 0, i % nb, 0))
    outs = pl.pallas_call(
        _rope_qkv_kernel,
        grid=(n // tm,),
        in_specs=[
            pl.BlockSpec((tm, 2 * SLAB), lambda i: (i, COL_DLQ // 2)),
            pl.BlockSpec((tm, SLAB), lambda i: (i, COL_DLV)),
            pl.BlockSpec((tm, LANES), lambda i: (i % nb, 0)),
            pl.BlockSpec((tm, LANES), lambda i: (i % nb, 0)),
        ] + [pl.BlockSpec((FOLD_CHUNK, FOLD_CHUNK), lambda i: (0, 0))] * len(FOLD_DILS),
        out_specs=[fold_spec(1)] + [fold_spec(d) for d in FOLD_DILS],
        out_shape=[jax.ShapeDtypeStruct((b, d, t // d, 3 * SLAB), BF16) for d in (1,) + FOLD_DILS],
        compiler_params=_cparams(("parallel",)),
        name="rope_qkv",
    )(proj, proj, cos, sin, *[fold_permutation(d) for d in FOLD_DILS])
    by_dil = dict(zip((1,) + FOLD_DILS, outs))
    return [by_dil[d] for _, d in DIL_PAIRS]


def _band_kernel(q_ref, k_ref, v_ref, o_ref, lse_ref, *, sub, half, span):
    tq = DIL_QBLOCK
    blocks = q_ref.shape[0] // tq
    for blk in range(blocks):
        qb = pl.program_id(2) * blocks + blk
        rows = slice(blk * tq, (blk + 1) * tq)
        start = jnp.clip(qb * tq - half, 0, sub - span)
        start = pl.multiple_of(start, half)
        q_pos = qb * tq + lax.broadcasted_iota(jnp.int32, (2 * tq, span), 0) % tq
        k_pos = start + lax.broadcasted_iota(jnp.int32, (2 * tq, span), 1)
        valid = jnp.abs(k_pos - q_pos) <= half
        pair_cols = [slice(pair * HEAD_PAIR, (pair + 1) * HEAD_PAIR) for pair in range(N_PAIRS)]
        scores = [jnp.where(valid, _pair_scores(q_ref[rows, cols], k_ref[pl.ds(start, span), cols]), NEG_INF)
                  for cols in pair_cols]
        head_slot = lax.broadcasted_iota(jnp.int32, (tq, LANES), 1) // LSE_LANES
        lse_all = jnp.zeros((tq, LANES), F32)
        for pair, (s, cols) in enumerate(zip(scores, pair_cols)):
            o, lse_heads = _pair_attend(s, v_ref[pl.ds(start, span), cols])
            o_ref[rows, cols] = o.astype(o_ref.dtype)
            for hh, lse in enumerate(lse_heads):
                lse_all = jnp.where(head_slot == pair * 2 + hh, lse, lse_all)
        lse_ref[rows, :] = lse_all


def band_attention(qkv, window, dil):
    b, _, sub, _ = qkv.shape
    half = window // (2 * dil)
    span = DIL_QBLOCK + 2 * half
    tq = DIL_QBLOCK * min(BAND_BLOCKS_PER_STEP, sub // DIL_QBLOCK)
    nqb = sub // tq
    return pl.pallas_call(
        functools.partial(_band_kernel, sub=sub, half=half, span=span),
        grid=(b, dil, nqb),
        in_specs=[
            pl.BlockSpec((None, None, tq, SLAB), lambda i, p, s: (i, p, s, 0)),
            pl.BlockSpec((None, None, sub, SLAB), lambda i, p, s: (i, p, 0, 1)),
            pl.BlockSpec((None, None, sub, SLAB), lambda i, p, s: (i, p, 0, 2)),
        ],
        out_specs=[pl.BlockSpec((None, None, tq, SLAB), lambda i, p, s: (i, p, s, 0)),
                   pl.BlockSpec((None, None, tq, LANES), lambda i, p, s: (i, p, s, 0))],
        out_shape=[jax.ShapeDtypeStruct((b, dil, sub, SLAB), BF16), jax.ShapeDtypeStruct((b, dil, sub, LANES), F32)],
        compiler_params=_cparams(("parallel", "parallel", "arbitrary")),
        name="band_attention_d%d" % dil,
    )(qkv, qkv, qkv)


def _dil_combine_kernel(*refs):
    nbr = len(DIL_PAIRS)
    o_refs, l_refs, expand_ref, out_ref = refs[:nbr], refs[nbr:2 * nbr], refs[2 * nbr], refs[2 * nbr + 1]
    scratch = iter(refs[2 * nbr + 2:])

    def token_order(ref):
        dil, per, width = ref.shape
        if dil == 1:
            return ref[0].astype(F32)
        buf = next(scratch)
        for p in range(dil):
            for c in range(width // LANES):
                buf[c, pl.ds(p, per, stride=dil), :] = ref[p, :, c * LANES:(c + 1) * LANES].astype(F32)
        return jnp.concatenate([buf[c] for c in range(width // LANES)], axis=1)

    def head_lanes(lse):
        full = jnp.zeros((lse.shape[0], SLAB), F32)
        rem = lse
        for _ in range(SPLIT_PARTS):
            part = rem.astype(BF16)
            full = full + _dot(part, expand_ref[...])
            rem = rem - part.astype(F32)
        return full

    os = [token_order(r) for r in o_refs]
    lses = [head_lanes(token_order(r)) for r in l_refs]
    m = functools.reduce(jnp.maximum, lses)
    ws = [jnp.exp(l - m) for l in lses]
    den = functools.reduce(jnp.add, ws)
    acc = functools.reduce(jnp.add, [(w / den) * o for w, o in zip(ws, os)])
    out_ref[...] = acc.astype(out_ref.dtype)


def dil_combine(outs, lses, tm):
    b, _, t, _ = outs[0].shape
    n = b * t
    nb = t // tm
    spec = lambda a: pl.BlockSpec((None, a.shape[1], tm // a.shape[1], a.shape[3]),
                                  lambda i: (i // nb, 0, i % nb, 0))
    arrays = tuple(outs) + tuple(lses)
    expand = np.zeros((LANES, SLAB), np.float32)
    for h in range(DIL_HEADS):
        expand[h * LSE_LANES, h * HEAD_DIM:(h + 1) * HEAD_DIM] = 1.0
    return pl.pallas_call(
        _dil_combine_kernel,
        grid=(n // tm,),
        in_specs=[spec(a) for a in arrays] + [pl.BlockSpec((LANES, SLAB), lambda i: (0, 0))],
        out_specs=pl.BlockSpec((tm, SLAB), lambda i: (i, 0)),
        out_shape=jax.ShapeDtypeStruct((n, SLAB), BF16),
        scratch_shapes=[pltpu.VMEM((a.shape[3] // LANES, tm, LANES), F32) for a in arrays if a.shape[1] > 1],
        compiler_params=_cparams(("parallel",)),
        name="dil_combine",
    )(*arrays, jnp.asarray(expand, BF16))


MLA_QK = MLA_NOPE + MLA_ROPE


def mla_tables(t):
    ang = _rope_angles(t, MLA_ROPE)
    cos2 = np.concatenate([np.cos(ang), np.cos(ang)], axis=1)
    sin2 = np.concatenate([np.sin(ang), np.sin(ang)], axis=1)
    z = lambda w: np.zeros((t, w), np.float32)
    q_cos = np.concatenate([np.ones((t, MLA_NOPE), np.float32), cos2, z(LANES - MLA_QK)], axis=1)
    q_sin = np.concatenate([z(MLA_NOPE), sin2, z(LANES - MLA_QK)], axis=1)
    k_cos = np.concatenate([cos2, z(LANES - MLA_ROPE)], axis=1)
    k_sin = np.concatenate([-sin2[:, :MLA_ROPE // 2], sin2[:, MLA_ROPE // 2:], z(LANES - MLA_ROPE)], axis=1)
    return tuple(jnp.asarray(a, F32) for a in (q_cos, q_sin, k_cos, k_sin))


def mla_weights(w_uq, w_ukv):
    hq = w_uq.reshape(MLA_Q_RANK, MLA_HEADS, MLA_QK)
    nope, pe = hq[..., :MLA_NOPE], hq[..., MLA_NOPE:]
    pe_rot = jnp.concatenate([-pe[..., MLA_ROPE // 2:], pe[..., :MLA_ROPE // 2]], axis=-1)
    zq = jnp.zeros((MLA_Q_RANK, MLA_HEADS, LANES - MLA_QK), w_uq.dtype)
    w1 = jnp.concatenate([nope, pe, zq], axis=-1).reshape(MLA_Q_RANK, MLA_HEADS * LANES)
    w2 = jnp.concatenate([jnp.zeros_like(nope), pe_rot, zq], axis=-1).reshape(MLA_Q_RANK, MLA_HEADS * LANES)
    hkv = w_ukv.reshape(MLA_KV_RANK, MLA_HEADS, MLA_NOPE + MLA_V)
    k_nope, v = hkv[..., :MLA_NOPE], hkv[..., MLA_NOPE:]
    zk = jnp.zeros((MLA_KV_RANK, MLA_HEADS, LANES - MLA_NOPE), w_ukv.dtype)
    wk = jnp.concatenate([k_nope, zk], axis=-1).reshape(MLA_KV_RANK, MLA_HEADS * LANES)
    zv = jnp.zeros((MLA_KV_RANK, MLA_HEADS, LANES - MLA_V), w_ukv.dtype)
    wv = jnp.concatenate([v, zv], axis=-1).reshape(MLA_KV_RANK, MLA_HEADS * LANES)
    place = np.zeros((LANES, MLA_HEADS * LANES), np.float32)
    ones = np.zeros((1, MLA_HEADS * LANES), np.float32)
    for h in range(MLA_HEADS):
        place[np.arange(MLA_ROPE), h * LANES + MLA_NOPE + np.arange(MLA_ROPE)] = 1.0
        ones[0, h * LANES + MLA_V] = 1.0
    return (w1.astype(BF16), w2.astype(BF16), wk.astype(BF16), wv.astype(BF16), jnp.asarray(place, BF16),
            jnp.asarray(ones, F32))


def _mla_q_kernel(c_ref, nw_ref, w1_ref, w2_ref, cos_ref, sin_ref, o_ref):
    cn = _rms(c_ref[...].astype(F32), nw_ref[...]).astype(BF16)
    cos = jnp.tile(cos_ref[...], (1, MLA_HEADS))
    sin = jnp.tile(sin_ref[...], (1, MLA_HEADS))
    q = _dot(cn, w1_ref[...]) * cos + _dot(cn, w2_ref[...]) * sin
    o_ref[...] = (q * (MLA_QK ** -0.5 * math.log2(math.e))).astype(o_ref.dtype)


def _mla_kv_kernel(c_ref, tail_ref, nw_ref, wk_ref, wv_ref, place_ref, ones_ref, cos_ref, sin_ref,
                   k_ref, v_ref):
    cn = _rms(c_ref[...].astype(F32), nw_ref[...]).astype(BF16)
    kr = tail_ref[...]
    half = MLA_ROPE // 2
    rot = jnp.where(_lane_lt(kr.shape, half), pltpu.roll(kr, LANES - half, 1), pltpu.roll(kr, half, 1))
    k_pe = (kr * cos_ref[...] + rot * sin_ref[...]).astype(BF16)
    k_ref[...] = (_dot(cn, wk_ref[...]) + _dot(k_pe, place_ref[...])).astype(k_ref.dtype)
    v_ref[...] = (_dot(cn, wv_ref[...]) + ones_ref[...]).astype(v_ref.dtype)


def mla_project(proj, tail, q_norm_w, kv_norm_w, w_uq, w_ukv, b, t, tm):
    n = b * t
    nb = t // tm
    w1, w2, wk, wv, place, ones = mla_weights(w_uq, w_ukv)
    q_cos, q_sin, k_cos, k_sin = mla_tables(t)
    wide = MLA_HEADS * LANES
    full = lambda shape: pl.BlockSpec(shape, lambda i: (0, 0))
    tab = pl.BlockSpec((tm, LANES), lambda i: (i % nb, 0))
    qf = pl.pallas_call(
        _mla_q_kernel,
        grid=(n // tm,),
        in_specs=[pl.BlockSpec((tm, SLAB), lambda i: (i, COL_CQ)), full((1, MLA_Q_RANK)),
                  full((MLA_Q_RANK, wide)), full((MLA_Q_RANK, wide)), tab, tab],
        out_specs=pl.BlockSpec((tm, wide), lambda i: (i, 0)),
        out_shape=jax.ShapeDtypeStruct((n, wide), BF16),
        compiler_params=_cparams(("parallel",)),
        name="mla_q_proj",
    )(proj, q_norm_w.reshape(1, MLA_Q_RANK), w1, w2, q_cos, q_sin)
    kf, vf = pl.pallas_call(
        _mla_kv_kernel,
        grid=(n // tm,),
        in_specs=[pl.BlockSpec((tm, SLAB), lambda i: (i, COL_CKV)),
                  pl.BlockSpec((tm, LANES), lambda i: (i, 0)), full((1, MLA_KV_RANK)),
                  full((MLA_KV_RANK, wide)), full((MLA_KV_RANK, wide)), full((LANES, wide)), full((1, wide)),
                  tab, tab],
        out_specs=[pl.BlockSpec((tm, wide), lambda i: (i, 0))] * 2,
        out_shape=[jax.ShapeDtypeStruct((n, wide), BF16)] * 2,
        compiler_params=_cparams(("parallel",)),
        name="mla_kv_proj",
    )(proj, tail, kv_norm_w.reshape(1, MLA_KV_RANK), wk, wv, place, ones, k_cos, k_sin)
    return qf, kf, vf


def _mla_attn_kernel(q_ref, k_ref, v_ref, *rest, t, tk, n_cast):
    cast_in, o_ref, cast_out = rest[:n_cast], rest[n_cast], rest[n_cast + 1:]
    for src_ref, dst_ref in zip(cast_in, cast_out):
        dst_ref[...] = src_ref[...].astype(dst_ref.dtype)
    tq = q_ref.shape[0]
    groups = [slice(hh * LANES, (hh + 1) * LANES) for hh in range(2)]
    qs = [q_ref[:, grp] for grp in groups]

    def scores(c):
        return [_dot_nt(q, k_ref[c * tk:(c + 1) * tk, grp]) for q, grp in zip(qs, groups)]

    n_chunks = t // tk
    ms = [jnp.full((tq, 1), -jnp.inf, F32)] * 2
    accs = [jnp.zeros((tq, LANES), F32)] * 2
    s_next = scores(0)
    for c in range(n_chunks):
        s_cur = s_next
        if c + 1 < n_chunks:
            s_next = scores(c + 1)
        for hh, grp in enumerate(groups):
            m_new = jnp.maximum(ms[hh], jnp.max(s_cur[hh], axis=-1, keepdims=True))
            p = jnp.exp2((s_cur[hh] - m_new).astype(BF16))
            accs[hh] = jnp.exp2(ms[hh] - m_new) * accs[hh] + _dot(p, v_ref[c * tk:(c + 1) * tk, grp])
            ms[hh] = m_new
    outs = [acc / acc[:, MLA_V:MLA_V + 1] for acc in accs]
    first = _lane_lt((tq, LANES), MLA_V)
    o_ref[...] = jnp.where(first, outs[0], pltpu.roll(outs[1], MLA_V, 1)).astype(o_ref.dtype)


def mla_cast_rows(w, b, t, tq):
    steps = b * N_PAIRS * (t // tq)
    rows = int(np.prod(w.shape[:-1]))
    per = rows // steps
    return per if rows % steps == 0 and per % 16 == 0 else None


def mla_attention(qf, kf, vf, b, t, tq, tk, cast=()):
    n = b * t
    nq = t // tq
    step = lambda i, p, s: ((i * N_PAIRS + p) * nq + s, 0)
    cast2d = [w.reshape(-1, w.shape[-1]) for w in cast]
    cast_specs = [pl.BlockSpec((mla_cast_rows(w, b, t, tq), w2.shape[1]), step) for w, w2 in zip(cast, cast2d)]
    outs = pl.pallas_call(
        functools.partial(_mla_attn_kernel, t=t, tk=tk, n_cast=len(cast)),
        grid=(b, N_PAIRS, nq),
        in_specs=[
            pl.BlockSpec((tq, 2 * LANES), lambda i, p, s: (i * nq + s, p)),
            pl.BlockSpec((t, 2 * LANES), lambda i, p, s: (i, p)),
            pl.BlockSpec((t, 2 * LANES), lambda i, p, s: (i, p)),
        ] + cast_specs,
        out_specs=[pl.BlockSpec((tq, HEAD_PAIR), lambda i, p, s: (i * nq + s, p))] + cast_specs,
        out_shape=[jax.ShapeDtypeStruct((n, SLAB), BF16)]
        + [jax.ShapeDtypeStruct(w2.shape, BF16) for w2 in cast2d],
        compiler_params=_cparams(("parallel", "parallel", "arbitrary")),
        name="mla_attention",
    )(qf, kf, vf, *cast2d)
    return outs[0], [o.reshape(w.shape) for o, w in zip(outs[1:], cast)]


def _in_proj_segments():
    sizes = (SSM_INNER, SSM_CONV_CH, 2 * SSM_HEADS, SLAB, SLAB, SLAB, MLA_Q_RANK, MLA_KV_RANK, MLA_ROPE,
             SLAB, SLAB, SLAB)
    off = [int(v) for v in np.concatenate([[0], np.cumsum(sizes)])]
    main = ((off[0], off[2]), (off[3], off[8]), (off[9], off[12]))
    tail = ((off[8], off[9]), (off[2], off[3]))
    return main, tail


def _in_proj_columns():
    main, tail = _in_proj_segments()
    cols = lambda segs: np.concatenate([np.arange(a, b) for a, b in segs])
    return cols(main), cols(tail)


def in_proj_weights(w_in_l):
    main, tail = _in_proj_segments()
    w_main = jnp.concatenate([w_in_l[:, a:b] for a, b in main], axis=1).astype(BF16)
    pad = jnp.zeros((D_MODEL, LANES - sum(b - a for a, b in tail)), w_in_l.dtype)
    w_tail = jnp.concatenate([w_in_l[:, a:b] for a, b in tail] + [pad], axis=1).astype(BF16)
    return w_main, w_tail


MLA_TQ = 1024
MLA_CASTS_PER_CALL = 2


def mixers(proj, tail, p, l, b, t, cast):
    xbc = conv_silu(proj, p["conv_w"][l], p["conv_b"][l], b, t)
    y_f, y_b = ssd_scan(xbc, tail, p["dt_bias"][l], p["a_log"][l], b, t)
    y_ssm = ssd_combine(y_f, y_b, xbc, proj, p["d_skip"][l], p["ssm_norm_w"][l], 1024)

    y_na = na_attention(proj, p["na_rpb"][l], b, t)

    qf, kf, vf = mla_project(proj, tail, p["mla_q_norm_w"][l], p["mla_kv_norm_w"][l],
                             p["mla_w_uq"][l], p["mla_w_ukv"][l], b, t, 1024)
    y_mla, cast_out = mla_attention(qf, kf, vf, b, t, MLA_TQ, 512, cast)

    qkvs = rope_qkv(proj, b, t, 1024)
    outs, lses = zip(*[band_attention(qkv, w, d) for qkv, (w, d) in zip(qkvs, DIL_PAIRS)])
    y_dil = dil_combine(outs, lses, 1024)
    return (y_ssm, y_na, y_mla, y_dil), cast_out


def kernel(x, attn_norm_w, w_in, conv_w, conv_b, a_log, dt_bias, d_skip, ssm_norm_w, na_rpb,
           mla_q_norm_w, mla_kv_norm_w, mla_w_uq, mla_w_ukv, w_o, ffn_norm_w, ffn_w_gate, ffn_w_up,
           ffn_w_down, router_w, exp_w_gate, exp_w_up, exp_w_down, final_norm_w):
    b, t, _ = x.shape
    n = b * t
    depth = w_in.shape[0]
    p = dict(conv_w=conv_w, conv_b=conv_b, a_log=a_log, dt_bias=dt_bias, d_skip=d_skip,
             ssm_norm_w=ssm_norm_w, na_rpb=na_rpb, mla_q_norm_w=mla_q_norm_w,
             mla_kv_norm_w=mla_kv_norm_w, mla_w_uq=mla_w_uq, mla_w_ukv=mla_w_ukv)
    x = x.reshape(n, D_MODEL)
    cast_rows = 256
    w_o_b = cast_bf16(w_o, cast_rows)
    ffn_b = [cast_bf16(w, cast_rows) for w in (ffn_w_gate, ffn_w_up, ffn_w_down)]
    exp_f32 = [exp_w_gate, exp_w_up, exp_w_down]
    exp_b = [None] * len(exp_f32)
    pending = [k for k, w in enumerate(exp_f32) if mla_cast_rows(w, b, t, MLA_TQ) is not None]
    moe_tm = 512
    normed = False
    for l in range(depth):
        w_main, w_tail = in_proj_weights(w_in[l])
        proj, tail = in_proj(x, attn_norm_w[l], w_main, w_tail, 512, PROJ_MAIN)
        jobs, pending = pending[:MLA_CASTS_PER_CALL], pending[MLA_CASTS_PER_CALL:]
        mix, cast_out = mixers(proj, tail, p, l, b, t, [exp_f32[k] for k in jobs])
        for k, w_b in zip(jobs, cast_out):
            exp_b[k] = w_b
        x = out_proj(mix, w_o_b, l, x, 512, D_MODEL)
        j = l // 2
        if l % 2 == 0:
            x = ffn_dense(x, ffn_norm_w[l], *ffn_b, j, 1024, 512)
        else:
            pending = []
            exp_b = [cast_bf16(w, cast_rows) if w_b is None else w_b for w, w_b in zip(exp_f32, exp_b)]
            top_i, gates = moe_router(x, ffn_norm_w[l], router_w[j], 1024)
            src, pos, tile_expert, tile_valid = moe_plan(top_i, moe_tm)
            y = moe_ffn(x, ffn_norm_w[l], *exp_b, j, src, tile_expert, tile_valid, moe_tm, 512)
            last = l == depth - 1
            x = moe_combine(x, gates, y, pos, final_norm_w if last else None, 512)
            normed = last
    if not normed:
        x = rmsnorm_rows(x, final_norm_w, 1024)
    return x.reshape(b, t, D_MODEL)
```
